```python
import math
import jax, jax.numpy as jnp
from jax import lax
import numpy as np

D_MODEL = 1024
BATCH = 8
SEQ = 4096
DEPTH = 1

CHUNK = 64
D_MIX = D_MODEL
D_RWKV = D_MIX // 2
D_CONV = D_MIX - D_RWKV
HEAD_SIZE = 64
N_HEADS = D_RWKV // HEAD_SIZE
LORA_W = 64
LORA_A = 64
LORA_G = 128
CONV_WIDTH = 31
D_FF = 2816
D_SHIFT = 3 * D_RWKV + LORA_W + LORA_A + LORA_G
D_IN = D_SHIFT + 2 * D_CONV
RMS_EPS = 1e-6
GN_EPS = 64e-5
LN_EPS = 1e-5
DECAY_SCALE = math.exp(-0.5)

kernel_name = "hybrid_rwkv7_conformer_conv_macaron_block"


def rmsnorm(x, g):
    xf = x.astype(jnp.float32)
    y = xf * lax.rsqrt(jnp.mean(xf * xf, axis=-1, keepdims=True) + RMS_EPS)
    return (y * g.astype(jnp.float32)).astype(x.dtype)


def layernorm(x, g, b):
    xf = x.astype(jnp.float32)
    mu = jnp.mean(xf, axis=-1, keepdims=True)
    var = jnp.mean(jnp.square(xf - mu), axis=-1, keepdims=True)
    y = (xf - mu) * lax.rsqrt(var + LN_EPS)
    return (y * g.astype(jnp.float32) + b.astype(jnp.float32)).astype(x.dtype)


def swiglu(x, w_gu, w_down):
    gu = x @ w_gu
    gate, up = gu[..., :D_FF], gu[..., D_FF:]
    return (jax.nn.silu(gate) * up) @ w_down


def token_shift(y):
    return jnp.pad(y[:, :-1], ((0, 0), (1, 0), (0, 0)))


def rwkv7_recurrence(r, w, k, v, z, b):
    bsz, seq, nh, n = r.shape
    n_chunks = seq // CHUNK

    def to_chunks(t):
        return jnp.transpose(t, (1, 0, 2, 3)).reshape(n_chunks, CHUNK, bsz, nh, n)

    xs = tuple(to_chunks(t) for t in (r, w, k, v, z, b))

    def frame_step(state, inp):
        r_t, w_t, k_t, v_t, z_t, b_t = inp
        sz = jnp.einsum('bhij,bhj->bhi', state, z_t)
        state = (state * w_t[:, :, None, :]
                 + sz[..., None] * b_t[:, :, None, :]
                 + v_t[..., None] * k_t[:, :, None, :])
        y_t = jnp.einsum('bhij,bhj->bhi', state, r_t)
        return state, y_t

    def chunk_step(state, chunk_inp):
        return lax.scan(frame_step, state, chunk_inp)

    state0 = jnp.zeros((bsz, nh, n, n), jnp.float32)
    _, ys = lax.scan(chunk_step, state0, xs)
    ys = ys.reshape(seq, bsz, nh, n)
    return jnp.transpose(ys, (1, 0, 2, 3))


def hybrid_mixer(h, w_in, shift_mu, w_up, w0, a_up, a0, g_up, k_k, k_a, r_k,
                 gn_w, gn_b, conv_dw, conv_b, conv_ln_w, conv_ln_b, w_out):
    bsz, seq, _ = h.shape
    p = h @ w_in
    ps, pc = p[..., :D_SHIFT], p[..., D_SHIFT:]

    ps = ps + (token_shift(ps) - ps) * shift_mu
    o1, o2, o3 = D_RWKV, 2 * D_RWKV, 3 * D_RWKV
    o4, o5 = o3 + LORA_W, o3 + LORA_W + LORA_A
    r, k, v = ps[..., :o1], ps[..., o1:o2], ps[..., o2:o3]
    xw, xa, xg = ps[..., o3:o4], ps[..., o4:o5], ps[..., o5:]

    d = (w0 + jnp.tanh(xw) @ w_up).astype(jnp.float32)
    decay = jnp.exp(-DECAY_SCALE * jax.nn.sigmoid(d))
    a = jax.nn.sigmoid(a0 + xa @ a_up)
    g = jax.nn.sigmoid(xg) @ g_up

    heads = lambda t: t.reshape(bsz, seq, N_HEADS, HEAD_SIZE).astype(jnp.float32)
    kk = heads(k * k_k)
    kk = kk * lax.rsqrt(jnp.maximum(jnp.sum(kk * kk, axis=-1, keepdims=True), 1e-12))
    k = k * (1.0 + (a - 1.0) * k_a)
    rh, kh, vh, ah, wh = heads(r), heads(k), heads(v), heads(a), heads(decay)

    y = rwkv7_recurrence(rh, wh, kh, vh, -kk, kk * ah)
    mu = jnp.mean(y, axis=-1, keepdims=True)
    var = jnp.mean(jnp.square(y - mu), axis=-1, keepdims=True)
    y = (y - mu) * lax.rsqrt(var + GN_EPS)
    y = y * gn_w.astype(jnp.float32).reshape(N_HEADS, HEAD_SIZE) + gn_b.astype(jnp.float32).reshape(N_HEADS, HEAD_SIZE)
    bonus = jnp.sum(rh * kh * r_k.astype(jnp.float32), axis=-1, keepdims=True) * vh
    y = (y + bonus).reshape(bsz, seq, D_RWKV).astype(h.dtype)
    out_a = y * g

    glu = pc[..., :D_CONV] * jax.nn.sigmoid(pc[..., D_CONV:])
    c = lax.conv_general_dilated(
        glu, conv_dw[:, None, :], window_strides=(1,),
        padding=[(CONV_WIDTH - 1, 0)],
        dimension_numbers=('NWC', 'WIO', 'NWC'),
        feature_group_count=D_CONV) + conv_b
    out_b = jax.nn.silu(layernorm(c, conv_ln_w, conv_ln_b))

    return jnp.concatenate([out_a, out_b], axis=-1) @ w_out


def _fwd_setup_inputs(seed: int = 0) -> dict:
    key = jax.random.key(seed)
    ks = iter(jax.random.split(key, 40))
    f32 = jnp.float32

    def nrm(shape, scale):
        return jax.random.normal(next(ks), shape, f32) * scale

    def gain(shape):
        return 1.0 + nrm(shape, 0.02)

    L = DEPTH
    return {
        "x": nrm((BATCH, SEQ, D_MODEL), 1.0),
        "ffn1_norm_pre": gain((L, D_MODEL)),
        "ffn1_norm_post": gain((L, D_MODEL)),
        "ffn1_w_gu": nrm((L, D_MODEL, 2 * D_FF), D_MODEL ** -0.5),
        "ffn1_w_down": nrm((L, D_FF, D_MODEL), D_FF ** -0.5),
        "mix_norm_pre": gain((L, D_MODEL)),
        "mix_norm_post": gain((L, D_MODEL)),
        "w_in": nrm((L, D_MODEL, D_IN), D_MODEL ** -0.5),
        "shift_mu": jax.random.uniform(next(ks), (L, D_SHIFT), f32, 0.1, 0.9),
        "w_up": nrm((L, LORA_W, D_RWKV), 0.3 * LORA_W ** -0.5),
        "w0": nrm((L, D_RWKV), 0.5),
        "a_up": nrm((L, LORA_A, D_RWKV), 0.3 * LORA_A ** -0.5),
        "a0": nrm((L, D_RWKV), 0.1),
        "g_up": nrm((L, LORA_G, D_RWKV), LORA_G ** -0.5),
        "k_k": 0.85 + nrm((L, D_RWKV), 0.02),
        "k_a": 1.0 + nrm((L, D_RWKV), 0.02),
        "r_k": nrm((L, N_HEADS, HEAD_SIZE), 0.1),
        "gn_w": gain((L, D_RWKV)),
        "gn_b": nrm((L, D_RWKV), 0.01),
        "conv_dw": nrm((L, CONV_WIDTH, D_CONV), CONV_WIDTH ** -0.5),
        "conv_b": nrm((L, D_CONV), 0.01),
        "conv_ln_w": gain((L, D_CONV)),
        "conv_ln_b": nrm((L, D_CONV), 0.01),
        "w_out": nrm((L, D_MIX, D_MODEL), D_MIX ** -0.5),
        "ffn2_norm_pre": gain((L, D_MODEL)),
        "ffn2_norm_post": gain((L, D_MODEL)),
        "ffn2_w_gu": nrm((L, D_MODEL, 2 * D_FF), D_MODEL ** -0.5),
        "ffn2_w_down": nrm((L, D_FF, D_MODEL), D_FF ** -0.5),
    }


def _fwd_reference(x, ffn1_norm_pre, ffn1_norm_post, ffn1_w_gu, ffn1_w_down,
              mix_norm_pre, mix_norm_post, w_in, shift_mu, w_up, w0, a_up, a0,
              g_up, k_k, k_a, r_k, gn_w, gn_b, conv_dw, conv_b, conv_ln_w,
              conv_ln_b, w_out, ffn2_norm_pre, ffn2_norm_post, ffn2_w_gu,
              ffn2_w_down):
    for l in range(DEPTH):
        f = swiglu(rmsnorm(x, ffn1_norm_pre[l]), ffn1_w_gu[l], ffn1_w_down[l])
        x = x + 0.5 * rmsnorm(f, ffn1_norm_post[l])
        m = hybrid_mixer(rmsnorm(x, mix_norm_pre[l]), w_in[l], shift_mu[l], w_up[l],
                         w0[l], a_up[l], a0[l], g_up[l], k_k[l], k_a[l], r_k[l],
                         gn_w[l], gn_b[l], conv_dw[l], conv_b[l], conv_ln_w[l],
                         conv_ln_b[l], w_out[l])
        x = x + rmsnorm(m, mix_norm_post[l])
        f = swiglu(rmsnorm(x, ffn2_norm_pre[l]), ffn2_w_gu[l], ffn2_w_down[l])
        x = x + 0.5 * rmsnorm(f, ffn2_norm_post[l])
    return x


import jax as _jax
import jax.numpy as _jnp

TWIN_FORMAT = 'train_step'
FWD_PARAMS = ['x', 'ffn1_norm_pre', 'ffn1_norm_post', 'ffn1_w_gu', 'ffn1_w_down', 'mix_norm_pre', 'mix_norm_post', 'w_in', 'shift_mu', 'w_up', 'w0', 'a_up', 'a0', 'g_up', 'k_k', 'k_a', 'r_k', 'gn_w', 'gn_b', 'conv_dw', 'conv_b', 'conv_ln_w', 'conv_ln_b', 'w_out', 'ffn2_norm_pre', 'ffn2_norm_post', 'ffn2_w_gu', 'ffn2_w_down']
TWIN_WEIGHTS = ['ffn1_norm_pre', 'ffn1_norm_post', 'ffn1_w_gu', 'ffn1_w_down', 'mix_norm_pre', 'mix_norm_post', 'w_in', 'shift_mu', 'w_up', 'w0', 'a_up', 'a0', 'g_up', 'k_k', 'k_a', 'r_k', 'gn_w', 'gn_b', 'conv_dw', 'conv_b', 'conv_ln_w', 'conv_ln_b', 'w_out', 'ffn2_norm_pre', 'ffn2_norm_post', 'ffn2_w_gu', 'ffn2_w_down']
TWIN_DIFF_INPUT = 'x'
TWIN_INPUTS = ['x', 'ffn1_norm_pre', 'ffn1_norm_post', 'ffn1_w_gu', 'ffn1_w_down', 'mix_norm_pre', 'mix_norm_post', 'w_in', 'shift_mu', 'w_up', 'w0', 'a_up', 'a0', 'g_up', 'k_k', 'k_a', 'r_k', 'gn_w', 'gn_b', 'conv_dw', 'conv_b', 'conv_ln_w', 'conv_ln_b', 'w_out', 'ffn2_norm_pre', 'ffn2_norm_post', 'ffn2_w_gu', 'ffn2_w_down', 'loss_target', 'm_ffn1_norm_pre', 'm_ffn1_norm_post', 'm_ffn1_w_gu', 'm_ffn1_w_down', 'm_mix_norm_pre', 'm_mix_norm_post', 'm_w_in', 'm_shift_mu', 'm_w_up', 'm_w0', 'm_a_up', 'm_a0', 'm_g_up', 'm_k_k', 'm_k_a', 'm_r_k', 'm_gn_w', 'm_gn_b', 'm_conv_dw', 'm_conv_b', 'm_conv_ln_w', 'm_conv_ln_b', 'm_w_out', 'm_ffn2_norm_pre', 'm_ffn2_norm_post', 'm_ffn2_w_gu', 'm_ffn2_w_down', 'v_ffn1_norm_pre', 'v_ffn1_norm_post', 'v_ffn1_w_gu', 'v_ffn1_w_down', 'v_mix_norm_pre', 'v_mix_norm_post', 'v_w_in', 'v_shift_mu', 'v_w_up', 'v_w0', 'v_a_up', 'v_a0', 'v_g_up', 'v_k_k', 'v_k_a', 'v_r_k', 'v_gn_w', 'v_gn_b', 'v_conv_dw', 'v_conv_b', 'v_conv_ln_w', 'v_conv_ln_b', 'v_w_out', 'v_ffn2_norm_pre', 'v_ffn2_norm_post', 'v_ffn2_w_gu', 'v_ffn2_w_down']
TWIN_OUTPUTS = ['loss', 'grad_x', 'grad_ffn1_norm_pre', 'grad_ffn1_norm_post', 'grad_ffn1_w_gu', 'grad_ffn1_w_down', 'grad_mix_norm_pre', 'grad_mix_norm_post', 'grad_w_in', 'grad_shift_mu', 'grad_w_up', 'grad_w0', 'grad_a_up', 'grad_a0', 'grad_g_up', 'grad_k_k', 'grad_k_a', 'grad_r_k', 'grad_gn_w', 'grad_gn_b', 'grad_conv_dw', 'grad_conv_b', 'grad_conv_ln_w', 'grad_conv_ln_b', 'grad_w_out', 'grad_ffn2_norm_pre', 'grad_ffn2_norm_post', 'grad_ffn2_w_gu', 'grad_ffn2_w_down', 'delta_ffn1_norm_pre', 'delta_ffn1_norm_post', 'delta_ffn1_w_gu', 'delta_ffn1_w_down', 'delta_mix_norm_pre', 'delta_mix_norm_post', 'delta_w_in', 'delta_shift_mu', 'delta_w_up', 'delta_w0', 'delta_a_up', 'delta_a0', 'delta_g_up', 'delta_k_k', 'delta_k_a', 'delta_r_k', 'delta_gn_w', 'delta_gn_b', 'delta_conv_dw', 'delta_conv_b', 'delta_conv_ln_w', 'delta_conv_ln_b', 'delta_w_out', 'delta_ffn2_norm_pre', 'delta_ffn2_norm_post', 'delta_ffn2_w_gu', 'delta_ffn2_w_down', 'new_m_ffn1_norm_pre', 'new_m_ffn1_norm_post', 'new_m_ffn1_w_gu', 'new_m_ffn1_w_down', 'new_m_mix_norm_pre', 'new_m_mix_norm_post', 'new_m_w_in', 'new_m_shift_mu', 'new_m_w_up', 'new_m_w0', 'new_m_a_up', 'new_m_a0', 'new_m_g_up', 'new_m_k_k', 'new_m_k_a', 'new_m_r_k', 'new_m_gn_w', 'new_m_gn_b', 'new_m_conv_dw', 'new_m_conv_b', 'new_m_conv_ln_w', 'new_m_conv_ln_b', 'new_m_w_out', 'new_m_ffn2_norm_pre', 'new_m_ffn2_norm_post', 'new_m_ffn2_w_gu', 'new_m_ffn2_w_down', 'new_v_ffn1_norm_pre', 'new_v_ffn1_norm_post', 'new_v_ffn1_w_gu', 'new_v_ffn1_w_down', 'new_v_mix_norm_pre', 'new_v_mix_norm_post', 'new_v_w_in', 'new_v_shift_mu', 'new_v_w_up', 'new_v_w0', 'new_v_a_up', 'new_v_a0', 'new_v_g_up', 'new_v_k_k', 'new_v_k_a', 'new_v_r_k', 'new_v_gn_w', 'new_v_gn_b', 'new_v_conv_dw', 'new_v_conv_b', 'new_v_conv_ln_w', 'new_v_conv_ln_b', 'new_v_w_out', 'new_v_ffn2_norm_pre', 'new_v_ffn2_norm_post', 'new_v_ffn2_w_gu', 'new_v_ffn2_w_down']
TWIN_LEAF_KINDS = {'loss': 'loss', 'grad_x': 'grad_x', 'grad_ffn1_norm_pre': 'grad_w', 'grad_ffn1_norm_post': 'grad_w', 'grad_ffn1_w_gu': 'grad_w', 'grad_ffn1_w_down': 'grad_w', 'grad_mix_norm_pre': 'grad_w', 'grad_mix_norm_post': 'grad_w', 'grad_w_in': 'grad_w', 'grad_shift_mu': 'grad_w', 'grad_w_up': 'grad_w', 'grad_w0': 'grad_w', 'grad_a_up': 'grad_w', 'grad_a0': 'grad_w', 'grad_g_up': 'grad_w', 'grad_k_k': 'grad_w', 'grad_k_a': 'grad_w', 'grad_r_k': 'grad_w', 'grad_gn_w': 'grad_w', 'grad_gn_b': 'grad_w', 'grad_conv_dw': 'grad_w', 'grad_conv_b': 'grad_w', 'grad_conv_ln_w': 'grad_w', 'grad_conv_ln_b': 'grad_w', 'grad_w_out': 'grad_w', 'grad_ffn2_norm_pre': 'grad_w', 'grad_ffn2_norm_post': 'grad_w', 'grad_ffn2_w_gu': 'grad_w', 'grad_ffn2_w_down': 'grad_w', 'delta_ffn1_norm_pre': 'delta_w', 'delta_ffn1_norm_post': 'delta_w', 'delta_ffn1_w_gu': 'delta_w', 'delta_ffn1_w_down': 'delta_w', 'delta_mix_norm_pre': 'delta_w', 'delta_mix_norm_post': 'delta_w', 'delta_w_in': 'delta_w', 'delta_shift_mu': 'delta_w', 'delta_w_up': 'delta_w', 'delta_w0': 'delta_w', 'delta_a_up': 'delta_w', 'delta_a0': 'delta_w', 'delta_g_up': 'delta_w', 'delta_k_k': 'delta_w', 'delta_k_a': 'delta_w', 'delta_r_k': 'delta_w', 'delta_gn_w': 'delta_w', 'delta_gn_b': 'delta_w', 'delta_conv_dw': 'delta_w', 'delta_conv_b': 'delta_w', 'delta_conv_ln_w': 'delta_w', 'delta_conv_ln_b': 'delta_w', 'delta_w_out': 'delta_w', 'delta_ffn2_norm_pre': 'delta_w', 'delta_ffn2_norm_post': 'delta_w', 'delta_ffn2_w_gu': 'delta_w', 'delta_ffn2_w_down': 'delta_w', 'new_m_ffn1_norm_pre': 'new_m', 'new_m_ffn1_norm_post': 'new_m', 'new_m_ffn1_w_gu': 'new_m', 'new_m_ffn1_w_down': 'new_m', 'new_m_mix_norm_pre': 'new_m', 'new_m_mix_norm_post': 'new_m', 'new_m_w_in': 'new_m', 'new_m_shift_mu': 'new_m', 'new_m_w_up': 'new_m', 'new_m_w0': 'new_m', 'new_m_a_up': 'new_m', 'new_m_a0': 'new_m', 'new_m_g_up': 'new_m', 'new_m_k_k': 'new_m', 'new_m_k_a': 'new_m', 'new_m_r_k': 'new_m', 'new_m_gn_w': 'new_m', 'new_m_gn_b': 'new_m', 'new_m_conv_dw': 'new_m', 'new_m_conv_b': 'new_m', 'new_m_conv_ln_w': 'new_m', 'new_m_conv_ln_b': 'new_m', 'new_m_w_out': 'new_m', 'new_m_ffn2_norm_pre': 'new_m', 'new_m_ffn2_norm_post': 'new_m', 'new_m_ffn2_w_gu': 'new_m', 'new_m_ffn2_w_down': 'new_m', 'new_v_ffn1_norm_pre': 'new_v', 'new_v_ffn1_norm_post': 'new_v', 'new_v_ffn1_w_gu': 'new_v', 'new_v_ffn1_w_down': 'new_v', 'new_v_mix_norm_pre': 'new_v', 'new_v_mix_norm_post': 'new_v', 'new_v_w_in': 'new_v', 'new_v_shift_mu': 'new_v', 'new_v_w_up': 'new_v', 'new_v_w0': 'new_v', 'new_v_a_up': 'new_v', 'new_v_a0': 'new_v', 'new_v_g_up': 'new_v', 'new_v_k_k': 'new_v', 'new_v_k_a': 'new_v', 'new_v_r_k': 'new_v', 'new_v_gn_w': 'new_v', 'new_v_gn_b': 'new_v', 'new_v_conv_dw': 'new_v', 'new_v_conv_b': 'new_v', 'new_v_conv_ln_w': 'new_v', 'new_v_conv_ln_b': 'new_v', 'new_v_w_out': 'new_v', 'new_v_ffn2_norm_pre': 'new_v', 'new_v_ffn2_norm_post': 'new_v', 'new_v_ffn2_w_gu': 'new_v', 'new_v_ffn2_w_down': 'new_v'}


def _forward(args):
    return _fwd_reference(*[args[k] for k in FWD_PARAMS])


def _output_shape():
    out = _jax.eval_shape(lambda: _forward(_fwd_setup_inputs(0)))
    return out.shape, out.dtype

N_MICROBATCH = 1
ADAM_LR = 0.001
ADAM_B1 = 0.9
ADAM_B2 = 0.999
ADAM_EPS = 1e-08
ADAM_WD = 0.01
ADAM_STEP = 10
PER_EXAMPLE_BATCH_AXIS = {'x': 0, 'loss_target': 0}
SHARED_INPUTS = []
_WEIGHT_DTYPES = {'ffn1_norm_pre': _jnp.float32, 'ffn1_norm_post': _jnp.float32, 'ffn1_w_gu': _jnp.float32, 'ffn1_w_down': _jnp.float32, 'mix_norm_pre': _jnp.float32, 'mix_norm_post': _jnp.float32, 'w_in': _jnp.float32, 'shift_mu': _jnp.float32, 'w_up': _jnp.float32, 'w0': _jnp.float32, 'a_up': _jnp.float32, 'a0': _jnp.float32, 'g_up': _jnp.float32, 'k_k': _jnp.float32, 'k_a': _jnp.float32, 'r_k': _jnp.float32, 'gn_w': _jnp.float32, 'gn_b': _jnp.float32, 'conv_dw': _jnp.float32, 'conv_b': _jnp.float32, 'conv_ln_w': _jnp.float32, 'conv_ln_b': _jnp.float32, 'w_out': _jnp.float32, 'ffn2_norm_pre': _jnp.float32, 'ffn2_norm_post': _jnp.float32, 'ffn2_w_gu': _jnp.float32, 'ffn2_w_down': _jnp.float32}
MOMENT_SCALE = {'ffn1_norm_pre': 4.290416e-01, 'ffn1_norm_post': 7.894914e+00, 'ffn1_w_gu': 1.834647e-01, 'ffn1_w_down': 3.077663e-01, 'mix_norm_pre': 5.020552e-01, 'mix_norm_post': 3.210657e+01, 'w_in': 3.053677e-01, 'shift_mu': 5.728187e-01, 'w_up': 3.970342e-02, 'w0': 1.264453e-01, 'a_up': 1.318090e-01, 'a0': 2.027401e-01, 'g_up': 3.540856e-01, 'k_k': 9.038760e-02, 'k_a': 4.411355e-01, 'r_k': 6.280579e-01, 'gn_w': 4.310969e-01, 'gn_b': 1.535712e+00, 'conv_dw': 3.722389e-01, 'conv_b': 2.123599e+00, 'conv_ln_w': 9.213612e-01, 'conv_ln_b': 1.246152e+00, 'w_out': 4.545856e-01, 'ffn2_norm_pre': 3.868495e-01, 'ffn2_norm_post': 7.945261e+00, 'ffn2_w_gu': 1.675295e-01, 'ffn2_w_down': 3.356286e-01}


def _to_microbatches(a, axis):
    t = _jnp.moveaxis(a, axis, 0)
    t = t.reshape((N_MICROBATCH, t.shape[0] // N_MICROBATCH) + t.shape[1:])
    return _jnp.moveaxis(t, 1, axis + 1)


def setup_inputs(seed: int = 0) -> dict:
    inp = _fwd_setup_inputs(seed)
    key = _jax.random.fold_in(_jax.random.key(seed), 7919)
    shape, _ = _output_shape()
    out = dict(inp)
    out["loss_target"] = _jax.random.normal(_jax.random.fold_in(key, 0), shape, _jnp.float32)
    for i, name in enumerate(TWIN_WEIGHTS):
        w = inp[name].astype(_jnp.float32)
        if MOMENT_SCALE is None:
            s = _jnp.sqrt(_jnp.mean(_jnp.square(w)) + 1e-30)
        else:
            s = MOMENT_SCALE[name]
        km, kv = _jax.random.split(_jax.random.fold_in(key, i + 1))
        out[name] = w
        out["m_" + name] = s * _jax.random.normal(km, w.shape, _jnp.float32)
        out["v_" + name] = (s * s) * _jax.random.uniform(kv, w.shape, _jnp.float32, 0.5, 1.5)
    if N_MICROBATCH > 1:
        for name, axis in PER_EXAMPLE_BATCH_AXIS.items():
            out[name] = _to_microbatches(out[name], axis)
    return {'x': out['x'], 'ffn1_norm_pre': out['ffn1_norm_pre'], 'ffn1_norm_post': out['ffn1_norm_post'], 'ffn1_w_gu': out['ffn1_w_gu'], 'ffn1_w_down': out['ffn1_w_down'], 'mix_norm_pre': out['mix_norm_pre'], 'mix_norm_post': out['mix_norm_post'], 'w_in': out['w_in'], 'shift_mu': out['shift_mu'], 'w_up': out['w_up'], 'w0': out['w0'], 'a_up': out['a_up'], 'a0': out['a0'], 'g_up': out['g_up'], 'k_k': out['k_k'], 'k_a': out['k_a'], 'r_k': out['r_k'], 'gn_w': out['gn_w'], 'gn_b': out['gn_b'], 'conv_dw': out['conv_dw'], 'conv_b': out['conv_b'], 'conv_ln_w': out['conv_ln_w'], 'conv_ln_b': out['conv_ln_b'], 'w_out': out['w_out'], 'ffn2_norm_pre': out['ffn2_norm_pre'], 'ffn2_norm_post': out['ffn2_norm_post'], 'ffn2_w_gu': out['ffn2_w_gu'], 'ffn2_w_down': out['ffn2_w_down'], 'loss_target': out['loss_target'], 'm_ffn1_norm_pre': out['m_ffn1_norm_pre'], 'm_ffn1_norm_post': out['m_ffn1_norm_post'], 'm_ffn1_w_gu': out['m_ffn1_w_gu'], 'm_ffn1_w_down': out['m_ffn1_w_down'], 'm_mix_norm_pre': out['m_mix_norm_pre'], 'm_mix_norm_post': out['m_mix_norm_post'], 'm_w_in': out['m_w_in'], 'm_shift_mu': out['m_shift_mu'], 'm_w_up': out['m_w_up'], 'm_w0': out['m_w0'], 'm_a_up': out['m_a_up'], 'm_a0': out['m_a0'], 'm_g_up': out['m_g_up'], 'm_k_k': out['m_k_k'], 'm_k_a': out['m_k_a'], 'm_r_k': out['m_r_k'], 'm_gn_w': out['m_gn_w'], 'm_gn_b': out['m_gn_b'], 'm_conv_dw': out['m_conv_dw'], 'm_conv_b': out['m_conv_b'], 'm_conv_ln_w': out['m_conv_ln_w'], 'm_conv_ln_b': out['m_conv_ln_b'], 'm_w_out': out['m_w_out'], 'm_ffn2_norm_pre': out['m_ffn2_norm_pre'], 'm_ffn2_norm_post': out['m_ffn2_norm_post'], 'm_ffn2_w_gu': out['m_ffn2_w_gu'], 'm_ffn2_w_down': out['m_ffn2_w_down'], 'v_ffn1_norm_pre': out['v_ffn1_norm_pre'], 'v_ffn1_norm_post': out['v_ffn1_norm_post'], 'v_ffn1_w_gu': out['v_ffn1_w_gu'], 'v_ffn1_w_down': out['v_ffn1_w_down'], 'v_mix_norm_pre': out['v_mix_norm_pre'], 'v_mix_norm_post': out['v_mix_norm_post'], 'v_w_in': out['v_w_in'], 'v_shift_mu': out['v_shift_mu'], 'v_w_up': out['v_w_up'], 'v_w0': out['v_w0'], 'v_a_up': out['v_a_up'], 'v_a0': out['v_a0'], 'v_g_up': out['v_g_up'], 'v_k_k': out['v_k_k'], 'v_k_a': out['v_k_a'], 'v_r_k': out['v_r_k'], 'v_gn_w': out['v_gn_w'], 'v_gn_b': out['v_gn_b'], 'v_conv_dw': out['v_conv_dw'], 'v_conv_b': out['v_conv_b'], 'v_conv_ln_w': out['v_conv_ln_w'], 'v_conv_ln_b': out['v_conv_ln_b'], 'v_w_out': out['v_w_out'], 'v_ffn2_norm_pre': out['v_ffn2_norm_pre'], 'v_ffn2_norm_post': out['v_ffn2_norm_post'], 'v_ffn2_w_gu': out['v_ffn2_w_gu'], 'v_ffn2_w_down': out['v_ffn2_w_down']}


def _loss(weights, diff, rest, loss_target):
    with _jax.named_scope("forward"):
        args = {**rest, TWIN_DIFF_INPUT: diff, **{k: w.astype(_WEIGHT_DTYPES[k]) for k, w in weights.items()}}
        y = _forward(args)
    with _jax.named_scope("loss_head"):
        err = _jnp.square(y.astype(_jnp.float32) - loss_target)
        return 0.5 * _jnp.sum(_jnp.mean(err, axis=-1)) if err.ndim else 0.5 * err


def _adamw(w, g, m, v):
    m = ADAM_B1 * m + (1.0 - ADAM_B1) * g
    v = ADAM_B2 * v + (1.0 - ADAM_B2) * _jnp.square(g)
    m_hat = m / (1.0 - ADAM_B1 ** ADAM_STEP)
    v_hat = v / (1.0 - ADAM_B2 ** ADAM_STEP)
    delta = -ADAM_LR * (m_hat / (_jnp.sqrt(v_hat) + ADAM_EPS) + ADAM_WD * w)
    return delta, m, v


def reference(x, ffn1_norm_pre, ffn1_norm_post, ffn1_w_gu, ffn1_w_down, mix_norm_pre, mix_norm_post, w_in, shift_mu, w_up, w0, a_up, a0, g_up, k_k, k_a, r_k, gn_w, gn_b, conv_dw, conv_b, conv_ln_w, conv_ln_b, w_out, ffn2_norm_pre, ffn2_norm_post, ffn2_w_gu, ffn2_w_down, loss_target, m_ffn1_norm_pre, m_ffn1_norm_post, m_ffn1_w_gu, m_ffn1_w_down, m_mix_norm_pre, m_mix_norm_post, m_w_in, m_shift_mu, m_w_up, m_w0, m_a_up, m_a0, m_g_up, m_k_k, m_k_a, m_r_k, m_gn_w, m_gn_b, m_conv_dw, m_conv_b, m_conv_ln_w, m_conv_ln_b, m_w_out, m_ffn2_norm_pre, m_ffn2_norm_post, m_ffn2_w_gu, m_ffn2_w_down, v_ffn1_norm_pre, v_ffn1_norm_post, v_ffn1_w_gu, v_ffn1_w_down, v_mix_norm_pre, v_mix_norm_post, v_w_in, v_shift_mu, v_w_up, v_w0, v_a_up, v_a0, v_g_up, v_k_k, v_k_a, v_r_k, v_gn_w, v_gn_b, v_conv_dw, v_conv_b, v_conv_ln_w, v_conv_ln_b, v_w_out, v_ffn2_norm_pre, v_ffn2_norm_post, v_ffn2_w_gu, v_ffn2_w_down):
    given = dict(x=x, ffn1_norm_pre=ffn1_norm_pre, ffn1_norm_post=ffn1_norm_post, ffn1_w_gu=ffn1_w_gu, ffn1_w_down=ffn1_w_down, mix_norm_pre=mix_norm_pre, mix_norm_post=mix_norm_post, w_in=w_in, shift_mu=shift_mu, w_up=w_up, w0=w0, a_up=a_up, a0=a0, g_up=g_up, k_k=k_k, k_a=k_a, r_k=r_k, gn_w=gn_w, gn_b=gn_b, conv_dw=conv_dw, conv_b=conv_b, conv_ln_w=conv_ln_w, conv_ln_b=conv_ln_b, w_out=w_out, ffn2_norm_pre=ffn2_norm_pre, ffn2_norm_post=ffn2_norm_post, ffn2_w_gu=ffn2_w_gu, ffn2_w_down=ffn2_w_down, loss_target=loss_target, m_ffn1_norm_pre=m_ffn1_norm_pre, m_ffn1_norm_post=m_ffn1_norm_post, m_ffn1_w_gu=m_ffn1_w_gu, m_ffn1_w_down=m_ffn1_w_down, m_mix_norm_pre=m_mix_norm_pre, m_mix_norm_post=m_mix_norm_post, m_w_in=m_w_in, m_shift_mu=m_shift_mu, m_w_up=m_w_up, m_w0=m_w0, m_a_up=m_a_up, m_a0=m_a0, m_g_up=m_g_up, m_k_k=m_k_k, m_k_a=m_k_a, m_r_k=m_r_k, m_gn_w=m_gn_w, m_gn_b=m_gn_b, m_conv_dw=m_conv_dw, m_conv_b=m_conv_b, m_conv_ln_w=m_conv_ln_w, m_conv_ln_b=m_conv_ln_b, m_w_out=m_w_out, m_ffn2_norm_pre=m_ffn2_norm_pre, m_ffn2_norm_post=m_ffn2_norm_post, m_ffn2_w_gu=m_ffn2_w_gu, m_ffn2_w_down=m_ffn2_w_down, v_ffn1_norm_pre=v_ffn1_norm_pre, v_ffn1_norm_post=v_ffn1_norm_post, v_ffn1_w_gu=v_ffn1_w_gu, v_ffn1_w_down=v_ffn1_w_down, v_mix_norm_pre=v_mix_norm_pre, v_mix_norm_post=v_mix_norm_post, v_w_in=v_w_in, v_shift_mu=v_shift_mu, v_w_up=v_w_up, v_w0=v_w0, v_a_up=v_a_up, v_a0=v_a0, v_g_up=v_g_up, v_k_k=v_k_k, v_k_a=v_k_a, v_r_k=v_r_k, v_gn_w=v_gn_w, v_gn_b=v_gn_b, v_conv_dw=v_conv_dw, v_conv_b=v_conv_b, v_conv_ln_w=v_conv_ln_w, v_conv_ln_b=v_conv_ln_b, v_w_out=v_w_out, v_ffn2_norm_pre=v_ffn2_norm_pre, v_ffn2_norm_post=v_ffn2_norm_post, v_ffn2_w_gu=v_ffn2_w_gu, v_ffn2_w_down=v_ffn2_w_down)
    weights = {n: given[n] for n in TWIN_WEIGHTS}
    shared = {n: given[n] for n in SHARED_INPUTS}
    per_example = {n: given[n] for n in ['x']}
    grad_fn = _jax.value_and_grad(_loss, argnums=(0, 1))

    def one_microbatch(ex, loss_target):
        ex = dict(ex)
        diff = ex.pop(TWIN_DIFF_INPUT)
        return grad_fn(weights, diff, {**shared, **ex}, loss_target)

    if N_MICROBATCH == 1:
        loss, (grad_w, grad_x) = one_microbatch(per_example, given["loss_target"])
    else:
        def body(carry, xs):
            loss_sum, grad_sum = carry
            l_k, (gw_k, gx_k) = one_microbatch(xs[0], xs[1])
            with _jax.named_scope("update"):
                return (loss_sum + l_k, _jax.tree.map(_jnp.add, grad_sum, gw_k)), gx_k

        init = (_jnp.zeros((), _jnp.float32), _jax.tree.map(_jnp.zeros_like, weights))
        (loss, grad_w), grad_x = _jax.lax.scan(body, init, (per_example, given["loss_target"]))
    with _jax.named_scope("update"):
        delta_w, new_m, new_v = {}, {}, {}
        for n in TWIN_WEIGHTS:
            delta_w[n], new_m[n], new_v[n] = _adamw(weights[n], grad_w[n], given["m_" + n], given["v_" + n])
    return (loss, grad_x, *[grad_w[n] for n in TWIN_WEIGHTS], *[delta_w[n] for n in TWIN_WEIGHTS],
            *[new_m[n] for n in TWIN_WEIGHTS], *[new_v[n] for n in TWIN_WEIGHTS])
```

```python
import functools
import math

import jax
import jax.numpy as jnp
from jax import lax
from jax.experimental import pallas as pl
from jax.experimental.pallas import tpu as pltpu

f32 = jnp.float32
bf16 = jnp.bfloat16

D = 1024
F = 2816
FB = 704
DR = 512
DC = 512
NH = 8
HN = 64
DS = 1792
DIN = 2816
CW = 31
CHUNK = 64
R1_CHUNKS = 2
R2_CHUNKS = 4
NDEV = 8
RMS_EPS = 1e-6
GN_EPS = 64e-5
LN_EPS = 1e-5
DECAY_SCALE = math.exp(-0.5)
ADAM_LR, ADAM_B1, ADAM_B2, ADAM_EPS, ADAM_WD, ADAM_STEP = 0.001, 0.9, 0.999, 1e-08, 0.01, 10
LANES = 128
HALO = 32
CONV_ROWS = 32
TM_MXU = 1024
TM_SUB = 256
TM_VPU = 256

ANY = pl.BlockSpec(memory_space=pl.ANY)


def _full(shape):
    return pl.BlockSpec(shape, lambda *_: (0,) * len(shape))


def _sds(shape, dtype=f32):
    return jax.ShapeDtypeStruct(shape, dtype)


def _dot(a, b):
    return jnp.dot(a, b, preferred_element_type=f32)


def _dot_nt(a, b):
    return lax.dot_general(a, b, (((1,), (1,)), ((), ())), preferred_element_type=f32)


def _dot_tn(a, b):
    return lax.dot_general(a, b, (((0,), (0,)), ((), ())), preferred_element_type=f32)


def _terms(a, n):
    out, rem = [], a
    for i in range(n):
        t = rem.astype(bf16)
        out.append(t)
        if i + 1 < n:
            rem = rem - t.astype(f32)
    return out


def _mm_raw(a, b, kind, mode):
    nb = a.ndim - 2
    bd = tuple(range(nb))
    ca = nb if kind == "tn" else nb + 1
    cb = nb + 1 if kind == "nt" else nb
    dn = (((ca,), (cb,)), (bd, bd))
    pa, pb = mode[:2]
    ta, tb = _terms(a, pa), _terms(b, pb)
    acc = None
    for i in range(pa):
        for j in range(pb):
            if i + j < max(pa, pb):
                p = lax.dot_general(ta[i], tb[j], dn, preferred_element_type=f32)
                acc = p if acc is None else acc + p
    return acc


@functools.partial(jax.custom_vjp, nondiff_argnums=(2, 3))
def mm(a, b, kind, mode):
    return _mm_raw(a, b, kind, mode)


def _mm_fwd(a, b, kind, mode):
    return _mm_raw(a, b, kind, mode), (a, b)


def _ct_terms(keep):
    return EXACT_TERMS if keep == 1 else keep


def _mm_bwd(kind, mode, res, ct):
    a, b = res
    pa, pb = mode[:2]
    if len(mode) == 3:
        pa = pb = ca = cb = mode[2]
    elif mode == (1, 1):
        ca = cb = 1
    else:
        ca, cb = _ct_terms(pb), _ct_terms(pa)
    if kind == "nn":
        da = mm(ct, b, "nt", (ca, pb))
        db = mm(a, ct, "tn", (pa, cb))
    elif kind == "nt":
        da = mm(ct, b, "nn", (ca, pb))
        db = mm(ct, a, "tn", (cb, pa))
    else:
        da = mm(b, ct, "nt", (pb, ca))
        db = mm(a, ct, "nn", (pa, cb))
    return da, db


mm.defvjp(_mm_fwd, _mm_bwd)

X3 = (2, 2)
EXACT_TERMS = 2
EA = (1, EXACT_TERMS)
EB = (EXACT_TERMS, 1)
X1 = (1, 1)
RGM = (1, 1, 2)
RIM = X1
RWM = X1
RAM = X1
R2M = X1
LORA = X1


def _rms_fwd(x, g):
    inv = lax.rsqrt(jnp.mean(x * x, axis=-1, keepdims=True) + RMS_EPS)
    return x * inv * g


def _rms_bwd(x, g, dy):
    inv = lax.rsqrt(jnp.mean(x * x, axis=-1, keepdims=True) + RMS_EPS)
    xh = x * inv
    dxh = dy * g
    dg = jnp.sum(dy * xh, axis=0, keepdims=True)
    dx = inv * (dxh - xh * jnp.mean(dxh * xh, axis=-1, keepdims=True))
    return dx, dg


def _tile(t, want):
    return min(t, want)


NPEER = NDEV - 1


def _comm_out_shape(arrays, modes, slots):
    out = []
    for a, mode, (lead, _) in zip(arrays, modes, slots):
        shp = tuple(lead) + tuple(a.shape) if mode == "gather" else tuple(a.shape)
        out.append(pltpu.HBM(shp, a.dtype))
    return out


def _comm_sems(n):
    return [pltpu.SemaphoreType.DMA((n * NPEER,)), pltpu.SemaphoreType.DMA((n * NPEER,)), pltpu.SemaphoreType.DMA((n,))]


RELAYED = (3, 5, 7)


def _comm_copies(ins, outs, sems, modes, slots, want):
    n = len(ins)
    send_sems, recv_sems, loc_sems = sems
    x, y, c = lax.axis_index("x"), lax.axis_index("y"), lax.axis_index("c")
    me = 4 * x + 2 * y + c

    def peer_of(k):
        px = 1 - x if k & 4 else x
        py = 1 - y if k & 2 else y
        pc = 1 - c if k & 1 else c
        return (px, py, pc), 4 * px + 2 * py + pc

    def slot(i, dev):
        return outs[i].at[slots[i][1](dev)]

    if want == "local":
        return [pltpu.make_async_copy(ins[i] if modes[i] == "gather" else ins[i].at[slots[i][1](me)], slot(i, me),
                                      loc_sems.at[i]) for i in range(n)]
    sibling = peer_of(1)[0]
    out = []
    for k in range(1, NDEV):
        peer, pid = peer_of(k)
        for i in range(n):
            sem = dict(send_sem=send_sems.at[i * NPEER + k - 1], recv_sem=recv_sems.at[i * NPEER + k - 1],
                       device_id_type=pl.DeviceIdType.MESH)
            relayed = modes[i] == "gather" and k in RELAYED
            if want == "recv":
                src = ins[i] if modes[i] == "gather" else ins[i].at[slots[i][1](pid)]
                out.append(pltpu.make_async_remote_copy(src_ref=src, dst_ref=slot(i, pid), device_id=peer, **sem))
            elif want == "first" and not relayed:
                src = ins[i] if modes[i] == "gather" else ins[i].at[slots[i][1](pid)]
                out.append(pltpu.make_async_remote_copy(src_ref=src, dst_ref=slot(i, me), device_id=peer, **sem))
            elif want == "relay" and relayed:
                origin = peer_of(k - 1)[1]
                out.append(((k - 2) * n + i, pltpu.make_async_remote_copy(
                    src_ref=slot(i, origin), dst_ref=slot(i, origin), device_id=sibling, **sem)))
    return out


def _comm_start(ins, outs, sems, modes, slots):
    for cp in _comm_copies(ins, outs, sems, modes, slots, "local") + _comm_copies(ins, outs, sems, modes, slots, "first"):
        cp.start()


def _comm_wait(ins, outs, sems, modes, slots):
    recvs = _comm_copies(ins, outs, sems, modes, slots, "recv")
    relays = _comm_copies(ins, outs, sems, modes, slots, "relay")
    for idx, cp in relays:
        recvs[idx].wait_recv()
        cp.start()
    passed_on = {idx for idx, _ in relays}
    for idx, cp in enumerate(recvs):
        if idx not in passed_on:
            cp.wait_recv()
    for cp in _comm_copies(ins, outs, sems, modes, slots, "first") + [cp for _, cp in relays]:
        cp.wait_send()
    for cp in _comm_copies(ins, outs, sems, modes, slots, "local"):
        cp.wait()


def _exchange(arrays, modes, slots, name):
    n = len(arrays)

    def body(*refs):
        ins, outs, sems = refs[:n], refs[n:2 * n], refs[2 * n:]
        _comm_start(ins, outs, sems, modes, slots)
        _comm_wait(ins, outs, sems, modes, slots)

    return pl.pallas_call(
        body, name=name, out_shape=_comm_out_shape(arrays, modes, slots),
        in_specs=[ANY] * n, out_specs=[ANY] * n, scratch_shapes=_comm_sems(n),
    )(*arrays)


def _pcall(body, *, name, grid, in_specs, out_specs, out_shape, args, scratch_shapes=(), comm=None):
    if comm is None:
        return pl.pallas_call(body, name=name, grid=grid, in_specs=in_specs, out_specs=out_specs, out_shape=out_shape,
                              scratch_shapes=list(scratch_shapes))(*args), None
    arrays, modes, slots = comm
    n_in, n_out, n_scr, nc = len(args), len(out_shape), len(scratch_shapes), len(arrays)

    def hosted(*refs):
        a_in, c_in = refs[:n_in], refs[n_in:n_in + nc]
        o = n_in + nc
        a_out, c_out = refs[o:o + n_out], refs[o + n_out:o + n_out + nc]
        o += n_out + nc
        a_scr, sems = refs[o:o + n_scr], refs[o + n_scr:]
        first = pl.program_id(0) == 0
        last = pl.program_id(0) == grid[0] - 1
        for ax in range(1, len(grid)):
            first = jnp.logical_and(first, pl.program_id(ax) == 0)
            last = jnp.logical_and(last, pl.program_id(ax) == grid[ax] - 1)

        @pl.when(first)
        def _():
            _comm_start(c_in, c_out, sems, modes, slots)

        body(*a_in, *a_out, *a_scr)

        @pl.when(last)
        def _():
            _comm_wait(c_in, c_out, sems, modes, slots)

    res = pl.pallas_call(
        hosted, name=name, grid=grid, in_specs=list(in_specs) + [ANY] * nc, out_specs=list(out_specs) + [ANY] * nc,
        out_shape=list(out_shape) + _comm_out_shape(arrays, modes, slots),
        scratch_shapes=list(scratch_shapes) + _comm_sems(nc),
    )(*args, *arrays)
    return res[:n_out], res[n_out:]


SLOT8 = ((NDEV,), lambda d: (d,))
SLOT42 = ((4, 2), lambda d: (lax.rem(d, 4), lax.div(d, 4)))


def _ffn_in_fwd(x, g, wg4, name, comm=None):
    t = x.shape[0]
    tm = _tile(t, TM_MXU)
    sub = _tile(tm, TM_SUB)

    def body(x_ref, g_ref, w_ref, xn_ref, gu_ref, h_ref):
        for s in range(tm // sub):
            rows = pl.ds(s * sub, sub)
            xn = _rms_fwd(x_ref[rows, :], g_ref[...]).astype(bf16)
            xn_ref[rows, :] = xn
            gate = _dot_nt(xn, w_ref[0, 0])
            up = _dot_nt(xn, w_ref[0, 1])
            gu_ref[0, 0, rows, :] = gate.astype(bf16)
            gu_ref[0, 1, rows, :] = up.astype(bf16)
            h_ref[0, rows, :] = (gate * jax.nn.sigmoid(gate) * up).astype(bf16)

    return _pcall(
        body, name=name, grid=(t // tm, 4),
        in_specs=[pl.BlockSpec((tm, D), lambda i, e: (i, 0)), _full((1, D)),
                  pl.BlockSpec((1, 2, FB, D), lambda i, e: (e, 0, 0, 0))],
        out_specs=[pl.BlockSpec((tm, D), lambda i, e: (i, 0)),
                   pl.BlockSpec((1, 2, tm, FB), lambda i, e: (e, 0, i, 0)),
                   pl.BlockSpec((1, tm, FB), lambda i, e: (e, i, 0))],
        out_shape=[_sds((t, D), bf16), _sds((4, 2, t, FB), bf16), _sds((4, t, FB), bf16)],
        args=(x, g, wg4), comm=comm)


def _mm_norm_res_fwd(hb, wb, xres, g, scale, name, comm=None):
    e_n, t, k = hb.shape
    tm = _tile(t, TM_MXU)

    def body(h_ref, w_ref, x_ref, g_ref, f_ref, o_ref):
        f = _dot(h_ref[0], w_ref[0])
        for e in range(1, e_n):
            f = f + _dot(h_ref[e], w_ref[e])
        f_ref[...] = f
        o_ref[...] = x_ref[...] + scale * _rms_fwd(f, g_ref[...])

    outs, got = _pcall(
        body, name=name, grid=(t // tm,),
        in_specs=[pl.BlockSpec((e_n, tm, k), lambda i: (0, i, 0)), _full((e_n, k, D)),
                  pl.BlockSpec((tm, D), lambda i: (i, 0)), _full((1, D))],
        out_specs=[pl.BlockSpec((tm, D), lambda i: (i, 0)), pl.BlockSpec((tm, D), lambda i: (i, 0))],
        out_shape=[_sds((t, D)), _sds((t, D))], args=(hb, wb, xres, g), comm=comm)
    return outs if comm is None else (outs, got)


def _post_bwd_nt(dxn, f, g, wb, scale, gu, name, comm=None):
    e_n, k, _ = wb.shape
    t = f.shape[0]
    tm = _tile(t, TM_MXU)
    sub = _tile(tm, TM_SUB)
    swiglu = gu is not None

    def body(*refs):
        if swiglu:
            dx_ref, f_ref, g_ref, w_ref, gu_ref, df_ref, dg_ref, dh_ref, df_s = refs
        else:
            dx_ref, f_ref, g_ref, w_ref, df_ref, dg_ref, dh_ref, df_s = refs
        i, e = pl.program_id(0), pl.program_id(1)

        @pl.when(e == 0)
        def _():
            df, dg = _rms_bwd(f_ref[...], g_ref[...], scale * dx_ref[...])
            df_s[...] = df.astype(bf16)
            df_ref[...] = df_s[...]

            @pl.when(i == 0)
            def _():
                dg_ref[...] = dg

            @pl.when(i != 0)
            def _():
                dg_ref[...] += dg

        for s in range(tm // sub):
            rows = pl.ds(s * sub, sub)
            dh = _dot_nt(df_s[rows, :], w_ref[0])
            if swiglu:
                gate = gu_ref[0, 0, rows, :].astype(f32)
                up = gu_ref[0, 1, rows, :].astype(f32)
                sg = jax.nn.sigmoid(gate)
                dh_ref[0, 0, rows, :] = (dh * up * (sg * (1.0 + gate * (1.0 - sg)))).astype(bf16)
                dh_ref[0, 1, rows, :] = (dh * gate * sg).astype(bf16)
            else:
                dh_ref[0, rows, :] = dh

    in_specs = [pl.BlockSpec((tm, D), lambda i, e: (i, 0)), pl.BlockSpec((tm, D), lambda i, e: (i, 0)), _full((1, D)),
                pl.BlockSpec((1, k, D), lambda i, e: (e, 0, 0))]
    args = [dxn, f, g, wb]
    if swiglu:
        in_specs.append(pl.BlockSpec((1, 2, tm, k), lambda i, e: (e, 0, i, 0)))
        args.append(gu)
        dh_spec = pl.BlockSpec((1, 2, tm, k), lambda i, e: (e, 0, i, 0))
        dh_shape = _sds((e_n, 2, t, k), bf16)
    else:
        dh_spec = pl.BlockSpec((1, tm, k), lambda i, e: (e, i, 0))
        dh_shape = _sds((e_n, t, k), f32)
    return _pcall(
        body, name=name, grid=(t // tm, e_n), in_specs=in_specs,
        out_specs=[pl.BlockSpec((tm, D), lambda i, e: (i, 0)), _full((1, D)), dh_spec],
        out_shape=[_sds((t, D), bf16), _sds((1, D)), dh_shape],
        scratch_shapes=[pltpu.VMEM((tm, D), bf16)], args=args, comm=comm)


def _nt_pre_bwd(dy, wb, x, g, dres, name, comm=None):
    e_n, q_n, t, k = dy.shape
    tm = _tile(t, TM_MXU)

    def body(dy_ref, w_ref, x_ref, g_ref, r_ref, dx_ref, dg_ref, acc):
        i, e = pl.program_id(0), pl.program_id(1)
        p = _dot(dy_ref[0, 0], w_ref[0, 0])
        for q in range(1, q_n):
            p = p + _dot(dy_ref[0, q], w_ref[0, q])

        @pl.when(e == 0)
        def _():
            acc[...] = p

        @pl.when(e != 0)
        def _():
            acc[...] += p

        @pl.when(e == e_n - 1)
        def _():
            dx, dg = _rms_bwd(x_ref[...], g_ref[...], acc[...])
            dx_ref[...] = r_ref[...] + dx

            @pl.when(i == 0)
            def _():
                dg_ref[...] = dg

            @pl.when(i != 0)
            def _():
                dg_ref[...] += dg

    return _pcall(
        body, name=name, grid=(t // tm, e_n),
        in_specs=[pl.BlockSpec((1, q_n, tm, k), lambda i, e: (e, 0, i, 0)),
                  pl.BlockSpec((1, q_n, k, D), lambda i, e: (e, 0, 0, 0)),
                  pl.BlockSpec((tm, D), lambda i, e: (i, 0)), _full((1, D)),
                  pl.BlockSpec((tm, D), lambda i, e: (i, 0))],
        out_specs=[pl.BlockSpec((tm, D), lambda i, e: (i, 0)), _full((1, D))],
        out_shape=[_sds((t, D)), _sds((1, D))],
        scratch_shapes=[pltpu.VMEM((tm, D), f32)], args=(dy, wb, x, g, dres), comm=comm)


def _wgrad(a, b, a_batched, b_batched, tn, name, comm=None):
    t, k = a.shape[1], a.shape[2]
    nn = b.shape[2]
    nb = max(a.shape[0], b.shape[0])
    tt = _tile(t, TM_MXU)
    nt = t // tt

    def body(a_ref, b_ref, o_ref, acc):
        @pl.when(pl.program_id(2) == 0)
        def _():
            acc[...] = _dot_tn(a_ref[0], b_ref[0])

        @pl.when(pl.program_id(2) != 0)
        def _():
            acc[...] += _dot_tn(a_ref[0], b_ref[0])

        @pl.when(pl.program_id(2) == nt - 1)
        def _():
            o_ref[0] = acc[...].astype(bf16)

    (out,), got = _pcall(
        body, name=name, grid=(nb, nn // tn, nt),
        in_specs=[pl.BlockSpec((1, tt, k), (lambda n, j, s: (n, s, 0)) if a_batched else (lambda n, j, s: (0, s, 0))),
                  pl.BlockSpec((1, tt, tn), (lambda n, j, s: (n, s, j)) if b_batched else (lambda n, j, s: (0, s, j)))],
        out_specs=[pl.BlockSpec((1, k, tn), lambda n, j, s: (n, 0, j))],
        out_shape=[_sds((nb, k, nn), bf16)],
        scratch_shapes=[pltpu.VMEM((k, tn), f32)], args=(a, b), comm=comm)
    return out if comm is None else (out, got)


def _norm_mm_fwd(x, g, w, name):
    t = x.shape[0]
    tm = _tile(t, TM_VPU)

    def body(x_ref, g_ref, w_ref, xn_ref, ps_ref, pc_ref):
        xn = _rms_fwd(x_ref[...], g_ref[...]).astype(bf16)
        xn_ref[...] = xn
        ps_ref[...] = _dot_nt(xn, w_ref[:DS, :])
        pc_ref[...] = _dot_nt(xn, w_ref[DS:, :])

    return pl.pallas_call(
        body, name=name, grid=(t // tm,),
        in_specs=[pl.BlockSpec((tm, D), lambda i: (i, 0)), _full((1, D)), _full((DIN, D))],
        out_specs=[pl.BlockSpec((tm, D), lambda i: (i, 0)), pl.BlockSpec((tm, DS), lambda i: (i, 0)),
                   pl.BlockSpec((tm, 2 * DC), lambda i: (i, 0))],
        out_shape=[_sds((t, D), bf16), _sds((t, DS)), _sds((t, 2 * DC))],
    )(x, g, w)


def _loss_fwd_bwd(y, tgt, name):
    t = y.shape[0]
    tm = _tile(t, TM_MXU)

    def body(y_ref, t_ref, dy_ref, l_ref):
        e = y_ref[...] - t_ref[...]
        dy_ref[...] = e * (1.0 / D)
        s = 0.5 * jnp.sum(jnp.mean(e * e, axis=-1, keepdims=True), axis=0, keepdims=True)

        @pl.when(pl.program_id(0) == 0)
        def _():
            l_ref[...] = s

        @pl.when(pl.program_id(0) != 0)
        def _():
            l_ref[...] += s

    return pl.pallas_call(
        body, name=name, grid=(t // tm,),
        in_specs=[pl.BlockSpec((tm, D), lambda i: (i, 0)), pl.BlockSpec((tm, D), lambda i: (i, 0))],
        out_specs=[pl.BlockSpec((tm, D), lambda i: (i, 0)), _full((1, 1))],
        out_shape=[_sds((t, D)), _sds((1, 1))],
    )(y, tgt)


def _segsum_b(x, seg, seg_t):
    return mm(mm(x, seg, "nn", EB), seg_t, "nn", EB)


def _prep_fn(psl, w0, a0, k_k, k_a, wup, aup, gup, seg, seg_t):
    r, k, v = psl[:, :DR], psl[:, DR:2 * DR], psl[:, 2 * DR:3 * DR]
    xwa, xg = psl[:, 3 * DR:3 * DR + LANES], psl[:, 3 * DR + LANES:]
    d = w0 + mm(jnp.tanh(xwa), wup, "nn", LORA)
    lw = -DECAY_SCALE * jax.nn.sigmoid(d)
    a = jax.nn.sigmoid(a0 + mm(xwa, aup, "nn", LORA))
    g = mm(jax.nn.sigmoid(xg), gup, "nn", LORA)
    kkr = k * k_k
    kk = kkr * lax.rsqrt(jnp.maximum(_segsum_b(kkr * kkr, seg, seg_t), 1e-12))
    kp = k * (1.0 + (a - 1.0) * k_a)
    return r, lw, kp, v, -kk, kk * a, g


def _shifted(ps, halo_row, first):
    prev = jnp.where(first, 0.0, halo_row)
    sh = pltpu.roll(ps, 1, 0)
    row = lax.broadcasted_iota(jnp.int32, ps.shape, 0)
    return jnp.where(row == 0, prev, sh)


def _heads_split(ref, val):
    for h in range(NH):
        ref[h] = val[:, h * HN:(h + 1) * HN]


def _heads_merge(ref):
    return jnp.concatenate([ref[h] for h in range(NH)], axis=-1)


_PREP_PARAM_SHAPES = [(1, DS), (1, DR), (1, DR), (1, DR), (1, DR), (LANES, DR), (LANES, DR), (LANES, DR),
                      (DR, LANES), (LANES, DR)]


def _prep_fwd(ps, params, name):
    t = ps.shape[0]
    tm = _tile(t, TM_VPU)
    hb = tm // 8

    def body(ps_ref, halo_ref, mu_ref, *rest):
        prm = [r[...] for r in rest[:9]]
        outs = rest[9:]
        x = ps_ref[...]
        sh = _shifted(x, halo_ref[pl.ds(7, 1), :], pl.program_id(0) == 0)
        psl = x + (sh - x) * mu_ref[...]
        vals = _prep_fn(psl, *prm)
        for ref, val in zip(outs[:6], vals[:6]):
            _heads_split(ref, val)
        outs[6][...] = vals[6]

    hm = pl.BlockSpec((NH, tm, HN), lambda i: (0, i, 0))
    return pl.pallas_call(
        body, name=name, grid=(t // tm,),
        in_specs=[pl.BlockSpec((tm, DS), lambda i: (i, 0)),
                  pl.BlockSpec((8, DS), lambda i: (jnp.maximum(i * hb - 1, 0), 0))]
        + [_full(s) for s in _PREP_PARAM_SHAPES],
        out_specs=[hm] * 6 + [pl.BlockSpec((tm, DR), lambda i: (i, 0))],
        out_shape=[_sds((NH, t, HN))] * 6 + [_sds((t, DR))],
    )(ps, ps, *params)


def _prep_bwd(ps, params, cts_hm, cts_tm, name):
    t = ps.shape[0]
    tm = _tile(t, TM_VPU)
    hb = tm // 8

    def body(ps_ref, halo_ref, mu_ref, *rest):
        prm = [r[...] for r in rest[:9]]
        chm = rest[9:15]
        ctm = rest[15:19]
        dpsl_ref = rest[19]
        gouts = rest[20:27]
        x = ps_ref[...]
        sh = _shifted(x, halo_ref[pl.ds(7, 1), :], pl.program_id(0) == 0)
        psl = x + (sh - x) * mu_ref[...]
        seg, seg_t = prm[7], prm[8]
        _, vjp = jax.vjp(lambda p, *w: _prep_fn(p, *w, seg, seg_t), psl, *prm[:7])
        c = [_heads_merge(r) for r in chm]
        cts = (c[0] + ctm[0][...], c[1], c[2] + ctm[1][...], c[3] + ctm[2][...], c[4], c[5], ctm[3][...])
        grads = vjp(cts)
        dpsl_ref[...] = grads[0]
        for ref, gval in zip(gouts, grads[1:]):
            @pl.when(pl.program_id(0) == 0)
            def _(ref=ref, gval=gval):
                ref[...] = gval

            @pl.when(pl.program_id(0) != 0)
            def _(ref=ref, gval=gval):
                ref[...] += gval

    hm = pl.BlockSpec((NH, tm, HN), lambda i: (0, i, 0))
    tk = pl.BlockSpec((tm, DR), lambda i: (i, 0))
    gshapes = _PREP_PARAM_SHAPES[1:8]
    return pl.pallas_call(
        body, name=name, grid=(t // tm,),
        in_specs=[pl.BlockSpec((tm, DS), lambda i: (i, 0)),
                  pl.BlockSpec((8, DS), lambda i: (jnp.maximum(i * hb - 1, 0), 0))]
        + [_full(s) for s in _PREP_PARAM_SHAPES] + [hm] * 6 + [tk] * 4,
        out_specs=[pl.BlockSpec((tm, DS), lambda i: (i, 0))] + [_full(s) for s in gshapes],
        out_shape=[_sds((t, DS))] + [_sds(s) for s in gshapes],
    )(ps, ps, *params, *cts_hm, *cts_tm)


def _shift_bwd(dpsl, ps, mu, dpc, name):
    t = ps.shape[0]
    tm = _tile(t, TM_VPU)
    hb = tm // 8
    last_blk = t // 8 - 1

    def body(d_ref, dn_ref, ps_ref, halo_ref, mu_ref, dpc_ref, dp_ref, dmu_ref):
        i = pl.program_id(0)
        mu_v = mu_ref[...]
        d = d_ref[...]
        nxt = jnp.where(i == pl.num_programs(0) - 1, 0.0, dn_ref[pl.ds(0, 1), :])
        up = pltpu.roll(d, tm - 1, 0)
        row = lax.broadcasted_iota(jnp.int32, d.shape, 0)
        up = jnp.where(row == tm - 1, nxt, up)
        dp_ref[:, :DS] = (d * (1.0 - mu_v) + up * mu_v).astype(bf16)
        dp_ref[:, DS:] = dpc_ref[...].astype(bf16)
        x = ps_ref[...]
        sh = _shifted(x, halo_ref[pl.ds(7, 1), :], i == 0)
        dmu = jnp.sum(d * (sh - x), axis=0, keepdims=True)

        @pl.when(i == 0)
        def _():
            dmu_ref[...] = dmu

        @pl.when(i != 0)
        def _():
            dmu_ref[...] += dmu

    return pl.pallas_call(
        body, name=name, grid=(t // tm,),
        in_specs=[pl.BlockSpec((tm, DS), lambda i: (i, 0)),
                  pl.BlockSpec((8, DS), lambda i: (jnp.minimum((i + 1) * hb, last_blk), 0)),
                  pl.BlockSpec((tm, DS), lambda i: (i, 0)),
                  pl.BlockSpec((8, DS), lambda i: (jnp.maximum(i * hb - 1, 0), 0)),
                  _full((1, DS)), pl.BlockSpec((tm, 2 * DC), lambda i: (i, 0))],
        out_specs=[pl.BlockSpec((tm, DIN), lambda i: (i, 0)), _full((1, DS))],
        out_shape=[_sds((t, DIN), bf16), _sds((1, DS))],
    )(dpsl, dpsl, ps, ps, mu, dpc)


def _post_fn(y, r, kp, v, g, gn_w, gn_b, r_k, seg, seg_t):
    mu = _segsum_b(y, seg, seg_t) * (1.0 / HN)
    yc = y - mu
    var = _segsum_b(yc * yc, seg, seg_t) * (1.0 / HN)
    yo = yc * lax.rsqrt(var + GN_EPS) * gn_w + gn_b
    bonus = _segsum_b(r * kp * r_k, seg, seg_t) * v
    return (yo + bonus) * g


_POST_PARAM_SHAPES = [(1, DR), (1, DR), (1, DR), (DR, LANES), (LANES, DR)]


def _post_fwd(y, r, kp, v, g, out_b, params, name):
    t = g.shape[0]
    tm = _tile(t, TM_VPU)

    def body(y_ref, r_ref, k_ref, v_ref, g_ref, ob_ref, *rest):
        prm = [p[...] for p in rest[:5]]
        cat_ref = rest[5]
        oa = _post_fn(_heads_merge(y_ref), _heads_merge(r_ref), _heads_merge(k_ref), _heads_merge(v_ref),
                      g_ref[...], *prm)
        cat_ref[:, :DR] = oa.astype(bf16)
        cat_ref[:, DR:] = ob_ref[...]

    hm = pl.BlockSpec((NH, tm, HN), lambda i: (0, i, 0))
    tk = pl.BlockSpec((tm, DR), lambda i: (i, 0))
    return pl.pallas_call(
        body, name=name, grid=(t // tm,),
        in_specs=[hm] * 4 + [tk, tk] + [_full(s) for s in _POST_PARAM_SHAPES],
        out_specs=pl.BlockSpec((tm, 2 * DR), lambda i: (i, 0)),
        out_shape=_sds((t, 2 * DR), bf16),
    )(y, r, kp, v, g, out_b, *params)


def _post_bwd(dcat, y, r, kp, v, g, params, name, comm=None):
    t = g.shape[0]
    tm = _tile(t, TM_VPU)

    def body(dc_ref, y_ref, r_ref, k_ref, v_ref, g_ref, *rest):
        prm = [p[...] for p in rest[:5]]
        dy_ref, dr_ref, dk_ref, dv_ref, dg_ref = rest[5:10]
        gouts = rest[10:13]
        seg, seg_t = prm[3], prm[4]
        _, vjp = jax.vjp(lambda *a: _post_fn(*a, seg, seg_t),
                         _heads_merge(y_ref), _heads_merge(r_ref), _heads_merge(k_ref), _heads_merge(v_ref),
                         g_ref[...], *prm[:3])
        grads = vjp(dc_ref[...])
        _heads_split(dy_ref, grads[0])
        dr_ref[...] = grads[1]
        dk_ref[...] = grads[2]
        dv_ref[...] = grads[3]
        dg_ref[...] = grads[4]
        for ref, gval in zip(gouts, grads[5:]):
            @pl.when(pl.program_id(0) == 0)
            def _(ref=ref, gval=gval):
                ref[...] = gval

            @pl.when(pl.program_id(0) != 0)
            def _(ref=ref, gval=gval):
                ref[...] += gval

    hm = pl.BlockSpec((NH, tm, HN), lambda i: (0, i, 0))
    tk = pl.BlockSpec((tm, DR), lambda i: (i, 0))
    return _pcall(
        body, name=name, grid=(t // tm,),
        in_specs=[tk] + [hm] * 4 + [tk] + [_full(s) for s in _POST_PARAM_SHAPES],
        out_specs=[hm] + [tk] * 4 + [_full((1, DR))] * 3,
        out_shape=[_sds((NH, t, HN))] + [_sds((t, DR))] * 4 + [_sds((1, DR))] * 3,
        args=(dcat, y, r, kp, v, g, *params), comm=comm)


def _r1_fn(r, lw, k, v, z, b):
    c = r.shape[1]
    row = lax.broadcasted_iota(jnp.int32, (c, c), 0)
    col = lax.broadcasted_iota(jnp.int32, (c, c), 1)
    incl = (row >= col)[None]
    strict = (row > col)[None]
    eye = (row == col).astype(f32)[None]
    tri = jnp.broadcast_to((row >= col).astype(f32)[None], (r.shape[0], c, c))
    cum = mm(tri, lw, "nn", EA)
    tot = jnp.sum(lw, axis=1, keepdims=True)
    zt = z * jnp.exp(cum - lw)
    rt = r * jnp.exp(cum)
    g_inv = jnp.exp(-cum)
    g_rem = jnp.exp(tot - cum)
    bt, kt = b * g_inv, k * g_inv
    bh, kh = b * g_rem, k * g_rem
    lb = jnp.where(strict, mm(zt, bt, "nt", RGM), 0.0)
    lk = jnp.where(strict, mm(zt, kt, "nt", RGM), 0.0)
    mb = jnp.where(incl, mm(rt, bt, "nt", RGM), 0.0)
    mk = jnp.where(incl, mm(rt, kt, "nt", RGM), 0.0)
    p = eye + lb
    m = lb
    for _ in range(5):
        m = mm(m, m, "nn", RIM)
        p = p + mm(m, p, "nn", RIM)
    w1 = mm(p, zt, "nn", RWM)
    w2 = mm(p, mm(lk, v, "nn", RWM), "nn", RWM)
    a = mm(w1, bh, "tn", RAM) + eye * jnp.exp(tot)
    g = mm(w2, bh, "tn", RAM) + mm(v, kh, "tn", RAM)
    q = rt + mm(mb, w1, "nn", RWM)
    yl = mm(mb, w2, "nn", RWM) + mm(mk, v, "nn", RWM)
    return a, g, q, yl


def _chunks_in(ref, n):
    return jnp.concatenate([ref[:, s * CHUNK:(s + 1) * CHUNK, :] for s in range(n)], axis=0)


def _chunks_out(ref, val, n):
    for s in range(n):
        ref[:, s * CHUNK:(s + 1) * CHUNK, :] = val[s * NH:(s + 1) * NH]


def _r1_fwd(ins, name, comm=None):
    t = ins[0].shape[1]
    n = R1_CHUNKS
    nc = t // CHUNK

    def body(r, lw, k, v, z, b, a_ref, g_ref, q_ref, yl_ref):
        a, g, q, yl = _r1_fn(*[_chunks_in(x, n) for x in (r, lw, k, v, z, b)])
        for s in range(n):
            a_ref[s] = a[s * NH:(s + 1) * NH]
            g_ref[s] = g[s * NH:(s + 1) * NH]
        _chunks_out(q_ref, q, n)
        _chunks_out(yl_ref, yl, n)

    ck = pl.BlockSpec((NH, n * CHUNK, HN), lambda c: (0, c, 0))
    st = pl.BlockSpec((n, NH, HN, HN), lambda c: (c, 0, 0, 0))
    return _pcall(
        body, name=name, grid=(nc // n,), in_specs=[ck] * 6, out_specs=[st, st, ck, ck],
        out_shape=[_sds((nc, NH, HN, HN))] * 2 + [_sds((NH, t, HN))] * 2, args=tuple(ins), comm=comm)


def _r1_bwd(ins, da, dg, dq, dyl, name, comm=None):
    t = ins[0].shape[1]
    n = R1_CHUNKS
    nc = t // CHUNK

    def body(r, lw, k, v, z, b, da_ref, dg_ref, dq_ref, dyl_ref, *outs):
        _, vjp = jax.vjp(_r1_fn, *[_chunks_in(x, n) for x in (r, lw, k, v, z, b)])
        cts = (jnp.concatenate([da_ref[s] for s in range(n)], axis=0),
               jnp.concatenate([dg_ref[s] for s in range(n)], axis=0), _chunks_in(dq_ref, n), _chunks_in(dyl_ref, n))
        for ref, gval in zip(outs, vjp(cts)):
            _chunks_out(ref, gval, n)

    ck = pl.BlockSpec((NH, n * CHUNK, HN), lambda c: (0, c, 0))
    st = pl.BlockSpec((n, NH, HN, HN), lambda c: (c, 0, 0, 0))
    return _pcall(
        body, name=name, grid=(nc // n,), in_specs=[ck] * 6 + [st, st, ck, ck], out_specs=[ck] * 6,
        out_shape=[_sds((NH, t, HN))] * 6, args=(*ins, da, dg, dq, dyl), comm=comm)


def _r2_fwd(a, g, q, yl, name):
    nc = a.shape[0]
    t = q.shape[1]

    def body(a_ref, g_ref, q_ref, yl_ref, y_ref, s_ref, s):
        @pl.when(pl.program_id(0) == 0)
        def _():
            s[...] = jnp.zeros_like(s)
        s0 = s[...]
        for j in range(n):
            rows = pl.ds(j * CHUNK, CHUNK)
            s_ref[j] = s0
            y_ref[:, rows, :] = mm(q_ref[:, rows, :], s0, "nt", R2M) + yl_ref[:, rows, :]
            s0 = mm(s0, a_ref[j], "nn", R2M) + g_ref[j]
        s[...] = s0

    n = math.gcd(nc, R2_CHUNKS)
    ck = pl.BlockSpec((NH, n * CHUNK, HN), lambda c: (0, c, 0))
    st = pl.BlockSpec((n, NH, HN, HN), lambda c: (c, 0, 0, 0))
    return pl.pallas_call(
        body, name=name, grid=(nc // n,), in_specs=[st, st, ck, ck], out_specs=[ck, st],
        out_shape=[_sds((NH, t, HN)), _sds((nc, NH, HN, HN))],
        scratch_shapes=[pltpu.VMEM((NH, HN, HN), f32)],
    )(a, g, q, yl)


def _r2_bwd(dy, q, s_all, a, name):
    nc = a.shape[0]
    t = q.shape[1]

    def body(dy_ref, q_ref, s_ref, a_ref, dq_ref, da_ref, dg_ref, ds):
        @pl.when(pl.program_id(0) == 0)
        def _():
            ds[...] = jnp.zeros_like(ds)
        dsn = ds[...]
        for j in reversed(range(n)):
            rows = pl.ds(j * CHUNK, CHUNK)
            s0 = s_ref[j]
            dyv = dy_ref[:, rows, :]
            dg_ref[j] = dsn
            da_ref[j] = mm(s0, dsn, "tn", R2M)
            dq_ref[:, rows, :] = mm(dyv, s0, "nn", R2M)
            dsn = mm(dsn, a_ref[j], "nt", R2M) + mm(dyv, q_ref[:, rows, :], "tn", R2M)
        ds[...] = dsn

    n = math.gcd(nc, R2_CHUNKS)
    nb = nc // n
    ck = pl.BlockSpec((NH, n * CHUNK, HN), lambda c: (0, nb - 1 - c, 0))
    st = pl.BlockSpec((n, NH, HN, HN), lambda c: (nb - 1 - c, 0, 0, 0))
    return pl.pallas_call(
        body, name=name, grid=(nb,), in_specs=[ck, ck, st, st], out_specs=[ck, st, st],
        out_shape=[_sds((NH, t, HN)), _sds((nc, NH, HN, HN)), _sds((nc, NH, HN, HN))],
        scratch_shapes=[pltpu.VMEM((NH, HN, HN), f32)],
    )(dy, q, s_all, a)


def _ln_silu(c, w, b):
    mu = jnp.mean(c, axis=-1, keepdims=True)
    cc = c - mu
    var = jnp.mean(cc * cc, axis=-1, keepdims=True)
    u = cc * lax.rsqrt(var + LN_EPS) * w + b
    return u * jax.nn.sigmoid(u)


def _glu_tile(pc):
    return pc[:, :DC] * jax.nn.sigmoid(pc[:, DC:])


def _fill_glu(s_ref, pc_ref, halo_ref, first, tm):
    s_ref[pl.ds(0, HALO), :] = jnp.where(first, 0.0, _glu_tile(halo_ref[...]))
    s_ref[pl.ds(HALO, tm), :] = _glu_tile(pc_ref[...])


def _conv_fwd(pc, dw, cb, lnw, lnb, name):
    t = pc.shape[0]
    tm = _tile(t, TM_VPU)
    hb = tm // HALO

    def body(pc_ref, halo_ref, dw_ref, cb_ref, w_ref, b_ref, c_ref, o_ref, s):
        _fill_glu(s, pc_ref, halo_ref, pl.program_id(0) == 0, tm)
        for r0 in range(0, tm, CONV_ROWS):
            rows = pl.ds(r0, CONV_ROWS)
            acc = s[pl.ds(r0 + HALO - CW + 1, CONV_ROWS), :] * dw_ref[pl.ds(0, 1), :]
            for j in range(1, CW):
                acc = acc + s[pl.ds(r0 + HALO - CW + 1 + j, CONV_ROWS), :] * dw_ref[pl.ds(j, 1), :]
            c = acc + cb_ref[...]
            c_ref[rows, :] = c
            o_ref[rows, :] = _ln_silu(c, w_ref[...], b_ref[...]).astype(bf16)

    return pl.pallas_call(
        body, name=name, grid=(t // tm,),
        in_specs=[pl.BlockSpec((tm, 2 * DC), lambda i: (i, 0)),
                  pl.BlockSpec((HALO, 2 * DC), lambda i: (jnp.maximum(i * hb - 1, 0), 0)),
                  _full((CW, DC)), _full((1, DC)), _full((1, DC)), _full((1, DC))],
        out_specs=[pl.BlockSpec((tm, DC), lambda i: (i, 0)), pl.BlockSpec((tm, DC), lambda i: (i, 0))],
        out_shape=[_sds((t, DC)), _sds((t, DC), bf16)],
        scratch_shapes=[pltpu.VMEM((HALO + tm, DC), f32)],
    )(pc, pc, dw, cb, lnw, lnb)


def _conv_bwd1(dcat, c, lnw, lnb, name):
    t = c.shape[0]
    tm = _tile(t, TM_VPU)

    def body(d_ref, c_ref, w_ref, b_ref, dc_ref, dw_ref, db_ref, dcb_ref):
        _, vjp = jax.vjp(_ln_silu, c_ref[...], w_ref[...], b_ref[...])
        dc, dw, db = vjp(d_ref[...])
        dc_ref[...] = dc
        dcb = jnp.sum(dc, axis=0, keepdims=True)
        for ref, gval in ((dw_ref, dw), (db_ref, db), (dcb_ref, dcb)):
            @pl.when(pl.program_id(0) == 0)
            def _(ref=ref, gval=gval):
                ref[...] = gval

            @pl.when(pl.program_id(0) != 0)
            def _(ref=ref, gval=gval):
                ref[...] += gval

    tk = pl.BlockSpec((tm, DC), lambda i: (i, 0))
    return pl.pallas_call(
        body, name=name, grid=(t // tm,),
        in_specs=[pl.BlockSpec((tm, DC), lambda i: (i, 1)), tk, _full((1, DC)), _full((1, DC))],
        out_specs=[tk] + [_full((1, DC))] * 3,
        out_shape=[_sds((t, DC))] + [_sds((1, DC))] * 3,
    )(dcat, c, lnw, lnb)


def _conv_bwd2(dc, pc, dw, name):
    t = pc.shape[0]
    tm = _tile(t, TM_VPU)
    hb = tm // HALO
    last_blk = t // HALO - 1

    def body(dc_ref, dn_ref, pc_ref, halo_ref, dw_ref, dpc_ref, ddw_ref, s, sd, acc):
        i = pl.program_id(0)
        _fill_glu(s, pc_ref, halo_ref, i == 0, tm)
        sd[pl.ds(0, tm), :] = dc_ref[...]
        sd[pl.ds(tm, HALO), :] = jnp.where(i == pl.num_programs(0) - 1, 0.0, dn_ref[...])
        for r0 in range(0, tm, CONV_ROWS):
            rows = pl.ds(r0, CONV_ROWS)
            dcb = sd[rows, :]
            dglu = None
            for j in range(CW):
                term = sd[pl.ds(r0 + CW - 1 - j, CONV_ROWS), :] * dw_ref[pl.ds(j, 1), :]
                dglu = term if dglu is None else dglu + term
                part = dcb * s[pl.ds(r0 + HALO - CW + 1 + j, CONV_ROWS), :]
                part8 = part[0:8]
                for q in range(8, CONV_ROWS, 8):
                    part8 = part8 + part[q:q + 8]
                if r0 == 0:
                    acc[pl.ds(8 * j, 8), :] = part8
                else:
                    acc[pl.ds(8 * j, 8), :] += part8
            a = pc_ref[rows, :DC]
            sg = jax.nn.sigmoid(pc_ref[rows, DC:])
            dpc_ref[rows, :DC] = dglu * sg
            dpc_ref[rows, DC:] = dglu * a * sg * (1.0 - sg)
        rows = [jnp.sum(acc[pl.ds(8 * j, 8), :], axis=0, keepdims=True) for j in range(CW)]

        @pl.when(i == 0)
        def _():
            for j in range(CW):
                ddw_ref[pl.ds(j, 1), :] = rows[j]

        @pl.when(i != 0)
        def _():
            for j in range(CW):
                ddw_ref[pl.ds(j, 1), :] += rows[j]

    return pl.pallas_call(
        body, name=name, grid=(t // tm,),
        in_specs=[pl.BlockSpec((tm, DC), lambda i: (i, 0)),
                  pl.BlockSpec((HALO, DC), lambda i: (jnp.minimum((i + 1) * hb, last_blk), 0)),
                  pl.BlockSpec((tm, 2 * DC), lambda i: (i, 0)),
                  pl.BlockSpec((HALO, 2 * DC), lambda i: (jnp.maximum(i * hb - 1, 0), 0)),
                  _full((CW, DC))],
        out_specs=[pl.BlockSpec((tm, 2 * DC), lambda i: (i, 0)), _full((CW, DC))],
        out_shape=[_sds((t, 2 * DC)), _sds((CW, DC))],
        scratch_shapes=[pltpu.VMEM((HALO + tm, DC), f32), pltpu.VMEM((tm + HALO, DC), f32),
                        pltpu.VMEM((8 * CW, DC), f32)],
    )(dc, dc, pc, pc, dw)


def _adamw(w, g, m, v):
    m = ADAM_B1 * m + (1.0 - ADAM_B1) * g
    v = ADAM_B2 * v + (1.0 - ADAM_B2) * (g * g)
    m_hat = m / (1.0 - ADAM_B1 ** ADAM_STEP)
    v_hat = v / (1.0 - ADAM_B2 ** ADAM_STEP)
    delta = -ADAM_LR * (m_hat / (jnp.sqrt(v_hat) + ADAM_EPS) + ADAM_WD * w)
    return delta, m, v


def _adam_slots(w, slots, m, v, name):
    r, c = w.shape
    tr = next(cand for cand in (512, 352, 256, 128, r) if r % cand == 0)

    def body(w_ref, s_ref, m_ref, v_ref, g_ref, d_ref, nm_ref, nv_ref):
        g = s_ref[0].astype(f32)
        for k in range(1, NDEV):
            g = g + s_ref[k].astype(f32)
        delta, nm, nv = _adamw(w_ref[...], g, m_ref[...], v_ref[...])
        g_ref[...] = g
        d_ref[...] = delta
        nm_ref[...] = nm
        nv_ref[...] = nv

    blk = pl.BlockSpec((tr, c), lambda i: (i, 0))
    return pl.pallas_call(
        body, name=name, grid=(r // tr,),
        in_specs=[blk, pl.BlockSpec((NDEV, tr, c), lambda i: (0, i, 0)), blk, blk],
        out_specs=[blk] * 4, out_shape=[_sds((r, c))] * 4,
    )(w, slots, m, v)


def _sum_slots(slots, name):
    _, r, c = slots.shape

    def body(s_ref, o_ref):
        g = s_ref[0]
        for k in range(1, NDEV):
            g = g + s_ref[k]
        o_ref[...] = g

    return pl.pallas_call(body, name=name, in_specs=[_full((NDEV, r, c))], out_specs=_full((r, c)),
                          out_shape=_sds((r, c)), grid=(1,))(slots)


def _adam_vec(w, g, m, v, name):
    shp = w.shape

    def body(w_ref, g_ref, m_ref, v_ref, d_ref, nm_ref, nv_ref):
        delta, nm, nv = _adamw(w_ref[...], g_ref[...], m_ref[...], v_ref[...])
        d_ref[...] = delta
        nm_ref[...] = nm
        nv_ref[...] = nv

    return pl.pallas_call(body, name=name, grid=(1,), in_specs=[_full(shp)] * 4, out_specs=[_full(shp)] * 3,
                          out_shape=[_sds(shp)] * 3)(w, g, m, v)


def _pack(pieces, total):
    flat = []
    n = 0
    for p in pieces:
        p = p.reshape(-1)
        pad = (-p.shape[0]) % LANES
        if pad:
            p = jnp.pad(p, (0, pad))
        flat.append(p)
        n += p.shape[0]
    if total > n:
        flat.append(jnp.zeros((total - n,), f32))
    return jnp.concatenate(flat).reshape(total // LANES, LANES)


def _unpack(vec, shapes):
    flat = vec.reshape(-1)
    out, off = [], 0
    for s in shapes:
        n = math.prod(s)
        out.append(flat[off:off + n].reshape(s))
        off += n + (-n) % LANES
    return out


def _round_up(n, m):
    return (n + m - 1) // m * m


def kernel(x, ffn1_norm_pre, ffn1_norm_post, ffn1_w_gu, ffn1_w_down, mix_norm_pre, mix_norm_post, w_in, shift_mu, w_up, w0, a_up, a0, g_up, k_k, k_a, r_k, gn_w, gn_b, conv_dw, conv_b, conv_ln_w, conv_ln_b, w_out, ffn2_norm_pre, ffn2_norm_post, ffn2_w_gu, ffn2_w_down, loss_target, m_ffn1_norm_pre, m_ffn1_norm_post, m_ffn1_w_gu, m_ffn1_w_down, m_mix_norm_pre, m_mix_norm_post, m_w_in, m_shift_mu, m_w_up, m_w0, m_a_up, m_a0, m_g_up, m_k_k, m_k_a, m_r_k, m_gn_w, m_gn_b, m_conv_dw, m_conv_b, m_conv_ln_w, m_conv_ln_b, m_w_out, m_ffn2_norm_pre, m_ffn2_norm_post, m_ffn2_w_gu, m_ffn2_w_down, v_ffn1_norm_pre, v_ffn1_norm_post, v_ffn1_w_gu, v_ffn1_w_down, v_mix_norm_pre, v_mix_norm_post, v_w_in, v_shift_mu, v_w_up, v_w0, v_a_up, v_a0, v_g_up, v_k_k, v_k_a, v_r_k, v_gn_w, v_gn_b, v_conv_dw, v_conv_b, v_conv_ln_w, v_conv_ln_b, v_w_out, v_ffn2_norm_pre, v_ffn2_norm_post, v_ffn2_w_gu, v_ffn2_w_down):
    t = x.shape[1]
    me = 4 * lax.axis_index("x") + 2 * lax.axis_index("y") + lax.axis_index("c")
    x0 = x.reshape(t, D)
    tgt = loss_target.reshape(t, D)

    def shard(a):
        return a[0].astype(bf16)

    def shard_t(a):
        return jnp.swapaxes(a[0], 0, 1).astype(bf16)

    (wg1,) = _exchange([shard_t(ffn1_w_gu)], ["gather"], [SLOT42], "gather_wg1")

    def cols(a):
        return jnp.transpose(a, (1, 0, 2)).reshape(a.shape[1], NDEV * a.shape[2])

    lane = jnp.arange(DR, dtype=jnp.int32) // HN
    seg = (lane[:, None] == jnp.arange(LANES, dtype=jnp.int32)[None, :]).astype(f32)
    seg_t = seg.T
    rk_row = r_k.reshape(1, DR)
    post_params = (gn_w, gn_b, rk_row, seg, seg_t)

    (xn1, gu1, h1), (wd1, win_g) = _ffn_in_fwd(x0, ffn1_norm_pre, wg1, "ffn1_in_fwd", comm=(
        [shard(ffn1_w_down), shard_t(w_in)], ["gather"] * 2, [SLOT8] * 2))
    wd1 = wd1.reshape(4, FB, D)
    win_t = win_g.reshape(DIN, D)
    (f1, x1), (wout_g, wup_g, aup_g, gup_g, cdw_g) = _mm_norm_res_fwd(
        h1, wd1, x0, ffn1_norm_post, 0.5, "ffn1_out_fwd", comm=(
            [shard(w_out), w_up[0], a_up[0], g_up[0], conv_dw[0]], ["gather"] * 5, [SLOT8] * 5))
    wout_full = wout_g.reshape(1, D, D)
    wup_full, aup_full, gup_full, cdw_full = cols(wup_g), cols(aup_g), cols(gup_g), cols(cdw_g)
    zeros64 = jnp.zeros((HN, DR), f32)
    wup_pad = jnp.concatenate([wup_full, zeros64], axis=0)
    aup_pad = jnp.concatenate([zeros64, aup_full], axis=0)
    prep_params = (shift_mu, w0, a0, k_k, k_a, wup_pad, aup_pad, gup_full, seg, seg_t)
    hm, ps, pc = _norm_mm_fwd(x1, mix_norm_pre, win_t, "mix_in_fwd")
    rec_in = _prep_fwd(ps, prep_params, "prep_fwd")
    g_gate = rec_in[6]
    rec_in = rec_in[:6]
    (a_c, g_c, q_c, yl_c), got = _r1_fwd(rec_in, "r1_fwd", comm=(
        [shard_t(ffn2_w_gu), shard(ffn2_w_down)], ["gather"] * 2, [SLOT42, SLOT8]))
    wg2, wd2 = got
    wd2 = wd2.reshape(4, FB, D)
    y_rec, s_all = _r2_fwd(a_c, g_c, q_c, yl_c, "r2_fwd")
    c_conv, out_b = _conv_fwd(pc, cdw_full, conv_b, conv_ln_w, conv_ln_b, "conv_fwd")
    cat = _post_fwd(y_rec, rec_in[0], rec_in[2], rec_in[3], g_gate, out_b, post_params, "post_fwd")
    fm, x2 = _mm_norm_res_fwd(cat.reshape(1, t, D), wout_full, x1, mix_norm_post, 1.0, "mix_out_fwd")
    (xn2, gu2, h2), _ = _ffn_in_fwd(x2, ffn2_norm_pre, wg2, "ffn2_in_fwd")
    f2, x3 = _mm_norm_res_fwd(h2, wd2, x2, ffn2_norm_post, 0.5, "ffn2_out_fwd")
    dy, loss_part = _loss_fwd_bwd(x3, tgt, "loss")

    (df2, dg_post2, dgu2), _ = _post_bwd_nt(dy, f2, ffn2_norm_post, wd2, 0.5, gu2, "ffn2_out_bwd")
    (dx2, dg_pre2), _ = _nt_pre_bwd(dgu2, wg2, x2, ffn2_norm_pre, dy, "ffn2_in_bwd")
    dwd2 = _wgrad(h2, df2.reshape(1, t, D), True, False, D, "ffn2_wd_grad")
    dwg2 = _wgrad(dgu2.reshape(8, t, FB), xn2.reshape(1, t, D), True, False, D, "ffn2_wg_grad")
    (dfm, dg_postm, dcat), _ = _post_bwd_nt(dx2, fm, mix_norm_post, wout_full, 1.0, None, "mix_out_bwd")
    dwout = _wgrad(cat.reshape(1, t, D), dfm.reshape(1, t, D), False, False, D, "wout_grad")
    dcat = dcat.reshape(t, D)
    (dy_rec, dr2, dkp2, dv2, dgate, dgn_w, dgn_b, drk), (s_wout,) = _post_bwd(
        dcat, y_rec, rec_in[0], rec_in[2], rec_in[3], g_gate, post_params, "post_bwd", comm=(
            [dwout.reshape(NDEV, D // NDEV, D)], ["scatter"], [SLOT8]))
    dq_c, da_c, dg_c = _r2_bwd(dy_rec, q_c, s_all, a_c, "r2_bwd")
    rec_grads, got = _r1_bwd(rec_in, da_c, dg_c, dq_c, dy_rec, "r1_bwd", comm=(
        [dwg2.reshape(4, 2, FB, D), dwd2.reshape(NDEV, F // NDEV, D)], ["scatter"] * 2, [SLOT42, SLOT8]))
    s_wg2, s_wd2 = got
    prep_out = _prep_bwd(ps, prep_params, rec_grads, (dr2, dkp2, dv2, dgate), "prep_bwd")
    dpsl, dw0, da0, dkk, dka, dwup_pad, daup_pad, dgup = prep_out
    dc_conv, dlnw, dlnb, dcb = _conv_bwd1(dcat, c_conv, conv_ln_w, conv_ln_b, "conv_bwd1")
    dpc, dcdw = _conv_bwd2(dc_conv, pc, cdw_full, "conv_bwd2")
    dp, dmu = _shift_bwd(dpsl, ps, shift_mu, dpc, "shift_bwd")
    (dx1, dg_prem), _ = _nt_pre_bwd(dp.reshape(1, 1, t, DIN), win_t.reshape(1, 1, DIN, D), x1, mix_norm_pre, dx2,
                                    "mix_in_bwd")
    dwin_s = _wgrad(dp.reshape(1, t, DIN), hm.reshape(1, t, D), False, False, D // 2, "win_grad").reshape(
        NDEV, DIN // NDEV, D)
    (df1, dg_post1, dgu1), (s_win,) = _post_bwd_nt(dx1, f1, ffn1_norm_post, wd1, 0.5, gu1, "ffn1_out_bwd", comm=(
        [dwin_s], ["scatter"], [SLOT8]))
    dwd1 = _wgrad(h1, df1.reshape(1, t, D), True, False, D, "ffn1_wd_grad")
    dwg1, (s_wd1,) = _wgrad(dgu1.reshape(8, t, FB), xn1.reshape(1, t, D), True, False, D, "ffn1_wg_grad", comm=(
        [dwd1.reshape(NDEV, F // NDEV, D)], ["scatter"], [SLOT8]))
    (dx0, dg_pre1), (s_wg1,) = _nt_pre_bwd(dgu1, wg1, x0, ffn1_norm_pre, dx1, "ffn1_in_bwd", comm=(
        [dwg1.reshape(4, 2, FB, D)], ["scatter"], [SLOT42]))

    small_shapes = [(1, 1)] + [(1, D)] * 6 + [(1, DS)] + [(1, DR)] * 10 + [(HN, DR), (HN, DR), (LANES, DR), (CW, DR)]
    small_parts = [loss_part, dg_pre1, dg_post1, dg_prem, dg_postm, dg_pre2, dg_post2, dmu,
                   dw0, da0, dkk, dka, drk, dgn_w, dgn_b, dcb, dlnw, dlnb,
                   dwup_pad[:HN], daup_pad[HN:], dgup, dcdw]
    n_small = _round_up(sum(_round_up(math.prod(s), LANES) for s in small_shapes), 8 * LANES)
    (s_small,) = _exchange([_pack(small_parts, n_small)], ["gather"], [SLOT8], "gather_small")

    res = {}
    for nm, w, s, m, v, transposed in (
            ("ffn1_w_gu", ffn1_w_gu, s_wg1.reshape(NDEV, FB, D), m_ffn1_w_gu, v_ffn1_w_gu, True),
            ("ffn1_w_down", ffn1_w_down, s_wd1, m_ffn1_w_down, v_ffn1_w_down, False),
            ("w_in", w_in, s_win, m_w_in, v_w_in, True),
            ("w_out", w_out, s_wout, m_w_out, v_w_out, False),
            ("ffn2_w_gu", ffn2_w_gu, s_wg2.reshape(NDEV, FB, D), m_ffn2_w_gu, v_ffn2_w_gu, True),
            ("ffn2_w_down", ffn2_w_down, s_wd2, m_ffn2_w_down, v_ffn2_w_down, False)):
        view = (lambda a: jnp.swapaxes(a[0], 0, 1)) if transposed else (lambda a: a[0])
        outs = _adam_slots(view(w), s, view(m), view(v), "adam_" + nm)
        res[nm] = [(jnp.swapaxes(o, 0, 1) if transposed else o)[None] for o in outs]

    gsum = _unpack(_sum_slots(s_small, "sum_small"), small_shapes)
    loss = gsum[0].reshape(())
    rep_names = ["ffn1_norm_pre", "ffn1_norm_post", "mix_norm_pre", "mix_norm_post", "ffn2_norm_pre", "ffn2_norm_post",
                 "shift_mu", "w0", "a0", "k_k", "k_a", "r_k", "gn_w", "gn_b", "conv_b", "conv_ln_w", "conv_ln_b"]
    shard_names = ["w_up", "a_up", "g_up", "conv_dw"]
    env = dict(locals())
    small_g = {n: gsum[1 + i] for i, n in enumerate(rep_names)}
    small_g["r_k"] = small_g["r_k"].reshape(1, NH, HN)
    for i, n in enumerate(shard_names):
        small_g[n] = lax.dynamic_slice_in_dim(gsum[18 + i], HN * me, HN, axis=1)[None]
    names = rep_names + shard_names
    shapes = [env[n].shape for n in names]
    n_vec = _round_up(sum(_round_up(math.prod(s), LANES) for s in shapes), 8 * LANES)
    packed = [_pack([d[n] for n in names], n_vec) for d in
              ({n: env[n] for n in names}, small_g, {n: env["m_" + n] for n in names}, {n: env["v_" + n] for n in names})]
    d_vec, m_vec, v_vec = _adam_vec(*packed, "adam_small")
    for n, dl, nm_, nv_ in zip(names, _unpack(d_vec, shapes), _unpack(m_vec, shapes), _unpack(v_vec, shapes)):
        res[n] = [small_g[n], dl, nm_, nv_]

    order = ["ffn1_norm_pre", "ffn1_norm_post", "ffn1_w_gu", "ffn1_w_down", "mix_norm_pre", "mix_norm_post", "w_in",
             "shift_mu", "w_up", "w0", "a_up", "a0", "g_up", "k_k", "k_a", "r_k", "gn_w", "gn_b", "conv_dw", "conv_b",
             "conv_ln_w", "conv_ln_b", "w_out", "ffn2_norm_pre", "ffn2_norm_post", "ffn2_w_gu", "ffn2_w_down"]
    return (loss, dx0.reshape(1, t, D), *[res[n][0] for n in order], *[res[n][1] for n in order],
            *[res[n][2] for n in order], *[res[n][3] for n in order])
```

```python
import functools
import math

import jax
import jax.numpy as jnp
from jax import lax
from jax.experimental import pallas as pl
from jax.experimental.pallas import tpu as pltpu

f32 = jnp.float32
bf16 = jnp.bfloat16

D = 1024
F = 2816
FB = 704
DR = 512
DC = 512
NH = 8
HN = 64
DS = 1792
DIN = 2816
CW = 31
CHUNK = 64
R1_CHUNKS = 2
R2_CHUNKS = 4
NDEV = 8
RMS_EPS = 1e-6
GN_EPS = 64e-5
LN_EPS = 1e-5
DECAY_SCALE = math.exp(-0.5)
ADAM_LR, ADAM_B1, ADAM_B2, ADAM_EPS, ADAM_WD, ADAM_STEP = 0.001, 0.9, 0.999, 1e-08, 0.01, 10
LANES = 128
HALO = 32
CONV_ROWS = 32
TM_MXU = 1024
TM_SUB = 256
TM_VPU = 256

ANY = pl.BlockSpec(memory_space=pl.ANY)


def _full(shape):
    return pl.BlockSpec(shape, lambda *_: (0,) * len(shape))


def _sds(shape, dtype=f32):
    return jax.ShapeDtypeStruct(shape, dtype)


def _dot(a, b):
    return jnp.dot(a, b, preferred_element_type=f32)


def _dot_nt(a, b):
    return lax.dot_general(a, b, (((1,), (1,)), ((), ())), preferred_element_type=f32)


def _dot_tn(a, b):
    return lax.dot_general(a, b, (((0,), (0,)), ((), ())), preferred_element_type=f32)


def _terms(a, n):
    out, rem = [], a
    for i in range(n):
        t = rem.astype(bf16)
        out.append(t)
        if i + 1 < n:
            rem = rem - t.astype(f32)
    return out


def _mm_raw(a, b, kind, mode):
    nb = a.ndim - 2
    bd = tuple(range(nb))
    ca = nb if kind == "tn" else nb + 1
    cb = nb + 1 if kind == "nt" else nb
    dn = (((ca,), (cb,)), (bd, bd))
    pa, pb = mode[:2]
    ta, tb = _terms(a, pa), _terms(b, pb)
    acc = None
    for i in range(pa):
        for j in range(pb):
            if i + j < max(pa, pb):
                p = lax.dot_general(ta[i], tb[j], dn, preferred_element_type=f32)
                acc = p if acc is None else acc + p
    return acc


@functools.partial(jax.custom_vjp, nondiff_argnums=(2, 3))
def mm(a, b, kind, mode):
    return _mm_raw(a, b, kind, mode)


def _mm_fwd(a, b, kind, mode):
    return _mm_raw(a, b, kind, mode), (a, b)


def _ct_terms(keep):
    return EXACT_TERMS if keep == 1 else keep


def _mm_bwd(kind, mode, res, ct):
    a, b = res
    pa, pb = mode[:2]
    if len(mode) == 3:
        pa = pb = ca = cb = mode[2]
    elif mode == (1, 1):
        ca = cb = 1
    else:
        ca, cb = _ct_terms(pb), _ct_terms(pa)
    if kind == "nn":
        da = mm(ct, b, "nt", (ca, pb))
        db = mm(a, ct, "tn", (pa, cb))
    elif kind == "nt":
        da = mm(ct, b, "nn", (ca, pb))
        db = mm(ct, a, "tn", (cb, pa))
    else:
        da = mm(b, ct, "nt", (pb, ca))
        db = mm(a, ct, "nn", (pa, cb))
    return da, db


mm.defvjp(_mm_fwd, _mm_bwd)

X3 = (2, 2)
EXACT_TERMS = 2
EA = (1, EXACT_TERMS)
EB = (EXACT_TERMS, 1)
X1 = (1, 1)
RGM = (1, 1, 2)
RIM = X1
RWM = X1
RAM = X1
R2M = X1
LORA = X1


def _rms_fwd(x, g):
    inv = lax.rsqrt(jnp.mean(x * x, axis=-1, keepdims=True) + RMS_EPS)
    return x * inv * g


def _rms_bwd(x, g, dy):
    inv = lax.rsqrt(jnp.mean(x * x, axis=-1, keepdims=True) + RMS_EPS)
    xh = x * inv
    dxh = dy * g
    dg = jnp.sum(dy * xh, axis=0, keepdims=True)
    dx = inv * (dxh - xh * jnp.mean(dxh * xh, axis=-1, keepdims=True))
    return dx, dg


def _tile(t, want):
    return min(t, want)


NPEER = NDEV - 1


def _comm_out_shape(arrays, modes, slots):
    out = []
    for a, mode, (lead, _) in zip(arrays, modes, slots):
        shp = tuple(lead) + tuple(a.shape) if mode == "gather" else tuple(a.shape)
        out.append(pltpu.HBM(shp, a.dtype))
    return out


def _comm_sems(n):
    return [pltpu.SemaphoreType.DMA((n * NPEER,)), pltpu.SemaphoreType.DMA((n * NPEER,)), pltpu.SemaphoreType.DMA((n,))]


RELAYED = (3, 5, 7)


def _comm_copies(ins, outs, sems, modes, slots, want):
    n = len(ins)
    send_sems, recv_sems, loc_sems = sems
    x, y, c = lax.axis_index("x"), lax.axis_index("y"), lax.axis_index("c")
    me = 4 * x + 2 * y + c

    def peer_of(k):
        px = 1 - x if k & 4 else x
        py = 1 - y if k & 2 else y
        pc = 1 - c if k & 1 else c
        return (px, py, pc), 4 * px + 2 * py + pc

    def slot(i, dev):
        return outs[i].at[slots[i][1](dev)]

    if want == "local":
        return [pltpu.make_async_copy(ins[i] if modes[i] == "gather" else ins[i].at[slots[i][1](me)], slot(i, me),
                                      loc_sems.at[i]) for i in range(n)]
    sibling = peer_of(1)[0]
    out = []
    for k in range(1, NDEV):
        peer, pid = peer_of(k)
        for i in range(n):
            sem = dict(send_sem=send_sems.at[i * NPEER + k - 1], recv_sem=recv_sems.at[i * NPEER + k - 1],
                       device_id_type=pl.DeviceIdType.MESH)
            relayed = modes[i] == "gather" and k in RELAYED
            if want == "recv":
                src = ins[i] if modes[i] == "gather" else ins[i].at[slots[i][1](pid)]
                out.append(pltpu.make_async_remote_copy(src_ref=src, dst_ref=slot(i, pid), device_id=peer, **sem))
            elif want == "first" and not relayed:
                src = ins[i] if modes[i] == "gather" else ins[i].at[slots[i][1](pid)]
                out.append(pltpu.make_async_remote_copy(src_ref=src, dst_ref=slot(i, me), device_id=peer, **sem))
            elif want == "relay" and relayed:
                origin = peer_of(k - 1)[1]
                out.append(((k - 2) * n + i, pltpu.make_async_remote_copy(
                    src_ref=slot(i, origin), dst_ref=slot(i, origin), device_id=sibling, **sem)))
    return out


def _comm_start(ins, outs, sems, modes, slots):
    for cp in _comm_copies(ins, outs, sems, modes, slots, "local") + _comm_copies(ins, outs, sems, modes, slots, "first"):
        cp.start()


def _comm_wait(ins, outs, sems, modes, slots):
    recvs = _comm_copies(ins, outs, sems, modes, slots, "recv")
    relays = _comm_copies(ins, outs, sems, modes, slots, "relay")
    for idx, cp in relays:
        recvs[idx].wait_recv()
        cp.start()
    passed_on = {idx for idx, _ in relays}
    for idx, cp in enumerate(recvs):
        if idx not in passed_on:
            cp.wait_recv()
    for cp in _comm_copies(ins, outs, sems, modes, slots, "first") + [cp for _, cp in relays]:
        cp.wait_send()
    for cp in _comm_copies(ins, outs, sems, modes, slots, "local"):
        cp.wait()


def _exchange(arrays, modes, slots, name):
    n = len(arrays)

    def body(*refs):
        ins, outs, sems = refs[:n], refs[n:2 * n], refs[2 * n:]
        _comm_start(ins, outs, sems, modes, slots)
        _comm_wait(ins, outs, sems, modes, slots)

    return pl.pallas_call(
        body, name=name, out_shape=_comm_out_shape(arrays, modes, slots),
        in_specs=[ANY] * n, out_specs=[ANY] * n, scratch_shapes=_comm_sems(n),
    )(*arrays)


def _pcall(body, *, name, grid, in_specs, out_specs, out_shape, args, scratch_shapes=(), comm=None):
    if comm is None:
        return pl.pallas_call(body, name=name, grid=grid, in_specs=in_specs, out_specs=out_specs, out_shape=out_shape,
                              scratch_shapes=list(scratch_shapes))(*args), None
    arrays, modes, slots = comm
    n_in, n_out, n_scr, nc = len(args), len(out_shape), len(scratch_shapes), len(arrays)

    def hosted(*refs):
        a_in, c_in = refs[:n_in], refs[n_in:n_in + nc]
        o = n_in + nc
        a_out, c_out = refs[o:o + n_out], refs[o + n_out:o + n_out + nc]
        o += n_out + nc
        a_scr, sems = refs[o:o + n_scr], refs[o + n_scr:]
        first = pl.program_id(0) == 0
        last = pl.program_id(0) == grid[0] - 1
        for ax in range(1, len(grid)):
            first = jnp.logical_and(first, pl.program_id(ax) == 0)
            last = jnp.logical_and(last, pl.program_id(ax) == grid[ax] - 1)

        @pl.when(first)
        def _():
            _comm_start(c_in, c_out, sems, modes, slots)

        body(*a_in, *a_out, *a_scr)

        @pl.when(last)
        def _():
            _comm_wait(c_in, c_out, sems, modes, slots)

    res = pl.pallas_call(
        hosted, name=name, grid=grid, in_specs=list(in_specs) + [ANY] * nc, out_specs=list(out_specs) + [ANY] * nc,
        out_shape=list(out_shape) + _comm_out_shape(arrays, modes, slots),
        scratch_shapes=list(scratch_shapes) + _comm_sems(nc),
    )(*args, *arrays)
    return res[:n_out], res[n_out:]


SLOT8 = ((NDEV,), lambda d: (d,))
SLOT42 = ((4, 2), lambda d: (lax.rem(d, 4), lax.div(d, 4)))


def _ffn_in_fwd(x, g, wg4, name, comm=None):
    t = x.shape[0]
    tm = _tile(t, TM_MXU)
    sub = _tile(tm, TM_SUB)

    def body(x_ref, g_ref, w_ref, xn_ref, gu_ref, h_ref):
        for s in range(tm // sub):
            rows = pl.ds(s * sub, sub)
            xn = _rms_fwd(x_ref[rows, :], g_ref[...]).astype(bf16)
            xn_ref[rows, :] = xn
            gate = _dot_nt(xn, w_ref[0, 0])
            up = _dot_nt(xn, w_ref[0, 1])
            gu_ref[0, 0, rows, :] = gate.astype(bf16)
            gu_ref[0, 1, rows, :] = up.astype(bf16)
            h_ref[0, rows, :] = (gate * jax.nn.sigmoid(gate) * up).astype(bf16)

    return _pcall(
        body, name=name, grid=(t // tm, 4),
        in_specs=[pl.BlockSpec((tm, D), lambda i, e: (i, 0)), _full((1, D)),
                  pl.BlockSpec((1, 2, FB, D), lambda i, e: (e, 0, 0, 0))],
        out_specs=[pl.BlockSpec((tm, D), lambda i, e: (i, 0)),
                   pl.BlockSpec((1, 2, tm, FB), lambda i, e: (e, 0, i, 0)),
                   pl.BlockSpec((1, tm, FB), lambda i, e: (e, i, 0))],
        out_shape=[_sds((t, D), bf16), _sds((4, 2, t, FB), bf16), _sds((4, t, FB), bf16)],
        args=(x, g, wg4), comm=comm)


def _mm_norm_res_fwd(hb, wb, xres, g, scale, name, comm=None):
    e_n, t, k = hb.shape
    tm = _tile(t, TM_MXU)

    def body(h_ref, w_ref, x_ref, g_ref, f_ref, o_ref):
        f = _dot(h_ref[0], w_ref[0])
        for e in range(1, e_n):
            f = f + _dot(h_ref[e], w_ref[e])
        f_ref[...] = f
        o_ref[...] = x_ref[...] + scale * _rms_fwd(f, g_ref[...])

    outs, got = _pcall(
        body, name=name, grid=(t // tm,),
        in_specs=[pl.BlockSpec((e_n, tm, k), lambda i: (0, i, 0)), _full((e_n, k, D)),
                  pl.BlockSpec((tm, D), lambda i: (i, 0)), _full((1, D))],
        out_specs=[pl.BlockSpec((tm, D), lambda i: (i, 0)), pl.BlockSpec((tm, D), lambda i: (i, 0))],
        out_shape=[_sds((t, D)), _sds((t, D))], args=(hb, wb, xres, g), comm=comm)
    return outs if comm is None else (outs, got)


def _post_bwd_nt(dxn, f, g, wb, scale, gu, name, comm=None):
    e_n, k, _ = wb.shape
    t = f.shape[0]
    tm = _tile(t, TM_MXU)
    sub = _tile(tm, TM_SUB)
    swiglu = gu is not None

    def body(*refs):
        if swiglu:
            dx_ref, f_ref, g_ref, w_ref, gu_ref, df_ref, dg_ref, dh_ref, df_s = refs
        else:
            dx_ref, f_ref, g_ref, w_ref, df_ref, dg_ref, dh_ref, df_s = refs
        i, e = pl.program_id(0), pl.program_id(1)

        @pl.when(e == 0)
        def _():
            df, dg = _rms_bwd(f_ref[...], g_ref[...], scale * dx_ref[...])
            df_s[...] = df.astype(bf16)
            df_ref[...] = df_s[...]

            @pl.when(i == 0)
            def _():
                dg_ref[...] = dg

            @pl.when(i != 0)
            def _():
                dg_ref[...] += dg

        for s in range(tm // sub):
            rows = pl.ds(s * sub, sub)
            dh = _dot_nt(df_s[rows, :], w_ref[0])
            if swiglu:
                gate = gu_ref[0, 0, rows, :].astype(f32)
                up = gu_ref[0, 1, rows, :].astype(f32)
                sg = jax.nn.sigmoid(gate)
                dh_ref[0, 0, rows, :] = (dh * up * (sg * (1.0 + gate * (1.0 - sg)))).astype(bf16)
                dh_ref[0, 1, rows, :] = (dh * gate * sg).astype(bf16)
            else:
                dh_ref[0, rows, :] = dh

    in_specs = [pl.BlockSpec((tm, D), lambda i, e: (i, 0)), pl.BlockSpec((tm, D), lambda i, e: (i, 0)), _full((1, D)),
                pl.BlockSpec((1, k, D), lambda i, e: (e, 0, 0))]
    args = [dxn, f, g, wb]
    if swiglu:
        in_specs.append(pl.BlockSpec((1, 2, tm, k), lambda i, e: (e, 0, i, 0)))
        args.append(gu)
        dh_spec = pl.BlockSpec((1, 2, tm, k), lambda i, e: (e, 0, i, 0))
        dh_shape = _sds((e_n, 2, t, k), bf16)
    else:
        dh_spec = pl.BlockSpec((1, tm, k), lambda i, e: (e, i, 0))
        dh_shape = _sds((e_n, t, k), f32)
    return _pcall(
        body, name=name, grid=(t // tm, e_n), in_specs=in_specs,
        out_specs=[pl.BlockSpec((tm, D), lambda i, e: (i, 0)), _full((1, D)), dh_spec],
        out_shape=[_sds((t, D), bf16), _sds((1, D)), dh_shape],
        scratch_shapes=[pltpu.VMEM((tm, D), bf16)], args=args, comm=comm)


def _nt_pre_bwd(dy, wb, x, g, dres, name, comm=None):
    e_n, q_n, t, k = dy.shape
    tm = _tile(t, TM_MXU)

    def body(dy_ref, w_ref, x_ref, g_ref, r_ref, dx_ref, dg_ref, acc):
        i, e = pl.program_id(0), pl.program_id(1)
        p = _dot(dy_ref[0, 0], w_ref[0, 0])
        for q in range(1, q_n):
            p = p + _dot(dy_ref[0, q], w_ref[0, q])

        @pl.when(e == 0)
        def _():
            acc[...] = p

        @pl.when(e != 0)
        def _():
            acc[...] += p

        @pl.when(e == e_n - 1)
        def _():
            dx, dg = _rms_bwd(x_ref[...], g_ref[...], acc[...])
            dx_ref[...] = r_ref[...] + dx

            @pl.when(i == 0)
            def _():
                dg_ref[...] = dg

            @pl.when(i != 0)
            def _():
                dg_ref[...] += dg

    return _pcall(
        body, name=name, grid=(t // tm, e_n),
        in_specs=[pl.BlockSpec((1, q_n, tm, k), lambda i, e: (e, 0, i, 0)),
                  pl.BlockSpec((1, q_n, k, D), lambda i, e: (e, 0, 0, 0)),
                  pl.BlockSpec((tm, D), lambda i, e: (i, 0)), _full((1, D)),
                  pl.BlockSpec((tm, D), lambda i, e: (i, 0))],
        out_specs=[pl.BlockSpec((tm, D), lambda i, e: (i, 0)), _full((1, D))],
        out_shape=[_sds((t, D)), _sds((1, D))],
        scratch_shapes=[pltpu.VMEM((tm, D), f32)], args=(dy, wb, x, g, dres), comm=comm)


def _wgrad(a, b, a_batched, b_batched, tn, name, comm=None):
    t, k = a.shape[1], a.shape[2]
    nn = b.shape[2]
    nb = max(a.shape[0], b.shape[0])
    tt = _tile(t, TM_MXU)
    nt = t // tt

    def body(a_ref, b_ref, o_ref, acc):
        @pl.when(pl.program_id(2) == 0)
        def _():
            acc[...] = _dot_tn(a_ref[0], b_ref[0])

        @pl.when(pl.program_id(2) != 0)
        def _():
            acc[...] += _dot_tn(a_ref[0], b_ref[0])

        @pl.when(pl.program_id(2) == nt - 1)
        def _():
            o_ref[0] = acc[...].astype(bf16)

    (out,), got = _pcall(
        body, name=name, grid=(nb, nn // tn, nt),
        in_specs=[pl.BlockSpec((1, tt, k), (lambda n, j, s: (n, s, 0)) if a_batched else (lambda n, j, s: (0, s, 0))),
                  pl.BlockSpec((1, tt, tn), (lambda n, j, s: (n, s, j)) if b_batched else (lambda n, j, s: (0, s, j)))],
        out_specs=[pl.BlockSpec((1, k, tn), lambda n, j, s: (n, 0, j))],
        out_shape=[_sds((nb, k, nn), bf16)],
        scratch_shapes=[pltpu.VMEM((k, tn), f32)], args=(a, b), comm=comm)
    return out if comm is None else (out, got)


def _norm_mm_fwd(x, g, w, name):
    t = x.shape[0]
    tm = _tile(t, TM_VPU)

    def body(x_ref, g_ref, w_ref, xn_ref, ps_ref, pc_ref):
        xn = _rms_fwd(x_ref[...], g_ref[...]).astype(bf16)
        xn_ref[...] = xn
        ps_ref[...] = _dot_nt(xn, w_ref[:DS, :])
        pc_ref[...] = _dot_nt(xn, w_ref[DS:, :])

    return pl.pallas_call(
        body, name=name, grid=(t // tm,),
        in_specs=[pl.BlockSpec((tm, D), lambda i: (i, 0)), _full((1, D)), _full((DIN, D))],
        out_specs=[pl.BlockSpec((tm, D), lambda i: (i, 0)), pl.BlockSpec((tm, DS), lambda i: (i, 0)),
                   pl.BlockSpec((tm, 2 * DC), lambda i: (i, 0))],
        out_shape=[_sds((t, D), bf16), _sds((t, DS)), _sds((t, 2 * DC))],
    )(x, g, w)


def _loss_fwd_bwd(y, tgt, name):
    t = y.shape[0]
    tm = _tile(t, TM_MXU)

    def body(y_ref, t_ref, dy_ref, l_ref):
        e = y_ref[...] - t_ref[...]
        dy_ref[...] = e * (1.0 / D)
        s = 0.5 * jnp.sum(jnp.mean(e * e, axis=-1, keepdims=True), axis=0, keepdims=True)

        @pl.when(pl.program_id(0) == 0)
        def _():
            l_ref[...] = s

        @pl.when(pl.program_id(0) != 0)
        def _():
            l_ref[...] += s

    return pl.pallas_call(
        body, name=name, grid=(t // tm,),
        in_specs=[pl.BlockSpec((tm, D), lambda i: (i, 0)), pl.BlockSpec((tm, D), lambda i: (i, 0))],
        out_specs=[pl.BlockSpec((tm, D), lambda i: (i, 0)), _full((1, 1))],
        out_shape=[_sds((t, D)), _sds((1, 1))],
    )(y, tgt)


def _segsum_b(x, seg, seg_t):
    return mm(mm(x, seg, "nn", EB), seg_t, "nn", EB)


def _prep_fn(psl, w0, a0, k_k, k_a, wup, aup, gup, seg, seg_t):
    r, k, v = psl[:, :DR], psl[:, DR:2 * DR], psl[:, 2 * DR:3 * DR]
    xwa, xg = psl[:, 3 * DR:3 * DR + LANES], psl[:, 3 * DR + LANES:]
    d = w0 + mm(jnp.tanh(xwa), wup, "nn", LORA)
    lw = -DECAY_SCALE * jax.nn.sigmoid(d)
    a = jax.nn.sigmoid(a0 + mm(xwa, aup, "nn", LORA))
    g = mm(jax.nn.sigmoid(xg), gup, "nn", LORA)
    kkr = k * k_k
    kk = kkr * lax.rsqrt(jnp.maximum(_segsum_b(kkr * kkr, seg, seg_t), 1e-12))
    kp = k * (1.0 + (a - 1.0) * k_a)
    return r, lw, kp, v, -kk, kk * a, g


def _shifted(ps, halo_row, first):
    prev = jnp.where(first, 0.0, halo_row)
    sh = pltpu.roll(ps, 1, 0)
    row = lax.broadcasted_iota(jnp.int32, ps.shape, 0)
    return jnp.where(row == 0, prev, sh)


def _heads_split(ref, val):
    for h in range(NH):
        ref[h] = val[:, h * HN:(h + 1) * HN]


def _heads_merge(ref):
    return jnp.concatenate([ref[h] for h in range(NH)], axis=-1)


_PREP_PARAM_SHAPES = [(1, DS), (1, DR), (1, DR), (1, DR), (1, DR), (LANES, DR), (LANES, DR), (LANES, DR),
                      (DR, LANES), (LANES, DR)]


def _prep_fwd(ps, params, name):
    t = ps.shape[0]
    tm = _tile(t, TM_VPU)
    hb = tm // 8

    def body(ps_ref, halo_ref, mu_ref, *rest):
        prm = [r[...] for r in rest[:9]]
        outs = rest[9:]
        x = ps_ref[...]
        sh = _shifted(x, halo_ref[pl.ds(7, 1), :], pl.program_id(0) == 0)
        psl = x + (sh - x) * mu_ref[...]
        vals = _prep_fn(psl, *prm)
        for ref, val in zip(outs[:6], vals[:6]):
            _heads_split(ref, val)
        outs[6][...] = vals[6]

    hm = pl.BlockSpec((NH, tm, HN), lambda i: (0, i, 0))
    return pl.pallas_call(
        body, name=name, grid=(t // tm,),
        in_specs=[pl.BlockSpec((tm, DS), lambda i: (i, 0)),
                  pl.BlockSpec((8, DS), lambda i: (jnp.maximum(i * hb - 1, 0), 0))]
        + [_full(s) for s in _PREP_PARAM_SHAPES],
        out_specs=[hm] * 6 + [pl.BlockSpec((tm, DR), lambda i: (i, 0))],
        out_shape=[_sds((NH, t, HN))] * 6 + [_sds((t, DR))],
    )(ps, ps, *params)


def _prep_bwd(ps, params, cts_hm, cts_tm, name):
    t = ps.shape[0]
    tm = _tile(t, TM_VPU)
    hb = tm // 8

    def body(ps_ref, halo_ref, mu_ref, *rest):
        prm = [r[...] for r in rest[:9]]
        chm = rest[9:15]
        ctm = rest[15:19]
        dpsl_ref = rest[19]
        gouts = rest[20:27]
        x = ps_ref[...]
        sh = _shifted(x, halo_ref[pl.ds(7, 1), :], pl.program_id(0) == 0)
        psl = x + (sh - x) * mu_ref[...]
        seg, seg_t = prm[7], prm[8]
        _, vjp = jax.vjp(lambda p, *w: _prep_fn(p, *w, seg, seg_t), psl, *prm[:7])
        c = [_heads_merge(r) for r in chm]
        cts = (c[0] + ctm[0][...], c[1], c[2] + ctm[1][...], c[3] + ctm[2][...], c[4], c[5], ctm[3][...])
        grads = vjp(cts)
        dpsl_ref[...] = grads[0]
        for ref, gval in zip(gouts, grads[1:]):
            @pl.when(pl.program_id(0) == 0)
            def _(ref=ref, gval=gval):
                ref[...] = gval

            @pl.when(pl.program_id(0) != 0)
            def _(ref=ref, gval=gval):
                ref[...] += gval

    hm = pl.BlockSpec((NH, tm, HN), lambda i: (0, i, 0))
    tk = pl.BlockSpec((tm, DR), lambda i: (i, 0))
    gshapes = _PREP_PARAM_SHAPES[1:8]
    return pl.pallas_call(
        body, name=name, grid=(t // tm,),
        in_specs=[pl.BlockSpec((tm, DS), lambda i: (i, 0)),
                  pl.BlockSpec((8, DS), lambda i: (jnp.maximum(i * hb - 1, 0), 0))]
        + [_full(s) for s in _PREP_PARAM_SHAPES] + [hm] * 6 + [tk] * 4,
        out_specs=[pl.BlockSpec((tm, DS), lambda i: (i, 0))] + [_full(s) for s in gshapes],
        out_shape=[_sds((t, DS))] + [_sds(s) for s in gshapes],
    )(ps, ps, *params, *cts_hm, *cts_tm)


def _shift_bwd(dpsl, ps, mu, dpc, name):
    t = ps.shape[0]
    tm = _tile(t, TM_VPU)
    hb = tm // 8
    last_blk = t // 8 - 1

    def body(d_ref, dn_ref, ps_ref, halo_ref, mu_ref, dpc_ref, dp_ref, dmu_ref):
        i = pl.program_id(0)
        mu_v = mu_ref[...]
        d = d_ref[...]
        nxt = jnp.where(i == pl.num_programs(0) - 1, 0.0, dn_ref[pl.ds(0, 1), :])
        up = pltpu.roll(d, tm - 1, 0)
        row = lax.broadcasted_iota(jnp.int32, d.shape, 0)
        up = jnp.where(row == tm - 1, nxt, up)
        dp_ref[:, :DS] = (d * (1.0 - mu_v) + up * mu_v).astype(bf16)
        dp_ref[:, DS:] = dpc_ref[...].astype(bf16)
        x = ps_ref[...]
        sh = _shifted(x, halo_ref[pl.ds(7, 1), :], i == 0)
        dmu = jnp.sum(d * (sh - x), axis=0, keepdims=True)

        @pl.when(i == 0)
        def _():
            dmu_ref[...] = dmu

        @pl.when(i != 0)
        def _():
            dmu_ref[...] += dmu

    return pl.pallas_call(
        body, name=name, grid=(t // tm,),
        in_specs=[pl.BlockSpec((tm, DS), lambda i: (i, 0)),
                  pl.BlockSpec((8, DS), lambda i: (jnp.minimum((i + 1) * hb, last_blk), 0)),
                  pl.BlockSpec((tm, DS), lambda i: (i, 0)),
                  pl.BlockSpec((8, DS), lambda i: (jnp.maximum(i * hb - 1, 0), 0)),
                  _full((1, DS)), pl.BlockSpec((tm, 2 * DC), lambda i: (i, 0))],
        out_specs=[pl.BlockSpec((tm, DIN), lambda i: (i, 0)), _full((1, DS))],
        out_shape=[_sds((t, DIN), bf16), _sds((1, DS))],
    )(dpsl, dpsl, ps, ps, mu, dpc)


def _post_fn(y, r, kp, v, g, gn_w, gn_b, r_k, seg, seg_t):
    mu = _segsum_b(y, seg, seg_t) * (1.0 / HN)
    yc = y - mu
    var = _segsum_b(yc * yc, seg, seg_t) * (1.0 / HN)
    yo = yc * lax.rsqrt(var + GN_EPS) * gn_w + gn_b
    bonus = _segsum_b(r * kp * r_k, seg, seg_t) * v
    return (yo + bonus) * g


_POST_PARAM_SHAPES = [(1, DR), (1, DR), (1, DR), (DR, LANES), (LANES, DR)]


def _post_fwd(y, r, kp, v, g, out_b, params, name):
    t = g.shape[0]
    tm = _tile(t, TM_VPU)

    def body(y_ref, r_ref, k_ref, v_ref, g_ref, ob_ref, *rest):
        prm = [p[...] for p in rest[:5]]
        cat_ref = rest[5]
        oa = _post_fn(_heads_merge(y_ref), _heads_merge(r_ref), _heads_merge(k_ref), _heads_merge(v_ref),
                      g_ref[...], *prm)
        cat_ref[:, :DR] = oa.astype(bf16)
        cat_ref[:, DR:] = ob_ref[...]

    hm = pl.BlockSpec((NH, tm, HN), lambda i: (0, i, 0))
    tk = pl.BlockSpec((tm, DR), lambda i: (i, 0))
    return pl.pallas_call(
        body, name=name, grid=(t // tm,),
        in_specs=[hm] * 4 + [tk, tk] + [_full(s) for s in _POST_PARAM_SHAPES],
        out_specs=pl.BlockSpec((tm, 2 * DR), lambda i: (i, 0)),
        out_shape=_sds((t, 2 * DR), bf16),
    )(y, r, kp, v, g, out_b, *params)


def _post_bwd(dcat, y, r, kp, v, g, params, name, comm=None):
    t = g.shape[0]
    tm = _tile(t, TM_VPU)

    def body(dc_ref, y_ref, r_ref, k_ref, v_ref, g_ref, *rest):
        prm = [p[...] for p in rest[:5]]
        dy_ref, dr_ref, dk_ref, dv_ref, dg_ref = rest[5:10]
        gouts = rest[10:13]
        seg, seg_t = prm[3], prm[4]
        _, vjp = jax.vjp(lambda *a: _post_fn(*a, seg, seg_t),
                         _heads_merge(y_ref), _heads_merge(r_ref), _heads_merge(k_ref), _heads_merge(v_ref),
                         g_ref[...], *prm[:3])
        grads = vjp(dc_ref[...])
        _heads_split(dy_ref, grads[0])
        dr_ref[...] = grads[1]
        dk_ref[...] = grads[2]
        dv_ref[...] = grads[3]
        dg_ref[...] = grads[4]
        for ref, gval in zip(gouts, grads[5:]):
            @pl.when(pl.program_id(0) == 0)
            def _(ref=ref, gval=gval):
                ref[...] = gval

            @pl.when(pl.program_id(0) != 0)
            def _(ref=ref, gval=gval):
                ref[...] += gval

    hm = pl.BlockSpec((NH, tm, HN), lambda i: (0, i, 0))
    tk = pl.BlockSpec((tm, DR), lambda i: (i, 0))
    return _pcall(
        body, name=name, grid=(t // tm,),
        in_specs=[tk] + [hm] * 4 + [tk] + [_full(s) for s in _POST_PARAM_SHAPES],
        out_specs=[hm] + [tk] * 4 + [_full((1, DR))] * 3,
        out_shape=[_sds((NH, t, HN))] + [_sds((t, DR))] * 4 + [_sds((1, DR))] * 3,
        args=(dcat, y, r, kp, v, g, *params), comm=comm)


@jax.custom_vjp
def _inv_unit_lower(lb):
    c = lb.shape[-1]
    row = lax.broadcasted_iota(jnp.int32, (c, c), 0)
    col = lax.broadcasted_iota(jnp.int32, (c, c), 1)
    p = (row == col).astype(f32)[None] + lb
    m = lb
    for _ in range(c.bit_length() - 2):
        m = mm(m, m, "nn", RIM)
        p = p + mm(m, p, "nn", RIM)
    return p


def _inv_unit_lower_fwd(lb):
    p = _inv_unit_lower(lb)
    return p, p


def _inv_unit_lower_bwd(p, ct):
    return (mm(mm(p, ct, "tn", RIM), p, "nt", RIM),)


_inv_unit_lower.defvjp(_inv_unit_lower_fwd, _inv_unit_lower_bwd)


def _r1_fn(r, lw, k, v, z, b):
    c = r.shape[1]
    row = lax.broadcasted_iota(jnp.int32, (c, c), 0)
    col = lax.broadcasted_iota(jnp.int32, (c, c), 1)
    incl = (row >= col)[None]
    strict = (row > col)[None]
    eye = (row == col).astype(f32)[None]
    tri = jnp.broadcast_to((row >= col).astype(f32)[None], (r.shape[0], c, c))
    cum = mm(tri, lw, "nn", EA)
    tot = jnp.sum(lw, axis=1, keepdims=True)
    zt = z * jnp.exp(cum - lw)
    rt = r * jnp.exp(cum)
    g_inv = jnp.exp(-cum)
    g_rem = jnp.exp(tot - cum)
    bt, kt = b * g_inv, k * g_inv
    bh, kh = b * g_rem, k * g_rem
    lb = jnp.where(strict, mm(zt, bt, "nt", RGM), 0.0)
    lk = jnp.where(strict, mm(zt, kt, "nt", RGM), 0.0)
    mb = jnp.where(incl, mm(rt, bt, "nt", RGM), 0.0)
    mk = jnp.where(incl, mm(rt, kt, "nt", RGM), 0.0)
    p = _inv_unit_lower(lb)
    w1 = mm(p, zt, "nn", RWM)
    w2 = mm(p, mm(lk, v, "nn", RWM), "nn", RWM)
    a = mm(w1, bh, "tn", RAM) + eye * jnp.exp(tot)
    g = mm(w2, bh, "tn", RAM) + mm(v, kh, "tn", RAM)
    q = rt + mm(mb, w1, "nn", RWM)
    yl = mm(mb, w2, "nn", RWM) + mm(mk, v, "nn", RWM)
    return a, g, q, yl


def _chunks_in(ref, n):
    return jnp.concatenate([ref[:, s * CHUNK:(s + 1) * CHUNK, :] for s in range(n)], axis=0)


def _chunks_out(ref, val, n):
    for s in range(n):
        ref[:, s * CHUNK:(s + 1) * CHUNK, :] = val[s * NH:(s + 1) * NH]


def _r1_fwd(ins, name, comm=None):
    t = ins[0].shape[1]
    n = R1_CHUNKS
    nc = t // CHUNK

    def body(r, lw, k, v, z, b, a_ref, g_ref, q_ref, yl_ref):
        a, g, q, yl = _r1_fn(*[_chunks_in(x, n) for x in (r, lw, k, v, z, b)])
        for s in range(n):
            a_ref[s] = a[s * NH:(s + 1) * NH]
            g_ref[s] = g[s * NH:(s + 1) * NH]
        _chunks_out(q_ref, q, n)
        _chunks_out(yl_ref, yl, n)

    ck = pl.BlockSpec((NH, n * CHUNK, HN), lambda c: (0, c, 0))
    st = pl.BlockSpec((n, NH, HN, HN), lambda c: (c, 0, 0, 0))
    return _pcall(
        body, name=name, grid=(nc // n,), in_specs=[ck] * 6, out_specs=[st, st, ck, ck],
        out_shape=[_sds((nc, NH, HN, HN))] * 2 + [_sds((NH, t, HN))] * 2, args=tuple(ins), comm=comm)


def _r1_bwd(ins, da, dg, dq, dyl, name, comm=None):
    t = ins[0].shape[1]
    n = R1_CHUNKS
    nc = t // CHUNK

    def body(r, lw, k, v, z, b, da_ref, dg_ref, dq_ref, dyl_ref, *outs):
        _, vjp = jax.vjp(_r1_fn, *[_chunks_in(x, n) for x in (r, lw, k, v, z, b)])
        cts = (jnp.concatenate([da_ref[s] for s in range(n)], axis=0),
               jnp.concatenate([dg_ref[s] for s in range(n)], axis=0), _chunks_in(dq_ref, n), _chunks_in(dyl_ref, n))
        for ref, gval in zip(outs, vjp(cts)):
            _chunks_out(ref, gval, n)

    ck = pl.BlockSpec((NH, n * CHUNK, HN), lambda c: (0, c, 0))
    st = pl.BlockSpec((n, NH, HN, HN), lambda c: (c, 0, 0, 0))
    return _pcall(
        body, name=name, grid=(nc // n,), in_specs=[ck] * 6 + [st, st, ck, ck], out_specs=[ck] * 6,
        out_shape=[_sds((NH, t, HN))] * 6, args=(*ins, da, dg, dq, dyl), comm=comm)


def _r2_fwd(a, g, q, yl, name):
    nc = a.shape[0]
    t = q.shape[1]

    def body(a_ref, g_ref, q_ref, yl_ref, y_ref, s_ref, s):
        @pl.when(pl.program_id(0) == 0)
        def _():
            s[...] = jnp.zeros_like(s)
        s0 = s[...]
        for j in range(n):
            rows = pl.ds(j * CHUNK, CHUNK)
            s_ref[j] = s0
            y_ref[:, rows, :] = mm(q_ref[:, rows, :], s0, "nt", R2M) + yl_ref[:, rows, :]
            s0 = mm(s0, a_ref[j], "nn", R2M) + g_ref[j]
        s[...] = s0

    n = math.gcd(nc, R2_CHUNKS)
    ck = pl.BlockSpec((NH, n * CHUNK, HN), lambda c: (0, c, 0))
    st = pl.BlockSpec((n, NH, HN, HN), lambda c: (c, 0, 0, 0))
    return pl.pallas_call(
        body, name=name, grid=(nc // n,), in_specs=[st, st, ck, ck], out_specs=[ck, st],
        out_shape=[_sds((NH, t, HN)), _sds((nc, NH, HN, HN))],
        scratch_shapes=[pltpu.VMEM((NH, HN, HN), f32)],
    )(a, g, q, yl)


def _r2_bwd(dy, q, s_all, a, name):
    nc = a.shape[0]
    t = q.shape[1]

    def body(dy_ref, q_ref, s_ref, a_ref, dq_ref, da_ref, dg_ref, ds):
        @pl.when(pl.program_id(0) == 0)
        def _():
            ds[...] = jnp.zeros_like(ds)
        dsn = ds[...]
        for j in reversed(range(n)):
            rows = pl.ds(j * CHUNK, CHUNK)
            s0 = s_ref[j]
            dyv = dy_ref[:, rows, :]
            dg_ref[j] = dsn
            da_ref[j] = mm(s0, dsn, "tn", R2M)
            dq_ref[:, rows, :] = mm(dyv, s0, "nn", R2M)
            dsn = mm(dsn, a_ref[j], "nt", R2M) + mm(dyv, q_ref[:, rows, :], "tn", R2M)
        ds[...] = dsn

    n = math.gcd(nc, R2_CHUNKS)
    nb = nc // n
    ck = pl.BlockSpec((NH, n * CHUNK, HN), lambda c: (0, nb - 1 - c, 0))
    st = pl.BlockSpec((n, NH, HN, HN), lambda c: (nb - 1 - c, 0, 0, 0))
    return pl.pallas_call(
        body, name=name, grid=(nb,), in_specs=[ck, ck, st, st], out_specs=[ck, st, st],
        out_shape=[_sds((NH, t, HN)), _sds((nc, NH, HN, HN)), _sds((nc, NH, HN, HN))],
        scratch_shapes=[pltpu.VMEM((NH, HN, HN), f32)],
    )(dy, q, s_all, a)


def _ln_silu(c, w, b):
    mu = jnp.mean(c, axis=-1, keepdims=True)
    cc = c - mu
    var = jnp.mean(cc * cc, axis=-1, keepdims=True)
    u = cc * lax.rsqrt(var + LN_EPS) * w + b
    return u * jax.nn.sigmoid(u)


def _glu_tile(pc):
    return pc[:, :DC] * jax.nn.sigmoid(pc[:, DC:])


def _fill_glu(s_ref, pc_ref, halo_ref, first, tm):
    s_ref[pl.ds(0, HALO), :] = jnp.where(first, 0.0, _glu_tile(halo_ref[...]))
    s_ref[pl.ds(HALO, tm), :] = _glu_tile(pc_ref[...])


def _phase_copies(sh):
    n = sh.shape[1] - 8
    for r in range(1, 8):
        sh[r, pl.ds(0, n), :] = sh[0, pl.ds(r, n), :]


def _rows_at(sh, o, n):
    return sh[o % 8, pl.ds(o - o % 8, n), :]


def _conv_fwd(pc, dw, cb, lnw, lnb, name):
    t = pc.shape[0]
    tm = _tile(t, TM_VPU)
    hb = tm // HALO

    def body(pc_ref, halo_ref, dw_ref, cb_ref, w_ref, b_ref, c_ref, o_ref, s):
        _fill_glu(s.at[0], pc_ref, halo_ref, pl.program_id(0) == 0, tm)
        _phase_copies(s)
        for r0 in range(0, tm, CONV_ROWS):
            rows = pl.ds(r0, CONV_ROWS)
            acc = _rows_at(s, r0 + HALO - CW + 1, CONV_ROWS) * dw_ref[pl.ds(0, 1), :]
            for j in range(1, CW):
                acc = acc + _rows_at(s, r0 + HALO - CW + 1 + j, CONV_ROWS) * dw_ref[pl.ds(j, 1), :]
            c = acc + cb_ref[...]
            c_ref[rows, :] = c
            o_ref[rows, :] = _ln_silu(c, w_ref[...], b_ref[...]).astype(bf16)

    return pl.pallas_call(
        body, name=name, grid=(t // tm,),
        in_specs=[pl.BlockSpec((tm, 2 * DC), lambda i: (i, 0)),
                  pl.BlockSpec((HALO, 2 * DC), lambda i: (jnp.maximum(i * hb - 1, 0), 0)),
                  _full((CW, DC)), _full((1, DC)), _full((1, DC)), _full((1, DC))],
        out_specs=[pl.BlockSpec((tm, DC), lambda i: (i, 0)), pl.BlockSpec((tm, DC), lambda i: (i, 0))],
        out_shape=[_sds((t, DC)), _sds((t, DC), bf16)],
        scratch_shapes=[pltpu.VMEM((8, HALO + tm, DC), f32)],
    )(pc, pc, dw, cb, lnw, lnb)


def _conv_bwd1(dcat, c, lnw, lnb, name):
    t = c.shape[0]
    tm = _tile(t, TM_VPU)

    def body(d_ref, c_ref, w_ref, b_ref, dc_ref, dw_ref, db_ref, dcb_ref):
        _, vjp = jax.vjp(_ln_silu, c_ref[...], w_ref[...], b_ref[...])
        dc, dw, db = vjp(d_ref[...])
        dc_ref[...] = dc
        dcb = jnp.sum(dc, axis=0, keepdims=True)
        for ref, gval in ((dw_ref, dw), (db_ref, db), (dcb_ref, dcb)):
            @pl.when(pl.program_id(0) == 0)
            def _(ref=ref, gval=gval):
                ref[...] = gval

            @pl.when(pl.program_id(0) != 0)
            def _(ref=ref, gval=gval):
                ref[...] += gval

    tk = pl.BlockSpec((tm, DC), lambda i: (i, 0))
    return pl.pallas_call(
        body, name=name, grid=(t // tm,),
        in_specs=[pl.BlockSpec((tm, DC), lambda i: (i, 1)), tk, _full((1, DC)), _full((1, DC))],
        out_specs=[tk] + [_full((1, DC))] * 3,
        out_shape=[_sds((t, DC))] + [_sds((1, DC))] * 3,
    )(dcat, c, lnw, lnb)


def _conv_bwd2(dc, pc, dw, name):
    t = pc.shape[0]
    tm = _tile(t, TM_VPU)
    hb = tm // HALO
    last_blk = t // HALO - 1

    def body(dc_ref, dn_ref, pc_ref, halo_ref, dw_ref, dpc_ref, ddw_ref, s, sd, acc):
        i = pl.program_id(0)
        _fill_glu(s.at[0], pc_ref, halo_ref, i == 0, tm)
        sd[0, pl.ds(0, tm), :] = dc_ref[...]
        sd[0, pl.ds(tm, HALO), :] = jnp.where(i == pl.num_programs(0) - 1, 0.0, dn_ref[...])
        _phase_copies(s)
        _phase_copies(sd)
        for r0 in range(0, tm, CONV_ROWS):
            rows = pl.ds(r0, CONV_ROWS)
            dcb = sd[0, rows, :]
            dglu = None
            for j in range(CW):
                term = _rows_at(sd, r0 + CW - 1 - j, CONV_ROWS) * dw_ref[pl.ds(j, 1), :]
                dglu = term if dglu is None else dglu + term
                part = dcb * _rows_at(s, r0 + HALO - CW + 1 + j, CONV_ROWS)
                part8 = part[0:8]
                for q in range(8, CONV_ROWS, 8):
                    part8 = part8 + part[q:q + 8]
                if r0 == 0:
                    acc[pl.ds(8 * j, 8), :] = part8
                else:
                    acc[pl.ds(8 * j, 8), :] += part8
            a = pc_ref[rows, :DC]
            sg = jax.nn.sigmoid(pc_ref[rows, DC:])
            dpc_ref[rows, :DC] = dglu * sg
            dpc_ref[rows, DC:] = dglu * a * sg * (1.0 - sg)
        rows = [jnp.sum(acc[pl.ds(8 * j, 8), :], axis=0, keepdims=True) for j in range(CW)]

        @pl.when(i == 0)
        def _():
            for j in range(CW):
                ddw_ref[pl.ds(j, 1), :] = rows[j]

        @pl.when(i != 0)
        def _():
            for j in range(CW):
                ddw_ref[pl.ds(j, 1), :] += rows[j]

    return pl.pallas_call(
        body, name=name, grid=(t // tm,),
        in_specs=[pl.BlockSpec((tm, DC), lambda i: (i, 0)),
                  pl.BlockSpec((HALO, DC), lambda i: (jnp.minimum((i + 1) * hb, last_blk), 0)),
                  pl.BlockSpec((tm, 2 * DC), lambda i: (i, 0)),
                  pl.BlockSpec((HALO, 2 * DC), lambda i: (jnp.maximum(i * hb - 1, 0), 0)),
                  _full((CW, DC))],
        out_specs=[pl.BlockSpec((tm, 2 * DC), lambda i: (i, 0)), _full((CW, DC))],
        out_shape=[_sds((t, 2 * DC)), _sds((CW, DC))],
        scratch_shapes=[pltpu.VMEM((8, HALO + tm, DC), f32), pltpu.VMEM((8, tm + HALO, DC), f32),
                        pltpu.VMEM((8 * CW, DC), f32)],
    )(dc, dc, pc, pc, dw)


def _adamw(w, g, m, v):
    m = ADAM_B1 * m + (1.0 - ADAM_B1) * g
    v = ADAM_B2 * v + (1.0 - ADAM_B2) * (g * g)
    m_hat = m / (1.0 - ADAM_B1 ** ADAM_STEP)
    v_hat = v / (1.0 - ADAM_B2 ** ADAM_STEP)
    delta = -ADAM_LR * (m_hat / (jnp.sqrt(v_hat) + ADAM_EPS) + ADAM_WD * w)
    return delta, m, v


def _adam_slots(w, slots, m, v, name):
    r, c = w.shape
    tr = next(cand for cand in (512, 352, 256, 128, r) if r % cand == 0)

    def body(w_ref, s_ref, m_ref, v_ref, g_ref, d_ref, nm_ref, nv_ref):
        g = s_ref[0].astype(f32)
        for k in range(1, NDEV):
            g = g + s_ref[k].astype(f32)
        delta, nm, nv = _adamw(w_ref[...], g, m_ref[...], v_ref[...])
        g_ref[...] = g
        d_ref[...] = delta
        nm_ref[...] = nm
        nv_ref[...] = nv

    blk = pl.BlockSpec((tr, c), lambda i: (i, 0))
    return pl.pallas_call(
        body, name=name, grid=(r // tr,),
        in_specs=[blk, pl.BlockSpec((NDEV, tr, c), lambda i: (0, i, 0)), blk, blk],
        out_specs=[blk] * 4, out_shape=[_sds((r, c))] * 4,
    )(w, slots, m, v)


def _sum_slots(slots, name):
    _, r, c = slots.shape

    def body(s_ref, o_ref):
        g = s_ref[0]
        for k in range(1, NDEV):
            g = g + s_ref[k]
        o_ref[...] = g

    return pl.pallas_call(body, name=name, in_specs=[_full((NDEV, r, c))], out_specs=_full((r, c)),
                          out_shape=_sds((r, c)), grid=(1,))(slots)


def _adam_vec(w, g, m, v, name):
    shp = w.shape

    def body(w_ref, g_ref, m_ref, v_ref, d_ref, nm_ref, nv_ref):
        delta, nm, nv = _adamw(w_ref[...], g_ref[...], m_ref[...], v_ref[...])
        d_ref[...] = delta
        nm_ref[...] = nm
        nv_ref[...] = nv

    return pl.pallas_call(body, name=name, grid=(1,), in_specs=[_full(shp)] * 4, out_specs=[_full(shp)] * 3,
                          out_shape=[_sds(shp)] * 3)(w, g, m, v)


def _pack(pieces, total):
    flat = []
    n = 0
    for p in pieces:
        p = p.reshape(-1)
        pad = (-p.shape[0]) % LANES
        if pad:
            p = jnp.pad(p, (0, pad))
        flat.append(p)
        n += p.shape[0]
    if total > n:
        flat.append(jnp.zeros((total - n,), f32))
    return jnp.concatenate(flat).reshape(total // LANES, LANES)


def _unpack(vec, shapes):
    flat = vec.reshape(-1)
    out, off = [], 0
    for s in shapes:
        n = math.prod(s)
        out.append(flat[off:off + n].reshape(s))
        off += n + (-n) % LANES
    return out


def _round_up(n, m):
    return (n + m - 1) // m * m


def kernel(x, ffn1_norm_pre, ffn1_norm_post, ffn1_w_gu, ffn1_w_down, mix_norm_pre, mix_norm_post, w_in, shift_mu, w_up, w0, a_up, a0, g_up, k_k, k_a, r_k, gn_w, gn_b, conv_dw, conv_b, conv_ln_w, conv_ln_b, w_out, ffn2_norm_pre, ffn2_norm_post, ffn2_w_gu, ffn2_w_down, loss_target, m_ffn1_norm_pre, m_ffn1_norm_post, m_ffn1_w_gu, m_ffn1_w_down, m_mix_norm_pre, m_mix_norm_post, m_w_in, m_shift_mu, m_w_up, m_w0, m_a_up, m_a0, m_g_up, m_k_k, m_k_a, m_r_k, m_gn_w, m_gn_b, m_conv_dw, m_conv_b, m_conv_ln_w, m_conv_ln_b, m_w_out, m_ffn2_norm_pre, m_ffn2_norm_post, m_ffn2_w_gu, m_ffn2_w_down, v_ffn1_norm_pre, v_ffn1_norm_post, v_ffn1_w_gu, v_ffn1_w_down, v_mix_norm_pre, v_mix_norm_post, v_w_in, v_shift_mu, v_w_up, v_w0, v_a_up, v_a0, v_g_up, v_k_k, v_k_a, v_r_k, v_gn_w, v_gn_b, v_conv_dw, v_conv_b, v_conv_ln_w, v_conv_ln_b, v_w_out, v_ffn2_norm_pre, v_ffn2_norm_post, v_ffn2_w_gu, v_ffn2_w_down):
    t = x.shape[1]
    me = 4 * lax.axis_index("x") + 2 * lax.axis_index("y") + lax.axis_index("c")
    x0 = x.reshape(t, D)
    tgt = loss_target.reshape(t, D)

    def shard(a):
        return a[0].astype(bf16)

    def shard_t(a):
        return jnp.swapaxes(a[0], 0, 1).astype(bf16)

    (wg1,) = _exchange([shard_t(ffn1_w_gu)], ["gather"], [SLOT42], "gather_wg1")

    def cols(a):
        return jnp.transpose(a, (1, 0, 2)).reshape(a.shape[1], NDEV * a.shape[2])

    lane = jnp.arange(DR, dtype=jnp.int32) // HN
    seg = (lane[:, None] == jnp.arange(LANES, dtype=jnp.int32)[None, :]).astype(f32)
    seg_t = seg.T
    rk_row = r_k.reshape(1, DR)
    post_params = (gn_w, gn_b, rk_row, seg, seg_t)

    (xn1, gu1, h1), (wd1, win_g) = _ffn_in_fwd(x0, ffn1_norm_pre, wg1, "ffn1_in_fwd", comm=(
        [shard(ffn1_w_down), shard_t(w_in)], ["gather"] * 2, [SLOT8] * 2))
    wd1 = wd1.reshape(4, FB, D)
    win_t = win_g.reshape(DIN, D)
    (f1, x1), (wout_g, wup_g, aup_g, gup_g, cdw_g) = _mm_norm_res_fwd(
        h1, wd1, x0, ffn1_norm_post, 0.5, "ffn1_out_fwd", comm=(
            [shard(w_out), w_up[0], a_up[0], g_up[0], conv_dw[0]], ["gather"] * 5, [SLOT8] * 5))
    wout_full = wout_g.reshape(1, D, D)
    wup_full, aup_full, gup_full, cdw_full = cols(wup_g), cols(aup_g), cols(gup_g), cols(cdw_g)
    zeros64 = jnp.zeros((HN, DR), f32)
    wup_pad = jnp.concatenate([wup_full, zeros64], axis=0)
    aup_pad = jnp.concatenate([zeros64, aup_full], axis=0)
    prep_params = (shift_mu, w0, a0, k_k, k_a, wup_pad, aup_pad, gup_full, seg, seg_t)
    hm, ps, pc = _norm_mm_fwd(x1, mix_norm_pre, win_t, "mix_in_fwd")
    rec_in = _prep_fwd(ps, prep_params, "prep_fwd")
    g_gate = rec_in[6]
    rec_in = rec_in[:6]
    (a_c, g_c, q_c, yl_c), got = _r1_fwd(rec_in, "r1_fwd", comm=(
        [shard_t(ffn2_w_gu), shard(ffn2_w_down)], ["gather"] * 2, [SLOT42, SLOT8]))
    wg2, wd2 = got
    wd2 = wd2.reshape(4, FB, D)
    y_rec, s_all = _r2_fwd(a_c, g_c, q_c, yl_c, "r2_fwd")
    c_conv, out_b = _conv_fwd(pc, cdw_full, conv_b, conv_ln_w, conv_ln_b, "conv_fwd")
    cat = _post_fwd(y_rec, rec_in[0], rec_in[2], rec_in[3], g_gate, out_b, post_params, "post_fwd")
    fm, x2 = _mm_norm_res_fwd(cat.reshape(1, t, D), wout_full, x1, mix_norm_post, 1.0, "mix_out_fwd")
    (xn2, gu2, h2), _ = _ffn_in_fwd(x2, ffn2_norm_pre, wg2, "ffn2_in_fwd")
    f2, x3 = _mm_norm_res_fwd(h2, wd2, x2, ffn2_norm_post, 0.5, "ffn2_out_fwd")
    dy, loss_part = _loss_fwd_bwd(x3, tgt, "loss")

    (df2, dg_post2, dgu2), _ = _post_bwd_nt(dy, f2, ffn2_norm_post, wd2, 0.5, gu2, "ffn2_out_bwd")
    (dx2, dg_pre2), _ = _nt_pre_bwd(dgu2, wg2, x2, ffn2_norm_pre, dy, "ffn2_in_bwd")
    dwd2 = _wgrad(h2, df2.reshape(1, t, D), True, False, D, "ffn2_wd_grad")
    dwg2 = _wgrad(dgu2.reshape(8, t, FB), xn2.reshape(1, t, D), True, False, D, "ffn2_wg_grad")
    (dfm, dg_postm, dcat), _ = _post_bwd_nt(dx2, fm, mix_norm_post, wout_full, 1.0, None, "mix_out_bwd")
    dwout = _wgrad(cat.reshape(1, t, D), dfm.reshape(1, t, D), False, False, D, "wout_grad")
    dcat = dcat.reshape(t, D)
    (dy_rec, dr2, dkp2, dv2, dgate, dgn_w, dgn_b, drk), (s_wout,) = _post_bwd(
        dcat, y_rec, rec_in[0], rec_in[2], rec_in[3], g_gate, post_params, "post_bwd", comm=(
            [dwout.reshape(NDEV, D // NDEV, D)], ["scatter"], [SLOT8]))
    dq_c, da_c, dg_c = _r2_bwd(dy_rec, q_c, s_all, a_c, "r2_bwd")
    rec_grads, got = _r1_bwd(rec_in, da_c, dg_c, dq_c, dy_rec, "r1_bwd", comm=(
        [dwg2.reshape(4, 2, FB, D), dwd2.reshape(NDEV, F // NDEV, D)], ["scatter"] * 2, [SLOT42, SLOT8]))
    s_wg2, s_wd2 = got
    prep_out = _prep_bwd(ps, prep_params, rec_grads, (dr2, dkp2, dv2, dgate), "prep_bwd")
    dpsl, dw0, da0, dkk, dka, dwup_pad, daup_pad, dgup = prep_out
    dc_conv, dlnw, dlnb, dcb = _conv_bwd1(dcat, c_conv, conv_ln_w, conv_ln_b, "conv_bwd1")
    dpc, dcdw = _conv_bwd2(dc_conv, pc, cdw_full, "conv_bwd2")
    dp, dmu = _shift_bwd(dpsl, ps, shift_mu, dpc, "shift_bwd")
    (dx1, dg_prem), _ = _nt_pre_bwd(dp.reshape(1, 1, t, DIN), win_t.reshape(1, 1, DIN, D), x1, mix_norm_pre, dx2,
                                    "mix_in_bwd")
    dwin_s = _wgrad(dp.reshape(1, t, DIN), hm.reshape(1, t, D), False, False, D // 2, "win_grad").reshape(
        NDEV, DIN // NDEV, D)
    (df1, dg_post1, dgu1), (s_win,) = _post_bwd_nt(dx1, f1, ffn1_norm_post, wd1, 0.5, gu1, "ffn1_out_bwd", comm=(
        [dwin_s], ["scatter"], [SLOT8]))
    dwd1 = _wgrad(h1, df1.reshape(1, t, D), True, False, D, "ffn1_wd_grad")
    dwg1, (s_wd1,) = _wgrad(dgu1.reshape(8, t, FB), xn1.reshape(1, t, D), True, False, D, "ffn1_wg_grad", comm=(
        [dwd1.reshape(NDEV, F // NDEV, D)], ["scatter"], [SLOT8]))
    (dx0, dg_pre1), (s_wg1,) = _nt_pre_bwd(dgu1, wg1, x0, ffn1_norm_pre, dx1, "ffn1_in_bwd", comm=(
        [dwg1.reshape(4, 2, FB, D)], ["scatter"], [SLOT42]))

    small_shapes = [(1, 1)] + [(1, D)] * 6 + [(1, DS)] + [(1, DR)] * 10 + [(HN, DR), (HN, DR), (LANES, DR), (CW, DR)]
    small_parts = [loss_part, dg_pre1, dg_post1, dg_prem, dg_postm, dg_pre2, dg_post2, dmu,
                   dw0, da0, dkk, dka, drk, dgn_w, dgn_b, dcb, dlnw, dlnb,
                   dwup_pad[:HN], daup_pad[HN:], dgup, dcdw]
    n_small = _round_up(sum(_round_up(math.prod(s), LANES) for s in small_shapes), 8 * LANES)
    (s_small,) = _exchange([_pack(small_parts, n_small)], ["gather"], [SLOT8], "gather_small")

    res = {}
    for nm, w, s, m, v, transposed in (
            ("ffn1_w_gu", ffn1_w_gu, s_wg1.reshape(NDEV, FB, D), m_ffn1_w_gu, v_ffn1_w_gu, True),
            ("ffn1_w_down", ffn1_w_down, s_wd1, m_ffn1_w_down, v_ffn1_w_down, False),
            ("w_in", w_in, s_win, m_w_in, v_w_in, True),
            ("w_out", w_out, s_wout, m_w_out, v_w_out, False),
            ("ffn2_w_gu", ffn2_w_gu, s_wg2.reshape(NDEV, FB, D), m_ffn2_w_gu, v_ffn2_w_gu, True),
            ("ffn2_w_down", ffn2_w_down, s_wd2, m_ffn2_w_down, v_ffn2_w_down, False)):
        view = (lambda a: jnp.swapaxes(a[0], 0, 1)) if transposed else (lambda a: a[0])
        outs = _adam_slots(view(w), s, view(m), view(v), "adam_" + nm)
        res[nm] = [(jnp.swapaxes(o, 0, 1) if transposed else o)[None] for o in outs]

    gsum = _unpack(_sum_slots(s_small, "sum_small"), small_shapes)
    loss = gsum[0].reshape(())
    rep_names = ["ffn1_norm_pre", "ffn1_norm_post", "mix_norm_pre", "mix_norm_post", "ffn2_norm_pre", "ffn2_norm_post",
                 "shift_mu", "w0", "a0", "k_k", "k_a", "r_k", "gn_w", "gn_b", "conv_b", "conv_ln_w", "conv_ln_b"]
    shard_names = ["w_up", "a_up", "g_up", "conv_dw"]
    env = dict(locals())
    small_g = {n: gsum[1 + i] for i, n in enumerate(rep_names)}
    small_g["r_k"] = small_g["r_k"].reshape(1, NH, HN)
    for i, n in enumerate(shard_names):
        small_g[n] = lax.dynamic_slice_in_dim(gsum[18 + i], HN * me, HN, axis=1)[None]
    names = rep_names + shard_names
    shapes = [env[n].shape for n in names]
    n_vec = _round_up(sum(_round_up(math.prod(s), LANES) for s in shapes), 8 * LANES)
    packed = [_pack([d[n] for n in names], n_vec) for d in
              ({n: env[n] for n in names}, small_g, {n: env["m_" + n] for n in names}, {n: env["v_" + n] for n in names})]
    d_vec, m_vec, v_vec = _adam_vec(*packed, "adam_small")
    for n, dl, nm_, nv_ in zip(names, _unpack(d_vec, shapes), _unpack(m_vec, shapes), _unpack(v_vec, shapes)):
        res[n] = [small_g[n], dl, nm_, nv_]

    order = ["ffn1_norm_pre", "ffn1_norm_post", "ffn1_w_gu", "ffn1_w_down", "mix_norm_pre", "mix_norm_post", "w_in",
             "shift_mu", "w_up", "w0", "a_up", "a0", "g_up", "k_k", "k_a", "r_k", "gn_w", "gn_b", "conv_dw", "conv_b",
             "conv_ln_w", "conv_ln_b", "w_out", "ffn2_norm_pre", "ffn2_norm_post", "ffn2_w_gu", "ffn2_w_down"]
    return (loss, dx0.reshape(1, t, D), *[res[n][0] for n in order], *[res[n][1] for n in order],
            *[res[n][2] for n in order], *[res[n][3] for n in order])
```

```python
import functools
import math

import jax
import jax.numpy as jnp
from jax import lax
from jax.experimental import pallas as pl
from jax.experimental.pallas import tpu as pltpu

f32 = jnp.float32
bf16 = jnp.bfloat16

D = 1024
F = 2816
FB = 704
DR = 512
DC = 512
NH = 8
HN = 64
DS = 1792
DIN = 2816
CW = 31
CHUNK = 64
R1_CHUNKS = 2
R2_CHUNKS = 4
NDEV = 8
RMS_EPS = 1e-6
GN_EPS = 64e-5
LN_EPS = 1e-5
DECAY_SCALE = math.exp(-0.5)
ADAM_LR, ADAM_B1, ADAM_B2, ADAM_EPS, ADAM_WD, ADAM_STEP = 0.001, 0.9, 0.999, 1e-08, 0.01, 10
LANES = 128
HALO = 32
CONV_ROWS = 32
TM_MXU = 1024
TM_SUB = 256
TM_VPU = 256

ANY = pl.BlockSpec(memory_space=pl.ANY)


def _full(shape):
    return pl.BlockSpec(shape, lambda *_: (0,) * len(shape))


def _sds(shape, dtype=f32):
    return jax.ShapeDtypeStruct(shape, dtype)


def _dot(a, b):
    return jnp.dot(a, b, preferred_element_type=f32)


def _dot_nt(a, b):
    return lax.dot_general(a, b, (((1,), (1,)), ((), ())), preferred_element_type=f32)


def _dot_tn(a, b):
    return lax.dot_general(a, b, (((0,), (0,)), ((), ())), preferred_element_type=f32)


def _terms(a, n):
    out, rem = [], a
    for i in range(n):
        t = rem.astype(bf16)
        out.append(t)
        if i + 1 < n:
            rem = rem - t.astype(f32)
    return out


def _mm_raw(a, b, kind, mode):
    nb = a.ndim - 2
    bd = tuple(range(nb))
    ca = nb if kind == "tn" else nb + 1
    cb = nb + 1 if kind == "nt" else nb
    dn = (((ca,), (cb,)), (bd, bd))
    pa, pb = mode[:2]
    ta, tb = _terms(a, pa), _terms(b, pb)
    acc = None
    for i in range(pa):
        for j in range(pb):
            if i + j < max(pa, pb):
                p = lax.dot_general(ta[i], tb[j], dn, preferred_element_type=f32)
                acc = p if acc is None else acc + p
    return acc


@functools.partial(jax.custom_vjp, nondiff_argnums=(2, 3))
def mm(a, b, kind, mode):
    return _mm_raw(a, b, kind, mode)


def _mm_fwd(a, b, kind, mode):
    return _mm_raw(a, b, kind, mode), (a, b)


def _ct_terms(keep):
    return EXACT_TERMS if keep == 1 else keep


def _mm_bwd(kind, mode, res, ct):
    a, b = res
    pa, pb = mode[:2]
    if len(mode) == 3:
        pa = pb = ca = cb = mode[2]
    elif mode == (1, 1):
        ca = cb = 1
    else:
        ca, cb = _ct_terms(pb), _ct_terms(pa)
    if kind == "nn":
        da = mm(ct, b, "nt", (ca, pb))
        db = mm(a, ct, "tn", (pa, cb))
    elif kind == "nt":
        da = mm(ct, b, "nn", (ca, pb))
        db = mm(ct, a, "tn", (cb, pa))
    else:
        da = mm(b, ct, "nt", (pb, ca))
        db = mm(a, ct, "nn", (pa, cb))
    return da, db


mm.defvjp(_mm_fwd, _mm_bwd)

X3 = (2, 2)
EXACT_TERMS = 2
EA = (1, EXACT_TERMS)
EB = (EXACT_TERMS, 1)
X1 = (1, 1)
RGM = (1, 1, 2)
RIM = X1
RWM = X1
RAM = X1
R2M = X1
LORA = X1


def _rms_fwd(x, g):
    inv = lax.rsqrt(jnp.mean(x * x, axis=-1, keepdims=True) + RMS_EPS)
    return x * inv * g


def _rms_bwd(x, g, dy):
    inv = lax.rsqrt(jnp.mean(x * x, axis=-1, keepdims=True) + RMS_EPS)
    xh = x * inv
    dxh = dy * g
    dg = jnp.sum(dy * xh, axis=0, keepdims=True)
    dx = inv * (dxh - xh * jnp.mean(dxh * xh, axis=-1, keepdims=True))
    return dx, dg


def _tile(t, want):
    return min(t, want)


NPEER = NDEV - 1


def _comm_out_shape(arrays, modes, slots):
    out = []
    for a, mode, (lead, _) in zip(arrays, modes, slots):
        shp = tuple(lead) + tuple(a.shape) if mode == "gather" else tuple(a.shape)
        out.append(pltpu.HBM(shp, a.dtype))
    return out


def _comm_sems(n):
    return [pltpu.SemaphoreType.DMA((n * NPEER,)), pltpu.SemaphoreType.DMA((n * NPEER,)), pltpu.SemaphoreType.DMA((n,))]


RELAYED = (3, 5, 7)


def _comm_copies(ins, outs, sems, modes, slots, want):
    n = len(ins)
    send_sems, recv_sems, loc_sems = sems
    x, y, c = lax.axis_index("x"), lax.axis_index("y"), lax.axis_index("c")
    me = 4 * x + 2 * y + c

    def peer_of(k):
        px = 1 - x if k & 4 else x
        py = 1 - y if k & 2 else y
        pc = 1 - c if k & 1 else c
        return (px, py, pc), 4 * px + 2 * py + pc

    def slot(i, dev):
        return outs[i].at[slots[i][1](dev)]

    if want == "local":
        return [pltpu.make_async_copy(ins[i] if modes[i] == "gather" else ins[i].at[slots[i][1](me)], slot(i, me),
                                      loc_sems.at[i]) for i in range(n)]
    sibling = peer_of(1)[0]
    out = []
    for k in range(1, NDEV):
        peer, pid = peer_of(k)
        for i in range(n):
            sem = dict(send_sem=send_sems.at[i * NPEER + k - 1], recv_sem=recv_sems.at[i * NPEER + k - 1],
                       device_id_type=pl.DeviceIdType.MESH)
            relayed = modes[i] == "gather" and k in RELAYED
            if want == "recv":
                src = ins[i] if modes[i] == "gather" else ins[i].at[slots[i][1](pid)]
                out.append(pltpu.make_async_remote_copy(src_ref=src, dst_ref=slot(i, pid), device_id=peer, **sem))
            elif want == "first" and not relayed:
                src = ins[i] if modes[i] == "gather" else ins[i].at[slots[i][1](pid)]
                out.append(pltpu.make_async_remote_copy(src_ref=src, dst_ref=slot(i, me), device_id=peer, **sem))
            elif want == "relay" and relayed:
                origin = peer_of(k - 1)[1]
                out.append(((k - 2) * n + i, pltpu.make_async_remote_copy(
                    src_ref=slot(i, origin), dst_ref=slot(i, origin), device_id=sibling, **sem)))
    return out


def _comm_start(ins, outs, sems, modes, slots):
    for cp in _comm_copies(ins, outs, sems, modes, slots, "local") + _comm_copies(ins, outs, sems, modes, slots, "first"):
        cp.start()


def _comm_wait(ins, outs, sems, modes, slots):
    recvs = _comm_copies(ins, outs, sems, modes, slots, "recv")
    relays = _comm_copies(ins, outs, sems, modes, slots, "relay")
    for idx, cp in relays:
        recvs[idx].wait_recv()
        cp.start()
    passed_on = {idx for idx, _ in relays}
    for idx, cp in enumerate(recvs):
        if idx not in passed_on:
            cp.wait_recv()
    for cp in _comm_copies(ins, outs, sems, modes, slots, "first") + [cp for _, cp in relays]:
        cp.wait_send()
    for cp in _comm_copies(ins, outs, sems, modes, slots, "local"):
        cp.wait()


def _exchange(arrays, modes, slots, name):
    n = len(arrays)

    def body(*refs):
        ins, outs, sems = refs[:n], refs[n:2 * n], refs[2 * n:]
        _comm_start(ins, outs, sems, modes, slots)
        _comm_wait(ins, outs, sems, modes, slots)

    return pl.pallas_call(
        body, name=name, out_shape=_comm_out_shape(arrays, modes, slots),
        in_specs=[ANY] * n, out_specs=[ANY] * n, scratch_shapes=_comm_sems(n),
    )(*arrays)


def _pcall(body, *, name, grid, in_specs, out_specs, out_shape, args, scratch_shapes=(), comm=None):
    if comm is None:
        return pl.pallas_call(body, name=name, grid=grid, in_specs=in_specs, out_specs=out_specs, out_shape=out_shape,
                              scratch_shapes=list(scratch_shapes))(*args), None
    arrays, modes, slots = comm
    n_in, n_out, n_scr, nc = len(args), len(out_shape), len(scratch_shapes), len(arrays)

    def hosted(*refs):
        a_in, c_in = refs[:n_in], refs[n_in:n_in + nc]
        o = n_in + nc
        a_out, c_out = refs[o:o + n_out], refs[o + n_out:o + n_out + nc]
        o += n_out + nc
        a_scr, sems = refs[o:o + n_scr], refs[o + n_scr:]
        first = pl.program_id(0) == 0
        last = pl.program_id(0) == grid[0] - 1
        for ax in range(1, len(grid)):
            first = jnp.logical_and(first, pl.program_id(ax) == 0)
            last = jnp.logical_and(last, pl.program_id(ax) == grid[ax] - 1)

        @pl.when(first)
        def _():
            _comm_start(c_in, c_out, sems, modes, slots)

        body(*a_in, *a_out, *a_scr)

        @pl.when(last)
        def _():
            _comm_wait(c_in, c_out, sems, modes, slots)

    res = pl.pallas_call(
        hosted, name=name, grid=grid, in_specs=list(in_specs) + [ANY] * nc, out_specs=list(out_specs) + [ANY] * nc,
        out_shape=list(out_shape) + _comm_out_shape(arrays, modes, slots),
        scratch_shapes=list(scratch_shapes) + _comm_sems(nc),
    )(*args, *arrays)
    return res[:n_out], res[n_out:]


SLOT8 = ((NDEV,), lambda d: (d,))
SLOT42 = ((4, 2), lambda d: (lax.rem(d, 4), lax.div(d, 4)))


def _ffn_in_fwd(x, g, wg4, name, comm=None):
    t = x.shape[0]
    tm = _tile(t, TM_MXU)
    sub = _tile(tm, TM_SUB)

    def body(x_ref, g_ref, w_ref, xn_ref, gu_ref, h_ref):
        for s in range(tm // sub):
            rows = pl.ds(s * sub, sub)
            xn = _rms_fwd(x_ref[rows, :], g_ref[...]).astype(bf16)
            xn_ref[rows, :] = xn
            gate = _dot_nt(xn, w_ref[0, 0])
            up = _dot_nt(xn, w_ref[0, 1])
            gu_ref[0, 0, rows, :] = gate.astype(bf16)
            gu_ref[0, 1, rows, :] = up.astype(bf16)
            h_ref[0, rows, :] = (gate * jax.nn.sigmoid(gate) * up).astype(bf16)

    return _pcall(
        body, name=name, grid=(t // tm, 4),
        in_specs=[pl.BlockSpec((tm, D), lambda i, e: (i, 0)), _full((1, D)),
                  pl.BlockSpec((1, 2, FB, D), lambda i, e: (e, 0, 0, 0))],
        out_specs=[pl.BlockSpec((tm, D), lambda i, e: (i, 0)),
                   pl.BlockSpec((1, 2, tm, FB), lambda i, e: (e, 0, i, 0)),
                   pl.BlockSpec((1, tm, FB), lambda i, e: (e, i, 0))],
        out_shape=[_sds((t, D), bf16), _sds((4, 2, t, FB), bf16), _sds((4, t, FB), bf16)],
        args=(x, g, wg4), comm=comm)


def _mm_norm_res_fwd(hb, wb, xres, g, scale, name, comm=None, tgt=None):
    e_n, t, k = hb.shape
    tm = _tile(t, TM_MXU)

    def body(h_ref, w_ref, x_ref, g_ref, *rest):
        f = _dot(h_ref[0], w_ref[0])
        for e in range(1, e_n):
            f = f + _dot(h_ref[e], w_ref[e])
        xnew = x_ref[...] + scale * _rms_fwd(f, g_ref[...])
        if tgt is None:
            f_ref, o_ref = rest
            o_ref[...] = xnew
        else:
            t_ref, f_ref, o_ref, l_ref = rest
            err = xnew - t_ref[...]
            o_ref[...] = err * (1.0 / D)
            part = 0.5 * jnp.sum(jnp.mean(err * err, axis=-1, keepdims=True), axis=0, keepdims=True)

            @pl.when(pl.program_id(0) == 0)
            def _():
                l_ref[...] = part

            @pl.when(pl.program_id(0) != 0)
            def _():
                l_ref[...] += part
        f_ref[...] = f

    tile = pl.BlockSpec((tm, D), lambda i: (i, 0))
    outs, got = _pcall(
        body, name=name, grid=(t // tm,),
        in_specs=[pl.BlockSpec((e_n, tm, k), lambda i: (0, i, 0)), _full((e_n, k, D)), tile, _full((1, D))]
        + ([] if tgt is None else [tile]),
        out_specs=[tile, tile] + ([] if tgt is None else [_full((1, 1))]),
        out_shape=[_sds((t, D)), _sds((t, D))] + ([] if tgt is None else [_sds((1, 1))]),
        args=(hb, wb, xres, g) + (() if tgt is None else (tgt,)), comm=comm)
    return outs if comm is None else (outs, got)


def _post_bwd_nt(dxn, f, g, wb, scale, gu, name, comm=None):
    e_n, k, _ = wb.shape
    t = f.shape[0]
    tm = _tile(t, TM_MXU)
    sub = _tile(tm, TM_SUB)
    swiglu = gu is not None

    def body(*refs):
        if swiglu:
            dx_ref, f_ref, g_ref, w_ref, gu_ref, df_ref, dg_ref, dh_ref, df_s = refs
        else:
            dx_ref, f_ref, g_ref, w_ref, df_ref, dg_ref, dh_ref, df_s = refs
        i, e = pl.program_id(0), pl.program_id(1)

        @pl.when(e == 0)
        def _():
            df, dg = _rms_bwd(f_ref[...], g_ref[...], scale * dx_ref[...])
            df_s[...] = df.astype(bf16)
            df_ref[...] = df_s[...]

            @pl.when(i == 0)
            def _():
                dg_ref[...] = dg

            @pl.when(i != 0)
            def _():
                dg_ref[...] += dg

        for s in range(tm // sub):
            rows = pl.ds(s * sub, sub)
            dh = _dot_nt(df_s[rows, :], w_ref[0])
            if swiglu:
                gate = gu_ref[0, 0, rows, :].astype(f32)
                up = gu_ref[0, 1, rows, :].astype(f32)
                sg = jax.nn.sigmoid(gate)
                dh_ref[0, 0, rows, :] = (dh * up * (sg * (1.0 + gate * (1.0 - sg)))).astype(bf16)
                dh_ref[0, 1, rows, :] = (dh * gate * sg).astype(bf16)
            else:
                dh_ref[0, rows, :] = dh

    in_specs = [pl.BlockSpec((tm, D), lambda i, e: (i, 0)), pl.BlockSpec((tm, D), lambda i, e: (i, 0)), _full((1, D)),
                pl.BlockSpec((1, k, D), lambda i, e: (e, 0, 0))]
    args = [dxn, f, g, wb]
    if swiglu:
        in_specs.append(pl.BlockSpec((1, 2, tm, k), lambda i, e: (e, 0, i, 0)))
        args.append(gu)
        dh_spec = pl.BlockSpec((1, 2, tm, k), lambda i, e: (e, 0, i, 0))
        dh_shape = _sds((e_n, 2, t, k), bf16)
    else:
        dh_spec = pl.BlockSpec((1, tm, k), lambda i, e: (e, i, 0))
        dh_shape = _sds((e_n, t, k), f32)
    return _pcall(
        body, name=name, grid=(t // tm, e_n), in_specs=in_specs,
        out_specs=[pl.BlockSpec((tm, D), lambda i, e: (i, 0)), _full((1, D)), dh_spec],
        out_shape=[_sds((t, D), bf16), _sds((1, D)), dh_shape],
        scratch_shapes=[pltpu.VMEM((tm, D), bf16)], args=args, comm=comm)


def _nt_pre_bwd(dy, wb, x, g, dres, name, comm=None):
    e_n, q_n, t, k = dy.shape
    tm = _tile(t, TM_MXU)

    def body(dy_ref, w_ref, x_ref, g_ref, r_ref, dx_ref, dg_ref, acc):
        i, e = pl.program_id(0), pl.program_id(1)
        p = _dot(dy_ref[0, 0], w_ref[0, 0])
        for q in range(1, q_n):
            p = p + _dot(dy_ref[0, q], w_ref[0, q])

        @pl.when(e == 0)
        def _():
            acc[...] = p

        @pl.when(e != 0)
        def _():
            acc[...] += p

        @pl.when(e == e_n - 1)
        def _():
            dx, dg = _rms_bwd(x_ref[...], g_ref[...], acc[...])
            dx_ref[...] = r_ref[...] + dx

            @pl.when(i == 0)
            def _():
                dg_ref[...] = dg

            @pl.when(i != 0)
            def _():
                dg_ref[...] += dg

    return _pcall(
        body, name=name, grid=(t // tm, e_n),
        in_specs=[pl.BlockSpec((1, q_n, tm, k), lambda i, e: (e, 0, i, 0)),
                  pl.BlockSpec((1, q_n, k, D), lambda i, e: (e, 0, 0, 0)),
                  pl.BlockSpec((tm, D), lambda i, e: (i, 0)), _full((1, D)),
                  pl.BlockSpec((tm, D), lambda i, e: (i, 0))],
        out_specs=[pl.BlockSpec((tm, D), lambda i, e: (i, 0)), _full((1, D))],
        out_shape=[_sds((t, D)), _sds((1, D))],
        scratch_shapes=[pltpu.VMEM((tm, D), f32)], args=(dy, wb, x, g, dres), comm=comm)


def _wgrad(a, b, a_batched, b_batched, tn, name, comm=None):
    t, k = a.shape[1], a.shape[2]
    nn = b.shape[2]
    nb = max(a.shape[0], b.shape[0])
    tt = _tile(t, TM_MXU)
    nt = t // tt

    def body(a_ref, b_ref, o_ref, acc):
        @pl.when(pl.program_id(2) == 0)
        def _():
            acc[...] = _dot_tn(a_ref[0], b_ref[0])

        @pl.when(pl.program_id(2) != 0)
        def _():
            acc[...] += _dot_tn(a_ref[0], b_ref[0])

        @pl.when(pl.program_id(2) == nt - 1)
        def _():
            o_ref[0] = acc[...].astype(bf16)

    (out,), got = _pcall(
        body, name=name, grid=(nb, nn // tn, nt),
        in_specs=[pl.BlockSpec((1, tt, k), (lambda n, j, s: (n, s, 0)) if a_batched else (lambda n, j, s: (0, s, 0))),
                  pl.BlockSpec((1, tt, tn), (lambda n, j, s: (n, s, j)) if b_batched else (lambda n, j, s: (0, s, j)))],
        out_specs=[pl.BlockSpec((1, k, tn), lambda n, j, s: (n, 0, j))],
        out_shape=[_sds((nb, k, nn), bf16)],
        scratch_shapes=[pltpu.VMEM((k, tn), f32)], args=(a, b), comm=comm)
    return out if comm is None else (out, got)


def _norm_mm_fwd(x, g, w, name):
    t = x.shape[0]
    tm = _tile(t, TM_VPU)

    def body(x_ref, g_ref, w_ref, xn_ref, ps_ref, pc_ref):
        xn = _rms_fwd(x_ref[...], g_ref[...]).astype(bf16)
        xn_ref[...] = xn
        ps_ref[...] = _dot_nt(xn, w_ref[:DS, :])
        pc_ref[...] = _dot_nt(xn, w_ref[DS:, :])

    return pl.pallas_call(
        body, name=name, grid=(t // tm,),
        in_specs=[pl.BlockSpec((tm, D), lambda i: (i, 0)), _full((1, D)), _full((DIN, D))],
        out_specs=[pl.BlockSpec((tm, D), lambda i: (i, 0)), pl.BlockSpec((tm, DS), lambda i: (i, 0)),
                   pl.BlockSpec((tm, 2 * DC), lambda i: (i, 0))],
        out_shape=[_sds((t, D), bf16), _sds((t, DS)), _sds((t, 2 * DC))],
    )(x, g, w)


def _segsum_b(x, seg, seg_t):
    return mm(mm(x, seg, "nn", EB), seg_t, "nn", EB)


def _prep_fn(psl, w0, a0, k_k, k_a, wup, aup, gup, seg, seg_t):
    r, k, v = psl[:, :DR], psl[:, DR:2 * DR], psl[:, 2 * DR:3 * DR]
    xwa, xg = psl[:, 3 * DR:3 * DR + LANES], psl[:, 3 * DR + LANES:]
    d = w0 + mm(jnp.tanh(xwa), wup, "nn", LORA)
    lw = -DECAY_SCALE * jax.nn.sigmoid(d)
    a = jax.nn.sigmoid(a0 + mm(xwa, aup, "nn", LORA))
    g = mm(jax.nn.sigmoid(xg), gup, "nn", LORA)
    kkr = k * k_k
    kk = kkr * lax.rsqrt(jnp.maximum(_segsum_b(kkr * kkr, seg, seg_t), 1e-12))
    kp = k * (1.0 + (a - 1.0) * k_a)
    return r, lw, kp, v, -kk, kk * a, g


def _shifted(ps, halo_row, first):
    prev = jnp.where(first, 0.0, halo_row)
    sh = pltpu.roll(ps, 1, 0)
    row = lax.broadcasted_iota(jnp.int32, ps.shape, 0)
    return jnp.where(row == 0, prev, sh)


def _heads_split(ref, val):
    for h in range(NH):
        ref[h] = val[:, h * HN:(h + 1) * HN]


def _heads_merge(ref):
    return jnp.concatenate([ref[h] for h in range(NH)], axis=-1)


_PREP_PARAM_SHAPES = [(1, DS), (1, DR), (1, DR), (1, DR), (1, DR), (LANES, DR), (LANES, DR), (LANES, DR),
                      (DR, LANES), (LANES, DR)]


def _prep_fwd(ps, params, name, comm=None):
    t = ps.shape[0]
    tm = _tile(t, TM_VPU)
    hb = tm // 8

    def body(ps_ref, halo_ref, mu_ref, *rest):
        prm = [r[...] for r in rest[:9]]
        outs = rest[9:]
        x = ps_ref[...]
        sh = _shifted(x, halo_ref[pl.ds(7, 1), :], pl.program_id(0) == 0)
        psl = x + (sh - x) * mu_ref[...]
        vals = _prep_fn(psl, *prm)
        for ref, val in zip(outs[:6], vals[:6]):
            _heads_split(ref, val)
        outs[6][...] = vals[6]

    hm = pl.BlockSpec((NH, tm, HN), lambda i: (0, i, 0))
    outs, got = _pcall(
        body, name=name, grid=(t // tm,),
        in_specs=[pl.BlockSpec((tm, DS), lambda i: (i, 0)),
                  pl.BlockSpec((8, DS), lambda i: (jnp.maximum(i * hb - 1, 0), 0))]
        + [_full(s) for s in _PREP_PARAM_SHAPES],
        out_specs=[hm] * 6 + [pl.BlockSpec((tm, DR), lambda i: (i, 0))],
        out_shape=[_sds((NH, t, HN))] * 6 + [_sds((t, DR))], args=(ps, ps, *params), comm=comm)
    return outs if comm is None else (outs, got)


def _prep_bwd(ps, params, cts_hm, cts_tm, name):
    t = ps.shape[0]
    tm = _tile(t, TM_VPU)
    hb = tm // 8

    def body(ps_ref, halo_ref, mu_ref, *rest):
        prm = [r[...] for r in rest[:9]]
        chm = rest[9:15]
        ctm = rest[15:19]
        dpsl_ref = rest[19]
        gouts = rest[20:27]
        x = ps_ref[...]
        sh = _shifted(x, halo_ref[pl.ds(7, 1), :], pl.program_id(0) == 0)
        psl = x + (sh - x) * mu_ref[...]
        seg, seg_t = prm[7], prm[8]
        _, vjp = jax.vjp(lambda p, *w: _prep_fn(p, *w, seg, seg_t), psl, *prm[:7])
        c = [_heads_merge(r) for r in chm]
        cts = (c[0] + ctm[0][...], c[1], c[2] + ctm[1][...], c[3] + ctm[2][...], c[4], c[5], ctm[3][...])
        grads = vjp(cts)
        dpsl_ref[...] = grads[0]
        for ref, gval in zip(gouts, grads[1:]):
            @pl.when(pl.program_id(0) == 0)
            def _(ref=ref, gval=gval):
                ref[...] = gval

            @pl.when(pl.program_id(0) != 0)
            def _(ref=ref, gval=gval):
                ref[...] += gval

    hm = pl.BlockSpec((NH, tm, HN), lambda i: (0, i, 0))
    tk = pl.BlockSpec((tm, DR), lambda i: (i, 0))
    gshapes = _PREP_PARAM_SHAPES[1:8]
    return pl.pallas_call(
        body, name=name, grid=(t // tm,),
        in_specs=[pl.BlockSpec((tm, DS), lambda i: (i, 0)),
                  pl.BlockSpec((8, DS), lambda i: (jnp.maximum(i * hb - 1, 0), 0))]
        + [_full(s) for s in _PREP_PARAM_SHAPES] + [hm] * 6 + [tk] * 4,
        out_specs=[pl.BlockSpec((tm, DS), lambda i: (i, 0))] + [_full(s) for s in gshapes],
        out_shape=[_sds((t, DS))] + [_sds(s) for s in gshapes],
    )(ps, ps, *params, *cts_hm, *cts_tm)


def _shift_bwd(dpsl, ps, mu, dpc, name):
    t = ps.shape[0]
    tm = _tile(t, TM_VPU)
    hb = tm // 8
    last_blk = t // 8 - 1

    def body(d_ref, dn_ref, ps_ref, halo_ref, mu_ref, dpc_ref, dp_ref, dmu_ref):
        i = pl.program_id(0)
        mu_v = mu_ref[...]
        d = d_ref[...]
        nxt = jnp.where(i == pl.num_programs(0) - 1, 0.0, dn_ref[pl.ds(0, 1), :])
        up = pltpu.roll(d, tm - 1, 0)
        row = lax.broadcasted_iota(jnp.int32, d.shape, 0)
        up = jnp.where(row == tm - 1, nxt, up)
        dp_ref[:, :DS] = (d * (1.0 - mu_v) + up * mu_v).astype(bf16)
        dp_ref[:, DS:] = dpc_ref[...].astype(bf16)
        x = ps_ref[...]
        sh = _shifted(x, halo_ref[pl.ds(7, 1), :], i == 0)
        dmu = jnp.sum(d * (sh - x), axis=0, keepdims=True)

        @pl.when(i == 0)
        def _():
            dmu_ref[...] = dmu

        @pl.when(i != 0)
        def _():
            dmu_ref[...] += dmu

    return pl.pallas_call(
        body, name=name, grid=(t // tm,),
        in_specs=[pl.BlockSpec((tm, DS), lambda i: (i, 0)),
                  pl.BlockSpec((8, DS), lambda i: (jnp.minimum((i + 1) * hb, last_blk), 0)),
                  pl.BlockSpec((tm, DS), lambda i: (i, 0)),
                  pl.BlockSpec((8, DS), lambda i: (jnp.maximum(i * hb - 1, 0), 0)),
                  _full((1, DS)), pl.BlockSpec((tm, 2 * DC), lambda i: (i, 0))],
        out_specs=[pl.BlockSpec((tm, DIN), lambda i: (i, 0)), _full((1, DS))],
        out_shape=[_sds((t, DIN), bf16), _sds((1, DS))],
    )(dpsl, dpsl, ps, ps, mu, dpc)


def _post_fn(y, r, kp, v, g, gn_w, gn_b, r_k, seg, seg_t):
    mu = _segsum_b(y, seg, seg_t) * (1.0 / HN)
    yc = y - mu
    var = _segsum_b(yc * yc, seg, seg_t) * (1.0 / HN)
    yo = yc * lax.rsqrt(var + GN_EPS) * gn_w + gn_b
    bonus = _segsum_b(r * kp * r_k, seg, seg_t) * v
    return (yo + bonus) * g


_POST_PARAM_SHAPES = [(1, DR), (1, DR), (1, DR), (DR, LANES), (LANES, DR)]


def _post_fwd(y, r, kp, v, g, out_b, params, name):
    t = g.shape[0]
    tm = _tile(t, TM_VPU)

    def body(y_ref, r_ref, k_ref, v_ref, g_ref, ob_ref, *rest):
        prm = [p[...] for p in rest[:5]]
        cat_ref = rest[5]
        oa = _post_fn(_heads_merge(y_ref), _heads_merge(r_ref), _heads_merge(k_ref), _heads_merge(v_ref),
                      g_ref[...], *prm)
        cat_ref[:, :DR] = oa.astype(bf16)
        cat_ref[:, DR:] = ob_ref[...]

    hm = pl.BlockSpec((NH, tm, HN), lambda i: (0, i, 0))
    tk = pl.BlockSpec((tm, DR), lambda i: (i, 0))
    return pl.pallas_call(
        body, name=name, grid=(t // tm,),
        in_specs=[hm] * 4 + [tk, tk] + [_full(s) for s in _POST_PARAM_SHAPES],
        out_specs=pl.BlockSpec((tm, 2 * DR), lambda i: (i, 0)),
        out_shape=_sds((t, 2 * DR), bf16),
    )(y, r, kp, v, g, out_b, *params)


def _post_bwd(dcat, y, r, kp, v, g, params, name, comm=None):
    t = g.shape[0]
    tm = _tile(t, TM_VPU)

    def body(dc_ref, y_ref, r_ref, k_ref, v_ref, g_ref, *rest):
        prm = [p[...] for p in rest[:5]]
        dy_ref, dr_ref, dk_ref, dv_ref, dg_ref = rest[5:10]
        gouts = rest[10:13]
        seg, seg_t = prm[3], prm[4]
        _, vjp = jax.vjp(lambda *a: _post_fn(*a, seg, seg_t),
                         _heads_merge(y_ref), _heads_merge(r_ref), _heads_merge(k_ref), _heads_merge(v_ref),
                         g_ref[...], *prm[:3])
        grads = vjp(dc_ref[...])
        _heads_split(dy_ref, grads[0])
        dr_ref[...] = grads[1]
        dk_ref[...] = grads[2]
        dv_ref[...] = grads[3]
        dg_ref[...] = grads[4]
        for ref, gval in zip(gouts, grads[5:]):
            @pl.when(pl.program_id(0) == 0)
            def _(ref=ref, gval=gval):
                ref[...] = gval

            @pl.when(pl.program_id(0) != 0)
            def _(ref=ref, gval=gval):
                ref[...] += gval

    hm = pl.BlockSpec((NH, tm, HN), lambda i: (0, i, 0))
    tk = pl.BlockSpec((tm, DR), lambda i: (i, 0))
    return _pcall(
        body, name=name, grid=(t // tm,),
        in_specs=[tk] + [hm] * 4 + [tk] + [_full(s) for s in _POST_PARAM_SHAPES],
        out_specs=[hm] + [tk] * 4 + [_full((1, DR))] * 3,
        out_shape=[_sds((NH, t, HN))] + [_sds((t, DR))] * 4 + [_sds((1, DR))] * 3,
        args=(dcat, y, r, kp, v, g, *params), comm=comm)


@jax.custom_vjp
def _inv_unit_lower(lb):
    c = lb.shape[-1]
    row = lax.broadcasted_iota(jnp.int32, (c, c), 0)
    col = lax.broadcasted_iota(jnp.int32, (c, c), 1)
    p = (row == col).astype(f32)[None] + lb
    m = lb
    for _ in range(c.bit_length() - 2):
        m = mm(m, m, "nn", RIM)
        p = p + mm(m, p, "nn", RIM)
    return p


def _inv_unit_lower_fwd(lb):
    p = _inv_unit_lower(lb)
    return p, p


def _inv_unit_lower_bwd(p, ct):
    return (mm(mm(p, ct, "tn", RIM), p, "nt", RIM),)


_inv_unit_lower.defvjp(_inv_unit_lower_fwd, _inv_unit_lower_bwd)


def _r1_fn(r, lw, k, v, z, b):
    c = r.shape[1]
    row = lax.broadcasted_iota(jnp.int32, (c, c), 0)
    col = lax.broadcasted_iota(jnp.int32, (c, c), 1)
    incl = (row >= col)[None]
    strict = (row > col)[None]
    eye = (row == col).astype(f32)[None]
    tri = jnp.broadcast_to((row >= col).astype(f32)[None], (r.shape[0], c, c))
    cum = mm(tri, lw, "nn", EA)
    tot = jnp.sum(lw, axis=1, keepdims=True)
    zt = z * jnp.exp(cum - lw)
    rt = r * jnp.exp(cum)
    g_inv = jnp.exp(-cum)
    g_rem = jnp.exp(tot - cum)
    bt, kt = b * g_inv, k * g_inv
    bh, kh = b * g_rem, k * g_rem
    lb = jnp.where(strict, mm(zt, bt, "nt", RGM), 0.0)
    lk = jnp.where(strict, mm(zt, kt, "nt", RGM), 0.0)
    mb = jnp.where(incl, mm(rt, bt, "nt", RGM), 0.0)
    mk = jnp.where(incl, mm(rt, kt, "nt", RGM), 0.0)
    p = _inv_unit_lower(lb)
    w1 = mm(p, zt, "nn", RWM)
    w2 = mm(p, mm(lk, v, "nn", RWM), "nn", RWM)
    a = mm(w1, bh, "tn", RAM) + eye * jnp.exp(tot)
    g = mm(w2, bh, "tn", RAM) + mm(v, kh, "tn", RAM)
    q = rt + mm(mb, w1, "nn", RWM)
    yl = mm(mb, w2, "nn", RWM) + mm(mk, v, "nn", RWM)
    return a, g, q, yl


def _chunks_in(ref, n):
    return jnp.concatenate([ref[:, s * CHUNK:(s + 1) * CHUNK, :] for s in range(n)], axis=0)


def _chunks_out(ref, val, n):
    for s in range(n):
        ref[:, s * CHUNK:(s + 1) * CHUNK, :] = val[s * NH:(s + 1) * NH]


def _r1_fwd(ins, name, comm=None):
    t = ins[0].shape[1]
    n = R1_CHUNKS
    nc = t // CHUNK

    def body(r, lw, k, v, z, b, a_ref, g_ref, q_ref, yl_ref):
        a, g, q, yl = _r1_fn(*[_chunks_in(x, n) for x in (r, lw, k, v, z, b)])
        for s in range(n):
            a_ref[s] = a[s * NH:(s + 1) * NH]
            g_ref[s] = g[s * NH:(s + 1) * NH]
        _chunks_out(q_ref, q, n)
        _chunks_out(yl_ref, yl, n)

    ck = pl.BlockSpec((NH, n * CHUNK, HN), lambda c: (0, c, 0))
    st = pl.BlockSpec((n, NH, HN, HN), lambda c: (c, 0, 0, 0))
    return _pcall(
        body, name=name, grid=(nc // n,), in_specs=[ck] * 6, out_specs=[st, st, ck, ck],
        out_shape=[_sds((nc, NH, HN, HN))] * 2 + [_sds((NH, t, HN))] * 2, args=tuple(ins), comm=comm)


def _r1_bwd(ins, da, dg, dq, dyl, name, comm=None):
    t = ins[0].shape[1]
    n = R1_CHUNKS
    nc = t // CHUNK

    def body(r, lw, k, v, z, b, da_ref, dg_ref, dq_ref, dyl_ref, *outs):
        _, vjp = jax.vjp(_r1_fn, *[_chunks_in(x, n) for x in (r, lw, k, v, z, b)])
        cts = (jnp.concatenate([da_ref[s] for s in range(n)], axis=0),
               jnp.concatenate([dg_ref[s] for s in range(n)], axis=0), _chunks_in(dq_ref, n), _chunks_in(dyl_ref, n))
        for ref, gval in zip(outs, vjp(cts)):
            _chunks_out(ref, gval, n)

    ck = pl.BlockSpec((NH, n * CHUNK, HN), lambda c: (0, c, 0))
    st = pl.BlockSpec((n, NH, HN, HN), lambda c: (c, 0, 0, 0))
    return _pcall(
        body, name=name, grid=(nc // n,), in_specs=[ck] * 6 + [st, st, ck, ck], out_specs=[ck] * 6,
        out_shape=[_sds((NH, t, HN))] * 6, args=(*ins, da, dg, dq, dyl), comm=comm)


def _r2_fwd(a, g, q, yl, name):
    nc = a.shape[0]
    t = q.shape[1]

    def body(a_ref, g_ref, q_ref, yl_ref, y_ref, s_ref, s):
        @pl.when(pl.program_id(0) == 0)
        def _():
            s[...] = jnp.zeros_like(s)
        s0 = s[...]
        for j in range(n):
            rows = pl.ds(j * CHUNK, CHUNK)
            s_ref[j] = s0
            y_ref[:, rows, :] = mm(q_ref[:, rows, :], s0, "nt", R2M) + yl_ref[:, rows, :]
            s0 = mm(s0, a_ref[j], "nn", R2M) + g_ref[j]
        s[...] = s0

    n = math.gcd(nc, R2_CHUNKS)
    ck = pl.BlockSpec((NH, n * CHUNK, HN), lambda c: (0, c, 0))
    st = pl.BlockSpec((n, NH, HN, HN), lambda c: (c, 0, 0, 0))
    return pl.pallas_call(
        body, name=name, grid=(nc // n,), in_specs=[st, st, ck, ck], out_specs=[ck, st],
        out_shape=[_sds((NH, t, HN)), _sds((nc, NH, HN, HN))],
        scratch_shapes=[pltpu.VMEM((NH, HN, HN), f32)],
    )(a, g, q, yl)


def _r2_bwd(dy, q, s_all, a, name):
    nc = a.shape[0]
    t = q.shape[1]

    def body(dy_ref, q_ref, s_ref, a_ref, dq_ref, da_ref, dg_ref, ds):
        @pl.when(pl.program_id(0) == 0)
        def _():
            ds[...] = jnp.zeros_like(ds)
        dsn = ds[...]
        for j in reversed(range(n)):
            rows = pl.ds(j * CHUNK, CHUNK)
            s0 = s_ref[j]
            dyv = dy_ref[:, rows, :]
            dg_ref[j] = dsn
            da_ref[j] = mm(s0, dsn, "tn", R2M)
            dq_ref[:, rows, :] = mm(dyv, s0, "nn", R2M)
            dsn = mm(dsn, a_ref[j], "nt", R2M) + mm(dyv, q_ref[:, rows, :], "tn", R2M)
        ds[...] = dsn

    n = math.gcd(nc, R2_CHUNKS)
    nb = nc // n
    ck = pl.BlockSpec((NH, n * CHUNK, HN), lambda c: (0, nb - 1 - c, 0))
    st = pl.BlockSpec((n, NH, HN, HN), lambda c: (nb - 1 - c, 0, 0, 0))
    return pl.pallas_call(
        body, name=name, grid=(nb,), in_specs=[ck, ck, st, st], out_specs=[ck, st, st],
        out_shape=[_sds((NH, t, HN)), _sds((nc, NH, HN, HN)), _sds((nc, NH, HN, HN))],
        scratch_shapes=[pltpu.VMEM((NH, HN, HN), f32)],
    )(dy, q, s_all, a)


def _ln_silu(c, w, b):
    mu = jnp.mean(c, axis=-1, keepdims=True)
    cc = c - mu
    var = jnp.mean(cc * cc, axis=-1, keepdims=True)
    u = cc * lax.rsqrt(var + LN_EPS) * w + b
    return u * jax.nn.sigmoid(u)


def _glu_tile(pc):
    return pc[:, :DC] * jax.nn.sigmoid(pc[:, DC:])


def _fill_glu(s_ref, pc_ref, halo_ref, first, tm):
    s_ref[pl.ds(0, HALO), :] = jnp.where(first, 0.0, _glu_tile(halo_ref[...]))
    s_ref[pl.ds(HALO, tm), :] = _glu_tile(pc_ref[...])


def _phase_copies(sh):
    n = sh.shape[1] - 8
    for r in range(1, 8):
        sh[r, pl.ds(0, n), :] = sh[0, pl.ds(r, n), :]


def _rows_at(sh, o, n):
    return sh[o % 8, pl.ds(o - o % 8, n), :]


def _conv_fwd(pc, dw, cb, lnw, lnb, name):
    t = pc.shape[0]
    tm = _tile(t, TM_VPU)
    hb = tm // HALO

    def body(pc_ref, halo_ref, dw_ref, cb_ref, w_ref, b_ref, c_ref, o_ref, s):
        _fill_glu(s.at[0], pc_ref, halo_ref, pl.program_id(0) == 0, tm)
        _phase_copies(s)
        for r0 in range(0, tm, CONV_ROWS):
            rows = pl.ds(r0, CONV_ROWS)
            acc = _rows_at(s, r0 + HALO - CW + 1, CONV_ROWS) * dw_ref[pl.ds(0, 1), :]
            for j in range(1, CW):
                acc = acc + _rows_at(s, r0 + HALO - CW + 1 + j, CONV_ROWS) * dw_ref[pl.ds(j, 1), :]
            c = acc + cb_ref[...]
            c_ref[rows, :] = c
            o_ref[rows, :] = _ln_silu(c, w_ref[...], b_ref[...]).astype(bf16)

    return pl.pallas_call(
        body, name=name, grid=(t // tm,),
        in_specs=[pl.BlockSpec((tm, 2 * DC), lambda i: (i, 0)),
                  pl.BlockSpec((HALO, 2 * DC), lambda i: (jnp.maximum(i * hb - 1, 0), 0)),
                  _full((CW, DC)), _full((1, DC)), _full((1, DC)), _full((1, DC))],
        out_specs=[pl.BlockSpec((tm, DC), lambda i: (i, 0)), pl.BlockSpec((tm, DC), lambda i: (i, 0))],
        out_shape=[_sds((t, DC)), _sds((t, DC), bf16)],
        scratch_shapes=[pltpu.VMEM((8, HALO + tm, DC), f32)],
    )(pc, pc, dw, cb, lnw, lnb)


def _conv_bwd1(dcat, c, lnw, lnb, name):
    t = c.shape[0]
    tm = _tile(t, TM_VPU)

    def body(d_ref, c_ref, w_ref, b_ref, dc_ref, dw_ref, db_ref, dcb_ref):
        _, vjp = jax.vjp(_ln_silu, c_ref[...], w_ref[...], b_ref[...])
        dc, dw, db = vjp(d_ref[...])
        dc_ref[...] = dc
        dcb = jnp.sum(dc, axis=0, keepdims=True)
        for ref, gval in ((dw_ref, dw), (db_ref, db), (dcb_ref, dcb)):
            @pl.when(pl.program_id(0) == 0)
            def _(ref=ref, gval=gval):
                ref[...] = gval

            @pl.when(pl.program_id(0) != 0)
            def _(ref=ref, gval=gval):
                ref[...] += gval

    tk = pl.BlockSpec((tm, DC), lambda i: (i, 0))
    return pl.pallas_call(
        body, name=name, grid=(t // tm,),
        in_specs=[pl.BlockSpec((tm, DC), lambda i: (i, 1)), tk, _full((1, DC)), _full((1, DC))],
        out_specs=[tk] + [_full((1, DC))] * 3,
        out_shape=[_sds((t, DC))] + [_sds((1, DC))] * 3,
    )(dcat, c, lnw, lnb)


def _conv_bwd2(dc, pc, dw, name):
    t = pc.shape[0]
    tm = _tile(t, TM_VPU)
    hb = tm // HALO
    last_blk = t // HALO - 1

    def body(dc_ref, dn_ref, pc_ref, halo_ref, dw_ref, dpc_ref, ddw_ref, s, sd, acc):
        i = pl.program_id(0)
        _fill_glu(s.at[0], pc_ref, halo_ref, i == 0, tm)
        sd[0, pl.ds(0, tm), :] = dc_ref[...]
        sd[0, pl.ds(tm, HALO), :] = jnp.where(i == pl.num_programs(0) - 1, 0.0, dn_ref[...])
        _phase_copies(s)
        _phase_copies(sd)
        for r0 in range(0, tm, CONV_ROWS):
            rows = pl.ds(r0, CONV_ROWS)
            dcb = sd[0, rows, :]
            dglu = None
            for j in range(CW):
                term = _rows_at(sd, r0 + CW - 1 - j, CONV_ROWS) * dw_ref[pl.ds(j, 1), :]
                dglu = term if dglu is None else dglu + term
                part = dcb * _rows_at(s, r0 + HALO - CW + 1 + j, CONV_ROWS)
                part8 = part[0:8]
                for q in range(8, CONV_ROWS, 8):
                    part8 = part8 + part[q:q + 8]
                if r0 == 0:
                    acc[pl.ds(8 * j, 8), :] = part8
                else:
                    acc[pl.ds(8 * j, 8), :] += part8
            a = pc_ref[rows, :DC]
            sg = jax.nn.sigmoid(pc_ref[rows, DC:])
            dpc_ref[rows, :DC] = dglu * sg
            dpc_ref[rows, DC:] = dglu * a * sg * (1.0 - sg)
        rows = [jnp.sum(acc[pl.ds(8 * j, 8), :], axis=0, keepdims=True) for j in range(CW)]

        @pl.when(i == 0)
        def _():
            for j in range(CW):
                ddw_ref[pl.ds(j, 1), :] = rows[j]

        @pl.when(i != 0)
        def _():
            for j in range(CW):
                ddw_ref[pl.ds(j, 1), :] += rows[j]

    return pl.pallas_call(
        body, name=name, grid=(t // tm,),
        in_specs=[pl.BlockSpec((tm, DC), lambda i: (i, 0)),
                  pl.BlockSpec((HALO, DC), lambda i: (jnp.minimum((i + 1) * hb, last_blk), 0)),
                  pl.BlockSpec((tm, 2 * DC), lambda i: (i, 0)),
                  pl.BlockSpec((HALO, 2 * DC), lambda i: (jnp.maximum(i * hb - 1, 0), 0)),
                  _full((CW, DC))],
        out_specs=[pl.BlockSpec((tm, 2 * DC), lambda i: (i, 0)), _full((CW, DC))],
        out_shape=[_sds((t, 2 * DC)), _sds((CW, DC))],
        scratch_shapes=[pltpu.VMEM((8, HALO + tm, DC), f32), pltpu.VMEM((8, tm + HALO, DC), f32),
                        pltpu.VMEM((8 * CW, DC), f32)],
    )(dc, dc, pc, pc, dw)


def _adamw(w, g, m, v):
    m = ADAM_B1 * m + (1.0 - ADAM_B1) * g
    v = ADAM_B2 * v + (1.0 - ADAM_B2) * (g * g)
    m_hat = m / (1.0 - ADAM_B1 ** ADAM_STEP)
    v_hat = v / (1.0 - ADAM_B2 ** ADAM_STEP)
    delta = -ADAM_LR * (m_hat / (jnp.sqrt(v_hat) + ADAM_EPS) + ADAM_WD * w)
    return delta, m, v


def _adam_slots(w, slots, m, v, name):
    r, c = w.shape
    tr = next(cand for cand in (512, 352, 256, 128, r) if r % cand == 0)

    def body(w_ref, s_ref, m_ref, v_ref, g_ref, d_ref, nm_ref, nv_ref):
        g = s_ref[0].astype(f32)
        for k in range(1, NDEV):
            g = g + s_ref[k].astype(f32)
        delta, nm, nv = _adamw(w_ref[...], g, m_ref[...], v_ref[...])
        g_ref[...] = g
        d_ref[...] = delta
        nm_ref[...] = nm
        nv_ref[...] = nv

    blk = pl.BlockSpec((tr, c), lambda i: (i, 0))
    return pl.pallas_call(
        body, name=name, grid=(r // tr,),
        in_specs=[blk, pl.BlockSpec((NDEV, tr, c), lambda i: (0, i, 0)), blk, blk],
        out_specs=[blk] * 4, out_shape=[_sds((r, c))] * 4,
    )(w, slots, m, v)


def _sum_slots(slots, name):
    _, r, c = slots.shape

    def body(s_ref, o_ref):
        g = s_ref[0]
        for k in range(1, NDEV):
            g = g + s_ref[k]
        o_ref[...] = g

    return pl.pallas_call(body, name=name, in_specs=[_full((NDEV, r, c))], out_specs=_full((r, c)),
                          out_shape=_sds((r, c)), grid=(1,))(slots)


def _adam_vec(w, g, m, v, name):
    shp = w.shape

    def body(w_ref, g_ref, m_ref, v_ref, d_ref, nm_ref, nv_ref):
        delta, nm, nv = _adamw(w_ref[...], g_ref[...], m_ref[...], v_ref[...])
        d_ref[...] = delta
        nm_ref[...] = nm
        nv_ref[...] = nv

    return pl.pallas_call(body, name=name, grid=(1,), in_specs=[_full(shp)] * 4, out_specs=[_full(shp)] * 3,
                          out_shape=[_sds(shp)] * 3)(w, g, m, v)


def _pack(pieces, total):
    flat = []
    n = 0
    for p in pieces:
        p = p.reshape(-1)
        pad = (-p.shape[0]) % LANES
        if pad:
            p = jnp.pad(p, (0, pad))
        flat.append(p)
        n += p.shape[0]
    if total > n:
        flat.append(jnp.zeros((total - n,), f32))
    return jnp.concatenate(flat).reshape(total // LANES, LANES)


def _unpack(vec, shapes):
    flat = vec.reshape(-1)
    out, off = [], 0
    for s in shapes:
        n = math.prod(s)
        out.append(flat[off:off + n].reshape(s))
        off += n + (-n) % LANES
    return out


def _round_up(n, m):
    return (n + m - 1) // m * m


def kernel(x, ffn1_norm_pre, ffn1_norm_post, ffn1_w_gu, ffn1_w_down, mix_norm_pre, mix_norm_post, w_in, shift_mu, w_up, w0, a_up, a0, g_up, k_k, k_a, r_k, gn_w, gn_b, conv_dw, conv_b, conv_ln_w, conv_ln_b, w_out, ffn2_norm_pre, ffn2_norm_post, ffn2_w_gu, ffn2_w_down, loss_target, m_ffn1_norm_pre, m_ffn1_norm_post, m_ffn1_w_gu, m_ffn1_w_down, m_mix_norm_pre, m_mix_norm_post, m_w_in, m_shift_mu, m_w_up, m_w0, m_a_up, m_a0, m_g_up, m_k_k, m_k_a, m_r_k, m_gn_w, m_gn_b, m_conv_dw, m_conv_b, m_conv_ln_w, m_conv_ln_b, m_w_out, m_ffn2_norm_pre, m_ffn2_norm_post, m_ffn2_w_gu, m_ffn2_w_down, v_ffn1_norm_pre, v_ffn1_norm_post, v_ffn1_w_gu, v_ffn1_w_down, v_mix_norm_pre, v_mix_norm_post, v_w_in, v_shift_mu, v_w_up, v_w0, v_a_up, v_a0, v_g_up, v_k_k, v_k_a, v_r_k, v_gn_w, v_gn_b, v_conv_dw, v_conv_b, v_conv_ln_w, v_conv_ln_b, v_w_out, v_ffn2_norm_pre, v_ffn2_norm_post, v_ffn2_w_gu, v_ffn2_w_down):
    t = x.shape[1]
    x0 = x.reshape(t, D)
    tgt = loss_target.reshape(t, D)

    def shard(a):
        return a[0].astype(bf16)

    def shard_t(a):
        return jnp.swapaxes(a[0], 0, 1).astype(bf16)

    (wg1,) = _exchange([shard_t(ffn1_w_gu)], ["gather"], [SLOT42], "gather_wg1")

    def cols(a):
        return jnp.transpose(a, (1, 0, 2)).reshape(a.shape[1], NDEV * a.shape[2])

    lane = jnp.arange(DR, dtype=jnp.int32) // HN
    seg = (lane[:, None] == jnp.arange(LANES, dtype=jnp.int32)[None, :]).astype(f32)
    seg_t = seg.T
    rk_row = r_k.reshape(1, DR)
    post_params = (gn_w, gn_b, rk_row, seg, seg_t)

    (xn1, gu1, h1), (wd1, win_g) = _ffn_in_fwd(x0, ffn1_norm_pre, wg1, "ffn1_in_fwd", comm=(
        [shard(ffn1_w_down), shard_t(w_in)], ["gather"] * 2, [SLOT8] * 2))
    wd1 = wd1.reshape(4, FB, D)
    win_t = win_g.reshape(DIN, D)
    (f1, x1), (wout_g, wup_g, aup_g, gup_g, cdw_g) = _mm_norm_res_fwd(
        h1, wd1, x0, ffn1_norm_post, 0.5, "ffn1_out_fwd", comm=(
            [shard(w_out), w_up[0], a_up[0], g_up[0], conv_dw[0]], ["gather"] * 5, [SLOT8] * 5))
    wout_full = wout_g.reshape(1, D, D)
    wup_full, aup_full, gup_full, cdw_full = cols(wup_g), cols(aup_g), cols(gup_g), cols(cdw_g)
    zeros64 = jnp.zeros((HN, DR), f32)
    wup_pad = jnp.concatenate([wup_full, zeros64], axis=0)
    aup_pad = jnp.concatenate([zeros64, aup_full], axis=0)
    prep_params = (shift_mu, w0, a0, k_k, k_a, wup_pad, aup_pad, gup_full, seg, seg_t)
    hm, ps, pc = _norm_mm_fwd(x1, mix_norm_pre, win_t, "mix_in_fwd")
    rec_in, (wd2,) = _prep_fwd(ps, prep_params, "prep_fwd", comm=([shard(ffn2_w_down)], ["gather"], [SLOT8]))
    wd2 = wd2.reshape(4, FB, D)
    g_gate = rec_in[6]
    rec_in = rec_in[:6]
    (a_c, g_c, q_c, yl_c), (wg2,) = _r1_fwd(rec_in, "r1_fwd", comm=([shard_t(ffn2_w_gu)], ["gather"], [SLOT42]))
    y_rec, s_all = _r2_fwd(a_c, g_c, q_c, yl_c, "r2_fwd")
    c_conv, out_b = _conv_fwd(pc, cdw_full, conv_b, conv_ln_w, conv_ln_b, "conv_fwd")
    cat = _post_fwd(y_rec, rec_in[0], rec_in[2], rec_in[3], g_gate, out_b, post_params, "post_fwd")
    fm, x2 = _mm_norm_res_fwd(cat.reshape(1, t, D), wout_full, x1, mix_norm_post, 1.0, "mix_out_fwd")
    (xn2, gu2, h2), _ = _ffn_in_fwd(x2, ffn2_norm_pre, wg2, "ffn2_in_fwd")
    f2, dy, loss_part = _mm_norm_res_fwd(h2, wd2, x2, ffn2_norm_post, 0.5, "ffn2_out_fwd", tgt=tgt)

    (df2, dg_post2, dgu2), _ = _post_bwd_nt(dy, f2, ffn2_norm_post, wd2, 0.5, gu2, "ffn2_out_bwd")
    (dx2, dg_pre2), _ = _nt_pre_bwd(dgu2, wg2, x2, ffn2_norm_pre, dy, "ffn2_in_bwd")
    dwd2 = _wgrad(h2, df2.reshape(1, t, D), True, False, D, "ffn2_wd_grad")
    dwg2 = _wgrad(dgu2.reshape(8, t, FB), xn2.reshape(1, t, D), True, False, D, "ffn2_wg_grad")
    (dfm, dg_postm, dcat), _ = _post_bwd_nt(dx2, fm, mix_norm_post, wout_full, 1.0, None, "mix_out_bwd")
    dwout = _wgrad(cat.reshape(1, t, D), dfm.reshape(1, t, D), False, False, D, "wout_grad")
    dcat = dcat.reshape(t, D)
    (dy_rec, dr2, dkp2, dv2, dgate, dgn_w, dgn_b, drk), (s_wout,) = _post_bwd(
        dcat, y_rec, rec_in[0], rec_in[2], rec_in[3], g_gate, post_params, "post_bwd", comm=(
            [dwout.reshape(NDEV, D // NDEV, D)], ["scatter"], [SLOT8]))
    dq_c, da_c, dg_c = _r2_bwd(dy_rec, q_c, s_all, a_c, "r2_bwd")
    rec_grads, got = _r1_bwd(rec_in, da_c, dg_c, dq_c, dy_rec, "r1_bwd", comm=(
        [dwg2.reshape(4, 2, FB, D), dwd2.reshape(NDEV, F // NDEV, D)], ["scatter"] * 2, [SLOT42, SLOT8]))
    s_wg2, s_wd2 = got
    prep_out = _prep_bwd(ps, prep_params, rec_grads, (dr2, dkp2, dv2, dgate), "prep_bwd")
    dpsl, dw0, da0, dkk, dka, dwup_pad, daup_pad, dgup = prep_out
    dc_conv, dlnw, dlnb, dcb = _conv_bwd1(dcat, c_conv, conv_ln_w, conv_ln_b, "conv_bwd1")
    dpc, dcdw = _conv_bwd2(dc_conv, pc, cdw_full, "conv_bwd2")
    dp, dmu = _shift_bwd(dpsl, ps, shift_mu, dpc, "shift_bwd")
    (dx1, dg_prem), _ = _nt_pre_bwd(dp.reshape(1, 1, t, DIN), win_t.reshape(1, 1, DIN, D), x1, mix_norm_pre, dx2,
                                    "mix_in_bwd")
    dwin_s = _wgrad(dp.reshape(1, t, DIN), hm.reshape(1, t, D), False, False, D // 2, "win_grad").reshape(
        NDEV, DIN // NDEV, D)
    (df1, dg_post1, dgu1), (s_win,) = _post_bwd_nt(dx1, f1, ffn1_norm_post, wd1, 0.5, gu1, "ffn1_out_bwd", comm=(
        [dwin_s], ["scatter"], [SLOT8]))
    dwd1 = _wgrad(h1, df1.reshape(1, t, D), True, False, D, "ffn1_wd_grad")
    dwg1, (s_wd1,) = _wgrad(dgu1.reshape(8, t, FB), xn1.reshape(1, t, D), True, False, D, "ffn1_wg_grad", comm=(
        [dwd1.reshape(NDEV, F // NDEV, D)], ["scatter"], [SLOT8]))
    (dx0, dg_pre1), (s_wg1,) = _nt_pre_bwd(dgu1, wg1, x0, ffn1_norm_pre, dx1, "ffn1_in_bwd", comm=(
        [dwg1.reshape(4, 2, FB, D)], ["scatter"], [SLOT42]))

    rep_shapes = [(1, 1)] + [(1, D)] * 6 + [(1, DS)] + [(1, DR)] * 10
    rep_parts = [loss_part, dg_pre1, dg_post1, dg_prem, dg_postm, dg_pre2, dg_post2, dmu,
                 dw0, da0, dkk, dka, drk, dgn_w, dgn_b, dcb, dlnw, dlnb]
    n_rep = _round_up(sum(_round_up(math.prod(s), LANES) for s in rep_shapes), 8 * LANES)
    sh_shapes = [(HN, HN), (HN, HN), (LANES, HN), (CW, HN)]
    n_sh = _round_up(sum(_round_up(math.prod(s), LANES) for s in sh_shapes), 8 * LANES)
    sh_parts = [jnp.transpose(a.reshape(a.shape[0], NDEV, HN), (1, 0, 2)) for a in (dwup_pad[:HN], daup_pad[HN:], dgup, dcdw)]
    sh_vec = jnp.stack([_pack([a[e] for a in sh_parts], n_sh) for e in range(NDEV)])
    s_rep, s_sh = _exchange([_pack(rep_parts, n_rep), sh_vec], ["gather", "scatter"], [SLOT8, SLOT8], "exchange_small")

    res = {}
    for nm, w, s, m, v, transposed in (
            ("ffn1_w_gu", ffn1_w_gu, s_wg1.reshape(NDEV, FB, D), m_ffn1_w_gu, v_ffn1_w_gu, True),
            ("ffn1_w_down", ffn1_w_down, s_wd1, m_ffn1_w_down, v_ffn1_w_down, False),
            ("w_in", w_in, s_win, m_w_in, v_w_in, True),
            ("w_out", w_out, s_wout, m_w_out, v_w_out, False),
            ("ffn2_w_gu", ffn2_w_gu, s_wg2.reshape(NDEV, FB, D), m_ffn2_w_gu, v_ffn2_w_gu, True),
            ("ffn2_w_down", ffn2_w_down, s_wd2, m_ffn2_w_down, v_ffn2_w_down, False)):
        view = (lambda a: jnp.swapaxes(a[0], 0, 1)) if transposed else (lambda a: a[0])
        outs = _adam_slots(view(w), s, view(m), view(v), "adam_" + nm)
        res[nm] = [(jnp.swapaxes(o, 0, 1) if transposed else o)[None] for o in outs]

    gsum = _unpack(_sum_slots(s_rep, "sum_rep"), rep_shapes)
    gsh = _unpack(_sum_slots(s_sh, "sum_sharded"), sh_shapes)
    loss = gsum[0].reshape(())
    rep_names = ["ffn1_norm_pre", "ffn1_norm_post", "mix_norm_pre", "mix_norm_post", "ffn2_norm_pre", "ffn2_norm_post",
                 "shift_mu", "w0", "a0", "k_k", "k_a", "r_k", "gn_w", "gn_b", "conv_b", "conv_ln_w", "conv_ln_b"]
    shard_names = ["w_up", "a_up", "g_up", "conv_dw"]
    env = dict(locals())
    small_g = {n: gsum[1 + i] for i, n in enumerate(rep_names)}
    small_g["r_k"] = small_g["r_k"].reshape(1, NH, HN)
    for i, n in enumerate(shard_names):
        small_g[n] = gsh[i][None]
    names = rep_names + shard_names
    shapes = [env[n].shape for n in names]
    n_vec = _round_up(sum(_round_up(math.prod(s), LANES) for s in shapes), 8 * LANES)
    packed = [_pack([d[n] for n in names], n_vec) for d in
              ({n: env[n] for n in names}, small_g, {n: env["m_" + n] for n in names}, {n: env["v_" + n] for n in names})]
    d_vec, m_vec, v_vec = _adam_vec(*packed, "adam_small")
    for n, dl, nm_, nv_ in zip(names, _unpack(d_vec, shapes), _unpack(m_vec, shapes), _unpack(v_vec, shapes)):
        res[n] = [small_g[n], dl, nm_, nv_]

    order = ["ffn1_norm_pre", "ffn1_norm_post", "ffn1_w_gu", "ffn1_w_down", "mix_norm_pre", "mix_norm_post", "w_in",
             "shift_mu", "w_up", "w0", "a_up", "a0", "g_up", "k_k", "k_a", "r_k", "gn_w", "gn_b", "conv_dw", "conv_b",
             "conv_ln_w", "conv_ln_b", "w_out", "ffn2_norm_pre", "ffn2_norm_post", "ffn2_w_gu", "ffn2_w_down"]
    return (loss, dx0.reshape(1, t, D), *[res[n][0] for n in order], *[res[n][1] for n in order],
            *[res[n][2] for n in order], *[res[n][3] for n in order])
```

```python
import functools
import math

import jax
import jax.numpy as jnp
from jax import lax
from jax.experimental import pallas as pl
from jax.experimental.pallas import tpu as pltpu

f32 = jnp.float32
bf16 = jnp.bfloat16

D = 1024
F = 2816
FB = 704
DR = 512
DC = 512
NH = 8
HN = 64
DS = 1792
DIN = 2816
CW = 31
CHUNK = 64
R1_CHUNKS = 4
R2_CHUNKS = 4
NDEV = 8
RMS_EPS = 1e-6
GN_EPS = 64e-5
LN_EPS = 1e-5
DECAY_SCALE = math.exp(-0.5)
ADAM_LR, ADAM_B1, ADAM_B2, ADAM_EPS, ADAM_WD, ADAM_STEP = 0.001, 0.9, 0.999, 1e-08, 0.01, 10
LANES = 128
HALO = 32
CONV_ROWS = 32
TM_MXU = 1024
TM_SUB = 256
TM_VPU = 256

ANY = pl.BlockSpec(memory_space=pl.ANY)


def _full(shape):
    return pl.BlockSpec(shape, lambda *_: (0,) * len(shape))


def _sds(shape, dtype=f32):
    return jax.ShapeDtypeStruct(shape, dtype)


def _dot(a, b):
    return jnp.dot(a, b, preferred_element_type=f32)


def _dot_nt(a, b):
    return lax.dot_general(a, b, (((1,), (1,)), ((), ())), preferred_element_type=f32)


def _dot_tn(a, b):
    return lax.dot_general(a, b, (((0,), (0,)), ((), ())), preferred_element_type=f32)


def _terms(a, n):
    out, rem = [], a
    for i in range(n):
        t = rem.astype(bf16)
        out.append(t)
        if i + 1 < n:
            rem = rem - t.astype(f32)
    return out


def _mm_raw(a, b, kind, mode):
    nb = a.ndim - 2
    bd = tuple(range(nb))
    ca = nb if kind == "tn" else nb + 1
    cb = nb + 1 if kind == "nt" else nb
    dn = (((ca,), (cb,)), (bd, bd))
    pa, pb = mode[:2]
    ta, tb = _terms(a, pa), _terms(b, pb)
    acc = None
    for i in range(pa):
        for j in range(pb):
            if i + j < max(pa, pb):
                p = lax.dot_general(ta[i], tb[j], dn, preferred_element_type=f32)
                acc = p if acc is None else acc + p
    return acc


@functools.partial(jax.custom_vjp, nondiff_argnums=(2, 3))
def mm(a, b, kind, mode):
    return _mm_raw(a, b, kind, mode)


def _mm_fwd(a, b, kind, mode):
    return _mm_raw(a, b, kind, mode), (a, b)


def _ct_terms(keep):
    return EXACT_TERMS if keep == 1 else keep


def _mm_bwd(kind, mode, res, ct):
    a, b = res
    pa, pb = mode[:2]
    if len(mode) == 3:
        pa = pb = ca = cb = mode[2]
    elif mode == (1, 1):
        ca = cb = 1
    else:
        ca, cb = _ct_terms(pb), _ct_terms(pa)
    if kind == "nn":
        da = mm(ct, b, "nt", (ca, pb))
        db = mm(a, ct, "tn", (pa, cb))
    elif kind == "nt":
        da = mm(ct, b, "nn", (ca, pb))
        db = mm(ct, a, "tn", (cb, pa))
    else:
        da = mm(b, ct, "nt", (pb, ca))
        db = mm(a, ct, "nn", (pa, cb))
    return da, db


mm.defvjp(_mm_fwd, _mm_bwd)

X3 = (2, 2)
EXACT_TERMS = 2
EA = (1, EXACT_TERMS)
EB = (EXACT_TERMS, 1)
X1 = (1, 1)
RGM = (1, 1, 2)
RIM = X1
RWM = X1
RAM = X1
R2M = X1
LORA = X1


def _rms_fwd(x, g):
    inv = lax.rsqrt(jnp.mean(x * x, axis=-1, keepdims=True) + RMS_EPS)
    return x * inv * g


def _rms_bwd(x, g, dy):
    inv = lax.rsqrt(jnp.mean(x * x, axis=-1, keepdims=True) + RMS_EPS)
    xh = x * inv
    dxh = dy * g
    dg = jnp.sum(dy * xh, axis=0, keepdims=True)
    dx = inv * (dxh - xh * jnp.mean(dxh * xh, axis=-1, keepdims=True))
    return dx, dg


def _tile(t, want):
    return min(t, want)


NPEER = NDEV - 1


def _comm_out_shape(arrays, modes, slots):
    out = []
    for a, mode, (lead, _) in zip(arrays, modes, slots):
        shp = tuple(lead) + tuple(a.shape) if mode == "gather" else tuple(a.shape)
        out.append(pltpu.HBM(shp, a.dtype))
    return out


def _comm_sems(n):
    return [pltpu.SemaphoreType.DMA((n * NPEER,)), pltpu.SemaphoreType.DMA((n * NPEER,)), pltpu.SemaphoreType.DMA((n,))]


RELAYED = (3, 5, 7)


def _comm_copies(ins, outs, sems, modes, slots, want):
    n = len(ins)
    send_sems, recv_sems, loc_sems = sems
    x, y, c = lax.axis_index("x"), lax.axis_index("y"), lax.axis_index("c")
    me = 4 * x + 2 * y + c

    def peer_of(k):
        px = 1 - x if k & 4 else x
        py = 1 - y if k & 2 else y
        pc = 1 - c if k & 1 else c
        return (px, py, pc), 4 * px + 2 * py + pc

    def slot(i, dev):
        return outs[i].at[slots[i][1](dev)]

    if want == "local":
        return [pltpu.make_async_copy(ins[i] if modes[i] == "gather" else ins[i].at[slots[i][1](me)], slot(i, me),
                                      loc_sems.at[i]) for i in range(n)]
    sibling = peer_of(1)[0]
    out = []
    for k in range(1, NDEV):
        peer, pid = peer_of(k)
        for i in range(n):
            sem = dict(send_sem=send_sems.at[i * NPEER + k - 1], recv_sem=recv_sems.at[i * NPEER + k - 1],
                       device_id_type=pl.DeviceIdType.MESH)
            relayed = modes[i] == "gather" and k in RELAYED
            if want == "recv":
                src = ins[i] if modes[i] == "gather" else ins[i].at[slots[i][1](pid)]
                out.append(pltpu.make_async_remote_copy(src_ref=src, dst_ref=slot(i, pid), device_id=peer, **sem))
            elif want == "first" and not relayed:
                src = ins[i] if modes[i] == "gather" else ins[i].at[slots[i][1](pid)]
                out.append(pltpu.make_async_remote_copy(src_ref=src, dst_ref=slot(i, me), device_id=peer, **sem))
            elif want == "relay" and relayed:
                origin = peer_of(k - 1)[1]
                out.append(((k - 2) * n + i, pltpu.make_async_remote_copy(
                    src_ref=slot(i, origin), dst_ref=slot(i, origin), device_id=sibling, **sem)))
    return out


def _comm_start(ins, outs, sems, modes, slots):
    for cp in _comm_copies(ins, outs, sems, modes, slots, "local") + _comm_copies(ins, outs, sems, modes, slots, "first"):
        cp.start()


def _comm_wait(ins, outs, sems, modes, slots):
    recvs = _comm_copies(ins, outs, sems, modes, slots, "recv")
    relays = _comm_copies(ins, outs, sems, modes, slots, "relay")
    for idx, cp in relays:
        recvs[idx].wait_recv()
        cp.start()
    passed_on = {idx for idx, _ in relays}
    for idx, cp in enumerate(recvs):
        if idx not in passed_on:
            cp.wait_recv()
    for cp in _comm_copies(ins, outs, sems, modes, slots, "first") + [cp for _, cp in relays]:
        cp.wait_send()
    for cp in _comm_copies(ins, outs, sems, modes, slots, "local"):
        cp.wait()


def _exchange(arrays, modes, slots, name):
    n = len(arrays)

    def body(*refs):
        ins, outs, sems = refs[:n], refs[n:2 * n], refs[2 * n:]
        _comm_start(ins, outs, sems, modes, slots)
        _comm_wait(ins, outs, sems, modes, slots)

    return pl.pallas_call(
        body, name=name, out_shape=_comm_out_shape(arrays, modes, slots),
        in_specs=[ANY] * n, out_specs=[ANY] * n, scratch_shapes=_comm_sems(n),
    )(*arrays)


def _pcall(body, *, name, grid, in_specs, out_specs, out_shape, args, scratch_shapes=(), comm=None):
    if comm is None:
        return pl.pallas_call(body, name=name, grid=grid, in_specs=in_specs, out_specs=out_specs, out_shape=out_shape,
                              scratch_shapes=list(scratch_shapes))(*args), None
    arrays, modes, slots = comm
    n_in, n_out, n_scr, nc = len(args), len(out_shape), len(scratch_shapes), len(arrays)

    def hosted(*refs):
        a_in, c_in = refs[:n_in], refs[n_in:n_in + nc]
        o = n_in + nc
        a_out, c_out = refs[o:o + n_out], refs[o + n_out:o + n_out + nc]
        o += n_out + nc
        a_scr, sems = refs[o:o + n_scr], refs[o + n_scr:]
        first = pl.program_id(0) == 0
        last = pl.program_id(0) == grid[0] - 1
        for ax in range(1, len(grid)):
            first = jnp.logical_and(first, pl.program_id(ax) == 0)
            last = jnp.logical_and(last, pl.program_id(ax) == grid[ax] - 1)

        @pl.when(first)
        def _():
            _comm_start(c_in, c_out, sems, modes, slots)

        body(*a_in, *a_out, *a_scr)

        @pl.when(last)
        def _():
            _comm_wait(c_in, c_out, sems, modes, slots)

    res = pl.pallas_call(
        hosted, name=name, grid=grid, in_specs=list(in_specs) + [ANY] * nc, out_specs=list(out_specs) + [ANY] * nc,
        out_shape=list(out_shape) + _comm_out_shape(arrays, modes, slots),
        scratch_shapes=list(scratch_shapes) + _comm_sems(nc),
    )(*args, *arrays)
    return res[:n_out], res[n_out:]


SLOT8 = ((NDEV,), lambda d: (d,))
SLOT42 = ((4, 2), lambda d: (lax.rem(d, 4), lax.div(d, 4)))


def _ffn_in_fwd(x, g, wg4, name, comm=None):
    t = x.shape[0]
    tm = _tile(t, TM_MXU)
    sub = _tile(tm, TM_SUB)

    def body(x_ref, g_ref, w_ref, xn_ref, gu_ref, h_ref):
        for s in range(tm // sub):
            rows = pl.ds(s * sub, sub)
            xn = _rms_fwd(x_ref[rows, :], g_ref[...]).astype(bf16)
            xn_ref[rows, :] = xn
            gate = _dot_nt(xn, w_ref[0, 0])
            up = _dot_nt(xn, w_ref[0, 1])
            gu_ref[0, 0, rows, :] = gate.astype(bf16)
            gu_ref[0, 1, rows, :] = up.astype(bf16)
            h_ref[0, rows, :] = (gate * jax.nn.sigmoid(gate) * up).astype(bf16)

    return _pcall(
        body, name=name, grid=(t // tm, 4),
        in_specs=[pl.BlockSpec((tm, D), lambda i, e: (i, 0)), _full((1, D)),
                  pl.BlockSpec((1, 2, FB, D), lambda i, e: (e, 0, 0, 0))],
        out_specs=[pl.BlockSpec((tm, D), lambda i, e: (i, 0)),
                   pl.BlockSpec((1, 2, tm, FB), lambda i, e: (e, 0, i, 0)),
                   pl.BlockSpec((1, tm, FB), lambda i, e: (e, i, 0))],
        out_shape=[_sds((t, D), bf16), _sds((4, 2, t, FB), bf16), _sds((4, t, FB), bf16)],
        args=(x, g, wg4), comm=comm)


def _mm_norm_res_fwd(hb, wb, xres, g, scale, name, comm=None, tgt=None):
    e_n, t, k = hb.shape
    tm = _tile(t, TM_MXU)

    def body(h_ref, w_ref, x_ref, g_ref, *rest):
        f = _dot(h_ref[0], w_ref[0])
        for e in range(1, e_n):
            f = f + _dot(h_ref[e], w_ref[e])
        xnew = x_ref[...] + scale * _rms_fwd(f, g_ref[...])
        if tgt is None:
            f_ref, o_ref = rest
            o_ref[...] = xnew
        else:
            t_ref, f_ref, o_ref, l_ref = rest
            err = xnew - t_ref[...]
            o_ref[...] = err * (1.0 / D)
            part = 0.5 * jnp.sum(jnp.mean(err * err, axis=-1, keepdims=True), axis=0, keepdims=True)

            @pl.when(pl.program_id(0) == 0)
            def _():
                l_ref[...] = part

            @pl.when(pl.program_id(0) != 0)
            def _():
                l_ref[...] += part
        f_ref[...] = f

    tile = pl.BlockSpec((tm, D), lambda i: (i, 0))
    outs, got = _pcall(
        body, name=name, grid=(t // tm,),
        in_specs=[pl.BlockSpec((e_n, tm, k), lambda i: (0, i, 0)), _full((e_n, k, D)), tile, _full((1, D))]
        + ([] if tgt is None else [tile]),
        out_specs=[tile, tile] + ([] if tgt is None else [_full((1, 1))]),
        out_shape=[_sds((t, D)), _sds((t, D))] + ([] if tgt is None else [_sds((1, 1))]),
        args=(hb, wb, xres, g) + (() if tgt is None else (tgt,)), comm=comm)
    return outs if comm is None else (outs, got)


def _post_bwd_nt(dxn, f, g, wb, scale, gu, name, comm=None):
    e_n, k, _ = wb.shape
    t = f.shape[0]
    tm = _tile(t, TM_MXU)
    sub = _tile(tm, TM_SUB)
    swiglu = gu is not None

    def body(*refs):
        if swiglu:
            dx_ref, f_ref, g_ref, w_ref, gu_ref, df_ref, dg_ref, dh_ref, df_s = refs
        else:
            dx_ref, f_ref, g_ref, w_ref, df_ref, dg_ref, dh_ref, df_s = refs
        i, e = pl.program_id(0), pl.program_id(1)

        @pl.when(e == 0)
        def _():
            df, dg = _rms_bwd(f_ref[...], g_ref[...], scale * dx_ref[...])
            df_s[...] = df.astype(bf16)
            df_ref[...] = df_s[...]

            @pl.when(i == 0)
            def _():
                dg_ref[...] = dg

            @pl.when(i != 0)
            def _():
                dg_ref[...] += dg

        for s in range(tm // sub):
            rows = pl.ds(s * sub, sub)
            dh = _dot_nt(df_s[rows, :], w_ref[0])
            if swiglu:
                gate = gu_ref[0, 0, rows, :].astype(f32)
                up = gu_ref[0, 1, rows, :].astype(f32)
                sg = jax.nn.sigmoid(gate)
                dh_ref[0, 0, rows, :] = (dh * up * (sg * (1.0 + gate * (1.0 - sg)))).astype(bf16)
                dh_ref[0, 1, rows, :] = (dh * gate * sg).astype(bf16)
            else:
                dh_ref[0, rows, :] = dh

    in_specs = [pl.BlockSpec((tm, D), lambda i, e: (i, 0)), pl.BlockSpec((tm, D), lambda i, e: (i, 0)), _full((1, D)),
                pl.BlockSpec((1, k, D), lambda i, e: (e, 0, 0))]
    args = [dxn, f, g, wb]
    if swiglu:
        in_specs.append(pl.BlockSpec((1, 2, tm, k), lambda i, e: (e, 0, i, 0)))
        args.append(gu)
        dh_spec = pl.BlockSpec((1, 2, tm, k), lambda i, e: (e, 0, i, 0))
        dh_shape = _sds((e_n, 2, t, k), bf16)
    else:
        dh_spec = pl.BlockSpec((1, tm, k), lambda i, e: (e, i, 0))
        dh_shape = _sds((e_n, t, k), f32)
    return _pcall(
        body, name=name, grid=(t // tm, e_n), in_specs=in_specs,
        out_specs=[pl.BlockSpec((tm, D), lambda i, e: (i, 0)), _full((1, D)), dh_spec],
        out_shape=[_sds((t, D), bf16), _sds((1, D)), dh_shape],
        scratch_shapes=[pltpu.VMEM((tm, D), bf16)], args=args, comm=comm)


def _nt_pre_bwd(dy, wb, x, g, dres, name, comm=None):
    e_n, q_n, t, k = dy.shape
    tm = _tile(t, TM_MXU)

    def body(dy_ref, w_ref, x_ref, g_ref, r_ref, dx_ref, dg_ref, acc):
        i, e = pl.program_id(0), pl.program_id(1)
        p = _dot(dy_ref[0, 0], w_ref[0, 0])
        for q in range(1, q_n):
            p = p + _dot(dy_ref[0, q], w_ref[0, q])

        @pl.when(e == 0)
        def _():
            acc[...] = p

        @pl.when(e != 0)
        def _():
            acc[...] += p

        @pl.when(e == e_n - 1)
        def _():
            dx, dg = _rms_bwd(x_ref[...], g_ref[...], acc[...])
            dx_ref[...] = r_ref[...] + dx

            @pl.when(i == 0)
            def _():
                dg_ref[...] = dg

            @pl.when(i != 0)
            def _():
                dg_ref[...] += dg

    return _pcall(
        body, name=name, grid=(t // tm, e_n),
        in_specs=[pl.BlockSpec((1, q_n, tm, k), lambda i, e: (e, 0, i, 0)),
                  pl.BlockSpec((1, q_n, k, D), lambda i, e: (e, 0, 0, 0)),
                  pl.BlockSpec((tm, D), lambda i, e: (i, 0)), _full((1, D)),
                  pl.BlockSpec((tm, D), lambda i, e: (i, 0))],
        out_specs=[pl.BlockSpec((tm, D), lambda i, e: (i, 0)), _full((1, D))],
        out_shape=[_sds((t, D)), _sds((1, D))],
        scratch_shapes=[pltpu.VMEM((tm, D), f32)], args=(dy, wb, x, g, dres), comm=comm)


def _wgrad(a, b, a_batched, b_batched, tn, name, comm=None):
    t, k = a.shape[1], a.shape[2]
    nn = b.shape[2]
    nb = max(a.shape[0], b.shape[0])
    tt = _tile(t, TM_MXU)
    nt = t // tt

    def body(a_ref, b_ref, o_ref, acc):
        @pl.when(pl.program_id(2) == 0)
        def _():
            acc[...] = _dot_tn(a_ref[0], b_ref[0])

        @pl.when(pl.program_id(2) != 0)
        def _():
            acc[...] += _dot_tn(a_ref[0], b_ref[0])

        @pl.when(pl.program_id(2) == nt - 1)
        def _():
            o_ref[0] = acc[...].astype(bf16)

    (out,), got = _pcall(
        body, name=name, grid=(nb, nn // tn, nt),
        in_specs=[pl.BlockSpec((1, tt, k), (lambda n, j, s: (n, s, 0)) if a_batched else (lambda n, j, s: (0, s, 0))),
                  pl.BlockSpec((1, tt, tn), (lambda n, j, s: (n, s, j)) if b_batched else (lambda n, j, s: (0, s, j)))],
        out_specs=[pl.BlockSpec((1, k, tn), lambda n, j, s: (n, 0, j))],
        out_shape=[_sds((nb, k, nn), bf16)],
        scratch_shapes=[pltpu.VMEM((k, tn), f32)], args=(a, b), comm=comm)
    return out if comm is None else (out, got)


def _norm_mm_fwd(x, g, w, name):
    t = x.shape[0]
    tm = _tile(t, TM_VPU)

    def body(x_ref, g_ref, w_ref, xn_ref, ps_ref, pc_ref):
        xn = _rms_fwd(x_ref[...], g_ref[...]).astype(bf16)
        xn_ref[...] = xn
        ps_ref[...] = _dot_nt(xn, w_ref[:DS, :])
        pc_ref[...] = _dot_nt(xn, w_ref[DS:, :])

    return pl.pallas_call(
        body, name=name, grid=(t // tm,),
        in_specs=[pl.BlockSpec((tm, D), lambda i: (i, 0)), _full((1, D)), _full((DIN, D))],
        out_specs=[pl.BlockSpec((tm, D), lambda i: (i, 0)), pl.BlockSpec((tm, DS), lambda i: (i, 0)),
                   pl.BlockSpec((tm, 2 * DC), lambda i: (i, 0))],
        out_shape=[_sds((t, D), bf16), _sds((t, DS)), _sds((t, 2 * DC))],
    )(x, g, w)


def _segsum_b(x, seg, seg_t):
    return mm(mm(x, seg, "nn", EB), seg_t, "nn", EB)


def _prep_fn(psl, w0, a0, k_k, k_a, wup, aup, gup, seg, seg_t):
    r, k, v = psl[:, :DR], psl[:, DR:2 * DR], psl[:, 2 * DR:3 * DR]
    xwa, xg = psl[:, 3 * DR:3 * DR + LANES], psl[:, 3 * DR + LANES:]
    d = w0 + mm(jnp.tanh(xwa), wup, "nn", LORA)
    lw = -DECAY_SCALE * jax.nn.sigmoid(d)
    a = jax.nn.sigmoid(a0 + mm(xwa, aup, "nn", LORA))
    g = mm(jax.nn.sigmoid(xg), gup, "nn", LORA)
    kkr = k * k_k
    kk = kkr * lax.rsqrt(jnp.maximum(_segsum_b(kkr * kkr, seg, seg_t), 1e-12))
    kp = k * (1.0 + (a - 1.0) * k_a)
    return r, lw, kp, v, -kk, kk * a, g


def _shifted(ps, halo_row, first):
    prev = jnp.where(first, 0.0, halo_row)
    sh = pltpu.roll(ps, 1, 0)
    row = lax.broadcasted_iota(jnp.int32, ps.shape, 0)
    return jnp.where(row == 0, prev, sh)


def _heads_of(ref, rows):
    return jnp.stack([ref[rows, h * HN:(h + 1) * HN] for h in range(NH)], axis=0)


def _heads_to(ref, rows, val):
    ref[rows, :] = jnp.concatenate([val[h] for h in range(NH)], axis=-1)


_PREP_PARAM_SHAPES = [(1, DS), (1, DR), (1, DR), (1, DR), (1, DR), (LANES, DR), (LANES, DR), (LANES, DR),
                      (DR, LANES), (LANES, DR)]


def _prep_fwd(ps, params, name, comm=None):
    t = ps.shape[0]
    tm = _tile(t, TM_VPU)
    hb = tm // 8

    def body(ps_ref, halo_ref, mu_ref, *rest):
        prm = [r[...] for r in rest[:9]]
        outs = rest[9:]
        x = ps_ref[...]
        sh = _shifted(x, halo_ref[pl.ds(7, 1), :], pl.program_id(0) == 0)
        psl = x + (sh - x) * mu_ref[...]
        vals = _prep_fn(psl, *prm)
        for ref, val in zip(outs, vals):
            ref[...] = val

    outs, got = _pcall(
        body, name=name, grid=(t // tm,),
        in_specs=[pl.BlockSpec((tm, DS), lambda i: (i, 0)),
                  pl.BlockSpec((8, DS), lambda i: (jnp.maximum(i * hb - 1, 0), 0))]
        + [_full(s) for s in _PREP_PARAM_SHAPES],
        out_specs=[pl.BlockSpec((tm, DR), lambda i: (i, 0))] * 7,
        out_shape=[_sds((t, DR))] * 7, args=(ps, ps, *params), comm=comm)
    return outs if comm is None else (outs, got)


def _prep_bwd(ps, params, cts_hm, cts_tm, name):
    t = ps.shape[0]
    tm = _tile(t, TM_VPU)
    hb = tm // 8

    def body(ps_ref, halo_ref, mu_ref, *rest):
        prm = [r[...] for r in rest[:9]]
        chm = rest[9:15]
        ctm = rest[15:19]
        dpsl_ref = rest[19]
        gouts = rest[20:27]
        x = ps_ref[...]
        sh = _shifted(x, halo_ref[pl.ds(7, 1), :], pl.program_id(0) == 0)
        psl = x + (sh - x) * mu_ref[...]
        seg, seg_t = prm[7], prm[8]
        _, vjp = jax.vjp(lambda p, *w: _prep_fn(p, *w, seg, seg_t), psl, *prm[:7])
        c = [r[...] for r in chm]
        cts = (c[0] + ctm[0][...], c[1], c[2] + ctm[1][...], c[3] + ctm[2][...], c[4], c[5], ctm[3][...])
        grads = vjp(cts)
        dpsl_ref[...] = grads[0]
        for ref, gval in zip(gouts, grads[1:]):
            @pl.when(pl.program_id(0) == 0)
            def _(ref=ref, gval=gval):
                ref[...] = gval

            @pl.when(pl.program_id(0) != 0)
            def _(ref=ref, gval=gval):
                ref[...] += gval

    tk = pl.BlockSpec((tm, DR), lambda i: (i, 0))
    gshapes = _PREP_PARAM_SHAPES[1:8]
    return pl.pallas_call(
        body, name=name, grid=(t // tm,),
        in_specs=[pl.BlockSpec((tm, DS), lambda i: (i, 0)),
                  pl.BlockSpec((8, DS), lambda i: (jnp.maximum(i * hb - 1, 0), 0))]
        + [_full(s) for s in _PREP_PARAM_SHAPES] + [tk] * 10,
        out_specs=[pl.BlockSpec((tm, DS), lambda i: (i, 0))] + [_full(s) for s in gshapes],
        out_shape=[_sds((t, DS))] + [_sds(s) for s in gshapes],
    )(ps, ps, *params, *cts_hm, *cts_tm)


def _shift_bwd(dpsl, ps, mu, dpc, name):
    t = ps.shape[0]
    tm = _tile(t, TM_VPU)
    hb = tm // 8
    last_blk = t // 8 - 1

    def body(d_ref, dn_ref, ps_ref, halo_ref, mu_ref, dpc_ref, dp_ref, dmu_ref):
        i = pl.program_id(0)
        mu_v = mu_ref[...]
        d = d_ref[...]
        nxt = jnp.where(i == pl.num_programs(0) - 1, 0.0, dn_ref[pl.ds(0, 1), :])
        up = pltpu.roll(d, tm - 1, 0)
        row = lax.broadcasted_iota(jnp.int32, d.shape, 0)
        up = jnp.where(row == tm - 1, nxt, up)
        dp_ref[:, :DS] = (d * (1.0 - mu_v) + up * mu_v).astype(bf16)
        dp_ref[:, DS:] = dpc_ref[...].astype(bf16)
        x = ps_ref[...]
        sh = _shifted(x, halo_ref[pl.ds(7, 1), :], i == 0)
        dmu = jnp.sum(d * (sh - x), axis=0, keepdims=True)

        @pl.when(i == 0)
        def _():
            dmu_ref[...] = dmu

        @pl.when(i != 0)
        def _():
            dmu_ref[...] += dmu

    return pl.pallas_call(
        body, name=name, grid=(t // tm,),
        in_specs=[pl.BlockSpec((tm, DS), lambda i: (i, 0)),
                  pl.BlockSpec((8, DS), lambda i: (jnp.minimum((i + 1) * hb, last_blk), 0)),
                  pl.BlockSpec((tm, DS), lambda i: (i, 0)),
                  pl.BlockSpec((8, DS), lambda i: (jnp.maximum(i * hb - 1, 0), 0)),
                  _full((1, DS)), pl.BlockSpec((tm, 2 * DC), lambda i: (i, 0))],
        out_specs=[pl.BlockSpec((tm, DIN), lambda i: (i, 0)), _full((1, DS))],
        out_shape=[_sds((t, DIN), bf16), _sds((1, DS))],
    )(dpsl, dpsl, ps, ps, mu, dpc)


def _post_fn(y, r, kp, v, g, gn_w, gn_b, r_k, seg, seg_t):
    mu = _segsum_b(y, seg, seg_t) * (1.0 / HN)
    yc = y - mu
    var = _segsum_b(yc * yc, seg, seg_t) * (1.0 / HN)
    yo = yc * lax.rsqrt(var + GN_EPS) * gn_w + gn_b
    bonus = _segsum_b(r * kp * r_k, seg, seg_t) * v
    return (yo + bonus) * g


_POST_PARAM_SHAPES = [(1, DR), (1, DR), (1, DR), (DR, LANES), (LANES, DR)]


def _post_fwd(y, r, kp, v, g, out_b, params, name):
    t = g.shape[0]
    tm = _tile(t, TM_VPU)

    def body(y_ref, r_ref, k_ref, v_ref, g_ref, ob_ref, *rest):
        prm = [p[...] for p in rest[:5]]
        cat_ref = rest[5]
        oa = _post_fn(y_ref[...], r_ref[...], k_ref[...], v_ref[...], g_ref[...], *prm)
        cat_ref[:, :DR] = oa.astype(bf16)
        cat_ref[:, DR:] = ob_ref[...]

    tk = pl.BlockSpec((tm, DR), lambda i: (i, 0))
    return pl.pallas_call(
        body, name=name, grid=(t // tm,),
        in_specs=[tk] * 6 + [_full(s) for s in _POST_PARAM_SHAPES],
        out_specs=pl.BlockSpec((tm, 2 * DR), lambda i: (i, 0)),
        out_shape=_sds((t, 2 * DR), bf16),
    )(y, r, kp, v, g, out_b, *params)


def _post_bwd(dcat, y, r, kp, v, g, params, name, comm=None):
    t = g.shape[0]
    tm = _tile(t, TM_VPU)

    def body(dc_ref, y_ref, r_ref, k_ref, v_ref, g_ref, *rest):
        prm = [p[...] for p in rest[:5]]
        dy_ref, dr_ref, dk_ref, dv_ref, dg_ref = rest[5:10]
        gouts = rest[10:13]
        seg, seg_t = prm[3], prm[4]
        _, vjp = jax.vjp(lambda *a: _post_fn(*a, seg, seg_t),
                         y_ref[...], r_ref[...], k_ref[...], v_ref[...], g_ref[...], *prm[:3])
        grads = vjp(dc_ref[...])
        dy_ref[...] = grads[0]
        dr_ref[...] = grads[1]
        dk_ref[...] = grads[2]
        dv_ref[...] = grads[3]
        dg_ref[...] = grads[4]
        for ref, gval in zip(gouts, grads[5:]):
            @pl.when(pl.program_id(0) == 0)
            def _(ref=ref, gval=gval):
                ref[...] = gval

            @pl.when(pl.program_id(0) != 0)
            def _(ref=ref, gval=gval):
                ref[...] += gval

    tk = pl.BlockSpec((tm, DR), lambda i: (i, 0))
    return _pcall(
        body, name=name, grid=(t // tm,),
        in_specs=[tk] * 6 + [_full(s) for s in _POST_PARAM_SHAPES],
        out_specs=[tk] * 5 + [_full((1, DR))] * 3,
        out_shape=[_sds((t, DR))] * 5 + [_sds((1, DR))] * 3,
        args=(dcat, y, r, kp, v, g, *params), comm=comm)


@jax.custom_vjp
def _inv_unit_lower(lb):
    c = lb.shape[-1]
    row = lax.broadcasted_iota(jnp.int32, (c, c), 0)
    col = lax.broadcasted_iota(jnp.int32, (c, c), 1)
    p = (row == col).astype(f32)[None] + lb
    m = lb
    for _ in range(c.bit_length() - 2):
        m = mm(m, m, "nn", RIM)
        p = p + mm(m, p, "nn", RIM)
    return p


def _inv_unit_lower_fwd(lb):
    p = _inv_unit_lower(lb)
    return p, p


def _inv_unit_lower_bwd(p, ct):
    return (mm(mm(p, ct, "tn", RIM), p, "nt", RIM),)


_inv_unit_lower.defvjp(_inv_unit_lower_fwd, _inv_unit_lower_bwd)


def _r1_fn(r, lw, k, v, z, b):
    c = r.shape[1]
    row = lax.broadcasted_iota(jnp.int32, (c, c), 0)
    col = lax.broadcasted_iota(jnp.int32, (c, c), 1)
    incl = (row >= col)[None]
    strict = (row > col)[None]
    eye = (row == col).astype(f32)[None]
    tri = jnp.broadcast_to((row >= col).astype(f32)[None], (r.shape[0], c, c))
    cum = mm(tri, lw, "nn", EA)
    tot = jnp.sum(lw, axis=1, keepdims=True)
    zt = z * jnp.exp(cum - lw)
    rt = r * jnp.exp(cum)
    g_inv = jnp.exp(-cum)
    g_rem = jnp.exp(tot - cum)
    bt, kt = b * g_inv, k * g_inv
    bh, kh = b * g_rem, k * g_rem
    lb = jnp.where(strict, mm(zt, bt, "nt", RGM), 0.0)
    lk = jnp.where(strict, mm(zt, kt, "nt", RGM), 0.0)
    mb = jnp.where(incl, mm(rt, bt, "nt", RGM), 0.0)
    mk = jnp.where(incl, mm(rt, kt, "nt", RGM), 0.0)
    p = _inv_unit_lower(lb)
    w1 = mm(p, zt, "nn", RWM)
    w2 = mm(p, mm(lk, v, "nn", RWM), "nn", RWM)
    a = mm(w1, bh, "tn", RAM) + eye * jnp.exp(tot)
    g = mm(w2, bh, "tn", RAM) + mm(v, kh, "tn", RAM)
    q = rt + mm(mb, w1, "nn", RWM)
    yl = mm(mb, w2, "nn", RWM) + mm(mk, v, "nn", RWM)
    return a, g, q, yl


def _chunks_in(ref, n):
    return jnp.concatenate([_heads_of(ref, pl.ds(s * CHUNK, CHUNK)) for s in range(n)], axis=0)


def _chunks_out(ref, val, n):
    for s in range(n):
        _heads_to(ref, pl.ds(s * CHUNK, CHUNK), val[s * NH:(s + 1) * NH])


def _r1_fwd(ins, name, comm=None):
    t = ins[0].shape[0]
    n = R1_CHUNKS
    nc = t // CHUNK

    def body(r, lw, k, v, z, b, a_ref, g_ref, q_ref, yl_ref):
        a, g, q, yl = _r1_fn(*[_chunks_in(x, n) for x in (r, lw, k, v, z, b)])
        for s in range(n):
            a_ref[s] = a[s * NH:(s + 1) * NH]
            g_ref[s] = g[s * NH:(s + 1) * NH]
        _chunks_out(q_ref, q, n)
        _chunks_out(yl_ref, yl, n)

    ck = pl.BlockSpec((n * CHUNK, DR), lambda c: (c, 0))
    st = pl.BlockSpec((n, NH, HN, HN), lambda c: (c, 0, 0, 0))
    return _pcall(
        body, name=name, grid=(nc // n,), in_specs=[ck] * 6, out_specs=[st, st, ck, ck],
        out_shape=[_sds((nc, NH, HN, HN))] * 2 + [_sds((t, DR))] * 2, args=tuple(ins), comm=comm)


def _r1_bwd(ins, da, dg, dq, dyl, name, comm=None):
    t = ins[0].shape[0]
    n = R1_CHUNKS
    nc = t // CHUNK

    def body(r, lw, k, v, z, b, da_ref, dg_ref, dq_ref, dyl_ref, *outs):
        _, vjp = jax.vjp(_r1_fn, *[_chunks_in(x, n) for x in (r, lw, k, v, z, b)])
        cts = (jnp.concatenate([da_ref[s] for s in range(n)], axis=0),
               jnp.concatenate([dg_ref[s] for s in range(n)], axis=0), _chunks_in(dq_ref, n), _chunks_in(dyl_ref, n))
        for ref, gval in zip(outs, vjp(cts)):
            _chunks_out(ref, gval, n)

    ck = pl.BlockSpec((n * CHUNK, DR), lambda c: (c, 0))
    st = pl.BlockSpec((n, NH, HN, HN), lambda c: (c, 0, 0, 0))
    return _pcall(
        body, name=name, grid=(nc // n,), in_specs=[ck] * 6 + [st, st, ck, ck], out_specs=[ck] * 6,
        out_shape=[_sds((t, DR))] * 6, args=(*ins, da, dg, dq, dyl), comm=comm)


def _r2_fwd(a, g, q, yl, name):
    nc = a.shape[0]
    t = q.shape[0]

    def body(a_ref, g_ref, q_ref, yl_ref, y_ref, s_ref, s):
        @pl.when(pl.program_id(0) == 0)
        def _():
            s[...] = jnp.zeros_like(s)
        s0 = s[...]
        for j in range(n):
            rows = pl.ds(j * CHUNK, CHUNK)
            s_ref[j] = s0
            _heads_to(y_ref, rows, mm(_heads_of(q_ref, rows), s0, "nt", R2M) + _heads_of(yl_ref, rows))
            s0 = mm(s0, a_ref[j], "nn", R2M) + g_ref[j]
        s[...] = s0

    n = math.gcd(nc, R2_CHUNKS)
    ck = pl.BlockSpec((n * CHUNK, DR), lambda c: (c, 0))
    st = pl.BlockSpec((n, NH, HN, HN), lambda c: (c, 0, 0, 0))
    return pl.pallas_call(
        body, name=name, grid=(nc // n,), in_specs=[st, st, ck, ck], out_specs=[ck, st],
        out_shape=[_sds((t, DR)), _sds((nc, NH, HN, HN))],
        scratch_shapes=[pltpu.VMEM((NH, HN, HN), f32)],
    )(a, g, q, yl)


def _r2_bwd(dy, q, s_all, a, name):
    nc = a.shape[0]
    t = q.shape[0]

    def body(dy_ref, q_ref, s_ref, a_ref, dq_ref, da_ref, dg_ref, ds):
        @pl.when(pl.program_id(0) == 0)
        def _():
            ds[...] = jnp.zeros_like(ds)
        dsn = ds[...]
        for j in reversed(range(n)):
            rows = pl.ds(j * CHUNK, CHUNK)
            s0 = s_ref[j]
            dyv = _heads_of(dy_ref, rows)
            dg_ref[j] = dsn
            da_ref[j] = mm(s0, dsn, "tn", R2M)
            _heads_to(dq_ref, rows, mm(dyv, s0, "nn", R2M))
            dsn = mm(dsn, a_ref[j], "nt", R2M) + mm(dyv, _heads_of(q_ref, rows), "tn", R2M)
        ds[...] = dsn

    n = math.gcd(nc, R2_CHUNKS)
    nb = nc // n
    ck = pl.BlockSpec((n * CHUNK, DR), lambda c: (nb - 1 - c, 0))
    st = pl.BlockSpec((n, NH, HN, HN), lambda c: (nb - 1 - c, 0, 0, 0))
    return pl.pallas_call(
        body, name=name, grid=(nb,), in_specs=[ck, ck, st, st], out_specs=[ck, st, st],
        out_shape=[_sds((t, DR)), _sds((nc, NH, HN, HN)), _sds((nc, NH, HN, HN))],
        scratch_shapes=[pltpu.VMEM((NH, HN, HN), f32)],
    )(dy, q, s_all, a)


def _ln_silu(c, w, b):
    mu = jnp.mean(c, axis=-1, keepdims=True)
    cc = c - mu
    var = jnp.mean(cc * cc, axis=-1, keepdims=True)
    u = cc * lax.rsqrt(var + LN_EPS) * w + b
    return u * jax.nn.sigmoid(u)


def _glu_tile(pc):
    return pc[:, :DC] * jax.nn.sigmoid(pc[:, DC:])


def _fill_glu(s_ref, pc_ref, halo_ref, first, tm):
    s_ref[pl.ds(0, HALO), :] = jnp.where(first, 0.0, _glu_tile(halo_ref[...]))
    s_ref[pl.ds(HALO, tm), :] = _glu_tile(pc_ref[...])


def _phase_copies(sh):
    n = sh.shape[1] - 8
    for r in range(1, 8):
        sh[r, pl.ds(0, n), :] = sh[0, pl.ds(r, n), :]


def _rows_at(sh, o, n):
    return sh[o % 8, pl.ds(o - o % 8, n), :]


def _conv_fwd(pc, dw, cb, lnw, lnb, name):
    t = pc.shape[0]
    tm = _tile(t, TM_VPU)
    hb = tm // HALO

    def body(pc_ref, halo_ref, dw_ref, cb_ref, w_ref, b_ref, c_ref, o_ref, s):
        _fill_glu(s.at[0], pc_ref, halo_ref, pl.program_id(0) == 0, tm)
        _phase_copies(s)
        for r0 in range(0, tm, CONV_ROWS):
            rows = pl.ds(r0, CONV_ROWS)
            acc = _rows_at(s, r0 + HALO - CW + 1, CONV_ROWS) * dw_ref[pl.ds(0, 1), :]
            for j in range(1, CW):
                acc = acc + _rows_at(s, r0 + HALO - CW + 1 + j, CONV_ROWS) * dw_ref[pl.ds(j, 1), :]
            c = acc + cb_ref[...]
            c_ref[rows, :] = c
            o_ref[rows, :] = _ln_silu(c, w_ref[...], b_ref[...]).astype(bf16)

    return pl.pallas_call(
        body, name=name, grid=(t // tm,),
        in_specs=[pl.BlockSpec((tm, 2 * DC), lambda i: (i, 0)),
                  pl.BlockSpec((HALO, 2 * DC), lambda i: (jnp.maximum(i * hb - 1, 0), 0)),
                  _full((CW, DC)), _full((1, DC)), _full((1, DC)), _full((1, DC))],
        out_specs=[pl.BlockSpec((tm, DC), lambda i: (i, 0)), pl.BlockSpec((tm, DC), lambda i: (i, 0))],
        out_shape=[_sds((t, DC)), _sds((t, DC), bf16)],
        scratch_shapes=[pltpu.VMEM((8, HALO + tm, DC), f32)],
    )(pc, pc, dw, cb, lnw, lnb)


def _conv_bwd1(dcat, c, lnw, lnb, name):
    t = c.shape[0]
    tm = _tile(t, TM_VPU)

    def body(d_ref, c_ref, w_ref, b_ref, dc_ref, dw_ref, db_ref, dcb_ref):
        _, vjp = jax.vjp(_ln_silu, c_ref[...], w_ref[...], b_ref[...])
        dc, dw, db = vjp(d_ref[...])
        dc_ref[...] = dc
        dcb = jnp.sum(dc, axis=0, keepdims=True)
        for ref, gval in ((dw_ref, dw), (db_ref, db), (dcb_ref, dcb)):
            @pl.when(pl.program_id(0) == 0)
            def _(ref=ref, gval=gval):
                ref[...] = gval

            @pl.when(pl.program_id(0) != 0)
            def _(ref=ref, gval=gval):
                ref[...] += gval

    tk = pl.BlockSpec((tm, DC), lambda i: (i, 0))
    return pl.pallas_call(
        body, name=name, grid=(t // tm,),
        in_specs=[pl.BlockSpec((tm, DC), lambda i: (i, 1)), tk, _full((1, DC)), _full((1, DC))],
        out_specs=[tk] + [_full((1, DC))] * 3,
        out_shape=[_sds((t, DC))] + [_sds((1, DC))] * 3,
    )(dcat, c, lnw, lnb)


def _conv_bwd2(dc, pc, dw, name):
    t = pc.shape[0]
    tm = _tile(t, TM_VPU)
    hb = tm // HALO
    last_blk = t // HALO - 1

    def body(dc_ref, dn_ref, pc_ref, halo_ref, dw_ref, dpc_ref, ddw_ref, s, sd, acc):
        i = pl.program_id(0)
        _fill_glu(s.at[0], pc_ref, halo_ref, i == 0, tm)
        sd[0, pl.ds(0, tm), :] = dc_ref[...]
        sd[0, pl.ds(tm, HALO), :] = jnp.where(i == pl.num_programs(0) - 1, 0.0, dn_ref[...])
        _phase_copies(s)
        _phase_copies(sd)
        for r0 in range(0, tm, CONV_ROWS):
            rows = pl.ds(r0, CONV_ROWS)
            dcb = sd[0, rows, :]
            dglu = None
            for j in range(CW):
                term = _rows_at(sd, r0 + CW - 1 - j, CONV_ROWS) * dw_ref[pl.ds(j, 1), :]
                dglu = term if dglu is None else dglu + term
                part = dcb * _rows_at(s, r0 + HALO - CW + 1 + j, CONV_ROWS)
                part8 = part[0:8]
                for q in range(8, CONV_ROWS, 8):
                    part8 = part8 + part[q:q + 8]
                if r0 == 0:
                    acc[pl.ds(8 * j, 8), :] = part8
                else:
                    acc[pl.ds(8 * j, 8), :] += part8
            a = pc_ref[rows, :DC]
            sg = jax.nn.sigmoid(pc_ref[rows, DC:])
            dpc_ref[rows, :DC] = dglu * sg
            dpc_ref[rows, DC:] = dglu * a * sg * (1.0 - sg)
        rows = [jnp.sum(acc[pl.ds(8 * j, 8), :], axis=0, keepdims=True) for j in range(CW)]

        @pl.when(i == 0)
        def _():
            for j in range(CW):
                ddw_ref[pl.ds(j, 1), :] = rows[j]

        @pl.when(i != 0)
        def _():
            for j in range(CW):
                ddw_ref[pl.ds(j, 1), :] += rows[j]

    return pl.pallas_call(
        body, name=name, grid=(t // tm,),
        in_specs=[pl.BlockSpec((tm, DC), lambda i: (i, 0)),
                  pl.BlockSpec((HALO, DC), lambda i: (jnp.minimum((i + 1) * hb, last_blk), 0)),
                  pl.BlockSpec((tm, 2 * DC), lambda i: (i, 0)),
                  pl.BlockSpec((HALO, 2 * DC), lambda i: (jnp.maximum(i * hb - 1, 0), 0)),
                  _full((CW, DC))],
        out_specs=[pl.BlockSpec((tm, 2 * DC), lambda i: (i, 0)), _full((CW, DC))],
        out_shape=[_sds((t, 2 * DC)), _sds((CW, DC))],
        scratch_shapes=[pltpu.VMEM((8, HALO + tm, DC), f32), pltpu.VMEM((8, tm + HALO, DC), f32),
                        pltpu.VMEM((8 * CW, DC), f32)],
    )(dc, dc, pc, pc, dw)


def _adamw(w, g, m, v):
    m = ADAM_B1 * m + (1.0 - ADAM_B1) * g
    v = ADAM_B2 * v + (1.0 - ADAM_B2) * (g * g)
    m_hat = m / (1.0 - ADAM_B1 ** ADAM_STEP)
    v_hat = v / (1.0 - ADAM_B2 ** ADAM_STEP)
    delta = -ADAM_LR * (m_hat / (jnp.sqrt(v_hat) + ADAM_EPS) + ADAM_WD * w)
    return delta, m, v


def _adam_slots(w, slots, m, v, name):
    r, c = w.shape
    tr = next(cand for cand in (512, 352, 256, 128, r) if r % cand == 0)

    def body(w_ref, s_ref, m_ref, v_ref, g_ref, d_ref, nm_ref, nv_ref):
        g = s_ref[0].astype(f32)
        for k in range(1, NDEV):
            g = g + s_ref[k].astype(f32)
        delta, nm, nv = _adamw(w_ref[...], g, m_ref[...], v_ref[...])
        g_ref[...] = g
        d_ref[...] = delta
        nm_ref[...] = nm
        nv_ref[...] = nv

    blk = pl.BlockSpec((tr, c), lambda i: (i, 0))
    return pl.pallas_call(
        body, name=name, grid=(r // tr,),
        in_specs=[blk, pl.BlockSpec((NDEV, tr, c), lambda i: (0, i, 0)), blk, blk],
        out_specs=[blk] * 4, out_shape=[_sds((r, c))] * 4,
    )(w, slots, m, v)


def _sum_slots(slots, name):
    _, r, c = slots.shape

    def body(s_ref, o_ref):
        g = s_ref[0]
        for k in range(1, NDEV):
            g = g + s_ref[k]
        o_ref[...] = g

    return pl.pallas_call(body, name=name, in_specs=[_full((NDEV, r, c))], out_specs=_full((r, c)),
                          out_shape=_sds((r, c)), grid=(1,))(slots)


def _adam_vec(w, g, m, v, name):
    shp = w.shape

    def body(w_ref, g_ref, m_ref, v_ref, d_ref, nm_ref, nv_ref):
        delta, nm, nv = _adamw(w_ref[...], g_ref[...], m_ref[...], v_ref[...])
        d_ref[...] = delta
        nm_ref[...] = nm
        nv_ref[...] = nv

    return pl.pallas_call(body, name=name, grid=(1,), in_specs=[_full(shp)] * 4, out_specs=[_full(shp)] * 3,
                          out_shape=[_sds(shp)] * 3)(w, g, m, v)


def _pack(pieces, total):
    flat = []
    n = 0
    for p in pieces:
        p = p.reshape(-1)
        pad = (-p.shape[0]) % LANES
        if pad:
            p = jnp.pad(p, (0, pad))
        flat.append(p)
        n += p.shape[0]
    if total > n:
        flat.append(jnp.zeros((total - n,), f32))
    return jnp.concatenate(flat).reshape(total // LANES, LANES)


def _unpack(vec, shapes):
    flat = vec.reshape(-1)
    out, off = [], 0
    for s in shapes:
        n = math.prod(s)
        out.append(flat[off:off + n].reshape(s))
        off += n + (-n) % LANES
    return out


def _round_up(n, m):
    return (n + m - 1) // m * m


def kernel(x, ffn1_norm_pre, ffn1_norm_post, ffn1_w_gu, ffn1_w_down, mix_norm_pre, mix_norm_post, w_in, shift_mu, w_up, w0, a_up, a0, g_up, k_k, k_a, r_k, gn_w, gn_b, conv_dw, conv_b, conv_ln_w, conv_ln_b, w_out, ffn2_norm_pre, ffn2_norm_post, ffn2_w_gu, ffn2_w_down, loss_target, m_ffn1_norm_pre, m_ffn1_norm_post, m_ffn1_w_gu, m_ffn1_w_down, m_mix_norm_pre, m_mix_norm_post, m_w_in, m_shift_mu, m_w_up, m_w0, m_a_up, m_a0, m_g_up, m_k_k, m_k_a, m_r_k, m_gn_w, m_gn_b, m_conv_dw, m_conv_b, m_conv_ln_w, m_conv_ln_b, m_w_out, m_ffn2_norm_pre, m_ffn2_norm_post, m_ffn2_w_gu, m_ffn2_w_down, v_ffn1_norm_pre, v_ffn1_norm_post, v_ffn1_w_gu, v_ffn1_w_down, v_mix_norm_pre, v_mix_norm_post, v_w_in, v_shift_mu, v_w_up, v_w0, v_a_up, v_a0, v_g_up, v_k_k, v_k_a, v_r_k, v_gn_w, v_gn_b, v_conv_dw, v_conv_b, v_conv_ln_w, v_conv_ln_b, v_w_out, v_ffn2_norm_pre, v_ffn2_norm_post, v_ffn2_w_gu, v_ffn2_w_down):
    t = x.shape[1]
    x0 = x.reshape(t, D)
    tgt = loss_target.reshape(t, D)

    def shard(a):
        return a[0].astype(bf16)

    def shard_t(a):
        return jnp.swapaxes(a[0], 0, 1).astype(bf16)

    (wg1,) = _exchange([shard_t(ffn1_w_gu)], ["gather"], [SLOT42], "gather_wg1")

    def cols(a):
        return jnp.transpose(a, (1, 0, 2)).reshape(a.shape[1], NDEV * a.shape[2])

    lane = jnp.arange(DR, dtype=jnp.int32) // HN
    seg = (lane[:, None] == jnp.arange(LANES, dtype=jnp.int32)[None, :]).astype(f32)
    seg_t = seg.T
    rk_row = r_k.reshape(1, DR)
    post_params = (gn_w, gn_b, rk_row, seg, seg_t)

    (xn1, gu1, h1), (wd1, win_g) = _ffn_in_fwd(x0, ffn1_norm_pre, wg1, "ffn1_in_fwd", comm=(
        [shard(ffn1_w_down), shard_t(w_in)], ["gather"] * 2, [SLOT8] * 2))
    wd1 = wd1.reshape(4, FB, D)
    win_t = win_g.reshape(DIN, D)
    (f1, x1), (wout_g, wup_g, aup_g, gup_g, cdw_g) = _mm_norm_res_fwd(
        h1, wd1, x0, ffn1_norm_post, 0.5, "ffn1_out_fwd", comm=(
            [shard(w_out), w_up[0], a_up[0], g_up[0], conv_dw[0]], ["gather"] * 5, [SLOT8] * 5))
    wout_full = wout_g.reshape(1, D, D)
    wup_full, aup_full, gup_full, cdw_full = cols(wup_g), cols(aup_g), cols(gup_g), cols(cdw_g)
    zeros64 = jnp.zeros((HN, DR), f32)
    wup_pad = jnp.concatenate([wup_full, zeros64], axis=0)
    aup_pad = jnp.concatenate([zeros64, aup_full], axis=0)
    prep_params = (shift_mu, w0, a0, k_k, k_a, wup_pad, aup_pad, gup_full, seg, seg_t)
    hm, ps, pc = _norm_mm_fwd(x1, mix_norm_pre, win_t, "mix_in_fwd")
    rec_in, (wd2,) = _prep_fwd(ps, prep_params, "prep_fwd", comm=([shard(ffn2_w_down)], ["gather"], [SLOT8]))
    wd2 = wd2.reshape(4, FB, D)
    g_gate = rec_in[6]
    rec_in = rec_in[:6]
    (a_c, g_c, q_c, yl_c), (wg2,) = _r1_fwd(rec_in, "r1_fwd", comm=([shard_t(ffn2_w_gu)], ["gather"], [SLOT42]))
    y_rec, s_all = _r2_fwd(a_c, g_c, q_c, yl_c, "r2_fwd")
    c_conv, out_b = _conv_fwd(pc, cdw_full, conv_b, conv_ln_w, conv_ln_b, "conv_fwd")
    cat = _post_fwd(y_rec, rec_in[0], rec_in[2], rec_in[3], g_gate, out_b, post_params, "post_fwd")
    fm, x2 = _mm_norm_res_fwd(cat.reshape(1, t, D), wout_full, x1, mix_norm_post, 1.0, "mix_out_fwd")
    (xn2, gu2, h2), _ = _ffn_in_fwd(x2, ffn2_norm_pre, wg2, "ffn2_in_fwd")
    f2, dy, loss_part = _mm_norm_res_fwd(h2, wd2, x2, ffn2_norm_post, 0.5, "ffn2_out_fwd", tgt=tgt)

    (df2, dg_post2, dgu2), _ = _post_bwd_nt(dy, f2, ffn2_norm_post, wd2, 0.5, gu2, "ffn2_out_bwd")
    (dx2, dg_pre2), _ = _nt_pre_bwd(dgu2, wg2, x2, ffn2_norm_pre, dy, "ffn2_in_bwd")
    dwd2 = _wgrad(h2, df2.reshape(1, t, D), True, False, D, "ffn2_wd_grad")
    dwg2 = _wgrad(dgu2.reshape(8, t, FB), xn2.reshape(1, t, D), True, False, D, "ffn2_wg_grad")
    (dfm, dg_postm, dcat), _ = _post_bwd_nt(dx2, fm, mix_norm_post, wout_full, 1.0, None, "mix_out_bwd")
    dwout = _wgrad(cat.reshape(1, t, D), dfm.reshape(1, t, D), False, False, D, "wout_grad")
    dcat = dcat.reshape(t, D)
    (dy_rec, dr2, dkp2, dv2, dgate, dgn_w, dgn_b, drk), (s_wout,) = _post_bwd(
        dcat, y_rec, rec_in[0], rec_in[2], rec_in[3], g_gate, post_params, "post_bwd", comm=(
            [dwout.reshape(NDEV, D // NDEV, D)], ["scatter"], [SLOT8]))
    dq_c, da_c, dg_c = _r2_bwd(dy_rec, q_c, s_all, a_c, "r2_bwd")
    rec_grads, got = _r1_bwd(rec_in, da_c, dg_c, dq_c, dy_rec, "r1_bwd", comm=(
        [dwg2.reshape(4, 2, FB, D), dwd2.reshape(NDEV, F // NDEV, D)], ["scatter"] * 2, [SLOT42, SLOT8]))
    s_wg2, s_wd2 = got
    prep_out = _prep_bwd(ps, prep_params, rec_grads, (dr2, dkp2, dv2, dgate), "prep_bwd")
    dpsl, dw0, da0, dkk, dka, dwup_pad, daup_pad, dgup = prep_out
    dc_conv, dlnw, dlnb, dcb = _conv_bwd1(dcat, c_conv, conv_ln_w, conv_ln_b, "conv_bwd1")
    dpc, dcdw = _conv_bwd2(dc_conv, pc, cdw_full, "conv_bwd2")
    dp, dmu = _shift_bwd(dpsl, ps, shift_mu, dpc, "shift_bwd")
    (dx1, dg_prem), _ = _nt_pre_bwd(dp.reshape(1, 1, t, DIN), win_t.reshape(1, 1, DIN, D), x1, mix_norm_pre, dx2,
                                    "mix_in_bwd")
    dwin_s = _wgrad(dp.reshape(1, t, DIN), hm.reshape(1, t, D), False, False, D // 2, "win_grad").reshape(
        NDEV, DIN // NDEV, D)
    (df1, dg_post1, dgu1), (s_win,) = _post_bwd_nt(dx1, f1, ffn1_norm_post, wd1, 0.5, gu1, "ffn1_out_bwd", comm=(
        [dwin_s], ["scatter"], [SLOT8]))
    dwd1 = _wgrad(h1, df1.reshape(1, t, D), True, False, D, "ffn1_wd_grad")
    dwg1, (s_wd1,) = _wgrad(dgu1.reshape(8, t, FB), xn1.reshape(1, t, D), True, False, D, "ffn1_wg_grad", comm=(
        [dwd1.reshape(NDEV, F // NDEV, D)], ["scatter"], [SLOT8]))
    (dx0, dg_pre1), (s_wg1,) = _nt_pre_bwd(dgu1, wg1, x0, ffn1_norm_pre, dx1, "ffn1_in_bwd", comm=(
        [dwg1.reshape(4, 2, FB, D)], ["scatter"], [SLOT42]))

    rep_shapes = [(1, 1)] + [(1, D)] * 6 + [(1, DS)] + [(1, DR)] * 10
    rep_parts = [loss_part, dg_pre1, dg_post1, dg_prem, dg_postm, dg_pre2, dg_post2, dmu,
                 dw0, da0, dkk, dka, drk, dgn_w, dgn_b, dcb, dlnw, dlnb]
    n_rep = _round_up(sum(_round_up(math.prod(s), LANES) for s in rep_shapes), 8 * LANES)
    sh_shapes = [(HN, HN), (HN, HN), (LANES, HN), (CW, HN)]
    n_sh = _round_up(sum(_round_up(math.prod(s), LANES) for s in sh_shapes), 8 * LANES)
    sh_parts = [jnp.transpose(a.reshape(a.shape[0], NDEV, HN), (1, 0, 2)) for a in (dwup_pad[:HN], daup_pad[HN:], dgup, dcdw)]
    sh_vec = jnp.stack([_pack([a[e] for a in sh_parts], n_sh) for e in range(NDEV)])
    s_rep, s_sh = _exchange([_pack(rep_parts, n_rep), sh_vec], ["gather", "scatter"], [SLOT8, SLOT8], "exchange_small")

    res = {}
    for nm, w, s, m, v, transposed in (
            ("ffn1_w_gu", ffn1_w_gu, s_wg1.reshape(NDEV, FB, D), m_ffn1_w_gu, v_ffn1_w_gu, True),
            ("ffn1_w_down", ffn1_w_down, s_wd1, m_ffn1_w_down, v_ffn1_w_down, False),
            ("w_in", w_in, s_win, m_w_in, v_w_in, True),
            ("w_out", w_out, s_wout, m_w_out, v_w_out, False),
            ("ffn2_w_gu", ffn2_w_gu, s_wg2.reshape(NDEV, FB, D), m_ffn2_w_gu, v_ffn2_w_gu, True),
            ("ffn2_w_down", ffn2_w_down, s_wd2, m_ffn2_w_down, v_ffn2_w_down, False)):
        view = (lambda a: jnp.swapaxes(a[0], 0, 1)) if transposed else (lambda a: a[0])
        outs = _adam_slots(view(w), s, view(m), view(v), "adam_" + nm)
        res[nm] = [(jnp.swapaxes(o, 0, 1) if transposed else o)[None] for o in outs]

    gsum = _unpack(_sum_slots(s_rep, "sum_rep"), rep_shapes)
    gsh = _unpack(_sum_slots(s_sh, "sum_sharded"), sh_shapes)
    loss = gsum[0].reshape(())
    rep_names = ["ffn1_norm_pre", "ffn1_norm_post", "mix_norm_pre", "mix_norm_post", "ffn2_norm_pre", "ffn2_norm_post",
                 "shift_mu", "w0", "a0", "k_k", "k_a", "r_k", "gn_w", "gn_b", "conv_b", "conv_ln_w", "conv_ln_b"]
    shard_names = ["w_up", "a_up", "g_up", "conv_dw"]
    env = dict(locals())
    small_g = {n: gsum[1 + i] for i, n in enumerate(rep_names)}
    small_g["r_k"] = small_g["r_k"].reshape(1, NH, HN)
    for i, n in enumerate(shard_names):
        small_g[n] = gsh[i][None]
    names = rep_names + shard_names
    shapes = [env[n].shape for n in names]
    n_vec = _round_up(sum(_round_up(math.prod(s), LANES) for s in shapes), 8 * LANES)
    packed = [_pack([d[n] for n in names], n_vec) for d in
              ({n: env[n] for n in names}, small_g, {n: env["m_" + n] for n in names}, {n: env["v_" + n] for n in names})]
    d_vec, m_vec, v_vec = _adam_vec(*packed, "adam_small")
    for n, dl, nm_, nv_ in zip(names, _unpack(d_vec, shapes), _unpack(m_vec, shapes), _unpack(v_vec, shapes)):
        res[n] = [small_g[n], dl, nm_, nv_]

    order = ["ffn1_norm_pre", "ffn1_norm_post", "ffn1_w_gu", "ffn1_w_down", "mix_norm_pre", "mix_norm_post", "w_in",
             "shift_mu", "w_up", "w0", "a_up", "a0", "g_up", "k_k", "k_a", "r_k", "gn_w", "gn_b", "conv_dw", "conv_b",
             "conv_ln_w", "conv_ln_b", "w_out", "ffn2_norm_pre", "ffn2_norm_post", "ffn2_w_gu", "ffn2_w_down"]
    return (loss, dx0.reshape(1, t, D), *[res[n][0] for n in order], *[res[n][1] for n in order],
            *[res[n][2] for n in order], *[res[n][3] for n in order])
```

```python
import functools
import math

import jax
import jax.numpy as jnp
from jax import lax
from jax.experimental import pallas as pl
from jax.experimental.pallas import tpu as pltpu

f32 = jnp.float32
bf16 = jnp.bfloat16

D = 1024
F = 2816
FB = 704
DR = 512
DC = 512
NH = 8
HN = 64
DS = 1792
DIN = 2816
CW = 31
CHUNK = 64
R1_CHUNKS = 4
R2_CHUNKS = 4
NDEV = 8
RMS_EPS = 1e-6
GN_EPS = 64e-5
LN_EPS = 1e-5
DECAY_SCALE = math.exp(-0.5)
ADAM_LR, ADAM_B1, ADAM_B2, ADAM_EPS, ADAM_WD, ADAM_STEP = 0.001, 0.9, 0.999, 1e-08, 0.01, 10
LANES = 128
HALO = 32
CONV_ROWS = 32
TM_MXU = 1024
TM_SUB = 256
TM_VPU = 256

ANY = pl.BlockSpec(memory_space=pl.ANY)


def _full(shape):
    return pl.BlockSpec(shape, lambda *_: (0,) * len(shape))


def _sds(shape, dtype=f32):
    return jax.ShapeDtypeStruct(shape, dtype)


def _dot(a, b):
    return jnp.dot(a, b, preferred_element_type=f32)


def _dot_nt(a, b):
    return lax.dot_general(a, b, (((1,), (1,)), ((), ())), preferred_element_type=f32)


def _dot_tn(a, b):
    return lax.dot_general(a, b, (((0,), (0,)), ((), ())), preferred_element_type=f32)


def _terms(a, n):
    out, rem = [], a
    for i in range(n):
        t = rem.astype(bf16)
        out.append(t)
        if i + 1 < n:
            rem = rem - t.astype(f32)
    return out


def _mm_raw(a, b, kind, mode):
    nb = a.ndim - 2
    bd = tuple(range(nb))
    ca = nb if kind == "tn" else nb + 1
    cb = nb + 1 if kind == "nt" else nb
    dn = (((ca,), (cb,)), (bd, bd))
    pa, pb = mode[:2]
    ta, tb = _terms(a, pa), _terms(b, pb)
    acc = None
    for i in range(pa):
        for j in range(pb):
            if i + j < max(pa, pb):
                p = lax.dot_general(ta[i], tb[j], dn, preferred_element_type=f32)
                acc = p if acc is None else acc + p
    return acc


@functools.partial(jax.custom_vjp, nondiff_argnums=(2, 3))
def mm(a, b, kind, mode):
    return _mm_raw(a, b, kind, mode)


def _mm_fwd(a, b, kind, mode):
    return _mm_raw(a, b, kind, mode), (a, b)


def _ct_terms(keep):
    return EXACT_TERMS if keep == 1 else keep


def _mm_bwd(kind, mode, res, ct):
    a, b = res
    pa, pb = mode[:2]
    if len(mode) == 4:
        ca = cb = mode[2]
        pa = pb = mode[3]
    elif mode == (1, 1):
        ca = cb = 1
    else:
        ca, cb = _ct_terms(pb), _ct_terms(pa)
    if kind == "nn":
        da = mm(ct, b, "nt", (ca, pb))
        db = mm(a, ct, "tn", (pa, cb))
    elif kind == "nt":
        da = mm(ct, b, "nn", (ca, pb))
        db = mm(ct, a, "tn", (cb, pa))
    else:
        da = mm(b, ct, "nt", (pb, ca))
        db = mm(a, ct, "nn", (pa, cb))
    return da, db


mm.defvjp(_mm_fwd, _mm_bwd)

X3 = (2, 2)
EXACT_TERMS = 2
EA = (1, EXACT_TERMS)
EB = (EXACT_TERMS, 1)
X1 = (1, 1)
RGM = (1, 1, 1, 2)
RIM = X1
RWM = X1
RAM = X1
R2M = X1
LORA = X1


def _rms_fwd(x, g):
    inv = lax.rsqrt(jnp.mean(x * x, axis=-1, keepdims=True) + RMS_EPS)
    return x * inv * g


def _rms_bwd(x, g, dy):
    inv = lax.rsqrt(jnp.mean(x * x, axis=-1, keepdims=True) + RMS_EPS)
    xh = x * inv
    dxh = dy * g
    dg = jnp.sum(dy * xh, axis=0, keepdims=True)
    dx = inv * (dxh - xh * jnp.mean(dxh * xh, axis=-1, keepdims=True))
    return dx, dg


def _tile(t, want):
    return min(t, want)


NPEER = NDEV - 1


def _comm_out_shape(arrays, modes, slots):
    out = []
    for a, mode, (lead, _) in zip(arrays, modes, slots):
        shp = tuple(lead) + tuple(a.shape) if mode == "gather" else tuple(a.shape)
        out.append(pltpu.HBM(shp, a.dtype))
    return out


def _comm_sems(n):
    return [pltpu.SemaphoreType.DMA((n * NPEER,)), pltpu.SemaphoreType.DMA((n * NPEER,)), pltpu.SemaphoreType.DMA((n,))]


RELAYED = (3, 5, 7)


def _comm_copies(ins, outs, sems, modes, slots, want):
    n = len(ins)
    send_sems, recv_sems, loc_sems = sems
    x, y, c = lax.axis_index("x"), lax.axis_index("y"), lax.axis_index("c")
    me = 4 * x + 2 * y + c

    def peer_of(k):
        px = 1 - x if k & 4 else x
        py = 1 - y if k & 2 else y
        pc = 1 - c if k & 1 else c
        return (px, py, pc), 4 * px + 2 * py + pc

    def slot(i, dev):
        return outs[i].at[slots[i][1](dev)]

    if want == "local":
        return [pltpu.make_async_copy(ins[i] if modes[i] == "gather" else ins[i].at[slots[i][1](me)], slot(i, me),
                                      loc_sems.at[i]) for i in range(n)]
    sibling = peer_of(1)[0]
    out = []
    for k in range(1, NDEV):
        peer, pid = peer_of(k)
        for i in range(n):
            sem = dict(send_sem=send_sems.at[i * NPEER + k - 1], recv_sem=recv_sems.at[i * NPEER + k - 1],
                       device_id_type=pl.DeviceIdType.MESH)
            relayed = modes[i] == "gather" and k in RELAYED
            if want == "recv":
                src = ins[i] if modes[i] == "gather" else ins[i].at[slots[i][1](pid)]
                out.append(pltpu.make_async_remote_copy(src_ref=src, dst_ref=slot(i, pid), device_id=peer, **sem))
            elif want == "first" and not relayed:
                src = ins[i] if modes[i] == "gather" else ins[i].at[slots[i][1](pid)]
                out.append(pltpu.make_async_remote_copy(src_ref=src, dst_ref=slot(i, me), device_id=peer, **sem))
            elif want == "relay" and relayed:
                origin = peer_of(k - 1)[1]
                out.append(((k - 2) * n + i, pltpu.make_async_remote_copy(
                    src_ref=slot(i, origin), dst_ref=slot(i, origin), device_id=sibling, **sem)))
    return out


def _comm_start(ins, outs, sems, modes, slots):
    for cp in _comm_copies(ins, outs, sems, modes, slots, "local") + _comm_copies(ins, outs, sems, modes, slots, "first"):
        cp.start()


def _comm_wait(ins, outs, sems, modes, slots):
    recvs = _comm_copies(ins, outs, sems, modes, slots, "recv")
    relays = _comm_copies(ins, outs, sems, modes, slots, "relay")
    for idx, cp in relays:
        recvs[idx].wait_recv()
        cp.start()
    passed_on = {idx for idx, _ in relays}
    for idx, cp in enumerate(recvs):
        if idx not in passed_on:
            cp.wait_recv()
    for cp in _comm_copies(ins, outs, sems, modes, slots, "first") + [cp for _, cp in relays]:
        cp.wait_send()
    for cp in _comm_copies(ins, outs, sems, modes, slots, "local"):
        cp.wait()


def _exchange(arrays, modes, slots, name):
    n = len(arrays)

    def body(*refs):
        ins, outs, sems = refs[:n], refs[n:2 * n], refs[2 * n:]
        _comm_start(ins, outs, sems, modes, slots)
        _comm_wait(ins, outs, sems, modes, slots)

    return pl.pallas_call(
        body, name=name, out_shape=_comm_out_shape(arrays, modes, slots),
        in_specs=[ANY] * n, out_specs=[ANY] * n, scratch_shapes=_comm_sems(n),
    )(*arrays)


def _pcall(body, *, name, grid, in_specs, out_specs, out_shape, args, scratch_shapes=(), comm=None):
    if comm is None:
        return pl.pallas_call(body, name=name, grid=grid, in_specs=in_specs, out_specs=out_specs, out_shape=out_shape,
                              scratch_shapes=list(scratch_shapes))(*args), None
    arrays, modes, slots = comm
    n_in, n_out, n_scr, nc = len(args), len(out_shape), len(scratch_shapes), len(arrays)

    def hosted(*refs):
        a_in, c_in = refs[:n_in], refs[n_in:n_in + nc]
        o = n_in + nc
        a_out, c_out = refs[o:o + n_out], refs[o + n_out:o + n_out + nc]
        o += n_out + nc
        a_scr, sems = refs[o:o + n_scr], refs[o + n_scr:]
        first = pl.program_id(0) == 0
        last = pl.program_id(0) == grid[0] - 1
        for ax in range(1, len(grid)):
            first = jnp.logical_and(first, pl.program_id(ax) == 0)
            last = jnp.logical_and(last, pl.program_id(ax) == grid[ax] - 1)

        @pl.when(first)
        def _():
            _comm_start(c_in, c_out, sems, modes, slots)

        body(*a_in, *a_out, *a_scr)

        @pl.when(last)
        def _():
            _comm_wait(c_in, c_out, sems, modes, slots)

    res = pl.pallas_call(
        hosted, name=name, grid=grid, in_specs=list(in_specs) + [ANY] * nc, out_specs=list(out_specs) + [ANY] * nc,
        out_shape=list(out_shape) + _comm_out_shape(arrays, modes, slots),
        scratch_shapes=list(scratch_shapes) + _comm_sems(nc),
    )(*args, *arrays)
    return res[:n_out], res[n_out:]


SLOT8 = ((NDEV,), lambda d: (d,))
SLOT42 = ((4, 2), lambda d: (lax.rem(d, 4), lax.div(d, 4)))


def _ffn_in_fwd(x, g, wg4, name, comm=None):
    t = x.shape[0]
    tm = _tile(t, TM_MXU)
    sub = _tile(tm, TM_SUB)

    def body(x_ref, g_ref, w_ref, xn_ref, gu_ref, h_ref):
        for s in range(tm // sub):
            rows = pl.ds(s * sub, sub)
            xn = _rms_fwd(x_ref[rows, :], g_ref[...]).astype(bf16)
            xn_ref[rows, :] = xn
            gate = _dot_nt(xn, w_ref[0, 0])
            up = _dot_nt(xn, w_ref[0, 1])
            gu_ref[0, 0, rows, :] = gate.astype(bf16)
            gu_ref[0, 1, rows, :] = up.astype(bf16)
            h_ref[0, rows, :] = (gate * jax.nn.sigmoid(gate) * up).astype(bf16)

    return _pcall(
        body, name=name, grid=(t // tm, 4),
        in_specs=[pl.BlockSpec((tm, D), lambda i, e: (i, 0)), _full((1, D)),
                  pl.BlockSpec((1, 2, FB, D), lambda i, e: (e, 0, 0, 0))],
        out_specs=[pl.BlockSpec((tm, D), lambda i, e: (i, 0)),
                   pl.BlockSpec((1, 2, tm, FB), lambda i, e: (e, 0, i, 0)),
                   pl.BlockSpec((1, tm, FB), lambda i, e: (e, i, 0))],
        out_shape=[_sds((t, D), bf16), _sds((4, 2, t, FB), bf16), _sds((4, t, FB), bf16)],
        args=(x, g, wg4), comm=comm)


def _mm_norm_res_fwd(hb, wb, xres, g, scale, name, comm=None, tgt=None):
    e_n, t, k = hb.shape
    tm = _tile(t, TM_MXU)

    def body(h_ref, w_ref, x_ref, g_ref, *rest):
        f = _dot(h_ref[0], w_ref[0])
        for e in range(1, e_n):
            f = f + _dot(h_ref[e], w_ref[e])
        xnew = x_ref[...] + scale * _rms_fwd(f, g_ref[...])
        if tgt is None:
            f_ref, o_ref = rest
            o_ref[...] = xnew
        else:
            t_ref, f_ref, o_ref, l_ref = rest
            err = xnew - t_ref[...]
            o_ref[...] = err * (1.0 / D)
            part = 0.5 * jnp.sum(jnp.mean(err * err, axis=-1, keepdims=True), axis=0, keepdims=True)

            @pl.when(pl.program_id(0) == 0)
            def _():
                l_ref[...] = part

            @pl.when(pl.program_id(0) != 0)
            def _():
                l_ref[...] += part
        f_ref[...] = f

    tile = pl.BlockSpec((tm, D), lambda i: (i, 0))
    outs, got = _pcall(
        body, name=name, grid=(t // tm,),
        in_specs=[pl.BlockSpec((e_n, tm, k), lambda i: (0, i, 0)), _full((e_n, k, D)), tile, _full((1, D))]
        + ([] if tgt is None else [tile]),
        out_specs=[tile, tile] + ([] if tgt is None else [_full((1, 1))]),
        out_shape=[_sds((t, D)), _sds((t, D))] + ([] if tgt is None else [_sds((1, 1))]),
        args=(hb, wb, xres, g) + (() if tgt is None else (tgt,)), comm=comm)
    return outs if comm is None else (outs, got)


def _post_bwd_nt(dxn, f, g, wb, scale, gu, name, comm=None):
    e_n, k, _ = wb.shape
    t = f.shape[0]
    tm = _tile(t, TM_MXU)
    sub = _tile(tm, TM_SUB)
    swiglu = gu is not None

    def body(*refs):
        if swiglu:
            dx_ref, f_ref, g_ref, w_ref, gu_ref, df_ref, dg_ref, dh_ref, df_s = refs
        else:
            dx_ref, f_ref, g_ref, w_ref, df_ref, dg_ref, dh_ref, df_s = refs
        i, e = pl.program_id(0), pl.program_id(1)

        @pl.when(e == 0)
        def _():
            df, dg = _rms_bwd(f_ref[...], g_ref[...], scale * dx_ref[...])
            df_s[...] = df.astype(bf16)
            df_ref[...] = df_s[...]

            @pl.when(i == 0)
            def _():
                dg_ref[...] = dg

            @pl.when(i != 0)
            def _():
                dg_ref[...] += dg

        for s in range(tm // sub):
            rows = pl.ds(s * sub, sub)
            dh = _dot_nt(df_s[rows, :], w_ref[0])
            if swiglu:
                gate = gu_ref[0, 0, rows, :].astype(f32)
                up = gu_ref[0, 1, rows, :].astype(f32)
                sg = jax.nn.sigmoid(gate)
                dh_ref[0, 0, rows, :] = (dh * up * (sg * (1.0 + gate * (1.0 - sg)))).astype(bf16)
                dh_ref[0, 1, rows, :] = (dh * gate * sg).astype(bf16)
            else:
                dh_ref[0, rows, :] = dh

    in_specs = [pl.BlockSpec((tm, D), lambda i, e: (i, 0)), pl.BlockSpec((tm, D), lambda i, e: (i, 0)), _full((1, D)),
                pl.BlockSpec((1, k, D), lambda i, e: (e, 0, 0))]
    args = [dxn, f, g, wb]
    if swiglu:
        in_specs.append(pl.BlockSpec((1, 2, tm, k), lambda i, e: (e, 0, i, 0)))
        args.append(gu)
        dh_spec = pl.BlockSpec((1, 2, tm, k), lambda i, e: (e, 0, i, 0))
        dh_shape = _sds((e_n, 2, t, k), bf16)
    else:
        dh_spec = pl.BlockSpec((1, tm, k), lambda i, e: (e, i, 0))
        dh_shape = _sds((e_n, t, k), f32)
    return _pcall(
        body, name=name, grid=(t // tm, e_n), in_specs=in_specs,
        out_specs=[pl.BlockSpec((tm, D), lambda i, e: (i, 0)), _full((1, D)), dh_spec],
        out_shape=[_sds((t, D), bf16), _sds((1, D)), dh_shape],
        scratch_shapes=[pltpu.VMEM((tm, D), bf16)], args=args, comm=comm)


def _nt_pre_bwd(dy, wb, x, g, dres, name, comm=None):
    e_n, q_n, t, k = dy.shape
    tm = _tile(t, TM_MXU)

    def body(dy_ref, w_ref, x_ref, g_ref, r_ref, dx_ref, dg_ref, acc):
        i, e = pl.program_id(0), pl.program_id(1)
        p = _dot(dy_ref[0, 0], w_ref[0, 0])
        for q in range(1, q_n):
            p = p + _dot(dy_ref[0, q], w_ref[0, q])

        @pl.when(e == 0)
        def _():
            acc[...] = p

        @pl.when(e != 0)
        def _():
            acc[...] += p

        @pl.when(e == e_n - 1)
        def _():
            dx, dg = _rms_bwd(x_ref[...], g_ref[...], acc[...])
            dx_ref[...] = r_ref[...] + dx

            @pl.when(i == 0)
            def _():
                dg_ref[...] = dg

            @pl.when(i != 0)
            def _():
                dg_ref[...] += dg

    return _pcall(
        body, name=name, grid=(t // tm, e_n),
        in_specs=[pl.BlockSpec((1, q_n, tm, k), lambda i, e: (e, 0, i, 0)),
                  pl.BlockSpec((1, q_n, k, D), lambda i, e: (e, 0, 0, 0)),
                  pl.BlockSpec((tm, D), lambda i, e: (i, 0)), _full((1, D)),
                  pl.BlockSpec((tm, D), lambda i, e: (i, 0))],
        out_specs=[pl.BlockSpec((tm, D), lambda i, e: (i, 0)), _full((1, D))],
        out_shape=[_sds((t, D)), _sds((1, D))],
        scratch_shapes=[pltpu.VMEM((tm, D), f32)], args=(dy, wb, x, g, dres), comm=comm)


def _wgrad(a, b, a_batched, b_batched, tn, name, comm=None):
    t, k = a.shape[1], a.shape[2]
    nn = b.shape[2]
    nb = max(a.shape[0], b.shape[0])
    tt = _tile(t, TM_MXU)
    nt = t // tt

    def body(a_ref, b_ref, o_ref, acc):
        @pl.when(pl.program_id(2) == 0)
        def _():
            acc[...] = _dot_tn(a_ref[0], b_ref[0])

        @pl.when(pl.program_id(2) != 0)
        def _():
            acc[...] += _dot_tn(a_ref[0], b_ref[0])

        @pl.when(pl.program_id(2) == nt - 1)
        def _():
            o_ref[0] = acc[...].astype(bf16)

    (out,), got = _pcall(
        body, name=name, grid=(nb, nn // tn, nt),
        in_specs=[pl.BlockSpec((1, tt, k), (lambda n, j, s: (n, s, 0)) if a_batched else (lambda n, j, s: (0, s, 0))),
                  pl.BlockSpec((1, tt, tn), (lambda n, j, s: (n, s, j)) if b_batched else (lambda n, j, s: (0, s, j)))],
        out_specs=[pl.BlockSpec((1, k, tn), lambda n, j, s: (n, 0, j))],
        out_shape=[_sds((nb, k, nn), bf16)],
        scratch_shapes=[pltpu.VMEM((k, tn), f32)], args=(a, b), comm=comm)
    return out if comm is None else (out, got)


def _norm_mm_fwd(x, g, w, name):
    t = x.shape[0]
    tm = _tile(t, TM_VPU)

    def body(x_ref, g_ref, w_ref, xn_ref, ps_ref, pc_ref):
        xn = _rms_fwd(x_ref[...], g_ref[...]).astype(bf16)
        xn_ref[...] = xn
        ps_ref[...] = _dot_nt(xn, w_ref[:DS, :])
        pc_ref[...] = _dot_nt(xn, w_ref[DS:, :])

    return pl.pallas_call(
        body, name=name, grid=(t // tm,),
        in_specs=[pl.BlockSpec((tm, D), lambda i: (i, 0)), _full((1, D)), _full((DIN, D))],
        out_specs=[pl.BlockSpec((tm, D), lambda i: (i, 0)), pl.BlockSpec((tm, DS), lambda i: (i, 0)),
                   pl.BlockSpec((tm, 2 * DC), lambda i: (i, 0))],
        out_shape=[_sds((t, D), bf16), _sds((t, DS)), _sds((t, 2 * DC))],
    )(x, g, w)


def _segsum_b(x, seg, seg_t):
    return mm(mm(x, seg, "nn", EB), seg_t, "nn", EB)


def _prep_fn(psl, w0, a0, k_k, k_a, wup, aup, gup, seg, seg_t):
    r, k, v = psl[:, :DR], psl[:, DR:2 * DR], psl[:, 2 * DR:3 * DR]
    xwa, xg = psl[:, 3 * DR:3 * DR + LANES], psl[:, 3 * DR + LANES:]
    d = w0 + mm(jnp.tanh(xwa), wup, "nn", LORA)
    lw = -DECAY_SCALE * jax.nn.sigmoid(d)
    a = jax.nn.sigmoid(a0 + mm(xwa, aup, "nn", LORA))
    g = mm(jax.nn.sigmoid(xg), gup, "nn", LORA)
    kkr = k * k_k
    kk = kkr * lax.rsqrt(jnp.maximum(_segsum_b(kkr * kkr, seg, seg_t), 1e-12))
    kp = k * (1.0 + (a - 1.0) * k_a)
    return r, lw, kp, v, -kk, kk * a, g


def _shifted(ps, halo_row, first):
    prev = jnp.where(first, 0.0, halo_row)
    sh = pltpu.roll(ps, 1, 0)
    row = lax.broadcasted_iota(jnp.int32, ps.shape, 0)
    return jnp.where(row == 0, prev, sh)


def _heads_of(ref, rows):
    return jnp.stack([ref[rows, h * HN:(h + 1) * HN] for h in range(NH)], axis=0)


def _heads_to(ref, rows, val):
    ref[rows, :] = jnp.concatenate([val[h] for h in range(NH)], axis=-1)


_PREP_PARAM_SHAPES = [(1, DS), (1, DR), (1, DR), (1, DR), (1, DR), (LANES, DR), (LANES, DR), (LANES, DR),
                      (DR, LANES), (LANES, DR)]


def _prep_fwd(ps, params, name, comm=None):
    t = ps.shape[0]
    tm = _tile(t, TM_VPU)
    hb = tm // 8

    def body(ps_ref, halo_ref, mu_ref, *rest):
        prm = [r[...] for r in rest[:9]]
        outs = rest[9:]
        x = ps_ref[...]
        sh = _shifted(x, halo_ref[pl.ds(7, 1), :], pl.program_id(0) == 0)
        psl = x + (sh - x) * mu_ref[...]
        vals = _prep_fn(psl, *prm)
        for ref, val in zip(outs, vals):
            ref[...] = val

    outs, got = _pcall(
        body, name=name, grid=(t // tm,),
        in_specs=[pl.BlockSpec((tm, DS), lambda i: (i, 0)),
                  pl.BlockSpec((8, DS), lambda i: (jnp.maximum(i * hb - 1, 0), 0))]
        + [_full(s) for s in _PREP_PARAM_SHAPES],
        out_specs=[pl.BlockSpec((tm, DR), lambda i: (i, 0))] * 7,
        out_shape=[_sds((t, DR))] * 7, args=(ps, ps, *params), comm=comm)
    return outs if comm is None else (outs, got)


def _prep_bwd(ps, params, cts_hm, cts_tm, name):
    t = ps.shape[0]
    tm = _tile(t, TM_VPU)
    hb = tm // 8

    def body(ps_ref, halo_ref, mu_ref, *rest):
        prm = [r[...] for r in rest[:9]]
        chm = rest[9:15]
        ctm = rest[15:19]
        dpsl_ref = rest[19]
        gouts = rest[20:27]
        x = ps_ref[...]
        sh = _shifted(x, halo_ref[pl.ds(7, 1), :], pl.program_id(0) == 0)
        psl = x + (sh - x) * mu_ref[...]
        seg, seg_t = prm[7], prm[8]
        _, vjp = jax.vjp(lambda p, *w: _prep_fn(p, *w, seg, seg_t), psl, *prm[:7])
        c = [r[...] for r in chm]
        cts = (c[0] + ctm[0][...], c[1], c[2] + ctm[1][...], c[3] + ctm[2][...], c[4], c[5], ctm[3][...])
        grads = vjp(cts)
        dpsl_ref[...] = grads[0]
        for ref, gval in zip(gouts, grads[1:]):
            @pl.when(pl.program_id(0) == 0)
            def _(ref=ref, gval=gval):
                ref[...] = gval

            @pl.when(pl.program_id(0) != 0)
            def _(ref=ref, gval=gval):
                ref[...] += gval

    tk = pl.BlockSpec((tm, DR), lambda i: (i, 0))
    gshapes = _PREP_PARAM_SHAPES[1:8]
    return pl.pallas_call(
        body, name=name, grid=(t // tm,),
        in_specs=[pl.BlockSpec((tm, DS), lambda i: (i, 0)),
                  pl.BlockSpec((8, DS), lambda i: (jnp.maximum(i * hb - 1, 0), 0))]
        + [_full(s) for s in _PREP_PARAM_SHAPES] + [tk] * 10,
        out_specs=[pl.BlockSpec((tm, DS), lambda i: (i, 0))] + [_full(s) for s in gshapes],
        out_shape=[_sds((t, DS))] + [_sds(s) for s in gshapes],
    )(ps, ps, *params, *cts_hm, *cts_tm)


def _shift_bwd(dpsl, ps, mu, dpc, name):
    t = ps.shape[0]
    tm = _tile(t, TM_VPU)
    hb = tm // 8
    last_blk = t // 8 - 1

    def body(d_ref, dn_ref, ps_ref, halo_ref, mu_ref, dpc_ref, dp_ref, dmu_ref):
        i = pl.program_id(0)
        mu_v = mu_ref[...]
        d = d_ref[...]
        nxt = jnp.where(i == pl.num_programs(0) - 1, 0.0, dn_ref[pl.ds(0, 1), :])
        up = pltpu.roll(d, tm - 1, 0)
        row = lax.broadcasted_iota(jnp.int32, d.shape, 0)
        up = jnp.where(row == tm - 1, nxt, up)
        dp_ref[:, :DS] = (d * (1.0 - mu_v) + up * mu_v).astype(bf16)
        dp_ref[:, DS:] = dpc_ref[...].astype(bf16)
        x = ps_ref[...]
        sh = _shifted(x, halo_ref[pl.ds(7, 1), :], i == 0)
        dmu = jnp.sum(d * (sh - x), axis=0, keepdims=True)

        @pl.when(i == 0)
        def _():
            dmu_ref[...] = dmu

        @pl.when(i != 0)
        def _():
            dmu_ref[...] += dmu

    return pl.pallas_call(
        body, name=name, grid=(t // tm,),
        in_specs=[pl.BlockSpec((tm, DS), lambda i: (i, 0)),
                  pl.BlockSpec((8, DS), lambda i: (jnp.minimum((i + 1) * hb, last_blk), 0)),
                  pl.BlockSpec((tm, DS), lambda i: (i, 0)),
                  pl.BlockSpec((8, DS), lambda i: (jnp.maximum(i * hb - 1, 0), 0)),
                  _full((1, DS)), pl.BlockSpec((tm, 2 * DC), lambda i: (i, 0))],
        out_specs=[pl.BlockSpec((tm, DIN), lambda i: (i, 0)), _full((1, DS))],
        out_shape=[_sds((t, DIN), bf16), _sds((1, DS))],
    )(dpsl, dpsl, ps, ps, mu, dpc)


def _post_fn(y, r, kp, v, g, gn_w, gn_b, r_k, seg, seg_t):
    mu = _segsum_b(y, seg, seg_t) * (1.0 / HN)
    yc = y - mu
    var = _segsum_b(yc * yc, seg, seg_t) * (1.0 / HN)
    yo = yc * lax.rsqrt(var + GN_EPS) * gn_w + gn_b
    bonus = _segsum_b(r * kp * r_k, seg, seg_t) * v
    return (yo + bonus) * g


_POST_PARAM_SHAPES = [(1, DR), (1, DR), (1, DR), (DR, LANES), (LANES, DR)]


def _post_fwd(y, r, kp, v, g, out_b, params, name):
    t = g.shape[0]
    tm = _tile(t, TM_VPU)

    def body(y_ref, r_ref, k_ref, v_ref, g_ref, ob_ref, *rest):
        prm = [p[...] for p in rest[:5]]
        cat_ref = rest[5]
        oa = _post_fn(y_ref[...], r_ref[...], k_ref[...], v_ref[...], g_ref[...], *prm)
        cat_ref[:, :DR] = oa.astype(bf16)
        cat_ref[:, DR:] = ob_ref[...]

    tk = pl.BlockSpec((tm, DR), lambda i: (i, 0))
    return pl.pallas_call(
        body, name=name, grid=(t // tm,),
        in_specs=[tk] * 6 + [_full(s) for s in _POST_PARAM_SHAPES],
        out_specs=pl.BlockSpec((tm, 2 * DR), lambda i: (i, 0)),
        out_shape=_sds((t, 2 * DR), bf16),
    )(y, r, kp, v, g, out_b, *params)


def _post_bwd(dcat, y, r, kp, v, g, params, name, comm=None):
    t = g.shape[0]
    tm = _tile(t, TM_VPU)

    def body(dc_ref, y_ref, r_ref, k_ref, v_ref, g_ref, *rest):
        prm = [p[...] for p in rest[:5]]
        dy_ref, dr_ref, dk_ref, dv_ref, dg_ref = rest[5:10]
        gouts = rest[10:13]
        seg, seg_t = prm[3], prm[4]
        _, vjp = jax.vjp(lambda *a: _post_fn(*a, seg, seg_t),
                         y_ref[...], r_ref[...], k_ref[...], v_ref[...], g_ref[...], *prm[:3])
        grads = vjp(dc_ref[...])
        dy_ref[...] = grads[0]
        dr_ref[...] = grads[1]
        dk_ref[...] = grads[2]
        dv_ref[...] = grads[3]
        dg_ref[...] = grads[4]
        for ref, gval in zip(gouts, grads[5:]):
            @pl.when(pl.program_id(0) == 0)
            def _(ref=ref, gval=gval):
                ref[...] = gval

            @pl.when(pl.program_id(0) != 0)
            def _(ref=ref, gval=gval):
                ref[...] += gval

    tk = pl.BlockSpec((tm, DR), lambda i: (i, 0))
    return _pcall(
        body, name=name, grid=(t // tm,),
        in_specs=[tk] * 6 + [_full(s) for s in _POST_PARAM_SHAPES],
        out_specs=[tk] * 5 + [_full((1, DR))] * 3,
        out_shape=[_sds((t, DR))] * 5 + [_sds((1, DR))] * 3,
        args=(dcat, y, r, kp, v, g, *params), comm=comm)


@jax.custom_vjp
def _inv_unit_lower(lb):
    c = lb.shape[-1]
    row = lax.broadcasted_iota(jnp.int32, (c, c), 0)
    col = lax.broadcasted_iota(jnp.int32, (c, c), 1)
    p = (row == col).astype(f32)[None] + lb
    m = lb
    for _ in range(c.bit_length() - 2):
        m = mm(m, m, "nn", RIM)
        p = p + mm(m, p, "nn", RIM)
    return p


def _inv_unit_lower_fwd(lb):
    p = _inv_unit_lower(lb)
    return p, p


def _inv_unit_lower_bwd(p, ct):
    return (mm(mm(p, ct, "tn", RIM), p, "nt", RIM),)


_inv_unit_lower.defvjp(_inv_unit_lower_fwd, _inv_unit_lower_bwd)


def _r1_fn(r, lw, k, v, z, b):
    c = r.shape[1]
    row = lax.broadcasted_iota(jnp.int32, (c, c), 0)
    col = lax.broadcasted_iota(jnp.int32, (c, c), 1)
    incl = (row >= col)[None]
    strict = (row > col)[None]
    eye = (row == col).astype(f32)[None]
    tri = jnp.broadcast_to((row >= col).astype(f32)[None], (r.shape[0], c, c))
    cum = mm(tri, lw, "nn", EA)
    tot = jnp.sum(lw, axis=1, keepdims=True)
    zt = z * jnp.exp(cum - lw)
    rt = r * jnp.exp(cum)
    g_inv = jnp.exp(-cum)
    g_rem = jnp.exp(tot - cum)
    bt, kt = b * g_inv, k * g_inv
    bh, kh = b * g_rem, k * g_rem
    lb = jnp.where(strict, mm(zt, bt, "nt", RGM), 0.0)
    lk = jnp.where(strict, mm(zt, kt, "nt", RGM), 0.0)
    mb = jnp.where(incl, mm(rt, bt, "nt", RGM), 0.0)
    mk = jnp.where(incl, mm(rt, kt, "nt", RGM), 0.0)
    p = _inv_unit_lower(lb)
    w1 = mm(p, zt, "nn", RWM)
    w2 = mm(p, mm(lk, v, "nn", RWM), "nn", RWM)
    a = mm(w1, bh, "tn", RAM) + eye * jnp.exp(tot)
    g = mm(w2, bh, "tn", RAM) + mm(v, kh, "tn", RAM)
    q = rt + mm(mb, w1, "nn", RWM)
    yl = mm(mb, w2, "nn", RWM) + mm(mk, v, "nn", RWM)
    return a, g, q, yl


def _chunks_in(ref, n):
    return jnp.concatenate([_heads_of(ref, pl.ds(s * CHUNK, CHUNK)) for s in range(n)], axis=0)


def _chunks_out(ref, val, n):
    for s in range(n):
        _heads_to(ref, pl.ds(s * CHUNK, CHUNK), val[s * NH:(s + 1) * NH])


def _r1_fwd(ins, name, comm=None):
    t = ins[0].shape[0]
    n = R1_CHUNKS
    nc = t // CHUNK

    def body(r, lw, k, v, z, b, a_ref, g_ref, q_ref, yl_ref):
        a, g, q, yl = _r1_fn(*[_chunks_in(x, n) for x in (r, lw, k, v, z, b)])
        for s in range(n):
            a_ref[s] = a[s * NH:(s + 1) * NH]
            g_ref[s] = g[s * NH:(s + 1) * NH]
        _chunks_out(q_ref, q, n)
        _chunks_out(yl_ref, yl, n)

    ck = pl.BlockSpec((n * CHUNK, DR), lambda c: (c, 0))
    st = pl.BlockSpec((n, NH, HN, HN), lambda c: (c, 0, 0, 0))
    return _pcall(
        body, name=name, grid=(nc // n,), in_specs=[ck] * 6, out_specs=[st, st, ck, ck],
        out_shape=[_sds((nc, NH, HN, HN))] * 2 + [_sds((t, DR))] * 2, args=tuple(ins), comm=comm)


def _r1_bwd(ins, da, dg, dq, dyl, name, comm=None):
    t = ins[0].shape[0]
    n = R1_CHUNKS
    nc = t // CHUNK

    def body(r, lw, k, v, z, b, da_ref, dg_ref, dq_ref, dyl_ref, *outs):
        _, vjp = jax.vjp(_r1_fn, *[_chunks_in(x, n) for x in (r, lw, k, v, z, b)])
        cts = (jnp.concatenate([da_ref[s] for s in range(n)], axis=0),
               jnp.concatenate([dg_ref[s] for s in range(n)], axis=0), _chunks_in(dq_ref, n), _chunks_in(dyl_ref, n))
        for ref, gval in zip(outs, vjp(cts)):
            _chunks_out(ref, gval, n)

    ck = pl.BlockSpec((n * CHUNK, DR), lambda c: (c, 0))
    st = pl.BlockSpec((n, NH, HN, HN), lambda c: (c, 0, 0, 0))
    return _pcall(
        body, name=name, grid=(nc // n,), in_specs=[ck] * 6 + [st, st, ck, ck], out_specs=[ck] * 6,
        out_shape=[_sds((t, DR))] * 6, args=(*ins, da, dg, dq, dyl), comm=comm)


def _r2_fwd(a, g, q, yl, name):
    nc = a.shape[0]
    t = q.shape[0]

    def body(a_ref, g_ref, q_ref, yl_ref, y_ref, s_ref, s):
        @pl.when(pl.program_id(0) == 0)
        def _():
            s[...] = jnp.zeros_like(s)
        s0 = s[...]
        for j in range(n):
            rows = pl.ds(j * CHUNK, CHUNK)
            s_ref[j] = s0
            _heads_to(y_ref, rows, mm(_heads_of(q_ref, rows), s0, "nt", R2M) + _heads_of(yl_ref, rows))
            s0 = mm(s0, a_ref[j], "nn", R2M) + g_ref[j]
        s[...] = s0

    n = math.gcd(nc, R2_CHUNKS)
    ck = pl.BlockSpec((n * CHUNK, DR), lambda c: (c, 0))
    st = pl.BlockSpec((n, NH, HN, HN), lambda c: (c, 0, 0, 0))
    return pl.pallas_call(
        body, name=name, grid=(nc // n,), in_specs=[st, st, ck, ck], out_specs=[ck, st],
        out_shape=[_sds((t, DR)), _sds((nc, NH, HN, HN))],
        scratch_shapes=[pltpu.VMEM((NH, HN, HN), f32)],
    )(a, g, q, yl)


def _r2_bwd(dy, q, s_all, a, name):
    nc = a.shape[0]
    t = q.shape[0]

    def body(dy_ref, q_ref, s_ref, a_ref, dq_ref, da_ref, dg_ref, ds):
        @pl.when(pl.program_id(0) == 0)
        def _():
            ds[...] = jnp.zeros_like(ds)
        dsn = ds[...]
        for j in reversed(range(n)):
            rows = pl.ds(j * CHUNK, CHUNK)
            s0 = s_ref[j]
            dyv = _heads_of(dy_ref, rows)
            dg_ref[j] = dsn
            da_ref[j] = mm(s0, dsn, "tn", R2M)
            _heads_to(dq_ref, rows, mm(dyv, s0, "nn", R2M))
            dsn = mm(dsn, a_ref[j], "nt", R2M) + mm(dyv, _heads_of(q_ref, rows), "tn", R2M)
        ds[...] = dsn

    n = math.gcd(nc, R2_CHUNKS)
    nb = nc // n
    ck = pl.BlockSpec((n * CHUNK, DR), lambda c: (nb - 1 - c, 0))
    st = pl.BlockSpec((n, NH, HN, HN), lambda c: (nb - 1 - c, 0, 0, 0))
    return pl.pallas_call(
        body, name=name, grid=(nb,), in_specs=[ck, ck, st, st], out_specs=[ck, st, st],
        out_shape=[_sds((t, DR)), _sds((nc, NH, HN, HN)), _sds((nc, NH, HN, HN))],
        scratch_shapes=[pltpu.VMEM((NH, HN, HN), f32)],
    )(dy, q, s_all, a)


def _ln_silu(c, w, b):
    mu = jnp.mean(c, axis=-1, keepdims=True)
    cc = c - mu
    var = jnp.mean(cc * cc, axis=-1, keepdims=True)
    u = cc * lax.rsqrt(var + LN_EPS) * w + b
    return u * jax.nn.sigmoid(u)


def _glu_tile(pc):
    return pc[:, :DC] * jax.nn.sigmoid(pc[:, DC:])


def _fill_glu(s_ref, pc_ref, halo_ref, first, tm):
    s_ref[pl.ds(0, HALO), :] = jnp.where(first, 0.0, _glu_tile(halo_ref[...]))
    s_ref[pl.ds(HALO, tm), :] = _glu_tile(pc_ref[...])


def _phase_copies(sh):
    n = sh.shape[1] - 8
    for r in range(1, 8):
        sh[r, pl.ds(0, n), :] = sh[0, pl.ds(r, n), :]


def _rows_at(sh, o, n):
    return sh[o % 8, pl.ds(o - o % 8, n), :]


def _conv_fwd(pc, dw, cb, lnw, lnb, name):
    t = pc.shape[0]
    tm = _tile(t, TM_VPU)
    hb = tm // HALO

    def body(pc_ref, halo_ref, dw_ref, cb_ref, w_ref, b_ref, c_ref, o_ref, s):
        _fill_glu(s.at[0], pc_ref, halo_ref, pl.program_id(0) == 0, tm)
        _phase_copies(s)
        for r0 in range(0, tm, CONV_ROWS):
            rows = pl.ds(r0, CONV_ROWS)
            acc = _rows_at(s, r0 + HALO - CW + 1, CONV_ROWS) * dw_ref[pl.ds(0, 1), :]
            for j in range(1, CW):
                acc = acc + _rows_at(s, r0 + HALO - CW + 1 + j, CONV_ROWS) * dw_ref[pl.ds(j, 1), :]
            c = acc + cb_ref[...]
            c_ref[rows, :] = c
            o_ref[rows, :] = _ln_silu(c, w_ref[...], b_ref[...]).astype(bf16)

    return pl.pallas_call(
        body, name=name, grid=(t // tm,),
        in_specs=[pl.BlockSpec((tm, 2 * DC), lambda i: (i, 0)),
                  pl.BlockSpec((HALO, 2 * DC), lambda i: (jnp.maximum(i * hb - 1, 0), 0)),
                  _full((CW, DC)), _full((1, DC)), _full((1, DC)), _full((1, DC))],
        out_specs=[pl.BlockSpec((tm, DC), lambda i: (i, 0)), pl.BlockSpec((tm, DC), lambda i: (i, 0))],
        out_shape=[_sds((t, DC)), _sds((t, DC), bf16)],
        scratch_shapes=[pltpu.VMEM((8, HALO + tm, DC), f32)],
    )(pc, pc, dw, cb, lnw, lnb)


def _conv_bwd1(dcat, c, lnw, lnb, name):
    t = c.shape[0]
    tm = _tile(t, TM_VPU)

    def body(d_ref, c_ref, w_ref, b_ref, dc_ref, dw_ref, db_ref, dcb_ref):
        _, vjp = jax.vjp(_ln_silu, c_ref[...], w_ref[...], b_ref[...])
        dc, dw, db = vjp(d_ref[...])
        dc_ref[...] = dc
        dcb = jnp.sum(dc, axis=0, keepdims=True)
        for ref, gval in ((dw_ref, dw), (db_ref, db), (dcb_ref, dcb)):
            @pl.when(pl.program_id(0) == 0)
            def _(ref=ref, gval=gval):
                ref[...] = gval

            @pl.when(pl.program_id(0) != 0)
            def _(ref=ref, gval=gval):
                ref[...] += gval

    tk = pl.BlockSpec((tm, DC), lambda i: (i, 0))
    return pl.pallas_call(
        body, name=name, grid=(t // tm,),
        in_specs=[pl.BlockSpec((tm, DC), lambda i: (i, 1)), tk, _full((1, DC)), _full((1, DC))],
        out_specs=[tk] + [_full((1, DC))] * 3,
        out_shape=[_sds((t, DC))] + [_sds((1, DC))] * 3,
    )(dcat, c, lnw, lnb)


def _conv_bwd2(dc, pc, dw, name):
    t = pc.shape[0]
    tm = _tile(t, TM_VPU)
    hb = tm // HALO
    last_blk = t // HALO - 1

    def body(dc_ref, dn_ref, pc_ref, halo_ref, dw_ref, dpc_ref, ddw_ref, s, sd, acc):
        i = pl.program_id(0)
        _fill_glu(s.at[0], pc_ref, halo_ref, i == 0, tm)
        sd[0, pl.ds(0, tm), :] = dc_ref[...]
        sd[0, pl.ds(tm, HALO), :] = jnp.where(i == pl.num_programs(0) - 1, 0.0, dn_ref[...])
        _phase_copies(s)
        _phase_copies(sd)
        for r0 in range(0, tm, CONV_ROWS):
            rows = pl.ds(r0, CONV_ROWS)
            dcb = sd[0, rows, :]
            dglu = None
            for j in range(CW):
                term = _rows_at(sd, r0 + CW - 1 - j, CONV_ROWS) * dw_ref[pl.ds(j, 1), :]
                dglu = term if dglu is None else dglu + term
                part = dcb * _rows_at(s, r0 + HALO - CW + 1 + j, CONV_ROWS)
                part8 = part[0:8]
                for q in range(8, CONV_ROWS, 8):
                    part8 = part8 + part[q:q + 8]
                if r0 == 0:
                    acc[pl.ds(8 * j, 8), :] = part8
                else:
                    acc[pl.ds(8 * j, 8), :] += part8
            a = pc_ref[rows, :DC]
            sg = jax.nn.sigmoid(pc_ref[rows, DC:])
            dpc_ref[rows, :DC] = dglu * sg
            dpc_ref[rows, DC:] = dglu * a * sg * (1.0 - sg)
        rows = [jnp.sum(acc[pl.ds(8 * j, 8), :], axis=0, keepdims=True) for j in range(CW)]

        @pl.when(i == 0)
        def _():
            for j in range(CW):
                ddw_ref[pl.ds(j, 1), :] = rows[j]

        @pl.when(i != 0)
        def _():
            for j in range(CW):
                ddw_ref[pl.ds(j, 1), :] += rows[j]

    return pl.pallas_call(
        body, name=name, grid=(t // tm,),
        in_specs=[pl.BlockSpec((tm, DC), lambda i: (i, 0)),
                  pl.BlockSpec((HALO, DC), lambda i: (jnp.minimum((i + 1) * hb, last_blk), 0)),
                  pl.BlockSpec((tm, 2 * DC), lambda i: (i, 0)),
                  pl.BlockSpec((HALO, 2 * DC), lambda i: (jnp.maximum(i * hb - 1, 0), 0)),
                  _full((CW, DC))],
        out_specs=[pl.BlockSpec((tm, 2 * DC), lambda i: (i, 0)), _full((CW, DC))],
        out_shape=[_sds((t, 2 * DC)), _sds((CW, DC))],
        scratch_shapes=[pltpu.VMEM((8, HALO + tm, DC), f32), pltpu.VMEM((8, tm + HALO, DC), f32),
                        pltpu.VMEM((8 * CW, DC), f32)],
    )(dc, dc, pc, pc, dw)


def _adamw(w, g, m, v):
    m = ADAM_B1 * m + (1.0 - ADAM_B1) * g
    v = ADAM_B2 * v + (1.0 - ADAM_B2) * (g * g)
    m_hat = m / (1.0 - ADAM_B1 ** ADAM_STEP)
    v_hat = v / (1.0 - ADAM_B2 ** ADAM_STEP)
    delta = -ADAM_LR * (m_hat / (jnp.sqrt(v_hat) + ADAM_EPS) + ADAM_WD * w)
    return delta, m, v


def _adam_slots(w, slots, m, v, name):
    r, c = w.shape
    tr = next(cand for cand in (512, 352, 256, 128, r) if r % cand == 0)

    def body(w_ref, s_ref, m_ref, v_ref, g_ref, d_ref, nm_ref, nv_ref):
        g = s_ref[0].astype(f32)
        for k in range(1, NDEV):
            g = g + s_ref[k].astype(f32)
        delta, nm, nv = _adamw(w_ref[...], g, m_ref[...], v_ref[...])
        g_ref[...] = g
        d_ref[...] = delta
        nm_ref[...] = nm
        nv_ref[...] = nv

    blk = pl.BlockSpec((tr, c), lambda i: (i, 0))
    return pl.pallas_call(
        body, name=name, grid=(r // tr,),
        in_specs=[blk, pl.BlockSpec((NDEV, tr, c), lambda i: (0, i, 0)), blk, blk],
        out_specs=[blk] * 4, out_shape=[_sds((r, c))] * 4,
    )(w, slots, m, v)


def _sum_slots(slots, name):
    _, r, c = slots.shape

    def body(s_ref, o_ref):
        g = s_ref[0]
        for k in range(1, NDEV):
            g = g + s_ref[k]
        o_ref[...] = g

    return pl.pallas_call(body, name=name, in_specs=[_full((NDEV, r, c))], out_specs=_full((r, c)),
                          out_shape=_sds((r, c)), grid=(1,))(slots)


def _adam_many(ws, gs, ms, vs, name):
    n = len(ws)
    shapes = [w.shape for w in ws]

    def body(*refs):
        for i in range(n):
            w_ref, g_ref, m_ref, v_ref = refs[i], refs[n + i], refs[2 * n + i], refs[3 * n + i]
            delta, nm, nv = _adamw(w_ref[...], g_ref[...], m_ref[...], v_ref[...])
            refs[4 * n + i][...] = delta
            refs[5 * n + i][...] = nm
            refs[6 * n + i][...] = nv

    outs = pl.pallas_call(body, name=name, grid=(1,), in_specs=[_full(s) for s in shapes] * 4,
                          out_specs=[_full(s) for s in shapes] * 3, out_shape=[_sds(s) for s in shapes] * 3,
                          )(*ws, *gs, *ms, *vs)
    return outs[:n], outs[n:2 * n], outs[2 * n:]


def _pack(pieces, total):
    flat = []
    n = 0
    for p in pieces:
        p = p.reshape(-1)
        pad = (-p.shape[0]) % LANES
        if pad:
            p = jnp.pad(p, (0, pad))
        flat.append(p)
        n += p.shape[0]
    if total > n:
        flat.append(jnp.zeros((total - n,), f32))
    return jnp.concatenate(flat).reshape(total // LANES, LANES)


def _unpack(vec, shapes):
    flat = vec.reshape(-1)
    out, off = [], 0
    for s in shapes:
        n = math.prod(s)
        out.append(flat[off:off + n].reshape(s))
        off += n + (-n) % LANES
    return out


def _round_up(n, m):
    return (n + m - 1) // m * m


def kernel(x, ffn1_norm_pre, ffn1_norm_post, ffn1_w_gu, ffn1_w_down, mix_norm_pre, mix_norm_post, w_in, shift_mu, w_up, w0, a_up, a0, g_up, k_k, k_a, r_k, gn_w, gn_b, conv_dw, conv_b, conv_ln_w, conv_ln_b, w_out, ffn2_norm_pre, ffn2_norm_post, ffn2_w_gu, ffn2_w_down, loss_target, m_ffn1_norm_pre, m_ffn1_norm_post, m_ffn1_w_gu, m_ffn1_w_down, m_mix_norm_pre, m_mix_norm_post, m_w_in, m_shift_mu, m_w_up, m_w0, m_a_up, m_a0, m_g_up, m_k_k, m_k_a, m_r_k, m_gn_w, m_gn_b, m_conv_dw, m_conv_b, m_conv_ln_w, m_conv_ln_b, m_w_out, m_ffn2_norm_pre, m_ffn2_norm_post, m_ffn2_w_gu, m_ffn2_w_down, v_ffn1_norm_pre, v_ffn1_norm_post, v_ffn1_w_gu, v_ffn1_w_down, v_mix_norm_pre, v_mix_norm_post, v_w_in, v_shift_mu, v_w_up, v_w0, v_a_up, v_a0, v_g_up, v_k_k, v_k_a, v_r_k, v_gn_w, v_gn_b, v_conv_dw, v_conv_b, v_conv_ln_w, v_conv_ln_b, v_w_out, v_ffn2_norm_pre, v_ffn2_norm_post, v_ffn2_w_gu, v_ffn2_w_down):
    t = x.shape[1]
    x0 = x.reshape(t, D)
    tgt = loss_target.reshape(t, D)

    def shard(a):
        return a[0].astype(bf16)

    def shard_t(a):
        return jnp.swapaxes(a[0], 0, 1).astype(bf16)

    (wg1,) = _exchange([shard_t(ffn1_w_gu)], ["gather"], [SLOT42], "gather_wg1")

    def cols(a):
        return jnp.transpose(a, (1, 0, 2)).reshape(a.shape[1], NDEV * a.shape[2])

    lane = jnp.arange(DR, dtype=jnp.int32) // HN
    seg = (lane[:, None] == jnp.arange(LANES, dtype=jnp.int32)[None, :]).astype(f32)
    seg_t = seg.T
    rk_row = r_k.reshape(1, DR)
    post_params = (gn_w, gn_b, rk_row, seg, seg_t)

    (xn1, gu1, h1), (wd1, win_g) = _ffn_in_fwd(x0, ffn1_norm_pre, wg1, "ffn1_in_fwd", comm=(
        [shard(ffn1_w_down), shard_t(w_in)], ["gather"] * 2, [SLOT8] * 2))
    wd1 = wd1.reshape(4, FB, D)
    win_t = win_g.reshape(DIN, D)
    (f1, x1), (wout_g, wup_g, aup_g, gup_g, cdw_g) = _mm_norm_res_fwd(
        h1, wd1, x0, ffn1_norm_post, 0.5, "ffn1_out_fwd", comm=(
            [shard(w_out), w_up[0], a_up[0], g_up[0], conv_dw[0]], ["gather"] * 5, [SLOT8] * 5))
    wout_full = wout_g.reshape(1, D, D)
    wup_full, aup_full, gup_full, cdw_full = cols(wup_g), cols(aup_g), cols(gup_g), cols(cdw_g)
    zeros64 = jnp.zeros((HN, DR), f32)
    wup_pad = jnp.concatenate([wup_full, zeros64], axis=0)
    aup_pad = jnp.concatenate([zeros64, aup_full], axis=0)
    prep_params = (shift_mu, w0, a0, k_k, k_a, wup_pad, aup_pad, gup_full, seg, seg_t)
    hm, ps, pc = _norm_mm_fwd(x1, mix_norm_pre, win_t, "mix_in_fwd")
    rec_in, (wd2,) = _prep_fwd(ps, prep_params, "prep_fwd", comm=([shard(ffn2_w_down)], ["gather"], [SLOT8]))
    wd2 = wd2.reshape(4, FB, D)
    g_gate = rec_in[6]
    rec_in = rec_in[:6]
    (a_c, g_c, q_c, yl_c), (wg2,) = _r1_fwd(rec_in, "r1_fwd", comm=([shard_t(ffn2_w_gu)], ["gather"], [SLOT42]))
    y_rec, s_all = _r2_fwd(a_c, g_c, q_c, yl_c, "r2_fwd")
    c_conv, out_b = _conv_fwd(pc, cdw_full, conv_b, conv_ln_w, conv_ln_b, "conv_fwd")
    cat = _post_fwd(y_rec, rec_in[0], rec_in[2], rec_in[3], g_gate, out_b, post_params, "post_fwd")
    fm, x2 = _mm_norm_res_fwd(cat.reshape(1, t, D), wout_full, x1, mix_norm_post, 1.0, "mix_out_fwd")
    (xn2, gu2, h2), _ = _ffn_in_fwd(x2, ffn2_norm_pre, wg2, "ffn2_in_fwd")
    f2, dy, loss_part = _mm_norm_res_fwd(h2, wd2, x2, ffn2_norm_post, 0.5, "ffn2_out_fwd", tgt=tgt)

    (df2, dg_post2, dgu2), _ = _post_bwd_nt(dy, f2, ffn2_norm_post, wd2, 0.5, gu2, "ffn2_out_bwd")
    (dx2, dg_pre2), _ = _nt_pre_bwd(dgu2, wg2, x2, ffn2_norm_pre, dy, "ffn2_in_bwd")
    dwd2 = _wgrad(h2, df2.reshape(1, t, D), True, False, D, "ffn2_wd_grad")
    dwg2 = _wgrad(dgu2.reshape(8, t, FB), xn2.reshape(1, t, D), True, False, D, "ffn2_wg_grad")
    (dfm, dg_postm, dcat), _ = _post_bwd_nt(dx2, fm, mix_norm_post, wout_full, 1.0, None, "mix_out_bwd")
    dwout = _wgrad(cat.reshape(1, t, D), dfm.reshape(1, t, D), False, False, D, "wout_grad")
    dcat = dcat.reshape(t, D)
    (dy_rec, dr2, dkp2, dv2, dgate, dgn_w, dgn_b, drk), (s_wout,) = _post_bwd(
        dcat, y_rec, rec_in[0], rec_in[2], rec_in[3], g_gate, post_params, "post_bwd", comm=(
            [dwout.reshape(NDEV, D // NDEV, D)], ["scatter"], [SLOT8]))
    dq_c, da_c, dg_c = _r2_bwd(dy_rec, q_c, s_all, a_c, "r2_bwd")
    rec_grads, got = _r1_bwd(rec_in, da_c, dg_c, dq_c, dy_rec, "r1_bwd", comm=(
        [dwg2.reshape(4, 2, FB, D), dwd2.reshape(NDEV, F // NDEV, D)], ["scatter"] * 2, [SLOT42, SLOT8]))
    s_wg2, s_wd2 = got
    prep_out = _prep_bwd(ps, prep_params, rec_grads, (dr2, dkp2, dv2, dgate), "prep_bwd")
    dpsl, dw0, da0, dkk, dka, dwup_pad, daup_pad, dgup = prep_out
    dc_conv, dlnw, dlnb, dcb = _conv_bwd1(dcat, c_conv, conv_ln_w, conv_ln_b, "conv_bwd1")
    dpc, dcdw = _conv_bwd2(dc_conv, pc, cdw_full, "conv_bwd2")
    dp, dmu = _shift_bwd(dpsl, ps, shift_mu, dpc, "shift_bwd")
    (dx1, dg_prem), _ = _nt_pre_bwd(dp.reshape(1, 1, t, DIN), win_t.reshape(1, 1, DIN, D), x1, mix_norm_pre, dx2,
                                    "mix_in_bwd")
    dwin_s = _wgrad(dp.reshape(1, t, DIN), hm.reshape(1, t, D), False, False, D // 2, "win_grad").reshape(
        NDEV, DIN // NDEV, D)
    (df1, dg_post1, dgu1), (s_win,) = _post_bwd_nt(dx1, f1, ffn1_norm_post, wd1, 0.5, gu1, "ffn1_out_bwd", comm=(
        [dwin_s], ["scatter"], [SLOT8]))
    dwd1 = _wgrad(h1, df1.reshape(1, t, D), True, False, D, "ffn1_wd_grad")
    dwg1, (s_wd1,) = _wgrad(dgu1.reshape(8, t, FB), xn1.reshape(1, t, D), True, False, D, "ffn1_wg_grad", comm=(
        [dwd1.reshape(NDEV, F // NDEV, D)], ["scatter"], [SLOT8]))
    (dx0, dg_pre1), (s_wg1,) = _nt_pre_bwd(dgu1, wg1, x0, ffn1_norm_pre, dx1, "ffn1_in_bwd", comm=(
        [dwg1.reshape(4, 2, FB, D)], ["scatter"], [SLOT42]))

    rep_shapes = [(1, 1)] + [(1, D)] * 6 + [(1, DS)] + [(1, DR)] * 10
    rep_parts = [loss_part, dg_pre1, dg_post1, dg_prem, dg_postm, dg_pre2, dg_post2, dmu,
                 dw0, da0, dkk, dka, drk, dgn_w, dgn_b, dcb, dlnw, dlnb]
    n_rep = _round_up(sum(_round_up(math.prod(s), LANES) for s in rep_shapes), 8 * LANES)
    sh_shapes = [(HN, HN), (HN, HN), (LANES, HN), (CW, HN)]
    n_sh = _round_up(sum(_round_up(math.prod(s), LANES) for s in sh_shapes), 8 * LANES)
    sh_parts = []
    for a in (dwup_pad[:HN], daup_pad[HN:], dgup, dcdw):
        rows = jnp.transpose(a.reshape(a.shape[0], NDEV, HN), (1, 0, 2)).reshape(NDEV, a.shape[0] * HN)
        sh_parts.append(jnp.pad(rows, ((0, 0), (0, (-rows.shape[1]) % LANES))))
    sh_vec = jnp.concatenate(sh_parts, axis=1)
    sh_vec = jnp.pad(sh_vec, ((0, 0), (0, n_sh - sh_vec.shape[1]))).reshape(NDEV, n_sh // LANES, LANES)
    s_rep, s_sh = _exchange([_pack(rep_parts, n_rep), sh_vec], ["gather", "scatter"], [SLOT8, SLOT8], "exchange_small")

    res = {}
    for nm, w, s, m, v, transposed in (
            ("ffn1_w_gu", ffn1_w_gu, s_wg1.reshape(NDEV, FB, D), m_ffn1_w_gu, v_ffn1_w_gu, True),
            ("ffn1_w_down", ffn1_w_down, s_wd1, m_ffn1_w_down, v_ffn1_w_down, False),
            ("w_in", w_in, s_win, m_w_in, v_w_in, True),
            ("w_out", w_out, s_wout, m_w_out, v_w_out, False),
            ("ffn2_w_gu", ffn2_w_gu, s_wg2.reshape(NDEV, FB, D), m_ffn2_w_gu, v_ffn2_w_gu, True),
            ("ffn2_w_down", ffn2_w_down, s_wd2, m_ffn2_w_down, v_ffn2_w_down, False)):
        view = (lambda a: jnp.swapaxes(a[0], 0, 1)) if transposed else (lambda a: a[0])
        outs = _adam_slots(view(w), s, view(m), view(v), "adam_" + nm)
        res[nm] = [(jnp.swapaxes(o, 0, 1) if transposed else o)[None] for o in outs]

    gsum = _unpack(_sum_slots(s_rep, "sum_rep"), rep_shapes)
    gsh = _unpack(_sum_slots(s_sh, "sum_sharded"), sh_shapes)
    loss = gsum[0].reshape(())
    rep_names = ["ffn1_norm_pre", "ffn1_norm_post", "mix_norm_pre", "mix_norm_post", "ffn2_norm_pre", "ffn2_norm_post",
                 "shift_mu", "w0", "a0", "k_k", "k_a", "r_k", "gn_w", "gn_b", "conv_b", "conv_ln_w", "conv_ln_b"]
    shard_names = ["w_up", "a_up", "g_up", "conv_dw"]
    env = dict(locals())
    small_g = {n: gsum[1 + i] for i, n in enumerate(rep_names)}
    small_g["r_k"] = small_g["r_k"].reshape(1, NH, HN)
    for i, n in enumerate(shard_names):
        small_g[n] = gsh[i][None]
    names = rep_names + shard_names
    shapes = [env[n].shape for n in names]

    def flat2(a):
        return a.reshape(1, DR) if a.shape == (1, NH, HN) else a.reshape(a.shape[-2:])

    deltas, new_ms, new_vs = _adam_many(*[[flat2(d[n]) for n in names] for d in (
        {n: env[n] for n in names}, small_g, {n: env["m_" + n] for n in names}, {n: env["v_" + n] for n in names})],
        "adam_small")
    for n, shp, dl, nm_, nv_ in zip(names, shapes, deltas, new_ms, new_vs):
        res[n] = [small_g[n], dl.reshape(shp), nm_.reshape(shp), nv_.reshape(shp)]

    order = ["ffn1_norm_pre", "ffn1_norm_post", "ffn1_w_gu", "ffn1_w_down", "mix_norm_pre", "mix_norm_post", "w_in",
             "shift_mu", "w_up", "w0", "a_up", "a0", "g_up", "k_k", "k_a", "r_k", "gn_w", "gn_b", "conv_dw", "conv_b",
             "conv_ln_w", "conv_ln_b", "w_out", "ffn2_norm_pre", "ffn2_norm_post", "ffn2_w_gu", "ffn2_w_down"]
    return (loss, dx0.reshape(1, t, D), *[res[n][0] for n in order], *[res[n][1] for n in order],
            *[res[n][2] for n in order], *[res[n][3] for n in order])
```

```python
import functools
import math

import jax
import jax.numpy as jnp
from jax import lax
from jax.experimental import pallas as pl
from jax.experimental.pallas import tpu as pltpu

f32 = jnp.float32
bf16 = jnp.bfloat16

D = 1024
F = 2816
FB = 704
DR = 512
DC = 512
NH = 8
HN = 64
DS = 1792
DIN = 2816
CW = 31
CHUNK = 64
R1_CHUNKS = 4
R2_CHUNKS = 4
NDEV = 8
RMS_EPS = 1e-6
GN_EPS = 64e-5
LN_EPS = 1e-5
DECAY_SCALE = math.exp(-0.5)
ADAM_LR, ADAM_B1, ADAM_B2, ADAM_EPS, ADAM_WD, ADAM_STEP = 0.001, 0.9, 0.999, 1e-08, 0.01, 10
LANES = 128
HALO = 32
CONV_ROWS = 32
TM_MXU = 1024
TM_SUB = 256
TM_VPU = 256

ANY = pl.BlockSpec(memory_space=pl.ANY)


def _full(shape):
    return pl.BlockSpec(shape, lambda *_: (0,) * len(shape))


def _sds(shape, dtype=f32):
    return jax.ShapeDtypeStruct(shape, dtype)


def _dot(a, b):
    return jnp.dot(a, b, preferred_element_type=f32)


def _dot_nt(a, b):
    return lax.dot_general(a, b, (((1,), (1,)), ((), ())), preferred_element_type=f32)


def _dot_tn(a, b):
    return lax.dot_general(a, b, (((0,), (0,)), ((), ())), preferred_element_type=f32)


def _terms(a, n):
    out, rem = [], a
    for i in range(n):
        t = rem.astype(bf16)
        out.append(t)
        if i + 1 < n:
            rem = rem - t.astype(f32)
    return out


def _mm_raw(a, b, kind, mode):
    nb = a.ndim - 2
    bd = tuple(range(nb))
    ca = nb if kind == "tn" else nb + 1
    cb = nb + 1 if kind == "nt" else nb
    dn = (((ca,), (cb,)), (bd, bd))
    pa, pb = mode[:2]
    ta, tb = _terms(a, pa), _terms(b, pb)
    acc = None
    for i in range(pa):
        for j in range(pb):
            if i + j < max(pa, pb):
                p = lax.dot_general(ta[i], tb[j], dn, preferred_element_type=f32)
                acc = p if acc is None else acc + p
    return acc


@functools.partial(jax.custom_vjp, nondiff_argnums=(2, 3))
def mm(a, b, kind, mode):
    return _mm_raw(a, b, kind, mode)


def _mm_fwd(a, b, kind, mode):
    return _mm_raw(a, b, kind, mode), (a, b)


def _ct_terms(keep):
    return EXACT_TERMS if keep == 1 else keep


def _mm_bwd(kind, mode, res, ct):
    a, b = res
    pa, pb = mode[:2]
    if len(mode) == 4:
        ca = cb = mode[2]
        pa = pb = mode[3]
    elif mode == (1, 1):
        ca = cb = 1
    else:
        ca, cb = _ct_terms(pb), _ct_terms(pa)
    if kind == "nn":
        da = mm(ct, b, "nt", (ca, pb))
        db = mm(a, ct, "tn", (pa, cb))
    elif kind == "nt":
        da = mm(ct, b, "nn", (ca, pb))
        db = mm(ct, a, "tn", (cb, pa))
    else:
        da = mm(b, ct, "nt", (pb, ca))
        db = mm(a, ct, "nn", (pa, cb))
    return da, db


mm.defvjp(_mm_fwd, _mm_bwd)

X3 = (2, 2)
EXACT_TERMS = 2
EA = (1, EXACT_TERMS)
EB = (EXACT_TERMS, 1)
X1 = (1, 1)
RGM = (1, 1, 1, 2)
RIM = X1
RWM = X1
RAM = X1
R2M = X1
LORA = X1


def _rms_fwd(x, g):
    inv = lax.rsqrt(jnp.mean(x * x, axis=-1, keepdims=True) + RMS_EPS)
    return x * inv * g


def _rms_bwd(x, g, dy):
    inv = lax.rsqrt(jnp.mean(x * x, axis=-1, keepdims=True) + RMS_EPS)
    xh = x * inv
    dxh = dy * g
    dg = jnp.sum(dy * xh, axis=0, keepdims=True)
    dx = inv * (dxh - xh * jnp.mean(dxh * xh, axis=-1, keepdims=True))
    return dx, dg


def _tile(t, want):
    return min(t, want)


NPEER = NDEV - 1


def _comm_out_shape(arrays, modes, slots):
    out = []
    for a, mode, (lead, _) in zip(arrays, modes, slots):
        shp = tuple(lead) + tuple(a.shape) if mode == "gather" else tuple(a.shape)
        out.append(pltpu.HBM(shp, a.dtype))
    return out


def _comm_sems(n):
    return [pltpu.SemaphoreType.DMA((n * NPEER,)), pltpu.SemaphoreType.DMA((n * NPEER,)), pltpu.SemaphoreType.DMA((n,))]


RELAYED = (3, 5, 7)


def _comm_copies(ins, outs, sems, modes, slots, want):
    n = len(ins)
    send_sems, recv_sems, loc_sems = sems
    x, y, c = lax.axis_index("x"), lax.axis_index("y"), lax.axis_index("c")
    me = 4 * x + 2 * y + c

    def peer_of(k):
        px = 1 - x if k & 4 else x
        py = 1 - y if k & 2 else y
        pc = 1 - c if k & 1 else c
        return (px, py, pc), 4 * px + 2 * py + pc

    def slot(i, dev):
        return outs[i].at[slots[i][1](dev)]

    if want == "local":
        return [pltpu.make_async_copy(ins[i] if modes[i] == "gather" else ins[i].at[slots[i][1](me)], slot(i, me),
                                      loc_sems.at[i]) for i in range(n)]
    sibling = peer_of(1)[0]
    out = []
    for k in range(1, NDEV):
        peer, pid = peer_of(k)
        for i in range(n):
            sem = dict(send_sem=send_sems.at[i * NPEER + k - 1], recv_sem=recv_sems.at[i * NPEER + k - 1],
                       device_id_type=pl.DeviceIdType.MESH)
            relayed = modes[i] == "gather" and k in RELAYED
            if want == "recv":
                src = ins[i] if modes[i] == "gather" else ins[i].at[slots[i][1](pid)]
                out.append(pltpu.make_async_remote_copy(src_ref=src, dst_ref=slot(i, pid), device_id=peer, **sem))
            elif want == "first" and not relayed:
                src = ins[i] if modes[i] == "gather" else ins[i].at[slots[i][1](pid)]
                out.append(pltpu.make_async_remote_copy(src_ref=src, dst_ref=slot(i, me), device_id=peer, **sem))
            elif want == "relay" and relayed:
                origin = peer_of(k - 1)[1]
                out.append(((k - 2) * n + i, pltpu.make_async_remote_copy(
                    src_ref=slot(i, origin), dst_ref=slot(i, origin), device_id=sibling, **sem)))
    return out


def _comm_start(ins, outs, sems, modes, slots):
    for cp in _comm_copies(ins, outs, sems, modes, slots, "local") + _comm_copies(ins, outs, sems, modes, slots, "first"):
        cp.start()


def _comm_wait(ins, outs, sems, modes, slots):
    recvs = _comm_copies(ins, outs, sems, modes, slots, "recv")
    relays = _comm_copies(ins, outs, sems, modes, slots, "relay")
    for idx, cp in relays:
        recvs[idx].wait_recv()
        cp.start()
    passed_on = {idx for idx, _ in relays}
    for idx, cp in enumerate(recvs):
        if idx not in passed_on:
            cp.wait_recv()
    for cp in _comm_copies(ins, outs, sems, modes, slots, "first") + [cp for _, cp in relays]:
        cp.wait_send()
    for cp in _comm_copies(ins, outs, sems, modes, slots, "local"):
        cp.wait()


def _exchange(arrays, modes, slots, name):
    n = len(arrays)

    def body(*refs):
        ins, outs, sems = refs[:n], refs[n:2 * n], refs[2 * n:]
        _comm_start(ins, outs, sems, modes, slots)
        _comm_wait(ins, outs, sems, modes, slots)

    return pl.pallas_call(
        body, name=name, out_shape=_comm_out_shape(arrays, modes, slots),
        in_specs=[ANY] * n, out_specs=[ANY] * n, scratch_shapes=_comm_sems(n),
    )(*arrays)


HBM_SPEC = pl.BlockSpec(memory_space=pltpu.HBM)
SEM_SPEC = pl.BlockSpec(memory_space=pltpu.SEMAPHORE)


def _scatter_start(arr, slot, name):
    def body(v_ref, land_ref, send_sems, recv_sems, v_thru, land_thru, token):
        for cp in _comm_copies([v_ref], [land_ref], (send_sems, recv_sems, None), ["scatter"], [slot], "first"):
            cp.start()
        token[...] = jnp.zeros_like(token)

    return pl.pallas_call(
        body, name=name,
        out_shape=(pltpu.SemaphoreType.DMA((NPEER,)), pltpu.SemaphoreType.DMA((NPEER,)),
                   pltpu.HBM(arr.shape, arr.dtype), pltpu.HBM(arr.shape, arr.dtype), _sds((8, LANES))),
        in_specs=(HBM_SPEC, HBM_SPEC),
        out_specs=(SEM_SPEC, SEM_SPEC, HBM_SPEC, HBM_SPEC, pl.BlockSpec(memory_space=pltpu.VMEM)),
        input_output_aliases={0: 2, 1: 3},
        compiler_params=pltpu.CompilerParams(has_side_effects=pltpu.SideEffectType.DATAFLOW_SIDE_EFFECTING),
    )(pltpu.with_memory_space_constraint(arr, pltpu.HBM),
      pltpu.with_memory_space_constraint(lax.empty(arr.shape, arr.dtype), pltpu.HBM))


def _scatter_wait(send_sems, recv_sems, v_thru, land_thru, after, slot, name):
    def body(v_ref, land_ref, send_sems, recv_sems, after_ref, v_dead, got_ref):
        sems = (send_sems, recv_sems, None)
        for cp in _comm_copies([v_ref], [land_ref], sems, ["scatter"], [slot], "first"):
            cp.wait_send()
        for cp in _comm_copies([v_ref], [land_ref], sems, ["scatter"], [slot], "recv"):
            cp.wait_recv()

    return pl.pallas_call(
        body, name=name, out_shape=(pltpu.HBM(v_thru.shape, v_thru.dtype), pltpu.HBM(v_thru.shape, v_thru.dtype)),
        in_specs=(HBM_SPEC, HBM_SPEC, SEM_SPEC, SEM_SPEC, ANY), out_specs=(HBM_SPEC, HBM_SPEC),
        input_output_aliases={0: 0, 1: 1},
        compiler_params=pltpu.CompilerParams(has_side_effects=pltpu.SideEffectType.DATAFLOW_SIDE_EFFECTING),
    )(v_thru, land_thru, send_sems, recv_sems, after)


def _pcall(body, *, name, grid, in_specs, out_specs, out_shape, args, scratch_shapes=(), comm=None):
    if comm is None:
        return pl.pallas_call(body, name=name, grid=grid, in_specs=in_specs, out_specs=out_specs, out_shape=out_shape,
                              scratch_shapes=list(scratch_shapes))(*args), None
    arrays, modes, slots = comm
    n_in, n_out, n_scr, nc = len(args), len(out_shape), len(scratch_shapes), len(arrays)

    def hosted(*refs):
        a_in, c_in = refs[:n_in], refs[n_in:n_in + nc]
        o = n_in + nc
        a_out, c_out = refs[o:o + n_out], refs[o + n_out:o + n_out + nc]
        o += n_out + nc
        a_scr, sems = refs[o:o + n_scr], refs[o + n_scr:]
        first = pl.program_id(0) == 0
        last = pl.program_id(0) == grid[0] - 1
        for ax in range(1, len(grid)):
            first = jnp.logical_and(first, pl.program_id(ax) == 0)
            last = jnp.logical_and(last, pl.program_id(ax) == grid[ax] - 1)

        @pl.when(first)
        def _():
            _comm_start(c_in, c_out, sems, modes, slots)

        body(*a_in, *a_out, *a_scr)

        @pl.when(last)
        def _():
            _comm_wait(c_in, c_out, sems, modes, slots)

    res = pl.pallas_call(
        hosted, name=name, grid=grid, in_specs=list(in_specs) + [ANY] * nc, out_specs=list(out_specs) + [ANY] * nc,
        out_shape=list(out_shape) + _comm_out_shape(arrays, modes, slots),
        scratch_shapes=list(scratch_shapes) + _comm_sems(nc),
    )(*args, *arrays)
    return res[:n_out], res[n_out:]


SLOT8 = ((NDEV,), lambda d: (d,))
SLOT42 = ((4, 2), lambda d: (lax.rem(d, 4), lax.div(d, 4)))


def _ffn_in_fwd(x, g, wg4, name, comm=None):
    t = x.shape[0]
    tm = _tile(t, TM_MXU)
    sub = _tile(tm, TM_SUB)

    def body(x_ref, g_ref, w_ref, xn_ref, gu_ref, h_ref):
        for s in range(tm // sub):
            rows = pl.ds(s * sub, sub)
            xn = _rms_fwd(x_ref[rows, :], g_ref[...]).astype(bf16)
            xn_ref[rows, :] = xn
            gate = _dot_nt(xn, w_ref[0, 0])
            up = _dot_nt(xn, w_ref[0, 1])
            gu_ref[0, 0, rows, :] = gate.astype(bf16)
            gu_ref[0, 1, rows, :] = up.astype(bf16)
            h_ref[0, rows, :] = (gate * jax.nn.sigmoid(gate) * up).astype(bf16)

    return _pcall(
        body, name=name, grid=(t // tm, 4),
        in_specs=[pl.BlockSpec((tm, D), lambda i, e: (i, 0)), _full((1, D)),
                  pl.BlockSpec((1, 2, FB, D), lambda i, e: (e, 0, 0, 0))],
        out_specs=[pl.BlockSpec((tm, D), lambda i, e: (i, 0)),
                   pl.BlockSpec((1, 2, tm, FB), lambda i, e: (e, 0, i, 0)),
                   pl.BlockSpec((1, tm, FB), lambda i, e: (e, i, 0))],
        out_shape=[_sds((t, D), bf16), _sds((4, 2, t, FB), bf16), _sds((4, t, FB), bf16)],
        args=(x, g, wg4), comm=comm)


def _mm_norm_res_fwd(hb, wb, xres, g, scale, name, comm=None, tgt=None):
    e_n, t, k = hb.shape
    tm = _tile(t, TM_MXU)

    def body(h_ref, w_ref, x_ref, g_ref, *rest):
        f = _dot(h_ref[0], w_ref[0])
        for e in range(1, e_n):
            f = f + _dot(h_ref[e], w_ref[e])
        xnew = x_ref[...] + scale * _rms_fwd(f, g_ref[...])
        if tgt is None:
            f_ref, o_ref = rest
            o_ref[...] = xnew
        else:
            t_ref, f_ref, o_ref, l_ref = rest
            err = xnew - t_ref[...]
            o_ref[...] = err * (1.0 / D)
            part = 0.5 * jnp.sum(jnp.mean(err * err, axis=-1, keepdims=True), axis=0, keepdims=True)

            @pl.when(pl.program_id(0) == 0)
            def _():
                l_ref[...] = part

            @pl.when(pl.program_id(0) != 0)
            def _():
                l_ref[...] += part
        f_ref[...] = f

    tile = pl.BlockSpec((tm, D), lambda i: (i, 0))
    outs, got = _pcall(
        body, name=name, grid=(t // tm,),
        in_specs=[pl.BlockSpec((e_n, tm, k), lambda i: (0, i, 0)), _full((e_n, k, D)), tile, _full((1, D))]
        + ([] if tgt is None else [tile]),
        out_specs=[tile, tile] + ([] if tgt is None else [_full((1, 1))]),
        out_shape=[_sds((t, D)), _sds((t, D))] + ([] if tgt is None else [_sds((1, 1))]),
        args=(hb, wb, xres, g) + (() if tgt is None else (tgt,)), comm=comm)
    return outs if comm is None else (outs, got)


def _post_bwd_nt(dxn, f, g, wb, scale, gu, name, comm=None):
    e_n, k, _ = wb.shape
    t = f.shape[0]
    tm = _tile(t, TM_MXU)
    sub = _tile(tm, TM_SUB)
    swiglu = gu is not None

    def body(*refs):
        if swiglu:
            dx_ref, f_ref, g_ref, w_ref, gu_ref, df_ref, dg_ref, dh_ref, df_s = refs
        else:
            dx_ref, f_ref, g_ref, w_ref, df_ref, dg_ref, dh_ref, df_s = refs
        i, e = pl.program_id(0), pl.program_id(1)

        @pl.when(e == 0)
        def _():
            df, dg = _rms_bwd(f_ref[...], g_ref[...], scale * dx_ref[...])
            df_s[...] = df.astype(bf16)
            df_ref[...] = df_s[...]

            @pl.when(i == 0)
            def _():
                dg_ref[...] = dg

            @pl.when(i != 0)
            def _():
                dg_ref[...] += dg

        for s in range(tm // sub):
            rows = pl.ds(s * sub, sub)
            dh = _dot_nt(df_s[rows, :], w_ref[0])
            if swiglu:
                gate = gu_ref[0, 0, rows, :].astype(f32)
                up = gu_ref[0, 1, rows, :].astype(f32)
                sg = jax.nn.sigmoid(gate)
                dh_ref[0, 0, rows, :] = (dh * up * (sg * (1.0 + gate * (1.0 - sg)))).astype(bf16)
                dh_ref[0, 1, rows, :] = (dh * gate * sg).astype(bf16)
            else:
                dh_ref[0, rows, :] = dh

    in_specs = [pl.BlockSpec((tm, D), lambda i, e: (i, 0)), pl.BlockSpec((tm, D), lambda i, e: (i, 0)), _full((1, D)),
                pl.BlockSpec((1, k, D), lambda i, e: (e, 0, 0))]
    args = [dxn, f, g, wb]
    if swiglu:
        in_specs.append(pl.BlockSpec((1, 2, tm, k), lambda i, e: (e, 0, i, 0)))
        args.append(gu)
        dh_spec = pl.BlockSpec((1, 2, tm, k), lambda i, e: (e, 0, i, 0))
        dh_shape = _sds((e_n, 2, t, k), bf16)
    else:
        dh_spec = pl.BlockSpec((1, tm, k), lambda i, e: (e, i, 0))
        dh_shape = _sds((e_n, t, k), f32)
    return _pcall(
        body, name=name, grid=(t // tm, e_n), in_specs=in_specs,
        out_specs=[pl.BlockSpec((tm, D), lambda i, e: (i, 0)), _full((1, D)), dh_spec],
        out_shape=[_sds((t, D), bf16), _sds((1, D)), dh_shape],
        scratch_shapes=[pltpu.VMEM((tm, D), bf16)], args=args, comm=comm)


def _nt_pre_bwd(dy, wb, x, g, dres, name, comm=None):
    e_n, q_n, t, k = dy.shape
    tm = _tile(t, TM_MXU)

    def body(dy_ref, w_ref, x_ref, g_ref, r_ref, dx_ref, dg_ref, acc):
        i, e = pl.program_id(0), pl.program_id(1)
        p = _dot(dy_ref[0, 0], w_ref[0, 0])
        for q in range(1, q_n):
            p = p + _dot(dy_ref[0, q], w_ref[0, q])

        @pl.when(e == 0)
        def _():
            acc[...] = p

        @pl.when(e != 0)
        def _():
            acc[...] += p

        @pl.when(e == e_n - 1)
        def _():
            dx, dg = _rms_bwd(x_ref[...], g_ref[...], acc[...])
            dx_ref[...] = r_ref[...] + dx

            @pl.when(i == 0)
            def _():
                dg_ref[...] = dg

            @pl.when(i != 0)
            def _():
                dg_ref[...] += dg

    return _pcall(
        body, name=name, grid=(t // tm, e_n),
        in_specs=[pl.BlockSpec((1, q_n, tm, k), lambda i, e: (e, 0, i, 0)),
                  pl.BlockSpec((1, q_n, k, D), lambda i, e: (e, 0, 0, 0)),
                  pl.BlockSpec((tm, D), lambda i, e: (i, 0)), _full((1, D)),
                  pl.BlockSpec((tm, D), lambda i, e: (i, 0))],
        out_specs=[pl.BlockSpec((tm, D), lambda i, e: (i, 0)), _full((1, D))],
        out_shape=[_sds((t, D)), _sds((1, D))],
        scratch_shapes=[pltpu.VMEM((tm, D), f32)], args=(dy, wb, x, g, dres), comm=comm)


def _wgrad(a, b, a_batched, b_batched, tn, name, comm=None):
    t, k = a.shape[1], a.shape[2]
    nn = b.shape[2]
    nb = max(a.shape[0], b.shape[0])
    tt = _tile(t, TM_MXU)
    nt = t // tt

    def body(a_ref, b_ref, o_ref, acc):
        @pl.when(pl.program_id(2) == 0)
        def _():
            acc[...] = _dot_tn(a_ref[0], b_ref[0])

        @pl.when(pl.program_id(2) != 0)
        def _():
            acc[...] += _dot_tn(a_ref[0], b_ref[0])

        @pl.when(pl.program_id(2) == nt - 1)
        def _():
            o_ref[0] = acc[...].astype(bf16)

    (out,), got = _pcall(
        body, name=name, grid=(nb, nn // tn, nt),
        in_specs=[pl.BlockSpec((1, tt, k), (lambda n, j, s: (n, s, 0)) if a_batched else (lambda n, j, s: (0, s, 0))),
                  pl.BlockSpec((1, tt, tn), (lambda n, j, s: (n, s, j)) if b_batched else (lambda n, j, s: (0, s, j)))],
        out_specs=[pl.BlockSpec((1, k, tn), lambda n, j, s: (n, 0, j))],
        out_shape=[_sds((nb, k, nn), bf16)],
        scratch_shapes=[pltpu.VMEM((k, tn), f32)], args=(a, b), comm=comm)
    return out if comm is None else (out, got)


def _norm_mm_fwd(x, g, w, name):
    t = x.shape[0]
    tm = _tile(t, TM_VPU)

    def body(x_ref, g_ref, w_ref, xn_ref, ps_ref, pc_ref):
        xn = _rms_fwd(x_ref[...], g_ref[...]).astype(bf16)
        xn_ref[...] = xn
        ps_ref[...] = _dot_nt(xn, w_ref[:DS, :])
        pc_ref[...] = _dot_nt(xn, w_ref[DS:, :])

    return pl.pallas_call(
        body, name=name, grid=(t // tm,),
        in_specs=[pl.BlockSpec((tm, D), lambda i: (i, 0)), _full((1, D)), _full((DIN, D))],
        out_specs=[pl.BlockSpec((tm, D), lambda i: (i, 0)), pl.BlockSpec((tm, DS), lambda i: (i, 0)),
                   pl.BlockSpec((tm, 2 * DC), lambda i: (i, 0))],
        out_shape=[_sds((t, D), bf16), _sds((t, DS)), _sds((t, 2 * DC))],
    )(x, g, w)


def _segsum_b(x, seg, seg_t):
    return mm(mm(x, seg, "nn", EB), seg_t, "nn", EB)


def _prep_fn(psl, w0, a0, k_k, k_a, wup, aup, gup, seg, seg_t):
    r, k, v = psl[:, :DR], psl[:, DR:2 * DR], psl[:, 2 * DR:3 * DR]
    xwa, xg = psl[:, 3 * DR:3 * DR + LANES], psl[:, 3 * DR + LANES:]
    d = w0 + mm(jnp.tanh(xwa), wup, "nn", LORA)
    lw = -DECAY_SCALE * jax.nn.sigmoid(d)
    a = jax.nn.sigmoid(a0 + mm(xwa, aup, "nn", LORA))
    g = mm(jax.nn.sigmoid(xg), gup, "nn", LORA)
    kkr = k * k_k
    kk = kkr * lax.rsqrt(jnp.maximum(_segsum_b(kkr * kkr, seg, seg_t), 1e-12))
    kp = k * (1.0 + (a - 1.0) * k_a)
    return r, lw, kp, v, -kk, kk * a, g


def _shifted(ps, halo_row, first):
    prev = jnp.where(first, 0.0, halo_row)
    sh = pltpu.roll(ps, 1, 0)
    row = lax.broadcasted_iota(jnp.int32, ps.shape, 0)
    return jnp.where(row == 0, prev, sh)


def _heads_of(ref, rows):
    return jnp.stack([ref[rows, h * HN:(h + 1) * HN] for h in range(NH)], axis=0)


def _heads_to(ref, rows, val):
    ref[rows, :] = jnp.concatenate([val[h] for h in range(NH)], axis=-1)


_PREP_PARAM_SHAPES = [(1, DS), (1, DR), (1, DR), (1, DR), (1, DR), (LANES, DR), (LANES, DR), (LANES, DR),
                      (DR, LANES), (LANES, DR)]


def _prep_fwd(ps, params, name, comm=None):
    t = ps.shape[0]
    tm = _tile(t, TM_VPU)
    hb = tm // 8

    def body(ps_ref, halo_ref, mu_ref, *rest):
        prm = [r[...] for r in rest[:9]]
        outs = rest[9:]
        x = ps_ref[...]
        sh = _shifted(x, halo_ref[pl.ds(7, 1), :], pl.program_id(0) == 0)
        psl = x + (sh - x) * mu_ref[...]
        vals = _prep_fn(psl, *prm)
        for ref, val in zip(outs, vals):
            ref[...] = val

    outs, got = _pcall(
        body, name=name, grid=(t // tm,),
        in_specs=[pl.BlockSpec((tm, DS), lambda i: (i, 0)),
                  pl.BlockSpec((8, DS), lambda i: (jnp.maximum(i * hb - 1, 0), 0))]
        + [_full(s) for s in _PREP_PARAM_SHAPES],
        out_specs=[pl.BlockSpec((tm, DR), lambda i: (i, 0))] * 7,
        out_shape=[_sds((t, DR))] * 7, args=(ps, ps, *params), comm=comm)
    return outs if comm is None else (outs, got)


def _prep_bwd(ps, params, cts_hm, cts_tm, name):
    t = ps.shape[0]
    tm = _tile(t, TM_VPU)
    hb = tm // 8

    def body(ps_ref, halo_ref, mu_ref, *rest):
        prm = [r[...] for r in rest[:9]]
        chm = rest[9:15]
        ctm = rest[15:19]
        dpsl_ref = rest[19]
        gouts = rest[20:27]
        x = ps_ref[...]
        sh = _shifted(x, halo_ref[pl.ds(7, 1), :], pl.program_id(0) == 0)
        psl = x + (sh - x) * mu_ref[...]
        seg, seg_t = prm[7], prm[8]
        _, vjp = jax.vjp(lambda p, *w: _prep_fn(p, *w, seg, seg_t), psl, *prm[:7])
        c = [r[...] for r in chm]
        cts = (c[0] + ctm[0][...], c[1], c[2] + ctm[1][...], c[3] + ctm[2][...], c[4], c[5], ctm[3][...])
        grads = vjp(cts)
        dpsl_ref[...] = grads[0]
        for ref, gval in zip(gouts, grads[1:]):
            @pl.when(pl.program_id(0) == 0)
            def _(ref=ref, gval=gval):
                ref[...] = gval

            @pl.when(pl.program_id(0) != 0)
            def _(ref=ref, gval=gval):
                ref[...] += gval

    tk = pl.BlockSpec((tm, DR), lambda i: (i, 0))
    gshapes = _PREP_PARAM_SHAPES[1:8]
    return pl.pallas_call(
        body, name=name, grid=(t // tm,),
        in_specs=[pl.BlockSpec((tm, DS), lambda i: (i, 0)),
                  pl.BlockSpec((8, DS), lambda i: (jnp.maximum(i * hb - 1, 0), 0))]
        + [_full(s) for s in _PREP_PARAM_SHAPES] + [tk] * 10,
        out_specs=[pl.BlockSpec((tm, DS), lambda i: (i, 0))] + [_full(s) for s in gshapes],
        out_shape=[_sds((t, DS))] + [_sds(s) for s in gshapes],
    )(ps, ps, *params, *cts_hm, *cts_tm)


def _shift_bwd(dpsl, ps, mu, dpc, name):
    t = ps.shape[0]
    tm = _tile(t, TM_VPU)
    hb = tm // 8
    last_blk = t // 8 - 1

    def body(d_ref, dn_ref, ps_ref, halo_ref, mu_ref, dpc_ref, dp_ref, dmu_ref):
        i = pl.program_id(0)
        mu_v = mu_ref[...]
        d = d_ref[...]
        nxt = jnp.where(i == pl.num_programs(0) - 1, 0.0, dn_ref[pl.ds(0, 1), :])
        up = pltpu.roll(d, tm - 1, 0)
        row = lax.broadcasted_iota(jnp.int32, d.shape, 0)
        up = jnp.where(row == tm - 1, nxt, up)
        dp_ref[:, :DS] = (d * (1.0 - mu_v) + up * mu_v).astype(bf16)
        dp_ref[:, DS:] = dpc_ref[...].astype(bf16)
        x = ps_ref[...]
        sh = _shifted(x, halo_ref[pl.ds(7, 1), :], i == 0)
        dmu = jnp.sum(d * (sh - x), axis=0, keepdims=True)

        @pl.when(i == 0)
        def _():
            dmu_ref[...] = dmu

        @pl.when(i != 0)
        def _():
            dmu_ref[...] += dmu

    return pl.pallas_call(
        body, name=name, grid=(t // tm,),
        in_specs=[pl.BlockSpec((tm, DS), lambda i: (i, 0)),
                  pl.BlockSpec((8, DS), lambda i: (jnp.minimum((i + 1) * hb, last_blk), 0)),
                  pl.BlockSpec((tm, DS), lambda i: (i, 0)),
                  pl.BlockSpec((8, DS), lambda i: (jnp.maximum(i * hb - 1, 0), 0)),
                  _full((1, DS)), pl.BlockSpec((tm, 2 * DC), lambda i: (i, 0))],
        out_specs=[pl.BlockSpec((tm, DIN), lambda i: (i, 0)), _full((1, DS))],
        out_shape=[_sds((t, DIN), bf16), _sds((1, DS))],
    )(dpsl, dpsl, ps, ps, mu, dpc)


def _post_fn(y, r, kp, v, g, gn_w, gn_b, r_k, seg, seg_t):
    mu = _segsum_b(y, seg, seg_t) * (1.0 / HN)
    yc = y - mu
    var = _segsum_b(yc * yc, seg, seg_t) * (1.0 / HN)
    yo = yc * lax.rsqrt(var + GN_EPS) * gn_w + gn_b
    bonus = _segsum_b(r * kp * r_k, seg, seg_t) * v
    return (yo + bonus) * g


_POST_PARAM_SHAPES = [(1, DR), (1, DR), (1, DR), (DR, LANES), (LANES, DR)]


def _post_fwd(y, r, kp, v, g, out_b, params, name):
    t = g.shape[0]
    tm = _tile(t, TM_VPU)

    def body(y_ref, r_ref, k_ref, v_ref, g_ref, ob_ref, *rest):
        prm = [p[...] for p in rest[:5]]
        cat_ref = rest[5]
        oa = _post_fn(y_ref[...], r_ref[...], k_ref[...], v_ref[...], g_ref[...], *prm)
        cat_ref[:, :DR] = oa.astype(bf16)
        cat_ref[:, DR:] = ob_ref[...]

    tk = pl.BlockSpec((tm, DR), lambda i: (i, 0))
    return pl.pallas_call(
        body, name=name, grid=(t // tm,),
        in_specs=[tk] * 6 + [_full(s) for s in _POST_PARAM_SHAPES],
        out_specs=pl.BlockSpec((tm, 2 * DR), lambda i: (i, 0)),
        out_shape=_sds((t, 2 * DR), bf16),
    )(y, r, kp, v, g, out_b, *params)


def _post_bwd(dcat, y, r, kp, v, g, params, name, comm=None):
    t = g.shape[0]
    tm = _tile(t, TM_VPU)

    def body(dc_ref, y_ref, r_ref, k_ref, v_ref, g_ref, *rest):
        prm = [p[...] for p in rest[:5]]
        dy_ref, dr_ref, dk_ref, dv_ref, dg_ref = rest[5:10]
        gouts = rest[10:13]
        seg, seg_t = prm[3], prm[4]
        _, vjp = jax.vjp(lambda *a: _post_fn(*a, seg, seg_t),
                         y_ref[...], r_ref[...], k_ref[...], v_ref[...], g_ref[...], *prm[:3])
        grads = vjp(dc_ref[...])
        dy_ref[...] = grads[0]
        dr_ref[...] = grads[1]
        dk_ref[...] = grads[2]
        dv_ref[...] = grads[3]
        dg_ref[...] = grads[4]
        for ref, gval in zip(gouts, grads[5:]):
            @pl.when(pl.program_id(0) == 0)
            def _(ref=ref, gval=gval):
                ref[...] = gval

            @pl.when(pl.program_id(0) != 0)
            def _(ref=ref, gval=gval):
                ref[...] += gval

    tk = pl.BlockSpec((tm, DR), lambda i: (i, 0))
    return _pcall(
        body, name=name, grid=(t // tm,),
        in_specs=[tk] * 6 + [_full(s) for s in _POST_PARAM_SHAPES],
        out_specs=[tk] * 5 + [_full((1, DR))] * 3,
        out_shape=[_sds((t, DR))] * 5 + [_sds((1, DR))] * 3,
        args=(dcat, y, r, kp, v, g, *params), comm=comm)


@jax.custom_vjp
def _inv_unit_lower(lb):
    c = lb.shape[-1]
    row = lax.broadcasted_iota(jnp.int32, (c, c), 0)
    col = lax.broadcasted_iota(jnp.int32, (c, c), 1)
    p = (row == col).astype(f32)[None] + lb
    m = lb
    for _ in range(c.bit_length() - 2):
        m = mm(m, m, "nn", RIM)
        p = p + mm(m, p, "nn", RIM)
    return p


def _inv_unit_lower_fwd(lb):
    p = _inv_unit_lower(lb)
    return p, p


def _inv_unit_lower_bwd(p, ct):
    return (mm(mm(p, ct, "tn", RIM), p, "nt", RIM),)


_inv_unit_lower.defvjp(_inv_unit_lower_fwd, _inv_unit_lower_bwd)


def _r1_fn(r, lw, k, v, z, b):
    c = r.shape[1]
    row = lax.broadcasted_iota(jnp.int32, (c, c), 0)
    col = lax.broadcasted_iota(jnp.int32, (c, c), 1)
    incl = (row >= col)[None]
    strict = (row > col)[None]
    eye = (row == col).astype(f32)[None]
    tri = jnp.broadcast_to((row >= col).astype(f32)[None], (r.shape[0], c, c))
    cum = mm(tri, lw, "nn", EA)
    tot = jnp.sum(lw, axis=1, keepdims=True)
    zt = z * jnp.exp(cum - lw)
    rt = r * jnp.exp(cum)
    g_inv = jnp.exp(-cum)
    g_rem = jnp.exp(tot - cum)
    bt, kt = b * g_inv, k * g_inv
    bh, kh = b * g_rem, k * g_rem
    lb = jnp.where(strict, mm(zt, bt, "nt", RGM), 0.0)
    lk = jnp.where(strict, mm(zt, kt, "nt", RGM), 0.0)
    mb = jnp.where(incl, mm(rt, bt, "nt", RGM), 0.0)
    mk = jnp.where(incl, mm(rt, kt, "nt", RGM), 0.0)
    p = _inv_unit_lower(lb)
    w1 = mm(p, zt, "nn", RWM)
    w2 = mm(p, mm(lk, v, "nn", RWM), "nn", RWM)
    a = mm(w1, bh, "tn", RAM) + eye * jnp.exp(tot)
    g = mm(w2, bh, "tn", RAM) + mm(v, kh, "tn", RAM)
    q = rt + mm(mb, w1, "nn", RWM)
    yl = mm(mb, w2, "nn", RWM) + mm(mk, v, "nn", RWM)
    return a, g, q, yl


def _chunks_in(ref, n):
    return jnp.concatenate([_heads_of(ref, pl.ds(s * CHUNK, CHUNK)) for s in range(n)], axis=0)


def _chunks_out(ref, val, n):
    for s in range(n):
        _heads_to(ref, pl.ds(s * CHUNK, CHUNK), val[s * NH:(s + 1) * NH])


def _r1_fwd(ins, name, comm=None):
    t = ins[0].shape[0]
    n = R1_CHUNKS
    nc = t // CHUNK

    def body(r, lw, k, v, z, b, a_ref, g_ref, q_ref, yl_ref):
        a, g, q, yl = _r1_fn(*[_chunks_in(x, n) for x in (r, lw, k, v, z, b)])
        for s in range(n):
            a_ref[s] = a[s * NH:(s + 1) * NH]
            g_ref[s] = g[s * NH:(s + 1) * NH]
        _chunks_out(q_ref, q, n)
        _chunks_out(yl_ref, yl, n)

    ck = pl.BlockSpec((n * CHUNK, DR), lambda c: (c, 0))
    st = pl.BlockSpec((n, NH, HN, HN), lambda c: (c, 0, 0, 0))
    return _pcall(
        body, name=name, grid=(nc // n,), in_specs=[ck] * 6, out_specs=[st, st, ck, ck],
        out_shape=[_sds((nc, NH, HN, HN))] * 2 + [_sds((t, DR))] * 2, args=tuple(ins), comm=comm)


def _r1_bwd(ins, da, dg, dq, dyl, name, comm=None):
    t = ins[0].shape[0]
    n = R1_CHUNKS
    nc = t // CHUNK

    def body(r, lw, k, v, z, b, da_ref, dg_ref, dq_ref, dyl_ref, *outs):
        _, vjp = jax.vjp(_r1_fn, *[_chunks_in(x, n) for x in (r, lw, k, v, z, b)])
        cts = (jnp.concatenate([da_ref[s] for s in range(n)], axis=0),
               jnp.concatenate([dg_ref[s] for s in range(n)], axis=0), _chunks_in(dq_ref, n), _chunks_in(dyl_ref, n))
        for ref, gval in zip(outs, vjp(cts)):
            _chunks_out(ref, gval, n)

    ck = pl.BlockSpec((n * CHUNK, DR), lambda c: (c, 0))
    st = pl.BlockSpec((n, NH, HN, HN), lambda c: (c, 0, 0, 0))
    return _pcall(
        body, name=name, grid=(nc // n,), in_specs=[ck] * 6 + [st, st, ck, ck], out_specs=[ck] * 6,
        out_shape=[_sds((t, DR))] * 6, args=(*ins, da, dg, dq, dyl), comm=comm)


def _r2_fwd(a, g, q, yl, name):
    nc = a.shape[0]
    t = q.shape[0]

    def body(a_ref, g_ref, q_ref, yl_ref, y_ref, s_ref, s):
        @pl.when(pl.program_id(0) == 0)
        def _():
            s[...] = jnp.zeros_like(s)
        s0 = s[...]
        for j in range(n):
            rows = pl.ds(j * CHUNK, CHUNK)
            s_ref[j] = s0
            _heads_to(y_ref, rows, mm(_heads_of(q_ref, rows), s0, "nt", R2M) + _heads_of(yl_ref, rows))
            s0 = mm(s0, a_ref[j], "nn", R2M) + g_ref[j]
        s[...] = s0

    n = math.gcd(nc, R2_CHUNKS)
    ck = pl.BlockSpec((n * CHUNK, DR), lambda c: (c, 0))
    st = pl.BlockSpec((n, NH, HN, HN), lambda c: (c, 0, 0, 0))
    return pl.pallas_call(
        body, name=name, grid=(nc // n,), in_specs=[st, st, ck, ck], out_specs=[ck, st],
        out_shape=[_sds((t, DR)), _sds((nc, NH, HN, HN))],
        scratch_shapes=[pltpu.VMEM((NH, HN, HN), f32)],
    )(a, g, q, yl)


def _r2_bwd(dy, q, s_all, a, name):
    nc = a.shape[0]
    t = q.shape[0]

    def body(dy_ref, q_ref, s_ref, a_ref, dq_ref, da_ref, dg_ref, ds):
        @pl.when(pl.program_id(0) == 0)
        def _():
            ds[...] = jnp.zeros_like(ds)
        dsn = ds[...]
        for j in reversed(range(n)):
            rows = pl.ds(j * CHUNK, CHUNK)
            s0 = s_ref[j]
            dyv = _heads_of(dy_ref, rows)
            dg_ref[j] = dsn
            da_ref[j] = mm(s0, dsn, "tn", R2M)
            _heads_to(dq_ref, rows, mm(dyv, s0, "nn", R2M))
            dsn = mm(dsn, a_ref[j], "nt", R2M) + mm(dyv, _heads_of(q_ref, rows), "tn", R2M)
        ds[...] = dsn

    n = math.gcd(nc, R2_CHUNKS)
    nb = nc // n
    ck = pl.BlockSpec((n * CHUNK, DR), lambda c: (nb - 1 - c, 0))
    st = pl.BlockSpec((n, NH, HN, HN), lambda c: (nb - 1 - c, 0, 0, 0))
    return pl.pallas_call(
        body, name=name, grid=(nb,), in_specs=[ck, ck, st, st], out_specs=[ck, st, st],
        out_shape=[_sds((t, DR)), _sds((nc, NH, HN, HN)), _sds((nc, NH, HN, HN))],
        scratch_shapes=[pltpu.VMEM((NH, HN, HN), f32)],
    )(dy, q, s_all, a)


def _ln_silu(c, w, b):
    mu = jnp.mean(c, axis=-1, keepdims=True)
    cc = c - mu
    var = jnp.mean(cc * cc, axis=-1, keepdims=True)
    u = cc * lax.rsqrt(var + LN_EPS) * w + b
    return u * jax.nn.sigmoid(u)


def _glu_tile(pc):
    return pc[:, :DC] * jax.nn.sigmoid(pc[:, DC:])


def _fill_glu(s_ref, pc_ref, halo_ref, first, tm):
    s_ref[pl.ds(0, HALO), :] = jnp.where(first, 0.0, _glu_tile(halo_ref[...]))
    s_ref[pl.ds(HALO, tm), :] = _glu_tile(pc_ref[...])


def _phase_copies(sh):
    n = sh.shape[1] - 8
    for r in range(1, 8):
        sh[r, pl.ds(0, n), :] = sh[0, pl.ds(r, n), :]


def _rows_at(sh, o, n):
    return sh[o % 8, pl.ds(o - o % 8, n), :]


def _conv_fwd(pc, dw, cb, lnw, lnb, name):
    t = pc.shape[0]
    tm = _tile(t, TM_VPU)
    hb = tm // HALO

    def body(pc_ref, halo_ref, dw_ref, cb_ref, w_ref, b_ref, c_ref, o_ref, s):
        _fill_glu(s.at[0], pc_ref, halo_ref, pl.program_id(0) == 0, tm)
        _phase_copies(s)
        for r0 in range(0, tm, CONV_ROWS):
            rows = pl.ds(r0, CONV_ROWS)
            acc = _rows_at(s, r0 + HALO - CW + 1, CONV_ROWS) * dw_ref[pl.ds(0, 1), :]
            for j in range(1, CW):
                acc = acc + _rows_at(s, r0 + HALO - CW + 1 + j, CONV_ROWS) * dw_ref[pl.ds(j, 1), :]
            c = acc + cb_ref[...]
            c_ref[rows, :] = c
            o_ref[rows, :] = _ln_silu(c, w_ref[...], b_ref[...]).astype(bf16)

    return pl.pallas_call(
        body, name=name, grid=(t // tm,),
        in_specs=[pl.BlockSpec((tm, 2 * DC), lambda i: (i, 0)),
                  pl.BlockSpec((HALO, 2 * DC), lambda i: (jnp.maximum(i * hb - 1, 0), 0)),
                  _full((CW, DC)), _full((1, DC)), _full((1, DC)), _full((1, DC))],
        out_specs=[pl.BlockSpec((tm, DC), lambda i: (i, 0)), pl.BlockSpec((tm, DC), lambda i: (i, 0))],
        out_shape=[_sds((t, DC)), _sds((t, DC), bf16)],
        scratch_shapes=[pltpu.VMEM((8, HALO + tm, DC), f32)],
    )(pc, pc, dw, cb, lnw, lnb)


def _conv_bwd1(dcat, c, lnw, lnb, name):
    t = c.shape[0]
    tm = _tile(t, TM_VPU)

    def body(d_ref, c_ref, w_ref, b_ref, dc_ref, dw_ref, db_ref, dcb_ref):
        _, vjp = jax.vjp(_ln_silu, c_ref[...], w_ref[...], b_ref[...])
        dc, dw, db = vjp(d_ref[...])
        dc_ref[...] = dc
        dcb = jnp.sum(dc, axis=0, keepdims=True)
        for ref, gval in ((dw_ref, dw), (db_ref, db), (dcb_ref, dcb)):
            @pl.when(pl.program_id(0) == 0)
            def _(ref=ref, gval=gval):
                ref[...] = gval

            @pl.when(pl.program_id(0) != 0)
            def _(ref=ref, gval=gval):
                ref[...] += gval

    tk = pl.BlockSpec((tm, DC), lambda i: (i, 0))
    return pl.pallas_call(
        body, name=name, grid=(t // tm,),
        in_specs=[pl.BlockSpec((tm, DC), lambda i: (i, 1)), tk, _full((1, DC)), _full((1, DC))],
        out_specs=[tk] + [_full((1, DC))] * 3,
        out_shape=[_sds((t, DC))] + [_sds((1, DC))] * 3,
    )(dcat, c, lnw, lnb)


def _conv_bwd2(dc, pc, dw, name):
    t = pc.shape[0]
    tm = _tile(t, TM_VPU)
    hb = tm // HALO
    last_blk = t // HALO - 1

    def body(dc_ref, dn_ref, pc_ref, halo_ref, dw_ref, dpc_ref, ddw_ref, s, sd, acc):
        i = pl.program_id(0)
        _fill_glu(s.at[0], pc_ref, halo_ref, i == 0, tm)
        sd[0, pl.ds(0, tm), :] = dc_ref[...]
        sd[0, pl.ds(tm, HALO), :] = jnp.where(i == pl.num_programs(0) - 1, 0.0, dn_ref[...])
        _phase_copies(s)
        _phase_copies(sd)
        for r0 in range(0, tm, CONV_ROWS):
            rows = pl.ds(r0, CONV_ROWS)
            dcb = sd[0, rows, :]
            dglu = None
            for j in range(CW):
                term = _rows_at(sd, r0 + CW - 1 - j, CONV_ROWS) * dw_ref[pl.ds(j, 1), :]
                dglu = term if dglu is None else dglu + term
                part = dcb * _rows_at(s, r0 + HALO - CW + 1 + j, CONV_ROWS)
                part8 = part[0:8]
                for q in range(8, CONV_ROWS, 8):
                    part8 = part8 + part[q:q + 8]
                if r0 == 0:
                    acc[pl.ds(8 * j, 8), :] = part8
                else:
                    acc[pl.ds(8 * j, 8), :] += part8
            a = pc_ref[rows, :DC]
            sg = jax.nn.sigmoid(pc_ref[rows, DC:])
            dpc_ref[rows, :DC] = dglu * sg
            dpc_ref[rows, DC:] = dglu * a * sg * (1.0 - sg)
        rows = [jnp.sum(acc[pl.ds(8 * j, 8), :], axis=0, keepdims=True) for j in range(CW)]

        @pl.when(i == 0)
        def _():
            for j in range(CW):
                ddw_ref[pl.ds(j, 1), :] = rows[j]

        @pl.when(i != 0)
        def _():
            for j in range(CW):
                ddw_ref[pl.ds(j, 1), :] += rows[j]

    return pl.pallas_call(
        body, name=name, grid=(t // tm,),
        in_specs=[pl.BlockSpec((tm, DC), lambda i: (i, 0)),
                  pl.BlockSpec((HALO, DC), lambda i: (jnp.minimum((i + 1) * hb, last_blk), 0)),
                  pl.BlockSpec((tm, 2 * DC), lambda i: (i, 0)),
                  pl.BlockSpec((HALO, 2 * DC), lambda i: (jnp.maximum(i * hb - 1, 0), 0)),
                  _full((CW, DC))],
        out_specs=[pl.BlockSpec((tm, 2 * DC), lambda i: (i, 0)), _full((CW, DC))],
        out_shape=[_sds((t, 2 * DC)), _sds((CW, DC))],
        scratch_shapes=[pltpu.VMEM((8, HALO + tm, DC), f32), pltpu.VMEM((8, tm + HALO, DC), f32),
                        pltpu.VMEM((8 * CW, DC), f32)],
    )(dc, dc, pc, pc, dw)


def _adamw(w, g, m, v):
    m = ADAM_B1 * m + (1.0 - ADAM_B1) * g
    v = ADAM_B2 * v + (1.0 - ADAM_B2) * (g * g)
    m_hat = m / (1.0 - ADAM_B1 ** ADAM_STEP)
    v_hat = v / (1.0 - ADAM_B2 ** ADAM_STEP)
    delta = -ADAM_LR * (m_hat / (jnp.sqrt(v_hat) + ADAM_EPS) + ADAM_WD * w)
    return delta, m, v


def _adam_slots(w, slots, m, v, name):
    r, c = w.shape
    tr = next(cand for cand in (512, 352, 256, 128, r) if r % cand == 0)

    def body(w_ref, s_ref, m_ref, v_ref, g_ref, d_ref, nm_ref, nv_ref):
        g = s_ref[0].astype(f32)
        for k in range(1, NDEV):
            g = g + s_ref[k].astype(f32)
        delta, nm, nv = _adamw(w_ref[...], g, m_ref[...], v_ref[...])
        g_ref[...] = g
        d_ref[...] = delta
        nm_ref[...] = nm
        nv_ref[...] = nv

    blk = pl.BlockSpec((tr, c), lambda i: (i, 0))
    return pl.pallas_call(
        body, name=name, grid=(r // tr,),
        in_specs=[blk, pl.BlockSpec((NDEV, tr, c), lambda i: (0, i, 0)), blk, blk],
        out_specs=[blk] * 4, out_shape=[_sds((r, c))] * 4,
    )(w, slots, m, v)


def _sum_slots(slots, name):
    _, r, c = slots.shape

    def body(s_ref, o_ref):
        g = s_ref[0]
        for k in range(1, NDEV):
            g = g + s_ref[k]
        o_ref[...] = g

    return pl.pallas_call(body, name=name, in_specs=[_full((NDEV, r, c))], out_specs=_full((r, c)),
                          out_shape=_sds((r, c)), grid=(1,))(slots)


def _adam_many(ws, gs, ms, vs, name):
    n = len(ws)
    shapes = [w.shape for w in ws]

    def body(*refs):
        for i in range(n):
            w_ref, g_ref, m_ref, v_ref = refs[i], refs[n + i], refs[2 * n + i], refs[3 * n + i]
            delta, nm, nv = _adamw(w_ref[...], g_ref[...], m_ref[...], v_ref[...])
            refs[4 * n + i][...] = delta
            refs[5 * n + i][...] = nm
            refs[6 * n + i][...] = nv

    outs = pl.pallas_call(body, name=name, grid=(1,), in_specs=[_full(s) for s in shapes] * 4,
                          out_specs=[_full(s) for s in shapes] * 3, out_shape=[_sds(s) for s in shapes] * 3,
                          )(*ws, *gs, *ms, *vs)
    return outs[:n], outs[n:2 * n], outs[2 * n:]


def _pack(pieces, total):
    flat = []
    n = 0
    for p in pieces:
        p = p.reshape(-1)
        pad = (-p.shape[0]) % LANES
        if pad:
            p = jnp.pad(p, (0, pad))
        flat.append(p)
        n += p.shape[0]
    if total > n:
        flat.append(jnp.zeros((total - n,), f32))
    return jnp.concatenate(flat).reshape(total // LANES, LANES)


def _unpack(vec, shapes):
    flat = vec.reshape(-1)
    out, off = [], 0
    for s in shapes:
        n = math.prod(s)
        out.append(flat[off:off + n].reshape(s))
        off += n + (-n) % LANES
    return out


def _round_up(n, m):
    return (n + m - 1) // m * m


def kernel(x, ffn1_norm_pre, ffn1_norm_post, ffn1_w_gu, ffn1_w_down, mix_norm_pre, mix_norm_post, w_in, shift_mu, w_up, w0, a_up, a0, g_up, k_k, k_a, r_k, gn_w, gn_b, conv_dw, conv_b, conv_ln_w, conv_ln_b, w_out, ffn2_norm_pre, ffn2_norm_post, ffn2_w_gu, ffn2_w_down, loss_target, m_ffn1_norm_pre, m_ffn1_norm_post, m_ffn1_w_gu, m_ffn1_w_down, m_mix_norm_pre, m_mix_norm_post, m_w_in, m_shift_mu, m_w_up, m_w0, m_a_up, m_a0, m_g_up, m_k_k, m_k_a, m_r_k, m_gn_w, m_gn_b, m_conv_dw, m_conv_b, m_conv_ln_w, m_conv_ln_b, m_w_out, m_ffn2_norm_pre, m_ffn2_norm_post, m_ffn2_w_gu, m_ffn2_w_down, v_ffn1_norm_pre, v_ffn1_norm_post, v_ffn1_w_gu, v_ffn1_w_down, v_mix_norm_pre, v_mix_norm_post, v_w_in, v_shift_mu, v_w_up, v_w0, v_a_up, v_a0, v_g_up, v_k_k, v_k_a, v_r_k, v_gn_w, v_gn_b, v_conv_dw, v_conv_b, v_conv_ln_w, v_conv_ln_b, v_w_out, v_ffn2_norm_pre, v_ffn2_norm_post, v_ffn2_w_gu, v_ffn2_w_down):
    t = x.shape[1]
    x0 = x.reshape(t, D)
    tgt = loss_target.reshape(t, D)

    def shard(a):
        return a[0].astype(bf16)

    def shard_t(a):
        return jnp.swapaxes(a[0], 0, 1).astype(bf16)

    (wg1,) = _exchange([shard_t(ffn1_w_gu)], ["gather"], [SLOT42], "gather_wg1")

    def cols(a):
        return jnp.transpose(a, (1, 0, 2)).reshape(a.shape[1], NDEV * a.shape[2])

    lane = jnp.arange(DR, dtype=jnp.int32) // HN
    seg = (lane[:, None] == jnp.arange(LANES, dtype=jnp.int32)[None, :]).astype(f32)
    seg_t = seg.T
    rk_row = r_k.reshape(1, DR)
    post_params = (gn_w, gn_b, rk_row, seg, seg_t)

    (xn1, gu1, h1), (wd1, win_g) = _ffn_in_fwd(x0, ffn1_norm_pre, wg1, "ffn1_in_fwd", comm=(
        [shard(ffn1_w_down), shard_t(w_in)], ["gather"] * 2, [SLOT8] * 2))
    wd1 = wd1.reshape(4, FB, D)
    win_t = win_g.reshape(DIN, D)
    (f1, x1), (wout_g, wup_g, aup_g, gup_g, cdw_g) = _mm_norm_res_fwd(
        h1, wd1, x0, ffn1_norm_post, 0.5, "ffn1_out_fwd", comm=(
            [shard(w_out), w_up[0], a_up[0], g_up[0], conv_dw[0]], ["gather"] * 5, [SLOT8] * 5))
    wout_full = wout_g.reshape(1, D, D)
    wup_full, aup_full, gup_full, cdw_full = cols(wup_g), cols(aup_g), cols(gup_g), cols(cdw_g)
    zeros64 = jnp.zeros((HN, DR), f32)
    wup_pad = jnp.concatenate([wup_full, zeros64], axis=0)
    aup_pad = jnp.concatenate([zeros64, aup_full], axis=0)
    prep_params = (shift_mu, w0, a0, k_k, k_a, wup_pad, aup_pad, gup_full, seg, seg_t)
    hm, ps, pc = _norm_mm_fwd(x1, mix_norm_pre, win_t, "mix_in_fwd")
    rec_in, (wd2,) = _prep_fwd(ps, prep_params, "prep_fwd", comm=([shard(ffn2_w_down)], ["gather"], [SLOT8]))
    wd2 = wd2.reshape(4, FB, D)
    g_gate = rec_in[6]
    rec_in = rec_in[:6]
    (a_c, g_c, q_c, yl_c), (wg2,) = _r1_fwd(rec_in, "r1_fwd", comm=([shard_t(ffn2_w_gu)], ["gather"], [SLOT42]))
    y_rec, s_all = _r2_fwd(a_c, g_c, q_c, yl_c, "r2_fwd")
    c_conv, out_b = _conv_fwd(pc, cdw_full, conv_b, conv_ln_w, conv_ln_b, "conv_fwd")
    cat = _post_fwd(y_rec, rec_in[0], rec_in[2], rec_in[3], g_gate, out_b, post_params, "post_fwd")
    fm, x2 = _mm_norm_res_fwd(cat.reshape(1, t, D), wout_full, x1, mix_norm_post, 1.0, "mix_out_fwd")
    (xn2, gu2, h2), _ = _ffn_in_fwd(x2, ffn2_norm_pre, wg2, "ffn2_in_fwd")
    f2, dy, loss_part = _mm_norm_res_fwd(h2, wd2, x2, ffn2_norm_post, 0.5, "ffn2_out_fwd", tgt=tgt)

    (df2, dg_post2, dgu2), _ = _post_bwd_nt(dy, f2, ffn2_norm_post, wd2, 0.5, gu2, "ffn2_out_bwd")
    (dx2, dg_pre2), _ = _nt_pre_bwd(dgu2, wg2, x2, ffn2_norm_pre, dy, "ffn2_in_bwd")
    dwd2 = _wgrad(h2, df2.reshape(1, t, D), True, False, D, "ffn2_wd_grad")
    dwg2 = _wgrad(dgu2.reshape(8, t, FB), xn2.reshape(1, t, D), True, False, D, "ffn2_wg_grad")
    (dfm, dg_postm, dcat), _ = _post_bwd_nt(dx2, fm, mix_norm_post, wout_full, 1.0, None, "mix_out_bwd")
    dwout = _wgrad(cat.reshape(1, t, D), dfm.reshape(1, t, D), False, False, D, "wout_grad")
    dcat = dcat.reshape(t, D)
    (dy_rec, dr2, dkp2, dv2, dgate, dgn_w, dgn_b, drk), (s_wout,) = _post_bwd(
        dcat, y_rec, rec_in[0], rec_in[2], rec_in[3], g_gate, post_params, "post_bwd", comm=(
            [dwout.reshape(NDEV, D // NDEV, D)], ["scatter"], [SLOT8]))
    dq_c, da_c, dg_c = _r2_bwd(dy_rec, q_c, s_all, a_c, "r2_bwd")
    rec_grads, got = _r1_bwd(rec_in, da_c, dg_c, dq_c, dy_rec, "r1_bwd", comm=(
        [dwg2.reshape(4, 2, FB, D), dwd2.reshape(NDEV, F // NDEV, D)], ["scatter"] * 2, [SLOT42, SLOT8]))
    s_wg2, s_wd2 = got
    prep_out = _prep_bwd(ps, prep_params, rec_grads, (dr2, dkp2, dv2, dgate), "prep_bwd")
    dpsl, dw0, da0, dkk, dka, dwup_pad, daup_pad, dgup = prep_out
    dc_conv, dlnw, dlnb, dcb = _conv_bwd1(dcat, c_conv, conv_ln_w, conv_ln_b, "conv_bwd1")
    dpc, dcdw = _conv_bwd2(dc_conv, pc, cdw_full, "conv_bwd2")
    dp, dmu = _shift_bwd(dpsl, ps, shift_mu, dpc, "shift_bwd")
    (dx1, dg_prem), _ = _nt_pre_bwd(dp.reshape(1, 1, t, DIN), win_t.reshape(1, 1, DIN, D), x1, mix_norm_pre, dx2,
                                    "mix_in_bwd")
    dwin_s = _wgrad(dp.reshape(1, t, DIN), hm.reshape(1, t, D), False, False, D // 2, "win_grad").reshape(
        NDEV, DIN // NDEV, D)
    (df1, dg_post1, dgu1), (s_win,) = _post_bwd_nt(dx1, f1, ffn1_norm_post, wd1, 0.5, gu1, "ffn1_out_bwd", comm=(
        [dwin_s], ["scatter"], [SLOT8]))
    dwd1 = _wgrad(h1, df1.reshape(1, t, D), True, False, D, "ffn1_wd_grad")
    dwg1, (s_wd1,) = _wgrad(dgu1.reshape(8, t, FB), xn1.reshape(1, t, D), True, False, D, "ffn1_wg_grad", comm=(
        [dwd1.reshape(NDEV, F // NDEV, D)], ["scatter"], [SLOT8]))
    wg1_sends, wg1_recvs, dwg1_thru, wg1_land, started = _scatter_start(dwg1.reshape(4, 2, FB, D), SLOT42, "wg1_scatter_start")
    (dx0, dg_pre1), _ = _nt_pre_bwd(dgu1, wg1, x0, ffn1_norm_pre + started[:1, :1], dx1, "ffn1_in_bwd")

    rep_shapes = [(1, 1)] + [(1, D)] * 6 + [(1, DS)] + [(1, DR)] * 10
    rep_parts = [loss_part, dg_pre1, dg_post1, dg_prem, dg_postm, dg_pre2, dg_post2, dmu,
                 dw0, da0, dkk, dka, drk, dgn_w, dgn_b, dcb, dlnw, dlnb]
    n_rep = _round_up(sum(_round_up(math.prod(s), LANES) for s in rep_shapes), 8 * LANES)
    sh_shapes = [(HN, HN), (HN, HN), (LANES, HN), (CW, HN)]
    n_sh = _round_up(sum(_round_up(math.prod(s), LANES) for s in sh_shapes), 8 * LANES)
    sh_parts = []
    for a in (dwup_pad[:HN], daup_pad[HN:], dgup, dcdw):
        rows = jnp.transpose(a.reshape(a.shape[0], NDEV, HN), (1, 0, 2)).reshape(NDEV, a.shape[0] * HN)
        sh_parts.append(jnp.pad(rows, ((0, 0), (0, (-rows.shape[1]) % LANES))))
    sh_vec = jnp.concatenate(sh_parts, axis=1)
    sh_vec = jnp.pad(sh_vec, ((0, 0), (0, n_sh - sh_vec.shape[1]))).reshape(NDEV, n_sh // LANES, LANES)
    s_rep, s_sh = _exchange([_pack(rep_parts, n_rep), sh_vec], ["gather", "scatter"], [SLOT8, SLOT8], "exchange_small")

    res = {}

    def adam_big(nm, w, s, m, v, transposed):
        view = (lambda a: jnp.swapaxes(a[0], 0, 1)) if transposed else (lambda a: a[0])
        outs = _adam_slots(view(w), s, view(m), view(v), "adam_" + nm)
        res[nm] = [(jnp.swapaxes(o, 0, 1) if transposed else o)[None] for o in outs]

    adam_big("ffn2_w_gu", ffn2_w_gu, s_wg2.reshape(NDEV, FB, D), m_ffn2_w_gu, v_ffn2_w_gu, True)
    adam_big("ffn2_w_down", ffn2_w_down, s_wd2, m_ffn2_w_down, v_ffn2_w_down, False)
    adam_big("w_in", w_in, s_win, m_w_in, v_w_in, True)
    adam_big("w_out", w_out, s_wout, m_w_out, v_w_out, False)
    adam_big("ffn1_w_down", ffn1_w_down, s_wd1, m_ffn1_w_down, v_ffn1_w_down, False)
    _, s_wg1 = _scatter_wait(wg1_sends, wg1_recvs, dwg1_thru, wg1_land, res["ffn1_w_down"][1], SLOT42, "wg1_scatter_wait")
    my_slot = SLOT42[1](4 * lax.axis_index("x") + 2 * lax.axis_index("y") + lax.axis_index("c"))
    own = lax.dynamic_slice(dwg1.reshape(4, 2, FB, D), (*my_slot, 0, 0), (1, 1, FB, D))
    s_wg1 = lax.dynamic_update_slice(s_wg1, own, (*my_slot, 0, 0))
    adam_big("ffn1_w_gu", ffn1_w_gu, s_wg1.reshape(NDEV, FB, D), m_ffn1_w_gu, v_ffn1_w_gu, True)

    gsum = _unpack(_sum_slots(s_rep, "sum_rep"), rep_shapes)
    gsh = _unpack(_sum_slots(s_sh, "sum_sharded"), sh_shapes)
    loss = gsum[0].reshape(())
    rep_names = ["ffn1_norm_pre", "ffn1_norm_post", "mix_norm_pre", "mix_norm_post", "ffn2_norm_pre", "ffn2_norm_post",
                 "shift_mu", "w0", "a0", "k_k", "k_a", "r_k", "gn_w", "gn_b", "conv_b", "conv_ln_w", "conv_ln_b"]
    shard_names = ["w_up", "a_up", "g_up", "conv_dw"]
    env = dict(locals())
    small_g = {n: gsum[1 + i] for i, n in enumerate(rep_names)}
    small_g["r_k"] = small_g["r_k"].reshape(1, NH, HN)
    for i, n in enumerate(shard_names):
        small_g[n] = gsh[i][None]
    names = rep_names + shard_names
    shapes = [env[n].shape for n in names]

    def flat2(a):
        return a.reshape(1, DR) if a.shape == (1, NH, HN) else a.reshape(a.shape[-2:])

    deltas, new_ms, new_vs = _adam_many(*[[flat2(d[n]) for n in names] for d in (
        {n: env[n] for n in names}, small_g, {n: env["m_" + n] for n in names}, {n: env["v_" + n] for n in names})],
        "adam_small")
    for n, shp, dl, nm_, nv_ in zip(names, shapes, deltas, new_ms, new_vs):
        res[n] = [small_g[n], dl.reshape(shp), nm_.reshape(shp), nv_.reshape(shp)]

    order = ["ffn1_norm_pre", "ffn1_norm_post", "ffn1_w_gu", "ffn1_w_down", "mix_norm_pre", "mix_norm_post", "w_in",
             "shift_mu", "w_up", "w0", "a_up", "a0", "g_up", "k_k", "k_a", "r_k", "gn_w", "gn_b", "conv_dw", "conv_b",
             "conv_ln_w", "conv_ln_b", "w_out", "ffn2_norm_pre", "ffn2_norm_post", "ffn2_w_gu", "ffn2_w_down"]
    return (loss, dx0.reshape(1, t, D), *[res[n][0] for n in order], *[res[n][1] for n in order],
            *[res[n][2] for n in order], *[res[n][3] for n in order])
```

```python
import functools
import math

import jax
import jax.numpy as jnp
from jax import lax
from jax.experimental import pallas as pl
from jax.experimental.pallas import tpu as pltpu

f32 = jnp.float32
bf16 = jnp.bfloat16

D = 1024
F = 2816
FB = 704
DR = 512
DC = 512
NH = 8
HN = 64
DS = 1792
DIN = 2816
CW = 31
CHUNK = 64
R1_CHUNKS = 4
R2_CHUNKS = 8
NDEV = 8
RMS_EPS = 1e-6
GN_EPS = 64e-5
LN_EPS = 1e-5
DECAY_SCALE = math.exp(-0.5)
ADAM_LR, ADAM_B1, ADAM_B2, ADAM_EPS, ADAM_WD, ADAM_STEP = 0.001, 0.9, 0.999, 1e-08, 0.01, 10
LANES = 128
HALO = 32
CONV_ROWS = 32
TM_MXU = 1024
TM_SUB = 256
TM_VPU = 256

ANY = pl.BlockSpec(memory_space=pl.ANY)


def _full(shape):
    return pl.BlockSpec(shape, lambda *_: (0,) * len(shape))


def _sds(shape, dtype=f32):
    return jax.ShapeDtypeStruct(shape, dtype)


def _dot(a, b):
    return jnp.dot(a, b, preferred_element_type=f32)


def _dot_nt(a, b):
    return lax.dot_general(a, b, (((1,), (1,)), ((), ())), preferred_element_type=f32)


def _dot_tn(a, b):
    return lax.dot_general(a, b, (((0,), (0,)), ((), ())), preferred_element_type=f32)


def _terms(a, n):
    out, rem = [], a
    for i in range(n):
        t = rem.astype(bf16)
        out.append(t)
        if i + 1 < n:
            rem = rem - t.astype(f32)
    return out


def _mm_raw(a, b, kind, mode):
    nb = a.ndim - 2
    bd = tuple(range(nb))
    ca = nb if kind == "tn" else nb + 1
    cb = nb + 1 if kind == "nt" else nb
    dn = (((ca,), (cb,)), (bd, bd))
    pa, pb = mode[:2]
    ta, tb = _terms(a, pa), _terms(b, pb)
    acc = None
    for i in range(pa):
        for j in range(pb):
            if i + j < max(pa, pb):
                p = lax.dot_general(ta[i], tb[j], dn, preferred_element_type=f32)
                acc = p if acc is None else acc + p
    return acc


@functools.partial(jax.custom_vjp, nondiff_argnums=(2, 3))
def mm(a, b, kind, mode):
    return _mm_raw(a, b, kind, mode)


def _mm_fwd(a, b, kind, mode):
    return _mm_raw(a, b, kind, mode), (a, b)


def _ct_terms(keep):
    return EXACT_TERMS if keep == 1 else keep


def _mm_bwd(kind, mode, res, ct):
    a, b = res
    pa, pb = mode[:2]
    if len(mode) == 4:
        ca = cb = mode[2]
        pa = pb = mode[3]
    elif mode == (1, 1):
        ca = cb = 1
    else:
        ca, cb = _ct_terms(pb), _ct_terms(pa)
    if kind == "nn":
        da = mm(ct, b, "nt", (ca, pb))
        db = mm(a, ct, "tn", (pa, cb))
    elif kind == "nt":
        da = mm(ct, b, "nn", (ca, pb))
        db = mm(ct, a, "tn", (cb, pa))
    else:
        da = mm(b, ct, "nt", (pb, ca))
        db = mm(a, ct, "nn", (pa, cb))
    return da, db


mm.defvjp(_mm_fwd, _mm_bwd)

X3 = (2, 2)
EXACT_TERMS = 2
EA = (1, EXACT_TERMS)
EB = (EXACT_TERMS, 1)
X1 = (1, 1)
RGM = (1, 1, 1, 2)
RIM = X1
RWM = X1
RAM = X1
R2M = X1
LORA = X1


def _rms_fwd(x, g):
    inv = lax.rsqrt(jnp.mean(x * x, axis=-1, keepdims=True) + RMS_EPS)
    return x * inv * g


def _rms_bwd(x, g, dy):
    inv = lax.rsqrt(jnp.mean(x * x, axis=-1, keepdims=True) + RMS_EPS)
    xh = x * inv
    dxh = dy * g
    dg = jnp.sum(dy * xh, axis=0, keepdims=True)
    dx = inv * (dxh - xh * jnp.mean(dxh * xh, axis=-1, keepdims=True))
    return dx, dg


def _tile(t, want):
    return min(t, want)


NPEER = NDEV - 1


def _comm_out_shape(arrays, modes, slots):
    out = []
    for a, mode, (lead, _) in zip(arrays, modes, slots):
        shp = tuple(lead) + tuple(a.shape) if mode == "gather" else tuple(a.shape)
        out.append(pltpu.HBM(shp, a.dtype))
    return out


def _comm_sems(n):
    return [pltpu.SemaphoreType.DMA((n * NPEER,)), pltpu.SemaphoreType.DMA((n * NPEER,)), pltpu.SemaphoreType.DMA((n,))]


RELAYED = (3, 5, 7)


def _comm_copies(ins, outs, sems, modes, slots, want):
    n = len(ins)
    send_sems, recv_sems, loc_sems = sems
    x, y, c = lax.axis_index("x"), lax.axis_index("y"), lax.axis_index("c")
    me = 4 * x + 2 * y + c

    def peer_of(k):
        px = 1 - x if k & 4 else x
        py = 1 - y if k & 2 else y
        pc = 1 - c if k & 1 else c
        return (px, py, pc), 4 * px + 2 * py + pc

    def slot(i, dev):
        return outs[i].at[slots[i][1](dev)]

    if want == "local":
        return [pltpu.make_async_copy(ins[i] if modes[i] == "gather" else ins[i].at[slots[i][1](me)], slot(i, me),
                                      loc_sems.at[i]) for i in range(n)]
    sibling = peer_of(1)[0]
    out = []
    for k in range(1, NDEV):
        peer, pid = peer_of(k)
        for i in range(n):
            sem = dict(send_sem=send_sems.at[i * NPEER + k - 1], recv_sem=recv_sems.at[i * NPEER + k - 1],
                       device_id_type=pl.DeviceIdType.MESH)
            relayed = modes[i] == "gather" and k in RELAYED
            if want == "recv":
                src = ins[i] if modes[i] == "gather" else ins[i].at[slots[i][1](pid)]
                out.append(pltpu.make_async_remote_copy(src_ref=src, dst_ref=slot(i, pid), device_id=peer, **sem))
            elif want == "first" and not relayed:
                src = ins[i] if modes[i] == "gather" else ins[i].at[slots[i][1](pid)]
                out.append(pltpu.make_async_remote_copy(src_ref=src, dst_ref=slot(i, me), device_id=peer, **sem))
            elif want == "relay" and relayed:
                origin = peer_of(k - 1)[1]
                out.append(((k - 2) * n + i, pltpu.make_async_remote_copy(
                    src_ref=slot(i, origin), dst_ref=slot(i, origin), device_id=sibling, **sem)))
    return out


def _comm_start(ins, outs, sems, modes, slots):
    for cp in _comm_copies(ins, outs, sems, modes, slots, "local") + _comm_copies(ins, outs, sems, modes, slots, "first"):
        cp.start()


def _comm_wait(ins, outs, sems, modes, slots):
    recvs = _comm_copies(ins, outs, sems, modes, slots, "recv")
    relays = _comm_copies(ins, outs, sems, modes, slots, "relay")
    for idx, cp in relays:
        recvs[idx].wait_recv()
        cp.start()
    passed_on = {idx for idx, _ in relays}
    for idx, cp in enumerate(recvs):
        if idx not in passed_on:
            cp.wait_recv()
    for cp in _comm_copies(ins, outs, sems, modes, slots, "first") + [cp for _, cp in relays]:
        cp.wait_send()
    for cp in _comm_copies(ins, outs, sems, modes, slots, "local"):
        cp.wait()


def _exchange(arrays, modes, slots, name):
    n = len(arrays)

    def body(*refs):
        ins, outs, sems = refs[:n], refs[n:2 * n], refs[2 * n:]
        _comm_start(ins, outs, sems, modes, slots)
        _comm_wait(ins, outs, sems, modes, slots)

    return pl.pallas_call(
        body, name=name, out_shape=_comm_out_shape(arrays, modes, slots),
        in_specs=[ANY] * n, out_specs=[ANY] * n, scratch_shapes=_comm_sems(n),
    )(*arrays)


HBM_SPEC = pl.BlockSpec(memory_space=pltpu.HBM)
SEM_SPEC = pl.BlockSpec(memory_space=pltpu.SEMAPHORE)


def _scatter_start(arr, slot, name):
    def body(v_ref, land_ref, send_sems, recv_sems, v_thru, land_thru, token):
        for cp in _comm_copies([v_ref], [land_ref], (send_sems, recv_sems, None), ["scatter"], [slot], "first"):
            cp.start()
        token[...] = jnp.zeros_like(token)

    return pl.pallas_call(
        body, name=name,
        out_shape=(pltpu.SemaphoreType.DMA((NPEER,)), pltpu.SemaphoreType.DMA((NPEER,)),
                   pltpu.HBM(arr.shape, arr.dtype), pltpu.HBM(arr.shape, arr.dtype), _sds((8, LANES))),
        in_specs=(HBM_SPEC, HBM_SPEC),
        out_specs=(SEM_SPEC, SEM_SPEC, HBM_SPEC, HBM_SPEC, pl.BlockSpec(memory_space=pltpu.VMEM)),
        input_output_aliases={0: 2, 1: 3},
        compiler_params=pltpu.CompilerParams(has_side_effects=pltpu.SideEffectType.DATAFLOW_SIDE_EFFECTING),
    )(pltpu.with_memory_space_constraint(arr, pltpu.HBM),
      pltpu.with_memory_space_constraint(lax.empty(arr.shape, arr.dtype), pltpu.HBM))


def _scatter_wait(send_sems, recv_sems, v_thru, land_thru, after, slot, name):
    def body(v_ref, land_ref, send_sems, recv_sems, after_ref, v_dead, got_ref):
        sems = (send_sems, recv_sems, None)
        for cp in _comm_copies([v_ref], [land_ref], sems, ["scatter"], [slot], "first"):
            cp.wait_send()
        for cp in _comm_copies([v_ref], [land_ref], sems, ["scatter"], [slot], "recv"):
            cp.wait_recv()

    return pl.pallas_call(
        body, name=name, out_shape=(pltpu.HBM(v_thru.shape, v_thru.dtype), pltpu.HBM(v_thru.shape, v_thru.dtype)),
        in_specs=(HBM_SPEC, HBM_SPEC, SEM_SPEC, SEM_SPEC, ANY), out_specs=(HBM_SPEC, HBM_SPEC),
        input_output_aliases={0: 0, 1: 1},
        compiler_params=pltpu.CompilerParams(has_side_effects=pltpu.SideEffectType.DATAFLOW_SIDE_EFFECTING),
    )(v_thru, land_thru, send_sems, recv_sems, after)


def _pcall(body, *, name, grid, in_specs, out_specs, out_shape, args, scratch_shapes=(), comm=None):
    if comm is None:
        return pl.pallas_call(body, name=name, grid=grid, in_specs=in_specs, out_specs=out_specs, out_shape=out_shape,
                              scratch_shapes=list(scratch_shapes))(*args), None
    arrays, modes, slots = comm
    n_in, n_out, n_scr, nc = len(args), len(out_shape), len(scratch_shapes), len(arrays)

    def hosted(*refs):
        a_in, c_in = refs[:n_in], refs[n_in:n_in + nc]
        o = n_in + nc
        a_out, c_out = refs[o:o + n_out], refs[o + n_out:o + n_out + nc]
        o += n_out + nc
        a_scr, sems = refs[o:o + n_scr], refs[o + n_scr:]
        first = pl.program_id(0) == 0
        last = pl.program_id(0) == grid[0] - 1
        for ax in range(1, len(grid)):
            first = jnp.logical_and(first, pl.program_id(ax) == 0)
            last = jnp.logical_and(last, pl.program_id(ax) == grid[ax] - 1)

        @pl.when(first)
        def _():
            _comm_start(c_in, c_out, sems, modes, slots)

        body(*a_in, *a_out, *a_scr)

        @pl.when(last)
        def _():
            _comm_wait(c_in, c_out, sems, modes, slots)

    res = pl.pallas_call(
        hosted, name=name, grid=grid, in_specs=list(in_specs) + [ANY] * nc, out_specs=list(out_specs) + [ANY] * nc,
        out_shape=list(out_shape) + _comm_out_shape(arrays, modes, slots),
        scratch_shapes=list(scratch_shapes) + _comm_sems(nc),
    )(*args, *arrays)
    return res[:n_out], res[n_out:]


SLOT8 = ((NDEV,), lambda d: (d,))
SLOT42 = ((4, 2), lambda d: (lax.rem(d, 4), lax.div(d, 4)))


def _ffn_in_fwd(x, g, wg4, name, comm=None):
    t = x.shape[0]
    tm = _tile(t, TM_MXU)
    sub = _tile(tm, TM_SUB)

    def body(x_ref, g_ref, w_ref, xn_ref, gu_ref, h_ref):
        for s in range(tm // sub):
            rows = pl.ds(s * sub, sub)
            xn = _rms_fwd(x_ref[rows, :], g_ref[...]).astype(bf16)
            xn_ref[rows, :] = xn
            gate = _dot_nt(xn, w_ref[0, 0])
            up = _dot_nt(xn, w_ref[0, 1])
            gu_ref[0, 0, rows, :] = gate.astype(bf16)
            gu_ref[0, 1, rows, :] = up.astype(bf16)
            h_ref[0, rows, :] = (gate * jax.nn.sigmoid(gate) * up).astype(bf16)

    return _pcall(
        body, name=name, grid=(t // tm, 4),
        in_specs=[pl.BlockSpec((tm, D), lambda i, e: (i, 0)), _full((1, D)),
                  pl.BlockSpec((1, 2, FB, D), lambda i, e: (e, 0, 0, 0))],
        out_specs=[pl.BlockSpec((tm, D), lambda i, e: (i, 0)),
                   pl.BlockSpec((1, 2, tm, FB), lambda i, e: (e, 0, i, 0)),
                   pl.BlockSpec((1, tm, FB), lambda i, e: (e, i, 0))],
        out_shape=[_sds((t, D), bf16), _sds((4, 2, t, FB), bf16), _sds((4, t, FB), bf16)],
        args=(x, g, wg4), comm=comm)


def _mm_norm_res_fwd(hb, wb, xres, g, scale, name, comm=None, tgt=None):
    e_n, t, k = hb.shape
    tm = _tile(t, TM_MXU)

    def body(h_ref, w_ref, x_ref, g_ref, *rest):
        f = _dot(h_ref[0], w_ref[0])
        for e in range(1, e_n):
            f = f + _dot(h_ref[e], w_ref[e])
        xnew = x_ref[...] + scale * _rms_fwd(f, g_ref[...])
        if tgt is None:
            f_ref, o_ref = rest
            o_ref[...] = xnew
        else:
            t_ref, f_ref, o_ref, l_ref = rest
            err = xnew - t_ref[...]
            o_ref[...] = err * (1.0 / D)
            part = 0.5 * jnp.sum(jnp.mean(err * err, axis=-1, keepdims=True), axis=0, keepdims=True)

            @pl.when(pl.program_id(0) == 0)
            def _():
                l_ref[...] = part

            @pl.when(pl.program_id(0) != 0)
            def _():
                l_ref[...] += part
        f_ref[...] = f

    tile = pl.BlockSpec((tm, D), lambda i: (i, 0))
    outs, got = _pcall(
        body, name=name, grid=(t // tm,),
        in_specs=[pl.BlockSpec((e_n, tm, k), lambda i: (0, i, 0)), _full((e_n, k, D)), tile, _full((1, D))]
        + ([] if tgt is None else [tile]),
        out_specs=[tile, tile] + ([] if tgt is None else [_full((1, 1))]),
        out_shape=[_sds((t, D)), _sds((t, D))] + ([] if tgt is None else [_sds((1, 1))]),
        args=(hb, wb, xres, g) + (() if tgt is None else (tgt,)), comm=comm)
    return outs if comm is None else (outs, got)


def _post_bwd_nt(dxn, f, g, wb, scale, gu, name, comm=None):
    e_n, k, _ = wb.shape
    t = f.shape[0]
    tm = _tile(t, TM_MXU)
    sub = _tile(tm, TM_SUB)
    swiglu = gu is not None

    def body(*refs):
        if swiglu:
            dx_ref, f_ref, g_ref, w_ref, gu_ref, df_ref, dg_ref, dh_ref, df_s = refs
        else:
            dx_ref, f_ref, g_ref, w_ref, df_ref, dg_ref, dh_ref, df_s = refs
        i, e = pl.program_id(0), pl.program_id(1)

        @pl.when(e == 0)
        def _():
            df, dg = _rms_bwd(f_ref[...], g_ref[...], scale * dx_ref[...])
            df_s[...] = df.astype(bf16)
            df_ref[...] = df_s[...]

            @pl.when(i == 0)
            def _():
                dg_ref[...] = dg

            @pl.when(i != 0)
            def _():
                dg_ref[...] += dg

        for s in range(tm // sub):
            rows = pl.ds(s * sub, sub)
            dh = _dot_nt(df_s[rows, :], w_ref[0])
            if swiglu:
                gate = gu_ref[0, 0, rows, :].astype(f32)
                up = gu_ref[0, 1, rows, :].astype(f32)
                sg = jax.nn.sigmoid(gate)
                dh_ref[0, 0, rows, :] = (dh * up * (sg * (1.0 + gate * (1.0 - sg)))).astype(bf16)
                dh_ref[0, 1, rows, :] = (dh * gate * sg).astype(bf16)
            else:
                dh_ref[0, rows, :] = dh

    in_specs = [pl.BlockSpec((tm, D), lambda i, e: (i, 0)), pl.BlockSpec((tm, D), lambda i, e: (i, 0)), _full((1, D)),
                pl.BlockSpec((1, k, D), lambda i, e: (e, 0, 0))]
    args = [dxn, f, g, wb]
    if swiglu:
        in_specs.append(pl.BlockSpec((1, 2, tm, k), lambda i, e: (e, 0, i, 0)))
        args.append(gu)
        dh_spec = pl.BlockSpec((1, 2, tm, k), lambda i, e: (e, 0, i, 0))
        dh_shape = _sds((e_n, 2, t, k), bf16)
    else:
        dh_spec = pl.BlockSpec((1, tm, k), lambda i, e: (e, i, 0))
        dh_shape = _sds((e_n, t, k), f32)
    return _pcall(
        body, name=name, grid=(t // tm, e_n), in_specs=in_specs,
        out_specs=[pl.BlockSpec((tm, D), lambda i, e: (i, 0)), _full((1, D)), dh_spec],
        out_shape=[_sds((t, D), bf16), _sds((1, D)), dh_shape],
        scratch_shapes=[pltpu.VMEM((tm, D), bf16)], args=args, comm=comm)


def _nt_pre_bwd(dy, wb, x, g, dres, name, comm=None):
    e_n, q_n, t, k = dy.shape
    tm = _tile(t, TM_MXU)

    def body(dy_ref, w_ref, x_ref, g_ref, r_ref, dx_ref, dg_ref, acc):
        i, e = pl.program_id(0), pl.program_id(1)
        p = _dot(dy_ref[0, 0], w_ref[0, 0])
        for q in range(1, q_n):
            p = p + _dot(dy_ref[0, q], w_ref[0, q])

        @pl.when(e == 0)
        def _():
            acc[...] = p

        @pl.when(e != 0)
        def _():
            acc[...] += p

        @pl.when(e == e_n - 1)
        def _():
            dx, dg = _rms_bwd(x_ref[...], g_ref[...], acc[...])
            dx_ref[...] = r_ref[...] + dx

            @pl.when(i == 0)
            def _():
                dg_ref[...] = dg

            @pl.when(i != 0)
            def _():
                dg_ref[...] += dg

    return _pcall(
        body, name=name, grid=(t // tm, e_n),
        in_specs=[pl.BlockSpec((1, q_n, tm, k), lambda i, e: (e, 0, i, 0)),
                  pl.BlockSpec((1, q_n, k, D), lambda i, e: (e, 0, 0, 0)),
                  pl.BlockSpec((tm, D), lambda i, e: (i, 0)), _full((1, D)),
                  pl.BlockSpec((tm, D), lambda i, e: (i, 0))],
        out_specs=[pl.BlockSpec((tm, D), lambda i, e: (i, 0)), _full((1, D))],
        out_shape=[_sds((t, D)), _sds((1, D))],
        scratch_shapes=[pltpu.VMEM((tm, D), f32)], args=(dy, wb, x, g, dres), comm=comm)


def _wgrad(a, b, a_batched, b_batched, tn, name, comm=None):
    t, k = a.shape[1], a.shape[2]
    nn = b.shape[2]
    nb = max(a.shape[0], b.shape[0])
    tt = _tile(t, TM_MXU)
    nt = t // tt

    def body(a_ref, b_ref, o_ref, acc):
        @pl.when(pl.program_id(2) == 0)
        def _():
            acc[...] = _dot_tn(a_ref[0], b_ref[0])

        @pl.when(pl.program_id(2) != 0)
        def _():
            acc[...] += _dot_tn(a_ref[0], b_ref[0])

        @pl.when(pl.program_id(2) == nt - 1)
        def _():
            o_ref[0] = acc[...].astype(bf16)

    (out,), got = _pcall(
        body, name=name, grid=(nb, nn // tn, nt),
        in_specs=[pl.BlockSpec((1, tt, k), (lambda n, j, s: (n, s, 0)) if a_batched else (lambda n, j, s: (0, s, 0))),
                  pl.BlockSpec((1, tt, tn), (lambda n, j, s: (n, s, j)) if b_batched else (lambda n, j, s: (0, s, j)))],
        out_specs=[pl.BlockSpec((1, k, tn), lambda n, j, s: (n, 0, j))],
        out_shape=[_sds((nb, k, nn), bf16)],
        scratch_shapes=[pltpu.VMEM((k, tn), f32)], args=(a, b), comm=comm)
    return out if comm is None else (out, got)


def _norm_mm_fwd(x, g, w, name):
    t = x.shape[0]
    tm = _tile(t, TM_VPU)

    def body(x_ref, g_ref, w_ref, xn_ref, ps_ref, pc_ref):
        xn = _rms_fwd(x_ref[...], g_ref[...]).astype(bf16)
        xn_ref[...] = xn
        ps_ref[...] = _dot_nt(xn, w_ref[:DS, :])
        pc_ref[...] = _dot_nt(xn, w_ref[DS:, :])

    return pl.pallas_call(
        body, name=name, grid=(t // tm,),
        in_specs=[pl.BlockSpec((tm, D), lambda i: (i, 0)), _full((1, D)), _full((DIN, D))],
        out_specs=[pl.BlockSpec((tm, D), lambda i: (i, 0)), pl.BlockSpec((tm, DS), lambda i: (i, 0)),
                   pl.BlockSpec((tm, 2 * DC), lambda i: (i, 0))],
        out_shape=[_sds((t, D), bf16), _sds((t, DS)), _sds((t, 2 * DC))],
    )(x, g, w)


def _segsum_b(x, seg, seg_t):
    return mm(mm(x, seg, "nn", EB), seg_t, "nn", EB)


def _prep_fn(psl, w0, a0, k_k, k_a, wup, aup, gup, seg, seg_t):
    r, k, v = psl[:, :DR], psl[:, DR:2 * DR], psl[:, 2 * DR:3 * DR]
    xwa, xg = psl[:, 3 * DR:3 * DR + LANES], psl[:, 3 * DR + LANES:]
    d = w0 + mm(jnp.tanh(xwa), wup, "nn", LORA)
    lw = -DECAY_SCALE * jax.nn.sigmoid(d)
    a = jax.nn.sigmoid(a0 + mm(xwa, aup, "nn", LORA))
    g = mm(jax.nn.sigmoid(xg), gup, "nn", LORA)
    kkr = k * k_k
    kk = kkr * lax.rsqrt(jnp.maximum(_segsum_b(kkr * kkr, seg, seg_t), 1e-12))
    kp = k * (1.0 + (a - 1.0) * k_a)
    return r, lw, kp, v, -kk, kk * a, g


def _shifted(ps, halo_row, first):
    prev = jnp.where(first, 0.0, halo_row)
    sh = pltpu.roll(ps, 1, 0)
    row = lax.broadcasted_iota(jnp.int32, ps.shape, 0)
    return jnp.where(row == 0, prev, sh)


def _heads_of(ref, rows):
    return jnp.stack([ref[rows, h * HN:(h + 1) * HN] for h in range(NH)], axis=0)


def _heads_to(ref, rows, val):
    ref[rows, :] = jnp.concatenate([val[h] for h in range(NH)], axis=-1)


_PREP_PARAM_SHAPES = [(1, DS), (1, DR), (1, DR), (1, DR), (1, DR), (LANES, DR), (LANES, DR), (LANES, DR),
                      (DR, LANES), (LANES, DR)]


def _prep_fwd(ps, params, name, comm=None):
    t = ps.shape[0]
    tm = _tile(t, TM_VPU)
    hb = tm // 8

    def body(ps_ref, halo_ref, mu_ref, *rest):
        prm = [r[...] for r in rest[:9]]
        outs = rest[9:]
        x = ps_ref[...]
        sh = _shifted(x, halo_ref[pl.ds(7, 1), :], pl.program_id(0) == 0)
        psl = x + (sh - x) * mu_ref[...]
        vals = _prep_fn(psl, *prm)
        for ref, val in zip(outs, vals):
            ref[...] = val

    outs, got = _pcall(
        body, name=name, grid=(t // tm,),
        in_specs=[pl.BlockSpec((tm, DS), lambda i: (i, 0)),
                  pl.BlockSpec((8, DS), lambda i: (jnp.maximum(i * hb - 1, 0), 0))]
        + [_full(s) for s in _PREP_PARAM_SHAPES],
        out_specs=[pl.BlockSpec((tm, DR), lambda i: (i, 0))] * 7,
        out_shape=[_sds((t, DR))] * 7, args=(ps, ps, *params), comm=comm)
    return outs if comm is None else (outs, got)


def _prep_bwd(ps, params, cts_hm, cts_tm, name):
    t = ps.shape[0]
    tm = _tile(t, TM_VPU)
    hb = tm // 8

    def body(ps_ref, halo_ref, mu_ref, *rest):
        prm = [r[...] for r in rest[:9]]
        chm = rest[9:15]
        ctm = rest[15:19]
        dpsl_ref = rest[19]
        gouts = rest[20:27]
        x = ps_ref[...]
        sh = _shifted(x, halo_ref[pl.ds(7, 1), :], pl.program_id(0) == 0)
        psl = x + (sh - x) * mu_ref[...]
        seg, seg_t = prm[7], prm[8]
        _, vjp = jax.vjp(lambda p, *w: _prep_fn(p, *w, seg, seg_t), psl, *prm[:7])
        c = [r[...] for r in chm]
        cts = (c[0] + ctm[0][...], c[1], c[2] + ctm[1][...], c[3] + ctm[2][...], c[4], c[5], ctm[3][...])
        grads = vjp(cts)
        dpsl_ref[...] = grads[0]
        for ref, gval in zip(gouts, grads[1:]):
            @pl.when(pl.program_id(0) == 0)
            def _(ref=ref, gval=gval):
                ref[...] = gval

            @pl.when(pl.program_id(0) != 0)
            def _(ref=ref, gval=gval):
                ref[...] += gval

    tk = pl.BlockSpec((tm, DR), lambda i: (i, 0))
    gshapes = _PREP_PARAM_SHAPES[1:8]
    return pl.pallas_call(
        body, name=name, grid=(t // tm,),
        in_specs=[pl.BlockSpec((tm, DS), lambda i: (i, 0)),
                  pl.BlockSpec((8, DS), lambda i: (jnp.maximum(i * hb - 1, 0), 0))]
        + [_full(s) for s in _PREP_PARAM_SHAPES] + [tk] * 10,
        out_specs=[pl.BlockSpec((tm, DS), lambda i: (i, 0))] + [_full(s) for s in gshapes],
        out_shape=[_sds((t, DS))] + [_sds(s) for s in gshapes],
    )(ps, ps, *params, *cts_hm, *cts_tm)


def _shift_bwd(dpsl, ps, mu, dpc, name):
    t = ps.shape[0]
    tm = _tile(t, TM_VPU)
    hb = tm // 8
    last_blk = t // 8 - 1

    def body(d_ref, dn_ref, ps_ref, halo_ref, mu_ref, dpc_ref, dp_ref, dmu_ref):
        i = pl.program_id(0)
        mu_v = mu_ref[...]
        d = d_ref[...]
        nxt = jnp.where(i == pl.num_programs(0) - 1, 0.0, dn_ref[pl.ds(0, 1), :])
        up = pltpu.roll(d, tm - 1, 0)
        row = lax.broadcasted_iota(jnp.int32, d.shape, 0)
        up = jnp.where(row == tm - 1, nxt, up)
        dp_ref[:, :DS] = (d * (1.0 - mu_v) + up * mu_v).astype(bf16)
        dp_ref[:, DS:] = dpc_ref[...].astype(bf16)
        x = ps_ref[...]
        sh = _shifted(x, halo_ref[pl.ds(7, 1), :], i == 0)
        dmu = jnp.sum(d * (sh - x), axis=0, keepdims=True)

        @pl.when(i == 0)
        def _():
            dmu_ref[...] = dmu

        @pl.when(i != 0)
        def _():
            dmu_ref[...] += dmu

    return pl.pallas_call(
        body, name=name, grid=(t // tm,),
        in_specs=[pl.BlockSpec((tm, DS), lambda i: (i, 0)),
                  pl.BlockSpec((8, DS), lambda i: (jnp.minimum((i + 1) * hb, last_blk), 0)),
                  pl.BlockSpec((tm, DS), lambda i: (i, 0)),
                  pl.BlockSpec((8, DS), lambda i: (jnp.maximum(i * hb - 1, 0), 0)),
                  _full((1, DS)), pl.BlockSpec((tm, 2 * DC), lambda i: (i, 0))],
        out_specs=[pl.BlockSpec((tm, DIN), lambda i: (i, 0)), _full((1, DS))],
        out_shape=[_sds((t, DIN), bf16), _sds((1, DS))],
    )(dpsl, dpsl, ps, ps, mu, dpc)


def _post_fn(y, r, kp, v, g, gn_w, gn_b, r_k, seg, seg_t):
    mu = _segsum_b(y, seg, seg_t) * (1.0 / HN)
    yc = y - mu
    var = _segsum_b(yc * yc, seg, seg_t) * (1.0 / HN)
    yo = yc * lax.rsqrt(var + GN_EPS) * gn_w + gn_b
    bonus = _segsum_b(r * kp * r_k, seg, seg_t) * v
    return (yo + bonus) * g


_POST_PARAM_SHAPES = [(1, DR), (1, DR), (1, DR), (DR, LANES), (LANES, DR)]


def _post_fwd(y, r, kp, v, g, out_b, params, name):
    t = g.shape[0]
    tm = _tile(t, TM_VPU)

    def body(y_ref, r_ref, k_ref, v_ref, g_ref, ob_ref, *rest):
        prm = [p[...] for p in rest[:5]]
        cat_ref = rest[5]
        oa = _post_fn(y_ref[...], r_ref[...], k_ref[...], v_ref[...], g_ref[...], *prm)
        cat_ref[:, :DR] = oa.astype(bf16)
        cat_ref[:, DR:] = ob_ref[...]

    tk = pl.BlockSpec((tm, DR), lambda i: (i, 0))
    return pl.pallas_call(
        body, name=name, grid=(t // tm,),
        in_specs=[tk] * 6 + [_full(s) for s in _POST_PARAM_SHAPES],
        out_specs=pl.BlockSpec((tm, 2 * DR), lambda i: (i, 0)),
        out_shape=_sds((t, 2 * DR), bf16),
    )(y, r, kp, v, g, out_b, *params)


def _post_bwd(dcat, y, r, kp, v, g, params, name, comm=None):
    t = g.shape[0]
    tm = _tile(t, TM_VPU)

    def body(dc_ref, y_ref, r_ref, k_ref, v_ref, g_ref, *rest):
        prm = [p[...] for p in rest[:5]]
        dy_ref, dr_ref, dk_ref, dv_ref, dg_ref = rest[5:10]
        gouts = rest[10:13]
        seg, seg_t = prm[3], prm[4]
        _, vjp = jax.vjp(lambda *a: _post_fn(*a, seg, seg_t),
                         y_ref[...], r_ref[...], k_ref[...], v_ref[...], g_ref[...], *prm[:3])
        grads = vjp(dc_ref[...])
        dy_ref[...] = grads[0]
        dr_ref[...] = grads[1]
        dk_ref[...] = grads[2]
        dv_ref[...] = grads[3]
        dg_ref[...] = grads[4]
        for ref, gval in zip(gouts, grads[5:]):
            @pl.when(pl.program_id(0) == 0)
            def _(ref=ref, gval=gval):
                ref[...] = gval

            @pl.when(pl.program_id(0) != 0)
            def _(ref=ref, gval=gval):
                ref[...] += gval

    tk = pl.BlockSpec((tm, DR), lambda i: (i, 0))
    return _pcall(
        body, name=name, grid=(t // tm,),
        in_specs=[tk] * 6 + [_full(s) for s in _POST_PARAM_SHAPES],
        out_specs=[tk] * 5 + [_full((1, DR))] * 3,
        out_shape=[_sds((t, DR))] * 5 + [_sds((1, DR))] * 3,
        args=(dcat, y, r, kp, v, g, *params), comm=comm)


@jax.custom_vjp
def _inv_unit_lower(lb):
    c = lb.shape[-1]
    row = lax.broadcasted_iota(jnp.int32, (c, c), 0)
    col = lax.broadcasted_iota(jnp.int32, (c, c), 1)
    p = (row == col).astype(f32)[None] + lb
    m = lb
    for _ in range(c.bit_length() - 2):
        m = mm(m, m, "nn", RIM)
        p = p + mm(m, p, "nn", RIM)
    return p


def _inv_unit_lower_fwd(lb):
    p = _inv_unit_lower(lb)
    return p, p


def _inv_unit_lower_bwd(p, ct):
    return (mm(mm(p, ct, "tn", RIM), p, "nt", RIM),)


_inv_unit_lower.defvjp(_inv_unit_lower_fwd, _inv_unit_lower_bwd)


def _r1_fn(r, lw, k, v, z, b):
    c = r.shape[1]
    row = lax.broadcasted_iota(jnp.int32, (c, c), 0)
    col = lax.broadcasted_iota(jnp.int32, (c, c), 1)
    incl = (row >= col)[None]
    strict = (row > col)[None]
    eye = (row == col).astype(f32)[None]
    tri = jnp.broadcast_to((row >= col).astype(f32)[None], (r.shape[0], c, c))
    cum = mm(tri, lw, "nn", EA)
    tot = jnp.sum(lw, axis=1, keepdims=True)
    zt = z * jnp.exp(cum - lw)
    rt = r * jnp.exp(cum)
    g_inv = jnp.exp(-cum)
    g_rem = jnp.exp(tot - cum)
    bt, kt = b * g_inv, k * g_inv
    bh, kh = b * g_rem, k * g_rem
    lb = jnp.where(strict, mm(zt, bt, "nt", RGM), 0.0)
    lk = jnp.where(strict, mm(zt, kt, "nt", RGM), 0.0)
    mb = jnp.where(incl, mm(rt, bt, "nt", RGM), 0.0)
    mk = jnp.where(incl, mm(rt, kt, "nt", RGM), 0.0)
    p = _inv_unit_lower(lb)
    w1 = mm(p, zt, "nn", RWM)
    w2 = mm(p, mm(lk, v, "nn", RWM), "nn", RWM)
    a = mm(w1, bh, "tn", RAM) + eye * jnp.exp(tot)
    g = mm(w2, bh, "tn", RAM) + mm(v, kh, "tn", RAM)
    q = rt + mm(mb, w1, "nn", RWM)
    yl = mm(mb, w2, "nn", RWM) + mm(mk, v, "nn", RWM)
    return a, g, q, yl


def _chunks_in(ref, n):
    return jnp.concatenate([_heads_of(ref, pl.ds(s * CHUNK, CHUNK)) for s in range(n)], axis=0)


def _chunks_out(ref, val, n):
    for s in range(n):
        _heads_to(ref, pl.ds(s * CHUNK, CHUNK), val[s * NH:(s + 1) * NH])


def _r1_fwd(ins, name, comm=None):
    t = ins[0].shape[0]
    n = R1_CHUNKS
    nc = t // CHUNK

    def body(r, lw, k, v, z, b, a_ref, g_ref, q_ref, yl_ref):
        a, g, q, yl = _r1_fn(*[_chunks_in(x, n) for x in (r, lw, k, v, z, b)])
        for s in range(n):
            a_ref[s] = a[s * NH:(s + 1) * NH]
            g_ref[s] = g[s * NH:(s + 1) * NH]
        _chunks_out(q_ref, q, n)
        _chunks_out(yl_ref, yl, n)

    ck = pl.BlockSpec((n * CHUNK, DR), lambda c: (c, 0))
    st = pl.BlockSpec((n, NH, HN, HN), lambda c: (c, 0, 0, 0))
    return _pcall(
        body, name=name, grid=(nc // n,), in_specs=[ck] * 6, out_specs=[st, st, ck, ck],
        out_shape=[_sds((nc, NH, HN, HN))] * 2 + [_sds((t, DR))] * 2, args=tuple(ins), comm=comm)


def _r1_bwd(ins, da, dg, dq, dyl, name, comm=None):
    t = ins[0].shape[0]
    n = R1_CHUNKS
    nc = t // CHUNK

    def body(r, lw, k, v, z, b, da_ref, dg_ref, dq_ref, dyl_ref, *outs):
        _, vjp = jax.vjp(_r1_fn, *[_chunks_in(x, n) for x in (r, lw, k, v, z, b)])
        cts = (jnp.concatenate([da_ref[s] for s in range(n)], axis=0),
               jnp.concatenate([dg_ref[s] for s in range(n)], axis=0), _chunks_in(dq_ref, n), _chunks_in(dyl_ref, n))
        for ref, gval in zip(outs, vjp(cts)):
            _chunks_out(ref, gval, n)

    ck = pl.BlockSpec((n * CHUNK, DR), lambda c: (c, 0))
    st = pl.BlockSpec((n, NH, HN, HN), lambda c: (c, 0, 0, 0))
    return _pcall(
        body, name=name, grid=(nc // n,), in_specs=[ck] * 6 + [st, st, ck, ck], out_specs=[ck] * 6,
        out_shape=[_sds((t, DR))] * 6, args=(*ins, da, dg, dq, dyl), comm=comm)


def _r2_fwd(a, g, q, yl, name):
    nc = a.shape[0]
    t = q.shape[0]

    def body(a_ref, g_ref, q_ref, yl_ref, y_ref, s_ref, s):
        @pl.when(pl.program_id(0) == 0)
        def _():
            s[...] = jnp.zeros_like(s)
        s0 = s[...]
        for j in range(n):
            rows = pl.ds(j * CHUNK, CHUNK)
            s_ref[j] = s0
            _heads_to(y_ref, rows, mm(_heads_of(q_ref, rows), s0, "nt", R2M) + _heads_of(yl_ref, rows))
            s0 = mm(s0, a_ref[j], "nn", R2M) + g_ref[j]
        s[...] = s0

    n = math.gcd(nc, R2_CHUNKS)
    ck = pl.BlockSpec((n * CHUNK, DR), lambda c: (c, 0))
    st = pl.BlockSpec((n, NH, HN, HN), lambda c: (c, 0, 0, 0))
    return pl.pallas_call(
        body, name=name, grid=(nc // n,), in_specs=[st, st, ck, ck], out_specs=[ck, st],
        out_shape=[_sds((t, DR)), _sds((nc, NH, HN, HN))],
        scratch_shapes=[pltpu.VMEM((NH, HN, HN), f32)],
    )(a, g, q, yl)


def _r2_bwd(dy, q, s_all, a, name):
    nc = a.shape[0]
    t = q.shape[0]

    def body(dy_ref, q_ref, s_ref, a_ref, dq_ref, da_ref, dg_ref, ds):
        @pl.when(pl.program_id(0) == 0)
        def _():
            ds[...] = jnp.zeros_like(ds)
        dsn = ds[...]
        for j in reversed(range(n)):
            rows = pl.ds(j * CHUNK, CHUNK)
            s0 = s_ref[j]
            dyv = _heads_of(dy_ref, rows)
            dg_ref[j] = dsn
            da_ref[j] = mm(s0, dsn, "tn", R2M)
            _heads_to(dq_ref, rows, mm(dyv, s0, "nn", R2M))
            dsn = mm(dsn, a_ref[j], "nt", R2M) + mm(dyv, _heads_of(q_ref, rows), "tn", R2M)
        ds[...] = dsn

    n = math.gcd(nc, R2_CHUNKS)
    nb = nc // n
    ck = pl.BlockSpec((n * CHUNK, DR), lambda c: (nb - 1 - c, 0))
    st = pl.BlockSpec((n, NH, HN, HN), lambda c: (nb - 1 - c, 0, 0, 0))
    return pl.pallas_call(
        body, name=name, grid=(nb,), in_specs=[ck, ck, st, st], out_specs=[ck, st, st],
        out_shape=[_sds((t, DR)), _sds((nc, NH, HN, HN)), _sds((nc, NH, HN, HN))],
        scratch_shapes=[pltpu.VMEM((NH, HN, HN), f32)],
    )(dy, q, s_all, a)


def _ln_silu(c, w, b):
    mu = jnp.mean(c, axis=-1, keepdims=True)
    cc = c - mu
    var = jnp.mean(cc * cc, axis=-1, keepdims=True)
    u = cc * lax.rsqrt(var + LN_EPS) * w + b
    return u * jax.nn.sigmoid(u)


def _glu_tile(pc):
    return pc[:, :DC] * jax.nn.sigmoid(pc[:, DC:])


def _fill_glu(s_ref, pc_ref, halo_ref, first, tm):
    s_ref[pl.ds(0, HALO), :] = jnp.where(first, 0.0, _glu_tile(halo_ref[...]))
    s_ref[pl.ds(HALO, tm), :] = _glu_tile(pc_ref[...])


def _phase_copies(sh):
    n = sh.shape[1] - 8
    for r in range(1, 8):
        sh[r, pl.ds(0, n), :] = sh[0, pl.ds(r, n), :]


def _rows_at(sh, o, n):
    return sh[o % 8, pl.ds(o - o % 8, n), :]


def _conv_fwd(pc, dw, cb, lnw, lnb, name):
    t = pc.shape[0]
    tm = _tile(t, TM_VPU)
    hb = tm // HALO

    def body(pc_ref, halo_ref, dw_ref, cb_ref, w_ref, b_ref, c_ref, o_ref, s):
        _fill_glu(s.at[0], pc_ref, halo_ref, pl.program_id(0) == 0, tm)
        _phase_copies(s)
        for r0 in range(0, tm, CONV_ROWS):
            rows = pl.ds(r0, CONV_ROWS)
            acc = _rows_at(s, r0 + HALO - CW + 1, CONV_ROWS) * dw_ref[pl.ds(0, 1), :]
            for j in range(1, CW):
                acc = acc + _rows_at(s, r0 + HALO - CW + 1 + j, CONV_ROWS) * dw_ref[pl.ds(j, 1), :]
            c = acc + cb_ref[...]
            c_ref[rows, :] = c
            o_ref[rows, :] = _ln_silu(c, w_ref[...], b_ref[...]).astype(bf16)

    return pl.pallas_call(
        body, name=name, grid=(t // tm,),
        in_specs=[pl.BlockSpec((tm, 2 * DC), lambda i: (i, 0)),
                  pl.BlockSpec((HALO, 2 * DC), lambda i: (jnp.maximum(i * hb - 1, 0), 0)),
                  _full((CW, DC)), _full((1, DC)), _full((1, DC)), _full((1, DC))],
        out_specs=[pl.BlockSpec((tm, DC), lambda i: (i, 0)), pl.BlockSpec((tm, DC), lambda i: (i, 0))],
        out_shape=[_sds((t, DC)), _sds((t, DC), bf16)],
        scratch_shapes=[pltpu.VMEM((8, HALO + tm, DC), f32)],
    )(pc, pc, dw, cb, lnw, lnb)


def _conv_bwd1(dcat, c, lnw, lnb, name):
    t = c.shape[0]
    tm = _tile(t, TM_VPU)

    def body(d_ref, c_ref, w_ref, b_ref, dc_ref, dw_ref, db_ref, dcb_ref):
        _, vjp = jax.vjp(_ln_silu, c_ref[...], w_ref[...], b_ref[...])
        dc, dw, db = vjp(d_ref[...])
        dc_ref[...] = dc
        dcb = jnp.sum(dc, axis=0, keepdims=True)
        for ref, gval in ((dw_ref, dw), (db_ref, db), (dcb_ref, dcb)):
            @pl.when(pl.program_id(0) == 0)
            def _(ref=ref, gval=gval):
                ref[...] = gval

            @pl.when(pl.program_id(0) != 0)
            def _(ref=ref, gval=gval):
                ref[...] += gval

    tk = pl.BlockSpec((tm, DC), lambda i: (i, 0))
    return pl.pallas_call(
        body, name=name, grid=(t // tm,),
        in_specs=[pl.BlockSpec((tm, DC), lambda i: (i, 1)), tk, _full((1, DC)), _full((1, DC))],
        out_specs=[tk] + [_full((1, DC))] * 3,
        out_shape=[_sds((t, DC))] + [_sds((1, DC))] * 3,
    )(dcat, c, lnw, lnb)


def _conv_bwd2(dc, pc, dw, name):
    t = pc.shape[0]
    tm = _tile(t, TM_VPU)
    hb = tm // HALO
    last_blk = t // HALO - 1

    def body(dc_ref, dn_ref, pc_ref, halo_ref, dw_ref, dpc_ref, ddw_ref, s, sd, acc):
        i = pl.program_id(0)
        _fill_glu(s.at[0], pc_ref, halo_ref, i == 0, tm)
        sd[0, pl.ds(0, tm), :] = dc_ref[...]
        sd[0, pl.ds(tm, HALO), :] = jnp.where(i == pl.num_programs(0) - 1, 0.0, dn_ref[...])
        _phase_copies(s)
        _phase_copies(sd)
        for r0 in range(0, tm, CONV_ROWS):
            rows = pl.ds(r0, CONV_ROWS)
            dcb = sd[0, rows, :]
            dglu = None
            for j in range(CW):
                term = _rows_at(sd, r0 + CW - 1 - j, CONV_ROWS) * dw_ref[pl.ds(j, 1), :]
                dglu = term if dglu is None else dglu + term
                part = dcb * _rows_at(s, r0 + HALO - CW + 1 + j, CONV_ROWS)
                part8 = part[0:8]
                for q in range(8, CONV_ROWS, 8):
                    part8 = part8 + part[q:q + 8]
                if r0 == 0:
                    acc[pl.ds(8 * j, 8), :] = part8
                else:
                    acc[pl.ds(8 * j, 8), :] += part8
            a = pc_ref[rows, :DC]
            sg = jax.nn.sigmoid(pc_ref[rows, DC:])
            dpc_ref[rows, :DC] = dglu * sg
            dpc_ref[rows, DC:] = dglu * a * sg * (1.0 - sg)
        rows = [jnp.sum(acc[pl.ds(8 * j, 8), :], axis=0, keepdims=True) for j in range(CW)]

        @pl.when(i == 0)
        def _():
            for j in range(CW):
                ddw_ref[pl.ds(j, 1), :] = rows[j]

        @pl.when(i != 0)
        def _():
            for j in range(CW):
                ddw_ref[pl.ds(j, 1), :] += rows[j]

    return pl.pallas_call(
        body, name=name, grid=(t // tm,),
        in_specs=[pl.BlockSpec((tm, DC), lambda i: (i, 0)),
                  pl.BlockSpec((HALO, DC), lambda i: (jnp.minimum((i + 1) * hb, last_blk), 0)),
                  pl.BlockSpec((tm, 2 * DC), lambda i: (i, 0)),
                  pl.BlockSpec((HALO, 2 * DC), lambda i: (jnp.maximum(i * hb - 1, 0), 0)),
                  _full((CW, DC))],
        out_specs=[pl.BlockSpec((tm, 2 * DC), lambda i: (i, 0)), _full((CW, DC))],
        out_shape=[_sds((t, 2 * DC)), _sds((CW, DC))],
        scratch_shapes=[pltpu.VMEM((8, HALO + tm, DC), f32), pltpu.VMEM((8, tm + HALO, DC), f32),
                        pltpu.VMEM((8 * CW, DC), f32)],
    )(dc, dc, pc, pc, dw)


def _adamw(w, g, m, v):
    m = ADAM_B1 * m + (1.0 - ADAM_B1) * g
    v = ADAM_B2 * v + (1.0 - ADAM_B2) * (g * g)
    m_hat = m / (1.0 - ADAM_B1 ** ADAM_STEP)
    v_hat = v / (1.0 - ADAM_B2 ** ADAM_STEP)
    delta = -ADAM_LR * (m_hat / (jnp.sqrt(v_hat) + ADAM_EPS) + ADAM_WD * w)
    return delta, m, v


def _adam_slots(w, slots, m, v, name):
    r, c = w.shape
    tr = next(cand for cand in (512, 352, 256, 128, r) if r % cand == 0)

    def body(w_ref, s_ref, m_ref, v_ref, g_ref, d_ref, nm_ref, nv_ref):
        g = s_ref[0].astype(f32)
        for k in range(1, NDEV):
            g = g + s_ref[k].astype(f32)
        delta, nm, nv = _adamw(w_ref[...], g, m_ref[...], v_ref[...])
        g_ref[...] = g
        d_ref[...] = delta
        nm_ref[...] = nm
        nv_ref[...] = nv

    blk = pl.BlockSpec((tr, c), lambda i: (i, 0))
    return pl.pallas_call(
        body, name=name, grid=(r // tr,),
        in_specs=[blk, pl.BlockSpec((NDEV, tr, c), lambda i: (0, i, 0)), blk, blk],
        out_specs=[blk] * 4, out_shape=[_sds((r, c))] * 4,
    )(w, slots, m, v)


def _sum_slots(slots, name):
    _, r, c = slots.shape

    def body(s_ref, o_ref):
        g = s_ref[0]
        for k in range(1, NDEV):
            g = g + s_ref[k]
        o_ref[...] = g

    return pl.pallas_call(body, name=name, in_specs=[_full((NDEV, r, c))], out_specs=_full((r, c)),
                          out_shape=_sds((r, c)), grid=(1,))(slots)


def _adam_many(ws, gs, ms, vs, name):
    n = len(ws)
    shapes = [w.shape for w in ws]

    def body(*refs):
        for i in range(n):
            w_ref, g_ref, m_ref, v_ref = refs[i], refs[n + i], refs[2 * n + i], refs[3 * n + i]
            delta, nm, nv = _adamw(w_ref[...], g_ref[...], m_ref[...], v_ref[...])
            refs[4 * n + i][...] = delta
            refs[5 * n + i][...] = nm
            refs[6 * n + i][...] = nv

    outs = pl.pallas_call(body, name=name, grid=(1,), in_specs=[_full(s) for s in shapes] * 4,
                          out_specs=[_full(s) for s in shapes] * 3, out_shape=[_sds(s) for s in shapes] * 3,
                          )(*ws, *gs, *ms, *vs)
    return outs[:n], outs[n:2 * n], outs[2 * n:]


def _pack(pieces, total):
    flat = []
    n = 0
    for p in pieces:
        p = p.reshape(-1)
        pad = (-p.shape[0]) % LANES
        if pad:
            p = jnp.pad(p, (0, pad))
        flat.append(p)
        n += p.shape[0]
    if total > n:
        flat.append(jnp.zeros((total - n,), f32))
    return jnp.concatenate(flat).reshape(total // LANES, LANES)


def _unpack(vec, shapes):
    flat = vec.reshape(-1)
    out, off = [], 0
    for s in shapes:
        n = math.prod(s)
        out.append(flat[off:off + n].reshape(s))
        off += n + (-n) % LANES
    return out


def _round_up(n, m):
    return (n + m - 1) // m * m


def kernel(x, ffn1_norm_pre, ffn1_norm_post, ffn1_w_gu, ffn1_w_down, mix_norm_pre, mix_norm_post, w_in, shift_mu, w_up, w0, a_up, a0, g_up, k_k, k_a, r_k, gn_w, gn_b, conv_dw, conv_b, conv_ln_w, conv_ln_b, w_out, ffn2_norm_pre, ffn2_norm_post, ffn2_w_gu, ffn2_w_down, loss_target, m_ffn1_norm_pre, m_ffn1_norm_post, m_ffn1_w_gu, m_ffn1_w_down, m_mix_norm_pre, m_mix_norm_post, m_w_in, m_shift_mu, m_w_up, m_w0, m_a_up, m_a0, m_g_up, m_k_k, m_k_a, m_r_k, m_gn_w, m_gn_b, m_conv_dw, m_conv_b, m_conv_ln_w, m_conv_ln_b, m_w_out, m_ffn2_norm_pre, m_ffn2_norm_post, m_ffn2_w_gu, m_ffn2_w_down, v_ffn1_norm_pre, v_ffn1_norm_post, v_ffn1_w_gu, v_ffn1_w_down, v_mix_norm_pre, v_mix_norm_post, v_w_in, v_shift_mu, v_w_up, v_w0, v_a_up, v_a0, v_g_up, v_k_k, v_k_a, v_r_k, v_gn_w, v_gn_b, v_conv_dw, v_conv_b, v_conv_ln_w, v_conv_ln_b, v_w_out, v_ffn2_norm_pre, v_ffn2_norm_post, v_ffn2_w_gu, v_ffn2_w_down):
    t = x.shape[1]
    x0 = x.reshape(t, D)
    tgt = loss_target.reshape(t, D)

    def shard(a):
        return a[0].astype(bf16)

    def shard_t(a):
        return jnp.swapaxes(a[0], 0, 1).astype(bf16)

    (wg1,) = _exchange([shard_t(ffn1_w_gu)], ["gather"], [SLOT42], "gather_wg1")

    def cols(a):
        return jnp.transpose(a, (1, 0, 2)).reshape(a.shape[1], NDEV * a.shape[2])

    lane = jnp.arange(DR, dtype=jnp.int32) // HN
    seg = (lane[:, None] == jnp.arange(LANES, dtype=jnp.int32)[None, :]).astype(f32)
    seg_t = seg.T
    rk_row = r_k.reshape(1, DR)
    post_params = (gn_w, gn_b, rk_row, seg, seg_t)

    (xn1, gu1, h1), (wd1, win_g) = _ffn_in_fwd(x0, ffn1_norm_pre, wg1, "ffn1_in_fwd", comm=(
        [shard(ffn1_w_down), shard_t(w_in)], ["gather"] * 2, [SLOT8] * 2))
    wd1 = wd1.reshape(4, FB, D)
    win_t = win_g.reshape(DIN, D)
    (f1, x1), (wout_g, wup_g, aup_g, gup_g, cdw_g) = _mm_norm_res_fwd(
        h1, wd1, x0, ffn1_norm_post, 0.5, "ffn1_out_fwd", comm=(
            [shard(w_out), w_up[0], a_up[0], g_up[0], conv_dw[0]], ["gather"] * 5, [SLOT8] * 5))
    wout_full = wout_g.reshape(1, D, D)
    wup_full, aup_full, gup_full, cdw_full = cols(wup_g), cols(aup_g), cols(gup_g), cols(cdw_g)
    zeros64 = jnp.zeros((HN, DR), f32)
    wup_pad = jnp.concatenate([wup_full, zeros64], axis=0)
    aup_pad = jnp.concatenate([zeros64, aup_full], axis=0)
    prep_params = (shift_mu, w0, a0, k_k, k_a, wup_pad, aup_pad, gup_full, seg, seg_t)
    hm, ps, pc = _norm_mm_fwd(x1, mix_norm_pre, win_t, "mix_in_fwd")
    rec_in, (wd2,) = _prep_fwd(ps, prep_params, "prep_fwd", comm=([shard(ffn2_w_down)], ["gather"], [SLOT8]))
    wd2 = wd2.reshape(4, FB, D)
    g_gate = rec_in[6]
    rec_in = rec_in[:6]
    (a_c, g_c, q_c, yl_c), (wg2,) = _r1_fwd(rec_in, "r1_fwd", comm=([shard_t(ffn2_w_gu)], ["gather"], [SLOT42]))
    y_rec, s_all = _r2_fwd(a_c, g_c, q_c, yl_c, "r2_fwd")
    c_conv, out_b = _conv_fwd(pc, cdw_full, conv_b, conv_ln_w, conv_ln_b, "conv_fwd")
    cat = _post_fwd(y_rec, rec_in[0], rec_in[2], rec_in[3], g_gate, out_b, post_params, "post_fwd")
    fm, x2 = _mm_norm_res_fwd(cat.reshape(1, t, D), wout_full, x1, mix_norm_post, 1.0, "mix_out_fwd")
    (xn2, gu2, h2), _ = _ffn_in_fwd(x2, ffn2_norm_pre, wg2, "ffn2_in_fwd")
    f2, dy, loss_part = _mm_norm_res_fwd(h2, wd2, x2, ffn2_norm_post, 0.5, "ffn2_out_fwd", tgt=tgt)

    (df2, dg_post2, dgu2), _ = _post_bwd_nt(dy, f2, ffn2_norm_post, wd2, 0.5, gu2, "ffn2_out_bwd")
    (dx2, dg_pre2), _ = _nt_pre_bwd(dgu2, wg2, x2, ffn2_norm_pre, dy, "ffn2_in_bwd")
    dwd2 = _wgrad(h2, df2.reshape(1, t, D), True, False, D, "ffn2_wd_grad")
    dwg2 = _wgrad(dgu2.reshape(8, t, FB), xn2.reshape(1, t, D), True, False, D, "ffn2_wg_grad")
    (dfm, dg_postm, dcat), _ = _post_bwd_nt(dx2, fm, mix_norm_post, wout_full, 1.0, None, "mix_out_bwd")
    dwout = _wgrad(cat.reshape(1, t, D), dfm.reshape(1, t, D), False, False, D, "wout_grad")
    dcat = dcat.reshape(t, D)
    (dy_rec, dr2, dkp2, dv2, dgate, dgn_w, dgn_b, drk), (s_wout,) = _post_bwd(
        dcat, y_rec, rec_in[0], rec_in[2], rec_in[3], g_gate, post_params, "post_bwd", comm=(
            [dwout.reshape(NDEV, D // NDEV, D)], ["scatter"], [SLOT8]))
    dq_c, da_c, dg_c = _r2_bwd(dy_rec, q_c, s_all, a_c, "r2_bwd")
    rec_grads, got = _r1_bwd(rec_in, da_c, dg_c, dq_c, dy_rec, "r1_bwd", comm=(
        [dwg2.reshape(4, 2, FB, D), dwd2.reshape(NDEV, F // NDEV, D)], ["scatter"] * 2, [SLOT42, SLOT8]))
    s_wg2, s_wd2 = got
    prep_out = _prep_bwd(ps, prep_params, rec_grads, (dr2, dkp2, dv2, dgate), "prep_bwd")
    dpsl, dw0, da0, dkk, dka, dwup_pad, daup_pad, dgup = prep_out
    dc_conv, dlnw, dlnb, dcb = _conv_bwd1(dcat, c_conv, conv_ln_w, conv_ln_b, "conv_bwd1")
    dpc, dcdw = _conv_bwd2(dc_conv, pc, cdw_full, "conv_bwd2")
    dp, dmu = _shift_bwd(dpsl, ps, shift_mu, dpc, "shift_bwd")
    (dx1, dg_prem), _ = _nt_pre_bwd(dp.reshape(1, 1, t, DIN), win_t.reshape(1, 1, DIN, D), x1, mix_norm_pre, dx2,
                                    "mix_in_bwd")
    dwin_s = _wgrad(dp.reshape(1, t, DIN), hm.reshape(1, t, D), False, False, D // 2, "win_grad").reshape(
        NDEV, DIN // NDEV, D)
    (df1, dg_post1, dgu1), (s_win,) = _post_bwd_nt(dx1, f1, ffn1_norm_post, wd1, 0.5, gu1, "ffn1_out_bwd", comm=(
        [dwin_s], ["scatter"], [SLOT8]))
    dwd1 = _wgrad(h1, df1.reshape(1, t, D), True, False, D, "ffn1_wd_grad")
    dwg1, (s_wd1,) = _wgrad(dgu1.reshape(8, t, FB), xn1.reshape(1, t, D), True, False, D, "ffn1_wg_grad", comm=(
        [dwd1.reshape(NDEV, F // NDEV, D)], ["scatter"], [SLOT8]))
    wg1_sends, wg1_recvs, dwg1_thru, wg1_land, started = _scatter_start(dwg1.reshape(4, 2, FB, D), SLOT42, "wg1_scatter_start")
    (dx0, dg_pre1), _ = _nt_pre_bwd(dgu1, wg1, x0, ffn1_norm_pre + started[:1, :1], dx1, "ffn1_in_bwd")

    res = {}

    def adam_big(nm, w, s, m, v, transposed):
        view = (lambda a: jnp.swapaxes(a[0], 0, 1)) if transposed else (lambda a: a[0])
        outs = _adam_slots(view(w), s, view(m), view(v), "adam_" + nm)
        res[nm] = [(jnp.swapaxes(o, 0, 1) if transposed else o)[None] for o in outs]

    adam_big("ffn2_w_gu", ffn2_w_gu, s_wg2.reshape(NDEV, FB, D), m_ffn2_w_gu, v_ffn2_w_gu, True)
    adam_big("ffn2_w_down", ffn2_w_down, s_wd2, m_ffn2_w_down, v_ffn2_w_down, False)
    adam_big("w_in", w_in, s_win, m_w_in, v_w_in, True)
    adam_big("w_out", w_out, s_wout, m_w_out, v_w_out, False)
    adam_big("ffn1_w_down", ffn1_w_down, s_wd1, m_ffn1_w_down, v_ffn1_w_down, False)
    _, s_wg1 = _scatter_wait(wg1_sends, wg1_recvs, dwg1_thru, wg1_land, res["ffn1_w_down"][1], SLOT42, "wg1_scatter_wait")
    my_slot = SLOT42[1](4 * lax.axis_index("x") + 2 * lax.axis_index("y") + lax.axis_index("c"))
    own = lax.dynamic_slice(dwg1.reshape(4, 2, FB, D), (*my_slot, 0, 0), (1, 1, FB, D))
    s_wg1 = lax.dynamic_update_slice(s_wg1, own, (*my_slot, 0, 0))
    rep_shapes = [(1, 1)] + [(1, D)] * 6 + [(1, DS)] + [(1, DR)] * 10
    rep_parts = [loss_part, dg_pre1, dg_post1, dg_prem, dg_postm, dg_pre2, dg_post2, dmu,
                 dw0, da0, dkk, dka, drk, dgn_w, dgn_b, dcb, dlnw, dlnb]
    n_rep = _round_up(sum(_round_up(math.prod(s), LANES) for s in rep_shapes), 8 * LANES)
    sh_shapes = [(HN, HN), (HN, HN), (LANES, HN), (CW, HN)]
    n_sh = _round_up(sum(_round_up(math.prod(s), LANES) for s in sh_shapes), 8 * LANES)
    sh_parts = []
    for a in (dwup_pad[:HN], daup_pad[HN:], dgup, dcdw):
        rows = jnp.transpose(a.reshape(a.shape[0], NDEV, HN), (1, 0, 2)).reshape(NDEV, a.shape[0] * HN)
        sh_parts.append(jnp.pad(rows, ((0, 0), (0, (-rows.shape[1]) % LANES))))
    sh_vec = jnp.concatenate(sh_parts, axis=1)
    sh_vec = jnp.pad(sh_vec, ((0, 0), (0, n_sh - sh_vec.shape[1]))).reshape(NDEV, n_sh // LANES, LANES)
    rep_vec, sh_vec, s_wg1 = lax.optimization_barrier((_pack(rep_parts, n_rep), sh_vec, s_wg1))
    s_rep, s_sh = _exchange([rep_vec, sh_vec], ["gather", "scatter"], [SLOT8, SLOT8], "exchange_small")
    adam_big("ffn1_w_gu", ffn1_w_gu, s_wg1.reshape(NDEV, FB, D), m_ffn1_w_gu, v_ffn1_w_gu, True)


    gsum = _unpack(_sum_slots(s_rep, "sum_rep"), rep_shapes)
    gsh = _unpack(_sum_slots(s_sh, "sum_sharded"), sh_shapes)
    loss = gsum[0].reshape(())
    rep_names = ["ffn1_norm_pre", "ffn1_norm_post", "mix_norm_pre", "mix_norm_post", "ffn2_norm_pre", "ffn2_norm_post",
                 "shift_mu", "w0", "a0", "k_k", "k_a", "r_k", "gn_w", "gn_b", "conv_b", "conv_ln_w", "conv_ln_b"]
    shard_names = ["w_up", "a_up", "g_up", "conv_dw"]
    env = dict(locals())
    small_g = {n: gsum[1 + i] for i, n in enumerate(rep_names)}
    small_g["r_k"] = small_g["r_k"].reshape(1, NH, HN)
    for i, n in enumerate(shard_names):
        small_g[n] = gsh[i][None]
    names = rep_names + shard_names
    shapes = [env[n].shape for n in names]

    def flat2(a):
        return a.reshape(1, DR) if a.shape == (1, NH, HN) else a.reshape(a.shape[-2:])

    deltas, new_ms, new_vs = _adam_many(*[[flat2(d[n]) for n in names] for d in (
        {n: env[n] for n in names}, small_g, {n: env["m_" + n] for n in names}, {n: env["v_" + n] for n in names})],
        "adam_small")
    for n, shp, dl, nm_, nv_ in zip(names, shapes, deltas, new_ms, new_vs):
        res[n] = [small_g[n], dl.reshape(shp), nm_.reshape(shp), nv_.reshape(shp)]

    order = ["ffn1_norm_pre", "ffn1_norm_post", "ffn1_w_gu", "ffn1_w_down", "mix_norm_pre", "mix_norm_post", "w_in",
             "shift_mu", "w_up", "w0", "a_up", "a0", "g_up", "k_k", "k_a", "r_k", "gn_w", "gn_b", "conv_dw", "conv_b",
             "conv_ln_w", "conv_ln_b", "w_out", "ffn2_norm_pre", "ffn2_norm_post", "ffn2_w_gu", "ffn2_w_down"]
    return (loss, dx0.reshape(1, t, D), *[res[n][0] for n in order], *[res[n][1] for n in order],
            *[res[n][2] for n in order], *[res[n][3] for n in order])
```

```python
import functools
import math

import jax
import jax.numpy as jnp
from jax import lax
from jax.experimental import pallas as pl
from jax.experimental.pallas import tpu as pltpu

f32 = jnp.float32
bf16 = jnp.bfloat16

D = 1024
F = 2816
FB = 704
DR = 512
DC = 512
NH = 8
HN = 64
DS = 1792
DIN = 2816
CW = 31
CHUNK = 64
R1_CHUNKS = 4
R2_CHUNKS = 8
NDEV = 8
RMS_EPS = 1e-6
GN_EPS = 64e-5
LN_EPS = 1e-5
DECAY_SCALE = math.exp(-0.5)
ADAM_LR, ADAM_B1, ADAM_B2, ADAM_EPS, ADAM_WD, ADAM_STEP = 0.001, 0.9, 0.999, 1e-08, 0.01, 10
LANES = 128
HALO = 32
CONV_ROWS = 32
TM_MXU = 1024
TM_SUB = 256
TM_VPU = 256

ANY = pl.BlockSpec(memory_space=pl.ANY)


def _full(shape):
    return pl.BlockSpec(shape, lambda *_: (0,) * len(shape))


def _sds(shape, dtype=f32):
    return jax.ShapeDtypeStruct(shape, dtype)


def _dot(a, b):
    return jnp.dot(a, b, preferred_element_type=f32)


def _dot_nt(a, b):
    return lax.dot_general(a, b, (((1,), (1,)), ((), ())), preferred_element_type=f32)


def _dot_tn(a, b):
    return lax.dot_general(a, b, (((0,), (0,)), ((), ())), preferred_element_type=f32)


def _terms(a, n):
    out, rem = [], a
    for i in range(n):
        t = rem.astype(bf16)
        out.append(t)
        if i + 1 < n:
            rem = rem - t.astype(f32)
    return out


def _mm_raw(a, b, kind, mode):
    nb = a.ndim - 2
    bd = tuple(range(nb))
    ca = nb if kind == "tn" else nb + 1
    cb = nb + 1 if kind == "nt" else nb
    dn = (((ca,), (cb,)), (bd, bd))
    pa, pb = mode[:2]
    ta, tb = _terms(a, pa), _terms(b, pb)
    acc = None
    for i in range(pa):
        for j in range(pb):
            if i + j < max(pa, pb):
                p = lax.dot_general(ta[i], tb[j], dn, preferred_element_type=f32)
                acc = p if acc is None else acc + p
    return acc


@functools.partial(jax.custom_vjp, nondiff_argnums=(2, 3))
def mm(a, b, kind, mode):
    return _mm_raw(a, b, kind, mode)


def _mm_fwd(a, b, kind, mode):
    return _mm_raw(a, b, kind, mode), (a, b)


def _ct_terms(keep):
    return EXACT_TERMS if keep == 1 else keep


def _mm_bwd(kind, mode, res, ct):
    a, b = res
    pa, pb = mode[:2]
    if len(mode) == 4:
        ca = cb = mode[2]
        pa = pb = mode[3]
    elif mode == (1, 1):
        ca = cb = 1
    else:
        ca, cb = _ct_terms(pb), _ct_terms(pa)
    if kind == "nn":
        da = mm(ct, b, "nt", (ca, pb))
        db = mm(a, ct, "tn", (pa, cb))
    elif kind == "nt":
        da = mm(ct, b, "nn", (ca, pb))
        db = mm(ct, a, "tn", (cb, pa))
    else:
        da = mm(b, ct, "nt", (pb, ca))
        db = mm(a, ct, "nn", (pa, cb))
    return da, db


mm.defvjp(_mm_fwd, _mm_bwd)

X3 = (2, 2)
EXACT_TERMS = 2
EA = (1, EXACT_TERMS)
EB = (EXACT_TERMS, 1)
X1 = (1, 1)
RGM = (1, 1, 1, 2)
RIM = X1
RWM = X1
RAM = X1
R2M = X1
LORA = X1


def _rms_fwd(x, g):
    inv = lax.rsqrt(jnp.mean(x * x, axis=-1, keepdims=True) + RMS_EPS)
    return x * inv * g


def _rms_bwd(x, g, dy):
    inv = lax.rsqrt(jnp.mean(x * x, axis=-1, keepdims=True) + RMS_EPS)
    xh = x * inv
    dxh = dy * g
    dg = jnp.sum(dy * xh, axis=0, keepdims=True)
    dx = inv * (dxh - xh * jnp.mean(dxh * xh, axis=-1, keepdims=True))
    return dx, dg


def _tile(t, want):
    return min(t, want)


NPEER = NDEV - 1


def _comm_out_shape(arrays, modes, slots):
    out = []
    for a, mode, (lead, _) in zip(arrays, modes, slots):
        shp = tuple(lead) + tuple(a.shape) if mode == "gather" else tuple(a.shape)
        out.append(pltpu.HBM(shp, a.dtype))
    return out


def _comm_sems(n):
    return [pltpu.SemaphoreType.DMA((n * NPEER,)), pltpu.SemaphoreType.DMA((n * NPEER,)), pltpu.SemaphoreType.DMA((n,))]


RELAYED = (3, 5, 7)


def _comm_copies(ins, outs, sems, modes, slots, want):
    n = len(ins)
    send_sems, recv_sems, loc_sems = sems
    x, y, c = lax.axis_index("x"), lax.axis_index("y"), lax.axis_index("c")
    me = 4 * x + 2 * y + c

    def peer_of(k):
        px = 1 - x if k & 4 else x
        py = 1 - y if k & 2 else y
        pc = 1 - c if k & 1 else c
        return (px, py, pc), 4 * px + 2 * py + pc

    def slot(i, dev):
        return outs[i].at[slots[i][1](dev)]

    if want == "local":
        return [pltpu.make_async_copy(ins[i] if modes[i] == "gather" else ins[i].at[slots[i][1](me)], slot(i, me),
                                      loc_sems.at[i]) for i in range(n)]
    sibling = peer_of(1)[0]
    out = []
    for k in range(1, NDEV):
        peer, pid = peer_of(k)
        for i in range(n):
            sem = dict(send_sem=send_sems.at[i * NPEER + k - 1], recv_sem=recv_sems.at[i * NPEER + k - 1],
                       device_id_type=pl.DeviceIdType.MESH)
            relayed = modes[i] == "gather" and k in RELAYED
            if want == "recv":
                src = ins[i] if modes[i] == "gather" else ins[i].at[slots[i][1](pid)]
                out.append(pltpu.make_async_remote_copy(src_ref=src, dst_ref=slot(i, pid), device_id=peer, **sem))
            elif want == "first" and not relayed:
                src = ins[i] if modes[i] == "gather" else ins[i].at[slots[i][1](pid)]
                out.append(pltpu.make_async_remote_copy(src_ref=src, dst_ref=slot(i, me), device_id=peer, **sem))
            elif want == "relay" and relayed:
                origin = peer_of(k - 1)[1]
                out.append(((k - 2) * n + i, pltpu.make_async_remote_copy(
                    src_ref=slot(i, origin), dst_ref=slot(i, origin), device_id=sibling, **sem)))
    return out


def _comm_start(ins, outs, sems, modes, slots):
    for cp in _comm_copies(ins, outs, sems, modes, slots, "local") + _comm_copies(ins, outs, sems, modes, slots, "first"):
        cp.start()


def _comm_wait(ins, outs, sems, modes, slots):
    recvs = _comm_copies(ins, outs, sems, modes, slots, "recv")
    relays = _comm_copies(ins, outs, sems, modes, slots, "relay")
    for idx, cp in relays:
        recvs[idx].wait_recv()
        cp.start()
    passed_on = {idx for idx, _ in relays}
    for idx, cp in enumerate(recvs):
        if idx not in passed_on:
            cp.wait_recv()
    for cp in _comm_copies(ins, outs, sems, modes, slots, "first") + [cp for _, cp in relays]:
        cp.wait_send()
    for cp in _comm_copies(ins, outs, sems, modes, slots, "local"):
        cp.wait()


def _exchange(arrays, modes, slots, name):
    n = len(arrays)

    def body(*refs):
        ins, outs, sems = refs[:n], refs[n:2 * n], refs[2 * n:]
        _comm_start(ins, outs, sems, modes, slots)
        _comm_wait(ins, outs, sems, modes, slots)

    return pl.pallas_call(
        body, name=name, out_shape=_comm_out_shape(arrays, modes, slots),
        in_specs=[ANY] * n, out_specs=[ANY] * n, scratch_shapes=_comm_sems(n),
    )(*arrays)


HBM_SPEC = pl.BlockSpec(memory_space=pltpu.HBM)
SEM_SPEC = pl.BlockSpec(memory_space=pltpu.SEMAPHORE)


def _scatter_start(arr, slot, name):
    def body(v_ref, land_ref, send_sems, recv_sems, v_thru, land_thru, token):
        for cp in _comm_copies([v_ref], [land_ref], (send_sems, recv_sems, None), ["scatter"], [slot], "first"):
            cp.start()
        token[...] = jnp.zeros_like(token)

    return pl.pallas_call(
        body, name=name,
        out_shape=(pltpu.SemaphoreType.DMA((NPEER,)), pltpu.SemaphoreType.DMA((NPEER,)),
                   pltpu.HBM(arr.shape, arr.dtype), pltpu.HBM(arr.shape, arr.dtype), _sds((8, LANES))),
        in_specs=(HBM_SPEC, HBM_SPEC),
        out_specs=(SEM_SPEC, SEM_SPEC, HBM_SPEC, HBM_SPEC, pl.BlockSpec(memory_space=pltpu.VMEM)),
        input_output_aliases={0: 2, 1: 3},
        compiler_params=pltpu.CompilerParams(has_side_effects=pltpu.SideEffectType.DATAFLOW_SIDE_EFFECTING),
    )(pltpu.with_memory_space_constraint(arr, pltpu.HBM),
      pltpu.with_memory_space_constraint(lax.empty(arr.shape, arr.dtype), pltpu.HBM))


def _scatter_wait(send_sems, recv_sems, v_thru, land_thru, after, slot, name):
    def body(v_ref, land_ref, send_sems, recv_sems, *rest):
        sems = (send_sems, recv_sems, None)
        for cp in _comm_copies([v_ref], [land_ref], sems, ["scatter"], [slot], "first"):
            cp.wait_send()
        for cp in _comm_copies([v_ref], [land_ref], sems, ["scatter"], [slot], "recv"):
            cp.wait_recv()

    return pl.pallas_call(
        body, name=name, out_shape=(pltpu.HBM(v_thru.shape, v_thru.dtype), pltpu.HBM(v_thru.shape, v_thru.dtype)),
        in_specs=(HBM_SPEC, HBM_SPEC, SEM_SPEC, SEM_SPEC) + (ANY,) * len(after), out_specs=(HBM_SPEC, HBM_SPEC),
        input_output_aliases={0: 0, 1: 1},
        compiler_params=pltpu.CompilerParams(has_side_effects=pltpu.SideEffectType.DATAFLOW_SIDE_EFFECTING),
    )(v_thru, land_thru, send_sems, recv_sems, *after)


def _pcall(body, *, name, grid, in_specs, out_specs, out_shape, args, scratch_shapes=(), comm=None):
    if comm is None:
        return pl.pallas_call(body, name=name, grid=grid, in_specs=in_specs, out_specs=out_specs, out_shape=out_shape,
                              scratch_shapes=list(scratch_shapes))(*args), None
    arrays, modes, slots = comm
    n_in, n_out, n_scr, nc = len(args), len(out_shape), len(scratch_shapes), len(arrays)

    def hosted(*refs):
        a_in, c_in = refs[:n_in], refs[n_in:n_in + nc]
        o = n_in + nc
        a_out, c_out = refs[o:o + n_out], refs[o + n_out:o + n_out + nc]
        o += n_out + nc
        a_scr, sems = refs[o:o + n_scr], refs[o + n_scr:]
        first = pl.program_id(0) == 0
        last = pl.program_id(0) == grid[0] - 1
        for ax in range(1, len(grid)):
            first = jnp.logical_and(first, pl.program_id(ax) == 0)
            last = jnp.logical_and(last, pl.program_id(ax) == grid[ax] - 1)

        @pl.when(first)
        def _():
            _comm_start(c_in, c_out, sems, modes, slots)

        body(*a_in, *a_out, *a_scr)

        @pl.when(last)
        def _():
            _comm_wait(c_in, c_out, sems, modes, slots)

    res = pl.pallas_call(
        hosted, name=name, grid=grid, in_specs=list(in_specs) + [ANY] * nc, out_specs=list(out_specs) + [ANY] * nc,
        out_shape=list(out_shape) + _comm_out_shape(arrays, modes, slots),
        scratch_shapes=list(scratch_shapes) + _comm_sems(nc),
    )(*args, *arrays)
    return res[:n_out], res[n_out:]


SLOT8 = ((NDEV,), lambda d: (d,))
SLOT42 = ((4, 2), lambda d: (lax.rem(d, 4), lax.div(d, 4)))


def _ffn_in_fwd(x, g, wg4, name, comm=None):
    t = x.shape[0]
    tm = _tile(t, TM_MXU)
    sub = _tile(tm, TM_SUB)

    def body(x_ref, g_ref, w_ref, xn_ref, gu_ref, h_ref):
        for s in range(tm // sub):
            rows = pl.ds(s * sub, sub)
            xn = _rms_fwd(x_ref[rows, :], g_ref[...]).astype(bf16)
            xn_ref[rows, :] = xn
            gate = _dot_nt(xn, w_ref[0, 0])
            up = _dot_nt(xn, w_ref[0, 1])
            gu_ref[0, 0, rows, :] = gate.astype(bf16)
            gu_ref[0, 1, rows, :] = up.astype(bf16)
            h_ref[0, rows, :] = (gate * jax.nn.sigmoid(gate) * up).astype(bf16)

    return _pcall(
        body, name=name, grid=(t // tm, 4),
        in_specs=[pl.BlockSpec((tm, D), lambda i, e: (i, 0)), _full((1, D)),
                  pl.BlockSpec((1, 2, FB, D), lambda i, e: (e, 0, 0, 0))],
        out_specs=[pl.BlockSpec((tm, D), lambda i, e: (i, 0)),
                   pl.BlockSpec((1, 2, tm, FB), lambda i, e: (e, 0, i, 0)),
                   pl.BlockSpec((1, tm, FB), lambda i, e: (e, i, 0))],
        out_shape=[_sds((t, D), bf16), _sds((4, 2, t, FB), bf16), _sds((4, t, FB), bf16)],
        args=(x, g, wg4), comm=comm)


def _mm_norm_res_fwd(hb, wb, xres, g, scale, name, comm=None, tgt=None):
    e_n, t, k = hb.shape
    tm = _tile(t, TM_MXU)

    def body(h_ref, w_ref, x_ref, g_ref, *rest):
        f = _dot(h_ref[0], w_ref[0])
        for e in range(1, e_n):
            f = f + _dot(h_ref[e], w_ref[e])
        xnew = x_ref[...] + scale * _rms_fwd(f, g_ref[...])
        if tgt is None:
            f_ref, o_ref = rest
            o_ref[...] = xnew
        else:
            t_ref, f_ref, o_ref, l_ref = rest
            err = xnew - t_ref[...]
            o_ref[...] = err * (1.0 / D)
            part = 0.5 * jnp.sum(jnp.mean(err * err, axis=-1, keepdims=True), axis=0, keepdims=True)

            @pl.when(pl.program_id(0) == 0)
            def _():
                l_ref[...] = part

            @pl.when(pl.program_id(0) != 0)
            def _():
                l_ref[...] += part
        f_ref[...] = f

    tile = pl.BlockSpec((tm, D), lambda i: (i, 0))
    outs, got = _pcall(
        body, name=name, grid=(t // tm,),
        in_specs=[pl.BlockSpec((e_n, tm, k), lambda i: (0, i, 0)), _full((e_n, k, D)), tile, _full((1, D))]
        + ([] if tgt is None else [tile]),
        out_specs=[tile, tile] + ([] if tgt is None else [_full((1, 1))]),
        out_shape=[_sds((t, D)), _sds((t, D))] + ([] if tgt is None else [_sds((1, 1))]),
        args=(hb, wb, xres, g) + (() if tgt is None else (tgt,)), comm=comm)
    return outs if comm is None else (outs, got)


def _post_bwd_nt(dxn, f, g, wb, scale, gu, name, comm=None):
    e_n, k, _ = wb.shape
    t = f.shape[0]
    tm = _tile(t, TM_MXU)
    sub = _tile(tm, TM_SUB)
    swiglu = gu is not None

    def body(*refs):
        if swiglu:
            dx_ref, f_ref, g_ref, w_ref, gu_ref, df_ref, dg_ref, dh_ref, df_s = refs
        else:
            dx_ref, f_ref, g_ref, w_ref, df_ref, dg_ref, dh_ref, df_s = refs
        i, e = pl.program_id(0), pl.program_id(1)

        @pl.when(e == 0)
        def _():
            df, dg = _rms_bwd(f_ref[...], g_ref[...], scale * dx_ref[...])
            df_s[...] = df.astype(bf16)
            df_ref[...] = df_s[...]

            @pl.when(i == 0)
            def _():
                dg_ref[...] = dg

            @pl.when(i != 0)
            def _():
                dg_ref[...] += dg

        for s in range(tm // sub):
            rows = pl.ds(s * sub, sub)
            dh = _dot_nt(df_s[rows, :], w_ref[0])
            if swiglu:
                gate = gu_ref[0, 0, rows, :].astype(f32)
                up = gu_ref[0, 1, rows, :].astype(f32)
                sg = jax.nn.sigmoid(gate)
                dh_ref[0, 0, rows, :] = (dh * up * (sg * (1.0 + gate * (1.0 - sg)))).astype(bf16)
                dh_ref[0, 1, rows, :] = (dh * gate * sg).astype(bf16)
            else:
                dh_ref[0, rows, :] = dh

    in_specs = [pl.BlockSpec((tm, D), lambda i, e: (i, 0)), pl.BlockSpec((tm, D), lambda i, e: (i, 0)), _full((1, D)),
                pl.BlockSpec((1, k, D), lambda i, e: (e, 0, 0))]
    args = [dxn, f, g, wb]
    if swiglu:
        in_specs.append(pl.BlockSpec((1, 2, tm, k), lambda i, e: (e, 0, i, 0)))
        args.append(gu)
        dh_spec = pl.BlockSpec((1, 2, tm, k), lambda i, e: (e, 0, i, 0))
        dh_shape = _sds((e_n, 2, t, k), bf16)
    else:
        dh_spec = pl.BlockSpec((1, tm, k), lambda i, e: (e, i, 0))
        dh_shape = _sds((e_n, t, k), f32)
    return _pcall(
        body, name=name, grid=(t // tm, e_n), in_specs=in_specs,
        out_specs=[pl.BlockSpec((tm, D), lambda i, e: (i, 0)), _full((1, D)), dh_spec],
        out_shape=[_sds((t, D), bf16), _sds((1, D)), dh_shape],
        scratch_shapes=[pltpu.VMEM((tm, D), bf16)], args=args, comm=comm)


def _nt_pre_bwd(dy, wb, x, g, dres, name, comm=None):
    e_n, q_n, t, k = dy.shape
    tm = _tile(t, TM_MXU)

    def body(dy_ref, w_ref, x_ref, g_ref, r_ref, dx_ref, dg_ref, acc):
        i, e = pl.program_id(0), pl.program_id(1)
        p = _dot(dy_ref[0, 0], w_ref[0, 0])
        for q in range(1, q_n):
            p = p + _dot(dy_ref[0, q], w_ref[0, q])

        @pl.when(e == 0)
        def _():
            acc[...] = p

        @pl.when(e != 0)
        def _():
            acc[...] += p

        @pl.when(e == e_n - 1)
        def _():
            dx, dg = _rms_bwd(x_ref[...], g_ref[...], acc[...])
            dx_ref[...] = r_ref[...] + dx

            @pl.when(i == 0)
            def _():
                dg_ref[...] = dg

            @pl.when(i != 0)
            def _():
                dg_ref[...] += dg

    return _pcall(
        body, name=name, grid=(t // tm, e_n),
        in_specs=[pl.BlockSpec((1, q_n, tm, k), lambda i, e: (e, 0, i, 0)),
                  pl.BlockSpec((1, q_n, k, D), lambda i, e: (e, 0, 0, 0)),
                  pl.BlockSpec((tm, D), lambda i, e: (i, 0)), _full((1, D)),
                  pl.BlockSpec((tm, D), lambda i, e: (i, 0))],
        out_specs=[pl.BlockSpec((tm, D), lambda i, e: (i, 0)), _full((1, D))],
        out_shape=[_sds((t, D)), _sds((1, D))],
        scratch_shapes=[pltpu.VMEM((tm, D), f32)], args=(dy, wb, x, g, dres), comm=comm)


def _wgrad(a, b, a_batched, b_batched, tn, name, comm=None):
    t, k = a.shape[1], a.shape[2]
    nn = b.shape[2]
    nb = max(a.shape[0], b.shape[0])
    tt = _tile(t, TM_MXU)
    nt = t // tt

    def body(a_ref, b_ref, o_ref, acc):
        @pl.when(pl.program_id(2) == 0)
        def _():
            acc[...] = _dot_tn(a_ref[0], b_ref[0])

        @pl.when(pl.program_id(2) != 0)
        def _():
            acc[...] += _dot_tn(a_ref[0], b_ref[0])

        @pl.when(pl.program_id(2) == nt - 1)
        def _():
            o_ref[0] = acc[...].astype(bf16)

    (out,), got = _pcall(
        body, name=name, grid=(nb, nn // tn, nt),
        in_specs=[pl.BlockSpec((1, tt, k), (lambda n, j, s: (n, s, 0)) if a_batched else (lambda n, j, s: (0, s, 0))),
                  pl.BlockSpec((1, tt, tn), (lambda n, j, s: (n, s, j)) if b_batched else (lambda n, j, s: (0, s, j)))],
        out_specs=[pl.BlockSpec((1, k, tn), lambda n, j, s: (n, 0, j))],
        out_shape=[_sds((nb, k, nn), bf16)],
        scratch_shapes=[pltpu.VMEM((k, tn), f32)], args=(a, b), comm=comm)
    return out if comm is None else (out, got)


def _norm_mm_fwd(x, g, w, name):
    t = x.shape[0]
    tm = _tile(t, TM_VPU)

    def body(x_ref, g_ref, w_ref, xn_ref, ps_ref, pc_ref):
        xn = _rms_fwd(x_ref[...], g_ref[...]).astype(bf16)
        xn_ref[...] = xn
        ps_ref[...] = _dot_nt(xn, w_ref[:DS, :])
        pc_ref[...] = _dot_nt(xn, w_ref[DS:, :])

    return pl.pallas_call(
        body, name=name, grid=(t // tm,),
        in_specs=[pl.BlockSpec((tm, D), lambda i: (i, 0)), _full((1, D)), _full((DIN, D))],
        out_specs=[pl.BlockSpec((tm, D), lambda i: (i, 0)), pl.BlockSpec((tm, DS), lambda i: (i, 0)),
                   pl.BlockSpec((tm, 2 * DC), lambda i: (i, 0))],
        out_shape=[_sds((t, D), bf16), _sds((t, DS)), _sds((t, 2 * DC))],
    )(x, g, w)


def _segsum_b(x, seg, seg_t):
    return mm(mm(x, seg, "nn", EB), seg_t, "nn", EB)


def _prep_fn(psl, w0, a0, k_k, k_a, wup, aup, gup, seg, seg_t):
    r, k, v = psl[:, :DR], psl[:, DR:2 * DR], psl[:, 2 * DR:3 * DR]
    xwa, xg = psl[:, 3 * DR:3 * DR + LANES], psl[:, 3 * DR + LANES:]
    d = w0 + mm(jnp.tanh(xwa), wup, "nn", LORA)
    lw = -DECAY_SCALE * jax.nn.sigmoid(d)
    a = jax.nn.sigmoid(a0 + mm(xwa, aup, "nn", LORA))
    g = mm(jax.nn.sigmoid(xg), gup, "nn", LORA)
    kkr = k * k_k
    kk = kkr * lax.rsqrt(jnp.maximum(_segsum_b(kkr * kkr, seg, seg_t), 1e-12))
    kp = k * (1.0 + (a - 1.0) * k_a)
    return r, lw, kp, v, -kk, kk * a, g


def _shifted(ps, halo_row, first):
    prev = jnp.where(first, 0.0, halo_row)
    sh = pltpu.roll(ps, 1, 0)
    row = lax.broadcasted_iota(jnp.int32, ps.shape, 0)
    return jnp.where(row == 0, prev, sh)


def _heads_of(ref, rows):
    return jnp.stack([ref[rows, h * HN:(h + 1) * HN] for h in range(NH)], axis=0)


def _heads_to(ref, rows, val):
    ref[rows, :] = jnp.concatenate([val[h] for h in range(NH)], axis=-1)


_PREP_PARAM_SHAPES = [(1, DS), (1, DR), (1, DR), (1, DR), (1, DR), (LANES, DR), (LANES, DR), (LANES, DR),
                      (DR, LANES), (LANES, DR)]


def _prep_fwd(ps, params, name, comm=None):
    t = ps.shape[0]
    tm = _tile(t, TM_VPU)
    hb = tm // 8

    def body(ps_ref, halo_ref, mu_ref, *rest):
        prm = [r[...] for r in rest[:9]]
        outs = rest[9:]
        x = ps_ref[...]
        sh = _shifted(x, halo_ref[pl.ds(7, 1), :], pl.program_id(0) == 0)
        psl = x + (sh - x) * mu_ref[...]
        vals = _prep_fn(psl, *prm)
        for ref, val in zip(outs, vals):
            ref[...] = val

    outs, got = _pcall(
        body, name=name, grid=(t // tm,),
        in_specs=[pl.BlockSpec((tm, DS), lambda i: (i, 0)),
                  pl.BlockSpec((8, DS), lambda i: (jnp.maximum(i * hb - 1, 0), 0))]
        + [_full(s) for s in _PREP_PARAM_SHAPES],
        out_specs=[pl.BlockSpec((tm, DR), lambda i: (i, 0))] * 7,
        out_shape=[_sds((t, DR))] * 7, args=(ps, ps, *params), comm=comm)
    return outs if comm is None else (outs, got)


def _prep_bwd(ps, params, cts_hm, cts_tm, name):
    t = ps.shape[0]
    tm = _tile(t, TM_VPU)
    hb = tm // 8

    def body(ps_ref, halo_ref, mu_ref, *rest):
        prm = [r[...] for r in rest[:9]]
        chm = rest[9:15]
        ctm = rest[15:19]
        dpsl_ref = rest[19]
        gouts = rest[20:27]
        x = ps_ref[...]
        sh = _shifted(x, halo_ref[pl.ds(7, 1), :], pl.program_id(0) == 0)
        psl = x + (sh - x) * mu_ref[...]
        seg, seg_t = prm[7], prm[8]
        _, vjp = jax.vjp(lambda p, *w: _prep_fn(p, *w, seg, seg_t), psl, *prm[:7])
        c = [r[...] for r in chm]
        cts = (c[0] + ctm[0][...], c[1], c[2] + ctm[1][...], c[3] + ctm[2][...], c[4], c[5], ctm[3][...])
        grads = vjp(cts)
        dpsl_ref[...] = grads[0]
        for ref, gval in zip(gouts, grads[1:]):
            @pl.when(pl.program_id(0) == 0)
            def _(ref=ref, gval=gval):
                ref[...] = gval

            @pl.when(pl.program_id(0) != 0)
            def _(ref=ref, gval=gval):
                ref[...] += gval

    tk = pl.BlockSpec((tm, DR), lambda i: (i, 0))
    gshapes = _PREP_PARAM_SHAPES[1:8]
    return pl.pallas_call(
        body, name=name, grid=(t // tm,),
        in_specs=[pl.BlockSpec((tm, DS), lambda i: (i, 0)),
                  pl.BlockSpec((8, DS), lambda i: (jnp.maximum(i * hb - 1, 0), 0))]
        + [_full(s) for s in _PREP_PARAM_SHAPES] + [tk] * 10,
        out_specs=[pl.BlockSpec((tm, DS), lambda i: (i, 0))] + [_full(s) for s in gshapes],
        out_shape=[_sds((t, DS))] + [_sds(s) for s in gshapes],
    )(ps, ps, *params, *cts_hm, *cts_tm)


def _shift_bwd(dpsl, ps, mu, dpc, name):
    t = ps.shape[0]
    tm = _tile(t, TM_VPU)
    hb = tm // 8
    last_blk = t // 8 - 1

    def body(d_ref, dn_ref, ps_ref, halo_ref, mu_ref, dpc_ref, dp_ref, dmu_ref):
        i = pl.program_id(0)
        mu_v = mu_ref[...]
        d = d_ref[...]
        nxt = jnp.where(i == pl.num_programs(0) - 1, 0.0, dn_ref[pl.ds(0, 1), :])
        up = pltpu.roll(d, tm - 1, 0)
        row = lax.broadcasted_iota(jnp.int32, d.shape, 0)
        up = jnp.where(row == tm - 1, nxt, up)
        dp_ref[:, :DS] = (d * (1.0 - mu_v) + up * mu_v).astype(bf16)
        dp_ref[:, DS:] = dpc_ref[...].astype(bf16)
        x = ps_ref[...]
        sh = _shifted(x, halo_ref[pl.ds(7, 1), :], i == 0)
        dmu = jnp.sum(d * (sh - x), axis=0, keepdims=True)

        @pl.when(i == 0)
        def _():
            dmu_ref[...] = dmu

        @pl.when(i != 0)
        def _():
            dmu_ref[...] += dmu

    return pl.pallas_call(
        body, name=name, grid=(t // tm,),
        in_specs=[pl.BlockSpec((tm, DS), lambda i: (i, 0)),
                  pl.BlockSpec((8, DS), lambda i: (jnp.minimum((i + 1) * hb, last_blk), 0)),
                  pl.BlockSpec((tm, DS), lambda i: (i, 0)),
                  pl.BlockSpec((8, DS), lambda i: (jnp.maximum(i * hb - 1, 0), 0)),
                  _full((1, DS)), pl.BlockSpec((tm, 2 * DC), lambda i: (i, 0))],
        out_specs=[pl.BlockSpec((tm, DIN), lambda i: (i, 0)), _full((1, DS))],
        out_shape=[_sds((t, DIN), bf16), _sds((1, DS))],
    )(dpsl, dpsl, ps, ps, mu, dpc)


def _post_fn(y, r, kp, v, g, gn_w, gn_b, r_k, seg, seg_t):
    mu = _segsum_b(y, seg, seg_t) * (1.0 / HN)
    yc = y - mu
    var = _segsum_b(yc * yc, seg, seg_t) * (1.0 / HN)
    yo = yc * lax.rsqrt(var + GN_EPS) * gn_w + gn_b
    bonus = _segsum_b(r * kp * r_k, seg, seg_t) * v
    return (yo + bonus) * g


_POST_PARAM_SHAPES = [(1, DR), (1, DR), (1, DR), (DR, LANES), (LANES, DR)]


def _post_fwd(y, r, kp, v, g, out_b, params, name):
    t = g.shape[0]
    tm = _tile(t, TM_VPU)

    def body(y_ref, r_ref, k_ref, v_ref, g_ref, ob_ref, *rest):
        prm = [p[...] for p in rest[:5]]
        cat_ref = rest[5]
        oa = _post_fn(y_ref[...], r_ref[...], k_ref[...], v_ref[...], g_ref[...], *prm)
        cat_ref[:, :DR] = oa.astype(bf16)
        cat_ref[:, DR:] = ob_ref[...]

    tk = pl.BlockSpec((tm, DR), lambda i: (i, 0))
    return pl.pallas_call(
        body, name=name, grid=(t // tm,),
        in_specs=[tk] * 6 + [_full(s) for s in _POST_PARAM_SHAPES],
        out_specs=pl.BlockSpec((tm, 2 * DR), lambda i: (i, 0)),
        out_shape=_sds((t, 2 * DR), bf16),
    )(y, r, kp, v, g, out_b, *params)


def _post_bwd(dcat, y, r, kp, v, g, params, name, comm=None):
    t = g.shape[0]
    tm = _tile(t, TM_VPU)

    def body(dc_ref, y_ref, r_ref, k_ref, v_ref, g_ref, *rest):
        prm = [p[...] for p in rest[:5]]
        dy_ref, dr_ref, dk_ref, dv_ref, dg_ref = rest[5:10]
        gouts = rest[10:13]
        seg, seg_t = prm[3], prm[4]
        _, vjp = jax.vjp(lambda *a: _post_fn(*a, seg, seg_t),
                         y_ref[...], r_ref[...], k_ref[...], v_ref[...], g_ref[...], *prm[:3])
        grads = vjp(dc_ref[...])
        dy_ref[...] = grads[0]
        dr_ref[...] = grads[1]
        dk_ref[...] = grads[2]
        dv_ref[...] = grads[3]
        dg_ref[...] = grads[4]
        for ref, gval in zip(gouts, grads[5:]):
            @pl.when(pl.program_id(0) == 0)
            def _(ref=ref, gval=gval):
                ref[...] = gval

            @pl.when(pl.program_id(0) != 0)
            def _(ref=ref, gval=gval):
                ref[...] += gval

    tk = pl.BlockSpec((tm, DR), lambda i: (i, 0))
    return _pcall(
        body, name=name, grid=(t // tm,),
        in_specs=[tk] * 6 + [_full(s) for s in _POST_PARAM_SHAPES],
        out_specs=[tk] * 5 + [_full((1, DR))] * 3,
        out_shape=[_sds((t, DR))] * 5 + [_sds((1, DR))] * 3,
        args=(dcat, y, r, kp, v, g, *params), comm=comm)


@jax.custom_vjp
def _inv_unit_lower(lb):
    c = lb.shape[-1]
    row = lax.broadcasted_iota(jnp.int32, (c, c), 0)
    col = lax.broadcasted_iota(jnp.int32, (c, c), 1)
    p = (row == col).astype(f32)[None] + lb
    m = lb
    for _ in range(c.bit_length() - 2):
        m = mm(m, m, "nn", RIM)
        p = p + mm(m, p, "nn", RIM)
    return p


def _inv_unit_lower_fwd(lb):
    p = _inv_unit_lower(lb)
    return p, p


def _inv_unit_lower_bwd(p, ct):
    return (mm(mm(p, ct, "tn", RIM), p, "nt", RIM),)


_inv_unit_lower.defvjp(_inv_unit_lower_fwd, _inv_unit_lower_bwd)


def _r1_fn(r, lw, k, v, z, b):
    c = r.shape[1]
    row = lax.broadcasted_iota(jnp.int32, (c, c), 0)
    col = lax.broadcasted_iota(jnp.int32, (c, c), 1)
    incl = (row >= col)[None]
    strict = (row > col)[None]
    eye = (row == col).astype(f32)[None]
    tri = jnp.broadcast_to((row >= col).astype(f32)[None], (r.shape[0], c, c))
    cum = mm(tri, lw, "nn", EA)
    tot = jnp.sum(lw, axis=1, keepdims=True)
    zt = z * jnp.exp(cum - lw)
    rt = r * jnp.exp(cum)
    g_inv = jnp.exp(-cum)
    g_rem = jnp.exp(tot - cum)
    bt, kt = b * g_inv, k * g_inv
    bh, kh = b * g_rem, k * g_rem
    lb = jnp.where(strict, mm(zt, bt, "nt", RGM), 0.0)
    lk = jnp.where(strict, mm(zt, kt, "nt", RGM), 0.0)
    mb = jnp.where(incl, mm(rt, bt, "nt", RGM), 0.0)
    mk = jnp.where(incl, mm(rt, kt, "nt", RGM), 0.0)
    p = _inv_unit_lower(lb)
    w1 = mm(p, zt, "nn", RWM)
    w2 = mm(p, mm(lk, v, "nn", RWM), "nn", RWM)
    a = mm(w1, bh, "tn", RAM) + eye * jnp.exp(tot)
    g = mm(w2, bh, "tn", RAM) + mm(v, kh, "tn", RAM)
    q = rt + mm(mb, w1, "nn", RWM)
    yl = mm(mb, w2, "nn", RWM) + mm(mk, v, "nn", RWM)
    return a, g, q, yl


def _chunks_in(ref, n):
    return jnp.concatenate([_heads_of(ref, pl.ds(s * CHUNK, CHUNK)) for s in range(n)], axis=0)


def _chunks_out(ref, val, n):
    for s in range(n):
        _heads_to(ref, pl.ds(s * CHUNK, CHUNK), val[s * NH:(s + 1) * NH])


def _r1_fwd(ins, name, comm=None):
    t = ins[0].shape[0]
    n = R1_CHUNKS
    nc = t // CHUNK

    def body(r, lw, k, v, z, b, a_ref, g_ref, q_ref, yl_ref):
        a, g, q, yl = _r1_fn(*[_chunks_in(x, n) for x in (r, lw, k, v, z, b)])
        for s in range(n):
            a_ref[s] = a[s * NH:(s + 1) * NH]
            g_ref[s] = g[s * NH:(s + 1) * NH]
        _chunks_out(q_ref, q, n)
        _chunks_out(yl_ref, yl, n)

    ck = pl.BlockSpec((n * CHUNK, DR), lambda c: (c, 0))
    st = pl.BlockSpec((n, NH, HN, HN), lambda c: (c, 0, 0, 0))
    return _pcall(
        body, name=name, grid=(nc // n,), in_specs=[ck] * 6, out_specs=[st, st, ck, ck],
        out_shape=[_sds((nc, NH, HN, HN))] * 2 + [_sds((t, DR))] * 2, args=tuple(ins), comm=comm)


def _r1_bwd(ins, da, dg, dq, dyl, name, comm=None):
    t = ins[0].shape[0]
    n = R1_CHUNKS
    nc = t // CHUNK

    def body(r, lw, k, v, z, b, da_ref, dg_ref, dq_ref, dyl_ref, *outs):
        _, vjp = jax.vjp(_r1_fn, *[_chunks_in(x, n) for x in (r, lw, k, v, z, b)])
        cts = (jnp.concatenate([da_ref[s] for s in range(n)], axis=0),
               jnp.concatenate([dg_ref[s] for s in range(n)], axis=0), _chunks_in(dq_ref, n), _chunks_in(dyl_ref, n))
        for ref, gval in zip(outs, vjp(cts)):
            _chunks_out(ref, gval, n)

    ck = pl.BlockSpec((n * CHUNK, DR), lambda c: (c, 0))
    st = pl.BlockSpec((n, NH, HN, HN), lambda c: (c, 0, 0, 0))
    return _pcall(
        body, name=name, grid=(nc // n,), in_specs=[ck] * 6 + [st, st, ck, ck], out_specs=[ck] * 6,
        out_shape=[_sds((t, DR))] * 6, args=(*ins, da, dg, dq, dyl), comm=comm)


def _r2_fwd(a, g, q, yl, name):
    nc = a.shape[0]
    t = q.shape[0]

    def body(a_ref, g_ref, q_ref, yl_ref, y_ref, s_ref, s):
        @pl.when(pl.program_id(0) == 0)
        def _():
            s[...] = jnp.zeros_like(s)
        s0 = s[...]
        for j in range(n):
            rows = pl.ds(j * CHUNK, CHUNK)
            s_ref[j] = s0
            _heads_to(y_ref, rows, mm(_heads_of(q_ref, rows), s0, "nt", R2M) + _heads_of(yl_ref, rows))
            s0 = mm(s0, a_ref[j], "nn", R2M) + g_ref[j]
        s[...] = s0

    n = math.gcd(nc, R2_CHUNKS)
    ck = pl.BlockSpec((n * CHUNK, DR), lambda c: (c, 0))
    st = pl.BlockSpec((n, NH, HN, HN), lambda c: (c, 0, 0, 0))
    return pl.pallas_call(
        body, name=name, grid=(nc // n,), in_specs=[st, st, ck, ck], out_specs=[ck, st],
        out_shape=[_sds((t, DR)), _sds((nc, NH, HN, HN))],
        scratch_shapes=[pltpu.VMEM((NH, HN, HN), f32)],
    )(a, g, q, yl)


def _r2_bwd(dy, q, s_all, a, name):
    nc = a.shape[0]
    t = q.shape[0]

    def body(dy_ref, q_ref, s_ref, a_ref, dq_ref, da_ref, dg_ref, ds):
        @pl.when(pl.program_id(0) == 0)
        def _():
            ds[...] = jnp.zeros_like(ds)
        dsn = ds[...]
        for j in reversed(range(n)):
            rows = pl.ds(j * CHUNK, CHUNK)
            s0 = s_ref[j]
            dyv = _heads_of(dy_ref, rows)
            dg_ref[j] = dsn
            da_ref[j] = mm(s0, dsn, "tn", R2M)
            _heads_to(dq_ref, rows, mm(dyv, s0, "nn", R2M))
            dsn = mm(dsn, a_ref[j], "nt", R2M) + mm(dyv, _heads_of(q_ref, rows), "tn", R2M)
        ds[...] = dsn

    n = math.gcd(nc, R2_CHUNKS)
    nb = nc // n
    ck = pl.BlockSpec((n * CHUNK, DR), lambda c: (nb - 1 - c, 0))
    st = pl.BlockSpec((n, NH, HN, HN), lambda c: (nb - 1 - c, 0, 0, 0))
    return pl.pallas_call(
        body, name=name, grid=(nb,), in_specs=[ck, ck, st, st], out_specs=[ck, st, st],
        out_shape=[_sds((t, DR)), _sds((nc, NH, HN, HN)), _sds((nc, NH, HN, HN))],
        scratch_shapes=[pltpu.VMEM((NH, HN, HN), f32)],
    )(dy, q, s_all, a)


def _ln_silu(c, w, b):
    mu = jnp.mean(c, axis=-1, keepdims=True)
    cc = c - mu
    var = jnp.mean(cc * cc, axis=-1, keepdims=True)
    u = cc * lax.rsqrt(var + LN_EPS) * w + b
    return u * jax.nn.sigmoid(u)


def _glu_tile(pc):
    return pc[:, :DC] * jax.nn.sigmoid(pc[:, DC:])


def _fill_glu(s_ref, pc_ref, halo_ref, first, tm):
    s_ref[pl.ds(0, HALO), :] = jnp.where(first, 0.0, _glu_tile(halo_ref[...]))
    s_ref[pl.ds(HALO, tm), :] = _glu_tile(pc_ref[...])


def _phase_copies(sh):
    n = sh.shape[1] - 8
    for r in range(1, 8):
        sh[r, pl.ds(0, n), :] = sh[0, pl.ds(r, n), :]


def _rows_at(sh, o, n):
    return sh[o % 8, pl.ds(o - o % 8, n), :]


def _conv_fwd(pc, dw, cb, lnw, lnb, name):
    t = pc.shape[0]
    tm = _tile(t, TM_VPU)
    hb = tm // HALO

    def body(pc_ref, halo_ref, dw_ref, cb_ref, w_ref, b_ref, c_ref, o_ref, s):
        _fill_glu(s.at[0], pc_ref, halo_ref, pl.program_id(0) == 0, tm)
        _phase_copies(s)
        for r0 in range(0, tm, CONV_ROWS):
            rows = pl.ds(r0, CONV_ROWS)
            acc = _rows_at(s, r0 + HALO - CW + 1, CONV_ROWS) * dw_ref[pl.ds(0, 1), :]
            for j in range(1, CW):
                acc = acc + _rows_at(s, r0 + HALO - CW + 1 + j, CONV_ROWS) * dw_ref[pl.ds(j, 1), :]
            c = acc + cb_ref[...]
            c_ref[rows, :] = c
            o_ref[rows, :] = _ln_silu(c, w_ref[...], b_ref[...]).astype(bf16)

    return pl.pallas_call(
        body, name=name, grid=(t // tm,),
        in_specs=[pl.BlockSpec((tm, 2 * DC), lambda i: (i, 0)),
                  pl.BlockSpec((HALO, 2 * DC), lambda i: (jnp.maximum(i * hb - 1, 0), 0)),
                  _full((CW, DC)), _full((1, DC)), _full((1, DC)), _full((1, DC))],
        out_specs=[pl.BlockSpec((tm, DC), lambda i: (i, 0)), pl.BlockSpec((tm, DC), lambda i: (i, 0))],
        out_shape=[_sds((t, DC)), _sds((t, DC), bf16)],
        scratch_shapes=[pltpu.VMEM((8, HALO + tm, DC), f32)],
    )(pc, pc, dw, cb, lnw, lnb)


def _conv_bwd1(dcat, c, lnw, lnb, name):
    t = c.shape[0]
    tm = _tile(t, TM_VPU)

    def body(d_ref, c_ref, w_ref, b_ref, dc_ref, dw_ref, db_ref, dcb_ref):
        _, vjp = jax.vjp(_ln_silu, c_ref[...], w_ref[...], b_ref[...])
        dc, dw, db = vjp(d_ref[...])
        dc_ref[...] = dc
        dcb = jnp.sum(dc, axis=0, keepdims=True)
        for ref, gval in ((dw_ref, dw), (db_ref, db), (dcb_ref, dcb)):
            @pl.when(pl.program_id(0) == 0)
            def _(ref=ref, gval=gval):
                ref[...] = gval

            @pl.when(pl.program_id(0) != 0)
            def _(ref=ref, gval=gval):
                ref[...] += gval

    tk = pl.BlockSpec((tm, DC), lambda i: (i, 0))
    return pl.pallas_call(
        body, name=name, grid=(t // tm,),
        in_specs=[pl.BlockSpec((tm, DC), lambda i: (i, 1)), tk, _full((1, DC)), _full((1, DC))],
        out_specs=[tk] + [_full((1, DC))] * 3,
        out_shape=[_sds((t, DC))] + [_sds((1, DC))] * 3,
    )(dcat, c, lnw, lnb)


def _conv_bwd2(dc, pc, dw, name):
    t = pc.shape[0]
    tm = _tile(t, TM_VPU)
    hb = tm // HALO
    last_blk = t // HALO - 1

    def body(dc_ref, dn_ref, pc_ref, halo_ref, dw_ref, dpc_ref, ddw_ref, s, sd, acc):
        i = pl.program_id(0)
        _fill_glu(s.at[0], pc_ref, halo_ref, i == 0, tm)
        sd[0, pl.ds(0, tm), :] = dc_ref[...]
        sd[0, pl.ds(tm, HALO), :] = jnp.where(i == pl.num_programs(0) - 1, 0.0, dn_ref[...])
        _phase_copies(s)
        _phase_copies(sd)
        for r0 in range(0, tm, CONV_ROWS):
            rows = pl.ds(r0, CONV_ROWS)
            dcb = sd[0, rows, :]
            dglu = None
            for j in range(CW):
                term = _rows_at(sd, r0 + CW - 1 - j, CONV_ROWS) * dw_ref[pl.ds(j, 1), :]
                dglu = term if dglu is None else dglu + term
                part = dcb * _rows_at(s, r0 + HALO - CW + 1 + j, CONV_ROWS)
                part8 = part[0:8]
                for q in range(8, CONV_ROWS, 8):
                    part8 = part8 + part[q:q + 8]
                if r0 == 0:
                    acc[pl.ds(8 * j, 8), :] = part8
                else:
                    acc[pl.ds(8 * j, 8), :] += part8
            a = pc_ref[rows, :DC]
            sg = jax.nn.sigmoid(pc_ref[rows, DC:])
            dpc_ref[rows, :DC] = dglu * sg
            dpc_ref[rows, DC:] = dglu * a * sg * (1.0 - sg)
        rows = [jnp.sum(acc[pl.ds(8 * j, 8), :], axis=0, keepdims=True) for j in range(CW)]

        @pl.when(i == 0)
        def _():
            for j in range(CW):
                ddw_ref[pl.ds(j, 1), :] = rows[j]

        @pl.when(i != 0)
        def _():
            for j in range(CW):
                ddw_ref[pl.ds(j, 1), :] += rows[j]

    return pl.pallas_call(
        body, name=name, grid=(t // tm,),
        in_specs=[pl.BlockSpec((tm, DC), lambda i: (i, 0)),
                  pl.BlockSpec((HALO, DC), lambda i: (jnp.minimum((i + 1) * hb, last_blk), 0)),
                  pl.BlockSpec((tm, 2 * DC), lambda i: (i, 0)),
                  pl.BlockSpec((HALO, 2 * DC), lambda i: (jnp.maximum(i * hb - 1, 0), 0)),
                  _full((CW, DC))],
        out_specs=[pl.BlockSpec((tm, 2 * DC), lambda i: (i, 0)), _full((CW, DC))],
        out_shape=[_sds((t, 2 * DC)), _sds((CW, DC))],
        scratch_shapes=[pltpu.VMEM((8, HALO + tm, DC), f32), pltpu.VMEM((8, tm + HALO, DC), f32),
                        pltpu.VMEM((8 * CW, DC), f32)],
    )(dc, dc, pc, pc, dw)


def _adamw(w, g, m, v):
    m = ADAM_B1 * m + (1.0 - ADAM_B1) * g
    v = ADAM_B2 * v + (1.0 - ADAM_B2) * (g * g)
    m_hat = m / (1.0 - ADAM_B1 ** ADAM_STEP)
    v_hat = v / (1.0 - ADAM_B2 ** ADAM_STEP)
    delta = -ADAM_LR * (m_hat / (jnp.sqrt(v_hat) + ADAM_EPS) + ADAM_WD * w)
    return delta, m, v


def _adam_slots(w, slots, m, v, name):
    r, c = w.shape
    tr = next(cand for cand in (512, 352, 256, 128, r) if r % cand == 0)

    def body(w_ref, s_ref, m_ref, v_ref, g_ref, d_ref, nm_ref, nv_ref):
        g = s_ref[0].astype(f32)
        for k in range(1, NDEV):
            g = g + s_ref[k].astype(f32)
        delta, nm, nv = _adamw(w_ref[...], g, m_ref[...], v_ref[...])
        g_ref[...] = g
        d_ref[...] = delta
        nm_ref[...] = nm
        nv_ref[...] = nv

    blk = pl.BlockSpec((tr, c), lambda i: (i, 0))
    return pl.pallas_call(
        body, name=name, grid=(r // tr,),
        in_specs=[blk, pl.BlockSpec((NDEV, tr, c), lambda i: (0, i, 0)), blk, blk],
        out_specs=[blk] * 4, out_shape=[_sds((r, c))] * 4,
    )(w, slots, m, v)


def _sum_slots(slots, name):
    _, r, c = slots.shape

    def body(s_ref, o_ref):
        g = s_ref[0]
        for k in range(1, NDEV):
            g = g + s_ref[k]
        o_ref[...] = g

    return pl.pallas_call(body, name=name, in_specs=[_full((NDEV, r, c))], out_specs=_full((r, c)),
                          out_shape=_sds((r, c)), grid=(1,))(slots)


def _adam_many(ws, gs, ms, vs, name):
    n = len(ws)
    shapes = [w.shape for w in ws]

    def body(*refs):
        for i in range(n):
            w_ref, g_ref, m_ref, v_ref = refs[i], refs[n + i], refs[2 * n + i], refs[3 * n + i]
            delta, nm, nv = _adamw(w_ref[...], g_ref[...], m_ref[...], v_ref[...])
            refs[4 * n + i][...] = delta
            refs[5 * n + i][...] = nm
            refs[6 * n + i][...] = nv

    outs = pl.pallas_call(body, name=name, grid=(1,), in_specs=[_full(s) for s in shapes] * 4,
                          out_specs=[_full(s) for s in shapes] * 3, out_shape=[_sds(s) for s in shapes] * 3,
                          )(*ws, *gs, *ms, *vs)
    return outs[:n], outs[n:2 * n], outs[2 * n:]


def _pack(pieces, total):
    flat = []
    n = 0
    for p in pieces:
        p = p.reshape(-1)
        pad = (-p.shape[0]) % LANES
        if pad:
            p = jnp.pad(p, (0, pad))
        flat.append(p)
        n += p.shape[0]
    if total > n:
        flat.append(jnp.zeros((total - n,), f32))
    return jnp.concatenate(flat).reshape(total // LANES, LANES)


def _unpack(vec, shapes):
    flat = vec.reshape(-1)
    out, off = [], 0
    for s in shapes:
        n = math.prod(s)
        out.append(flat[off:off + n].reshape(s))
        off += n + (-n) % LANES
    return out


def _round_up(n, m):
    return (n + m - 1) // m * m


def kernel(x, ffn1_norm_pre, ffn1_norm_post, ffn1_w_gu, ffn1_w_down, mix_norm_pre, mix_norm_post, w_in, shift_mu, w_up, w0, a_up, a0, g_up, k_k, k_a, r_k, gn_w, gn_b, conv_dw, conv_b, conv_ln_w, conv_ln_b, w_out, ffn2_norm_pre, ffn2_norm_post, ffn2_w_gu, ffn2_w_down, loss_target, m_ffn1_norm_pre, m_ffn1_norm_post, m_ffn1_w_gu, m_ffn1_w_down, m_mix_norm_pre, m_mix_norm_post, m_w_in, m_shift_mu, m_w_up, m_w0, m_a_up, m_a0, m_g_up, m_k_k, m_k_a, m_r_k, m_gn_w, m_gn_b, m_conv_dw, m_conv_b, m_conv_ln_w, m_conv_ln_b, m_w_out, m_ffn2_norm_pre, m_ffn2_norm_post, m_ffn2_w_gu, m_ffn2_w_down, v_ffn1_norm_pre, v_ffn1_norm_post, v_ffn1_w_gu, v_ffn1_w_down, v_mix_norm_pre, v_mix_norm_post, v_w_in, v_shift_mu, v_w_up, v_w0, v_a_up, v_a0, v_g_up, v_k_k, v_k_a, v_r_k, v_gn_w, v_gn_b, v_conv_dw, v_conv_b, v_conv_ln_w, v_conv_ln_b, v_w_out, v_ffn2_norm_pre, v_ffn2_norm_post, v_ffn2_w_gu, v_ffn2_w_down):
    t = x.shape[1]
    x0 = x.reshape(t, D)
    tgt = loss_target.reshape(t, D)

    def shard(a):
        return a[0].astype(bf16)

    def shard_t(a):
        return jnp.swapaxes(a[0], 0, 1).astype(bf16)

    (wg1,) = _exchange([shard_t(ffn1_w_gu)], ["gather"], [SLOT42], "gather_wg1")

    def cols(a):
        return jnp.transpose(a, (1, 0, 2)).reshape(a.shape[1], NDEV * a.shape[2])

    lane = jnp.arange(DR, dtype=jnp.int32) // HN
    seg = (lane[:, None] == jnp.arange(LANES, dtype=jnp.int32)[None, :]).astype(f32)
    seg_t = seg.T
    rk_row = r_k.reshape(1, DR)
    post_params = (gn_w, gn_b, rk_row, seg, seg_t)

    (xn1, gu1, h1), (wd1, win_g) = _ffn_in_fwd(x0, ffn1_norm_pre, wg1, "ffn1_in_fwd", comm=(
        [shard(ffn1_w_down), shard_t(w_in)], ["gather"] * 2, [SLOT8] * 2))
    wd1 = wd1.reshape(4, FB, D)
    win_t = win_g.reshape(DIN, D)
    (f1, x1), (wout_g, wup_g, aup_g, gup_g, cdw_g) = _mm_norm_res_fwd(
        h1, wd1, x0, ffn1_norm_post, 0.5, "ffn1_out_fwd", comm=(
            [shard(w_out), w_up[0], a_up[0], g_up[0], conv_dw[0]], ["gather"] * 5, [SLOT8] * 5))
    wout_full = wout_g.reshape(1, D, D)
    wup_full, aup_full, gup_full, cdw_full = cols(wup_g), cols(aup_g), cols(gup_g), cols(cdw_g)
    zeros64 = jnp.zeros((HN, DR), f32)
    wup_pad = jnp.concatenate([wup_full, zeros64], axis=0)
    aup_pad = jnp.concatenate([zeros64, aup_full], axis=0)
    prep_params = (shift_mu, w0, a0, k_k, k_a, wup_pad, aup_pad, gup_full, seg, seg_t)
    hm, ps, pc = _norm_mm_fwd(x1, mix_norm_pre, win_t, "mix_in_fwd")
    rec_in, (wd2,) = _prep_fwd(ps, prep_params, "prep_fwd", comm=([shard(ffn2_w_down)], ["gather"], [SLOT8]))
    wd2 = wd2.reshape(4, FB, D)
    g_gate = rec_in[6]
    rec_in = rec_in[:6]
    (a_c, g_c, q_c, yl_c), (wg2,) = _r1_fwd(rec_in, "r1_fwd", comm=([shard_t(ffn2_w_gu)], ["gather"], [SLOT42]))
    y_rec, s_all = _r2_fwd(a_c, g_c, q_c, yl_c, "r2_fwd")
    c_conv, out_b = _conv_fwd(pc, cdw_full, conv_b, conv_ln_w, conv_ln_b, "conv_fwd")
    cat = _post_fwd(y_rec, rec_in[0], rec_in[2], rec_in[3], g_gate, out_b, post_params, "post_fwd")
    fm, x2 = _mm_norm_res_fwd(cat.reshape(1, t, D), wout_full, x1, mix_norm_post, 1.0, "mix_out_fwd")
    (xn2, gu2, h2), _ = _ffn_in_fwd(x2, ffn2_norm_pre, wg2, "ffn2_in_fwd")
    f2, dy, loss_part = _mm_norm_res_fwd(h2, wd2, x2, ffn2_norm_post, 0.5, "ffn2_out_fwd", tgt=tgt)

    (df2, dg_post2, dgu2), _ = _post_bwd_nt(dy, f2, ffn2_norm_post, wd2, 0.5, gu2, "ffn2_out_bwd")
    (dx2, dg_pre2), _ = _nt_pre_bwd(dgu2, wg2, x2, ffn2_norm_pre, dy, "ffn2_in_bwd")
    dwd2 = _wgrad(h2, df2.reshape(1, t, D), True, False, D, "ffn2_wd_grad")
    dwg2 = _wgrad(dgu2.reshape(8, t, FB), xn2.reshape(1, t, D), True, False, D, "ffn2_wg_grad")
    (dfm, dg_postm, dcat), _ = _post_bwd_nt(dx2, fm, mix_norm_post, wout_full, 1.0, None, "mix_out_bwd")
    dwout = _wgrad(cat.reshape(1, t, D), dfm.reshape(1, t, D), False, False, D, "wout_grad")
    dcat = dcat.reshape(t, D)
    (dy_rec, dr2, dkp2, dv2, dgate, dgn_w, dgn_b, drk), (s_wout,) = _post_bwd(
        dcat, y_rec, rec_in[0], rec_in[2], rec_in[3], g_gate, post_params, "post_bwd", comm=(
            [dwout.reshape(NDEV, D // NDEV, D)], ["scatter"], [SLOT8]))
    dq_c, da_c, dg_c = _r2_bwd(dy_rec, q_c, s_all, a_c, "r2_bwd")
    rec_grads, got = _r1_bwd(rec_in, da_c, dg_c, dq_c, dy_rec, "r1_bwd", comm=(
        [dwg2.reshape(4, 2, FB, D), dwd2.reshape(NDEV, F // NDEV, D)], ["scatter"] * 2, [SLOT42, SLOT8]))
    s_wg2, s_wd2 = got
    prep_out = _prep_bwd(ps, prep_params, rec_grads, (dr2, dkp2, dv2, dgate), "prep_bwd")
    dpsl, dw0, da0, dkk, dka, dwup_pad, daup_pad, dgup = prep_out
    dc_conv, dlnw, dlnb, dcb = _conv_bwd1(dcat, c_conv, conv_ln_w, conv_ln_b, "conv_bwd1")
    dpc, dcdw = _conv_bwd2(dc_conv, pc, cdw_full, "conv_bwd2")
    dp, dmu = _shift_bwd(dpsl, ps, shift_mu, dpc, "shift_bwd")
    (dx1, dg_prem), _ = _nt_pre_bwd(dp.reshape(1, 1, t, DIN), win_t.reshape(1, 1, DIN, D), x1, mix_norm_pre, dx2,
                                    "mix_in_bwd")
    dwin_s = _wgrad(dp.reshape(1, t, DIN), hm.reshape(1, t, D), False, False, D // 2, "win_grad").reshape(
        NDEV, DIN // NDEV, D)
    (df1, dg_post1, dgu1), (s_win,) = _post_bwd_nt(dx1, f1, ffn1_norm_post, wd1, 0.5, gu1, "ffn1_out_bwd", comm=(
        [dwin_s], ["scatter"], [SLOT8]))
    dwd1 = _wgrad(h1, df1.reshape(1, t, D), True, False, D, "ffn1_wd_grad")
    dwg1, (s_wd1,) = _wgrad(dgu1.reshape(8, t, FB), xn1.reshape(1, t, D), True, False, D, "ffn1_wg_grad", comm=(
        [dwd1.reshape(NDEV, F // NDEV, D)], ["scatter"], [SLOT8]))
    wg1_sends, wg1_recvs, dwg1_thru, wg1_land, started = _scatter_start(dwg1.reshape(4, 2, FB, D), SLOT42, "wg1_scatter_start")
    (dx0, dg_pre1), _ = _nt_pre_bwd(dgu1, wg1, x0, ffn1_norm_pre + started[:1, :1], dx1, "ffn1_in_bwd")

    res = {}

    def adam_big(nm, w, s, m, v, transposed):
        view = (lambda a: jnp.swapaxes(a[0], 0, 1)) if transposed else (lambda a: a[0])
        outs = _adam_slots(view(w), s, view(m), view(v), "adam_" + nm)
        done.append(outs[1])
        res[nm] = [(jnp.swapaxes(o, 0, 1) if transposed else o)[None] for o in outs]

    done = []

    adam_big("ffn2_w_gu", ffn2_w_gu, s_wg2.reshape(NDEV, FB, D), m_ffn2_w_gu, v_ffn2_w_gu, True)
    adam_big("ffn2_w_down", ffn2_w_down, s_wd2, m_ffn2_w_down, v_ffn2_w_down, False)
    adam_big("w_in", w_in, s_win, m_w_in, v_w_in, True)
    adam_big("w_out", w_out, s_wout, m_w_out, v_w_out, False)
    adam_big("ffn1_w_down", ffn1_w_down, s_wd1, m_ffn1_w_down, v_ffn1_w_down, False)
    _, s_wg1 = _scatter_wait(wg1_sends, wg1_recvs, dwg1_thru, wg1_land, [dg_pre1] + done, SLOT42, "wg1_scatter_wait")
    my_slot = SLOT42[1](4 * lax.axis_index("x") + 2 * lax.axis_index("y") + lax.axis_index("c"))
    own = lax.dynamic_slice(dwg1.reshape(4, 2, FB, D), (*my_slot, 0, 0), (1, 1, FB, D))
    s_wg1 = lax.dynamic_update_slice(s_wg1, own, (*my_slot, 0, 0))
    rep_shapes = [(1, 1)] + [(1, D)] * 6 + [(1, DS)] + [(1, DR)] * 10
    rep_parts = [loss_part, dg_pre1, dg_post1, dg_prem, dg_postm, dg_pre2, dg_post2, dmu,
                 dw0, da0, dkk, dka, drk, dgn_w, dgn_b, dcb, dlnw, dlnb]
    n_rep = _round_up(sum(_round_up(math.prod(s), LANES) for s in rep_shapes), 8 * LANES)
    sh_shapes = [(HN, HN), (HN, HN), (LANES, HN), (CW, HN)]
    n_sh = _round_up(sum(_round_up(math.prod(s), LANES) for s in sh_shapes), 8 * LANES)
    sh_parts = []
    for a in (dwup_pad[:HN], daup_pad[HN:], dgup, dcdw):
        rows = jnp.transpose(a.reshape(a.shape[0], NDEV, HN), (1, 0, 2)).reshape(NDEV, a.shape[0] * HN)
        sh_parts.append(jnp.pad(rows, ((0, 0), (0, (-rows.shape[1]) % LANES))))
    sh_vec = jnp.concatenate(sh_parts, axis=1)
    sh_vec = jnp.pad(sh_vec, ((0, 0), (0, n_sh - sh_vec.shape[1]))).reshape(NDEV, n_sh // LANES, LANES)
    rep_vec, sh_vec, s_wg1 = lax.optimization_barrier((_pack(rep_parts, n_rep), sh_vec, s_wg1))
    s_rep, s_sh = _exchange([rep_vec, sh_vec], ["gather", "scatter"], [SLOT8, SLOT8], "exchange_small")
    adam_big("ffn1_w_gu", ffn1_w_gu, s_wg1.reshape(NDEV, FB, D), m_ffn1_w_gu, v_ffn1_w_gu, True)


    gsum = _unpack(_sum_slots(s_rep, "sum_rep"), rep_shapes)
    gsh = _unpack(_sum_slots(s_sh, "sum_sharded"), sh_shapes)
    loss = gsum[0].reshape(())
    rep_names = ["ffn1_norm_pre", "ffn1_norm_post", "mix_norm_pre", "mix_norm_post", "ffn2_norm_pre", "ffn2_norm_post",
                 "shift_mu", "w0", "a0", "k_k", "k_a", "r_k", "gn_w", "gn_b", "conv_b", "conv_ln_w", "conv_ln_b"]
    shard_names = ["w_up", "a_up", "g_up", "conv_dw"]
    env = dict(locals())
    small_g = {n: gsum[1 + i] for i, n in enumerate(rep_names)}
    small_g["r_k"] = small_g["r_k"].reshape(1, NH, HN)
    for i, n in enumerate(shard_names):
        small_g[n] = gsh[i][None]
    names = rep_names + shard_names
    shapes = [env[n].shape for n in names]

    def flat2(a):
        return a.reshape(1, DR) if a.shape == (1, NH, HN) else a.reshape(a.shape[-2:])

    deltas, new_ms, new_vs = _adam_many(*[[flat2(d[n]) for n in names] for d in (
        {n: env[n] for n in names}, small_g, {n: env["m_" + n] for n in names}, {n: env["v_" + n] for n in names})],
        "adam_small")
    for n, shp, dl, nm_, nv_ in zip(names, shapes, deltas, new_ms, new_vs):
        res[n] = [small_g[n], dl.reshape(shp), nm_.reshape(shp), nv_.reshape(shp)]

    order = ["ffn1_norm_pre", "ffn1_norm_post", "ffn1_w_gu", "ffn1_w_down", "mix_norm_pre", "mix_norm_post", "w_in",
             "shift_mu", "w_up", "w0", "a_up", "a0", "g_up", "k_k", "k_a", "r_k", "gn_w", "gn_b", "conv_dw", "conv_b",
             "conv_ln_w", "conv_ln_b", "w_out", "ffn2_norm_pre", "ffn2_norm_post", "ffn2_w_gu", "ffn2_w_down"]
    return (loss, dx0.reshape(1, t, D), *[res[n][0] for n in order], *[res[n][1] for n in order],
            *[res[n][2] for n in order], *[res[n][3] for n in order])
```

```python
import functools
import math

import jax
import jax.numpy as jnp
from jax import lax
from jax.experimental import pallas as pl
from jax.experimental.pallas import tpu as pltpu

f32 = jnp.float32
bf16 = jnp.bfloat16

D = 1024
F = 2816
FB = 704
DR = 512
DC = 512
NH = 8
HN = 64
DS = 1792
DIN = 2816
CW = 31
CHUNK = 64
R1_CHUNKS = 4
R2_CHUNKS = 8
NDEV = 8
RMS_EPS = 1e-6
GN_EPS = 64e-5
LN_EPS = 1e-5
DECAY_SCALE = math.exp(-0.5)
ADAM_LR, ADAM_B1, ADAM_B2, ADAM_EPS, ADAM_WD, ADAM_STEP = 0.001, 0.9, 0.999, 1e-08, 0.01, 10
LANES = 128
HALO = 32
CONV_ROWS = 32
TM_MXU = 1024
TM_SUB = 256
TM_VPU = 256

ANY = pl.BlockSpec(memory_space=pl.ANY)


def _full(shape):
    return pl.BlockSpec(shape, lambda *_: (0,) * len(shape))


def _sds(shape, dtype=f32):
    return jax.ShapeDtypeStruct(shape, dtype)


def _dot(a, b):
    return jnp.dot(a, b, preferred_element_type=f32)


def _dot_nt(a, b):
    return lax.dot_general(a, b, (((1,), (1,)), ((), ())), preferred_element_type=f32)


def _dot_tn(a, b):
    return lax.dot_general(a, b, (((0,), (0,)), ((), ())), preferred_element_type=f32)


def _terms(a, n):
    out, rem = [], a
    for i in range(n):
        t = rem.astype(bf16)
        out.append(t)
        if i + 1 < n:
            rem = rem - t.astype(f32)
    return out


def _mm_raw(a, b, kind, mode):
    nb = a.ndim - 2
    bd = tuple(range(nb))
    ca = nb if kind == "tn" else nb + 1
    cb = nb + 1 if kind == "nt" else nb
    dn = (((ca,), (cb,)), (bd, bd))
    pa, pb = mode[:2]
    ta, tb = _terms(a, pa), _terms(b, pb)
    acc = None
    for i in range(pa):
        for j in range(pb):
            if i + j < max(pa, pb):
                p = lax.dot_general(ta[i], tb[j], dn, preferred_element_type=f32)
                acc = p if acc is None else acc + p
    return acc


@functools.partial(jax.custom_vjp, nondiff_argnums=(2, 3))
def mm(a, b, kind, mode):
    return _mm_raw(a, b, kind, mode)


def _mm_fwd(a, b, kind, mode):
    return _mm_raw(a, b, kind, mode), (a, b)


def _ct_terms(keep):
    return EXACT_TERMS if keep == 1 else keep


def _mm_bwd(kind, mode, res, ct):
    a, b = res
    pa, pb = mode[:2]
    if len(mode) == 4:
        ca = cb = mode[2]
        pa = pb = mode[3]
    elif mode == (1, 1):
        ca = cb = 1
    else:
        ca, cb = _ct_terms(pb), _ct_terms(pa)
    if kind == "nn":
        da = mm(ct, b, "nt", (ca, pb))
        db = mm(a, ct, "tn", (pa, cb))
    elif kind == "nt":
        da = mm(ct, b, "nn", (ca, pb))
        db = mm(ct, a, "tn", (cb, pa))
    else:
        da = mm(b, ct, "nt", (pb, ca))
        db = mm(a, ct, "nn", (pa, cb))
    return da, db


mm.defvjp(_mm_fwd, _mm_bwd)

X3 = (2, 2)
EXACT_TERMS = 2
EA = (1, EXACT_TERMS)
EB = (EXACT_TERMS, 1)
X1 = (1, 1)
RGM = (1, 1, 1, 2)
RIM = X1
RWM = X1
RAM = X1
R2M = X1
LORA = X1


def _rms_fwd(x, g):
    inv = lax.rsqrt(jnp.mean(x * x, axis=-1, keepdims=True) + RMS_EPS)
    return x * inv * g


def _rms_bwd(x, g, dy):
    inv = lax.rsqrt(jnp.mean(x * x, axis=-1, keepdims=True) + RMS_EPS)
    xh = x * inv
    dxh = dy * g
    dg = jnp.sum(dy * xh, axis=0, keepdims=True)
    dx = inv * (dxh - xh * jnp.mean(dxh * xh, axis=-1, keepdims=True))
    return dx, dg


def _tile(t, want):
    return min(t, want)


NPEER = NDEV - 1


def _comm_out_shape(arrays, modes, slots):
    out = []
    for a, mode, (lead, _) in zip(arrays, modes, slots):
        shp = tuple(lead) + tuple(a.shape) if mode == "gather" else tuple(a.shape)
        out.append(pltpu.HBM(shp, a.dtype))
    return out


def _comm_sems(n):
    return [pltpu.SemaphoreType.DMA((n * NPEER,)), pltpu.SemaphoreType.DMA((n * NPEER,)), pltpu.SemaphoreType.DMA((n,))]


RELAYED = (3, 5, 7)


def _comm_copies(ins, outs, sems, modes, slots, want):
    n = len(ins)
    send_sems, recv_sems, loc_sems = sems
    x, y, c = lax.axis_index("x"), lax.axis_index("y"), lax.axis_index("c")
    me = 4 * x + 2 * y + c

    def peer_of(k):
        px = 1 - x if k & 4 else x
        py = 1 - y if k & 2 else y
        pc = 1 - c if k & 1 else c
        return (px, py, pc), 4 * px + 2 * py + pc

    def slot(i, dev):
        return outs[i].at[slots[i][1](dev)]

    if want == "local":
        return [pltpu.make_async_copy(ins[i] if modes[i] == "gather" else ins[i].at[slots[i][1](me)], slot(i, me),
                                      loc_sems.at[i]) for i in range(n)]
    sibling = peer_of(1)[0]
    out = []
    for k in range(1, NDEV):
        peer, pid = peer_of(k)
        for i in range(n):
            sem = dict(send_sem=send_sems.at[i * NPEER + k - 1], recv_sem=recv_sems.at[i * NPEER + k - 1],
                       device_id_type=pl.DeviceIdType.MESH)
            relayed = modes[i] == "gather" and k in RELAYED
            if want == "recv":
                src = ins[i] if modes[i] == "gather" else ins[i].at[slots[i][1](pid)]
                out.append(pltpu.make_async_remote_copy(src_ref=src, dst_ref=slot(i, pid), device_id=peer, **sem))
            elif want == "first" and not relayed:
                src = ins[i] if modes[i] == "gather" else ins[i].at[slots[i][1](pid)]
                out.append(pltpu.make_async_remote_copy(src_ref=src, dst_ref=slot(i, me), device_id=peer, **sem))
            elif want == "relay" and relayed:
                origin = peer_of(k - 1)[1]
                out.append(((k - 2) * n + i, pltpu.make_async_remote_copy(
                    src_ref=slot(i, origin), dst_ref=slot(i, origin), device_id=sibling, **sem)))
    return out


def _comm_start(ins, outs, sems, modes, slots):
    for cp in _comm_copies(ins, outs, sems, modes, slots, "local") + _comm_copies(ins, outs, sems, modes, slots, "first"):
        cp.start()


def _comm_wait(ins, outs, sems, modes, slots):
    recvs = _comm_copies(ins, outs, sems, modes, slots, "recv")
    relays = _comm_copies(ins, outs, sems, modes, slots, "relay")
    for idx, cp in relays:
        recvs[idx].wait_recv()
        cp.start()
    passed_on = {idx for idx, _ in relays}
    for idx, cp in enumerate(recvs):
        if idx not in passed_on:
            cp.wait_recv()
    for cp in _comm_copies(ins, outs, sems, modes, slots, "first") + [cp for _, cp in relays]:
        cp.wait_send()
    for cp in _comm_copies(ins, outs, sems, modes, slots, "local"):
        cp.wait()


def _exchange(arrays, modes, slots, name):
    n = len(arrays)

    def body(*refs):
        ins, outs, sems = refs[:n], refs[n:2 * n], refs[2 * n:]
        _comm_start(ins, outs, sems, modes, slots)
        _comm_wait(ins, outs, sems, modes, slots)

    return pl.pallas_call(
        body, name=name, out_shape=_comm_out_shape(arrays, modes, slots),
        in_specs=[ANY] * n, out_specs=[ANY] * n, scratch_shapes=_comm_sems(n),
    )(*arrays)


HBM_SPEC = pl.BlockSpec(memory_space=pltpu.HBM)
SEM_SPEC = pl.BlockSpec(memory_space=pltpu.SEMAPHORE)


def _scatter_start(arr, slot, name):
    def body(v_ref, land_ref, send_sems, recv_sems, v_thru, land_thru, token):
        for cp in _comm_copies([v_ref], [land_ref], (send_sems, recv_sems, None), ["scatter"], [slot], "first"):
            cp.start()
        token[...] = jnp.zeros_like(token)

    return pl.pallas_call(
        body, name=name,
        out_shape=(pltpu.SemaphoreType.DMA((NPEER,)), pltpu.SemaphoreType.DMA((NPEER,)),
                   pltpu.HBM(arr.shape, arr.dtype), pltpu.HBM(arr.shape, arr.dtype), _sds((8, LANES))),
        in_specs=(HBM_SPEC, HBM_SPEC),
        out_specs=(SEM_SPEC, SEM_SPEC, HBM_SPEC, HBM_SPEC, pl.BlockSpec(memory_space=pltpu.VMEM)),
        input_output_aliases={0: 2, 1: 3},
        compiler_params=pltpu.CompilerParams(has_side_effects=pltpu.SideEffectType.DATAFLOW_SIDE_EFFECTING),
    )(pltpu.with_memory_space_constraint(arr, pltpu.HBM),
      pltpu.with_memory_space_constraint(lax.empty(arr.shape, arr.dtype), pltpu.HBM))


def _scatter_wait(send_sems, recv_sems, v_thru, land_thru, after, slot, name):
    def body(v_ref, land_ref, send_sems, recv_sems, *rest):
        sems = (send_sems, recv_sems, None)
        for cp in _comm_copies([v_ref], [land_ref], sems, ["scatter"], [slot], "first"):
            cp.wait_send()
        for cp in _comm_copies([v_ref], [land_ref], sems, ["scatter"], [slot], "recv"):
            cp.wait_recv()

    return pl.pallas_call(
        body, name=name, out_shape=(pltpu.HBM(v_thru.shape, v_thru.dtype), pltpu.HBM(v_thru.shape, v_thru.dtype)),
        in_specs=(HBM_SPEC, HBM_SPEC, SEM_SPEC, SEM_SPEC) + (ANY,) * len(after), out_specs=(HBM_SPEC, HBM_SPEC),
        input_output_aliases={0: 0, 1: 1},
        compiler_params=pltpu.CompilerParams(has_side_effects=pltpu.SideEffectType.DATAFLOW_SIDE_EFFECTING),
    )(v_thru, land_thru, send_sems, recv_sems, *after)


def _pcall(body, *, name, grid, in_specs, out_specs, out_shape, args, scratch_shapes=(), comm=None):
    if comm is None:
        return pl.pallas_call(body, name=name, grid=grid, in_specs=in_specs, out_specs=out_specs, out_shape=out_shape,
                              scratch_shapes=list(scratch_shapes))(*args), None
    arrays, modes, slots = comm
    n_in, n_out, n_scr, nc = len(args), len(out_shape), len(scratch_shapes), len(arrays)

    def hosted(*refs):
        a_in, c_in = refs[:n_in], refs[n_in:n_in + nc]
        o = n_in + nc
        a_out, c_out = refs[o:o + n_out], refs[o + n_out:o + n_out + nc]
        o += n_out + nc
        a_scr, sems = refs[o:o + n_scr], refs[o + n_scr:]
        first = pl.program_id(0) == 0
        last = pl.program_id(0) == grid[0] - 1
        for ax in range(1, len(grid)):
            first = jnp.logical_and(first, pl.program_id(ax) == 0)
            last = jnp.logical_and(last, pl.program_id(ax) == grid[ax] - 1)

        @pl.when(first)
        def _():
            _comm_start(c_in, c_out, sems, modes, slots)

        body(*a_in, *a_out, *a_scr)

        @pl.when(last)
        def _():
            _comm_wait(c_in, c_out, sems, modes, slots)

    res = pl.pallas_call(
        hosted, name=name, grid=grid, in_specs=list(in_specs) + [ANY] * nc, out_specs=list(out_specs) + [ANY] * nc,
        out_shape=list(out_shape) + _comm_out_shape(arrays, modes, slots),
        scratch_shapes=list(scratch_shapes) + _comm_sems(nc),
    )(*args, *arrays)
    return res[:n_out], res[n_out:]


SLOT8 = ((NDEV,), lambda d: (d,))
SLOT42 = ((4, 2), lambda d: (lax.rem(d, 4), lax.div(d, 4)))


def _ffn_in_fwd(x, g, wg4, name, comm=None):
    t = x.shape[0]
    tm = _tile(t, TM_MXU)
    sub = _tile(tm, TM_SUB)

    def body(x_ref, g_ref, w_ref, xn_ref, gu_ref, h_ref):
        for s in range(tm // sub):
            rows = pl.ds(s * sub, sub)
            xn = _rms_fwd(x_ref[rows, :], g_ref[...]).astype(bf16)
            xn_ref[rows, :] = xn
            gate = _dot_nt(xn, w_ref[0, 0])
            up = _dot_nt(xn, w_ref[0, 1])
            gu_ref[0, 0, rows, :] = gate.astype(bf16)
            gu_ref[0, 1, rows, :] = up.astype(bf16)
            h_ref[0, rows, :] = (gate * jax.nn.sigmoid(gate) * up).astype(bf16)

    return _pcall(
        body, name=name, grid=(t // tm, 4),
        in_specs=[pl.BlockSpec((tm, D), lambda i, e: (i, 0)), _full((1, D)),
                  pl.BlockSpec((1, 2, FB, D), lambda i, e: (e, 0, 0, 0))],
        out_specs=[pl.BlockSpec((tm, D), lambda i, e: (i, 0)),
                   pl.BlockSpec((1, 2, tm, FB), lambda i, e: (e, 0, i, 0)),
                   pl.BlockSpec((1, tm, FB), lambda i, e: (e, i, 0))],
        out_shape=[_sds((t, D), bf16), _sds((4, 2, t, FB), bf16), _sds((4, t, FB), bf16)],
        args=(x, g, wg4), comm=comm)


def _mm_norm_res_fwd(hb, wb, xres, g, scale, name, comm=None, tgt=None):
    e_n, t, k = hb.shape
    tm = _tile(t, TM_MXU)

    def body(h_ref, w_ref, x_ref, g_ref, *rest):
        f = _dot(h_ref[0], w_ref[0])
        for e in range(1, e_n):
            f = f + _dot(h_ref[e], w_ref[e])
        xnew = x_ref[...] + scale * _rms_fwd(f, g_ref[...])
        if tgt is None:
            f_ref, o_ref = rest
            o_ref[...] = xnew
        else:
            t_ref, f_ref, o_ref, l_ref = rest
            err = xnew - t_ref[...]
            o_ref[...] = err * (1.0 / D)
            part = 0.5 * jnp.sum(jnp.mean(err * err, axis=-1, keepdims=True), axis=0, keepdims=True)

            @pl.when(pl.program_id(0) == 0)
            def _():
                l_ref[...] = part

            @pl.when(pl.program_id(0) != 0)
            def _():
                l_ref[...] += part
        f_ref[...] = f

    tile = pl.BlockSpec((tm, D), lambda i: (i, 0))
    outs, got = _pcall(
        body, name=name, grid=(t // tm,),
        in_specs=[pl.BlockSpec((e_n, tm, k), lambda i: (0, i, 0)), _full((e_n, k, D)), tile, _full((1, D))]
        + ([] if tgt is None else [tile]),
        out_specs=[tile, tile] + ([] if tgt is None else [_full((1, 1))]),
        out_shape=[_sds((t, D)), _sds((t, D))] + ([] if tgt is None else [_sds((1, 1))]),
        args=(hb, wb, xres, g) + (() if tgt is None else (tgt,)), comm=comm)
    return outs if comm is None else (outs, got)


def _post_bwd_nt(dxn, f, g, wb, scale, gu, name, comm=None):
    e_n, k, _ = wb.shape
    t = f.shape[0]
    tm = _tile(t, TM_MXU)
    sub = _tile(tm, TM_SUB)
    swiglu = gu is not None

    def body(*refs):
        if swiglu:
            dx_ref, f_ref, g_ref, w_ref, gu_ref, df_ref, dg_ref, dh_ref, df_s = refs
        else:
            dx_ref, f_ref, g_ref, w_ref, df_ref, dg_ref, dh_ref, df_s = refs
        i, e = pl.program_id(0), pl.program_id(1)

        @pl.when(e == 0)
        def _():
            df, dg = _rms_bwd(f_ref[...], g_ref[...], scale * dx_ref[...])
            df_s[...] = df.astype(bf16)
            df_ref[...] = df_s[...]

            @pl.when(i == 0)
            def _():
                dg_ref[...] = dg

            @pl.when(i != 0)
            def _():
                dg_ref[...] += dg

        for s in range(tm // sub):
            rows = pl.ds(s * sub, sub)
            dh = _dot_nt(df_s[rows, :], w_ref[0])
            if swiglu:
                gate = gu_ref[0, 0, rows, :].astype(f32)
                up = gu_ref[0, 1, rows, :].astype(f32)
                sg = jax.nn.sigmoid(gate)
                dh_ref[0, 0, rows, :] = (dh * up * (sg * (1.0 + gate * (1.0 - sg)))).astype(bf16)
                dh_ref[0, 1, rows, :] = (dh * gate * sg).astype(bf16)
            else:
                dh_ref[0, rows, :] = dh

    in_specs = [pl.BlockSpec((tm, D), lambda i, e: (i, 0)), pl.BlockSpec((tm, D), lambda i, e: (i, 0)), _full((1, D)),
                pl.BlockSpec((1, k, D), lambda i, e: (e, 0, 0))]
    args = [dxn, f, g, wb]
    if swiglu:
        in_specs.append(pl.BlockSpec((1, 2, tm, k), lambda i, e: (e, 0, i, 0)))
        args.append(gu)
        dh_spec = pl.BlockSpec((1, 2, tm, k), lambda i, e: (e, 0, i, 0))
        dh_shape = _sds((e_n, 2, t, k), bf16)
    else:
        dh_spec = pl.BlockSpec((1, tm, k), lambda i, e: (e, i, 0))
        dh_shape = _sds((e_n, t, k), f32)
    return _pcall(
        body, name=name, grid=(t // tm, e_n), in_specs=in_specs,
        out_specs=[pl.BlockSpec((tm, D), lambda i, e: (i, 0)), _full((1, D)), dh_spec],
        out_shape=[_sds((t, D), bf16), _sds((1, D)), dh_shape],
        scratch_shapes=[pltpu.VMEM((tm, D), bf16)], args=args, comm=comm)


def _nt_pre_bwd(dy, wb, x, g, dres, name, comm=None):
    e_n, q_n, t, k = dy.shape
    tm = _tile(t, TM_MXU)

    def body(dy_ref, w_ref, x_ref, g_ref, r_ref, dx_ref, dg_ref, acc):
        i, e = pl.program_id(0), pl.program_id(1)
        p = _dot(dy_ref[0, 0], w_ref[0, 0])
        for q in range(1, q_n):
            p = p + _dot(dy_ref[0, q], w_ref[0, q])

        @pl.when(e == 0)
        def _():
            acc[...] = p

        @pl.when(e != 0)
        def _():
            acc[...] += p

        @pl.when(e == e_n - 1)
        def _():
            dx, dg = _rms_bwd(x_ref[...], g_ref[...], acc[...])
            dx_ref[...] = r_ref[...] + dx

            @pl.when(i == 0)
            def _():
                dg_ref[...] = dg

            @pl.when(i != 0)
            def _():
                dg_ref[...] += dg

    return _pcall(
        body, name=name, grid=(t // tm, e_n),
        in_specs=[pl.BlockSpec((1, q_n, tm, k), lambda i, e: (e, 0, i, 0)),
                  pl.BlockSpec((1, q_n, k, D), lambda i, e: (e, 0, 0, 0)),
                  pl.BlockSpec((tm, D), lambda i, e: (i, 0)), _full((1, D)),
                  pl.BlockSpec((tm, D), lambda i, e: (i, 0))],
        out_specs=[pl.BlockSpec((tm, D), lambda i, e: (i, 0)), _full((1, D))],
        out_shape=[_sds((t, D)), _sds((1, D))],
        scratch_shapes=[pltpu.VMEM((tm, D), f32)], args=(dy, wb, x, g, dres), comm=comm)


def _wgrad(a, b, a_batched, b_batched, tn, name, comm=None):
    t, k = a.shape[1], a.shape[2]
    nn = b.shape[2]
    nb = max(a.shape[0], b.shape[0])
    tt = _tile(t, TM_MXU)
    nt = t // tt

    def body(a_ref, b_ref, o_ref, acc):
        @pl.when(pl.program_id(2) == 0)
        def _():
            acc[...] = _dot_tn(a_ref[0], b_ref[0])

        @pl.when(pl.program_id(2) != 0)
        def _():
            acc[...] += _dot_tn(a_ref[0], b_ref[0])

        @pl.when(pl.program_id(2) == nt - 1)
        def _():
            o_ref[0] = acc[...].astype(bf16)

    (out,), got = _pcall(
        body, name=name, grid=(nb, nn // tn, nt),
        in_specs=[pl.BlockSpec((1, tt, k), (lambda n, j, s: (n, s, 0)) if a_batched else (lambda n, j, s: (0, s, 0))),
                  pl.BlockSpec((1, tt, tn), (lambda n, j, s: (n, s, j)) if b_batched else (lambda n, j, s: (0, s, j)))],
        out_specs=[pl.BlockSpec((1, k, tn), lambda n, j, s: (n, 0, j))],
        out_shape=[_sds((nb, k, nn), bf16)],
        scratch_shapes=[pltpu.VMEM((k, tn), f32)], args=(a, b), comm=comm)
    return out if comm is None else (out, got)


def _norm_mm_fwd(x, g, w, name):
    t = x.shape[0]
    tm = _tile(t, TM_VPU)

    def body(x_ref, g_ref, w_ref, xn_ref, ps_ref, pc_ref):
        xn = _rms_fwd(x_ref[...], g_ref[...]).astype(bf16)
        xn_ref[...] = xn
        ps_ref[...] = _dot_nt(xn, w_ref[:DS, :])
        pc_ref[...] = _dot_nt(xn, w_ref[DS:, :])

    return pl.pallas_call(
        body, name=name, grid=(t // tm,),
        in_specs=[pl.BlockSpec((tm, D), lambda i: (i, 0)), _full((1, D)), _full((DIN, D))],
        out_specs=[pl.BlockSpec((tm, D), lambda i: (i, 0)), pl.BlockSpec((tm, DS), lambda i: (i, 0)),
                   pl.BlockSpec((tm, 2 * DC), lambda i: (i, 0))],
        out_shape=[_sds((t, D), bf16), _sds((t, DS)), _sds((t, 2 * DC))],
    )(x, g, w)


def _segsum_b(x, seg, seg_t):
    return mm(mm(x, seg, "nn", EB), seg_t, "nn", EB)


def _prep_fn(psl, w0, a0, k_k, k_a, wup, aup, gup, seg, seg_t):
    r, k, v = psl[:, :DR], psl[:, DR:2 * DR], psl[:, 2 * DR:3 * DR]
    xwa, xg = psl[:, 3 * DR:3 * DR + LANES], psl[:, 3 * DR + LANES:]
    d = w0 + mm(jnp.tanh(xwa), wup, "nn", LORA)
    lw = -DECAY_SCALE * jax.nn.sigmoid(d)
    a = jax.nn.sigmoid(a0 + mm(xwa, aup, "nn", LORA))
    g = mm(jax.nn.sigmoid(xg), gup, "nn", LORA)
    kkr = k * k_k
    kk = kkr * lax.rsqrt(jnp.maximum(_segsum_b(kkr * kkr, seg, seg_t), 1e-12))
    kp = k * (1.0 + (a - 1.0) * k_a)
    return r, lw, kp, v, -kk, kk * a, g


def _shifted(ps, halo_row, first):
    prev = jnp.where(first, 0.0, halo_row)
    sh = pltpu.roll(ps, 1, 0)
    row = lax.broadcasted_iota(jnp.int32, ps.shape, 0)
    return jnp.where(row == 0, prev, sh)


def _heads_of(ref, rows):
    return jnp.stack([ref[rows, h * HN:(h + 1) * HN] for h in range(NH)], axis=0)


def _heads_to(ref, rows, val):
    ref[rows, :] = jnp.concatenate([val[h] for h in range(NH)], axis=-1)


_PREP_PARAM_SHAPES = [(1, DS), (1, DR), (1, DR), (1, DR), (1, DR), (LANES, DR), (LANES, DR), (LANES, DR),
                      (DR, LANES), (LANES, DR)]


def _prep_fwd(ps, params, name, comm=None):
    t = ps.shape[0]
    tm = _tile(t, TM_VPU)
    hb = tm // 8

    def body(ps_ref, halo_ref, mu_ref, *rest):
        prm = [r[...] for r in rest[:9]]
        outs = rest[9:]
        x = ps_ref[...]
        sh = _shifted(x, halo_ref[pl.ds(7, 1), :], pl.program_id(0) == 0)
        psl = x + (sh - x) * mu_ref[...]
        vals = _prep_fn(psl, *prm)
        for ref, val in zip(outs, vals):
            ref[...] = val

    outs, got = _pcall(
        body, name=name, grid=(t // tm,),
        in_specs=[pl.BlockSpec((tm, DS), lambda i: (i, 0)),
                  pl.BlockSpec((8, DS), lambda i: (jnp.maximum(i * hb - 1, 0), 0))]
        + [_full(s) for s in _PREP_PARAM_SHAPES],
        out_specs=[pl.BlockSpec((tm, DR), lambda i: (i, 0))] * 7,
        out_shape=[_sds((t, DR))] * 7, args=(ps, ps, *params), comm=comm)
    return outs if comm is None else (outs, got)


def _prep_bwd(ps, params, cts_hm, cts_tm, name):
    t = ps.shape[0]
    tm = _tile(t, TM_VPU)
    hb = tm // 8

    def body(ps_ref, halo_ref, mu_ref, *rest):
        prm = [r[...] for r in rest[:9]]
        chm = rest[9:15]
        ctm = rest[15:19]
        dpsl_ref = rest[19]
        gouts = rest[20:27]
        x = ps_ref[...]
        sh = _shifted(x, halo_ref[pl.ds(7, 1), :], pl.program_id(0) == 0)
        psl = x + (sh - x) * mu_ref[...]
        seg, seg_t = prm[7], prm[8]
        _, vjp = jax.vjp(lambda p, *w: _prep_fn(p, *w, seg, seg_t), psl, *prm[:7])
        c = [r[...] for r in chm]
        cts = (c[0] + ctm[0][...], c[1], c[2] + ctm[1][...], c[3] + ctm[2][...], c[4], c[5], ctm[3][...])
        grads = vjp(cts)
        dpsl_ref[...] = grads[0]
        for ref, gval in zip(gouts, grads[1:]):
            @pl.when(pl.program_id(0) == 0)
            def _(ref=ref, gval=gval):
                ref[...] = gval

            @pl.when(pl.program_id(0) != 0)
            def _(ref=ref, gval=gval):
                ref[...] += gval

    tk = pl.BlockSpec((tm, DR), lambda i: (i, 0))
    gshapes = _PREP_PARAM_SHAPES[1:8]
    return pl.pallas_call(
        body, name=name, grid=(t // tm,),
        in_specs=[pl.BlockSpec((tm, DS), lambda i: (i, 0)),
                  pl.BlockSpec((8, DS), lambda i: (jnp.maximum(i * hb - 1, 0), 0))]
        + [_full(s) for s in _PREP_PARAM_SHAPES] + [tk] * 10,
        out_specs=[pl.BlockSpec((tm, DS), lambda i: (i, 0))] + [_full(s) for s in gshapes],
        out_shape=[_sds((t, DS))] + [_sds(s) for s in gshapes],
    )(ps, ps, *params, *cts_hm, *cts_tm)


def _shift_bwd(dpsl, ps, mu, dpc, name):
    t = ps.shape[0]
    tm = _tile(t, TM_VPU)
    hb = tm // 8
    last_blk = t // 8 - 1

    def body(d_ref, dn_ref, ps_ref, halo_ref, mu_ref, dpc_ref, dp_ref, dmu_ref):
        i = pl.program_id(0)
        mu_v = mu_ref[...]
        d = d_ref[...]
        nxt = jnp.where(i == pl.num_programs(0) - 1, 0.0, dn_ref[pl.ds(0, 1), :])
        up = pltpu.roll(d, tm - 1, 0)
        row = lax.broadcasted_iota(jnp.int32, d.shape, 0)
        up = jnp.where(row == tm - 1, nxt, up)
        dp_ref[:, :DS] = (d * (1.0 - mu_v) + up * mu_v).astype(bf16)
        dp_ref[:, DS:] = dpc_ref[...].astype(bf16)
        x = ps_ref[...]
        sh = _shifted(x, halo_ref[pl.ds(7, 1), :], i == 0)
        dmu = jnp.sum(d * (sh - x), axis=0, keepdims=True)

        @pl.when(i == 0)
        def _():
            dmu_ref[...] = dmu

        @pl.when(i != 0)
        def _():
            dmu_ref[...] += dmu

    return pl.pallas_call(
        body, name=name, grid=(t // tm,),
        in_specs=[pl.BlockSpec((tm, DS), lambda i: (i, 0)),
                  pl.BlockSpec((8, DS), lambda i: (jnp.minimum((i + 1) * hb, last_blk), 0)),
                  pl.BlockSpec((tm, DS), lambda i: (i, 0)),
                  pl.BlockSpec((8, DS), lambda i: (jnp.maximum(i * hb - 1, 0), 0)),
                  _full((1, DS)), pl.BlockSpec((tm, 2 * DC), lambda i: (i, 0))],
        out_specs=[pl.BlockSpec((tm, DIN), lambda i: (i, 0)), _full((1, DS))],
        out_shape=[_sds((t, DIN), bf16), _sds((1, DS))],
    )(dpsl, dpsl, ps, ps, mu, dpc)


def _post_fn(y, r, kp, v, g, gn_w, gn_b, r_k, seg, seg_t):
    mu = _segsum_b(y, seg, seg_t) * (1.0 / HN)
    yc = y - mu
    var = _segsum_b(yc * yc, seg, seg_t) * (1.0 / HN)
    yo = yc * lax.rsqrt(var + GN_EPS) * gn_w + gn_b
    bonus = _segsum_b(r * kp * r_k, seg, seg_t) * v
    return (yo + bonus) * g


_POST_PARAM_SHAPES = [(1, DR), (1, DR), (1, DR), (DR, LANES), (LANES, DR)]


def _post_fwd(y, r, kp, v, g, out_b, params, name):
    t = g.shape[0]
    tm = _tile(t, TM_VPU)

    def body(y_ref, r_ref, k_ref, v_ref, g_ref, ob_ref, *rest):
        prm = [p[...] for p in rest[:5]]
        cat_ref = rest[5]
        oa = _post_fn(y_ref[...], r_ref[...], k_ref[...], v_ref[...], g_ref[...], *prm)
        cat_ref[:, :DR] = oa.astype(bf16)
        cat_ref[:, DR:] = ob_ref[...]

    tk = pl.BlockSpec((tm, DR), lambda i: (i, 0))
    return pl.pallas_call(
        body, name=name, grid=(t // tm,),
        in_specs=[tk] * 6 + [_full(s) for s in _POST_PARAM_SHAPES],
        out_specs=pl.BlockSpec((tm, 2 * DR), lambda i: (i, 0)),
        out_shape=_sds((t, 2 * DR), bf16),
    )(y, r, kp, v, g, out_b, *params)


def _post_bwd(dcat, y, r, kp, v, g, params, name, comm=None):
    t = g.shape[0]
    tm = _tile(t, TM_VPU)

    def body(dc_ref, y_ref, r_ref, k_ref, v_ref, g_ref, *rest):
        prm = [p[...] for p in rest[:5]]
        dy_ref, dr_ref, dk_ref, dv_ref, dg_ref = rest[5:10]
        gouts = rest[10:13]
        seg, seg_t = prm[3], prm[4]
        _, vjp = jax.vjp(lambda *a: _post_fn(*a, seg, seg_t),
                         y_ref[...], r_ref[...], k_ref[...], v_ref[...], g_ref[...], *prm[:3])
        grads = vjp(dc_ref[...])
        dy_ref[...] = grads[0]
        dr_ref[...] = grads[1]
        dk_ref[...] = grads[2]
        dv_ref[...] = grads[3]
        dg_ref[...] = grads[4]
        for ref, gval in zip(gouts, grads[5:]):
            @pl.when(pl.program_id(0) == 0)
            def _(ref=ref, gval=gval):
                ref[...] = gval

            @pl.when(pl.program_id(0) != 0)
            def _(ref=ref, gval=gval):
                ref[...] += gval

    tk = pl.BlockSpec((tm, DR), lambda i: (i, 0))
    return _pcall(
        body, name=name, grid=(t // tm,),
        in_specs=[tk] * 6 + [_full(s) for s in _POST_PARAM_SHAPES],
        out_specs=[tk] * 5 + [_full((1, DR))] * 3,
        out_shape=[_sds((t, DR))] * 5 + [_sds((1, DR))] * 3,
        args=(dcat, y, r, kp, v, g, *params), comm=comm)


@jax.custom_vjp
def _inv_unit_lower(lb):
    c = lb.shape[-1]
    row = lax.broadcasted_iota(jnp.int32, (c, c), 0)
    col = lax.broadcasted_iota(jnp.int32, (c, c), 1)
    p = (row == col).astype(f32)[None] + lb
    m = lb
    for _ in range(c.bit_length() - 2):
        m = mm(m, m, "nn", RIM)
        p = p + mm(m, p, "nn", RIM)
    return p


def _inv_unit_lower_fwd(lb):
    p = _inv_unit_lower(lb)
    return p, p


def _inv_unit_lower_bwd(p, ct):
    return (mm(mm(p, ct, "tn", RIM), p, "nt", RIM),)


_inv_unit_lower.defvjp(_inv_unit_lower_fwd, _inv_unit_lower_bwd)


@jax.custom_vjp
def _known_inverse(lb, p):
    return p


def _known_inverse_fwd(lb, p):
    return p, p


def _known_inverse_bwd(p, ct):
    return mm(mm(p, ct, "tn", RIM), p, "nt", RIM), jnp.zeros_like(p)


_known_inverse.defvjp(_known_inverse_fwd, _known_inverse_bwd)


def _r1_fn(r, lw, k, v, z, b, p_saved=None, with_p=False):
    c = r.shape[1]
    row = lax.broadcasted_iota(jnp.int32, (c, c), 0)
    col = lax.broadcasted_iota(jnp.int32, (c, c), 1)
    incl = (row >= col)[None]
    strict = (row > col)[None]
    eye = (row == col).astype(f32)[None]
    tri = jnp.broadcast_to((row >= col).astype(f32)[None], (r.shape[0], c, c))
    cum = mm(tri, lw, "nn", EA)
    tot = jnp.sum(lw, axis=1, keepdims=True)
    zt = z * jnp.exp(cum - lw)
    rt = r * jnp.exp(cum)
    g_inv = jnp.exp(-cum)
    g_rem = jnp.exp(tot - cum)
    bt, kt = b * g_inv, k * g_inv
    bh, kh = b * g_rem, k * g_rem
    lb = jnp.where(strict, mm(zt, bt, "nt", RGM), 0.0)
    lk = jnp.where(strict, mm(zt, kt, "nt", RGM), 0.0)
    mb = jnp.where(incl, mm(rt, bt, "nt", RGM), 0.0)
    mk = jnp.where(incl, mm(rt, kt, "nt", RGM), 0.0)
    p = _inv_unit_lower(lb) if p_saved is None else _known_inverse(lb, p_saved)
    w1 = mm(p, zt, "nn", RWM)
    w2 = mm(p, mm(lk, v, "nn", RWM), "nn", RWM)
    a = mm(w1, bh, "tn", RAM) + eye * jnp.exp(tot)
    g = mm(w2, bh, "tn", RAM) + mm(v, kh, "tn", RAM)
    q = rt + mm(mb, w1, "nn", RWM)
    yl = mm(mb, w2, "nn", RWM) + mm(mk, v, "nn", RWM)
    return (a, g, q, yl, p) if with_p else (a, g, q, yl)


def _chunks_in(ref, n):
    return jnp.concatenate([_heads_of(ref, pl.ds(s * CHUNK, CHUNK)) for s in range(n)], axis=0)


def _chunks_out(ref, val, n):
    for s in range(n):
        _heads_to(ref, pl.ds(s * CHUNK, CHUNK), val[s * NH:(s + 1) * NH])


def _r1_fwd(ins, name, comm=None):
    t = ins[0].shape[0]
    n = R1_CHUNKS
    nc = t // CHUNK

    def body(r, lw, k, v, z, b, a_ref, g_ref, p_ref, q_ref, yl_ref):
        a, g, q, yl, p = _r1_fn(*[_chunks_in(x, n) for x in (r, lw, k, v, z, b)], with_p=True)
        for s in range(n):
            a_ref[s] = a[s * NH:(s + 1) * NH]
            g_ref[s] = g[s * NH:(s + 1) * NH]
            p_ref[s] = p[s * NH:(s + 1) * NH]
        _chunks_out(q_ref, q, n)
        _chunks_out(yl_ref, yl, n)

    ck = pl.BlockSpec((n * CHUNK, DR), lambda c: (c, 0))
    st = pl.BlockSpec((n, NH, HN, HN), lambda c: (c, 0, 0, 0))
    return _pcall(
        body, name=name, grid=(nc // n,), in_specs=[ck] * 6, out_specs=[st, st, st, ck, ck],
        out_shape=[_sds((nc, NH, HN, HN))] * 3 + [_sds((t, DR))] * 2, args=tuple(ins), comm=comm)


def _r1_bwd(ins, p_inv, da, dg, dq, dyl, name, comm=None):
    t = ins[0].shape[0]
    n = R1_CHUNKS
    nc = t // CHUNK

    def body(r, lw, k, v, z, b, p_ref, da_ref, dg_ref, dq_ref, dyl_ref, *outs):
        p_saved = jnp.concatenate([p_ref[s] for s in range(n)], axis=0)
        _, vjp = jax.vjp(lambda *a: _r1_fn(*a, p_saved=p_saved), *[_chunks_in(x, n) for x in (r, lw, k, v, z, b)])
        cts = (jnp.concatenate([da_ref[s] for s in range(n)], axis=0),
               jnp.concatenate([dg_ref[s] for s in range(n)], axis=0), _chunks_in(dq_ref, n), _chunks_in(dyl_ref, n))
        for ref, gval in zip(outs, vjp(cts)):
            _chunks_out(ref, gval, n)

    ck = pl.BlockSpec((n * CHUNK, DR), lambda c: (c, 0))
    st = pl.BlockSpec((n, NH, HN, HN), lambda c: (c, 0, 0, 0))
    return _pcall(
        body, name=name, grid=(nc // n,), in_specs=[ck] * 6 + [st, st, st, ck, ck], out_specs=[ck] * 6,
        out_shape=[_sds((t, DR))] * 6, args=(*ins, p_inv, da, dg, dq, dyl), comm=comm)


def _r2_fwd(a, g, q, yl, name):
    nc = a.shape[0]
    t = q.shape[0]

    def body(a_ref, g_ref, q_ref, yl_ref, y_ref, s_ref, s):
        @pl.when(pl.program_id(0) == 0)
        def _():
            s[...] = jnp.zeros_like(s)
        s0 = s[...]
        for j in range(n):
            rows = pl.ds(j * CHUNK, CHUNK)
            s_ref[j] = s0
            _heads_to(y_ref, rows, mm(_heads_of(q_ref, rows), s0, "nt", R2M) + _heads_of(yl_ref, rows))
            s0 = mm(s0, a_ref[j], "nn", R2M) + g_ref[j]
        s[...] = s0

    n = math.gcd(nc, R2_CHUNKS)
    ck = pl.BlockSpec((n * CHUNK, DR), lambda c: (c, 0))
    st = pl.BlockSpec((n, NH, HN, HN), lambda c: (c, 0, 0, 0))
    return pl.pallas_call(
        body, name=name, grid=(nc // n,), in_specs=[st, st, ck, ck], out_specs=[ck, st],
        out_shape=[_sds((t, DR)), _sds((nc, NH, HN, HN))],
        scratch_shapes=[pltpu.VMEM((NH, HN, HN), f32)],
    )(a, g, q, yl)


def _r2_bwd(dy, q, s_all, a, name):
    nc = a.shape[0]
    t = q.shape[0]

    def body(dy_ref, q_ref, s_ref, a_ref, dq_ref, da_ref, dg_ref, ds):
        @pl.when(pl.program_id(0) == 0)
        def _():
            ds[...] = jnp.zeros_like(ds)
        dsn = ds[...]
        for j in reversed(range(n)):
            rows = pl.ds(j * CHUNK, CHUNK)
            s0 = s_ref[j]
            dyv = _heads_of(dy_ref, rows)
            dg_ref[j] = dsn
            da_ref[j] = mm(s0, dsn, "tn", R2M)
            _heads_to(dq_ref, rows, mm(dyv, s0, "nn", R2M))
            dsn = mm(dsn, a_ref[j], "nt", R2M) + mm(dyv, _heads_of(q_ref, rows), "tn", R2M)
        ds[...] = dsn

    n = math.gcd(nc, R2_CHUNKS)
    nb = nc // n
    ck = pl.BlockSpec((n * CHUNK, DR), lambda c: (nb - 1 - c, 0))
    st = pl.BlockSpec((n, NH, HN, HN), lambda c: (nb - 1 - c, 0, 0, 0))
    return pl.pallas_call(
        body, name=name, grid=(nb,), in_specs=[ck, ck, st, st], out_specs=[ck, st, st],
        out_shape=[_sds((t, DR)), _sds((nc, NH, HN, HN)), _sds((nc, NH, HN, HN))],
        scratch_shapes=[pltpu.VMEM((NH, HN, HN), f32)],
    )(dy, q, s_all, a)


def _ln_silu(c, w, b):
    mu = jnp.mean(c, axis=-1, keepdims=True)
    cc = c - mu
    var = jnp.mean(cc * cc, axis=-1, keepdims=True)
    u = cc * lax.rsqrt(var + LN_EPS) * w + b
    return u * jax.nn.sigmoid(u)


def _glu_tile(pc):
    return pc[:, :DC] * jax.nn.sigmoid(pc[:, DC:])


def _fill_glu(s_ref, pc_ref, halo_ref, first, tm):
    s_ref[pl.ds(0, HALO), :] = jnp.where(first, 0.0, _glu_tile(halo_ref[...]))
    s_ref[pl.ds(HALO, tm), :] = _glu_tile(pc_ref[...])


def _phase_copies(sh):
    n = sh.shape[1] - 8
    for r in range(1, 8):
        sh[r, pl.ds(0, n), :] = sh[0, pl.ds(r, n), :]


def _rows_at(sh, o, n):
    return sh[o % 8, pl.ds(o - o % 8, n), :]


def _conv_fwd(pc, dw, cb, lnw, lnb, name):
    t = pc.shape[0]
    tm = _tile(t, TM_VPU)
    hb = tm // HALO

    def body(pc_ref, halo_ref, dw_ref, cb_ref, w_ref, b_ref, c_ref, o_ref, s):
        _fill_glu(s.at[0], pc_ref, halo_ref, pl.program_id(0) == 0, tm)
        _phase_copies(s)
        for r0 in range(0, tm, CONV_ROWS):
            rows = pl.ds(r0, CONV_ROWS)
            acc = _rows_at(s, r0 + HALO - CW + 1, CONV_ROWS) * dw_ref[pl.ds(0, 1), :]
            for j in range(1, CW):
                acc = acc + _rows_at(s, r0 + HALO - CW + 1 + j, CONV_ROWS) * dw_ref[pl.ds(j, 1), :]
            c = acc + cb_ref[...]
            c_ref[rows, :] = c
            o_ref[rows, :] = _ln_silu(c, w_ref[...], b_ref[...]).astype(bf16)

    return pl.pallas_call(
        body, name=name, grid=(t // tm,),
        in_specs=[pl.BlockSpec((tm, 2 * DC), lambda i: (i, 0)),
                  pl.BlockSpec((HALO, 2 * DC), lambda i: (jnp.maximum(i * hb - 1, 0), 0)),
                  _full((CW, DC)), _full((1, DC)), _full((1, DC)), _full((1, DC))],
        out_specs=[pl.BlockSpec((tm, DC), lambda i: (i, 0)), pl.BlockSpec((tm, DC), lambda i: (i, 0))],
        out_shape=[_sds((t, DC)), _sds((t, DC), bf16)],
        scratch_shapes=[pltpu.VMEM((8, HALO + tm, DC), f32)],
    )(pc, pc, dw, cb, lnw, lnb)


def _conv_bwd1(dcat, c, lnw, lnb, name):
    t = c.shape[0]
    tm = _tile(t, TM_VPU)

    def body(d_ref, c_ref, w_ref, b_ref, dc_ref, dw_ref, db_ref, dcb_ref):
        _, vjp = jax.vjp(_ln_silu, c_ref[...], w_ref[...], b_ref[...])
        dc, dw, db = vjp(d_ref[...])
        dc_ref[...] = dc
        dcb = jnp.sum(dc, axis=0, keepdims=True)
        for ref, gval in ((dw_ref, dw), (db_ref, db), (dcb_ref, dcb)):
            @pl.when(pl.program_id(0) == 0)
            def _(ref=ref, gval=gval):
                ref[...] = gval

            @pl.when(pl.program_id(0) != 0)
            def _(ref=ref, gval=gval):
                ref[...] += gval

    tk = pl.BlockSpec((tm, DC), lambda i: (i, 0))
    return pl.pallas_call(
        body, name=name, grid=(t // tm,),
        in_specs=[pl.BlockSpec((tm, DC), lambda i: (i, 1)), tk, _full((1, DC)), _full((1, DC))],
        out_specs=[tk] + [_full((1, DC))] * 3,
        out_shape=[_sds((t, DC))] + [_sds((1, DC))] * 3,
    )(dcat, c, lnw, lnb)


def _conv_bwd2(dc, pc, dw, name):
    t = pc.shape[0]
    tm = _tile(t, TM_VPU)
    hb = tm // HALO
    last_blk = t // HALO - 1

    def body(dc_ref, dn_ref, pc_ref, halo_ref, dw_ref, dpc_ref, ddw_ref, s, sd, acc):
        i = pl.program_id(0)
        _fill_glu(s.at[0], pc_ref, halo_ref, i == 0, tm)
        sd[0, pl.ds(0, tm), :] = dc_ref[...]
        sd[0, pl.ds(tm, HALO), :] = jnp.where(i == pl.num_programs(0) - 1, 0.0, dn_ref[...])
        _phase_copies(s)
        _phase_copies(sd)
        for r0 in range(0, tm, CONV_ROWS):
            rows = pl.ds(r0, CONV_ROWS)
            dcb = sd[0, rows, :]
            dglu = None
            for j in range(CW):
                term = _rows_at(sd, r0 + CW - 1 - j, CONV_ROWS) * dw_ref[pl.ds(j, 1), :]
                dglu = term if dglu is None else dglu + term
                part = dcb * _rows_at(s, r0 + HALO - CW + 1 + j, CONV_ROWS)
                part8 = part[0:8]
                for q in range(8, CONV_ROWS, 8):
                    part8 = part8 + part[q:q + 8]
                if r0 == 0:
                    acc[pl.ds(8 * j, 8), :] = part8
                else:
                    acc[pl.ds(8 * j, 8), :] += part8
            a = pc_ref[rows, :DC]
            sg = jax.nn.sigmoid(pc_ref[rows, DC:])
            dpc_ref[rows, :DC] = dglu * sg
            dpc_ref[rows, DC:] = dglu * a * sg * (1.0 - sg)
        rows = [jnp.sum(acc[pl.ds(8 * j, 8), :], axis=0, keepdims=True) for j in range(CW)]

        @pl.when(i == 0)
        def _():
            for j in range(CW):
                ddw_ref[pl.ds(j, 1), :] = rows[j]

        @pl.when(i != 0)
        def _():
            for j in range(CW):
                ddw_ref[pl.ds(j, 1), :] += rows[j]

    return pl.pallas_call(
        body, name=name, grid=(t // tm,),
        in_specs=[pl.BlockSpec((tm, DC), lambda i: (i, 0)),
                  pl.BlockSpec((HALO, DC), lambda i: (jnp.minimum((i + 1) * hb, last_blk), 0)),
                  pl.BlockSpec((tm, 2 * DC), lambda i: (i, 0)),
                  pl.BlockSpec((HALO, 2 * DC), lambda i: (jnp.maximum(i * hb - 1, 0), 0)),
                  _full((CW, DC))],
        out_specs=[pl.BlockSpec((tm, 2 * DC), lambda i: (i, 0)), _full((CW, DC))],
        out_shape=[_sds((t, 2 * DC)), _sds((CW, DC))],
        scratch_shapes=[pltpu.VMEM((8, HALO + tm, DC), f32), pltpu.VMEM((8, tm + HALO, DC), f32),
                        pltpu.VMEM((8 * CW, DC), f32)],
    )(dc, dc, pc, pc, dw)


def _adamw(w, g, m, v):
    m = ADAM_B1 * m + (1.0 - ADAM_B1) * g
    v = ADAM_B2 * v + (1.0 - ADAM_B2) * (g * g)
    m_hat = m / (1.0 - ADAM_B1 ** ADAM_STEP)
    v_hat = v / (1.0 - ADAM_B2 ** ADAM_STEP)
    delta = -ADAM_LR * (m_hat / (jnp.sqrt(v_hat) + ADAM_EPS) + ADAM_WD * w)
    return delta, m, v


def _adam_slots(w, slots, m, v, name):
    r, c = w.shape
    tr = next(cand for cand in (512, 352, 256, 128, r) if r % cand == 0)

    def body(w_ref, s_ref, m_ref, v_ref, g_ref, d_ref, nm_ref, nv_ref):
        g = s_ref[0].astype(f32)
        for k in range(1, NDEV):
            g = g + s_ref[k].astype(f32)
        delta, nm, nv = _adamw(w_ref[...], g, m_ref[...], v_ref[...])
        g_ref[...] = g
        d_ref[...] = delta
        nm_ref[...] = nm
        nv_ref[...] = nv

    blk = pl.BlockSpec((tr, c), lambda i: (i, 0))
    return pl.pallas_call(
        body, name=name, grid=(r // tr,),
        in_specs=[blk, pl.BlockSpec((NDEV, tr, c), lambda i: (0, i, 0)), blk, blk],
        out_specs=[blk] * 4, out_shape=[_sds((r, c))] * 4,
    )(w, slots, m, v)


def _sum_slots(slots, name):
    _, r, c = slots.shape

    def body(s_ref, o_ref):
        g = s_ref[0]
        for k in range(1, NDEV):
            g = g + s_ref[k]
        o_ref[...] = g

    return pl.pallas_call(body, name=name, in_specs=[_full((NDEV, r, c))], out_specs=_full((r, c)),
                          out_shape=_sds((r, c)), grid=(1,))(slots)


def _adam_many(ws, gs, ms, vs, name):
    n = len(ws)
    shapes = [w.shape for w in ws]

    def body(*refs):
        for i in range(n):
            w_ref, g_ref, m_ref, v_ref = refs[i], refs[n + i], refs[2 * n + i], refs[3 * n + i]
            delta, nm, nv = _adamw(w_ref[...], g_ref[...], m_ref[...], v_ref[...])
            refs[4 * n + i][...] = delta
            refs[5 * n + i][...] = nm
            refs[6 * n + i][...] = nv

    outs = pl.pallas_call(body, name=name, grid=(1,), in_specs=[_full(s) for s in shapes] * 4,
                          out_specs=[_full(s) for s in shapes] * 3, out_shape=[_sds(s) for s in shapes] * 3,
                          )(*ws, *gs, *ms, *vs)
    return outs[:n], outs[n:2 * n], outs[2 * n:]


def _pack(pieces, total):
    flat = []
    n = 0
    for p in pieces:
        p = p.reshape(-1)
        pad = (-p.shape[0]) % LANES
        if pad:
            p = jnp.pad(p, (0, pad))
        flat.append(p)
        n += p.shape[0]
    if total > n:
        flat.append(jnp.zeros((total - n,), f32))
    return jnp.concatenate(flat).reshape(total // LANES, LANES)


def _unpack(vec, shapes):
    flat = vec.reshape(-1)
    out, off = [], 0
    for s in shapes:
        n = math.prod(s)
        out.append(flat[off:off + n].reshape(s))
        off += n + (-n) % LANES
    return out


def _round_up(n, m):
    return (n + m - 1) // m * m


def kernel(x, ffn1_norm_pre, ffn1_norm_post, ffn1_w_gu, ffn1_w_down, mix_norm_pre, mix_norm_post, w_in, shift_mu, w_up, w0, a_up, a0, g_up, k_k, k_a, r_k, gn_w, gn_b, conv_dw, conv_b, conv_ln_w, conv_ln_b, w_out, ffn2_norm_pre, ffn2_norm_post, ffn2_w_gu, ffn2_w_down, loss_target, m_ffn1_norm_pre, m_ffn1_norm_post, m_ffn1_w_gu, m_ffn1_w_down, m_mix_norm_pre, m_mix_norm_post, m_w_in, m_shift_mu, m_w_up, m_w0, m_a_up, m_a0, m_g_up, m_k_k, m_k_a, m_r_k, m_gn_w, m_gn_b, m_conv_dw, m_conv_b, m_conv_ln_w, m_conv_ln_b, m_w_out, m_ffn2_norm_pre, m_ffn2_norm_post, m_ffn2_w_gu, m_ffn2_w_down, v_ffn1_norm_pre, v_ffn1_norm_post, v_ffn1_w_gu, v_ffn1_w_down, v_mix_norm_pre, v_mix_norm_post, v_w_in, v_shift_mu, v_w_up, v_w0, v_a_up, v_a0, v_g_up, v_k_k, v_k_a, v_r_k, v_gn_w, v_gn_b, v_conv_dw, v_conv_b, v_conv_ln_w, v_conv_ln_b, v_w_out, v_ffn2_norm_pre, v_ffn2_norm_post, v_ffn2_w_gu, v_ffn2_w_down):
    t = x.shape[1]
    x0 = x.reshape(t, D)
    tgt = loss_target.reshape(t, D)

    def shard(a):
        return a[0].astype(bf16)

    def shard_t(a):
        return jnp.swapaxes(a[0], 0, 1).astype(bf16)

    (wg1,) = _exchange([shard_t(ffn1_w_gu)], ["gather"], [SLOT42], "gather_wg1")

    def cols(a):
        return jnp.transpose(a, (1, 0, 2)).reshape(a.shape[1], NDEV * a.shape[2])

    lane = jnp.arange(DR, dtype=jnp.int32) // HN
    seg = (lane[:, None] == jnp.arange(LANES, dtype=jnp.int32)[None, :]).astype(f32)
    seg_t = seg.T
    rk_row = r_k.reshape(1, DR)
    post_params = (gn_w, gn_b, rk_row, seg, seg_t)

    (xn1, gu1, h1), (wd1, win_g) = _ffn_in_fwd(x0, ffn1_norm_pre, wg1, "ffn1_in_fwd", comm=(
        [shard(ffn1_w_down), shard_t(w_in)], ["gather"] * 2, [SLOT8] * 2))
    wd1 = wd1.reshape(4, FB, D)
    win_t = win_g.reshape(DIN, D)
    (f1, x1), (wout_g, wup_g, aup_g, gup_g, cdw_g) = _mm_norm_res_fwd(
        h1, wd1, x0, ffn1_norm_post, 0.5, "ffn1_out_fwd", comm=(
            [shard(w_out), w_up[0], a_up[0], g_up[0], conv_dw[0]], ["gather"] * 5, [SLOT8] * 5))
    wout_full = wout_g.reshape(1, D, D)
    wup_full, aup_full, gup_full, cdw_full = cols(wup_g), cols(aup_g), cols(gup_g), cols(cdw_g)
    zeros64 = jnp.zeros((HN, DR), f32)
    wup_pad = jnp.concatenate([wup_full, zeros64], axis=0)
    aup_pad = jnp.concatenate([zeros64, aup_full], axis=0)
    prep_params = (shift_mu, w0, a0, k_k, k_a, wup_pad, aup_pad, gup_full, seg, seg_t)
    hm, ps, pc = _norm_mm_fwd(x1, mix_norm_pre, win_t, "mix_in_fwd")
    rec_in, (wd2,) = _prep_fwd(ps, prep_params, "prep_fwd", comm=([shard(ffn2_w_down)], ["gather"], [SLOT8]))
    wd2 = wd2.reshape(4, FB, D)
    g_gate = rec_in[6]
    rec_in = rec_in[:6]
    (a_c, g_c, p_c, q_c, yl_c), (wg2,) = _r1_fwd(rec_in, "r1_fwd", comm=([shard_t(ffn2_w_gu)], ["gather"], [SLOT42]))
    y_rec, s_all = _r2_fwd(a_c, g_c, q_c, yl_c, "r2_fwd")
    c_conv, out_b = _conv_fwd(pc, cdw_full, conv_b, conv_ln_w, conv_ln_b, "conv_fwd")
    cat = _post_fwd(y_rec, rec_in[0], rec_in[2], rec_in[3], g_gate, out_b, post_params, "post_fwd")
    fm, x2 = _mm_norm_res_fwd(cat.reshape(1, t, D), wout_full, x1, mix_norm_post, 1.0, "mix_out_fwd")
    (xn2, gu2, h2), _ = _ffn_in_fwd(x2, ffn2_norm_pre, wg2, "ffn2_in_fwd")
    f2, dy, loss_part = _mm_norm_res_fwd(h2, wd2, x2, ffn2_norm_post, 0.5, "ffn2_out_fwd", tgt=tgt)

    (df2, dg_post2, dgu2), _ = _post_bwd_nt(dy, f2, ffn2_norm_post, wd2, 0.5, gu2, "ffn2_out_bwd")
    (dx2, dg_pre2), _ = _nt_pre_bwd(dgu2, wg2, x2, ffn2_norm_pre, dy, "ffn2_in_bwd")
    dwd2 = _wgrad(h2, df2.reshape(1, t, D), True, False, D, "ffn2_wd_grad")
    dwg2 = _wgrad(dgu2.reshape(8, t, FB), xn2.reshape(1, t, D), True, False, D, "ffn2_wg_grad")
    (dfm, dg_postm, dcat), _ = _post_bwd_nt(dx2, fm, mix_norm_post, wout_full, 1.0, None, "mix_out_bwd")
    dwout = _wgrad(cat.reshape(1, t, D), dfm.reshape(1, t, D), False, False, D, "wout_grad")
    dcat = dcat.reshape(t, D)
    (dy_rec, dr2, dkp2, dv2, dgate, dgn_w, dgn_b, drk), (s_wout,) = _post_bwd(
        dcat, y_rec, rec_in[0], rec_in[2], rec_in[3], g_gate, post_params, "post_bwd", comm=(
            [dwout.reshape(NDEV, D // NDEV, D)], ["scatter"], [SLOT8]))
    dq_c, da_c, dg_c = _r2_bwd(dy_rec, q_c, s_all, a_c, "r2_bwd")
    rec_grads, got = _r1_bwd(rec_in, p_c, da_c, dg_c, dq_c, dy_rec, "r1_bwd", comm=(
        [dwg2.reshape(4, 2, FB, D), dwd2.reshape(NDEV, F // NDEV, D)], ["scatter"] * 2, [SLOT42, SLOT8]))
    s_wg2, s_wd2 = got
    prep_out = _prep_bwd(ps, prep_params, rec_grads, (dr2, dkp2, dv2, dgate), "prep_bwd")
    dpsl, dw0, da0, dkk, dka, dwup_pad, daup_pad, dgup = prep_out
    dc_conv, dlnw, dlnb, dcb = _conv_bwd1(dcat, c_conv, conv_ln_w, conv_ln_b, "conv_bwd1")
    dpc, dcdw = _conv_bwd2(dc_conv, pc, cdw_full, "conv_bwd2")
    dp, dmu = _shift_bwd(dpsl, ps, shift_mu, dpc, "shift_bwd")
    (dx1, dg_prem), _ = _nt_pre_bwd(dp.reshape(1, 1, t, DIN), win_t.reshape(1, 1, DIN, D), x1, mix_norm_pre, dx2,
                                    "mix_in_bwd")
    dwin_s = _wgrad(dp.reshape(1, t, DIN), hm.reshape(1, t, D), False, False, D // 2, "win_grad").reshape(
        NDEV, DIN // NDEV, D)
    (df1, dg_post1, dgu1), (s_win,) = _post_bwd_nt(dx1, f1, ffn1_norm_post, wd1, 0.5, gu1, "ffn1_out_bwd", comm=(
        [dwin_s], ["scatter"], [SLOT8]))
    dwd1 = _wgrad(h1, df1.reshape(1, t, D), True, False, D, "ffn1_wd_grad")
    dwg1, (s_wd1,) = _wgrad(dgu1.reshape(8, t, FB), xn1.reshape(1, t, D), True, False, D, "ffn1_wg_grad", comm=(
        [dwd1.reshape(NDEV, F // NDEV, D)], ["scatter"], [SLOT8]))
    wg1_sends, wg1_recvs, dwg1_thru, wg1_land, started = _scatter_start(dwg1.reshape(4, 2, FB, D), SLOT42, "wg1_scatter_start")
    (dx0, dg_pre1), _ = _nt_pre_bwd(dgu1, wg1, x0, ffn1_norm_pre + started[:1, :1], dx1, "ffn1_in_bwd")

    res = {}

    def adam_big(nm, w, s, m, v, transposed):
        view = (lambda a: jnp.swapaxes(a[0], 0, 1)) if transposed else (lambda a: a[0])
        outs = _adam_slots(view(w), s, view(m), view(v), "adam_" + nm)
        done.append(outs[1])
        res[nm] = [(jnp.swapaxes(o, 0, 1) if transposed else o)[None] for o in outs]

    done = []

    adam_big("ffn2_w_gu", ffn2_w_gu, s_wg2.reshape(NDEV, FB, D), m_ffn2_w_gu, v_ffn2_w_gu, True)
    adam_big("ffn2_w_down", ffn2_w_down, s_wd2, m_ffn2_w_down, v_ffn2_w_down, False)
    adam_big("w_in", w_in, s_win, m_w_in, v_w_in, True)
    adam_big("w_out", w_out, s_wout, m_w_out, v_w_out, False)
    adam_big("ffn1_w_down", ffn1_w_down, s_wd1, m_ffn1_w_down, v_ffn1_w_down, False)
    _, s_wg1 = _scatter_wait(wg1_sends, wg1_recvs, dwg1_thru, wg1_land, [dg_pre1] + done, SLOT42, "wg1_scatter_wait")
    my_slot = SLOT42[1](4 * lax.axis_index("x") + 2 * lax.axis_index("y") + lax.axis_index("c"))
    own = lax.dynamic_slice(dwg1.reshape(4, 2, FB, D), (*my_slot, 0, 0), (1, 1, FB, D))
    s_wg1 = lax.dynamic_update_slice(s_wg1, own, (*my_slot, 0, 0))
    rep_shapes = [(1, 1)] + [(1, D)] * 6 + [(1, DS)] + [(1, DR)] * 10
    rep_parts = [loss_part, dg_pre1, dg_post1, dg_prem, dg_postm, dg_pre2, dg_post2, dmu,
                 dw0, da0, dkk, dka, drk, dgn_w, dgn_b, dcb, dlnw, dlnb]
    n_rep = _round_up(sum(_round_up(math.prod(s), LANES) for s in rep_shapes), 8 * LANES)
    sh_shapes = [(HN, HN), (HN, HN), (LANES, HN), (CW, HN)]
    n_sh = _round_up(sum(_round_up(math.prod(s), LANES) for s in sh_shapes), 8 * LANES)
    sh_parts = []
    for a in (dwup_pad[:HN], daup_pad[HN:], dgup, dcdw):
        rows = jnp.transpose(a.reshape(a.shape[0], NDEV, HN), (1, 0, 2)).reshape(NDEV, a.shape[0] * HN)
        sh_parts.append(jnp.pad(rows, ((0, 0), (0, (-rows.shape[1]) % LANES))))
    sh_vec = jnp.concatenate(sh_parts, axis=1)
    sh_vec = jnp.pad(sh_vec, ((0, 0), (0, n_sh - sh_vec.shape[1]))).reshape(NDEV, n_sh // LANES, LANES)
    rep_vec, sh_vec, s_wg1 = lax.optimization_barrier((_pack(rep_parts, n_rep), sh_vec, s_wg1))
    s_rep, s_sh = _exchange([rep_vec, sh_vec], ["gather", "scatter"], [SLOT8, SLOT8], "exchange_small")
    adam_big("ffn1_w_gu", ffn1_w_gu, s_wg1.reshape(NDEV, FB, D), m_ffn1_w_gu, v_ffn1_w_gu, True)


    gsum = _unpack(_sum_slots(s_rep, "sum_rep"), rep_shapes)
    gsh = _unpack(_sum_slots(s_sh, "sum_sharded"), sh_shapes)
    loss = gsum[0].reshape(())
    rep_names = ["ffn1_norm_pre", "ffn1_norm_post", "mix_norm_pre", "mix_norm_post", "ffn2_norm_pre", "ffn2_norm_post",
                 "shift_mu", "w0", "a0", "k_k", "k_a", "r_k", "gn_w", "gn_b", "conv_b", "conv_ln_w", "conv_ln_b"]
    shard_names = ["w_up", "a_up", "g_up", "conv_dw"]
    env = dict(locals())
    small_g = {n: gsum[1 + i] for i, n in enumerate(rep_names)}
    small_g["r_k"] = small_g["r_k"].reshape(1, NH, HN)
    for i, n in enumerate(shard_names):
        small_g[n] = gsh[i][None]
    names = rep_names + shard_names
    shapes = [env[n].shape for n in names]

    def flat2(a):
        return a.reshape(1, DR) if a.shape == (1, NH, HN) else a.reshape(a.shape[-2:])

    deltas, new_ms, new_vs = _adam_many(*[[flat2(d[n]) for n in names] for d in (
        {n: env[n] for n in names}, small_g, {n: env["m_" + n] for n in names}, {n: env["v_" + n] for n in names})],
        "adam_small")
    for n, shp, dl, nm_, nv_ in zip(names, shapes, deltas, new_ms, new_vs):
        res[n] = [small_g[n], dl.reshape(shp), nm_.reshape(shp), nv_.reshape(shp)]

    order = ["ffn1_norm_pre", "ffn1_norm_post", "ffn1_w_gu", "ffn1_w_down", "mix_norm_pre", "mix_norm_post", "w_in",
             "shift_mu", "w_up", "w0", "a_up", "a0", "g_up", "k_k", "k_a", "r_k", "gn_w", "gn_b", "conv_dw", "conv_b",
             "conv_ln_w", "conv_ln_b", "w_out", "ffn2_norm_pre", "ffn2_norm_post", "ffn2_w_gu", "ffn2_w_down"]
    return (loss, dx0.reshape(1, t, D), *[res[n][0] for n in order], *[res[n][1] for n in order],
            *[res[n][2] for n in order], *[res[n][3] for n in order])
```

```python
import functools
import math

import jax
import jax.numpy as jnp
from jax import lax
from jax.experimental import pallas as pl
from jax.experimental.pallas import tpu as pltpu

f32 = jnp.float32
bf16 = jnp.bfloat16

D = 1024
F = 2816
FB = 704
DR = 512
DC = 512
NH = 8
HN = 64
DS = 1792
DIN = 2816
CW = 31
CHUNK = 64
R1_CHUNKS = 4
R2_CHUNKS = 8
NDEV = 8
RMS_EPS = 1e-6
GN_EPS = 64e-5
LN_EPS = 1e-5
DECAY_SCALE = math.exp(-0.5)
ADAM_LR, ADAM_B1, ADAM_B2, ADAM_EPS, ADAM_WD, ADAM_STEP = 0.001, 0.9, 0.999, 1e-08, 0.01, 10
LANES = 128
HALO = 32
CONV_ROWS = 32
TM_MXU = 1024
TM_SUB = 256
TM_VPU = 256

ANY = pl.BlockSpec(memory_space=pl.ANY)


def _full(shape):
    return pl.BlockSpec(shape, lambda *_: (0,) * len(shape))


def _sds(shape, dtype=f32):
    return jax.ShapeDtypeStruct(shape, dtype)


def _dot(a, b):
    return jnp.dot(a, b, preferred_element_type=f32)


def _dot_nt(a, b):
    return lax.dot_general(a, b, (((1,), (1,)), ((), ())), preferred_element_type=f32)


def _dot_tn(a, b):
    return lax.dot_general(a, b, (((0,), (0,)), ((), ())), preferred_element_type=f32)


def _terms(a, n):
    out, rem = [], a
    for i in range(n):
        t = rem.astype(bf16)
        out.append(t)
        if i + 1 < n:
            rem = rem - t.astype(f32)
    return out


def _mm_raw(a, b, kind, mode):
    nb = a.ndim - 2
    bd = tuple(range(nb))
    ca = nb if kind == "tn" else nb + 1
    cb = nb + 1 if kind == "nt" else nb
    dn = (((ca,), (cb,)), (bd, bd))
    pa, pb = mode[:2]
    ta, tb = _terms(a, pa), _terms(b, pb)
    acc = None
    for i in range(pa):
        for j in range(pb):
            if i + j < max(pa, pb):
                p = lax.dot_general(ta[i], tb[j], dn, preferred_element_type=f32)
                acc = p if acc is None else acc + p
    return acc


@functools.partial(jax.custom_vjp, nondiff_argnums=(2, 3))
def mm(a, b, kind, mode):
    return _mm_raw(a, b, kind, mode)


def _mm_fwd(a, b, kind, mode):
    return _mm_raw(a, b, kind, mode), (a, b)


def _ct_terms(keep):
    return EXACT_TERMS if keep == 1 else keep


def _mm_bwd(kind, mode, res, ct):
    a, b = res
    pa, pb = mode[:2]
    if len(mode) == 4:
        ca = cb = mode[2]
        pa = pb = mode[3]
    elif mode == (1, 1):
        ca = cb = 1
    else:
        ca, cb = _ct_terms(pb), _ct_terms(pa)
    if kind == "nn":
        da = mm(ct, b, "nt", (ca, pb))
        db = mm(a, ct, "tn", (pa, cb))
    elif kind == "nt":
        da = mm(ct, b, "nn", (ca, pb))
        db = mm(ct, a, "tn", (cb, pa))
    else:
        da = mm(b, ct, "nt", (pb, ca))
        db = mm(a, ct, "nn", (pa, cb))
    return da, db


mm.defvjp(_mm_fwd, _mm_bwd)

X3 = (2, 2)
EXACT_TERMS = 2
EA = (1, EXACT_TERMS)
EB = (EXACT_TERMS, 1)
X1 = (1, 1)
RGM = (1, 1, 1, 2)
RIM = X1
RWM = X1
RAM = X1
R2M = X1
LORA = X1


def _rms_fwd(x, g):
    inv = lax.rsqrt(jnp.mean(x * x, axis=-1, keepdims=True) + RMS_EPS)
    return x * inv * g


def _rms_bwd(x, g, dy):
    inv = lax.rsqrt(jnp.mean(x * x, axis=-1, keepdims=True) + RMS_EPS)
    xh = x * inv
    dxh = dy * g
    dg = jnp.sum(dy * xh, axis=0, keepdims=True)
    dx = inv * (dxh - xh * jnp.mean(dxh * xh, axis=-1, keepdims=True))
    return dx, dg


def _tile(t, want):
    return min(t, want)


NPEER = NDEV - 1


def _comm_out_shape(arrays, modes, slots):
    out = []
    for a, mode, (lead, _) in zip(arrays, modes, slots):
        shp = tuple(lead) + tuple(a.shape) if mode == "gather" else tuple(a.shape)
        out.append(pltpu.HBM(shp, a.dtype))
    return out


def _comm_sems(n):
    return [pltpu.SemaphoreType.DMA((n * NPEER,)), pltpu.SemaphoreType.DMA((n * NPEER,)), pltpu.SemaphoreType.DMA((n,))]


RELAYED = (3, 5, 7)


def _comm_copies(ins, outs, sems, modes, slots, want):
    n = len(ins)
    send_sems, recv_sems, loc_sems = sems
    x, y, c = lax.axis_index("x"), lax.axis_index("y"), lax.axis_index("c")
    me = 4 * x + 2 * y + c

    def peer_of(k):
        px = 1 - x if k & 4 else x
        py = 1 - y if k & 2 else y
        pc = 1 - c if k & 1 else c
        return (px, py, pc), 4 * px + 2 * py + pc

    def slot(i, dev):
        return outs[i].at[slots[i][1](dev)]

    if want == "local":
        return [pltpu.make_async_copy(ins[i] if modes[i] == "gather" else ins[i].at[slots[i][1](me)], slot(i, me),
                                      loc_sems.at[i]) for i in range(n)]
    sibling = peer_of(1)[0]
    out = []
    for k in range(1, NDEV):
        peer, pid = peer_of(k)
        for i in range(n):
            sem = dict(send_sem=send_sems.at[i * NPEER + k - 1], recv_sem=recv_sems.at[i * NPEER + k - 1],
                       device_id_type=pl.DeviceIdType.MESH)
            relayed = modes[i] == "gather" and k in RELAYED
            if want == "recv":
                src = ins[i] if modes[i] == "gather" else ins[i].at[slots[i][1](pid)]
                out.append(pltpu.make_async_remote_copy(src_ref=src, dst_ref=slot(i, pid), device_id=peer, **sem))
            elif want == "first" and not relayed:
                src = ins[i] if modes[i] == "gather" else ins[i].at[slots[i][1](pid)]
                out.append(pltpu.make_async_remote_copy(src_ref=src, dst_ref=slot(i, me), device_id=peer, **sem))
            elif want == "relay" and relayed:
                origin = peer_of(k - 1)[1]
                out.append(((k - 2) * n + i, pltpu.make_async_remote_copy(
                    src_ref=slot(i, origin), dst_ref=slot(i, origin), device_id=sibling, **sem)))
    return out


def _comm_start(ins, outs, sems, modes, slots):
    for cp in _comm_copies(ins, outs, sems, modes, slots, "local") + _comm_copies(ins, outs, sems, modes, slots, "first"):
        cp.start()


def _comm_wait(ins, outs, sems, modes, slots):
    recvs = _comm_copies(ins, outs, sems, modes, slots, "recv")
    relays = _comm_copies(ins, outs, sems, modes, slots, "relay")
    for idx, cp in relays:
        recvs[idx].wait_recv()
        cp.start()
    passed_on = {idx for idx, _ in relays}
    for idx, cp in enumerate(recvs):
        if idx not in passed_on:
            cp.wait_recv()
    for cp in _comm_copies(ins, outs, sems, modes, slots, "first") + [cp for _, cp in relays]:
        cp.wait_send()
    for cp in _comm_copies(ins, outs, sems, modes, slots, "local"):
        cp.wait()


def _exchange(arrays, modes, slots, name):
    n = len(arrays)

    def body(*refs):
        ins, outs, sems = refs[:n], refs[n:2 * n], refs[2 * n:]
        _comm_start(ins, outs, sems, modes, slots)
        _comm_wait(ins, outs, sems, modes, slots)

    return pl.pallas_call(
        body, name=name, out_shape=_comm_out_shape(arrays, modes, slots),
        in_specs=[ANY] * n, out_specs=[ANY] * n, scratch_shapes=_comm_sems(n),
    )(*arrays)


HBM_SPEC = pl.BlockSpec(memory_space=pltpu.HBM)
SEM_SPEC = pl.BlockSpec(memory_space=pltpu.SEMAPHORE)


def _scatter_start(arr, slot, name):
    def body(v_ref, land_ref, send_sems, recv_sems, v_thru, land_thru, token):
        for cp in _comm_copies([v_ref], [land_ref], (send_sems, recv_sems, None), ["scatter"], [slot], "first"):
            cp.start()
        token[...] = jnp.zeros_like(token)

    return pl.pallas_call(
        body, name=name,
        out_shape=(pltpu.SemaphoreType.DMA((NPEER,)), pltpu.SemaphoreType.DMA((NPEER,)),
                   pltpu.HBM(arr.shape, arr.dtype), pltpu.HBM(arr.shape, arr.dtype), _sds((8, LANES))),
        in_specs=(HBM_SPEC, HBM_SPEC),
        out_specs=(SEM_SPEC, SEM_SPEC, HBM_SPEC, HBM_SPEC, pl.BlockSpec(memory_space=pltpu.VMEM)),
        input_output_aliases={0: 2, 1: 3},
        compiler_params=pltpu.CompilerParams(has_side_effects=pltpu.SideEffectType.DATAFLOW_SIDE_EFFECTING),
    )(pltpu.with_memory_space_constraint(arr, pltpu.HBM),
      pltpu.with_memory_space_constraint(lax.empty(arr.shape, arr.dtype), pltpu.HBM))


def _scatter_wait(send_sems, recv_sems, v_thru, land_thru, after, slot, name):
    def body(v_ref, land_ref, send_sems, recv_sems, *rest):
        sems = (send_sems, recv_sems, None)
        for cp in _comm_copies([v_ref], [land_ref], sems, ["scatter"], [slot], "first"):
            cp.wait_send()
        for cp in _comm_copies([v_ref], [land_ref], sems, ["scatter"], [slot], "recv"):
            cp.wait_recv()

    return pl.pallas_call(
        body, name=name, out_shape=(pltpu.HBM(v_thru.shape, v_thru.dtype), pltpu.HBM(v_thru.shape, v_thru.dtype)),
        in_specs=(HBM_SPEC, HBM_SPEC, SEM_SPEC, SEM_SPEC) + (ANY,) * len(after), out_specs=(HBM_SPEC, HBM_SPEC),
        input_output_aliases={0: 0, 1: 1},
        compiler_params=pltpu.CompilerParams(has_side_effects=pltpu.SideEffectType.DATAFLOW_SIDE_EFFECTING),
    )(v_thru, land_thru, send_sems, recv_sems, *after)


def _pcall(body, *, name, grid, in_specs, out_specs, out_shape, args, scratch_shapes=(), comm=None):
    if comm is None:
        return pl.pallas_call(body, name=name, grid=grid, in_specs=in_specs, out_specs=out_specs, out_shape=out_shape,
                              scratch_shapes=list(scratch_shapes))(*args), None
    arrays, modes, slots = comm
    n_in, n_out, n_scr, nc = len(args), len(out_shape), len(scratch_shapes), len(arrays)

    def hosted(*refs):
        a_in, c_in = refs[:n_in], refs[n_in:n_in + nc]
        o = n_in + nc
        a_out, c_out = refs[o:o + n_out], refs[o + n_out:o + n_out + nc]
        o += n_out + nc
        a_scr, sems = refs[o:o + n_scr], refs[o + n_scr:]
        first = pl.program_id(0) == 0
        last = pl.program_id(0) == grid[0] - 1
        for ax in range(1, len(grid)):
            first = jnp.logical_and(first, pl.program_id(ax) == 0)
            last = jnp.logical_and(last, pl.program_id(ax) == grid[ax] - 1)

        @pl.when(first)
        def _():
            _comm_start(c_in, c_out, sems, modes, slots)

        body(*a_in, *a_out, *a_scr)

        @pl.when(last)
        def _():
            _comm_wait(c_in, c_out, sems, modes, slots)

    res = pl.pallas_call(
        hosted, name=name, grid=grid, in_specs=list(in_specs) + [ANY] * nc, out_specs=list(out_specs) + [ANY] * nc,
        out_shape=list(out_shape) + _comm_out_shape(arrays, modes, slots),
        scratch_shapes=list(scratch_shapes) + _comm_sems(nc),
    )(*args, *arrays)
    return res[:n_out], res[n_out:]


SLOT8 = ((NDEV,), lambda d: (d,))
SLOT42 = ((4, 2), lambda d: (lax.rem(d, 4), lax.div(d, 4)))


def _ffn_in_fwd(x, g, wg4, name, comm=None):
    t = x.shape[0]
    tm = _tile(t, TM_MXU)
    sub = _tile(tm, TM_SUB)

    def body(x_ref, g_ref, w_ref, xn_ref, gu_ref, h_ref):
        for s in range(tm // sub):
            rows = pl.ds(s * sub, sub)
            xn = _rms_fwd(x_ref[rows, :], g_ref[...]).astype(bf16)
            xn_ref[rows, :] = xn
            gate = _dot_nt(xn, w_ref[0, 0])
            up = _dot_nt(xn, w_ref[0, 1])
            gu_ref[0, 0, rows, :] = gate.astype(bf16)
            gu_ref[0, 1, rows, :] = up.astype(bf16)
            h_ref[0, rows, :] = (gate * jax.nn.sigmoid(gate) * up).astype(bf16)

    return _pcall(
        body, name=name, grid=(t // tm, 4),
        in_specs=[pl.BlockSpec((tm, D), lambda i, e: (i, 0)), _full((1, D)),
                  pl.BlockSpec((1, 2, FB, D), lambda i, e: (e, 0, 0, 0))],
        out_specs=[pl.BlockSpec((tm, D), lambda i, e: (i, 0)),
                   pl.BlockSpec((1, 2, tm, FB), lambda i, e: (e, 0, i, 0)),
                   pl.BlockSpec((1, tm, FB), lambda i, e: (e, i, 0))],
        out_shape=[_sds((t, D), bf16), _sds((4, 2, t, FB), bf16), _sds((4, t, FB), bf16)],
        args=(x, g, wg4), comm=comm)


def _mm_norm_res_fwd(hb, wb, xres, g, scale, name, comm=None, tgt=None):
    e_n, t, k = hb.shape
    tm = _tile(t, TM_MXU)

    def body(h_ref, w_ref, x_ref, g_ref, *rest):
        f = _dot(h_ref[0], w_ref[0])
        for e in range(1, e_n):
            f = f + _dot(h_ref[e], w_ref[e])
        xnew = x_ref[...] + scale * _rms_fwd(f, g_ref[...])
        if tgt is None:
            f_ref, o_ref = rest
            o_ref[...] = xnew
        else:
            t_ref, f_ref, o_ref, l_ref = rest
            err = xnew - t_ref[...]
            o_ref[...] = err * (1.0 / D)
            part = 0.5 * jnp.sum(jnp.mean(err * err, axis=-1, keepdims=True), axis=0, keepdims=True)

            @pl.when(pl.program_id(0) == 0)
            def _():
                l_ref[...] = part

            @pl.when(pl.program_id(0) != 0)
            def _():
                l_ref[...] += part
        f_ref[...] = f

    tile = pl.BlockSpec((tm, D), lambda i: (i, 0))
    outs, got = _pcall(
        body, name=name, grid=(t // tm,),
        in_specs=[pl.BlockSpec((e_n, tm, k), lambda i: (0, i, 0)), _full((e_n, k, D)), tile, _full((1, D))]
        + ([] if tgt is None else [tile]),
        out_specs=[tile, tile] + ([] if tgt is None else [_full((1, 1))]),
        out_shape=[_sds((t, D)), _sds((t, D))] + ([] if tgt is None else [_sds((1, 1))]),
        args=(hb, wb, xres, g) + (() if tgt is None else (tgt,)), comm=comm)
    return outs if comm is None else (outs, got)


def _post_bwd_nt(dxn, f, g, wb, scale, gu, name, comm=None):
    e_n, k, _ = wb.shape
    t = f.shape[0]
    tm = _tile(t, TM_MXU)
    sub = _tile(tm, TM_SUB)
    swiglu = gu is not None

    def body(*refs):
        if swiglu:
            dx_ref, f_ref, g_ref, w_ref, gu_ref, df_ref, dg_ref, dh_ref, df_s = refs
        else:
            dx_ref, f_ref, g_ref, w_ref, df_ref, dg_ref, dh_ref, df_s = refs
        i, e = pl.program_id(0), pl.program_id(1)

        @pl.when(e == 0)
        def _():
            df, dg = _rms_bwd(f_ref[...], g_ref[...], scale * dx_ref[...])
            df_s[...] = df.astype(bf16)
            df_ref[...] = df_s[...]

            @pl.when(i == 0)
            def _():
                dg_ref[...] = dg

            @pl.when(i != 0)
            def _():
                dg_ref[...] += dg

        for s in range(tm // sub):
            rows = pl.ds(s * sub, sub)
            dh = _dot_nt(df_s[rows, :], w_ref[0])
            if swiglu:
                gate = gu_ref[0, 0, rows, :].astype(f32)
                up = gu_ref[0, 1, rows, :].astype(f32)
                sg = jax.nn.sigmoid(gate)
                dh_ref[0, 0, rows, :] = (dh * up * (sg * (1.0 + gate * (1.0 - sg)))).astype(bf16)
                dh_ref[0, 1, rows, :] = (dh * gate * sg).astype(bf16)
            else:
                dh_ref[0, rows, :] = dh

    in_specs = [pl.BlockSpec((tm, D), lambda i, e: (i, 0)), pl.BlockSpec((tm, D), lambda i, e: (i, 0)), _full((1, D)),
                pl.BlockSpec((1, k, D), lambda i, e: (e, 0, 0))]
    args = [dxn, f, g, wb]
    if swiglu:
        in_specs.append(pl.BlockSpec((1, 2, tm, k), lambda i, e: (e, 0, i, 0)))
        args.append(gu)
        dh_spec = pl.BlockSpec((1, 2, tm, k), lambda i, e: (e, 0, i, 0))
        dh_shape = _sds((e_n, 2, t, k), bf16)
    else:
        dh_spec = pl.BlockSpec((1, tm, k), lambda i, e: (e, i, 0))
        dh_shape = _sds((e_n, t, k), f32)
    return _pcall(
        body, name=name, grid=(t // tm, e_n), in_specs=in_specs,
        out_specs=[pl.BlockSpec((tm, D), lambda i, e: (i, 0)), _full((1, D)), dh_spec],
        out_shape=[_sds((t, D), bf16), _sds((1, D)), dh_shape],
        scratch_shapes=[pltpu.VMEM((tm, D), bf16)], args=args, comm=comm)


def _nt_pre_bwd(dy, wb, x, g, dres, name, comm=None):
    e_n, q_n, t, k = dy.shape
    tm = _tile(t, TM_MXU)

    def body(dy_ref, w_ref, x_ref, g_ref, r_ref, dx_ref, dg_ref, acc):
        i, e = pl.program_id(0), pl.program_id(1)
        p = _dot(dy_ref[0, 0], w_ref[0, 0])
        for q in range(1, q_n):
            p = p + _dot(dy_ref[0, q], w_ref[0, q])

        @pl.when(e == 0)
        def _():
            acc[...] = p

        @pl.when(e != 0)
        def _():
            acc[...] += p

        @pl.when(e == e_n - 1)
        def _():
            dx, dg = _rms_bwd(x_ref[...], g_ref[...], acc[...])
            dx_ref[...] = r_ref[...] + dx

            @pl.when(i == 0)
            def _():
                dg_ref[...] = dg

            @pl.when(i != 0)
            def _():
                dg_ref[...] += dg

    return _pcall(
        body, name=name, grid=(t // tm, e_n),
        in_specs=[pl.BlockSpec((1, q_n, tm, k), lambda i, e: (e, 0, i, 0)),
                  pl.BlockSpec((1, q_n, k, D), lambda i, e: (e, 0, 0, 0)),
                  pl.BlockSpec((tm, D), lambda i, e: (i, 0)), _full((1, D)),
                  pl.BlockSpec((tm, D), lambda i, e: (i, 0))],
        out_specs=[pl.BlockSpec((tm, D), lambda i, e: (i, 0)), _full((1, D))],
        out_shape=[_sds((t, D)), _sds((1, D))],
        scratch_shapes=[pltpu.VMEM((tm, D), f32)], args=(dy, wb, x, g, dres), comm=comm)


def _wgrad(a, b, a_batched, b_batched, tn, name, comm=None):
    t, k = a.shape[1], a.shape[2]
    nn = b.shape[2]
    nb = max(a.shape[0], b.shape[0])
    tt = _tile(t, TM_MXU)
    nt = t // tt

    def body(a_ref, b_ref, o_ref, acc):
        @pl.when(pl.program_id(2) == 0)
        def _():
            acc[...] = _dot_tn(a_ref[0], b_ref[0])

        @pl.when(pl.program_id(2) != 0)
        def _():
            acc[...] += _dot_tn(a_ref[0], b_ref[0])

        @pl.when(pl.program_id(2) == nt - 1)
        def _():
            o_ref[0] = acc[...].astype(bf16)

    (out,), got = _pcall(
        body, name=name, grid=(nb, nn // tn, nt),
        in_specs=[pl.BlockSpec((1, tt, k), (lambda n, j, s: (n, s, 0)) if a_batched else (lambda n, j, s: (0, s, 0))),
                  pl.BlockSpec((1, tt, tn), (lambda n, j, s: (n, s, j)) if b_batched else (lambda n, j, s: (0, s, j)))],
        out_specs=[pl.BlockSpec((1, k, tn), lambda n, j, s: (n, 0, j))],
        out_shape=[_sds((nb, k, nn), bf16)],
        scratch_shapes=[pltpu.VMEM((k, tn), f32)], args=(a, b), comm=comm)
    return out if comm is None else (out, got)


def _norm_mm_fwd(x, g, w, name):
    t = x.shape[0]
    tm = _tile(t, TM_VPU)

    def body(x_ref, g_ref, w_ref, xn_ref, ps_ref, pc_ref):
        xn = _rms_fwd(x_ref[...], g_ref[...]).astype(bf16)
        xn_ref[...] = xn
        ps_ref[...] = _dot_nt(xn, w_ref[:DS, :])
        pc_ref[...] = _dot_nt(xn, w_ref[DS:, :])

    return pl.pallas_call(
        body, name=name, grid=(t // tm,),
        in_specs=[pl.BlockSpec((tm, D), lambda i: (i, 0)), _full((1, D)), _full((DIN, D))],
        out_specs=[pl.BlockSpec((tm, D), lambda i: (i, 0)), pl.BlockSpec((tm, DS), lambda i: (i, 0)),
                   pl.BlockSpec((tm, 2 * DC), lambda i: (i, 0))],
        out_shape=[_sds((t, D), bf16), _sds((t, DS)), _sds((t, 2 * DC))],
    )(x, g, w)


def _segsum_b(x, seg, seg_t):
    return mm(mm(x, seg, "nn", EB), seg_t, "nn", EB)


def _prep_fn(psl, w0, a0, k_k, k_a, wup, aup, gup, seg, seg_t):
    r, k, v = psl[:, :DR], psl[:, DR:2 * DR], psl[:, 2 * DR:3 * DR]
    xwa, xg = psl[:, 3 * DR:3 * DR + LANES], psl[:, 3 * DR + LANES:]
    d = w0 + mm(jnp.tanh(xwa), wup, "nn", LORA)
    lw = -DECAY_SCALE * jax.nn.sigmoid(d)
    a = jax.nn.sigmoid(a0 + mm(xwa, aup, "nn", LORA))
    g = mm(jax.nn.sigmoid(xg), gup, "nn", LORA)
    kkr = k * k_k
    kk = kkr * lax.rsqrt(jnp.maximum(_segsum_b(kkr * kkr, seg, seg_t), 1e-12))
    kp = k * (1.0 + (a - 1.0) * k_a)
    return r, lw, kp, v, -kk, kk * a, g


def _shifted(ps, halo_row, first):
    prev = jnp.where(first, 0.0, halo_row)
    sh = pltpu.roll(ps, 1, 0)
    row = lax.broadcasted_iota(jnp.int32, ps.shape, 0)
    return jnp.where(row == 0, prev, sh)


def _heads_of(ref, rows):
    return jnp.stack([ref[rows, h * HN:(h + 1) * HN] for h in range(NH)], axis=0)


def _heads_to(ref, rows, val):
    ref[rows, :] = jnp.concatenate([val[h] for h in range(NH)], axis=-1)


_PREP_PARAM_SHAPES = [(1, DS), (1, DR), (1, DR), (1, DR), (1, DR), (LANES, DR), (LANES, DR), (LANES, DR),
                      (DR, LANES), (LANES, DR)]


def _prep_fwd(ps, params, name, comm=None):
    t = ps.shape[0]
    tm = _tile(t, TM_VPU)
    hb = tm // 8

    def body(ps_ref, halo_ref, mu_ref, *rest):
        prm = [r[...] for r in rest[:9]]
        outs = rest[9:]
        x = ps_ref[...]
        sh = _shifted(x, halo_ref[pl.ds(7, 1), :], pl.program_id(0) == 0)
        psl = x + (sh - x) * mu_ref[...]
        vals = _prep_fn(psl, *prm)
        for ref, val in zip(outs, vals):
            ref[...] = val

    outs, got = _pcall(
        body, name=name, grid=(t // tm,),
        in_specs=[pl.BlockSpec((tm, DS), lambda i: (i, 0)),
                  pl.BlockSpec((8, DS), lambda i: (jnp.maximum(i * hb - 1, 0), 0))]
        + [_full(s) for s in _PREP_PARAM_SHAPES],
        out_specs=[pl.BlockSpec((tm, DR), lambda i: (i, 0))] * 7,
        out_shape=[_sds((t, DR))] * 7, args=(ps, ps, *params), comm=comm)
    return outs if comm is None else (outs, got)


def _prep_bwd(ps, params, cts_hm, cts_tm, name, comm=None):
    t = ps.shape[0]
    tm = _tile(t, TM_VPU)
    hb = tm // 8

    def body(ps_ref, halo_ref, mu_ref, *rest):
        prm = [r[...] for r in rest[:9]]
        chm = rest[9:15]
        ctm = rest[15:19]
        dpsl_ref = rest[19]
        gouts = rest[20:27]
        x = ps_ref[...]
        sh = _shifted(x, halo_ref[pl.ds(7, 1), :], pl.program_id(0) == 0)
        psl = x + (sh - x) * mu_ref[...]
        seg, seg_t = prm[7], prm[8]
        _, vjp = jax.vjp(lambda p, *w: _prep_fn(p, *w, seg, seg_t), psl, *prm[:7])
        c = [r[...] for r in chm]
        cts = (c[0] + ctm[0][...], c[1], c[2] + ctm[1][...], c[3] + ctm[2][...], c[4], c[5], ctm[3][...])
        grads = vjp(cts)
        dpsl_ref[...] = grads[0]
        for ref, gval in zip(gouts, grads[1:]):
            @pl.when(pl.program_id(0) == 0)
            def _(ref=ref, gval=gval):
                ref[...] = gval

            @pl.when(pl.program_id(0) != 0)
            def _(ref=ref, gval=gval):
                ref[...] += gval

    tk = pl.BlockSpec((tm, DR), lambda i: (i, 0))
    gshapes = _PREP_PARAM_SHAPES[1:8]
    outs, got = _pcall(
        body, name=name, grid=(t // tm,),
        in_specs=[pl.BlockSpec((tm, DS), lambda i: (i, 0)),
                  pl.BlockSpec((8, DS), lambda i: (jnp.maximum(i * hb - 1, 0), 0))]
        + [_full(s) for s in _PREP_PARAM_SHAPES] + [tk] * 10,
        out_specs=[pl.BlockSpec((tm, DS), lambda i: (i, 0))] + [_full(s) for s in gshapes],
        out_shape=[_sds((t, DS))] + [_sds(s) for s in gshapes],
        args=(ps, ps, *params, *cts_hm, *cts_tm), comm=comm)
    return outs if comm is None else (outs, got)


def _shift_bwd(dpsl, ps, mu, dpc, name):
    t = ps.shape[0]
    tm = _tile(t, TM_VPU)
    hb = tm // 8
    last_blk = t // 8 - 1

    def body(d_ref, dn_ref, ps_ref, halo_ref, mu_ref, dpc_ref, dp_ref, dmu_ref):
        i = pl.program_id(0)
        mu_v = mu_ref[...]
        d = d_ref[...]
        nxt = jnp.where(i == pl.num_programs(0) - 1, 0.0, dn_ref[pl.ds(0, 1), :])
        up = pltpu.roll(d, tm - 1, 0)
        row = lax.broadcasted_iota(jnp.int32, d.shape, 0)
        up = jnp.where(row == tm - 1, nxt, up)
        dp_ref[:, :DS] = (d * (1.0 - mu_v) + up * mu_v).astype(bf16)
        dp_ref[:, DS:] = dpc_ref[...].astype(bf16)
        x = ps_ref[...]
        sh = _shifted(x, halo_ref[pl.ds(7, 1), :], i == 0)
        dmu = jnp.sum(d * (sh - x), axis=0, keepdims=True)

        @pl.when(i == 0)
        def _():
            dmu_ref[...] = dmu

        @pl.when(i != 0)
        def _():
            dmu_ref[...] += dmu

    return pl.pallas_call(
        body, name=name, grid=(t // tm,),
        in_specs=[pl.BlockSpec((tm, DS), lambda i: (i, 0)),
                  pl.BlockSpec((8, DS), lambda i: (jnp.minimum((i + 1) * hb, last_blk), 0)),
                  pl.BlockSpec((tm, DS), lambda i: (i, 0)),
                  pl.BlockSpec((8, DS), lambda i: (jnp.maximum(i * hb - 1, 0), 0)),
                  _full((1, DS)), pl.BlockSpec((tm, 2 * DC), lambda i: (i, 0))],
        out_specs=[pl.BlockSpec((tm, DIN), lambda i: (i, 0)), _full((1, DS))],
        out_shape=[_sds((t, DIN), bf16), _sds((1, DS))],
    )(dpsl, dpsl, ps, ps, mu, dpc)


def _post_fn(y, r, kp, v, g, gn_w, gn_b, r_k, seg, seg_t):
    mu = _segsum_b(y, seg, seg_t) * (1.0 / HN)
    yc = y - mu
    var = _segsum_b(yc * yc, seg, seg_t) * (1.0 / HN)
    yo = yc * lax.rsqrt(var + GN_EPS) * gn_w + gn_b
    bonus = _segsum_b(r * kp * r_k, seg, seg_t) * v
    return (yo + bonus) * g


_POST_PARAM_SHAPES = [(1, DR), (1, DR), (1, DR), (DR, LANES), (LANES, DR)]


def _post_fwd(y, r, kp, v, g, out_b, params, name):
    t = g.shape[0]
    tm = _tile(t, TM_VPU)

    def body(y_ref, r_ref, k_ref, v_ref, g_ref, ob_ref, *rest):
        prm = [p[...] for p in rest[:5]]
        cat_ref = rest[5]
        oa = _post_fn(y_ref[...], r_ref[...], k_ref[...], v_ref[...], g_ref[...], *prm)
        cat_ref[:, :DR] = oa.astype(bf16)
        cat_ref[:, DR:] = ob_ref[...]

    tk = pl.BlockSpec((tm, DR), lambda i: (i, 0))
    return pl.pallas_call(
        body, name=name, grid=(t // tm,),
        in_specs=[tk] * 6 + [_full(s) for s in _POST_PARAM_SHAPES],
        out_specs=pl.BlockSpec((tm, 2 * DR), lambda i: (i, 0)),
        out_shape=_sds((t, 2 * DR), bf16),
    )(y, r, kp, v, g, out_b, *params)


def _post_bwd(dcat, y, r, kp, v, g, params, name, comm=None):
    t = g.shape[0]
    tm = _tile(t, TM_VPU)

    def body(dc_ref, y_ref, r_ref, k_ref, v_ref, g_ref, *rest):
        prm = [p[...] for p in rest[:5]]
        dy_ref, dr_ref, dk_ref, dv_ref, dg_ref = rest[5:10]
        gouts = rest[10:13]
        seg, seg_t = prm[3], prm[4]
        _, vjp = jax.vjp(lambda *a: _post_fn(*a, seg, seg_t),
                         y_ref[...], r_ref[...], k_ref[...], v_ref[...], g_ref[...], *prm[:3])
        grads = vjp(dc_ref[...])
        dy_ref[...] = grads[0]
        dr_ref[...] = grads[1]
        dk_ref[...] = grads[2]
        dv_ref[...] = grads[3]
        dg_ref[...] = grads[4]
        for ref, gval in zip(gouts, grads[5:]):
            @pl.when(pl.program_id(0) == 0)
            def _(ref=ref, gval=gval):
                ref[...] = gval

            @pl.when(pl.program_id(0) != 0)
            def _(ref=ref, gval=gval):
                ref[...] += gval

    tk = pl.BlockSpec((tm, DR), lambda i: (i, 0))
    return _pcall(
        body, name=name, grid=(t // tm,),
        in_specs=[tk] * 6 + [_full(s) for s in _POST_PARAM_SHAPES],
        out_specs=[tk] * 5 + [_full((1, DR))] * 3,
        out_shape=[_sds((t, DR))] * 5 + [_sds((1, DR))] * 3,
        args=(dcat, y, r, kp, v, g, *params), comm=comm)


@jax.custom_vjp
def _inv_unit_lower(lb):
    c = lb.shape[-1]
    row = lax.broadcasted_iota(jnp.int32, (c, c), 0)
    col = lax.broadcasted_iota(jnp.int32, (c, c), 1)
    p = (row == col).astype(f32)[None] + lb
    m = lb
    for _ in range(c.bit_length() - 2):
        m = mm(m, m, "nn", RIM)
        p = p + mm(m, p, "nn", RIM)
    return p


def _inv_unit_lower_fwd(lb):
    p = _inv_unit_lower(lb)
    return p, p


def _inv_unit_lower_bwd(p, ct):
    return (mm(mm(p, ct, "tn", RIM), p, "nt", RIM),)


_inv_unit_lower.defvjp(_inv_unit_lower_fwd, _inv_unit_lower_bwd)


@jax.custom_vjp
def _known_inverse(lb, p):
    return p


def _known_inverse_fwd(lb, p):
    return p, p


def _known_inverse_bwd(p, ct):
    return mm(mm(p, ct, "tn", RIM), p, "nt", RIM), jnp.zeros_like(p)


_known_inverse.defvjp(_known_inverse_fwd, _known_inverse_bwd)


def _r1_fn(r, lw, k, v, z, b, p_saved=None, with_p=False):
    c = r.shape[1]
    row = lax.broadcasted_iota(jnp.int32, (c, c), 0)
    col = lax.broadcasted_iota(jnp.int32, (c, c), 1)
    incl = (row >= col)[None]
    strict = (row > col)[None]
    eye = (row == col).astype(f32)[None]
    tri = jnp.broadcast_to((row >= col).astype(f32)[None], (r.shape[0], c, c))
    cum = mm(tri, lw, "nn", EA)
    tot = jnp.sum(lw, axis=1, keepdims=True)
    zt = z * jnp.exp(cum - lw)
    rt = r * jnp.exp(cum)
    g_inv = jnp.exp(-cum)
    g_rem = jnp.exp(tot - cum)
    bt, kt = b * g_inv, k * g_inv
    bh, kh = b * g_rem, k * g_rem
    lb = jnp.where(strict, mm(zt, bt, "nt", RGM), 0.0)
    lk = jnp.where(strict, mm(zt, kt, "nt", RGM), 0.0)
    mb = jnp.where(incl, mm(rt, bt, "nt", RGM), 0.0)
    mk = jnp.where(incl, mm(rt, kt, "nt", RGM), 0.0)
    p = _inv_unit_lower(lb) if p_saved is None else _known_inverse(lb, p_saved)
    w1 = mm(p, zt, "nn", RWM)
    w2 = mm(p, mm(lk, v, "nn", RWM), "nn", RWM)
    a = mm(w1, bh, "tn", RAM) + eye * jnp.exp(tot)
    g = mm(w2, bh, "tn", RAM) + mm(v, kh, "tn", RAM)
    q = rt + mm(mb, w1, "nn", RWM)
    yl = mm(mb, w2, "nn", RWM) + mm(mk, v, "nn", RWM)
    return (a, g, q, yl, p) if with_p else (a, g, q, yl)


def _chunks_in(ref, n):
    return jnp.concatenate([_heads_of(ref, pl.ds(s * CHUNK, CHUNK)) for s in range(n)], axis=0)


def _chunks_out(ref, val, n):
    for s in range(n):
        _heads_to(ref, pl.ds(s * CHUNK, CHUNK), val[s * NH:(s + 1) * NH])


def _r1_fwd(ins, name, comm=None):
    t = ins[0].shape[0]
    n = R1_CHUNKS
    nc = t // CHUNK

    def body(r, lw, k, v, z, b, a_ref, g_ref, p_ref, q_ref, yl_ref):
        a, g, q, yl, p = _r1_fn(*[_chunks_in(x, n) for x in (r, lw, k, v, z, b)], with_p=True)
        for s in range(n):
            a_ref[s] = a[s * NH:(s + 1) * NH]
            g_ref[s] = g[s * NH:(s + 1) * NH]
            p_ref[s] = p[s * NH:(s + 1) * NH]
        _chunks_out(q_ref, q, n)
        _chunks_out(yl_ref, yl, n)

    ck = pl.BlockSpec((n * CHUNK, DR), lambda c: (c, 0))
    st = pl.BlockSpec((n, NH, HN, HN), lambda c: (c, 0, 0, 0))
    return _pcall(
        body, name=name, grid=(nc // n,), in_specs=[ck] * 6, out_specs=[st, st, st, ck, ck],
        out_shape=[_sds((nc, NH, HN, HN))] * 3 + [_sds((t, DR))] * 2, args=tuple(ins), comm=comm)


def _r1_bwd(ins, p_inv, da, dg, dq, dyl, name, comm=None):
    t = ins[0].shape[0]
    n = R1_CHUNKS
    nc = t // CHUNK

    def body(r, lw, k, v, z, b, p_ref, da_ref, dg_ref, dq_ref, dyl_ref, *outs):
        p_saved = jnp.concatenate([p_ref[s] for s in range(n)], axis=0)
        _, vjp = jax.vjp(lambda *a: _r1_fn(*a, p_saved=p_saved), *[_chunks_in(x, n) for x in (r, lw, k, v, z, b)])
        cts = (jnp.concatenate([da_ref[s] for s in range(n)], axis=0),
               jnp.concatenate([dg_ref[s] for s in range(n)], axis=0), _chunks_in(dq_ref, n), _chunks_in(dyl_ref, n))
        for ref, gval in zip(outs, vjp(cts)):
            _chunks_out(ref, gval, n)

    ck = pl.BlockSpec((n * CHUNK, DR), lambda c: (c, 0))
    st = pl.BlockSpec((n, NH, HN, HN), lambda c: (c, 0, 0, 0))
    return _pcall(
        body, name=name, grid=(nc // n,), in_specs=[ck] * 6 + [st, st, st, ck, ck], out_specs=[ck] * 6,
        out_shape=[_sds((t, DR))] * 6, args=(*ins, p_inv, da, dg, dq, dyl), comm=comm)


def _r2_fwd(a, g, q, yl, name):
    nc = a.shape[0]
    t = q.shape[0]

    def body(a_ref, g_ref, q_ref, yl_ref, y_ref, s_ref, s):
        @pl.when(pl.program_id(0) == 0)
        def _():
            s[...] = jnp.zeros_like(s)
        s0 = s[...]
        for j in range(n):
            rows = pl.ds(j * CHUNK, CHUNK)
            s_ref[j] = s0
            _heads_to(y_ref, rows, mm(_heads_of(q_ref, rows), s0, "nt", R2M) + _heads_of(yl_ref, rows))
            s0 = mm(s0, a_ref[j], "nn", R2M) + g_ref[j]
        s[...] = s0

    n = math.gcd(nc, R2_CHUNKS)
    ck = pl.BlockSpec((n * CHUNK, DR), lambda c: (c, 0))
    st = pl.BlockSpec((n, NH, HN, HN), lambda c: (c, 0, 0, 0))
    return pl.pallas_call(
        body, name=name, grid=(nc // n,), in_specs=[st, st, ck, ck], out_specs=[ck, st],
        out_shape=[_sds((t, DR)), _sds((nc, NH, HN, HN))],
        scratch_shapes=[pltpu.VMEM((NH, HN, HN), f32)],
    )(a, g, q, yl)


def _r2_bwd(dy, q, s_all, a, name):
    nc = a.shape[0]
    t = q.shape[0]

    def body(dy_ref, q_ref, s_ref, a_ref, dq_ref, da_ref, dg_ref, ds):
        @pl.when(pl.program_id(0) == 0)
        def _():
            ds[...] = jnp.zeros_like(ds)
        dsn = ds[...]
        for j in reversed(range(n)):
            rows = pl.ds(j * CHUNK, CHUNK)
            s0 = s_ref[j]
            dyv = _heads_of(dy_ref, rows)
            dg_ref[j] = dsn
            da_ref[j] = mm(s0, dsn, "tn", R2M)
            _heads_to(dq_ref, rows, mm(dyv, s0, "nn", R2M))
            dsn = mm(dsn, a_ref[j], "nt", R2M) + mm(dyv, _heads_of(q_ref, rows), "tn", R2M)
        ds[...] = dsn

    n = math.gcd(nc, R2_CHUNKS)
    nb = nc // n
    ck = pl.BlockSpec((n * CHUNK, DR), lambda c: (nb - 1 - c, 0))
    st = pl.BlockSpec((n, NH, HN, HN), lambda c: (nb - 1 - c, 0, 0, 0))
    return pl.pallas_call(
        body, name=name, grid=(nb,), in_specs=[ck, ck, st, st], out_specs=[ck, st, st],
        out_shape=[_sds((t, DR)), _sds((nc, NH, HN, HN)), _sds((nc, NH, HN, HN))],
        scratch_shapes=[pltpu.VMEM((NH, HN, HN), f32)],
    )(dy, q, s_all, a)


def _ln_silu(c, w, b):
    mu = jnp.mean(c, axis=-1, keepdims=True)
    cc = c - mu
    var = jnp.mean(cc * cc, axis=-1, keepdims=True)
    u = cc * lax.rsqrt(var + LN_EPS) * w + b
    return u * jax.nn.sigmoid(u)


def _glu_tile(pc):
    return pc[:, :DC] * jax.nn.sigmoid(pc[:, DC:])


def _fill_glu(s_ref, pc_ref, halo_ref, first, tm):
    s_ref[pl.ds(0, HALO), :] = jnp.where(first, 0.0, _glu_tile(halo_ref[...]))
    s_ref[pl.ds(HALO, tm), :] = _glu_tile(pc_ref[...])


def _phase_copies(sh):
    n = sh.shape[1] - 8
    for r in range(1, 8):
        sh[r, pl.ds(0, n), :] = sh[0, pl.ds(r, n), :]


def _rows_at(sh, o, n):
    return sh[o % 8, pl.ds(o - o % 8, n), :]


def _conv_fwd(pc, dw, cb, lnw, lnb, name):
    t = pc.shape[0]
    tm = _tile(t, TM_VPU)
    hb = tm // HALO

    def body(pc_ref, halo_ref, dw_ref, cb_ref, w_ref, b_ref, c_ref, o_ref, s):
        _fill_glu(s.at[0], pc_ref, halo_ref, pl.program_id(0) == 0, tm)
        _phase_copies(s)
        for r0 in range(0, tm, CONV_ROWS):
            rows = pl.ds(r0, CONV_ROWS)
            acc = _rows_at(s, r0 + HALO - CW + 1, CONV_ROWS) * dw_ref[pl.ds(0, 1), :]
            for j in range(1, CW):
                acc = acc + _rows_at(s, r0 + HALO - CW + 1 + j, CONV_ROWS) * dw_ref[pl.ds(j, 1), :]
            c = acc + cb_ref[...]
            c_ref[rows, :] = c
            o_ref[rows, :] = _ln_silu(c, w_ref[...], b_ref[...]).astype(bf16)

    return pl.pallas_call(
        body, name=name, grid=(t // tm,),
        in_specs=[pl.BlockSpec((tm, 2 * DC), lambda i: (i, 0)),
                  pl.BlockSpec((HALO, 2 * DC), lambda i: (jnp.maximum(i * hb - 1, 0), 0)),
                  _full((CW, DC)), _full((1, DC)), _full((1, DC)), _full((1, DC))],
        out_specs=[pl.BlockSpec((tm, DC), lambda i: (i, 0)), pl.BlockSpec((tm, DC), lambda i: (i, 0))],
        out_shape=[_sds((t, DC)), _sds((t, DC), bf16)],
        scratch_shapes=[pltpu.VMEM((8, HALO + tm, DC), f32)],
    )(pc, pc, dw, cb, lnw, lnb)


def _conv_bwd1(dcat, c, lnw, lnb, name):
    t = c.shape[0]
    tm = _tile(t, TM_VPU)

    def body(d_ref, c_ref, w_ref, b_ref, dc_ref, dw_ref, db_ref, dcb_ref):
        _, vjp = jax.vjp(_ln_silu, c_ref[...], w_ref[...], b_ref[...])
        dc, dw, db = vjp(d_ref[...])
        dc_ref[...] = dc
        dcb = jnp.sum(dc, axis=0, keepdims=True)
        for ref, gval in ((dw_ref, dw), (db_ref, db), (dcb_ref, dcb)):
            @pl.when(pl.program_id(0) == 0)
            def _(ref=ref, gval=gval):
                ref[...] = gval

            @pl.when(pl.program_id(0) != 0)
            def _(ref=ref, gval=gval):
                ref[...] += gval

    tk = pl.BlockSpec((tm, DC), lambda i: (i, 0))
    return pl.pallas_call(
        body, name=name, grid=(t // tm,),
        in_specs=[pl.BlockSpec((tm, DC), lambda i: (i, 1)), tk, _full((1, DC)), _full((1, DC))],
        out_specs=[tk] + [_full((1, DC))] * 3,
        out_shape=[_sds((t, DC))] + [_sds((1, DC))] * 3,
    )(dcat, c, lnw, lnb)


def _conv_bwd2(dc, pc, dw, name):
    t = pc.shape[0]
    tm = _tile(t, TM_VPU)
    hb = tm // HALO
    last_blk = t // HALO - 1

    def body(dc_ref, dn_ref, pc_ref, halo_ref, dw_ref, dpc_ref, ddw_ref, s, sd, acc):
        i = pl.program_id(0)
        _fill_glu(s.at[0], pc_ref, halo_ref, i == 0, tm)
        sd[0, pl.ds(0, tm), :] = dc_ref[...]
        sd[0, pl.ds(tm, HALO), :] = jnp.where(i == pl.num_programs(0) - 1, 0.0, dn_ref[...])
        _phase_copies(s)
        _phase_copies(sd)
        for r0 in range(0, tm, CONV_ROWS):
            rows = pl.ds(r0, CONV_ROWS)
            dcb = sd[0, rows, :]
            dglu = None
            for j in range(CW):
                term = _rows_at(sd, r0 + CW - 1 - j, CONV_ROWS) * dw_ref[pl.ds(j, 1), :]
                dglu = term if dglu is None else dglu + term
                part = dcb * _rows_at(s, r0 + HALO - CW + 1 + j, CONV_ROWS)
                part8 = part[0:8]
                for q in range(8, CONV_ROWS, 8):
                    part8 = part8 + part[q:q + 8]
                if r0 == 0:
                    acc[pl.ds(8 * j, 8), :] = part8
                else:
                    acc[pl.ds(8 * j, 8), :] += part8
            a = pc_ref[rows, :DC]
            sg = jax.nn.sigmoid(pc_ref[rows, DC:])
            dpc_ref[rows, :DC] = dglu * sg
            dpc_ref[rows, DC:] = dglu * a * sg * (1.0 - sg)
        rows = [jnp.sum(acc[pl.ds(8 * j, 8), :], axis=0, keepdims=True) for j in range(CW)]

        @pl.when(i == 0)
        def _():
            for j in range(CW):
                ddw_ref[pl.ds(j, 1), :] = rows[j]

        @pl.when(i != 0)
        def _():
            for j in range(CW):
                ddw_ref[pl.ds(j, 1), :] += rows[j]

    return pl.pallas_call(
        body, name=name, grid=(t // tm,),
        in_specs=[pl.BlockSpec((tm, DC), lambda i: (i, 0)),
                  pl.BlockSpec((HALO, DC), lambda i: (jnp.minimum((i + 1) * hb, last_blk), 0)),
                  pl.BlockSpec((tm, 2 * DC), lambda i: (i, 0)),
                  pl.BlockSpec((HALO, 2 * DC), lambda i: (jnp.maximum(i * hb - 1, 0), 0)),
                  _full((CW, DC))],
        out_specs=[pl.BlockSpec((tm, 2 * DC), lambda i: (i, 0)), _full((CW, DC))],
        out_shape=[_sds((t, 2 * DC)), _sds((CW, DC))],
        scratch_shapes=[pltpu.VMEM((8, HALO + tm, DC), f32), pltpu.VMEM((8, tm + HALO, DC), f32),
                        pltpu.VMEM((8 * CW, DC), f32)],
    )(dc, dc, pc, pc, dw)


def _adamw(w, g, m, v):
    m = ADAM_B1 * m + (1.0 - ADAM_B1) * g
    v = ADAM_B2 * v + (1.0 - ADAM_B2) * (g * g)
    m_hat = m / (1.0 - ADAM_B1 ** ADAM_STEP)
    v_hat = v / (1.0 - ADAM_B2 ** ADAM_STEP)
    delta = -ADAM_LR * (m_hat / (jnp.sqrt(v_hat) + ADAM_EPS) + ADAM_WD * w)
    return delta, m, v


def _adam_slots(w, slots, m, v, name):
    r, c = w.shape
    tr = next(cand for cand in (512, 352, 256, 128, r) if r % cand == 0)

    def body(w_ref, s_ref, m_ref, v_ref, g_ref, d_ref, nm_ref, nv_ref):
        g = s_ref[0].astype(f32)
        for k in range(1, NDEV):
            g = g + s_ref[k].astype(f32)
        delta, nm, nv = _adamw(w_ref[...], g, m_ref[...], v_ref[...])
        g_ref[...] = g
        d_ref[...] = delta
        nm_ref[...] = nm
        nv_ref[...] = nv

    blk = pl.BlockSpec((tr, c), lambda i: (i, 0))
    return pl.pallas_call(
        body, name=name, grid=(r // tr,),
        in_specs=[blk, pl.BlockSpec((NDEV, tr, c), lambda i: (0, i, 0)), blk, blk],
        out_specs=[blk] * 4, out_shape=[_sds((r, c))] * 4,
    )(w, slots, m, v)


def _sum_slots(slots, name):
    _, r, c = slots.shape

    def body(s_ref, o_ref):
        g = s_ref[0]
        for k in range(1, NDEV):
            g = g + s_ref[k]
        o_ref[...] = g

    return pl.pallas_call(body, name=name, in_specs=[_full((NDEV, r, c))], out_specs=_full((r, c)),
                          out_shape=_sds((r, c)), grid=(1,))(slots)


def _adam_many(ws, gs, ms, vs, name):
    n = len(ws)
    shapes = [w.shape for w in ws]

    def body(*refs):
        for i in range(n):
            w_ref, g_ref, m_ref, v_ref = refs[i], refs[n + i], refs[2 * n + i], refs[3 * n + i]
            delta, nm, nv = _adamw(w_ref[...], g_ref[...], m_ref[...], v_ref[...])
            refs[4 * n + i][...] = delta
            refs[5 * n + i][...] = nm
            refs[6 * n + i][...] = nv

    outs = pl.pallas_call(body, name=name, grid=(1,), in_specs=[_full(s) for s in shapes] * 4,
                          out_specs=[_full(s) for s in shapes] * 3, out_shape=[_sds(s) for s in shapes] * 3,
                          )(*ws, *gs, *ms, *vs)
    return outs[:n], outs[n:2 * n], outs[2 * n:]


def _pack(pieces, total):
    flat = []
    n = 0
    for p in pieces:
        p = p.reshape(-1)
        pad = (-p.shape[0]) % LANES
        if pad:
            p = jnp.pad(p, (0, pad))
        flat.append(p)
        n += p.shape[0]
    if total > n:
        flat.append(jnp.zeros((total - n,), f32))
    return jnp.concatenate(flat).reshape(total // LANES, LANES)


def _unpack(vec, shapes):
    flat = vec.reshape(-1)
    out, off = [], 0
    for s in shapes:
        n = math.prod(s)
        out.append(flat[off:off + n].reshape(s))
        off += n + (-n) % LANES
    return out


def _round_up(n, m):
    return (n + m - 1) // m * m


def kernel(x, ffn1_norm_pre, ffn1_norm_post, ffn1_w_gu, ffn1_w_down, mix_norm_pre, mix_norm_post, w_in, shift_mu, w_up, w0, a_up, a0, g_up, k_k, k_a, r_k, gn_w, gn_b, conv_dw, conv_b, conv_ln_w, conv_ln_b, w_out, ffn2_norm_pre, ffn2_norm_post, ffn2_w_gu, ffn2_w_down, loss_target, m_ffn1_norm_pre, m_ffn1_norm_post, m_ffn1_w_gu, m_ffn1_w_down, m_mix_norm_pre, m_mix_norm_post, m_w_in, m_shift_mu, m_w_up, m_w0, m_a_up, m_a0, m_g_up, m_k_k, m_k_a, m_r_k, m_gn_w, m_gn_b, m_conv_dw, m_conv_b, m_conv_ln_w, m_conv_ln_b, m_w_out, m_ffn2_norm_pre, m_ffn2_norm_post, m_ffn2_w_gu, m_ffn2_w_down, v_ffn1_norm_pre, v_ffn1_norm_post, v_ffn1_w_gu, v_ffn1_w_down, v_mix_norm_pre, v_mix_norm_post, v_w_in, v_shift_mu, v_w_up, v_w0, v_a_up, v_a0, v_g_up, v_k_k, v_k_a, v_r_k, v_gn_w, v_gn_b, v_conv_dw, v_conv_b, v_conv_ln_w, v_conv_ln_b, v_w_out, v_ffn2_norm_pre, v_ffn2_norm_post, v_ffn2_w_gu, v_ffn2_w_down):
    named = dict(locals())
    t = x.shape[1]
    x0 = x.reshape(t, D)
    tgt = loss_target.reshape(t, D)

    def shard(a):
        return a[0].astype(bf16)

    def shard_t(a):
        return jnp.swapaxes(a[0], 0, 1).astype(bf16)

    (wg1,) = _exchange([shard_t(ffn1_w_gu)], ["gather"], [SLOT42], "gather_wg1")

    def cols(a):
        return jnp.transpose(a, (1, 0, 2)).reshape(a.shape[1], NDEV * a.shape[2])

    lane = jnp.arange(DR, dtype=jnp.int32) // HN
    seg = (lane[:, None] == jnp.arange(LANES, dtype=jnp.int32)[None, :]).astype(f32)
    seg_t = seg.T
    rk_row = r_k.reshape(1, DR)
    post_params = (gn_w, gn_b, rk_row, seg, seg_t)

    (xn1, gu1, h1), (wd1, win_g) = _ffn_in_fwd(x0, ffn1_norm_pre, wg1, "ffn1_in_fwd", comm=(
        [shard(ffn1_w_down), shard_t(w_in)], ["gather"] * 2, [SLOT8] * 2))
    wd1 = wd1.reshape(4, FB, D)
    win_t = win_g.reshape(DIN, D)
    (f1, x1), (wout_g, wup_g, aup_g, gup_g, cdw_g) = _mm_norm_res_fwd(
        h1, wd1, x0, ffn1_norm_post, 0.5, "ffn1_out_fwd", comm=(
            [shard(w_out), w_up[0], a_up[0], g_up[0], conv_dw[0]], ["gather"] * 5, [SLOT8] * 5))
    wout_full = wout_g.reshape(1, D, D)
    wup_full, aup_full, gup_full, cdw_full = cols(wup_g), cols(aup_g), cols(gup_g), cols(cdw_g)
    zeros64 = jnp.zeros((HN, DR), f32)
    wup_pad = jnp.concatenate([wup_full, zeros64], axis=0)
    aup_pad = jnp.concatenate([zeros64, aup_full], axis=0)
    prep_params = (shift_mu, w0, a0, k_k, k_a, wup_pad, aup_pad, gup_full, seg, seg_t)
    hm, ps, pc = _norm_mm_fwd(x1, mix_norm_pre, win_t, "mix_in_fwd")
    rec_in, (wd2,) = _prep_fwd(ps, prep_params, "prep_fwd", comm=([shard(ffn2_w_down)], ["gather"], [SLOT8]))
    wd2 = wd2.reshape(4, FB, D)
    g_gate = rec_in[6]
    rec_in = rec_in[:6]
    (a_c, g_c, p_c, q_c, yl_c), (wg2,) = _r1_fwd(rec_in, "r1_fwd", comm=([shard_t(ffn2_w_gu)], ["gather"], [SLOT42]))
    y_rec, s_all = _r2_fwd(a_c, g_c, q_c, yl_c, "r2_fwd")
    c_conv, out_b = _conv_fwd(pc, cdw_full, conv_b, conv_ln_w, conv_ln_b, "conv_fwd")
    cat = _post_fwd(y_rec, rec_in[0], rec_in[2], rec_in[3], g_gate, out_b, post_params, "post_fwd")
    fm, x2 = _mm_norm_res_fwd(cat.reshape(1, t, D), wout_full, x1, mix_norm_post, 1.0, "mix_out_fwd")
    (xn2, gu2, h2), _ = _ffn_in_fwd(x2, ffn2_norm_pre, wg2, "ffn2_in_fwd")
    f2, dy, loss_part = _mm_norm_res_fwd(h2, wd2, x2, ffn2_norm_post, 0.5, "ffn2_out_fwd", tgt=tgt)

    (df2, dg_post2, dgu2), _ = _post_bwd_nt(dy, f2, ffn2_norm_post, wd2, 0.5, gu2, "ffn2_out_bwd")
    (dx2, dg_pre2), _ = _nt_pre_bwd(dgu2, wg2, x2, ffn2_norm_pre, dy, "ffn2_in_bwd")
    dwd2 = _wgrad(h2, df2.reshape(1, t, D), True, False, D, "ffn2_wd_grad")
    dwg2 = _wgrad(dgu2.reshape(8, t, FB), xn2.reshape(1, t, D), True, False, D, "ffn2_wg_grad")
    (dfm, dg_postm, dcat), _ = _post_bwd_nt(dx2, fm, mix_norm_post, wout_full, 1.0, None, "mix_out_bwd")
    dwout = _wgrad(cat.reshape(1, t, D), dfm.reshape(1, t, D), False, False, D, "wout_grad")
    dcat = dcat.reshape(t, D)
    (dy_rec, dr2, dkp2, dv2, dgate, dgn_w, dgn_b, drk), (s_wout,) = _post_bwd(
        dcat, y_rec, rec_in[0], rec_in[2], rec_in[3], g_gate, post_params, "post_bwd", comm=(
            [dwout.reshape(NDEV, D // NDEV, D)], ["scatter"], [SLOT8]))
    dq_c, da_c, dg_c = _r2_bwd(dy_rec, q_c, s_all, a_c, "r2_bwd")
    rec_grads, (s_wg2,) = _r1_bwd(rec_in, p_c, da_c, dg_c, dq_c, dy_rec, "r1_bwd", comm=(
        [dwg2.reshape(4, 2, FB, D)], ["scatter"], [SLOT42]))
    prep_out, (s_wd2,) = _prep_bwd(ps, prep_params, rec_grads, (dr2, dkp2, dv2, dgate), "prep_bwd", comm=(
        [dwd2.reshape(NDEV, F // NDEV, D)], ["scatter"], [SLOT8]))
    dpsl, dw0, da0, dkk, dka, dwup_pad, daup_pad, dgup = prep_out
    dc_conv, dlnw, dlnb, dcb = _conv_bwd1(dcat, c_conv, conv_ln_w, conv_ln_b, "conv_bwd1")
    dpc, dcdw = _conv_bwd2(dc_conv, pc, cdw_full, "conv_bwd2")
    dp, dmu = _shift_bwd(dpsl, ps, shift_mu, dpc, "shift_bwd")
    (dx1, dg_prem), _ = _nt_pre_bwd(dp.reshape(1, 1, t, DIN), win_t.reshape(1, 1, DIN, D), x1, mix_norm_pre, dx2,
                                    "mix_in_bwd")
    dwin_s = _wgrad(dp.reshape(1, t, DIN), hm.reshape(1, t, D), False, False, D // 2, "win_grad").reshape(
        NDEV, DIN // NDEV, D)
    (df1, dg_post1, dgu1), (s_win,) = _post_bwd_nt(dx1, f1, ffn1_norm_post, wd1, 0.5, gu1, "ffn1_out_bwd", comm=(
        [dwin_s], ["scatter"], [SLOT8]))
    dwd1 = _wgrad(h1, df1.reshape(1, t, D), True, False, D, "ffn1_wd_grad")
    dwg1, (s_wd1,) = _wgrad(dgu1.reshape(8, t, FB), xn1.reshape(1, t, D), True, False, D, "ffn1_wg_grad", comm=(
        [dwd1.reshape(NDEV, F // NDEV, D)], ["scatter"], [SLOT8]))
    wg1_sends, wg1_recvs, dwg1_thru, wg1_land, started = _scatter_start(dwg1.reshape(4, 2, FB, D), SLOT42, "wg1_scatter_start")
    (dx0, dg_pre1), _ = _nt_pre_bwd(dgu1, wg1, x0, ffn1_norm_pre + started[:1, :1], dx1, "ffn1_in_bwd")

    res = {}

    def adam_big(nm, w, s, m, v, transposed):
        view = (lambda a: jnp.swapaxes(a[0], 0, 1)) if transposed else (lambda a: a[0])
        outs = _adam_slots(view(w), s, view(m), view(v), "adam_" + nm)
        done.append(outs[1])
        res[nm] = [(jnp.swapaxes(o, 0, 1) if transposed else o)[None] for o in outs]

    done = []

    adam_big("ffn2_w_gu", ffn2_w_gu, s_wg2.reshape(NDEV, FB, D), m_ffn2_w_gu, v_ffn2_w_gu, True)
    adam_big("ffn2_w_down", ffn2_w_down, s_wd2, m_ffn2_w_down, v_ffn2_w_down, False)
    adam_big("w_in", w_in, s_win, m_w_in, v_w_in, True)
    adam_big("w_out", w_out, s_wout, m_w_out, v_w_out, False)
    adam_big("ffn1_w_down", ffn1_w_down, s_wd1, m_ffn1_w_down, v_ffn1_w_down, False)
    _, s_wg1 = _scatter_wait(wg1_sends, wg1_recvs, dwg1_thru, wg1_land, [dg_pre1] + done, SLOT42, "wg1_scatter_wait")
    my_slot = SLOT42[1](4 * lax.axis_index("x") + 2 * lax.axis_index("y") + lax.axis_index("c"))
    own = lax.dynamic_slice(dwg1.reshape(4, 2, FB, D), (*my_slot, 0, 0), (1, 1, FB, D))
    s_wg1 = lax.dynamic_update_slice(s_wg1, own, (*my_slot, 0, 0))
    rep_shapes = [(1, 1)] + [(1, D)] * 6 + [(1, DS)] + [(1, DR)] * 10
    rep_parts = [loss_part, dg_pre1, dg_post1, dg_prem, dg_postm, dg_pre2, dg_post2, dmu,
                 dw0, da0, dkk, dka, drk, dgn_w, dgn_b, dcb, dlnw, dlnb]
    n_rep = _round_up(sum(_round_up(math.prod(s), LANES) for s in rep_shapes), 8 * LANES)
    sh_shapes = [(HN, HN), (HN, HN), (LANES, HN), (CW, HN)]
    n_sh = _round_up(sum(_round_up(math.prod(s), LANES) for s in sh_shapes), 8 * LANES)
    sh_parts = []
    for a in (dwup_pad[:HN], daup_pad[HN:], dgup, dcdw):
        rows = jnp.transpose(a.reshape(a.shape[0], NDEV, HN), (1, 0, 2)).reshape(NDEV, a.shape[0] * HN)
        sh_parts.append(jnp.pad(rows, ((0, 0), (0, (-rows.shape[1]) % LANES))))
    sh_vec = jnp.concatenate(sh_parts, axis=1)
    sh_vec = jnp.pad(sh_vec, ((0, 0), (0, n_sh - sh_vec.shape[1]))).reshape(NDEV, n_sh // LANES, LANES)
    rep_vec, sh_vec, s_wg1 = lax.optimization_barrier((_pack(rep_parts, n_rep), sh_vec, s_wg1))
    s_rep, s_sh = _exchange([rep_vec, sh_vec], ["gather", "scatter"], [SLOT8, SLOT8], "exchange_small")
    adam_big("ffn1_w_gu", ffn1_w_gu, s_wg1.reshape(NDEV, FB, D), m_ffn1_w_gu, v_ffn1_w_gu, True)


    gsum = _unpack(_sum_slots(s_rep, "sum_rep"), rep_shapes)
    gsh = _unpack(_sum_slots(s_sh, "sum_sharded"), sh_shapes)
    loss = gsum[0].reshape(())
    rep_names = ["ffn1_norm_pre", "ffn1_norm_post", "mix_norm_pre", "mix_norm_post", "ffn2_norm_pre", "ffn2_norm_post",
                 "shift_mu", "w0", "a0", "k_k", "k_a", "r_k", "gn_w", "gn_b", "conv_b", "conv_ln_w", "conv_ln_b"]
    shard_names = ["w_up", "a_up", "g_up", "conv_dw"]
    small_g = {n: gsum[1 + i] for i, n in enumerate(rep_names)}
    small_g["r_k"] = small_g["r_k"].reshape(1, NH, HN)
    for i, n in enumerate(shard_names):
        small_g[n] = gsh[i][None]
    names = rep_names + shard_names
    shapes = [named[n].shape for n in names]

    def flat2(a):
        return a.reshape(1, DR) if a.shape == (1, NH, HN) else a.reshape(a.shape[-2:])

    deltas, new_ms, new_vs = _adam_many(*[[flat2(d[n]) for n in names] for d in (
        {n: named[n] for n in names}, small_g, {n: named["m_" + n] for n in names}, {n: named["v_" + n] for n in names})],
        "adam_small")
    for n, shp, dl, nm_, nv_ in zip(names, shapes, deltas, new_ms, new_vs):
        res[n] = [small_g[n], dl.reshape(shp), nm_.reshape(shp), nv_.reshape(shp)]

    order = ["ffn1_norm_pre", "ffn1_norm_post", "ffn1_w_gu", "ffn1_w_down", "mix_norm_pre", "mix_norm_post", "w_in",
             "shift_mu", "w_up", "w0", "a_up", "a0", "g_up", "k_k", "k_a", "r_k", "gn_w", "gn_b", "conv_dw", "conv_b",
             "conv_ln_w", "conv_ln_b", "w_out", "ffn2_norm_pre", "ffn2_norm_post", "ffn2_w_gu", "ffn2_w_down"]
    return (loss, dx0.reshape(1, t, D), *[res[n][0] for n in order], *[res[n][1] for n in order],
            *[res[n][2] for n in order], *[res[n][3] for n in order])
```

```python
import functools
import math

import jax
import jax.numpy as jnp
from jax import lax
from jax.experimental import pallas as pl
from jax.experimental.pallas import tpu as pltpu

f32 = jnp.float32
bf16 = jnp.bfloat16

D = 1024
F = 2816
FB = 704
DR = 512
DC = 512
NH = 8
HN = 64
DS = 1792
DIN = 2816
CW = 31
CHUNK = 64
R1_CHUNKS = 4
R2_CHUNKS = 8
NDEV = 8
RMS_EPS = 1e-6
GN_EPS = 64e-5
LN_EPS = 1e-5
DECAY_SCALE = math.exp(-0.5)
ADAM_LR, ADAM_B1, ADAM_B2, ADAM_EPS, ADAM_WD, ADAM_STEP = 0.001, 0.9, 0.999, 1e-08, 0.01, 10
LANES = 128
HALO = 32
CONV_ROWS = 32
TM_MXU = 1024
TM_SUB = 256
TM_VPU = 256

ANY = pl.BlockSpec(memory_space=pl.ANY)


def _full(shape):
    return pl.BlockSpec(shape, lambda *_: (0,) * len(shape))


def _sds(shape, dtype=f32):
    return jax.ShapeDtypeStruct(shape, dtype)


def _dot(a, b):
    return jnp.dot(a, b, preferred_element_type=f32)


def _dot_nt(a, b):
    return lax.dot_general(a, b, (((1,), (1,)), ((), ())), preferred_element_type=f32)


def _dot_tn(a, b):
    return lax.dot_general(a, b, (((0,), (0,)), ((), ())), preferred_element_type=f32)


def _terms(a, n):
    out, rem = [], a
    for i in range(n):
        t = rem.astype(bf16)
        out.append(t)
        if i + 1 < n:
            rem = rem - t.astype(f32)
    return out


def _mm_raw(a, b, kind, mode):
    nb = a.ndim - 2
    bd = tuple(range(nb))
    ca = nb if kind == "tn" else nb + 1
    cb = nb + 1 if kind == "nt" else nb
    dn = (((ca,), (cb,)), (bd, bd))
    pa, pb = mode[:2]
    ta, tb = _terms(a, pa), _terms(b, pb)
    acc = None
    for i in range(pa):
        for j in range(pb):
            if i + j < max(pa, pb):
                p = lax.dot_general(ta[i], tb[j], dn, preferred_element_type=f32)
                acc = p if acc is None else acc + p
    return acc


@functools.partial(jax.custom_vjp, nondiff_argnums=(2, 3))
def mm(a, b, kind, mode):
    return _mm_raw(a, b, kind, mode)


def _mm_fwd(a, b, kind, mode):
    return _mm_raw(a, b, kind, mode), (a, b)


def _ct_terms(keep):
    return EXACT_TERMS if keep == 1 else keep


def _mm_bwd(kind, mode, res, ct):
    a, b = res
    pa, pb = mode[:2]
    if len(mode) == 4:
        ca = cb = mode[2]
        pa = pb = mode[3]
    elif mode == (1, 1):
        ca = cb = 1
    else:
        ca, cb = _ct_terms(pb), _ct_terms(pa)
    if kind == "nn":
        da = mm(ct, b, "nt", (ca, pb))
        db = mm(a, ct, "tn", (pa, cb))
    elif kind == "nt":
        da = mm(ct, b, "nn", (ca, pb))
        db = mm(ct, a, "tn", (cb, pa))
    else:
        da = mm(b, ct, "nt", (pb, ca))
        db = mm(a, ct, "nn", (pa, cb))
    return da, db


mm.defvjp(_mm_fwd, _mm_bwd)

X3 = (2, 2)
EXACT_TERMS = 2
EA = (1, EXACT_TERMS)
EB = (EXACT_TERMS, 1)
X1 = (1, 1)
RGM = (1, 1, 1, 2)
RIM = X1
RWM = X1
RAM = X1
R2M = X1
LORA = X1


def _rms_fwd(x, g):
    inv = lax.rsqrt(jnp.mean(x * x, axis=-1, keepdims=True) + RMS_EPS)
    return x * inv * g


def _rms_bwd(x, g, dy):
    inv = lax.rsqrt(jnp.mean(x * x, axis=-1, keepdims=True) + RMS_EPS)
    xh = x * inv
    dxh = dy * g
    dg = jnp.sum(dy * xh, axis=0, keepdims=True)
    dx = inv * (dxh - xh * jnp.mean(dxh * xh, axis=-1, keepdims=True))
    return dx, dg


def _tile(t, want):
    return min(t, want)


NPEER = NDEV - 1


def _comm_out_shape(arrays, modes, slots):
    out = []
    for a, mode, (lead, _) in zip(arrays, modes, slots):
        shp = tuple(lead) + tuple(a.shape) if mode == "gather" else tuple(a.shape)
        out.append(pltpu.HBM(shp, a.dtype))
    return out


def _comm_sems(n):
    return [pltpu.SemaphoreType.DMA((n * NPEER,)), pltpu.SemaphoreType.DMA((n * NPEER,)), pltpu.SemaphoreType.DMA((n,))]


RELAYED = (3, 5, 7)


def _comm_copies(ins, outs, sems, modes, slots, want):
    n = len(ins)
    send_sems, recv_sems, loc_sems = sems
    x, y, c = lax.axis_index("x"), lax.axis_index("y"), lax.axis_index("c")
    me = 4 * x + 2 * y + c

    def peer_of(k):
        px = 1 - x if k & 4 else x
        py = 1 - y if k & 2 else y
        pc = 1 - c if k & 1 else c
        return (px, py, pc), 4 * px + 2 * py + pc

    def slot(i, dev):
        return outs[i].at[slots[i][1](dev)]

    if want == "local":
        return [pltpu.make_async_copy(ins[i] if modes[i] == "gather" else ins[i].at[slots[i][1](me)], slot(i, me),
                                      loc_sems.at[i]) for i in range(n)]
    sibling = peer_of(1)[0]
    out = []
    for k in range(1, NDEV):
        peer, pid = peer_of(k)
        for i in range(n):
            sem = dict(send_sem=send_sems.at[i * NPEER + k - 1], recv_sem=recv_sems.at[i * NPEER + k - 1],
                       device_id_type=pl.DeviceIdType.MESH)
            relayed = modes[i] == "gather" and k in RELAYED
            if want == "recv":
                src = ins[i] if modes[i] == "gather" else ins[i].at[slots[i][1](pid)]
                out.append(pltpu.make_async_remote_copy(src_ref=src, dst_ref=slot(i, pid), device_id=peer, **sem))
            elif want == "first" and not relayed:
                src = ins[i] if modes[i] == "gather" else ins[i].at[slots[i][1](pid)]
                out.append(pltpu.make_async_remote_copy(src_ref=src, dst_ref=slot(i, me), device_id=peer, **sem))
            elif want == "relay" and relayed:
                origin = peer_of(k - 1)[1]
                out.append(((k - 2) * n + i, pltpu.make_async_remote_copy(
                    src_ref=slot(i, origin), dst_ref=slot(i, origin), device_id=sibling, **sem)))
    return out


def _comm_start(ins, outs, sems, modes, slots):
    for cp in _comm_copies(ins, outs, sems, modes, slots, "local") + _comm_copies(ins, outs, sems, modes, slots, "first"):
        cp.start()


def _comm_wait(ins, outs, sems, modes, slots):
    recvs = _comm_copies(ins, outs, sems, modes, slots, "recv")
    relays = _comm_copies(ins, outs, sems, modes, slots, "relay")
    for idx, cp in relays:
        recvs[idx].wait_recv()
        cp.start()
    passed_on = {idx for idx, _ in relays}
    for idx, cp in enumerate(recvs):
        if idx not in passed_on:
            cp.wait_recv()
    for cp in _comm_copies(ins, outs, sems, modes, slots, "first") + [cp for _, cp in relays]:
        cp.wait_send()
    for cp in _comm_copies(ins, outs, sems, modes, slots, "local"):
        cp.wait()


def _exchange(arrays, modes, slots, name):
    n = len(arrays)

    def body(*refs):
        ins, outs, sems = refs[:n], refs[n:2 * n], refs[2 * n:]
        _comm_start(ins, outs, sems, modes, slots)
        _comm_wait(ins, outs, sems, modes, slots)

    return pl.pallas_call(
        body, name=name, out_shape=_comm_out_shape(arrays, modes, slots),
        in_specs=[ANY] * n, out_specs=[ANY] * n, scratch_shapes=_comm_sems(n),
    )(*arrays)


HBM_SPEC = pl.BlockSpec(memory_space=pltpu.HBM)
SEM_SPEC = pl.BlockSpec(memory_space=pltpu.SEMAPHORE)


def _scatter_start(arr, slot, name):
    def body(v_ref, land_ref, send_sems, recv_sems, v_thru, land_thru, token):
        for cp in _comm_copies([v_ref], [land_ref], (send_sems, recv_sems, None), ["scatter"], [slot], "first"):
            cp.start()
        token[...] = jnp.zeros_like(token)

    return pl.pallas_call(
        body, name=name,
        out_shape=(pltpu.SemaphoreType.DMA((NPEER,)), pltpu.SemaphoreType.DMA((NPEER,)),
                   pltpu.HBM(arr.shape, arr.dtype), pltpu.HBM(arr.shape, arr.dtype), _sds((8, LANES))),
        in_specs=(HBM_SPEC, HBM_SPEC),
        out_specs=(SEM_SPEC, SEM_SPEC, HBM_SPEC, HBM_SPEC, pl.BlockSpec(memory_space=pltpu.VMEM)),
        input_output_aliases={0: 2, 1: 3},
        compiler_params=pltpu.CompilerParams(has_side_effects=pltpu.SideEffectType.DATAFLOW_SIDE_EFFECTING),
    )(pltpu.with_memory_space_constraint(arr, pltpu.HBM),
      pltpu.with_memory_space_constraint(lax.empty(arr.shape, arr.dtype), pltpu.HBM))


def _scatter_wait(send_sems, recv_sems, v_thru, land_thru, after, slot, name):
    def body(v_ref, land_ref, send_sems, recv_sems, *rest):
        sems = (send_sems, recv_sems, None)
        for cp in _comm_copies([v_ref], [land_ref], sems, ["scatter"], [slot], "first"):
            cp.wait_send()
        for cp in _comm_copies([v_ref], [land_ref], sems, ["scatter"], [slot], "recv"):
            cp.wait_recv()

    return pl.pallas_call(
        body, name=name, out_shape=(pltpu.HBM(v_thru.shape, v_thru.dtype), pltpu.HBM(v_thru.shape, v_thru.dtype)),
        in_specs=(HBM_SPEC, HBM_SPEC, SEM_SPEC, SEM_SPEC) + (ANY,) * len(after), out_specs=(HBM_SPEC, HBM_SPEC),
        input_output_aliases={0: 0, 1: 1},
        compiler_params=pltpu.CompilerParams(has_side_effects=pltpu.SideEffectType.DATAFLOW_SIDE_EFFECTING),
    )(v_thru, land_thru, send_sems, recv_sems, *after)


def _pcall(body, *, name, grid, in_specs, out_specs, out_shape, args, scratch_shapes=(), comm=None):
    if comm is None:
        return pl.pallas_call(body, name=name, grid=grid, in_specs=in_specs, out_specs=out_specs, out_shape=out_shape,
                              scratch_shapes=list(scratch_shapes))(*args), None
    arrays, modes, slots = comm
    n_in, n_out, n_scr, nc = len(args), len(out_shape), len(scratch_shapes), len(arrays)

    def hosted(*refs):
        a_in, c_in = refs[:n_in], refs[n_in:n_in + nc]
        o = n_in + nc
        a_out, c_out = refs[o:o + n_out], refs[o + n_out:o + n_out + nc]
        o += n_out + nc
        a_scr, sems = refs[o:o + n_scr], refs[o + n_scr:]
        first = pl.program_id(0) == 0
        last = pl.program_id(0) == grid[0] - 1
        for ax in range(1, len(grid)):
            first = jnp.logical_and(first, pl.program_id(ax) == 0)
            last = jnp.logical_and(last, pl.program_id(ax) == grid[ax] - 1)

        @pl.when(first)
        def _():
            _comm_start(c_in, c_out, sems, modes, slots)

        body(*a_in, *a_out, *a_scr)

        @pl.when(last)
        def _():
            _comm_wait(c_in, c_out, sems, modes, slots)

    res = pl.pallas_call(
        hosted, name=name, grid=grid, in_specs=list(in_specs) + [ANY] * nc, out_specs=list(out_specs) + [ANY] * nc,
        out_shape=list(out_shape) + _comm_out_shape(arrays, modes, slots),
        scratch_shapes=list(scratch_shapes) + _comm_sems(nc),
    )(*args, *arrays)
    return res[:n_out], res[n_out:]


SLOT8 = ((NDEV,), lambda d: (d,))
SLOT42 = ((4, 2), lambda d: (lax.rem(d, 4), lax.div(d, 4)))


def _ffn_in_fwd(x, g, wg4, name, comm=None):
    t = x.shape[0]
    tm = _tile(t, TM_MXU)
    sub = _tile(tm, TM_SUB)

    def body(x_ref, g_ref, w_ref, xn_ref, gu_ref, h_ref):
        for s in range(tm // sub):
            rows = pl.ds(s * sub, sub)
            xn = _rms_fwd(x_ref[rows, :], g_ref[...]).astype(bf16)
            xn_ref[rows, :] = xn
            gate = _dot_nt(xn, w_ref[0, 0])
            up = _dot_nt(xn, w_ref[0, 1])
            gu_ref[0, 0, rows, :] = gate.astype(bf16)
            gu_ref[0, 1, rows, :] = up.astype(bf16)
            h_ref[0, rows, :] = (gate * jax.nn.sigmoid(gate) * up).astype(bf16)

    return _pcall(
        body, name=name, grid=(t // tm, 4),
        in_specs=[pl.BlockSpec((tm, D), lambda i, e: (i, 0)), _full((1, D)),
                  pl.BlockSpec((1, 2, FB, D), lambda i, e: (e, 0, 0, 0))],
        out_specs=[pl.BlockSpec((tm, D), lambda i, e: (i, 0)),
                   pl.BlockSpec((1, 2, tm, FB), lambda i, e: (e, 0, i, 0)),
                   pl.BlockSpec((1, tm, FB), lambda i, e: (e, i, 0))],
        out_shape=[_sds((t, D), bf16), _sds((4, 2, t, FB), bf16), _sds((4, t, FB), bf16)],
        args=(x, g, wg4), comm=comm)


def _mm_norm_res_fwd(hb, wb, xres, g, scale, name, comm=None, tgt=None):
    e_n, t, k = hb.shape
    tm = _tile(t, TM_MXU)

    def body(h_ref, w_ref, x_ref, g_ref, *rest):
        f = _dot(h_ref[0], w_ref[0])
        for e in range(1, e_n):
            f = f + _dot(h_ref[e], w_ref[e])
        xnew = x_ref[...] + scale * _rms_fwd(f, g_ref[...])
        if tgt is None:
            f_ref, o_ref = rest
            o_ref[...] = xnew
        else:
            t_ref, f_ref, o_ref, l_ref = rest
            err = xnew - t_ref[...]
            o_ref[...] = err * (1.0 / D)
            part = 0.5 * jnp.sum(jnp.mean(err * err, axis=-1, keepdims=True), axis=0, keepdims=True)

            @pl.when(pl.program_id(0) == 0)
            def _():
                l_ref[...] = part

            @pl.when(pl.program_id(0) != 0)
            def _():
                l_ref[...] += part
        f_ref[...] = f

    tile = pl.BlockSpec((tm, D), lambda i: (i, 0))
    outs, got = _pcall(
        body, name=name, grid=(t // tm,),
        in_specs=[pl.BlockSpec((e_n, tm, k), lambda i: (0, i, 0)), _full((e_n, k, D)), tile, _full((1, D))]
        + ([] if tgt is None else [tile]),
        out_specs=[tile, tile] + ([] if tgt is None else [_full((1, 1))]),
        out_shape=[_sds((t, D)), _sds((t, D))] + ([] if tgt is None else [_sds((1, 1))]),
        args=(hb, wb, xres, g) + (() if tgt is None else (tgt,)), comm=comm)
    return outs if comm is None else (outs, got)


def _post_bwd_nt(dxn, f, g, wb, scale, gu, name, comm=None):
    e_n, k, _ = wb.shape
    t = f.shape[0]
    tm = _tile(t, TM_MXU)
    sub = _tile(tm, TM_SUB)
    swiglu = gu is not None

    def body(*refs):
        if swiglu:
            dx_ref, f_ref, g_ref, w_ref, gu_ref, df_ref, dg_ref, dh_ref, df_s = refs
        else:
            dx_ref, f_ref, g_ref, w_ref, df_ref, dg_ref, dh_ref, df_s = refs
        i, e = pl.program_id(0), pl.program_id(1)

        @pl.when(e == 0)
        def _():
            df, dg = _rms_bwd(f_ref[...], g_ref[...], scale * dx_ref[...])
            df_s[...] = df.astype(bf16)
            df_ref[...] = df_s[...]

            @pl.when(i == 0)
            def _():
                dg_ref[...] = dg

            @pl.when(i != 0)
            def _():
                dg_ref[...] += dg

        for s in range(tm // sub):
            rows = pl.ds(s * sub, sub)
            dh = _dot_nt(df_s[rows, :], w_ref[0])
            if swiglu:
                gate = gu_ref[0, 0, rows, :].astype(f32)
                up = gu_ref[0, 1, rows, :].astype(f32)
                sg = jax.nn.sigmoid(gate)
                dh_ref[0, 0, rows, :] = (dh * up * (sg * (1.0 + gate * (1.0 - sg)))).astype(bf16)
                dh_ref[0, 1, rows, :] = (dh * gate * sg).astype(bf16)
            else:
                dh_ref[0, rows, :] = dh

    in_specs = [pl.BlockSpec((tm, D), lambda i, e: (i, 0)), pl.BlockSpec((tm, D), lambda i, e: (i, 0)), _full((1, D)),
                pl.BlockSpec((1, k, D), lambda i, e: (e, 0, 0))]
    args = [dxn, f, g, wb]
    if swiglu:
        in_specs.append(pl.BlockSpec((1, 2, tm, k), lambda i, e: (e, 0, i, 0)))
        args.append(gu)
        dh_spec = pl.BlockSpec((1, 2, tm, k), lambda i, e: (e, 0, i, 0))
        dh_shape = _sds((e_n, 2, t, k), bf16)
    else:
        dh_spec = pl.BlockSpec((1, tm, k), lambda i, e: (e, i, 0))
        dh_shape = _sds((e_n, t, k), f32)
    return _pcall(
        body, name=name, grid=(t // tm, e_n), in_specs=in_specs,
        out_specs=[pl.BlockSpec((tm, D), lambda i, e: (i, 0)), _full((1, D)), dh_spec],
        out_shape=[_sds((t, D), bf16), _sds((1, D)), dh_shape],
        scratch_shapes=[pltpu.VMEM((tm, D), bf16)], args=args, comm=comm)


def _nt_pre_bwd(dy, wb, x, g, dres, name, comm=None):
    e_n, q_n, t, k = dy.shape
    tm = _tile(t, TM_MXU)

    def body(dy_ref, w_ref, x_ref, g_ref, r_ref, dx_ref, dg_ref, acc):
        i, e = pl.program_id(0), pl.program_id(1)
        p = _dot(dy_ref[0, 0], w_ref[0, 0])
        for q in range(1, q_n):
            p = p + _dot(dy_ref[0, q], w_ref[0, q])

        @pl.when(e == 0)
        def _():
            acc[...] = p

        @pl.when(e != 0)
        def _():
            acc[...] += p

        @pl.when(e == e_n - 1)
        def _():
            dx, dg = _rms_bwd(x_ref[...], g_ref[...], acc[...])
            dx_ref[...] = r_ref[...] + dx

            @pl.when(i == 0)
            def _():
                dg_ref[...] = dg

            @pl.when(i != 0)
            def _():
                dg_ref[...] += dg

    return _pcall(
        body, name=name, grid=(t // tm, e_n),
        in_specs=[pl.BlockSpec((1, q_n, tm, k), lambda i, e: (e, 0, i, 0)),
                  pl.BlockSpec((1, q_n, k, D), lambda i, e: (e, 0, 0, 0)),
                  pl.BlockSpec((tm, D), lambda i, e: (i, 0)), _full((1, D)),
                  pl.BlockSpec((tm, D), lambda i, e: (i, 0))],
        out_specs=[pl.BlockSpec((tm, D), lambda i, e: (i, 0)), _full((1, D))],
        out_shape=[_sds((t, D)), _sds((1, D))],
        scratch_shapes=[pltpu.VMEM((tm, D), f32)], args=(dy, wb, x, g, dres), comm=comm)


def _wgrad(a, b, a_batched, b_batched, tn, name, comm=None):
    t, k = a.shape[1], a.shape[2]
    nn = b.shape[2]
    nb = max(a.shape[0], b.shape[0])
    tt = _tile(t, TM_MXU)
    nt = t // tt

    def body(a_ref, b_ref, o_ref, acc):
        @pl.when(pl.program_id(2) == 0)
        def _():
            acc[...] = _dot_tn(a_ref[0], b_ref[0])

        @pl.when(pl.program_id(2) != 0)
        def _():
            acc[...] += _dot_tn(a_ref[0], b_ref[0])

        @pl.when(pl.program_id(2) == nt - 1)
        def _():
            o_ref[0] = acc[...].astype(bf16)

    (out,), got = _pcall(
        body, name=name, grid=(nb, nn // tn, nt),
        in_specs=[pl.BlockSpec((1, tt, k), (lambda n, j, s: (n, s, 0)) if a_batched else (lambda n, j, s: (0, s, 0))),
                  pl.BlockSpec((1, tt, tn), (lambda n, j, s: (n, s, j)) if b_batched else (lambda n, j, s: (0, s, j)))],
        out_specs=[pl.BlockSpec((1, k, tn), lambda n, j, s: (n, 0, j))],
        out_shape=[_sds((nb, k, nn), bf16)],
        scratch_shapes=[pltpu.VMEM((k, tn), f32)], args=(a, b), comm=comm)
    return out if comm is None else (out, got)


def _norm_mm_fwd(x, g, w, name):
    t = x.shape[0]
    tm = _tile(t, TM_VPU)

    def body(x_ref, g_ref, w_ref, xn_ref, ps_ref, pc_ref):
        xn = _rms_fwd(x_ref[...], g_ref[...]).astype(bf16)
        xn_ref[...] = xn
        ps_ref[...] = _dot_nt(xn, w_ref[:DS, :])
        pc_ref[...] = _dot_nt(xn, w_ref[DS:, :])

    return pl.pallas_call(
        body, name=name, grid=(t // tm,),
        in_specs=[pl.BlockSpec((tm, D), lambda i: (i, 0)), _full((1, D)), _full((DIN, D))],
        out_specs=[pl.BlockSpec((tm, D), lambda i: (i, 0)), pl.BlockSpec((tm, DS), lambda i: (i, 0)),
                   pl.BlockSpec((tm, 2 * DC), lambda i: (i, 0))],
        out_shape=[_sds((t, D), bf16), _sds((t, DS)), _sds((t, 2 * DC))],
    )(x, g, w)


def _segsum_b(x, seg, seg_t):
    return mm(mm(x, seg, "nn", EB), seg_t, "nn", EB)


def _prep_fn(psl, w0, a0, k_k, k_a, wup, aup, gup, seg, seg_t):
    r, k, v = psl[:, :DR], psl[:, DR:2 * DR], psl[:, 2 * DR:3 * DR]
    xwa, xg = psl[:, 3 * DR:3 * DR + LANES], psl[:, 3 * DR + LANES:]
    d = w0 + mm(jnp.tanh(xwa), wup, "nn", LORA)
    lw = -DECAY_SCALE * jax.nn.sigmoid(d)
    a = jax.nn.sigmoid(a0 + mm(xwa, aup, "nn", LORA))
    g = mm(jax.nn.sigmoid(xg), gup, "nn", LORA)
    kkr = k * k_k
    kk = kkr * lax.rsqrt(jnp.maximum(_segsum_b(kkr * kkr, seg, seg_t), 1e-12))
    kp = k * (1.0 + (a - 1.0) * k_a)
    return r, lw, kp, v, -kk, kk * a, g


def _shifted(ps, halo_row, first):
    prev = jnp.where(first, 0.0, halo_row)
    sh = pltpu.roll(ps, 1, 0)
    row = lax.broadcasted_iota(jnp.int32, ps.shape, 0)
    return jnp.where(row == 0, prev, sh)


def _heads_of(ref, rows):
    return jnp.stack([ref[rows, h * HN:(h + 1) * HN] for h in range(NH)], axis=0)


def _heads_to(ref, rows, val):
    ref[rows, :] = jnp.concatenate([val[h] for h in range(NH)], axis=-1)


_PREP_PARAM_SHAPES = [(1, DS), (1, DR), (1, DR), (1, DR), (1, DR), (LANES, DR), (LANES, DR), (LANES, DR),
                      (DR, LANES), (LANES, DR)]


def _prep_fwd(ps, params, name, comm=None):
    t = ps.shape[0]
    tm = _tile(t, TM_VPU)
    hb = tm // 8

    def body(ps_ref, halo_ref, mu_ref, *rest):
        prm = [r[...] for r in rest[:9]]
        outs = rest[9:]
        x = ps_ref[...]
        sh = _shifted(x, halo_ref[pl.ds(7, 1), :], pl.program_id(0) == 0)
        psl = x + (sh - x) * mu_ref[...]
        vals = _prep_fn(psl, *prm)
        for ref, val in zip(outs, vals):
            ref[...] = val

    outs, got = _pcall(
        body, name=name, grid=(t // tm,),
        in_specs=[pl.BlockSpec((tm, DS), lambda i: (i, 0)),
                  pl.BlockSpec((8, DS), lambda i: (jnp.maximum(i * hb - 1, 0), 0))]
        + [_full(s) for s in _PREP_PARAM_SHAPES],
        out_specs=[pl.BlockSpec((tm, DR), lambda i: (i, 0))] * 7,
        out_shape=[_sds((t, DR))] * 7, args=(ps, ps, *params), comm=comm)
    return outs if comm is None else (outs, got)


def _prep_bwd(ps, params, cts_hm, cts_tm, name, comm=None):
    t = ps.shape[0]
    tm = _tile(t, TM_VPU)
    hb = tm // 8

    def body(ps_ref, halo_ref, mu_ref, *rest):
        prm = [r[...] for r in rest[:9]]
        chm = rest[9:15]
        ctm = rest[15:19]
        dpsl_ref = rest[19]
        gouts = rest[20:27]
        x = ps_ref[...]
        sh = _shifted(x, halo_ref[pl.ds(7, 1), :], pl.program_id(0) == 0)
        psl = x + (sh - x) * mu_ref[...]
        seg, seg_t = prm[7], prm[8]
        _, vjp = jax.vjp(lambda p, *w: _prep_fn(p, *w, seg, seg_t), psl, *prm[:7])
        c = [r[...] for r in chm]
        cts = (c[0] + ctm[0][...], c[1], c[2] + ctm[1][...], c[3] + ctm[2][...], c[4], c[5], ctm[3][...])
        grads = vjp(cts)
        dpsl_ref[...] = grads[0]
        for ref, gval in zip(gouts, grads[1:]):
            @pl.when(pl.program_id(0) == 0)
            def _(ref=ref, gval=gval):
                ref[...] = gval

            @pl.when(pl.program_id(0) != 0)
            def _(ref=ref, gval=gval):
                ref[...] += gval

    tk = pl.BlockSpec((tm, DR), lambda i: (i, 0))
    gshapes = _PREP_PARAM_SHAPES[1:8]
    outs, got = _pcall(
        body, name=name, grid=(t // tm,),
        in_specs=[pl.BlockSpec((tm, DS), lambda i: (i, 0)),
                  pl.BlockSpec((8, DS), lambda i: (jnp.maximum(i * hb - 1, 0), 0))]
        + [_full(s) for s in _PREP_PARAM_SHAPES] + [tk] * 10,
        out_specs=[pl.BlockSpec((tm, DS), lambda i: (i, 0))] + [_full(s) for s in gshapes],
        out_shape=[_sds((t, DS))] + [_sds(s) for s in gshapes],
        args=(ps, ps, *params, *cts_hm, *cts_tm), comm=comm)
    return outs if comm is None else (outs, got)


def _shift_bwd(dpsl, ps, mu, dpc, name):
    t = ps.shape[0]
    tm = _tile(t, TM_VPU)
    hb = tm // 8
    last_blk = t // 8 - 1

    def body(d_ref, dn_ref, ps_ref, halo_ref, mu_ref, dpc_ref, dp_ref, dmu_ref):
        i = pl.program_id(0)
        mu_v = mu_ref[...]
        d = d_ref[...]
        nxt = jnp.where(i == pl.num_programs(0) - 1, 0.0, dn_ref[pl.ds(0, 1), :])
        up = pltpu.roll(d, tm - 1, 0)
        row = lax.broadcasted_iota(jnp.int32, d.shape, 0)
        up = jnp.where(row == tm - 1, nxt, up)
        dp_ref[:, :DS] = (d * (1.0 - mu_v) + up * mu_v).astype(bf16)
        dp_ref[:, DS:] = dpc_ref[...].astype(bf16)
        x = ps_ref[...]
        sh = _shifted(x, halo_ref[pl.ds(7, 1), :], i == 0)
        dmu = jnp.sum(d * (sh - x), axis=0, keepdims=True)

        @pl.when(i == 0)
        def _():
            dmu_ref[...] = dmu

        @pl.when(i != 0)
        def _():
            dmu_ref[...] += dmu

    return pl.pallas_call(
        body, name=name, grid=(t // tm,),
        in_specs=[pl.BlockSpec((tm, DS), lambda i: (i, 0)),
                  pl.BlockSpec((8, DS), lambda i: (jnp.minimum((i + 1) * hb, last_blk), 0)),
                  pl.BlockSpec((tm, DS), lambda i: (i, 0)),
                  pl.BlockSpec((8, DS), lambda i: (jnp.maximum(i * hb - 1, 0), 0)),
                  _full((1, DS)), pl.BlockSpec((tm, 2 * DC), lambda i: (i, 0))],
        out_specs=[pl.BlockSpec((tm, DIN), lambda i: (i, 0)), _full((1, DS))],
        out_shape=[_sds((t, DIN), bf16), _sds((1, DS))],
    )(dpsl, dpsl, ps, ps, mu, dpc)


def _post_fn(y, r, kp, v, g, gn_w, gn_b, r_k, seg, seg_t):
    mu = _segsum_b(y, seg, seg_t) * (1.0 / HN)
    yc = y - mu
    var = _segsum_b(yc * yc, seg, seg_t) * (1.0 / HN)
    yo = yc * lax.rsqrt(var + GN_EPS) * gn_w + gn_b
    bonus = _segsum_b(r * kp * r_k, seg, seg_t) * v
    return (yo + bonus) * g


_POST_PARAM_SHAPES = [(1, DR), (1, DR), (1, DR), (DR, LANES), (LANES, DR)]


def _post_fwd(y, r, kp, v, g, out_b, params, name):
    t = g.shape[0]
    tm = _tile(t, TM_VPU)

    def body(y_ref, r_ref, k_ref, v_ref, g_ref, ob_ref, *rest):
        prm = [p[...] for p in rest[:5]]
        cat_ref = rest[5]
        oa = _post_fn(y_ref[...], r_ref[...], k_ref[...], v_ref[...], g_ref[...], *prm)
        cat_ref[:, :DR] = oa.astype(bf16)
        cat_ref[:, DR:] = ob_ref[...]

    tk = pl.BlockSpec((tm, DR), lambda i: (i, 0))
    return pl.pallas_call(
        body, name=name, grid=(t // tm,),
        in_specs=[tk] * 6 + [_full(s) for s in _POST_PARAM_SHAPES],
        out_specs=pl.BlockSpec((tm, 2 * DR), lambda i: (i, 0)),
        out_shape=_sds((t, 2 * DR), bf16),
    )(y, r, kp, v, g, out_b, *params)


def _post_bwd(dcat, y, r, kp, v, g, params, name, comm=None):
    t = g.shape[0]
    tm = _tile(t, TM_VPU)

    def body(dc_ref, y_ref, r_ref, k_ref, v_ref, g_ref, *rest):
        prm = [p[...] for p in rest[:5]]
        dy_ref, dr_ref, dk_ref, dv_ref, dg_ref = rest[5:10]
        gouts = rest[10:13]
        seg, seg_t = prm[3], prm[4]
        _, vjp = jax.vjp(lambda *a: _post_fn(*a, seg, seg_t),
                         y_ref[...], r_ref[...], k_ref[...], v_ref[...], g_ref[...], *prm[:3])
        grads = vjp(dc_ref[...])
        dy_ref[...] = grads[0]
        dr_ref[...] = grads[1]
        dk_ref[...] = grads[2]
        dv_ref[...] = grads[3]
        dg_ref[...] = grads[4]
        for ref, gval in zip(gouts, grads[5:]):
            @pl.when(pl.program_id(0) == 0)
            def _(ref=ref, gval=gval):
                ref[...] = gval

            @pl.when(pl.program_id(0) != 0)
            def _(ref=ref, gval=gval):
                ref[...] += gval

    tk = pl.BlockSpec((tm, DR), lambda i: (i, 0))
    return _pcall(
        body, name=name, grid=(t // tm,),
        in_specs=[tk] * 6 + [_full(s) for s in _POST_PARAM_SHAPES],
        out_specs=[tk] * 5 + [_full((1, DR))] * 3,
        out_shape=[_sds((t, DR))] * 5 + [_sds((1, DR))] * 3,
        args=(dcat, y, r, kp, v, g, *params), comm=comm)


@jax.custom_vjp
def _inv_unit_lower(lb):
    c = lb.shape[-1]
    row = lax.broadcasted_iota(jnp.int32, (c, c), 0)
    col = lax.broadcasted_iota(jnp.int32, (c, c), 1)
    p = (row == col).astype(f32)[None] + lb
    m = lb
    for _ in range(c.bit_length() - 2):
        m = mm(m, m, "nn", RIM)
        p = p + mm(m, p, "nn", RIM)
    return p


def _inv_unit_lower_fwd(lb):
    p = _inv_unit_lower(lb)
    return p, p


def _inv_unit_lower_bwd(p, ct):
    return (mm(mm(p, ct, "tn", RIM), p, "nt", RIM),)


_inv_unit_lower.defvjp(_inv_unit_lower_fwd, _inv_unit_lower_bwd)


@jax.custom_vjp
def _known_inverse(lb, p):
    return p


def _known_inverse_fwd(lb, p):
    return p, p


def _known_inverse_bwd(p, ct):
    return mm(mm(p, ct, "tn", RIM), p, "nt", RIM), jnp.zeros_like(p)


_known_inverse.defvjp(_known_inverse_fwd, _known_inverse_bwd)


def _r1_fn(r, lw, k, v, z, b, p_saved=None, with_p=False):
    c = r.shape[1]
    row = lax.broadcasted_iota(jnp.int32, (c, c), 0)
    col = lax.broadcasted_iota(jnp.int32, (c, c), 1)
    incl = (row >= col)[None]
    strict = (row > col)[None]
    eye = (row == col).astype(f32)[None]
    tri = jnp.broadcast_to((row >= col).astype(f32)[None], (r.shape[0], c, c))
    cum = mm(tri, lw, "nn", EA)
    tot = jnp.sum(lw, axis=1, keepdims=True)
    zt = z * jnp.exp(cum - lw)
    rt = r * jnp.exp(cum)
    g_inv = jnp.exp(-cum)
    g_rem = jnp.exp(tot - cum)
    bt, kt = b * g_inv, k * g_inv
    bh, kh = b * g_rem, k * g_rem
    lb = jnp.where(strict, mm(zt, bt, "nt", RGM), 0.0)
    lk = jnp.where(strict, mm(zt, kt, "nt", RGM), 0.0)
    mb = jnp.where(incl, mm(rt, bt, "nt", RGM), 0.0)
    mk = jnp.where(incl, mm(rt, kt, "nt", RGM), 0.0)
    p = _inv_unit_lower(lb) if p_saved is None else _known_inverse(lb, p_saved)
    w1 = mm(p, zt, "nn", RWM)
    w2 = mm(p, mm(lk, v, "nn", RWM), "nn", RWM)
    a = mm(w1, bh, "tn", RAM) + eye * jnp.exp(tot)
    g = mm(w2, bh, "tn", RAM) + mm(v, kh, "tn", RAM)
    q = rt + mm(mb, w1, "nn", RWM)
    yl = mm(mb, w2, "nn", RWM) + mm(mk, v, "nn", RWM)
    return (a, g, q, yl, p) if with_p else (a, g, q, yl)


def _chunks_in(ref, n):
    return jnp.concatenate([_heads_of(ref, pl.ds(s * CHUNK, CHUNK)) for s in range(n)], axis=0)


def _chunks_out(ref, val, n):
    for s in range(n):
        _heads_to(ref, pl.ds(s * CHUNK, CHUNK), val[s * NH:(s + 1) * NH])


def _r1_fwd(ins, name, comm=None):
    t = ins[0].shape[0]
    n = R1_CHUNKS
    nc = t // CHUNK

    def body(r, lw, k, v, z, b, a_ref, g_ref, p_ref, q_ref, yl_ref):
        a, g, q, yl, p = _r1_fn(*[_chunks_in(x, n) for x in (r, lw, k, v, z, b)], with_p=True)
        for s in range(n):
            a_ref[s] = a[s * NH:(s + 1) * NH]
            g_ref[s] = g[s * NH:(s + 1) * NH]
            p_ref[s] = p[s * NH:(s + 1) * NH]
        _chunks_out(q_ref, q, n)
        _chunks_out(yl_ref, yl, n)

    ck = pl.BlockSpec((n * CHUNK, DR), lambda c: (c, 0))
    st = pl.BlockSpec((n, NH, HN, HN), lambda c: (c, 0, 0, 0))
    return _pcall(
        body, name=name, grid=(nc // n,), in_specs=[ck] * 6, out_specs=[st, st, st, ck, ck],
        out_shape=[_sds((nc, NH, HN, HN))] * 3 + [_sds((t, DR))] * 2, args=tuple(ins), comm=comm)


def _r1_bwd(ins, p_inv, da, dg, dq, dyl, name, comm=None):
    t = ins[0].shape[0]
    n = R1_CHUNKS
    nc = t // CHUNK

    def body(r, lw, k, v, z, b, p_ref, da_ref, dg_ref, dq_ref, dyl_ref, *outs):
        p_saved = jnp.concatenate([p_ref[s] for s in range(n)], axis=0)
        _, vjp = jax.vjp(lambda *a: _r1_fn(*a, p_saved=p_saved), *[_chunks_in(x, n) for x in (r, lw, k, v, z, b)])
        cts = (jnp.concatenate([da_ref[s] for s in range(n)], axis=0),
               jnp.concatenate([dg_ref[s] for s in range(n)], axis=0), _chunks_in(dq_ref, n), _chunks_in(dyl_ref, n))
        for ref, gval in zip(outs, vjp(cts)):
            _chunks_out(ref, gval, n)

    ck = pl.BlockSpec((n * CHUNK, DR), lambda c: (c, 0))
    st = pl.BlockSpec((n, NH, HN, HN), lambda c: (c, 0, 0, 0))
    return _pcall(
        body, name=name, grid=(nc // n,), in_specs=[ck] * 6 + [st, st, st, ck, ck], out_specs=[ck] * 6,
        out_shape=[_sds((t, DR))] * 6, args=(*ins, p_inv, da, dg, dq, dyl), comm=comm)


def _r2_fwd(a, g, q, yl, name):
    nc = a.shape[0]
    t = q.shape[0]

    def body(a_ref, g_ref, q_ref, yl_ref, y_ref, s_ref, s):
        @pl.when(pl.program_id(0) == 0)
        def _():
            s[...] = jnp.zeros_like(s)
        s0 = s[...]
        for j in range(n):
            rows = pl.ds(j * CHUNK, CHUNK)
            s_ref[j] = s0
            _heads_to(y_ref, rows, mm(_heads_of(q_ref, rows), s0, "nt", R2M) + _heads_of(yl_ref, rows))
            s0 = mm(s0, a_ref[j], "nn", R2M) + g_ref[j]
        s[...] = s0

    n = math.gcd(nc, R2_CHUNKS)
    ck = pl.BlockSpec((n * CHUNK, DR), lambda c: (c, 0))
    st = pl.BlockSpec((n, NH, HN, HN), lambda c: (c, 0, 0, 0))
    return pl.pallas_call(
        body, name=name, grid=(nc // n,), in_specs=[st, st, ck, ck], out_specs=[ck, st],
        out_shape=[_sds((t, DR)), _sds((nc, NH, HN, HN))],
        scratch_shapes=[pltpu.VMEM((NH, HN, HN), f32)],
    )(a, g, q, yl)


def _r2_bwd(dy, q, s_all, a, name):
    nc = a.shape[0]
    t = q.shape[0]

    def body(dy_ref, q_ref, s_ref, a_ref, dq_ref, da_ref, dg_ref, ds):
        @pl.when(pl.program_id(0) == 0)
        def _():
            ds[...] = jnp.zeros_like(ds)
        dsn = ds[...]
        for j in reversed(range(n)):
            rows = pl.ds(j * CHUNK, CHUNK)
            s0 = s_ref[j]
            dyv = _heads_of(dy_ref, rows)
            dg_ref[j] = dsn
            da_ref[j] = mm(s0, dsn, "tn", R2M)
            _heads_to(dq_ref, rows, mm(dyv, s0, "nn", R2M))
            dsn = mm(dsn, a_ref[j], "nt", R2M) + mm(dyv, _heads_of(q_ref, rows), "tn", R2M)
        ds[...] = dsn

    n = math.gcd(nc, R2_CHUNKS)
    nb = nc // n
    ck = pl.BlockSpec((n * CHUNK, DR), lambda c: (nb - 1 - c, 0))
    st = pl.BlockSpec((n, NH, HN, HN), lambda c: (nb - 1 - c, 0, 0, 0))
    return pl.pallas_call(
        body, name=name, grid=(nb,), in_specs=[ck, ck, st, st], out_specs=[ck, st, st],
        out_shape=[_sds((t, DR)), _sds((nc, NH, HN, HN)), _sds((nc, NH, HN, HN))],
        scratch_shapes=[pltpu.VMEM((NH, HN, HN), f32)],
    )(dy, q, s_all, a)


def _ln_silu(c, w, b):
    mu = jnp.mean(c, axis=-1, keepdims=True)
    cc = c - mu
    var = jnp.mean(cc * cc, axis=-1, keepdims=True)
    u = cc * lax.rsqrt(var + LN_EPS) * w + b
    return u * jax.nn.sigmoid(u)


def _glu_tile(pc):
    return pc[:, :DC] * jax.nn.sigmoid(pc[:, DC:])


def _fill_glu(s_ref, pc_ref, halo_ref, first, tm):
    s_ref[pl.ds(0, HALO), :] = jnp.where(first, 0.0, _glu_tile(halo_ref[...]))
    s_ref[pl.ds(HALO, tm), :] = _glu_tile(pc_ref[...])


def _phase_copies(sh):
    n = sh.shape[1] - 8
    for r in range(1, 8):
        sh[r, pl.ds(0, n), :] = sh[0, pl.ds(r, n), :]


def _rows_at(sh, o, n):
    return sh[o % 8, pl.ds(o - o % 8, n), :]


def _conv_fwd(pc, dw, cb, lnw, lnb, name):
    t = pc.shape[0]
    tm = _tile(t, TM_VPU)
    hb = tm // HALO

    def body(pc_ref, halo_ref, dw_ref, cb_ref, w_ref, b_ref, c_ref, o_ref, s):
        _fill_glu(s.at[0], pc_ref, halo_ref, pl.program_id(0) == 0, tm)
        _phase_copies(s)
        for r0 in range(0, tm, CONV_ROWS):
            rows = pl.ds(r0, CONV_ROWS)
            acc = _rows_at(s, r0 + HALO - CW + 1, CONV_ROWS) * dw_ref[pl.ds(0, 1), :]
            for j in range(1, CW):
                acc = acc + _rows_at(s, r0 + HALO - CW + 1 + j, CONV_ROWS) * dw_ref[pl.ds(j, 1), :]
            c = acc + cb_ref[...]
            c_ref[rows, :] = c
            o_ref[rows, :] = _ln_silu(c, w_ref[...], b_ref[...]).astype(bf16)

    return pl.pallas_call(
        body, name=name, grid=(t // tm,),
        in_specs=[pl.BlockSpec((tm, 2 * DC), lambda i: (i, 0)),
                  pl.BlockSpec((HALO, 2 * DC), lambda i: (jnp.maximum(i * hb - 1, 0), 0)),
                  _full((CW, DC)), _full((1, DC)), _full((1, DC)), _full((1, DC))],
        out_specs=[pl.BlockSpec((tm, DC), lambda i: (i, 0)), pl.BlockSpec((tm, DC), lambda i: (i, 0))],
        out_shape=[_sds((t, DC)), _sds((t, DC), bf16)],
        scratch_shapes=[pltpu.VMEM((8, HALO + tm, DC), f32)],
    )(pc, pc, dw, cb, lnw, lnb)


def _conv_bwd1(dcat, c, lnw, lnb, name):
    t = c.shape[0]
    tm = _tile(t, TM_VPU)

    def body(d_ref, c_ref, w_ref, b_ref, dc_ref, dw_ref, db_ref, dcb_ref):
        _, vjp = jax.vjp(_ln_silu, c_ref[...], w_ref[...], b_ref[...])
        dc, dw, db = vjp(d_ref[...])
        dc_ref[...] = dc
        dcb = jnp.sum(dc, axis=0, keepdims=True)
        for ref, gval in ((dw_ref, dw), (db_ref, db), (dcb_ref, dcb)):
            @pl.when(pl.program_id(0) == 0)
            def _(ref=ref, gval=gval):
                ref[...] = gval

            @pl.when(pl.program_id(0) != 0)
            def _(ref=ref, gval=gval):
                ref[...] += gval

    tk = pl.BlockSpec((tm, DC), lambda i: (i, 0))
    return pl.pallas_call(
        body, name=name, grid=(t // tm,),
        in_specs=[pl.BlockSpec((tm, DC), lambda i: (i, 1)), tk, _full((1, DC)), _full((1, DC))],
        out_specs=[tk] + [_full((1, DC))] * 3,
        out_shape=[_sds((t, DC))] + [_sds((1, DC))] * 3,
    )(dcat, c, lnw, lnb)


def _conv_bwd2(dc, pc, dw, name, comm=None):
    t = pc.shape[0]
    tm = _tile(t, TM_VPU)
    hb = tm // HALO
    last_blk = t // HALO - 1

    def body(dc_ref, dn_ref, pc_ref, halo_ref, dw_ref, dpc_ref, ddw_ref, s, sd, acc):
        i = pl.program_id(0)
        _fill_glu(s.at[0], pc_ref, halo_ref, i == 0, tm)
        sd[0, pl.ds(0, tm), :] = dc_ref[...]
        sd[0, pl.ds(tm, HALO), :] = jnp.where(i == pl.num_programs(0) - 1, 0.0, dn_ref[...])
        _phase_copies(s)
        _phase_copies(sd)
        for r0 in range(0, tm, CONV_ROWS):
            rows = pl.ds(r0, CONV_ROWS)
            dcb = sd[0, rows, :]
            dglu = None
            for j in range(CW):
                term = _rows_at(sd, r0 + CW - 1 - j, CONV_ROWS) * dw_ref[pl.ds(j, 1), :]
                dglu = term if dglu is None else dglu + term
                part = dcb * _rows_at(s, r0 + HALO - CW + 1 + j, CONV_ROWS)
                part8 = part[0:8]
                for q in range(8, CONV_ROWS, 8):
                    part8 = part8 + part[q:q + 8]
                if r0 == 0:
                    acc[pl.ds(8 * j, 8), :] = part8
                else:
                    acc[pl.ds(8 * j, 8), :] += part8
            a = pc_ref[rows, :DC]
            sg = jax.nn.sigmoid(pc_ref[rows, DC:])
            dpc_ref[rows, :DC] = dglu * sg
            dpc_ref[rows, DC:] = dglu * a * sg * (1.0 - sg)
        rows = [jnp.sum(acc[pl.ds(8 * j, 8), :], axis=0, keepdims=True) for j in range(CW)]

        @pl.when(i == 0)
        def _():
            for j in range(CW):
                ddw_ref[pl.ds(j, 1), :] = rows[j]

        @pl.when(i != 0)
        def _():
            for j in range(CW):
                ddw_ref[pl.ds(j, 1), :] += rows[j]

    outs, got = _pcall(
        body, name=name, grid=(t // tm,),
        in_specs=[pl.BlockSpec((tm, DC), lambda i: (i, 0)),
                  pl.BlockSpec((HALO, DC), lambda i: (jnp.minimum((i + 1) * hb, last_blk), 0)),
                  pl.BlockSpec((tm, 2 * DC), lambda i: (i, 0)),
                  pl.BlockSpec((HALO, 2 * DC), lambda i: (jnp.maximum(i * hb - 1, 0), 0)),
                  _full((CW, DC))],
        out_specs=[pl.BlockSpec((tm, 2 * DC), lambda i: (i, 0)), _full((CW, DC))],
        out_shape=[_sds((t, 2 * DC)), _sds((CW, DC))],
        scratch_shapes=[pltpu.VMEM((8, HALO + tm, DC), f32), pltpu.VMEM((8, tm + HALO, DC), f32),
                        pltpu.VMEM((8 * CW, DC), f32)], args=(dc, dc, pc, pc, dw), comm=comm)
    return outs if comm is None else (outs, got)


def _adamw(w, g, m, v):
    m = ADAM_B1 * m + (1.0 - ADAM_B1) * g
    v = ADAM_B2 * v + (1.0 - ADAM_B2) * (g * g)
    m_hat = m / (1.0 - ADAM_B1 ** ADAM_STEP)
    v_hat = v / (1.0 - ADAM_B2 ** ADAM_STEP)
    delta = -ADAM_LR * (m_hat / (jnp.sqrt(v_hat) + ADAM_EPS) + ADAM_WD * w)
    return delta, m, v


def _adam_slots(w, slots, m, v, name):
    r, c = w.shape
    tr = next(cand for cand in (512, 352, 256, 128, r) if r % cand == 0)

    def body(w_ref, s_ref, m_ref, v_ref, g_ref, d_ref, nm_ref, nv_ref):
        g = s_ref[0].astype(f32)
        for k in range(1, NDEV):
            g = g + s_ref[k].astype(f32)
        delta, nm, nv = _adamw(w_ref[...], g, m_ref[...], v_ref[...])
        g_ref[...] = g
        d_ref[...] = delta
        nm_ref[...] = nm
        nv_ref[...] = nv

    blk = pl.BlockSpec((tr, c), lambda i: (i, 0))
    return pl.pallas_call(
        body, name=name, grid=(r // tr,),
        in_specs=[blk, pl.BlockSpec((NDEV, tr, c), lambda i: (0, i, 0)), blk, blk],
        out_specs=[blk] * 4, out_shape=[_sds((r, c))] * 4,
    )(w, slots, m, v)


def _sum_slots(slots, name):
    _, r, c = slots.shape

    def body(s_ref, o_ref):
        g = s_ref[0]
        for k in range(1, NDEV):
            g = g + s_ref[k]
        o_ref[...] = g

    return pl.pallas_call(body, name=name, in_specs=[_full((NDEV, r, c))], out_specs=_full((r, c)),
                          out_shape=_sds((r, c)), grid=(1,))(slots)


def _adam_many(ws, gs, ms, vs, name):
    n = len(ws)
    shapes = [w.shape for w in ws]

    def body(*refs):
        for i in range(n):
            w_ref, g_ref, m_ref, v_ref = refs[i], refs[n + i], refs[2 * n + i], refs[3 * n + i]
            delta, nm, nv = _adamw(w_ref[...], g_ref[...], m_ref[...], v_ref[...])
            refs[4 * n + i][...] = delta
            refs[5 * n + i][...] = nm
            refs[6 * n + i][...] = nv

    outs = pl.pallas_call(body, name=name, grid=(1,), in_specs=[_full(s) for s in shapes] * 4,
                          out_specs=[_full(s) for s in shapes] * 3, out_shape=[_sds(s) for s in shapes] * 3,
                          )(*ws, *gs, *ms, *vs)
    return outs[:n], outs[n:2 * n], outs[2 * n:]


def _pack(pieces, total):
    flat = []
    n = 0
    for p in pieces:
        p = p.reshape(-1)
        pad = (-p.shape[0]) % LANES
        if pad:
            p = jnp.pad(p, (0, pad))
        flat.append(p)
        n += p.shape[0]
    if total > n:
        flat.append(jnp.zeros((total - n,), f32))
    return jnp.concatenate(flat).reshape(total // LANES, LANES)


def _unpack(vec, shapes):
    flat = vec.reshape(-1)
    out, off = [], 0
    for s in shapes:
        n = math.prod(s)
        out.append(flat[off:off + n].reshape(s))
        off += n + (-n) % LANES
    return out


def _round_up(n, m):
    return (n + m - 1) // m * m


def kernel(x, ffn1_norm_pre, ffn1_norm_post, ffn1_w_gu, ffn1_w_down, mix_norm_pre, mix_norm_post, w_in, shift_mu, w_up, w0, a_up, a0, g_up, k_k, k_a, r_k, gn_w, gn_b, conv_dw, conv_b, conv_ln_w, conv_ln_b, w_out, ffn2_norm_pre, ffn2_norm_post, ffn2_w_gu, ffn2_w_down, loss_target, m_ffn1_norm_pre, m_ffn1_norm_post, m_ffn1_w_gu, m_ffn1_w_down, m_mix_norm_pre, m_mix_norm_post, m_w_in, m_shift_mu, m_w_up, m_w0, m_a_up, m_a0, m_g_up, m_k_k, m_k_a, m_r_k, m_gn_w, m_gn_b, m_conv_dw, m_conv_b, m_conv_ln_w, m_conv_ln_b, m_w_out, m_ffn2_norm_pre, m_ffn2_norm_post, m_ffn2_w_gu, m_ffn2_w_down, v_ffn1_norm_pre, v_ffn1_norm_post, v_ffn1_w_gu, v_ffn1_w_down, v_mix_norm_pre, v_mix_norm_post, v_w_in, v_shift_mu, v_w_up, v_w0, v_a_up, v_a0, v_g_up, v_k_k, v_k_a, v_r_k, v_gn_w, v_gn_b, v_conv_dw, v_conv_b, v_conv_ln_w, v_conv_ln_b, v_w_out, v_ffn2_norm_pre, v_ffn2_norm_post, v_ffn2_w_gu, v_ffn2_w_down):
    named = dict(locals())
    t = x.shape[1]
    x0 = x.reshape(t, D)
    tgt = loss_target.reshape(t, D)

    def shard(a):
        return a[0].astype(bf16)

    def shard_t(a):
        return jnp.swapaxes(a[0], 0, 1).astype(bf16)

    (wg1,) = _exchange([shard_t(ffn1_w_gu)], ["gather"], [SLOT42], "gather_wg1")

    def cols(a):
        return jnp.transpose(a, (1, 0, 2)).reshape(a.shape[1], NDEV * a.shape[2])

    lane = jnp.arange(DR, dtype=jnp.int32) // HN
    seg = (lane[:, None] == jnp.arange(LANES, dtype=jnp.int32)[None, :]).astype(f32)
    seg_t = seg.T
    rk_row = r_k.reshape(1, DR)
    post_params = (gn_w, gn_b, rk_row, seg, seg_t)

    (xn1, gu1, h1), (wd1, win_g) = _ffn_in_fwd(x0, ffn1_norm_pre, wg1, "ffn1_in_fwd", comm=(
        [shard(ffn1_w_down), shard_t(w_in)], ["gather"] * 2, [SLOT8] * 2))
    wd1 = wd1.reshape(4, FB, D)
    win_t = win_g.reshape(DIN, D)
    (f1, x1), (wout_g, wup_g, aup_g, gup_g, cdw_g) = _mm_norm_res_fwd(
        h1, wd1, x0, ffn1_norm_post, 0.5, "ffn1_out_fwd", comm=(
            [shard(w_out), w_up[0], a_up[0], g_up[0], conv_dw[0]], ["gather"] * 5, [SLOT8] * 5))
    wout_full = wout_g.reshape(1, D, D)
    wup_full, aup_full, gup_full, cdw_full = cols(wup_g), cols(aup_g), cols(gup_g), cols(cdw_g)
    zeros64 = jnp.zeros((HN, DR), f32)
    wup_pad = jnp.concatenate([wup_full, zeros64], axis=0)
    aup_pad = jnp.concatenate([zeros64, aup_full], axis=0)
    prep_params = (shift_mu, w0, a0, k_k, k_a, wup_pad, aup_pad, gup_full, seg, seg_t)
    hm, ps, pc = _norm_mm_fwd(x1, mix_norm_pre, win_t, "mix_in_fwd")
    rec_in, (wd2,) = _prep_fwd(ps, prep_params, "prep_fwd", comm=([shard(ffn2_w_down)], ["gather"], [SLOT8]))
    wd2 = wd2.reshape(4, FB, D)
    g_gate = rec_in[6]
    rec_in = rec_in[:6]
    (a_c, g_c, p_c, q_c, yl_c), (wg2,) = _r1_fwd(rec_in, "r1_fwd", comm=([shard_t(ffn2_w_gu)], ["gather"], [SLOT42]))
    y_rec, s_all = _r2_fwd(a_c, g_c, q_c, yl_c, "r2_fwd")
    c_conv, out_b = _conv_fwd(pc, cdw_full, conv_b, conv_ln_w, conv_ln_b, "conv_fwd")
    cat = _post_fwd(y_rec, rec_in[0], rec_in[2], rec_in[3], g_gate, out_b, post_params, "post_fwd")
    fm, x2 = _mm_norm_res_fwd(cat.reshape(1, t, D), wout_full, x1, mix_norm_post, 1.0, "mix_out_fwd")
    (xn2, gu2, h2), _ = _ffn_in_fwd(x2, ffn2_norm_pre, wg2, "ffn2_in_fwd")
    f2, dy, loss_part = _mm_norm_res_fwd(h2, wd2, x2, ffn2_norm_post, 0.5, "ffn2_out_fwd", tgt=tgt)

    (df2, dg_post2, dgu2), _ = _post_bwd_nt(dy, f2, ffn2_norm_post, wd2, 0.5, gu2, "ffn2_out_bwd")
    (dx2, dg_pre2), _ = _nt_pre_bwd(dgu2, wg2, x2, ffn2_norm_pre, dy, "ffn2_in_bwd")
    dwd2 = _wgrad(h2, df2.reshape(1, t, D), True, False, D, "ffn2_wd_grad")
    dwg2 = _wgrad(dgu2.reshape(8, t, FB), xn2.reshape(1, t, D), True, False, D, "ffn2_wg_grad")
    (dfm, dg_postm, dcat), _ = _post_bwd_nt(dx2, fm, mix_norm_post, wout_full, 1.0, None, "mix_out_bwd")
    dwout = _wgrad(cat.reshape(1, t, D), dfm.reshape(1, t, D), False, False, D, "wout_grad")
    dcat = dcat.reshape(t, D)
    (dy_rec, dr2, dkp2, dv2, dgate, dgn_w, dgn_b, drk), (s_wout,) = _post_bwd(
        dcat, y_rec, rec_in[0], rec_in[2], rec_in[3], g_gate, post_params, "post_bwd", comm=(
            [dwout.reshape(NDEV, D // NDEV, D)], ["scatter"], [SLOT8]))
    dq_c, da_c, dg_c = _r2_bwd(dy_rec, q_c, s_all, a_c, "r2_bwd")
    rec_grads, (s_wg2,) = _r1_bwd(rec_in, p_c, da_c, dg_c, dq_c, dy_rec, "r1_bwd", comm=(
        [dwg2.reshape(4, 2, FB, D)], ["scatter"], [SLOT42]))
    prep_out = _prep_bwd(ps, prep_params, rec_grads, (dr2, dkp2, dv2, dgate), "prep_bwd")
    dpsl, dw0, da0, dkk, dka, dwup_pad, daup_pad, dgup = prep_out
    dc_conv, dlnw, dlnb, dcb = _conv_bwd1(dcat, c_conv, conv_ln_w, conv_ln_b, "conv_bwd1")
    (dpc, dcdw), (s_wd2,) = _conv_bwd2(dc_conv, pc, cdw_full, "conv_bwd2", comm=(
        [dwd2.reshape(NDEV, F // NDEV, D)], ["scatter"], [SLOT8]))
    dp, dmu = _shift_bwd(dpsl, ps, shift_mu, dpc, "shift_bwd")
    (dx1, dg_prem), _ = _nt_pre_bwd(dp.reshape(1, 1, t, DIN), win_t.reshape(1, 1, DIN, D), x1, mix_norm_pre, dx2,
                                    "mix_in_bwd")
    dwin_s = _wgrad(dp.reshape(1, t, DIN), hm.reshape(1, t, D), False, False, D // 2, "win_grad").reshape(
        NDEV, DIN // NDEV, D)
    (df1, dg_post1, dgu1), (s_win,) = _post_bwd_nt(dx1, f1, ffn1_norm_post, wd1, 0.5, gu1, "ffn1_out_bwd", comm=(
        [dwin_s], ["scatter"], [SLOT8]))
    dwd1 = _wgrad(h1, df1.reshape(1, t, D), True, False, D, "ffn1_wd_grad")
    dwg1, (s_wd1,) = _wgrad(dgu1.reshape(8, t, FB), xn1.reshape(1, t, D), True, False, D, "ffn1_wg_grad", comm=(
        [dwd1.reshape(NDEV, F // NDEV, D)], ["scatter"], [SLOT8]))
    wg1_sends, wg1_recvs, dwg1_thru, wg1_land, started = _scatter_start(dwg1.reshape(4, 2, FB, D), SLOT42, "wg1_scatter_start")
    (dx0, dg_pre1), _ = _nt_pre_bwd(dgu1, wg1, x0, ffn1_norm_pre + started[:1, :1], dx1, "ffn1_in_bwd")

    res = {}

    def adam_big(nm, w, s, m, v, transposed):
        view = (lambda a: jnp.swapaxes(a[0], 0, 1)) if transposed else (lambda a: a[0])
        outs = _adam_slots(view(w), s, view(m), view(v), "adam_" + nm)
        done.append(outs[1])
        res[nm] = [(jnp.swapaxes(o, 0, 1) if transposed else o)[None] for o in outs]

    done = []

    adam_big("ffn2_w_gu", ffn2_w_gu, s_wg2.reshape(NDEV, FB, D), m_ffn2_w_gu, v_ffn2_w_gu, True)
    adam_big("ffn2_w_down", ffn2_w_down, s_wd2, m_ffn2_w_down, v_ffn2_w_down, False)
    adam_big("w_in", w_in, s_win, m_w_in, v_w_in, True)
    adam_big("w_out", w_out, s_wout, m_w_out, v_w_out, False)
    adam_big("ffn1_w_down", ffn1_w_down, s_wd1, m_ffn1_w_down, v_ffn1_w_down, False)
    _, s_wg1 = _scatter_wait(wg1_sends, wg1_recvs, dwg1_thru, wg1_land, [dg_pre1] + done, SLOT42, "wg1_scatter_wait")
    my_slot = SLOT42[1](4 * lax.axis_index("x") + 2 * lax.axis_index("y") + lax.axis_index("c"))
    own = lax.dynamic_slice(dwg1.reshape(4, 2, FB, D), (*my_slot, 0, 0), (1, 1, FB, D))
    s_wg1 = lax.dynamic_update_slice(s_wg1, own, (*my_slot, 0, 0))
    rep_shapes = [(1, 1)] + [(1, D)] * 6 + [(1, DS)] + [(1, DR)] * 10
    rep_parts = [loss_part, dg_pre1, dg_post1, dg_prem, dg_postm, dg_pre2, dg_post2, dmu,
                 dw0, da0, dkk, dka, drk, dgn_w, dgn_b, dcb, dlnw, dlnb]
    n_rep = _round_up(sum(_round_up(math.prod(s), LANES) for s in rep_shapes), 8 * LANES)
    sh_shapes = [(HN, HN), (HN, HN), (LANES, HN), (CW, HN)]
    n_sh = _round_up(sum(_round_up(math.prod(s), LANES) for s in sh_shapes), 8 * LANES)
    sh_parts = []
    for a in (dwup_pad[:HN], daup_pad[HN:], dgup, dcdw):
        rows = jnp.transpose(a.reshape(a.shape[0], NDEV, HN), (1, 0, 2)).reshape(NDEV, a.shape[0] * HN)
        sh_parts.append(jnp.pad(rows, ((0, 0), (0, (-rows.shape[1]) % LANES))))
    sh_vec = jnp.concatenate(sh_parts, axis=1)
    sh_vec = jnp.pad(sh_vec, ((0, 0), (0, n_sh - sh_vec.shape[1]))).reshape(NDEV, n_sh // LANES, LANES)
    rep_vec, sh_vec, s_wg1 = lax.optimization_barrier((_pack(rep_parts, n_rep), sh_vec, s_wg1))
    s_rep, s_sh = _exchange([rep_vec, sh_vec], ["gather", "scatter"], [SLOT8, SLOT8], "exchange_small")
    adam_big("ffn1_w_gu", ffn1_w_gu, s_wg1.reshape(NDEV, FB, D), m_ffn1_w_gu, v_ffn1_w_gu, True)


    gsum = _unpack(_sum_slots(s_rep, "sum_rep"), rep_shapes)
    gsh = _unpack(_sum_slots(s_sh, "sum_sharded"), sh_shapes)
    loss = gsum[0].reshape(())
    rep_names = ["ffn1_norm_pre", "ffn1_norm_post", "mix_norm_pre", "mix_norm_post", "ffn2_norm_pre", "ffn2_norm_post",
                 "shift_mu", "w0", "a0", "k_k", "k_a", "r_k", "gn_w", "gn_b", "conv_b", "conv_ln_w", "conv_ln_b"]
    shard_names = ["w_up", "a_up", "g_up", "conv_dw"]
    small_g = {n: gsum[1 + i] for i, n in enumerate(rep_names)}
    small_g["r_k"] = small_g["r_k"].reshape(1, NH, HN)
    for i, n in enumerate(shard_names):
        small_g[n] = gsh[i][None]
    names = rep_names + shard_names
    shapes = [named[n].shape for n in names]

    def flat2(a):
        return a.reshape(1, DR) if a.shape == (1, NH, HN) else a.reshape(a.shape[-2:])

    deltas, new_ms, new_vs = _adam_many(*[[flat2(d[n]) for n in names] for d in (
        {n: named[n] for n in names}, small_g, {n: named["m_" + n] for n in names}, {n: named["v_" + n] for n in names})],
        "adam_small")
    for n, shp, dl, nm_, nv_ in zip(names, shapes, deltas, new_ms, new_vs):
        res[n] = [small_g[n], dl.reshape(shp), nm_.reshape(shp), nv_.reshape(shp)]

    order = ["ffn1_norm_pre", "ffn1_norm_post", "ffn1_w_gu", "ffn1_w_down", "mix_norm_pre", "mix_norm_post", "w_in",
             "shift_mu", "w_up", "w0", "a_up", "a0", "g_up", "k_k", "k_a", "r_k", "gn_w", "gn_b", "conv_dw", "conv_b",
             "conv_ln_w", "conv_ln_b", "w_out", "ffn2_norm_pre", "ffn2_norm_post", "ffn2_w_gu", "ffn2_w_down"]
    return (loss, dx0.reshape(1, t, D), *[res[n][0] for n in order], *[res[n][1] for n in order],
            *[res[n][2] for n in order], *[res[n][3] for n in order])
```

```python
import functools
import math

import jax
import jax.numpy as jnp
from jax import lax
from jax.experimental import pallas as pl
from jax.experimental.pallas import tpu as pltpu

f32 = jnp.float32
bf16 = jnp.bfloat16

D = 1024
F = 2816
FB = 704
DR = 512
DC = 512
NH = 8
HN = 64
DS = 1792
DIN = 2816
CW = 31
CHUNK = 64
R1_CHUNKS = 4
R2_CHUNKS = 8
NDEV = 8
RMS_EPS = 1e-6
GN_EPS = 64e-5
LN_EPS = 1e-5
DECAY_SCALE = math.exp(-0.5)
ADAM_LR, ADAM_B1, ADAM_B2, ADAM_EPS, ADAM_WD, ADAM_STEP = 0.001, 0.9, 0.999, 1e-08, 0.01, 10
LANES = 128
HALO = 32
CONV_ROWS = 32
TM_MXU = 1024
TM_SUB = 256
TM_VPU = 256

ANY = pl.BlockSpec(memory_space=pl.ANY)


def _full(shape):
    return pl.BlockSpec(shape, lambda *_: (0,) * len(shape))


def _sds(shape, dtype=f32):
    return jax.ShapeDtypeStruct(shape, dtype)


def _dot(a, b):
    return jnp.dot(a, b, preferred_element_type=f32)


def _dot_nt(a, b):
    return lax.dot_general(a, b, (((1,), (1,)), ((), ())), preferred_element_type=f32)


def _dot_tn(a, b):
    return lax.dot_general(a, b, (((0,), (0,)), ((), ())), preferred_element_type=f32)


def _terms(a, n):
    out, rem = [], a
    for i in range(n):
        t = rem.astype(bf16)
        out.append(t)
        if i + 1 < n:
            rem = rem - t.astype(f32)
    return out


def _mm_raw(a, b, kind, mode):
    nb = a.ndim - 2
    bd = tuple(range(nb))
    ca = nb if kind == "tn" else nb + 1
    cb = nb + 1 if kind == "nt" else nb
    dn = (((ca,), (cb,)), (bd, bd))
    pa, pb = mode[:2]
    ta, tb = _terms(a, pa), _terms(b, pb)
    acc = None
    for i in range(pa):
        for j in range(pb):
            if i + j < max(pa, pb):
                p = lax.dot_general(ta[i], tb[j], dn, preferred_element_type=f32)
                acc = p if acc is None else acc + p
    return acc


@functools.partial(jax.custom_vjp, nondiff_argnums=(2, 3))
def mm(a, b, kind, mode):
    return _mm_raw(a, b, kind, mode)


def _mm_fwd(a, b, kind, mode):
    return _mm_raw(a, b, kind, mode), (a, b)


def _ct_terms(keep):
    return EXACT_TERMS if keep == 1 else keep


def _mm_bwd(kind, mode, res, ct):
    a, b = res
    pa, pb = mode[:2]
    if len(mode) == 4:
        ca = cb = mode[2]
        pa = pb = mode[3]
    elif mode == (1, 1):
        ca = cb = 1
    else:
        ca, cb = _ct_terms(pb), _ct_terms(pa)
    if kind == "nn":
        da = mm(ct, b, "nt", (ca, pb))
        db = mm(a, ct, "tn", (pa, cb))
    elif kind == "nt":
        da = mm(ct, b, "nn", (ca, pb))
        db = mm(ct, a, "tn", (cb, pa))
    else:
        da = mm(b, ct, "nt", (pb, ca))
        db = mm(a, ct, "nn", (pa, cb))
    return da, db


mm.defvjp(_mm_fwd, _mm_bwd)

X3 = (2, 2)
EXACT_TERMS = 2
EA = (1, EXACT_TERMS)
EB = (EXACT_TERMS, 1)
X1 = (1, 1)
RGM = (1, 1, 1, 2)
RIM = X1
RWM = X1
RAM = X1
R2M = X1
LORA = X1


def _rms_fwd(x, g):
    inv = lax.rsqrt(jnp.mean(x * x, axis=-1, keepdims=True) + RMS_EPS)
    return x * inv * g


def _rms_bwd(x, g, dy):
    inv = lax.rsqrt(jnp.mean(x * x, axis=-1, keepdims=True) + RMS_EPS)
    xh = x * inv
    dxh = dy * g
    dg = jnp.sum(dy * xh, axis=0, keepdims=True)
    dx = inv * (dxh - xh * jnp.mean(dxh * xh, axis=-1, keepdims=True))
    return dx, dg


def _tile(t, want):
    return min(t, want)


NPEER = NDEV - 1


def _comm_out_shape(arrays, modes, slots):
    out = []
    for a, mode, (lead, _) in zip(arrays, modes, slots):
        shp = tuple(lead) + tuple(a.shape) if mode == "gather" else tuple(a.shape)
        out.append(pltpu.HBM(shp, a.dtype))
    return out


def _comm_sems(n):
    return [pltpu.SemaphoreType.DMA((n * NPEER,)), pltpu.SemaphoreType.DMA((n * NPEER,)), pltpu.SemaphoreType.DMA((n,))]


RELAYED = (3, 5, 7)


def _comm_copies(ins, outs, sems, modes, slots, want):
    n = len(ins)
    send_sems, recv_sems, loc_sems = sems
    x, y, c = lax.axis_index("x"), lax.axis_index("y"), lax.axis_index("c")
    me = 4 * x + 2 * y + c

    def peer_of(k):
        px = 1 - x if k & 4 else x
        py = 1 - y if k & 2 else y
        pc = 1 - c if k & 1 else c
        return (px, py, pc), 4 * px + 2 * py + pc

    def slot(i, dev):
        return outs[i].at[slots[i][1](dev)]

    if want == "local":
        return [pltpu.make_async_copy(ins[i] if modes[i] == "gather" else ins[i].at[slots[i][1](me)], slot(i, me),
                                      loc_sems.at[i]) for i in range(n)]
    sibling = peer_of(1)[0]
    out = []
    for k in range(1, NDEV):
        peer, pid = peer_of(k)
        for i in range(n):
            sem = dict(send_sem=send_sems.at[i * NPEER + k - 1], recv_sem=recv_sems.at[i * NPEER + k - 1],
                       device_id_type=pl.DeviceIdType.MESH)
            relayed = modes[i] == "gather" and k in RELAYED
            if want == "recv":
                src = ins[i] if modes[i] == "gather" else ins[i].at[slots[i][1](pid)]
                out.append(pltpu.make_async_remote_copy(src_ref=src, dst_ref=slot(i, pid), device_id=peer, **sem))
            elif want == "first" and not relayed:
                src = ins[i] if modes[i] == "gather" else ins[i].at[slots[i][1](pid)]
                out.append(pltpu.make_async_remote_copy(src_ref=src, dst_ref=slot(i, me), device_id=peer, **sem))
            elif want == "relay" and relayed:
                origin = peer_of(k - 1)[1]
                out.append(((k - 2) * n + i, pltpu.make_async_remote_copy(
                    src_ref=slot(i, origin), dst_ref=slot(i, origin), device_id=sibling, **sem)))
    return out


def _comm_start(ins, outs, sems, modes, slots):
    for cp in _comm_copies(ins, outs, sems, modes, slots, "local") + _comm_copies(ins, outs, sems, modes, slots, "first"):
        cp.start()


def _comm_wait(ins, outs, sems, modes, slots):
    recvs = _comm_copies(ins, outs, sems, modes, slots, "recv")
    relays = _comm_copies(ins, outs, sems, modes, slots, "relay")
    for idx, cp in relays:
        recvs[idx].wait_recv()
        cp.start()
    passed_on = {idx for idx, _ in relays}
    for idx, cp in enumerate(recvs):
        if idx not in passed_on:
            cp.wait_recv()
    for cp in _comm_copies(ins, outs, sems, modes, slots, "first") + [cp for _, cp in relays]:
        cp.wait_send()
    for cp in _comm_copies(ins, outs, sems, modes, slots, "local"):
        cp.wait()


def _exchange(arrays, modes, slots, name):
    n = len(arrays)

    def body(*refs):
        ins, outs, sems = refs[:n], refs[n:2 * n], refs[2 * n:]
        _comm_start(ins, outs, sems, modes, slots)
        _comm_wait(ins, outs, sems, modes, slots)

    return pl.pallas_call(
        body, name=name, out_shape=_comm_out_shape(arrays, modes, slots),
        in_specs=[ANY] * n, out_specs=[ANY] * n, scratch_shapes=_comm_sems(n),
    )(*arrays)


HBM_SPEC = pl.BlockSpec(memory_space=pltpu.HBM)
SEM_SPEC = pl.BlockSpec(memory_space=pltpu.SEMAPHORE)


def _scatter_start(arr, slot, name):
    def body(v_ref, land_ref, send_sems, recv_sems, v_thru, land_thru, token):
        for cp in _comm_copies([v_ref], [land_ref], (send_sems, recv_sems, None), ["scatter"], [slot], "first"):
            cp.start()
        token[...] = jnp.zeros_like(token)

    return pl.pallas_call(
        body, name=name,
        out_shape=(pltpu.SemaphoreType.DMA((NPEER,)), pltpu.SemaphoreType.DMA((NPEER,)),
                   pltpu.HBM(arr.shape, arr.dtype), pltpu.HBM(arr.shape, arr.dtype), _sds((8, LANES))),
        in_specs=(HBM_SPEC, HBM_SPEC),
        out_specs=(SEM_SPEC, SEM_SPEC, HBM_SPEC, HBM_SPEC, pl.BlockSpec(memory_space=pltpu.VMEM)),
        input_output_aliases={0: 2, 1: 3},
        compiler_params=pltpu.CompilerParams(has_side_effects=pltpu.SideEffectType.DATAFLOW_SIDE_EFFECTING),
    )(pltpu.with_memory_space_constraint(arr, pltpu.HBM),
      pltpu.with_memory_space_constraint(lax.empty(arr.shape, arr.dtype), pltpu.HBM))


def _scatter_wait(send_sems, recv_sems, v_thru, land_thru, after, slot, name):
    def body(v_ref, land_ref, send_sems, recv_sems, *rest):
        sems = (send_sems, recv_sems, None)
        for cp in _comm_copies([v_ref], [land_ref], sems, ["scatter"], [slot], "first"):
            cp.wait_send()
        for cp in _comm_copies([v_ref], [land_ref], sems, ["scatter"], [slot], "recv"):
            cp.wait_recv()

    return pl.pallas_call(
        body, name=name, out_shape=(pltpu.HBM(v_thru.shape, v_thru.dtype), pltpu.HBM(v_thru.shape, v_thru.dtype)),
        in_specs=(HBM_SPEC, HBM_SPEC, SEM_SPEC, SEM_SPEC) + (ANY,) * len(after), out_specs=(HBM_SPEC, HBM_SPEC),
        input_output_aliases={0: 0, 1: 1},
        compiler_params=pltpu.CompilerParams(has_side_effects=pltpu.SideEffectType.DATAFLOW_SIDE_EFFECTING),
    )(v_thru, land_thru, send_sems, recv_sems, *after)


def _pcall(body, *, name, grid, in_specs, out_specs, out_shape, args, scratch_shapes=(), comm=None):
    if comm is None:
        return pl.pallas_call(body, name=name, grid=grid, in_specs=in_specs, out_specs=out_specs, out_shape=out_shape,
                              scratch_shapes=list(scratch_shapes))(*args), None
    arrays, modes, slots = comm
    n_in, n_out, n_scr, nc = len(args), len(out_shape), len(scratch_shapes), len(arrays)

    def hosted(*refs):
        a_in, c_in = refs[:n_in], refs[n_in:n_in + nc]
        o = n_in + nc
        a_out, c_out = refs[o:o + n_out], refs[o + n_out:o + n_out + nc]
        o += n_out + nc
        a_scr, sems = refs[o:o + n_scr], refs[o + n_scr:]
        first = pl.program_id(0) == 0
        last = pl.program_id(0) == grid[0] - 1
        for ax in range(1, len(grid)):
            first = jnp.logical_and(first, pl.program_id(ax) == 0)
            last = jnp.logical_and(last, pl.program_id(ax) == grid[ax] - 1)

        @pl.when(first)
        def _():
            _comm_start(c_in, c_out, sems, modes, slots)

        body(*a_in, *a_out, *a_scr)

        @pl.when(last)
        def _():
            _comm_wait(c_in, c_out, sems, modes, slots)

    res = pl.pallas_call(
        hosted, name=name, grid=grid, in_specs=list(in_specs) + [ANY] * nc, out_specs=list(out_specs) + [ANY] * nc,
        out_shape=list(out_shape) + _comm_out_shape(arrays, modes, slots),
        scratch_shapes=list(scratch_shapes) + _comm_sems(nc),
    )(*args, *arrays)
    return res[:n_out], res[n_out:]


SLOT8 = ((NDEV,), lambda d: (d,))
SLOT42 = ((4, 2), lambda d: (lax.rem(d, 4), lax.div(d, 4)))


def _ffn_in_fwd(x, g, wg4, name, comm=None):
    t = x.shape[0]
    tm = _tile(t, TM_MXU)
    sub = _tile(tm, TM_SUB)

    def body(x_ref, g_ref, w_ref, xn_ref, gu_ref, h_ref):
        for s in range(tm // sub):
            rows = pl.ds(s * sub, sub)
            xn = _rms_fwd(x_ref[rows, :], g_ref[...]).astype(bf16)
            xn_ref[rows, :] = xn
            gate = _dot_nt(xn, w_ref[0, 0])
            up = _dot_nt(xn, w_ref[0, 1])
            gu_ref[0, 0, rows, :] = gate.astype(bf16)
            gu_ref[0, 1, rows, :] = up.astype(bf16)
            h_ref[0, rows, :] = (gate * jax.nn.sigmoid(gate) * up).astype(bf16)

    return _pcall(
        body, name=name, grid=(t // tm, 4),
        in_specs=[pl.BlockSpec((tm, D), lambda i, e: (i, 0)), _full((1, D)),
                  pl.BlockSpec((1, 2, FB, D), lambda i, e: (e, 0, 0, 0))],
        out_specs=[pl.BlockSpec((tm, D), lambda i, e: (i, 0)),
                   pl.BlockSpec((1, 2, tm, FB), lambda i, e: (e, 0, i, 0)),
                   pl.BlockSpec((1, tm, FB), lambda i, e: (e, i, 0))],
        out_shape=[_sds((t, D), bf16), _sds((4, 2, t, FB), bf16), _sds((4, t, FB), bf16)],
        args=(x, g, wg4), comm=comm)


def _mm_norm_res_fwd(hb, wb, xres, g, scale, name, comm=None, tgt=None):
    e_n, t, k = hb.shape
    tm = _tile(t, TM_MXU)

    def body(h_ref, w_ref, x_ref, g_ref, *rest):
        f = _dot(h_ref[0], w_ref[0])
        for e in range(1, e_n):
            f = f + _dot(h_ref[e], w_ref[e])
        xnew = x_ref[...] + scale * _rms_fwd(f, g_ref[...])
        if tgt is None:
            f_ref, o_ref = rest
            o_ref[...] = xnew
        else:
            t_ref, f_ref, o_ref, l_ref = rest
            err = xnew - t_ref[...]
            o_ref[...] = err * (1.0 / D)
            part = 0.5 * jnp.sum(jnp.mean(err * err, axis=-1, keepdims=True), axis=0, keepdims=True)

            @pl.when(pl.program_id(0) == 0)
            def _():
                l_ref[...] = part

            @pl.when(pl.program_id(0) != 0)
            def _():
                l_ref[...] += part
        f_ref[...] = f

    tile = pl.BlockSpec((tm, D), lambda i: (i, 0))
    outs, got = _pcall(
        body, name=name, grid=(t // tm,),
        in_specs=[pl.BlockSpec((e_n, tm, k), lambda i: (0, i, 0)), _full((e_n, k, D)), tile, _full((1, D))]
        + ([] if tgt is None else [tile]),
        out_specs=[tile, tile] + ([] if tgt is None else [_full((1, 1))]),
        out_shape=[_sds((t, D)), _sds((t, D))] + ([] if tgt is None else [_sds((1, 1))]),
        args=(hb, wb, xres, g) + (() if tgt is None else (tgt,)), comm=comm)
    return outs if comm is None else (outs, got)


def _post_bwd_nt(dxn, f, g, wb, scale, gu, name, comm=None):
    e_n, k, _ = wb.shape
    t = f.shape[0]
    tm = _tile(t, TM_MXU)
    sub = _tile(tm, TM_SUB)
    swiglu = gu is not None

    def body(*refs):
        if swiglu:
            dx_ref, f_ref, g_ref, w_ref, gu_ref, df_ref, dg_ref, dh_ref, df_s = refs
        else:
            dx_ref, f_ref, g_ref, w_ref, df_ref, dg_ref, dh_ref, df_s = refs
        i, e = pl.program_id(0), pl.program_id(1)

        @pl.when(e == 0)
        def _():
            df, dg = _rms_bwd(f_ref[...], g_ref[...], scale * dx_ref[...])
            df_s[...] = df.astype(bf16)
            df_ref[...] = df_s[...]

            @pl.when(i == 0)
            def _():
                dg_ref[...] = dg

            @pl.when(i != 0)
            def _():
                dg_ref[...] += dg

        for s in range(tm // sub):
            rows = pl.ds(s * sub, sub)
            dh = _dot_nt(df_s[rows, :], w_ref[0])
            if swiglu:
                gate = gu_ref[0, 0, rows, :].astype(f32)
                up = gu_ref[0, 1, rows, :].astype(f32)
                sg = jax.nn.sigmoid(gate)
                dh_ref[0, 0, rows, :] = (dh * up * (sg * (1.0 + gate * (1.0 - sg)))).astype(bf16)
                dh_ref[0, 1, rows, :] = (dh * gate * sg).astype(bf16)
            else:
                dh_ref[0, rows, :] = dh

    in_specs = [pl.BlockSpec((tm, D), lambda i, e: (i, 0)), pl.BlockSpec((tm, D), lambda i, e: (i, 0)), _full((1, D)),
                pl.BlockSpec((1, k, D), lambda i, e: (e, 0, 0))]
    args = [dxn, f, g, wb]
    if swiglu:
        in_specs.append(pl.BlockSpec((1, 2, tm, k), lambda i, e: (e, 0, i, 0)))
        args.append(gu)
        dh_spec = pl.BlockSpec((1, 2, tm, k), lambda i, e: (e, 0, i, 0))
        dh_shape = _sds((e_n, 2, t, k), bf16)
    else:
        dh_spec = pl.BlockSpec((1, tm, k), lambda i, e: (e, i, 0))
        dh_shape = _sds((e_n, t, k), f32)
    return _pcall(
        body, name=name, grid=(t // tm, e_n), in_specs=in_specs,
        out_specs=[pl.BlockSpec((tm, D), lambda i, e: (i, 0)), _full((1, D)), dh_spec],
        out_shape=[_sds((t, D), bf16), _sds((1, D)), dh_shape],
        scratch_shapes=[pltpu.VMEM((tm, D), bf16)], args=args, comm=comm)


def _nt_pre_bwd(dy, wb, x, g, dres, name, comm=None):
    e_n, q_n, t, k = dy.shape
    tm = _tile(t, TM_MXU)

    def body(dy_ref, w_ref, x_ref, g_ref, r_ref, dx_ref, dg_ref, acc):
        i, e = pl.program_id(0), pl.program_id(1)
        p = _dot(dy_ref[0, 0], w_ref[0, 0])
        for q in range(1, q_n):
            p = p + _dot(dy_ref[0, q], w_ref[0, q])

        @pl.when(e == 0)
        def _():
            acc[...] = p

        @pl.when(e != 0)
        def _():
            acc[...] += p

        @pl.when(e == e_n - 1)
        def _():
            dx, dg = _rms_bwd(x_ref[...], g_ref[...], acc[...])
            dx_ref[...] = r_ref[...] + dx

            @pl.when(i == 0)
            def _():
                dg_ref[...] = dg

            @pl.when(i != 0)
            def _():
                dg_ref[...] += dg

    return _pcall(
        body, name=name, grid=(t // tm, e_n),
        in_specs=[pl.BlockSpec((1, q_n, tm, k), lambda i, e: (e, 0, i, 0)),
                  pl.BlockSpec((1, q_n, k, D), lambda i, e: (e, 0, 0, 0)),
                  pl.BlockSpec((tm, D), lambda i, e: (i, 0)), _full((1, D)),
                  pl.BlockSpec((tm, D), lambda i, e: (i, 0))],
        out_specs=[pl.BlockSpec((tm, D), lambda i, e: (i, 0)), _full((1, D))],
        out_shape=[_sds((t, D)), _sds((1, D))],
        scratch_shapes=[pltpu.VMEM((tm, D), f32)], args=(dy, wb, x, g, dres), comm=comm)


def _wgrad(a, b, a_batched, b_batched, tn, name, comm=None):
    t, k = a.shape[1], a.shape[2]
    nn = b.shape[2]
    nb = max(a.shape[0], b.shape[0])
    tt = _tile(t, TM_MXU)
    nt = t // tt

    def body(a_ref, b_ref, o_ref, acc):
        @pl.when(pl.program_id(2) == 0)
        def _():
            acc[...] = _dot_tn(a_ref[0], b_ref[0])

        @pl.when(pl.program_id(2) != 0)
        def _():
            acc[...] += _dot_tn(a_ref[0], b_ref[0])

        @pl.when(pl.program_id(2) == nt - 1)
        def _():
            o_ref[0] = acc[...].astype(bf16)

    (out,), got = _pcall(
        body, name=name, grid=(nb, nn // tn, nt),
        in_specs=[pl.BlockSpec((1, tt, k), (lambda n, j, s: (n, s, 0)) if a_batched else (lambda n, j, s: (0, s, 0))),
                  pl.BlockSpec((1, tt, tn), (lambda n, j, s: (n, s, j)) if b_batched else (lambda n, j, s: (0, s, j)))],
        out_specs=[pl.BlockSpec((1, k, tn), lambda n, j, s: (n, 0, j))],
        out_shape=[_sds((nb, k, nn), bf16)],
        scratch_shapes=[pltpu.VMEM((k, tn), f32)], args=(a, b), comm=comm)
    return out if comm is None else (out, got)


def _norm_mm_fwd(x, g, w, name):
    t = x.shape[0]
    tm = _tile(t, TM_VPU)

    def body(x_ref, g_ref, w_ref, xn_ref, ps_ref, pc_ref):
        xn = _rms_fwd(x_ref[...], g_ref[...]).astype(bf16)
        xn_ref[...] = xn
        ps_ref[...] = _dot_nt(xn, w_ref[:DS, :])
        pc_ref[...] = _dot_nt(xn, w_ref[DS:, :])

    return pl.pallas_call(
        body, name=name, grid=(t // tm,),
        in_specs=[pl.BlockSpec((tm, D), lambda i: (i, 0)), _full((1, D)), _full((DIN, D))],
        out_specs=[pl.BlockSpec((tm, D), lambda i: (i, 0)), pl.BlockSpec((tm, DS), lambda i: (i, 0)),
                   pl.BlockSpec((tm, 2 * DC), lambda i: (i, 0))],
        out_shape=[_sds((t, D), bf16), _sds((t, DS)), _sds((t, 2 * DC))],
    )(x, g, w)


def _segsum_b(x, seg, seg_t):
    return mm(mm(x, seg, "nn", EB), seg_t, "nn", EB)


def _prep_fn(psl, w0, a0, k_k, k_a, wup, aup, gup, seg, seg_t):
    r, k, v = psl[:, :DR], psl[:, DR:2 * DR], psl[:, 2 * DR:3 * DR]
    xwa, xg = psl[:, 3 * DR:3 * DR + LANES], psl[:, 3 * DR + LANES:]
    d = w0 + mm(jnp.tanh(xwa), wup, "nn", LORA)
    lw = -DECAY_SCALE * jax.nn.sigmoid(d)
    a = jax.nn.sigmoid(a0 + mm(xwa, aup, "nn", LORA))
    g = mm(jax.nn.sigmoid(xg), gup, "nn", LORA)
    kkr = k * k_k
    kk = kkr * lax.rsqrt(jnp.maximum(_segsum_b(kkr * kkr, seg, seg_t), 1e-12))
    kp = k * (1.0 + (a - 1.0) * k_a)
    return r, lw, kp, v, -kk, kk * a, g


def _shifted(ps, halo_row, first):
    prev = jnp.where(first, 0.0, halo_row)
    sh = pltpu.roll(ps, 1, 0)
    row = lax.broadcasted_iota(jnp.int32, ps.shape, 0)
    return jnp.where(row == 0, prev, sh)


def _heads_of(ref, rows):
    return jnp.stack([ref[rows, h * HN:(h + 1) * HN] for h in range(NH)], axis=0)


def _heads_to(ref, rows, val):
    ref[rows, :] = jnp.concatenate([val[h] for h in range(NH)], axis=-1)


_PREP_PARAM_SHAPES = [(1, DS), (1, DR), (1, DR), (1, DR), (1, DR), (LANES, DR), (LANES, DR), (LANES, DR),
                      (DR, LANES), (LANES, DR)]


def _prep_fwd(ps, params, name, comm=None):
    t = ps.shape[0]
    tm = _tile(t, TM_VPU)
    hb = tm // 8

    def body(ps_ref, halo_ref, mu_ref, *rest):
        prm = [r[...] for r in rest[:9]]
        outs = rest[9:]
        x = ps_ref[...]
        sh = _shifted(x, halo_ref[pl.ds(7, 1), :], pl.program_id(0) == 0)
        psl = x + (sh - x) * mu_ref[...]
        vals = _prep_fn(psl, *prm)
        for ref, val in zip(outs, vals):
            ref[...] = val

    outs, got = _pcall(
        body, name=name, grid=(t // tm,),
        in_specs=[pl.BlockSpec((tm, DS), lambda i: (i, 0)),
                  pl.BlockSpec((8, DS), lambda i: (jnp.maximum(i * hb - 1, 0), 0))]
        + [_full(s) for s in _PREP_PARAM_SHAPES],
        out_specs=[pl.BlockSpec((tm, DR), lambda i: (i, 0))] * 7,
        out_shape=[_sds((t, DR))] * 7, args=(ps, ps, *params), comm=comm)
    return outs if comm is None else (outs, got)


def _prep_bwd(ps, params, cts_hm, cts_tm, name, comm=None):
    t = ps.shape[0]
    tm = _tile(t, TM_VPU)
    hb = tm // 8

    def body(ps_ref, halo_ref, mu_ref, *rest):
        prm = [r[...] for r in rest[:9]]
        chm = rest[9:15]
        ctm = rest[15:19]
        dpsl_ref = rest[19]
        gouts = rest[20:27]
        x = ps_ref[...]
        sh = _shifted(x, halo_ref[pl.ds(7, 1), :], pl.program_id(0) == 0)
        psl = x + (sh - x) * mu_ref[...]
        seg, seg_t = prm[7], prm[8]
        _, vjp = jax.vjp(lambda p, *w: _prep_fn(p, *w, seg, seg_t), psl, *prm[:7])
        c = [r[...] for r in chm]
        cts = (c[0] + ctm[0][...], c[1], c[2] + ctm[1][...], c[3] + ctm[2][...], c[4], c[5], ctm[3][...])
        grads = vjp(cts)
        dpsl_ref[...] = grads[0]
        for ref, gval in zip(gouts, grads[1:]):
            @pl.when(pl.program_id(0) == 0)
            def _(ref=ref, gval=gval):
                ref[...] = gval

            @pl.when(pl.program_id(0) != 0)
            def _(ref=ref, gval=gval):
                ref[...] += gval

    tk = pl.BlockSpec((tm, DR), lambda i: (i, 0))
    gshapes = _PREP_PARAM_SHAPES[1:8]
    outs, got = _pcall(
        body, name=name, grid=(t // tm,),
        in_specs=[pl.BlockSpec((tm, DS), lambda i: (i, 0)),
                  pl.BlockSpec((8, DS), lambda i: (jnp.maximum(i * hb - 1, 0), 0))]
        + [_full(s) for s in _PREP_PARAM_SHAPES] + [tk] * 10,
        out_specs=[pl.BlockSpec((tm, DS), lambda i: (i, 0))] + [_full(s) for s in gshapes],
        out_shape=[_sds((t, DS))] + [_sds(s) for s in gshapes],
        args=(ps, ps, *params, *cts_hm, *cts_tm), comm=comm)
    return outs if comm is None else (outs, got)


def _shift_bwd(dpsl, ps, mu, dpc, name):
    t = ps.shape[0]
    tm = _tile(t, TM_VPU)
    hb = tm // 8
    last_blk = t // 8 - 1

    def body(d_ref, dn_ref, ps_ref, halo_ref, mu_ref, dpc_ref, dp_ref, dmu_ref):
        i = pl.program_id(0)
        mu_v = mu_ref[...]
        d = d_ref[...]
        nxt = jnp.where(i == pl.num_programs(0) - 1, 0.0, dn_ref[pl.ds(0, 1), :])
        up = pltpu.roll(d, tm - 1, 0)
        row = lax.broadcasted_iota(jnp.int32, d.shape, 0)
        up = jnp.where(row == tm - 1, nxt, up)
        dp_ref[:, :DS] = (d * (1.0 - mu_v) + up * mu_v).astype(bf16)
        dp_ref[:, DS:] = dpc_ref[...].astype(bf16)
        x = ps_ref[...]
        sh = _shifted(x, halo_ref[pl.ds(7, 1), :], i == 0)
        dmu = jnp.sum(d * (sh - x), axis=0, keepdims=True)

        @pl.when(i == 0)
        def _():
            dmu_ref[...] = dmu

        @pl.when(i != 0)
        def _():
            dmu_ref[...] += dmu

    return pl.pallas_call(
        body, name=name, grid=(t // tm,),
        in_specs=[pl.BlockSpec((tm, DS), lambda i: (i, 0)),
                  pl.BlockSpec((8, DS), lambda i: (jnp.minimum((i + 1) * hb, last_blk), 0)),
                  pl.BlockSpec((tm, DS), lambda i: (i, 0)),
                  pl.BlockSpec((8, DS), lambda i: (jnp.maximum(i * hb - 1, 0), 0)),
                  _full((1, DS)), pl.BlockSpec((tm, 2 * DC), lambda i: (i, 0))],
        out_specs=[pl.BlockSpec((tm, DIN), lambda i: (i, 0)), _full((1, DS))],
        out_shape=[_sds((t, DIN), bf16), _sds((1, DS))],
    )(dpsl, dpsl, ps, ps, mu, dpc)


def _post_fn(y, r, kp, v, g, gn_w, gn_b, r_k, seg, seg_t):
    mu = _segsum_b(y, seg, seg_t) * (1.0 / HN)
    yc = y - mu
    var = _segsum_b(yc * yc, seg, seg_t) * (1.0 / HN)
    yo = yc * lax.rsqrt(var + GN_EPS) * gn_w + gn_b
    bonus = _segsum_b(r * kp * r_k, seg, seg_t) * v
    return (yo + bonus) * g


_POST_PARAM_SHAPES = [(1, DR), (1, DR), (1, DR), (DR, LANES), (LANES, DR)]


def _post_fwd(y, r, kp, v, g, out_b, params, name):
    t = g.shape[0]
    tm = _tile(t, TM_VPU)

    def body(y_ref, r_ref, k_ref, v_ref, g_ref, ob_ref, *rest):
        prm = [p[...] for p in rest[:5]]
        cat_ref = rest[5]
        oa = _post_fn(y_ref[...], r_ref[...], k_ref[...], v_ref[...], g_ref[...], *prm)
        cat_ref[:, :DR] = oa.astype(bf16)
        cat_ref[:, DR:] = ob_ref[...]

    tk = pl.BlockSpec((tm, DR), lambda i: (i, 0))
    return pl.pallas_call(
        body, name=name, grid=(t // tm,),
        in_specs=[tk] * 6 + [_full(s) for s in _POST_PARAM_SHAPES],
        out_specs=pl.BlockSpec((tm, 2 * DR), lambda i: (i, 0)),
        out_shape=_sds((t, 2 * DR), bf16),
    )(y, r, kp, v, g, out_b, *params)


def _post_bwd(dcat, y, r, kp, v, g, params, name, comm=None):
    t = g.shape[0]
    tm = _tile(t, TM_VPU)

    def body(dc_ref, y_ref, r_ref, k_ref, v_ref, g_ref, *rest):
        prm = [p[...] for p in rest[:5]]
        dy_ref, dr_ref, dk_ref, dv_ref, dg_ref = rest[5:10]
        gouts = rest[10:13]
        seg, seg_t = prm[3], prm[4]
        _, vjp = jax.vjp(lambda *a: _post_fn(*a, seg, seg_t),
                         y_ref[...], r_ref[...], k_ref[...], v_ref[...], g_ref[...], *prm[:3])
        grads = vjp(dc_ref[...])
        dy_ref[...] = grads[0]
        dr_ref[...] = grads[1]
        dk_ref[...] = grads[2]
        dv_ref[...] = grads[3]
        dg_ref[...] = grads[4]
        for ref, gval in zip(gouts, grads[5:]):
            @pl.when(pl.program_id(0) == 0)
            def _(ref=ref, gval=gval):
                ref[...] = gval

            @pl.when(pl.program_id(0) != 0)
            def _(ref=ref, gval=gval):
                ref[...] += gval

    tk = pl.BlockSpec((tm, DR), lambda i: (i, 0))
    return _pcall(
        body, name=name, grid=(t // tm,),
        in_specs=[tk] * 6 + [_full(s) for s in _POST_PARAM_SHAPES],
        out_specs=[tk] * 5 + [_full((1, DR))] * 3,
        out_shape=[_sds((t, DR))] * 5 + [_sds((1, DR))] * 3,
        args=(dcat, y, r, kp, v, g, *params), comm=comm)


@jax.custom_vjp
def _inv_unit_lower(lb):
    c = lb.shape[-1]
    row = lax.broadcasted_iota(jnp.int32, (c, c), 0)
    col = lax.broadcasted_iota(jnp.int32, (c, c), 1)
    p = (row == col).astype(f32)[None] + lb
    m = lb
    for _ in range(c.bit_length() - 2):
        m = mm(m, m, "nn", RIM)
        p = p + mm(m, p, "nn", RIM)
    return p


def _inv_unit_lower_fwd(lb):
    p = _inv_unit_lower(lb)
    return p, p


def _inv_unit_lower_bwd(p, ct):
    return (mm(mm(p, ct, "tn", RIM), p, "nt", RIM),)


_inv_unit_lower.defvjp(_inv_unit_lower_fwd, _inv_unit_lower_bwd)


@jax.custom_vjp
def _known_inverse(lb, p):
    return p


def _known_inverse_fwd(lb, p):
    return p, p


def _known_inverse_bwd(p, ct):
    return mm(mm(p, ct, "tn", RIM), p, "nt", RIM), jnp.zeros_like(p)


_known_inverse.defvjp(_known_inverse_fwd, _known_inverse_bwd)


def _r1_fn(r, lw, k, v, z, b, p_saved=None, with_p=False):
    c = r.shape[1]
    row = lax.broadcasted_iota(jnp.int32, (c, c), 0)
    col = lax.broadcasted_iota(jnp.int32, (c, c), 1)
    incl = (row >= col)[None]
    strict = (row > col)[None]
    eye = (row == col).astype(f32)[None]
    tri = jnp.broadcast_to((row >= col).astype(f32)[None], (r.shape[0], c, c))
    cum = mm(tri, lw, "nn", EA)
    tot = jnp.sum(lw, axis=1, keepdims=True)
    zt = z * jnp.exp(cum - lw)
    rt = r * jnp.exp(cum)
    g_inv = jnp.exp(-cum)
    g_rem = jnp.exp(tot - cum)
    bt, kt = b * g_inv, k * g_inv
    bh, kh = b * g_rem, k * g_rem
    lb = jnp.where(strict, mm(zt, bt, "nt", RGM), 0.0)
    lk = jnp.where(strict, mm(zt, kt, "nt", RGM), 0.0)
    mb = jnp.where(incl, mm(rt, bt, "nt", RGM), 0.0)
    mk = jnp.where(incl, mm(rt, kt, "nt", RGM), 0.0)
    p = _inv_unit_lower(lb) if p_saved is None else _known_inverse(lb, p_saved)
    w1 = mm(p, zt, "nn", RWM)
    w2 = mm(p, mm(lk, v, "nn", RWM), "nn", RWM)
    a = mm(w1, bh, "tn", RAM) + eye * jnp.exp(tot)
    g = mm(w2, bh, "tn", RAM) + mm(v, kh, "tn", RAM)
    q = rt + mm(mb, w1, "nn", RWM)
    yl = mm(mb, w2, "nn", RWM) + mm(mk, v, "nn", RWM)
    return (a, g, q, yl, p) if with_p else (a, g, q, yl)


def _chunks_in(ref, n):
    return jnp.concatenate([_heads_of(ref, pl.ds(s * CHUNK, CHUNK)) for s in range(n)], axis=0)


def _chunks_out(ref, val, n):
    for s in range(n):
        _heads_to(ref, pl.ds(s * CHUNK, CHUNK), val[s * NH:(s + 1) * NH])


def _r1_fwd(ins, name, comm=None):
    t = ins[0].shape[0]
    n = R1_CHUNKS
    nc = t // CHUNK

    def body(r, lw, k, v, z, b, a_ref, g_ref, p_ref, q_ref, yl_ref):
        a, g, q, yl, p = _r1_fn(*[_chunks_in(x, n) for x in (r, lw, k, v, z, b)], with_p=True)
        for s in range(n):
            a_ref[s] = a[s * NH:(s + 1) * NH]
            g_ref[s] = g[s * NH:(s + 1) * NH]
            p_ref[s] = p[s * NH:(s + 1) * NH]
        _chunks_out(q_ref, q, n)
        _chunks_out(yl_ref, yl, n)

    ck = pl.BlockSpec((n * CHUNK, DR), lambda c: (c, 0))
    st = pl.BlockSpec((n, NH, HN, HN), lambda c: (c, 0, 0, 0))
    return _pcall(
        body, name=name, grid=(nc // n,), in_specs=[ck] * 6, out_specs=[st, st, st, ck, ck],
        out_shape=[_sds((nc, NH, HN, HN))] * 3 + [_sds((t, DR))] * 2, args=tuple(ins), comm=comm)


def _r1_bwd(ins, p_inv, da, dg, dq, dyl, name, comm=None):
    t = ins[0].shape[0]
    n = R1_CHUNKS
    nc = t // CHUNK

    def body(r, lw, k, v, z, b, p_ref, da_ref, dg_ref, dq_ref, dyl_ref, *outs):
        p_saved = jnp.concatenate([p_ref[s] for s in range(n)], axis=0)
        _, vjp = jax.vjp(lambda *a: _r1_fn(*a, p_saved=p_saved), *[_chunks_in(x, n) for x in (r, lw, k, v, z, b)])
        cts = (jnp.concatenate([da_ref[s] for s in range(n)], axis=0),
               jnp.concatenate([dg_ref[s] for s in range(n)], axis=0), _chunks_in(dq_ref, n), _chunks_in(dyl_ref, n))
        for ref, gval in zip(outs, vjp(cts)):
            _chunks_out(ref, gval, n)

    ck = pl.BlockSpec((n * CHUNK, DR), lambda c: (c, 0))
    st = pl.BlockSpec((n, NH, HN, HN), lambda c: (c, 0, 0, 0))
    return _pcall(
        body, name=name, grid=(nc // n,), in_specs=[ck] * 6 + [st, st, st, ck, ck], out_specs=[ck] * 6,
        out_shape=[_sds((t, DR))] * 6, args=(*ins, p_inv, da, dg, dq, dyl), comm=comm)


def _r2_fwd(a, g, q, yl, name):
    nc = a.shape[0]
    t = q.shape[0]

    def body(a_ref, g_ref, q_ref, yl_ref, y_ref, s_ref, s):
        @pl.when(pl.program_id(0) == 0)
        def _():
            s[...] = jnp.zeros_like(s)
        s0 = s[...]
        for j in range(n):
            rows = pl.ds(j * CHUNK, CHUNK)
            s_ref[j] = s0
            _heads_to(y_ref, rows, mm(_heads_of(q_ref, rows), s0, "nt", R2M) + _heads_of(yl_ref, rows))
            s0 = mm(s0, a_ref[j], "nn", R2M) + g_ref[j]
        s[...] = s0

    n = math.gcd(nc, R2_CHUNKS)
    ck = pl.BlockSpec((n * CHUNK, DR), lambda c: (c, 0))
    st = pl.BlockSpec((n, NH, HN, HN), lambda c: (c, 0, 0, 0))
    return pl.pallas_call(
        body, name=name, grid=(nc // n,), in_specs=[st, st, ck, ck], out_specs=[ck, st],
        out_shape=[_sds((t, DR)), _sds((nc, NH, HN, HN))],
        scratch_shapes=[pltpu.VMEM((NH, HN, HN), f32)],
    )(a, g, q, yl)


def _r2_bwd(dy, q, s_all, a, name):
    nc = a.shape[0]
    t = q.shape[0]

    def body(dy_ref, q_ref, s_ref, a_ref, dq_ref, da_ref, dg_ref, ds):
        @pl.when(pl.program_id(0) == 0)
        def _():
            ds[...] = jnp.zeros_like(ds)
        dsn = ds[...]
        for j in reversed(range(n)):
            rows = pl.ds(j * CHUNK, CHUNK)
            s0 = s_ref[j]
            dyv = _heads_of(dy_ref, rows)
            dg_ref[j] = dsn
            da_ref[j] = mm(s0, dsn, "tn", R2M)
            _heads_to(dq_ref, rows, mm(dyv, s0, "nn", R2M))
            dsn = mm(dsn, a_ref[j], "nt", R2M) + mm(dyv, _heads_of(q_ref, rows), "tn", R2M)
        ds[...] = dsn

    n = math.gcd(nc, R2_CHUNKS)
    nb = nc // n
    ck = pl.BlockSpec((n * CHUNK, DR), lambda c: (nb - 1 - c, 0))
    st = pl.BlockSpec((n, NH, HN, HN), lambda c: (nb - 1 - c, 0, 0, 0))
    return pl.pallas_call(
        body, name=name, grid=(nb,), in_specs=[ck, ck, st, st], out_specs=[ck, st, st],
        out_shape=[_sds((t, DR)), _sds((nc, NH, HN, HN)), _sds((nc, NH, HN, HN))],
        scratch_shapes=[pltpu.VMEM((NH, HN, HN), f32)],
    )(dy, q, s_all, a)


def _ln_silu(c, w, b):
    mu = jnp.mean(c, axis=-1, keepdims=True)
    cc = c - mu
    var = jnp.mean(cc * cc, axis=-1, keepdims=True)
    u = cc * lax.rsqrt(var + LN_EPS) * w + b
    return u * jax.nn.sigmoid(u)


def _glu_tile(pc):
    return pc[:, :DC] * jax.nn.sigmoid(pc[:, DC:])


def _fill_glu(s_ref, pc_ref, halo_ref, first, tm):
    s_ref[pl.ds(0, HALO), :] = jnp.where(first, 0.0, _glu_tile(halo_ref[...]))
    s_ref[pl.ds(HALO, tm), :] = _glu_tile(pc_ref[...])


def _phase_copies(sh):
    n = sh.shape[1] - 8
    for r in range(1, 8):
        sh[r, pl.ds(0, n), :] = sh[0, pl.ds(r, n), :]


def _rows_at(sh, o, n):
    return sh[o % 8, pl.ds(o - o % 8, n), :]


def _conv_fwd(pc, dw, cb, lnw, lnb, name):
    t = pc.shape[0]
    tm = _tile(t, TM_VPU)
    hb = tm // HALO

    def body(pc_ref, halo_ref, dw_ref, cb_ref, w_ref, b_ref, c_ref, o_ref, s):
        _fill_glu(s.at[0], pc_ref, halo_ref, pl.program_id(0) == 0, tm)
        _phase_copies(s)
        for r0 in range(0, tm, CONV_ROWS):
            rows = pl.ds(r0, CONV_ROWS)
            acc = _rows_at(s, r0 + HALO - CW + 1, CONV_ROWS) * dw_ref[pl.ds(0, 1), :]
            for j in range(1, CW):
                acc = acc + _rows_at(s, r0 + HALO - CW + 1 + j, CONV_ROWS) * dw_ref[pl.ds(j, 1), :]
            c = acc + cb_ref[...]
            c_ref[rows, :] = c
            o_ref[rows, :] = _ln_silu(c, w_ref[...], b_ref[...]).astype(bf16)

    return pl.pallas_call(
        body, name=name, grid=(t // tm,),
        in_specs=[pl.BlockSpec((tm, 2 * DC), lambda i: (i, 0)),
                  pl.BlockSpec((HALO, 2 * DC), lambda i: (jnp.maximum(i * hb - 1, 0), 0)),
                  _full((CW, DC)), _full((1, DC)), _full((1, DC)), _full((1, DC))],
        out_specs=[pl.BlockSpec((tm, DC), lambda i: (i, 0)), pl.BlockSpec((tm, DC), lambda i: (i, 0))],
        out_shape=[_sds((t, DC)), _sds((t, DC), bf16)],
        scratch_shapes=[pltpu.VMEM((8, HALO + tm, DC), f32)],
    )(pc, pc, dw, cb, lnw, lnb)


def _conv_bwd1(dcat, c, lnw, lnb, name):
    t = c.shape[0]
    tm = _tile(t, TM_VPU)

    def body(d_ref, c_ref, w_ref, b_ref, dc_ref, dw_ref, db_ref, dcb_ref):
        _, vjp = jax.vjp(_ln_silu, c_ref[...], w_ref[...], b_ref[...])
        dc, dw, db = vjp(d_ref[...])
        dc_ref[...] = dc
        dcb = jnp.sum(dc, axis=0, keepdims=True)
        for ref, gval in ((dw_ref, dw), (db_ref, db), (dcb_ref, dcb)):
            @pl.when(pl.program_id(0) == 0)
            def _(ref=ref, gval=gval):
                ref[...] = gval

            @pl.when(pl.program_id(0) != 0)
            def _(ref=ref, gval=gval):
                ref[...] += gval

    tk = pl.BlockSpec((tm, DC), lambda i: (i, 0))
    return pl.pallas_call(
        body, name=name, grid=(t // tm,),
        in_specs=[pl.BlockSpec((tm, DC), lambda i: (i, 1)), tk, _full((1, DC)), _full((1, DC))],
        out_specs=[tk] + [_full((1, DC))] * 3,
        out_shape=[_sds((t, DC))] + [_sds((1, DC))] * 3,
    )(dcat, c, lnw, lnb)


def _conv_bwd2(dc, pc, dw, name, comm=None):
    t = pc.shape[0]
    tm = _tile(t, TM_VPU)
    hb = tm // HALO
    last_blk = t // HALO - 1

    def body(dc_ref, dn_ref, pc_ref, halo_ref, dw_ref, dpc_ref, ddw_ref, s, sd, acc):
        i = pl.program_id(0)
        _fill_glu(s.at[0], pc_ref, halo_ref, i == 0, tm)
        sd[0, pl.ds(0, tm), :] = dc_ref[...]
        sd[0, pl.ds(tm, HALO), :] = jnp.where(i == pl.num_programs(0) - 1, 0.0, dn_ref[...])
        _phase_copies(s)
        _phase_copies(sd)
        for r0 in range(0, tm, CONV_ROWS):
            rows = pl.ds(r0, CONV_ROWS)
            dcb = sd[0, rows, :]
            dglu = None
            for j in range(CW):
                term = _rows_at(sd, r0 + CW - 1 - j, CONV_ROWS) * dw_ref[pl.ds(j, 1), :]
                dglu = term if dglu is None else dglu + term
                part = dcb * _rows_at(s, r0 + HALO - CW + 1 + j, CONV_ROWS)
                part8 = part[0:8]
                for q in range(8, CONV_ROWS, 8):
                    part8 = part8 + part[q:q + 8]
                if r0 == 0:
                    acc[pl.ds(8 * j, 8), :] = part8
                else:
                    acc[pl.ds(8 * j, 8), :] += part8
            a = pc_ref[rows, :DC]
            sg = jax.nn.sigmoid(pc_ref[rows, DC:])
            dpc_ref[rows, :DC] = dglu * sg
            dpc_ref[rows, DC:] = dglu * a * sg * (1.0 - sg)
        rows = [jnp.sum(acc[pl.ds(8 * j, 8), :], axis=0, keepdims=True) for j in range(CW)]

        @pl.when(i == 0)
        def _():
            for j in range(CW):
                ddw_ref[pl.ds(j, 1), :] = rows[j]

        @pl.when(i != 0)
        def _():
            for j in range(CW):
                ddw_ref[pl.ds(j, 1), :] += rows[j]

    outs, got = _pcall(
        body, name=name, grid=(t // tm,),
        in_specs=[pl.BlockSpec((tm, DC), lambda i: (i, 0)),
                  pl.BlockSpec((HALO, DC), lambda i: (jnp.minimum((i + 1) * hb, last_blk), 0)),
                  pl.BlockSpec((tm, 2 * DC), lambda i: (i, 0)),
                  pl.BlockSpec((HALO, 2 * DC), lambda i: (jnp.maximum(i * hb - 1, 0), 0)),
                  _full((CW, DC))],
        out_specs=[pl.BlockSpec((tm, 2 * DC), lambda i: (i, 0)), _full((CW, DC))],
        out_shape=[_sds((t, 2 * DC)), _sds((CW, DC))],
        scratch_shapes=[pltpu.VMEM((8, HALO + tm, DC), f32), pltpu.VMEM((8, tm + HALO, DC), f32),
                        pltpu.VMEM((8 * CW, DC), f32)], args=(dc, dc, pc, pc, dw), comm=comm)
    return outs if comm is None else (outs, got)


def _adamw(w, g, m, v):
    m = ADAM_B1 * m + (1.0 - ADAM_B1) * g
    v = ADAM_B2 * v + (1.0 - ADAM_B2) * (g * g)
    m_hat = m / (1.0 - ADAM_B1 ** ADAM_STEP)
    v_hat = v / (1.0 - ADAM_B2 ** ADAM_STEP)
    delta = -ADAM_LR * (m_hat / (jnp.sqrt(v_hat) + ADAM_EPS) + ADAM_WD * w)
    return delta, m, v


def _adam_slots(w, slots, m, v, name):
    r, c = w.shape
    tr = next(cand for cand in (512, 352, 256, 128, r) if r % cand == 0)

    def body(w_ref, s_ref, m_ref, v_ref, g_ref, d_ref, nm_ref, nv_ref):
        g = s_ref[0].astype(f32)
        for k in range(1, NDEV):
            g = g + s_ref[k].astype(f32)
        delta, nm, nv = _adamw(w_ref[...], g, m_ref[...], v_ref[...])
        g_ref[...] = g
        d_ref[...] = delta
        nm_ref[...] = nm
        nv_ref[...] = nv

    blk = pl.BlockSpec((tr, c), lambda i: (i, 0))
    return pl.pallas_call(
        body, name=name, grid=(r // tr,),
        in_specs=[blk, pl.BlockSpec((NDEV, tr, c), lambda i: (0, i, 0)), blk, blk],
        out_specs=[blk] * 4, out_shape=[_sds((r, c))] * 4,
    )(w, slots, m, v)


def _sum_slots(slots, name):
    _, r, c = slots.shape

    def body(s_ref, o_ref):
        g = s_ref[0]
        for k in range(1, NDEV):
            g = g + s_ref[k]
        o_ref[...] = g

    return pl.pallas_call(body, name=name, in_specs=[_full((NDEV, r, c))], out_specs=_full((r, c)),
                          out_shape=_sds((r, c)), grid=(1,))(slots)


def _adam_many(ws, gs, ms, vs, name):
    n = len(ws)
    shapes = [w.shape for w in ws]

    def body(*refs):
        for i in range(n):
            w_ref, g_ref, m_ref, v_ref = refs[i], refs[n + i], refs[2 * n + i], refs[3 * n + i]
            delta, nm, nv = _adamw(w_ref[...], g_ref[...], m_ref[...], v_ref[...])
            refs[4 * n + i][...] = delta
            refs[5 * n + i][...] = nm
            refs[6 * n + i][...] = nv

    outs = pl.pallas_call(body, name=name, grid=(1,), in_specs=[_full(s) for s in shapes] * 4,
                          out_specs=[_full(s) for s in shapes] * 3, out_shape=[_sds(s) for s in shapes] * 3,
                          )(*ws, *gs, *ms, *vs)
    return outs[:n], outs[n:2 * n], outs[2 * n:]


def _pack(pieces, total):
    flat = []
    n = 0
    for p in pieces:
        p = p.reshape(-1)
        pad = (-p.shape[0]) % LANES
        if pad:
            p = jnp.pad(p, (0, pad))
        flat.append(p)
        n += p.shape[0]
    if total > n:
        flat.append(jnp.zeros((total - n,), f32))
    return jnp.concatenate(flat).reshape(total // LANES, LANES)


def _unpack(vec, shapes):
    flat = vec.reshape(-1)
    out, off = [], 0
    for s in shapes:
        n = math.prod(s)
        out.append(flat[off:off + n].reshape(s))
        off += n + (-n) % LANES
    return out


def _round_up(n, m):
    return (n + m - 1) // m * m


def kernel(x, ffn1_norm_pre, ffn1_norm_post, ffn1_w_gu, ffn1_w_down, mix_norm_pre, mix_norm_post, w_in, shift_mu, w_up, w0, a_up, a0, g_up, k_k, k_a, r_k, gn_w, gn_b, conv_dw, conv_b, conv_ln_w, conv_ln_b, w_out, ffn2_norm_pre, ffn2_norm_post, ffn2_w_gu, ffn2_w_down, loss_target, m_ffn1_norm_pre, m_ffn1_norm_post, m_ffn1_w_gu, m_ffn1_w_down, m_mix_norm_pre, m_mix_norm_post, m_w_in, m_shift_mu, m_w_up, m_w0, m_a_up, m_a0, m_g_up, m_k_k, m_k_a, m_r_k, m_gn_w, m_gn_b, m_conv_dw, m_conv_b, m_conv_ln_w, m_conv_ln_b, m_w_out, m_ffn2_norm_pre, m_ffn2_norm_post, m_ffn2_w_gu, m_ffn2_w_down, v_ffn1_norm_pre, v_ffn1_norm_post, v_ffn1_w_gu, v_ffn1_w_down, v_mix_norm_pre, v_mix_norm_post, v_w_in, v_shift_mu, v_w_up, v_w0, v_a_up, v_a0, v_g_up, v_k_k, v_k_a, v_r_k, v_gn_w, v_gn_b, v_conv_dw, v_conv_b, v_conv_ln_w, v_conv_ln_b, v_w_out, v_ffn2_norm_pre, v_ffn2_norm_post, v_ffn2_w_gu, v_ffn2_w_down):
    named = dict(locals())
    t = x.shape[1]
    x0 = x.reshape(t, D)
    tgt = loss_target.reshape(t, D)

    def shard(a):
        return a[0].astype(bf16)

    def shard_t(a):
        return jnp.swapaxes(a[0], 0, 1).astype(bf16)

    (wg1,) = _exchange([shard_t(ffn1_w_gu)], ["gather"], [SLOT42], "gather_wg1")

    def cols(a):
        return jnp.transpose(a, (1, 0, 2)).reshape(a.shape[1], NDEV * a.shape[2])

    lane = jnp.arange(DR, dtype=jnp.int32) // HN
    seg = (lane[:, None] == jnp.arange(LANES, dtype=jnp.int32)[None, :]).astype(f32)
    seg_t = seg.T
    rk_row = r_k.reshape(1, DR)
    post_params = (gn_w, gn_b, rk_row, seg, seg_t)

    (xn1, gu1, h1), (wd1,) = _ffn_in_fwd(x0, ffn1_norm_pre, wg1, "ffn1_in_fwd", comm=(
        [shard(ffn1_w_down)], ["gather"], [SLOT8]))
    wd1 = wd1.reshape(4, FB, D)
    (f1, x1), (win_g, wout_g, wup_g, aup_g, gup_g, cdw_g) = _mm_norm_res_fwd(
        h1, wd1, x0, ffn1_norm_post, 0.5, "ffn1_out_fwd", comm=(
            [shard_t(w_in), shard(w_out), w_up[0], a_up[0], g_up[0], conv_dw[0]], ["gather"] * 6, [SLOT8] * 6))
    win_t = win_g.reshape(DIN, D)
    wout_full = wout_g.reshape(1, D, D)
    wup_full, aup_full, gup_full, cdw_full = cols(wup_g), cols(aup_g), cols(gup_g), cols(cdw_g)
    zeros64 = jnp.zeros((HN, DR), f32)
    wup_pad = jnp.concatenate([wup_full, zeros64], axis=0)
    aup_pad = jnp.concatenate([zeros64, aup_full], axis=0)
    prep_params = (shift_mu, w0, a0, k_k, k_a, wup_pad, aup_pad, gup_full, seg, seg_t)
    hm, ps, pc = _norm_mm_fwd(x1, mix_norm_pre, win_t, "mix_in_fwd")
    rec_in, (wd2,) = _prep_fwd(ps, prep_params, "prep_fwd", comm=([shard(ffn2_w_down)], ["gather"], [SLOT8]))
    wd2 = wd2.reshape(4, FB, D)
    g_gate = rec_in[6]
    rec_in = rec_in[:6]
    (a_c, g_c, p_c, q_c, yl_c), (wg2,) = _r1_fwd(rec_in, "r1_fwd", comm=([shard_t(ffn2_w_gu)], ["gather"], [SLOT42]))
    y_rec, s_all = _r2_fwd(a_c, g_c, q_c, yl_c, "r2_fwd")
    c_conv, out_b = _conv_fwd(pc, cdw_full, conv_b, conv_ln_w, conv_ln_b, "conv_fwd")
    cat = _post_fwd(y_rec, rec_in[0], rec_in[2], rec_in[3], g_gate, out_b, post_params, "post_fwd")
    fm, x2 = _mm_norm_res_fwd(cat.reshape(1, t, D), wout_full, x1, mix_norm_post, 1.0, "mix_out_fwd")
    (xn2, gu2, h2), _ = _ffn_in_fwd(x2, ffn2_norm_pre, wg2, "ffn2_in_fwd")
    f2, dy, loss_part = _mm_norm_res_fwd(h2, wd2, x2, ffn2_norm_post, 0.5, "ffn2_out_fwd", tgt=tgt)

    (df2, dg_post2, dgu2), _ = _post_bwd_nt(dy, f2, ffn2_norm_post, wd2, 0.5, gu2, "ffn2_out_bwd")
    (dx2, dg_pre2), _ = _nt_pre_bwd(dgu2, wg2, x2, ffn2_norm_pre, dy, "ffn2_in_bwd")
    dwd2 = _wgrad(h2, df2.reshape(1, t, D), True, False, D, "ffn2_wd_grad")
    dwg2 = _wgrad(dgu2.reshape(8, t, FB), xn2.reshape(1, t, D), True, False, D, "ffn2_wg_grad")
    (dfm, dg_postm, dcat), _ = _post_bwd_nt(dx2, fm, mix_norm_post, wout_full, 1.0, None, "mix_out_bwd")
    dwout = _wgrad(cat.reshape(1, t, D), dfm.reshape(1, t, D), False, False, D, "wout_grad")
    dcat = dcat.reshape(t, D)
    (dy_rec, dr2, dkp2, dv2, dgate, dgn_w, dgn_b, drk), (s_wout,) = _post_bwd(
        dcat, y_rec, rec_in[0], rec_in[2], rec_in[3], g_gate, post_params, "post_bwd", comm=(
            [dwout.reshape(NDEV, D // NDEV, D)], ["scatter"], [SLOT8]))
    dq_c, da_c, dg_c = _r2_bwd(dy_rec, q_c, s_all, a_c, "r2_bwd")
    rec_grads, (s_wg2,) = _r1_bwd(rec_in, p_c, da_c, dg_c, dq_c, dy_rec, "r1_bwd", comm=(
        [dwg2.reshape(4, 2, FB, D)], ["scatter"], [SLOT42]))
    prep_out = _prep_bwd(ps, prep_params, rec_grads, (dr2, dkp2, dv2, dgate), "prep_bwd")
    dpsl, dw0, da0, dkk, dka, dwup_pad, daup_pad, dgup = prep_out
    dc_conv, dlnw, dlnb, dcb = _conv_bwd1(dcat, c_conv, conv_ln_w, conv_ln_b, "conv_bwd1")
    (dpc, dcdw), (s_wd2,) = _conv_bwd2(dc_conv, pc, cdw_full, "conv_bwd2", comm=(
        [dwd2.reshape(NDEV, F // NDEV, D)], ["scatter"], [SLOT8]))
    dp, dmu = _shift_bwd(dpsl, ps, shift_mu, dpc, "shift_bwd")
    (dx1, dg_prem), _ = _nt_pre_bwd(dp.reshape(1, 1, t, DIN), win_t.reshape(1, 1, DIN, D), x1, mix_norm_pre, dx2,
                                    "mix_in_bwd")
    dwin_s = _wgrad(dp.reshape(1, t, DIN), hm.reshape(1, t, D), False, False, D // 2, "win_grad").reshape(
        NDEV, DIN // NDEV, D)
    (df1, dg_post1, dgu1), (s_win,) = _post_bwd_nt(dx1, f1, ffn1_norm_post, wd1, 0.5, gu1, "ffn1_out_bwd", comm=(
        [dwin_s], ["scatter"], [SLOT8]))
    dwd1 = _wgrad(h1, df1.reshape(1, t, D), True, False, D, "ffn1_wd_grad")
    dwg1, (s_wd1,) = _wgrad(dgu1.reshape(8, t, FB), xn1.reshape(1, t, D), True, False, D, "ffn1_wg_grad", comm=(
        [dwd1.reshape(NDEV, F // NDEV, D)], ["scatter"], [SLOT8]))
    wg1_sends, wg1_recvs, dwg1_thru, wg1_land, started = _scatter_start(dwg1.reshape(4, 2, FB, D), SLOT42, "wg1_scatter_start")
    (dx0, dg_pre1), _ = _nt_pre_bwd(dgu1, wg1, x0, ffn1_norm_pre + started[:1, :1], dx1, "ffn1_in_bwd")

    res = {}

    def adam_big(nm, w, s, m, v, transposed):
        view = (lambda a: jnp.swapaxes(a[0], 0, 1)) if transposed else (lambda a: a[0])
        outs = _adam_slots(view(w), s, view(m), view(v), "adam_" + nm)
        done.append(outs[1])
        res[nm] = [(jnp.swapaxes(o, 0, 1) if transposed else o)[None] for o in outs]

    done = []

    adam_big("ffn2_w_gu", ffn2_w_gu, s_wg2.reshape(NDEV, FB, D), m_ffn2_w_gu, v_ffn2_w_gu, True)
    adam_big("ffn2_w_down", ffn2_w_down, s_wd2, m_ffn2_w_down, v_ffn2_w_down, False)
    adam_big("w_in", w_in, s_win, m_w_in, v_w_in, True)
    adam_big("w_out", w_out, s_wout, m_w_out, v_w_out, False)
    adam_big("ffn1_w_down", ffn1_w_down, s_wd1, m_ffn1_w_down, v_ffn1_w_down, False)
    _, s_wg1 = _scatter_wait(wg1_sends, wg1_recvs, dwg1_thru, wg1_land, [dg_pre1] + done, SLOT42, "wg1_scatter_wait")
    my_slot = SLOT42[1](4 * lax.axis_index("x") + 2 * lax.axis_index("y") + lax.axis_index("c"))
    own = lax.dynamic_slice(dwg1.reshape(4, 2, FB, D), (*my_slot, 0, 0), (1, 1, FB, D))
    s_wg1 = lax.dynamic_update_slice(s_wg1, own, (*my_slot, 0, 0))
    rep_shapes = [(1, 1)] + [(1, D)] * 6 + [(1, DS)] + [(1, DR)] * 10
    rep_parts = [loss_part, dg_pre1, dg_post1, dg_prem, dg_postm, dg_pre2, dg_post2, dmu,
                 dw0, da0, dkk, dka, drk, dgn_w, dgn_b, dcb, dlnw, dlnb]
    n_rep = _round_up(sum(_round_up(math.prod(s), LANES) for s in rep_shapes), 8 * LANES)
    sh_shapes = [(HN, HN), (HN, HN), (LANES, HN), (CW, HN)]
    n_sh = _round_up(sum(_round_up(math.prod(s), LANES) for s in sh_shapes), 8 * LANES)
    sh_parts = []
    for a in (dwup_pad[:HN], daup_pad[HN:], dgup, dcdw):
        rows = jnp.transpose(a.reshape(a.shape[0], NDEV, HN), (1, 0, 2)).reshape(NDEV, a.shape[0] * HN)
        sh_parts.append(jnp.pad(rows, ((0, 0), (0, (-rows.shape[1]) % LANES))))
    sh_vec = jnp.concatenate(sh_parts, axis=1)
    sh_vec = jnp.pad(sh_vec, ((0, 0), (0, n_sh - sh_vec.shape[1]))).reshape(NDEV, n_sh // LANES, LANES)
    rep_vec, sh_vec, s_wg1 = lax.optimization_barrier((_pack(rep_parts, n_rep), sh_vec, s_wg1))
    s_rep, s_sh = _exchange([rep_vec, sh_vec], ["gather", "scatter"], [SLOT8, SLOT8], "exchange_small")
    adam_big("ffn1_w_gu", ffn1_w_gu, s_wg1.reshape(NDEV, FB, D), m_ffn1_w_gu, v_ffn1_w_gu, True)


    gsum = _unpack(_sum_slots(s_rep, "sum_rep"), rep_shapes)
    gsh = _unpack(_sum_slots(s_sh, "sum_sharded"), sh_shapes)
    loss = gsum[0].reshape(())
    rep_names = ["ffn1_norm_pre", "ffn1_norm_post", "mix_norm_pre", "mix_norm_post", "ffn2_norm_pre", "ffn2_norm_post",
                 "shift_mu", "w0", "a0", "k_k", "k_a", "r_k", "gn_w", "gn_b", "conv_b", "conv_ln_w", "conv_ln_b"]
    shard_names = ["w_up", "a_up", "g_up", "conv_dw"]
    small_g = {n: gsum[1 + i] for i, n in enumerate(rep_names)}
    small_g["r_k"] = small_g["r_k"].reshape(1, NH, HN)
    for i, n in enumerate(shard_names):
        small_g[n] = gsh[i][None]
    names = rep_names + shard_names
    shapes = [named[n].shape for n in names]

    def flat2(a):
        return a.reshape(1, DR) if a.shape == (1, NH, HN) else a.reshape(a.shape[-2:])

    deltas, new_ms, new_vs = _adam_many(*[[flat2(d[n]) for n in names] for d in (
        {n: named[n] for n in names}, small_g, {n: named["m_" + n] for n in names}, {n: named["v_" + n] for n in names})],
        "adam_small")
    for n, shp, dl, nm_, nv_ in zip(names, shapes, deltas, new_ms, new_vs):
        res[n] = [small_g[n], dl.reshape(shp), nm_.reshape(shp), nv_.reshape(shp)]

    order = ["ffn1_norm_pre", "ffn1_norm_post", "ffn1_w_gu", "ffn1_w_down", "mix_norm_pre", "mix_norm_post", "w_in",
             "shift_mu", "w_up", "w0", "a_up", "a0", "g_up", "k_k", "k_a", "r_k", "gn_w", "gn_b", "conv_dw", "conv_b",
             "conv_ln_w", "conv_ln_b", "w_out", "ffn2_norm_pre", "ffn2_norm_post", "ffn2_w_gu", "ffn2_w_down"]
    return (loss, dx0.reshape(1, t, D), *[res[n][0] for n in order], *[res[n][1] for n in order],
            *[res[n][2] for n in order], *[res[n][3] for n in order])
```

```python
import functools
import math

import jax
import jax.numpy as jnp
from jax import lax
from jax.experimental import pallas as pl
from jax.experimental.pallas import tpu as pltpu

f32 = jnp.float32
bf16 = jnp.bfloat16

D = 1024
F = 2816
FB = 704
DR = 512
DC = 512
NH = 8
HN = 64
DS = 1792
DIN = 2816
CW = 31
CHUNK = 64
R1_CHUNKS = 4
R2_CHUNKS = 8
NDEV = 8
RMS_EPS = 1e-6
GN_EPS = 64e-5
LN_EPS = 1e-5
DECAY_SCALE = math.exp(-0.5)
ADAM_LR, ADAM_B1, ADAM_B2, ADAM_EPS, ADAM_WD, ADAM_STEP = 0.001, 0.9, 0.999, 1e-08, 0.01, 10
LANES = 128
HALO = 32
CONV_ROWS = 32
TM_MXU = 1024
TM_SUB = 256
TM_VPU = 512

ANY = pl.BlockSpec(memory_space=pl.ANY)


def _full(shape):
    return pl.BlockSpec(shape, lambda *_: (0,) * len(shape))


def _sds(shape, dtype=f32):
    return jax.ShapeDtypeStruct(shape, dtype)


def _dot(a, b):
    return jnp.dot(a, b, preferred_element_type=f32)


def _dot_nt(a, b):
    return lax.dot_general(a, b, (((1,), (1,)), ((), ())), preferred_element_type=f32)


def _dot_tn(a, b):
    return lax.dot_general(a, b, (((0,), (0,)), ((), ())), preferred_element_type=f32)


def _terms(a, n):
    out, rem = [], a
    for i in range(n):
        t = rem.astype(bf16)
        out.append(t)
        if i + 1 < n:
            rem = rem - t.astype(f32)
    return out


def _mm_raw(a, b, kind, mode):
    nb = a.ndim - 2
    bd = tuple(range(nb))
    ca = nb if kind == "tn" else nb + 1
    cb = nb + 1 if kind == "nt" else nb
    dn = (((ca,), (cb,)), (bd, bd))
    pa, pb = mode[:2]
    ta, tb = _terms(a, pa), _terms(b, pb)
    acc = None
    for i in range(pa):
        for j in range(pb):
            if i + j < max(pa, pb):
                p = lax.dot_general(ta[i], tb[j], dn, preferred_element_type=f32)
                acc = p if acc is None else acc + p
    return acc


@functools.partial(jax.custom_vjp, nondiff_argnums=(2, 3))
def mm(a, b, kind, mode):
    return _mm_raw(a, b, kind, mode)


def _mm_fwd(a, b, kind, mode):
    return _mm_raw(a, b, kind, mode), (a, b)


def _ct_terms(keep):
    return EXACT_TERMS if keep == 1 else keep


def _mm_bwd(kind, mode, res, ct):
    a, b = res
    pa, pb = mode[:2]
    if len(mode) == 4:
        ca = cb = mode[2]
        pa = pb = mode[3]
    elif mode == (1, 1):
        ca = cb = 1
    else:
        ca, cb = _ct_terms(pb), _ct_terms(pa)
    if kind == "nn":
        da = mm(ct, b, "nt", (ca, pb))
        db = mm(a, ct, "tn", (pa, cb))
    elif kind == "nt":
        da = mm(ct, b, "nn", (ca, pb))
        db = mm(ct, a, "tn", (cb, pa))
    else:
        da = mm(b, ct, "nt", (pb, ca))
        db = mm(a, ct, "nn", (pa, cb))
    return da, db


mm.defvjp(_mm_fwd, _mm_bwd)

X3 = (2, 2)
EXACT_TERMS = 2
EA = (1, EXACT_TERMS)
EB = (EXACT_TERMS, 1)
X1 = (1, 1)
RGM = (1, 1, 1, 2)
RIM = X1
RWM = X1
RAM = X1
R2M = X1
LORA = X1


def _rms_fwd(x, g):
    inv = lax.rsqrt(jnp.mean(x * x, axis=-1, keepdims=True) + RMS_EPS)
    return x * inv * g


def _rms_bwd(x, g, dy):
    inv = lax.rsqrt(jnp.mean(x * x, axis=-1, keepdims=True) + RMS_EPS)
    xh = x * inv
    dxh = dy * g
    dg = jnp.sum(dy * xh, axis=0, keepdims=True)
    dx = inv * (dxh - xh * jnp.mean(dxh * xh, axis=-1, keepdims=True))
    return dx, dg


def _tile(t, want):
    return min(t, want)


NPEER = NDEV - 1


def _comm_out_shape(arrays, modes, slots):
    out = []
    for a, mode, (lead, _) in zip(arrays, modes, slots):
        shp = tuple(lead) + tuple(a.shape) if mode == "gather" else tuple(a.shape)
        out.append(pltpu.HBM(shp, a.dtype))
    return out


def _comm_sems(n):
    return [pltpu.SemaphoreType.DMA((n * NPEER,)), pltpu.SemaphoreType.DMA((n * NPEER,)), pltpu.SemaphoreType.DMA((n,))]


RELAYED = (3, 5, 7)


def _comm_copies(ins, outs, sems, modes, slots, want):
    n = len(ins)
    send_sems, recv_sems, loc_sems = sems
    x, y, c = lax.axis_index("x"), lax.axis_index("y"), lax.axis_index("c")
    me = 4 * x + 2 * y + c

    def peer_of(k):
        px = 1 - x if k & 4 else x
        py = 1 - y if k & 2 else y
        pc = 1 - c if k & 1 else c
        return (px, py, pc), 4 * px + 2 * py + pc

    def slot(i, dev):
        return outs[i].at[slots[i][1](dev)]

    if want == "local":
        return [pltpu.make_async_copy(ins[i] if modes[i] == "gather" else ins[i].at[slots[i][1](me)], slot(i, me),
                                      loc_sems.at[i]) for i in range(n)]
    sibling = peer_of(1)[0]
    out = []
    for k in range(1, NDEV):
        peer, pid = peer_of(k)
        for i in range(n):
            sem = dict(send_sem=send_sems.at[i * NPEER + k - 1], recv_sem=recv_sems.at[i * NPEER + k - 1],
                       device_id_type=pl.DeviceIdType.MESH)
            relayed = modes[i] == "gather" and k in RELAYED
            if want == "recv":
                src = ins[i] if modes[i] == "gather" else ins[i].at[slots[i][1](pid)]
                out.append(pltpu.make_async_remote_copy(src_ref=src, dst_ref=slot(i, pid), device_id=peer, **sem))
            elif want == "first" and not relayed:
                src = ins[i] if modes[i] == "gather" else ins[i].at[slots[i][1](pid)]
                out.append(pltpu.make_async_remote_copy(src_ref=src, dst_ref=slot(i, me), device_id=peer, **sem))
            elif want == "relay" and relayed:
                origin = peer_of(k - 1)[1]
                out.append(((k - 2) * n + i, pltpu.make_async_remote_copy(
                    src_ref=slot(i, origin), dst_ref=slot(i, origin), device_id=sibling, **sem)))
    return out


def _comm_start(ins, outs, sems, modes, slots):
    for cp in _comm_copies(ins, outs, sems, modes, slots, "local") + _comm_copies(ins, outs, sems, modes, slots, "first"):
        cp.start()


def _comm_wait(ins, outs, sems, modes, slots):
    recvs = _comm_copies(ins, outs, sems, modes, slots, "recv")
    relays = _comm_copies(ins, outs, sems, modes, slots, "relay")
    for idx, cp in relays:
        recvs[idx].wait_recv()
        cp.start()
    passed_on = {idx for idx, _ in relays}
    for idx, cp in enumerate(recvs):
        if idx not in passed_on:
            cp.wait_recv()
    for cp in _comm_copies(ins, outs, sems, modes, slots, "first") + [cp for _, cp in relays]:
        cp.wait_send()
    for cp in _comm_copies(ins, outs, sems, modes, slots, "local"):
        cp.wait()


def _exchange(arrays, modes, slots, name):
    n = len(arrays)

    def body(*refs):
        ins, outs, sems = refs[:n], refs[n:2 * n], refs[2 * n:]
        _comm_start(ins, outs, sems, modes, slots)
        _comm_wait(ins, outs, sems, modes, slots)

    return pl.pallas_call(
        body, name=name, out_shape=_comm_out_shape(arrays, modes, slots),
        in_specs=[ANY] * n, out_specs=[ANY] * n, scratch_shapes=_comm_sems(n),
    )(*arrays)


HBM_SPEC = pl.BlockSpec(memory_space=pltpu.HBM)
SEM_SPEC = pl.BlockSpec(memory_space=pltpu.SEMAPHORE)


def _scatter_start(arr, slot, name):
    def body(v_ref, land_ref, send_sems, recv_sems, v_thru, land_thru, token):
        for cp in _comm_copies([v_ref], [land_ref], (send_sems, recv_sems, None), ["scatter"], [slot], "first"):
            cp.start()
        token[...] = jnp.zeros_like(token)

    return pl.pallas_call(
        body, name=name,
        out_shape=(pltpu.SemaphoreType.DMA((NPEER,)), pltpu.SemaphoreType.DMA((NPEER,)),
                   pltpu.HBM(arr.shape, arr.dtype), pltpu.HBM(arr.shape, arr.dtype), _sds((8, LANES))),
        in_specs=(HBM_SPEC, HBM_SPEC),
        out_specs=(SEM_SPEC, SEM_SPEC, HBM_SPEC, HBM_SPEC, pl.BlockSpec(memory_space=pltpu.VMEM)),
        input_output_aliases={0: 2, 1: 3},
        compiler_params=pltpu.CompilerParams(has_side_effects=pltpu.SideEffectType.DATAFLOW_SIDE_EFFECTING),
    )(pltpu.with_memory_space_constraint(arr, pltpu.HBM),
      pltpu.with_memory_space_constraint(lax.empty(arr.shape, arr.dtype), pltpu.HBM))


def _scatter_wait(send_sems, recv_sems, v_thru, land_thru, after, slot, name):
    def body(v_ref, land_ref, send_sems, recv_sems, *rest):
        sems = (send_sems, recv_sems, None)
        for cp in _comm_copies([v_ref], [land_ref], sems, ["scatter"], [slot], "first"):
            cp.wait_send()
        for cp in _comm_copies([v_ref], [land_ref], sems, ["scatter"], [slot], "recv"):
            cp.wait_recv()

    return pl.pallas_call(
        body, name=name, out_shape=(pltpu.HBM(v_thru.shape, v_thru.dtype), pltpu.HBM(v_thru.shape, v_thru.dtype)),
        in_specs=(HBM_SPEC, HBM_SPEC, SEM_SPEC, SEM_SPEC) + (ANY,) * len(after), out_specs=(HBM_SPEC, HBM_SPEC),
        input_output_aliases={0: 0, 1: 1},
        compiler_params=pltpu.CompilerParams(has_side_effects=pltpu.SideEffectType.DATAFLOW_SIDE_EFFECTING),
    )(v_thru, land_thru, send_sems, recv_sems, *after)


def _pcall(body, *, name, grid, in_specs, out_specs, out_shape, args, scratch_shapes=(), comm=None):
    if comm is None:
        return pl.pallas_call(body, name=name, grid=grid, in_specs=in_specs, out_specs=out_specs, out_shape=out_shape,
                              scratch_shapes=list(scratch_shapes))(*args), None
    arrays, modes, slots = comm
    n_in, n_out, n_scr, nc = len(args), len(out_shape), len(scratch_shapes), len(arrays)

    def hosted(*refs):
        a_in, c_in = refs[:n_in], refs[n_in:n_in + nc]
        o = n_in + nc
        a_out, c_out = refs[o:o + n_out], refs[o + n_out:o + n_out + nc]
        o += n_out + nc
        a_scr, sems = refs[o:o + n_scr], refs[o + n_scr:]
        first = pl.program_id(0) == 0
        last = pl.program_id(0) == grid[0] - 1
        for ax in range(1, len(grid)):
            first = jnp.logical_and(first, pl.program_id(ax) == 0)
            last = jnp.logical_and(last, pl.program_id(ax) == grid[ax] - 1)

        @pl.when(first)
        def _():
            _comm_start(c_in, c_out, sems, modes, slots)

        body(*a_in, *a_out, *a_scr)

        @pl.when(last)
        def _():
            _comm_wait(c_in, c_out, sems, modes, slots)

    res = pl.pallas_call(
        hosted, name=name, grid=grid, in_specs=list(in_specs) + [ANY] * nc, out_specs=list(out_specs) + [ANY] * nc,
        out_shape=list(out_shape) + _comm_out_shape(arrays, modes, slots),
        scratch_shapes=list(scratch_shapes) + _comm_sems(nc),
    )(*args, *arrays)
    return res[:n_out], res[n_out:]


SLOT8 = ((NDEV,), lambda d: (d,))
SLOT42 = ((4, 2), lambda d: (lax.rem(d, 4), lax.div(d, 4)))


def _ffn_in_fwd(x, g, wg4, name, comm=None):
    t = x.shape[0]
    tm = _tile(t, TM_MXU)
    sub = _tile(tm, TM_SUB)

    def body(x_ref, g_ref, w_ref, xn_ref, gu_ref, h_ref):
        for s in range(tm // sub):
            rows = pl.ds(s * sub, sub)
            xn = _rms_fwd(x_ref[rows, :], g_ref[...]).astype(bf16)
            xn_ref[rows, :] = xn
            gate = _dot_nt(xn, w_ref[0, 0])
            up = _dot_nt(xn, w_ref[0, 1])
            gu_ref[0, 0, rows, :] = gate.astype(bf16)
            gu_ref[0, 1, rows, :] = up.astype(bf16)
            h_ref[0, rows, :] = (gate * jax.nn.sigmoid(gate) * up).astype(bf16)

    return _pcall(
        body, name=name, grid=(t // tm, 4),
        in_specs=[pl.BlockSpec((tm, D), lambda i, e: (i, 0)), _full((1, D)),
                  pl.BlockSpec((1, 2, FB, D), lambda i, e: (e, 0, 0, 0))],
        out_specs=[pl.BlockSpec((tm, D), lambda i, e: (i, 0)),
                   pl.BlockSpec((1, 2, tm, FB), lambda i, e: (e, 0, i, 0)),
                   pl.BlockSpec((1, tm, FB), lambda i, e: (e, i, 0))],
        out_shape=[_sds((t, D), bf16), _sds((4, 2, t, FB), bf16), _sds((4, t, FB), bf16)],
        args=(x, g, wg4), comm=comm)


def _mm_norm_res_fwd(hb, wb, xres, g, scale, name, comm=None, tgt=None):
    e_n, t, k = hb.shape
    tm = _tile(t, TM_MXU)

    def body(h_ref, w_ref, x_ref, g_ref, *rest):
        f = _dot(h_ref[0], w_ref[0])
        for e in range(1, e_n):
            f = f + _dot(h_ref[e], w_ref[e])
        xnew = x_ref[...] + scale * _rms_fwd(f, g_ref[...])
        if tgt is None:
            f_ref, o_ref = rest
            o_ref[...] = xnew
        else:
            t_ref, f_ref, o_ref, l_ref = rest
            err = xnew - t_ref[...]
            o_ref[...] = err * (1.0 / D)
            part = 0.5 * jnp.sum(jnp.mean(err * err, axis=-1, keepdims=True), axis=0, keepdims=True)

            @pl.when(pl.program_id(0) == 0)
            def _():
                l_ref[...] = part

            @pl.when(pl.program_id(0) != 0)
            def _():
                l_ref[...] += part
        f_ref[...] = f

    tile = pl.BlockSpec((tm, D), lambda i: (i, 0))
    outs, got = _pcall(
        body, name=name, grid=(t // tm,),
        in_specs=[pl.BlockSpec((e_n, tm, k), lambda i: (0, i, 0)), _full((e_n, k, D)), tile, _full((1, D))]
        + ([] if tgt is None else [tile]),
        out_specs=[tile, tile] + ([] if tgt is None else [_full((1, 1))]),
        out_shape=[_sds((t, D)), _sds((t, D))] + ([] if tgt is None else [_sds((1, 1))]),
        args=(hb, wb, xres, g) + (() if tgt is None else (tgt,)), comm=comm)
    return outs if comm is None else (outs, got)


def _post_bwd_nt(dxn, f, g, wb, scale, gu, name, comm=None):
    e_n, k, _ = wb.shape
    t = f.shape[0]
    tm = _tile(t, TM_MXU)
    sub = _tile(tm, TM_SUB)
    swiglu = gu is not None

    def body(*refs):
        if swiglu:
            dx_ref, f_ref, g_ref, w_ref, gu_ref, df_ref, dg_ref, dh_ref, df_s = refs
        else:
            dx_ref, f_ref, g_ref, w_ref, df_ref, dg_ref, dh_ref, df_s = refs
        i, e = pl.program_id(0), pl.program_id(1)

        @pl.when(e == 0)
        def _():
            df, dg = _rms_bwd(f_ref[...], g_ref[...], scale * dx_ref[...])
            df_s[...] = df.astype(bf16)
            df_ref[...] = df_s[...]

            @pl.when(i == 0)
            def _():
                dg_ref[...] = dg

            @pl.when(i != 0)
            def _():
                dg_ref[...] += dg

        for s in range(tm // sub):
            rows = pl.ds(s * sub, sub)
            dh = _dot_nt(df_s[rows, :], w_ref[0])
            if swiglu:
                gate = gu_ref[0, 0, rows, :].astype(f32)
                up = gu_ref[0, 1, rows, :].astype(f32)
                sg = jax.nn.sigmoid(gate)
                dh_ref[0, 0, rows, :] = (dh * up * (sg * (1.0 + gate * (1.0 - sg)))).astype(bf16)
                dh_ref[0, 1, rows, :] = (dh * gate * sg).astype(bf16)
            else:
                dh_ref[0, rows, :] = dh

    in_specs = [pl.BlockSpec((tm, D), lambda i, e: (i, 0)), pl.BlockSpec((tm, D), lambda i, e: (i, 0)), _full((1, D)),
                pl.BlockSpec((1, k, D), lambda i, e: (e, 0, 0))]
    args = [dxn, f, g, wb]
    if swiglu:
        in_specs.append(pl.BlockSpec((1, 2, tm, k), lambda i, e: (e, 0, i, 0)))
        args.append(gu)
        dh_spec = pl.BlockSpec((1, 2, tm, k), lambda i, e: (e, 0, i, 0))
        dh_shape = _sds((e_n, 2, t, k), bf16)
    else:
        dh_spec = pl.BlockSpec((1, tm, k), lambda i, e: (e, i, 0))
        dh_shape = _sds((e_n, t, k), f32)
    return _pcall(
        body, name=name, grid=(t // tm, e_n), in_specs=in_specs,
        out_specs=[pl.BlockSpec((tm, D), lambda i, e: (i, 0)), _full((1, D)), dh_spec],
        out_shape=[_sds((t, D), bf16), _sds((1, D)), dh_shape],
        scratch_shapes=[pltpu.VMEM((tm, D), bf16)], args=args, comm=comm)


def _nt_pre_bwd(dy, wb, x, g, dres, name, comm=None):
    e_n, q_n, t, k = dy.shape
    tm = _tile(t, TM_MXU)

    def body(dy_ref, w_ref, x_ref, g_ref, r_ref, dx_ref, dg_ref, acc):
        i, e = pl.program_id(0), pl.program_id(1)
        p = _dot(dy_ref[0, 0], w_ref[0, 0])
        for q in range(1, q_n):
            p = p + _dot(dy_ref[0, q], w_ref[0, q])

        @pl.when(e == 0)
        def _():
            acc[...] = p

        @pl.when(e != 0)
        def _():
            acc[...] += p

        @pl.when(e == e_n - 1)
        def _():
            dx, dg = _rms_bwd(x_ref[...], g_ref[...], acc[...])
            dx_ref[...] = r_ref[...] + dx

            @pl.when(i == 0)
            def _():
                dg_ref[...] = dg

            @pl.when(i != 0)
            def _():
                dg_ref[...] += dg

    return _pcall(
        body, name=name, grid=(t // tm, e_n),
        in_specs=[pl.BlockSpec((1, q_n, tm, k), lambda i, e: (e, 0, i, 0)),
                  pl.BlockSpec((1, q_n, k, D), lambda i, e: (e, 0, 0, 0)),
                  pl.BlockSpec((tm, D), lambda i, e: (i, 0)), _full((1, D)),
                  pl.BlockSpec((tm, D), lambda i, e: (i, 0))],
        out_specs=[pl.BlockSpec((tm, D), lambda i, e: (i, 0)), _full((1, D))],
        out_shape=[_sds((t, D)), _sds((1, D))],
        scratch_shapes=[pltpu.VMEM((tm, D), f32)], args=(dy, wb, x, g, dres), comm=comm)


def _wgrad(a, b, a_batched, b_batched, tn, name, comm=None):
    t, k = a.shape[1], a.shape[2]
    nn = b.shape[2]
    nb = max(a.shape[0], b.shape[0])
    tt = _tile(t, TM_MXU)
    nt = t // tt

    def body(a_ref, b_ref, o_ref, acc):
        @pl.when(pl.program_id(2) == 0)
        def _():
            acc[...] = _dot_tn(a_ref[0], b_ref[0])

        @pl.when(pl.program_id(2) != 0)
        def _():
            acc[...] += _dot_tn(a_ref[0], b_ref[0])

        @pl.when(pl.program_id(2) == nt - 1)
        def _():
            o_ref[0] = acc[...].astype(bf16)

    (out,), got = _pcall(
        body, name=name, grid=(nb, nn // tn, nt),
        in_specs=[pl.BlockSpec((1, tt, k), (lambda n, j, s: (n, s, 0)) if a_batched else (lambda n, j, s: (0, s, 0))),
                  pl.BlockSpec((1, tt, tn), (lambda n, j, s: (n, s, j)) if b_batched else (lambda n, j, s: (0, s, j)))],
        out_specs=[pl.BlockSpec((1, k, tn), lambda n, j, s: (n, 0, j))],
        out_shape=[_sds((nb, k, nn), bf16)],
        scratch_shapes=[pltpu.VMEM((k, tn), f32)], args=(a, b), comm=comm)
    return out if comm is None else (out, got)


def _norm_mm_fwd(x, g, w, name):
    t = x.shape[0]
    tm = _tile(t, TM_VPU)

    def body(x_ref, g_ref, w_ref, xn_ref, ps_ref, pc_ref):
        xn = _rms_fwd(x_ref[...], g_ref[...]).astype(bf16)
        xn_ref[...] = xn
        ps_ref[...] = _dot_nt(xn, w_ref[:DS, :])
        pc_ref[...] = _dot_nt(xn, w_ref[DS:, :])

    return pl.pallas_call(
        body, name=name, grid=(t // tm,),
        in_specs=[pl.BlockSpec((tm, D), lambda i: (i, 0)), _full((1, D)), _full((DIN, D))],
        out_specs=[pl.BlockSpec((tm, D), lambda i: (i, 0)), pl.BlockSpec((tm, DS), lambda i: (i, 0)),
                   pl.BlockSpec((tm, 2 * DC), lambda i: (i, 0))],
        out_shape=[_sds((t, D), bf16), _sds((t, DS)), _sds((t, 2 * DC))],
    )(x, g, w)


def _segsum_b(x, seg, seg_t):
    return mm(mm(x, seg, "nn", EB), seg_t, "nn", EB)


def _prep_fn(psl, w0, a0, k_k, k_a, wup, aup, gup, seg, seg_t):
    r, k, v = psl[:, :DR], psl[:, DR:2 * DR], psl[:, 2 * DR:3 * DR]
    xwa, xg = psl[:, 3 * DR:3 * DR + LANES], psl[:, 3 * DR + LANES:]
    d = w0 + mm(jnp.tanh(xwa), wup, "nn", LORA)
    lw = -DECAY_SCALE * jax.nn.sigmoid(d)
    a = jax.nn.sigmoid(a0 + mm(xwa, aup, "nn", LORA))
    g = mm(jax.nn.sigmoid(xg), gup, "nn", LORA)
    kkr = k * k_k
    kk = kkr * lax.rsqrt(jnp.maximum(_segsum_b(kkr * kkr, seg, seg_t), 1e-12))
    kp = k * (1.0 + (a - 1.0) * k_a)
    return r, lw, kp, v, -kk, kk * a, g


def _shifted(ps, halo_row, first):
    prev = jnp.where(first, 0.0, halo_row)
    sh = pltpu.roll(ps, 1, 0)
    row = lax.broadcasted_iota(jnp.int32, ps.shape, 0)
    return jnp.where(row == 0, prev, sh)


def _heads_of(ref, rows):
    return jnp.stack([ref[rows, h * HN:(h + 1) * HN] for h in range(NH)], axis=0)


def _heads_to(ref, rows, val):
    ref[rows, :] = jnp.concatenate([val[h] for h in range(NH)], axis=-1)


_PREP_PARAM_SHAPES = [(1, DS), (1, DR), (1, DR), (1, DR), (1, DR), (LANES, DR), (LANES, DR), (LANES, DR),
                      (DR, LANES), (LANES, DR)]


def _prep_fwd(ps, params, name, comm=None):
    t = ps.shape[0]
    tm = _tile(t, TM_VPU)
    hb = tm // 8

    def body(ps_ref, halo_ref, mu_ref, *rest):
        prm = [r[...] for r in rest[:9]]
        outs = rest[9:]
        x = ps_ref[...]
        sh = _shifted(x, halo_ref[pl.ds(7, 1), :], pl.program_id(0) == 0)
        psl = x + (sh - x) * mu_ref[...]
        vals = _prep_fn(psl, *prm)
        for ref, val in zip(outs, vals):
            ref[...] = val

    outs, got = _pcall(
        body, name=name, grid=(t // tm,),
        in_specs=[pl.BlockSpec((tm, DS), lambda i: (i, 0)),
                  pl.BlockSpec((8, DS), lambda i: (jnp.maximum(i * hb - 1, 0), 0))]
        + [_full(s) for s in _PREP_PARAM_SHAPES],
        out_specs=[pl.BlockSpec((tm, DR), lambda i: (i, 0))] * 7,
        out_shape=[_sds((t, DR))] * 7, args=(ps, ps, *params), comm=comm)
    return outs if comm is None else (outs, got)


def _prep_bwd(ps, params, cts_hm, cts_tm, name, comm=None):
    t = ps.shape[0]
    tm = _tile(t, TM_VPU)
    hb = tm // 8

    def body(ps_ref, halo_ref, mu_ref, *rest):
        prm = [r[...] for r in rest[:9]]
        chm = rest[9:15]
        ctm = rest[15:19]
        dpsl_ref = rest[19]
        gouts = rest[20:27]
        x = ps_ref[...]
        sh = _shifted(x, halo_ref[pl.ds(7, 1), :], pl.program_id(0) == 0)
        psl = x + (sh - x) * mu_ref[...]
        seg, seg_t = prm[7], prm[8]
        _, vjp = jax.vjp(lambda p, *w: _prep_fn(p, *w, seg, seg_t), psl, *prm[:7])
        c = [r[...] for r in chm]
        cts = (c[0] + ctm[0][...], c[1], c[2] + ctm[1][...], c[3] + ctm[2][...], c[4], c[5], ctm[3][...])
        grads = vjp(cts)
        dpsl_ref[...] = grads[0]
        for ref, gval in zip(gouts, grads[1:]):
            @pl.when(pl.program_id(0) == 0)
            def _(ref=ref, gval=gval):
                ref[...] = gval

            @pl.when(pl.program_id(0) != 0)
            def _(ref=ref, gval=gval):
                ref[...] += gval

    tk = pl.BlockSpec((tm, DR), lambda i: (i, 0))
    gshapes = _PREP_PARAM_SHAPES[1:8]
    outs, got = _pcall(
        body, name=name, grid=(t // tm,),
        in_specs=[pl.BlockSpec((tm, DS), lambda i: (i, 0)),
                  pl.BlockSpec((8, DS), lambda i: (jnp.maximum(i * hb - 1, 0), 0))]
        + [_full(s) for s in _PREP_PARAM_SHAPES] + [tk] * 10,
        out_specs=[pl.BlockSpec((tm, DS), lambda i: (i, 0))] + [_full(s) for s in gshapes],
        out_shape=[_sds((t, DS))] + [_sds(s) for s in gshapes],
        args=(ps, ps, *params, *cts_hm, *cts_tm), comm=comm)
    return outs if comm is None else (outs, got)


def _shift_bwd(dpsl, ps, mu, dpc, name):
    t = ps.shape[0]
    tm = _tile(t, TM_VPU)
    hb = tm // 8
    last_blk = t // 8 - 1

    def body(d_ref, dn_ref, ps_ref, halo_ref, mu_ref, dpc_ref, dp_ref, dmu_ref):
        i = pl.program_id(0)
        mu_v = mu_ref[...]
        d = d_ref[...]
        nxt = jnp.where(i == pl.num_programs(0) - 1, 0.0, dn_ref[pl.ds(0, 1), :])
        up = pltpu.roll(d, tm - 1, 0)
        row = lax.broadcasted_iota(jnp.int32, d.shape, 0)
        up = jnp.where(row == tm - 1, nxt, up)
        dp_ref[:, :DS] = (d * (1.0 - mu_v) + up * mu_v).astype(bf16)
        dp_ref[:, DS:] = dpc_ref[...].astype(bf16)
        x = ps_ref[...]
        sh = _shifted(x, halo_ref[pl.ds(7, 1), :], i == 0)
        dmu = jnp.sum(d * (sh - x), axis=0, keepdims=True)

        @pl.when(i == 0)
        def _():
            dmu_ref[...] = dmu

        @pl.when(i != 0)
        def _():
            dmu_ref[...] += dmu

    return pl.pallas_call(
        body, name=name, grid=(t // tm,),
        in_specs=[pl.BlockSpec((tm, DS), lambda i: (i, 0)),
                  pl.BlockSpec((8, DS), lambda i: (jnp.minimum((i + 1) * hb, last_blk), 0)),
                  pl.BlockSpec((tm, DS), lambda i: (i, 0)),
                  pl.BlockSpec((8, DS), lambda i: (jnp.maximum(i * hb - 1, 0), 0)),
                  _full((1, DS)), pl.BlockSpec((tm, 2 * DC), lambda i: (i, 0))],
        out_specs=[pl.BlockSpec((tm, DIN), lambda i: (i, 0)), _full((1, DS))],
        out_shape=[_sds((t, DIN), bf16), _sds((1, DS))],
    )(dpsl, dpsl, ps, ps, mu, dpc)


def _post_fn(y, r, kp, v, g, gn_w, gn_b, r_k, seg, seg_t):
    mu = _segsum_b(y, seg, seg_t) * (1.0 / HN)
    yc = y - mu
    var = _segsum_b(yc * yc, seg, seg_t) * (1.0 / HN)
    yo = yc * lax.rsqrt(var + GN_EPS) * gn_w + gn_b
    bonus = _segsum_b(r * kp * r_k, seg, seg_t) * v
    return (yo + bonus) * g


_POST_PARAM_SHAPES = [(1, DR), (1, DR), (1, DR), (DR, LANES), (LANES, DR)]


def _post_fwd(y, r, kp, v, g, out_b, params, name):
    t = g.shape[0]
    tm = _tile(t, TM_VPU)

    def body(y_ref, r_ref, k_ref, v_ref, g_ref, ob_ref, *rest):
        prm = [p[...] for p in rest[:5]]
        cat_ref = rest[5]
        oa = _post_fn(y_ref[...], r_ref[...], k_ref[...], v_ref[...], g_ref[...], *prm)
        cat_ref[:, :DR] = oa.astype(bf16)
        cat_ref[:, DR:] = ob_ref[...]

    tk = pl.BlockSpec((tm, DR), lambda i: (i, 0))
    return pl.pallas_call(
        body, name=name, grid=(t // tm,),
        in_specs=[tk] * 6 + [_full(s) for s in _POST_PARAM_SHAPES],
        out_specs=pl.BlockSpec((tm, 2 * DR), lambda i: (i, 0)),
        out_shape=_sds((t, 2 * DR), bf16),
    )(y, r, kp, v, g, out_b, *params)


def _post_bwd(dcat, y, r, kp, v, g, params, name, comm=None):
    t = g.shape[0]
    tm = _tile(t, TM_VPU)

    def body(dc_ref, y_ref, r_ref, k_ref, v_ref, g_ref, *rest):
        prm = [p[...] for p in rest[:5]]
        dy_ref, dr_ref, dk_ref, dv_ref, dg_ref = rest[5:10]
        gouts = rest[10:13]
        seg, seg_t = prm[3], prm[4]
        _, vjp = jax.vjp(lambda *a: _post_fn(*a, seg, seg_t),
                         y_ref[...], r_ref[...], k_ref[...], v_ref[...], g_ref[...], *prm[:3])
        grads = vjp(dc_ref[...])
        dy_ref[...] = grads[0]
        dr_ref[...] = grads[1]
        dk_ref[...] = grads[2]
        dv_ref[...] = grads[3]
        dg_ref[...] = grads[4]
        for ref, gval in zip(gouts, grads[5:]):
            @pl.when(pl.program_id(0) == 0)
            def _(ref=ref, gval=gval):
                ref[...] = gval

            @pl.when(pl.program_id(0) != 0)
            def _(ref=ref, gval=gval):
                ref[...] += gval

    tk = pl.BlockSpec((tm, DR), lambda i: (i, 0))
    return _pcall(
        body, name=name, grid=(t // tm,),
        in_specs=[tk] * 6 + [_full(s) for s in _POST_PARAM_SHAPES],
        out_specs=[tk] * 5 + [_full((1, DR))] * 3,
        out_shape=[_sds((t, DR))] * 5 + [_sds((1, DR))] * 3,
        args=(dcat, y, r, kp, v, g, *params), comm=comm)


@jax.custom_vjp
def _inv_unit_lower(lb):
    c = lb.shape[-1]
    row = lax.broadcasted_iota(jnp.int32, (c, c), 0)
    col = lax.broadcasted_iota(jnp.int32, (c, c), 1)
    p = (row == col).astype(f32)[None] + lb
    m = lb
    for _ in range(c.bit_length() - 2):
        m = mm(m, m, "nn", RIM)
        p = p + mm(m, p, "nn", RIM)
    return p


def _inv_unit_lower_fwd(lb):
    p = _inv_unit_lower(lb)
    return p, p


def _inv_unit_lower_bwd(p, ct):
    return (mm(mm(p, ct, "tn", RIM), p, "nt", RIM),)


_inv_unit_lower.defvjp(_inv_unit_lower_fwd, _inv_unit_lower_bwd)


@jax.custom_vjp
def _known_inverse(lb, p):
    return p


def _known_inverse_fwd(lb, p):
    return p, p


def _known_inverse_bwd(p, ct):
    return mm(mm(p, ct, "tn", RIM), p, "nt", RIM), jnp.zeros_like(p)


_known_inverse.defvjp(_known_inverse_fwd, _known_inverse_bwd)


def _r1_fn(r, lw, k, v, z, b, p_saved=None, with_p=False):
    c = r.shape[1]
    row = lax.broadcasted_iota(jnp.int32, (c, c), 0)
    col = lax.broadcasted_iota(jnp.int32, (c, c), 1)
    incl = (row >= col)[None]
    strict = (row > col)[None]
    eye = (row == col).astype(f32)[None]
    tri = jnp.broadcast_to((row >= col).astype(f32)[None], (r.shape[0], c, c))
    cum = mm(tri, lw, "nn", EA)
    tot = jnp.sum(lw, axis=1, keepdims=True)
    zt = z * jnp.exp(cum - lw)
    rt = r * jnp.exp(cum)
    g_inv = jnp.exp(-cum)
    g_rem = jnp.exp(tot - cum)
    bt, kt = b * g_inv, k * g_inv
    bh, kh = b * g_rem, k * g_rem
    lb = jnp.where(strict, mm(zt, bt, "nt", RGM), 0.0)
    lk = jnp.where(strict, mm(zt, kt, "nt", RGM), 0.0)
    mb = jnp.where(incl, mm(rt, bt, "nt", RGM), 0.0)
    mk = jnp.where(incl, mm(rt, kt, "nt", RGM), 0.0)
    p = _inv_unit_lower(lb) if p_saved is None else _known_inverse(lb, p_saved)
    w1 = mm(p, zt, "nn", RWM)
    w2 = mm(p, mm(lk, v, "nn", RWM), "nn", RWM)
    a = mm(w1, bh, "tn", RAM) + eye * jnp.exp(tot)
    g = mm(w2, bh, "tn", RAM) + mm(v, kh, "tn", RAM)
    q = rt + mm(mb, w1, "nn", RWM)
    yl = mm(mb, w2, "nn", RWM) + mm(mk, v, "nn", RWM)
    return (a, g, q, yl, p) if with_p else (a, g, q, yl)


def _chunks_in(ref, n):
    return jnp.concatenate([_heads_of(ref, pl.ds(s * CHUNK, CHUNK)) for s in range(n)], axis=0)


def _chunks_out(ref, val, n):
    for s in range(n):
        _heads_to(ref, pl.ds(s * CHUNK, CHUNK), val[s * NH:(s + 1) * NH])


def _r1_fwd(ins, name, comm=None):
    t = ins[0].shape[0]
    n = R1_CHUNKS
    nc = t // CHUNK

    def body(r, lw, k, v, z, b, a_ref, g_ref, p_ref, q_ref, yl_ref):
        a, g, q, yl, p = _r1_fn(*[_chunks_in(x, n) for x in (r, lw, k, v, z, b)], with_p=True)
        for s in range(n):
            a_ref[s] = a[s * NH:(s + 1) * NH]
            g_ref[s] = g[s * NH:(s + 1) * NH]
            p_ref[s] = p[s * NH:(s + 1) * NH]
        _chunks_out(q_ref, q, n)
        _chunks_out(yl_ref, yl, n)

    ck = pl.BlockSpec((n * CHUNK, DR), lambda c: (c, 0))
    st = pl.BlockSpec((n, NH, HN, HN), lambda c: (c, 0, 0, 0))
    return _pcall(
        body, name=name, grid=(nc // n,), in_specs=[ck] * 6, out_specs=[st, st, st, ck, ck],
        out_shape=[_sds((nc, NH, HN, HN))] * 3 + [_sds((t, DR))] * 2, args=tuple(ins), comm=comm)


def _r1_bwd(ins, p_inv, da, dg, dq, dyl, name, comm=None):
    t = ins[0].shape[0]
    n = R1_CHUNKS
    nc = t // CHUNK

    def body(r, lw, k, v, z, b, p_ref, da_ref, dg_ref, dq_ref, dyl_ref, *outs):
        p_saved = jnp.concatenate([p_ref[s] for s in range(n)], axis=0)
        _, vjp = jax.vjp(lambda *a: _r1_fn(*a, p_saved=p_saved), *[_chunks_in(x, n) for x in (r, lw, k, v, z, b)])
        cts = (jnp.concatenate([da_ref[s] for s in range(n)], axis=0),
               jnp.concatenate([dg_ref[s] for s in range(n)], axis=0), _chunks_in(dq_ref, n), _chunks_in(dyl_ref, n))
        for ref, gval in zip(outs, vjp(cts)):
            _chunks_out(ref, gval, n)

    ck = pl.BlockSpec((n * CHUNK, DR), lambda c: (c, 0))
    st = pl.BlockSpec((n, NH, HN, HN), lambda c: (c, 0, 0, 0))
    return _pcall(
        body, name=name, grid=(nc // n,), in_specs=[ck] * 6 + [st, st, st, ck, ck], out_specs=[ck] * 6,
        out_shape=[_sds((t, DR))] * 6, args=(*ins, p_inv, da, dg, dq, dyl), comm=comm)


def _r2_fwd(a, g, q, yl, name):
    nc = a.shape[0]
    t = q.shape[0]

    def body(a_ref, g_ref, q_ref, yl_ref, y_ref, s_ref, s):
        @pl.when(pl.program_id(0) == 0)
        def _():
            s[...] = jnp.zeros_like(s)
        s0 = s[...]
        for j in range(n):
            rows = pl.ds(j * CHUNK, CHUNK)
            s_ref[j] = s0
            _heads_to(y_ref, rows, mm(_heads_of(q_ref, rows), s0, "nt", R2M) + _heads_of(yl_ref, rows))
            s0 = mm(s0, a_ref[j], "nn", R2M) + g_ref[j]
        s[...] = s0

    n = math.gcd(nc, R2_CHUNKS)
    ck = pl.BlockSpec((n * CHUNK, DR), lambda c: (c, 0))
    st = pl.BlockSpec((n, NH, HN, HN), lambda c: (c, 0, 0, 0))
    return pl.pallas_call(
        body, name=name, grid=(nc // n,), in_specs=[st, st, ck, ck], out_specs=[ck, st],
        out_shape=[_sds((t, DR)), _sds((nc, NH, HN, HN))],
        scratch_shapes=[pltpu.VMEM((NH, HN, HN), f32)],
    )(a, g, q, yl)


def _r2_bwd(dy, q, s_all, a, name):
    nc = a.shape[0]
    t = q.shape[0]

    def body(dy_ref, q_ref, s_ref, a_ref, dq_ref, da_ref, dg_ref, ds):
        @pl.when(pl.program_id(0) == 0)
        def _():
            ds[...] = jnp.zeros_like(ds)
        dsn = ds[...]
        for j in reversed(range(n)):
            rows = pl.ds(j * CHUNK, CHUNK)
            s0 = s_ref[j]
            dyv = _heads_of(dy_ref, rows)
            dg_ref[j] = dsn
            da_ref[j] = mm(s0, dsn, "tn", R2M)
            _heads_to(dq_ref, rows, mm(dyv, s0, "nn", R2M))
            dsn = mm(dsn, a_ref[j], "nt", R2M) + mm(dyv, _heads_of(q_ref, rows), "tn", R2M)
        ds[...] = dsn

    n = math.gcd(nc, R2_CHUNKS)
    nb = nc // n
    ck = pl.BlockSpec((n * CHUNK, DR), lambda c: (nb - 1 - c, 0))
    st = pl.BlockSpec((n, NH, HN, HN), lambda c: (nb - 1 - c, 0, 0, 0))
    return pl.pallas_call(
        body, name=name, grid=(nb,), in_specs=[ck, ck, st, st], out_specs=[ck, st, st],
        out_shape=[_sds((t, DR)), _sds((nc, NH, HN, HN)), _sds((nc, NH, HN, HN))],
        scratch_shapes=[pltpu.VMEM((NH, HN, HN), f32)],
    )(dy, q, s_all, a)


def _ln_silu(c, w, b):
    mu = jnp.mean(c, axis=-1, keepdims=True)
    cc = c - mu
    var = jnp.mean(cc * cc, axis=-1, keepdims=True)
    u = cc * lax.rsqrt(var + LN_EPS) * w + b
    return u * jax.nn.sigmoid(u)


def _glu_tile(pc):
    return pc[:, :DC] * jax.nn.sigmoid(pc[:, DC:])


def _fill_glu(s_ref, pc_ref, halo_ref, first, tm):
    s_ref[pl.ds(0, HALO), :] = jnp.where(first, 0.0, _glu_tile(halo_ref[...]))
    s_ref[pl.ds(HALO, tm), :] = _glu_tile(pc_ref[...])


def _phase_copies(sh):
    n = sh.shape[1] - 8
    for r in range(1, 8):
        sh[r, pl.ds(0, n), :] = sh[0, pl.ds(r, n), :]


def _rows_at(sh, o, n):
    return sh[o % 8, pl.ds(o - o % 8, n), :]


def _conv_fwd(pc, dw, cb, lnw, lnb, name):
    t = pc.shape[0]
    tm = _tile(t, TM_VPU)
    hb = tm // HALO

    def body(pc_ref, halo_ref, dw_ref, cb_ref, w_ref, b_ref, c_ref, o_ref, s):
        _fill_glu(s.at[0], pc_ref, halo_ref, pl.program_id(0) == 0, tm)
        _phase_copies(s)
        for r0 in range(0, tm, CONV_ROWS):
            rows = pl.ds(r0, CONV_ROWS)
            acc = _rows_at(s, r0 + HALO - CW + 1, CONV_ROWS) * dw_ref[pl.ds(0, 1), :]
            for j in range(1, CW):
                acc = acc + _rows_at(s, r0 + HALO - CW + 1 + j, CONV_ROWS) * dw_ref[pl.ds(j, 1), :]
            c = acc + cb_ref[...]
            c_ref[rows, :] = c
            o_ref[rows, :] = _ln_silu(c, w_ref[...], b_ref[...]).astype(bf16)

    return pl.pallas_call(
        body, name=name, grid=(t // tm,),
        in_specs=[pl.BlockSpec((tm, 2 * DC), lambda i: (i, 0)),
                  pl.BlockSpec((HALO, 2 * DC), lambda i: (jnp.maximum(i * hb - 1, 0), 0)),
                  _full((CW, DC)), _full((1, DC)), _full((1, DC)), _full((1, DC))],
        out_specs=[pl.BlockSpec((tm, DC), lambda i: (i, 0)), pl.BlockSpec((tm, DC), lambda i: (i, 0))],
        out_shape=[_sds((t, DC)), _sds((t, DC), bf16)],
        scratch_shapes=[pltpu.VMEM((8, HALO + tm, DC), f32)],
    )(pc, pc, dw, cb, lnw, lnb)


def _conv_bwd1(dcat, c, lnw, lnb, name):
    t = c.shape[0]
    tm = _tile(t, TM_VPU)

    def body(d_ref, c_ref, w_ref, b_ref, dc_ref, dw_ref, db_ref, dcb_ref):
        _, vjp = jax.vjp(_ln_silu, c_ref[...], w_ref[...], b_ref[...])
        dc, dw, db = vjp(d_ref[...])
        dc_ref[...] = dc
        dcb = jnp.sum(dc, axis=0, keepdims=True)
        for ref, gval in ((dw_ref, dw), (db_ref, db), (dcb_ref, dcb)):
            @pl.when(pl.program_id(0) == 0)
            def _(ref=ref, gval=gval):
                ref[...] = gval

            @pl.when(pl.program_id(0) != 0)
            def _(ref=ref, gval=gval):
                ref[...] += gval

    tk = pl.BlockSpec((tm, DC), lambda i: (i, 0))
    return pl.pallas_call(
        body, name=name, grid=(t // tm,),
        in_specs=[pl.BlockSpec((tm, DC), lambda i: (i, 1)), tk, _full((1, DC)), _full((1, DC))],
        out_specs=[tk] + [_full((1, DC))] * 3,
        out_shape=[_sds((t, DC))] + [_sds((1, DC))] * 3,
    )(dcat, c, lnw, lnb)


def _conv_bwd2(dc, pc, dw, name, comm=None):
    t = pc.shape[0]
    tm = _tile(t, TM_VPU)
    hb = tm // HALO
    last_blk = t // HALO - 1

    def body(dc_ref, dn_ref, pc_ref, halo_ref, dw_ref, dpc_ref, ddw_ref, s, sd, acc):
        i = pl.program_id(0)
        _fill_glu(s.at[0], pc_ref, halo_ref, i == 0, tm)
        sd[0, pl.ds(0, tm), :] = dc_ref[...]
        sd[0, pl.ds(tm, HALO), :] = jnp.where(i == pl.num_programs(0) - 1, 0.0, dn_ref[...])
        _phase_copies(s)
        _phase_copies(sd)
        for r0 in range(0, tm, CONV_ROWS):
            rows = pl.ds(r0, CONV_ROWS)
            dcb = sd[0, rows, :]
            dglu = None
            for j in range(CW):
                term = _rows_at(sd, r0 + CW - 1 - j, CONV_ROWS) * dw_ref[pl.ds(j, 1), :]
                dglu = term if dglu is None else dglu + term
                part = dcb * _rows_at(s, r0 + HALO - CW + 1 + j, CONV_ROWS)
                part8 = part[0:8]
                for q in range(8, CONV_ROWS, 8):
                    part8 = part8 + part[q:q + 8]
                if r0 == 0:
                    acc[pl.ds(8 * j, 8), :] = part8
                else:
                    acc[pl.ds(8 * j, 8), :] += part8
            a = pc_ref[rows, :DC]
            sg = jax.nn.sigmoid(pc_ref[rows, DC:])
            dpc_ref[rows, :DC] = dglu * sg
            dpc_ref[rows, DC:] = dglu * a * sg * (1.0 - sg)
        rows = [jnp.sum(acc[pl.ds(8 * j, 8), :], axis=0, keepdims=True) for j in range(CW)]

        @pl.when(i == 0)
        def _():
            for j in range(CW):
                ddw_ref[pl.ds(j, 1), :] = rows[j]

        @pl.when(i != 0)
        def _():
            for j in range(CW):
                ddw_ref[pl.ds(j, 1), :] += rows[j]

    outs, got = _pcall(
        body, name=name, grid=(t // tm,),
        in_specs=[pl.BlockSpec((tm, DC), lambda i: (i, 0)),
                  pl.BlockSpec((HALO, DC), lambda i: (jnp.minimum((i + 1) * hb, last_blk), 0)),
                  pl.BlockSpec((tm, 2 * DC), lambda i: (i, 0)),
                  pl.BlockSpec((HALO, 2 * DC), lambda i: (jnp.maximum(i * hb - 1, 0), 0)),
                  _full((CW, DC))],
        out_specs=[pl.BlockSpec((tm, 2 * DC), lambda i: (i, 0)), _full((CW, DC))],
        out_shape=[_sds((t, 2 * DC)), _sds((CW, DC))],
        scratch_shapes=[pltpu.VMEM((8, HALO + tm, DC), f32), pltpu.VMEM((8, tm + HALO, DC), f32),
                        pltpu.VMEM((8 * CW, DC), f32)], args=(dc, dc, pc, pc, dw), comm=comm)
    return outs if comm is None else (outs, got)


def _adamw(w, g, m, v):
    m = ADAM_B1 * m + (1.0 - ADAM_B1) * g
    v = ADAM_B2 * v + (1.0 - ADAM_B2) * (g * g)
    m_hat = m / (1.0 - ADAM_B1 ** ADAM_STEP)
    v_hat = v / (1.0 - ADAM_B2 ** ADAM_STEP)
    delta = -ADAM_LR * (m_hat / (jnp.sqrt(v_hat) + ADAM_EPS) + ADAM_WD * w)
    return delta, m, v


def _adam_slots(w, slots, m, v, name):
    r, c = w.shape
    tr = next(cand for cand in (512, 352, 256, 128, r) if r % cand == 0)

    def body(w_ref, s_ref, m_ref, v_ref, g_ref, d_ref, nm_ref, nv_ref):
        g = s_ref[0].astype(f32)
        for k in range(1, NDEV):
            g = g + s_ref[k].astype(f32)
        delta, nm, nv = _adamw(w_ref[...], g, m_ref[...], v_ref[...])
        g_ref[...] = g
        d_ref[...] = delta
        nm_ref[...] = nm
        nv_ref[...] = nv

    blk = pl.BlockSpec((tr, c), lambda i: (i, 0))
    return pl.pallas_call(
        body, name=name, grid=(r // tr,),
        in_specs=[blk, pl.BlockSpec((NDEV, tr, c), lambda i: (0, i, 0)), blk, blk],
        out_specs=[blk] * 4, out_shape=[_sds((r, c))] * 4,
    )(w, slots, m, v)


def _sum_slots(slots, name):
    _, r, c = slots.shape

    def body(s_ref, o_ref):
        g = s_ref[0]
        for k in range(1, NDEV):
            g = g + s_ref[k]
        o_ref[...] = g

    return pl.pallas_call(body, name=name, in_specs=[_full((NDEV, r, c))], out_specs=_full((r, c)),
                          out_shape=_sds((r, c)), grid=(1,))(slots)


def _adam_many(ws, gs, ms, vs, name):
    n = len(ws)
    shapes = [w.shape for w in ws]

    def body(*refs):
        for i in range(n):
            w_ref, g_ref, m_ref, v_ref = refs[i], refs[n + i], refs[2 * n + i], refs[3 * n + i]
            delta, nm, nv = _adamw(w_ref[...], g_ref[...], m_ref[...], v_ref[...])
            refs[4 * n + i][...] = delta
            refs[5 * n + i][...] = nm
            refs[6 * n + i][...] = nv

    outs = pl.pallas_call(body, name=name, grid=(1,), in_specs=[_full(s) for s in shapes] * 4,
                          out_specs=[_full(s) for s in shapes] * 3, out_shape=[_sds(s) for s in shapes] * 3,
                          )(*ws, *gs, *ms, *vs)
    return outs[:n], outs[n:2 * n], outs[2 * n:]


def _pack(pieces, total):
    flat = []
    n = 0
    for p in pieces:
        p = p.reshape(-1)
        pad = (-p.shape[0]) % LANES
        if pad:
            p = jnp.pad(p, (0, pad))
        flat.append(p)
        n += p.shape[0]
    if total > n:
        flat.append(jnp.zeros((total - n,), f32))
    return jnp.concatenate(flat).reshape(total // LANES, LANES)


def _unpack(vec, shapes):
    flat = vec.reshape(-1)
    out, off = [], 0
    for s in shapes:
        n = math.prod(s)
        out.append(flat[off:off + n].reshape(s))
        off += n + (-n) % LANES
    return out


def _round_up(n, m):
    return (n + m - 1) // m * m


def kernel(x, ffn1_norm_pre, ffn1_norm_post, ffn1_w_gu, ffn1_w_down, mix_norm_pre, mix_norm_post, w_in, shift_mu, w_up, w0, a_up, a0, g_up, k_k, k_a, r_k, gn_w, gn_b, conv_dw, conv_b, conv_ln_w, conv_ln_b, w_out, ffn2_norm_pre, ffn2_norm_post, ffn2_w_gu, ffn2_w_down, loss_target, m_ffn1_norm_pre, m_ffn1_norm_post, m_ffn1_w_gu, m_ffn1_w_down, m_mix_norm_pre, m_mix_norm_post, m_w_in, m_shift_mu, m_w_up, m_w0, m_a_up, m_a0, m_g_up, m_k_k, m_k_a, m_r_k, m_gn_w, m_gn_b, m_conv_dw, m_conv_b, m_conv_ln_w, m_conv_ln_b, m_w_out, m_ffn2_norm_pre, m_ffn2_norm_post, m_ffn2_w_gu, m_ffn2_w_down, v_ffn1_norm_pre, v_ffn1_norm_post, v_ffn1_w_gu, v_ffn1_w_down, v_mix_norm_pre, v_mix_norm_post, v_w_in, v_shift_mu, v_w_up, v_w0, v_a_up, v_a0, v_g_up, v_k_k, v_k_a, v_r_k, v_gn_w, v_gn_b, v_conv_dw, v_conv_b, v_conv_ln_w, v_conv_ln_b, v_w_out, v_ffn2_norm_pre, v_ffn2_norm_post, v_ffn2_w_gu, v_ffn2_w_down):
    named = dict(locals())
    t = x.shape[1]
    x0 = x.reshape(t, D)
    tgt = loss_target.reshape(t, D)

    def shard(a):
        return a[0].astype(bf16)

    def shard_t(a):
        return jnp.swapaxes(a[0], 0, 1).astype(bf16)

    (wg1,) = _exchange([shard_t(ffn1_w_gu)], ["gather"], [SLOT42], "gather_wg1")

    def cols(a):
        return jnp.transpose(a, (1, 0, 2)).reshape(a.shape[1], NDEV * a.shape[2])

    lane = jnp.arange(DR, dtype=jnp.int32) // HN
    seg = (lane[:, None] == jnp.arange(LANES, dtype=jnp.int32)[None, :]).astype(f32)
    seg_t = seg.T
    rk_row = r_k.reshape(1, DR)
    post_params = (gn_w, gn_b, rk_row, seg, seg_t)

    (xn1, gu1, h1), (wd1, win_g) = _ffn_in_fwd(x0, ffn1_norm_pre, wg1, "ffn1_in_fwd", comm=(
        [shard(ffn1_w_down), shard_t(w_in)], ["gather"] * 2, [SLOT8] * 2))
    wd1 = wd1.reshape(4, FB, D)
    win_t = win_g.reshape(DIN, D)
    (f1, x1), (wout_g, wup_g, aup_g, gup_g, cdw_g) = _mm_norm_res_fwd(
        h1, wd1, x0, ffn1_norm_post, 0.5, "ffn1_out_fwd", comm=(
            [shard(w_out), w_up[0], a_up[0], g_up[0], conv_dw[0]], ["gather"] * 5, [SLOT8] * 5))
    wout_full = wout_g.reshape(1, D, D)
    wup_full, aup_full, gup_full, cdw_full = cols(wup_g), cols(aup_g), cols(gup_g), cols(cdw_g)
    zeros64 = jnp.zeros((HN, DR), f32)
    wup_pad = jnp.concatenate([wup_full, zeros64], axis=0)
    aup_pad = jnp.concatenate([zeros64, aup_full], axis=0)
    prep_params = (shift_mu, w0, a0, k_k, k_a, wup_pad, aup_pad, gup_full, seg, seg_t)
    hm, ps, pc = _norm_mm_fwd(x1, mix_norm_pre, win_t, "mix_in_fwd")
    rec_in, (wd2,) = _prep_fwd(ps, prep_params, "prep_fwd", comm=([shard(ffn2_w_down)], ["gather"], [SLOT8]))
    wd2 = wd2.reshape(4, FB, D)
    g_gate = rec_in[6]
    rec_in = rec_in[:6]
    (a_c, g_c, p_c, q_c, yl_c), (wg2,) = _r1_fwd(rec_in, "r1_fwd", comm=([shard_t(ffn2_w_gu)], ["gather"], [SLOT42]))
    y_rec, s_all = _r2_fwd(a_c, g_c, q_c, yl_c, "r2_fwd")
    c_conv, out_b = _conv_fwd(pc, cdw_full, conv_b, conv_ln_w, conv_ln_b, "conv_fwd")
    cat = _post_fwd(y_rec, rec_in[0], rec_in[2], rec_in[3], g_gate, out_b, post_params, "post_fwd")
    fm, x2 = _mm_norm_res_fwd(cat.reshape(1, t, D), wout_full, x1, mix_norm_post, 1.0, "mix_out_fwd")
    (xn2, gu2, h2), _ = _ffn_in_fwd(x2, ffn2_norm_pre, wg2, "ffn2_in_fwd")
    f2, dy, loss_part = _mm_norm_res_fwd(h2, wd2, x2, ffn2_norm_post, 0.5, "ffn2_out_fwd", tgt=tgt)

    (df2, dg_post2, dgu2), _ = _post_bwd_nt(dy, f2, ffn2_norm_post, wd2, 0.5, gu2, "ffn2_out_bwd")
    (dx2, dg_pre2), _ = _nt_pre_bwd(dgu2, wg2, x2, ffn2_norm_pre, dy, "ffn2_in_bwd")
    dwd2 = _wgrad(h2, df2.reshape(1, t, D), True, False, D, "ffn2_wd_grad")
    dwg2 = _wgrad(dgu2.reshape(8, t, FB), xn2.reshape(1, t, D), True, False, D, "ffn2_wg_grad")
    (dfm, dg_postm, dcat), _ = _post_bwd_nt(dx2, fm, mix_norm_post, wout_full, 1.0, None, "mix_out_bwd")
    dwout = _wgrad(cat.reshape(1, t, D), dfm.reshape(1, t, D), False, False, D, "wout_grad")
    dcat = dcat.reshape(t, D)
    (dy_rec, dr2, dkp2, dv2, dgate, dgn_w, dgn_b, drk), (s_wout,) = _post_bwd(
        dcat, y_rec, rec_in[0], rec_in[2], rec_in[3], g_gate, post_params, "post_bwd", comm=(
            [dwout.reshape(NDEV, D // NDEV, D)], ["scatter"], [SLOT8]))
    dq_c, da_c, dg_c = _r2_bwd(dy_rec, q_c, s_all, a_c, "r2_bwd")
    rec_grads, (s_wg2,) = _r1_bwd(rec_in, p_c, da_c, dg_c, dq_c, dy_rec, "r1_bwd", comm=(
        [dwg2.reshape(4, 2, FB, D)], ["scatter"], [SLOT42]))
    prep_out = _prep_bwd(ps, prep_params, rec_grads, (dr2, dkp2, dv2, dgate), "prep_bwd")
    dpsl, dw0, da0, dkk, dka, dwup_pad, daup_pad, dgup = prep_out
    dc_conv, dlnw, dlnb, dcb = _conv_bwd1(dcat, c_conv, conv_ln_w, conv_ln_b, "conv_bwd1")
    (dpc, dcdw), (s_wd2,) = _conv_bwd2(dc_conv, pc, cdw_full, "conv_bwd2", comm=(
        [dwd2.reshape(NDEV, F // NDEV, D)], ["scatter"], [SLOT8]))
    dp, dmu = _shift_bwd(dpsl, ps, shift_mu, dpc, "shift_bwd")
    (dx1, dg_prem), _ = _nt_pre_bwd(dp.reshape(1, 1, t, DIN), win_t.reshape(1, 1, DIN, D), x1, mix_norm_pre, dx2,
                                    "mix_in_bwd")
    dwin_s = _wgrad(dp.reshape(1, t, DIN), hm.reshape(1, t, D), False, False, D // 2, "win_grad").reshape(
        NDEV, DIN // NDEV, D)
    (df1, dg_post1, dgu1), (s_win,) = _post_bwd_nt(dx1, f1, ffn1_norm_post, wd1, 0.5, gu1, "ffn1_out_bwd", comm=(
        [dwin_s], ["scatter"], [SLOT8]))
    dwd1 = _wgrad(h1, df1.reshape(1, t, D), True, False, D, "ffn1_wd_grad")
    dwg1, (s_wd1,) = _wgrad(dgu1.reshape(8, t, FB), xn1.reshape(1, t, D), True, False, D, "ffn1_wg_grad", comm=(
        [dwd1.reshape(NDEV, F // NDEV, D)], ["scatter"], [SLOT8]))
    wg1_sends, wg1_recvs, dwg1_thru, wg1_land, started = _scatter_start(dwg1.reshape(4, 2, FB, D), SLOT42, "wg1_scatter_start")
    (dx0, dg_pre1), _ = _nt_pre_bwd(dgu1, wg1, x0, ffn1_norm_pre + started[:1, :1], dx1, "ffn1_in_bwd")

    res = {}

    def adam_big(nm, w, s, m, v, transposed):
        view = (lambda a: jnp.swapaxes(a[0], 0, 1)) if transposed else (lambda a: a[0])
        outs = _adam_slots(view(w), s, view(m), view(v), "adam_" + nm)
        done.append(outs[1])
        res[nm] = [(jnp.swapaxes(o, 0, 1) if transposed else o)[None] for o in outs]

    done = []

    adam_big("ffn2_w_gu", ffn2_w_gu, s_wg2.reshape(NDEV, FB, D), m_ffn2_w_gu, v_ffn2_w_gu, True)
    adam_big("ffn2_w_down", ffn2_w_down, s_wd2, m_ffn2_w_down, v_ffn2_w_down, False)
    adam_big("w_in", w_in, s_win, m_w_in, v_w_in, True)
    adam_big("w_out", w_out, s_wout, m_w_out, v_w_out, False)
    adam_big("ffn1_w_down", ffn1_w_down, s_wd1, m_ffn1_w_down, v_ffn1_w_down, False)
    _, s_wg1 = _scatter_wait(wg1_sends, wg1_recvs, dwg1_thru, wg1_land, [dg_pre1] + done, SLOT42, "wg1_scatter_wait")
    my_slot = SLOT42[1](4 * lax.axis_index("x") + 2 * lax.axis_index("y") + lax.axis_index("c"))
    own = lax.dynamic_slice(dwg1.reshape(4, 2, FB, D), (*my_slot, 0, 0), (1, 1, FB, D))
    s_wg1 = lax.dynamic_update_slice(s_wg1, own, (*my_slot, 0, 0))
    rep_shapes = [(1, 1)] + [(1, D)] * 6 + [(1, DS)] + [(1, DR)] * 10
    rep_parts = [loss_part, dg_pre1, dg_post1, dg_prem, dg_postm, dg_pre2, dg_post2, dmu,
                 dw0, da0, dkk, dka, drk, dgn_w, dgn_b, dcb, dlnw, dlnb]
    n_rep = _round_up(sum(_round_up(math.prod(s), LANES) for s in rep_shapes), 8 * LANES)
    sh_shapes = [(HN, HN), (HN, HN), (LANES, HN), (CW, HN)]
    n_sh = _round_up(sum(_round_up(math.prod(s), LANES) for s in sh_shapes), 8 * LANES)
    sh_parts = []
    for a in (dwup_pad[:HN], daup_pad[HN:], dgup, dcdw):
        rows = jnp.transpose(a.reshape(a.shape[0], NDEV, HN), (1, 0, 2)).reshape(NDEV, a.shape[0] * HN)
        sh_parts.append(jnp.pad(rows, ((0, 0), (0, (-rows.shape[1]) % LANES))))
    sh_vec = jnp.concatenate(sh_parts, axis=1)
    sh_vec = jnp.pad(sh_vec, ((0, 0), (0, n_sh - sh_vec.shape[1]))).reshape(NDEV, n_sh // LANES, LANES)
    rep_vec, sh_vec, s_wg1 = lax.optimization_barrier((_pack(rep_parts, n_rep), sh_vec, s_wg1))
    s_rep, s_sh = _exchange([rep_vec, sh_vec], ["gather", "scatter"], [SLOT8, SLOT8], "exchange_small")
    adam_big("ffn1_w_gu", ffn1_w_gu, s_wg1.reshape(NDEV, FB, D), m_ffn1_w_gu, v_ffn1_w_gu, True)


    gsum = _unpack(_sum_slots(s_rep, "sum_rep"), rep_shapes)
    gsh = _unpack(_sum_slots(s_sh, "sum_sharded"), sh_shapes)
    loss = gsum[0].reshape(())
    rep_names = ["ffn1_norm_pre", "ffn1_norm_post", "mix_norm_pre", "mix_norm_post", "ffn2_norm_pre", "ffn2_norm_post",
                 "shift_mu", "w0", "a0", "k_k", "k_a", "r_k", "gn_w", "gn_b", "conv_b", "conv_ln_w", "conv_ln_b"]
    shard_names = ["w_up", "a_up", "g_up", "conv_dw"]
    small_g = {n: gsum[1 + i] for i, n in enumerate(rep_names)}
    small_g["r_k"] = small_g["r_k"].reshape(1, NH, HN)
    for i, n in enumerate(shard_names):
        small_g[n] = gsh[i][None]
    names = rep_names + shard_names
    shapes = [named[n].shape for n in names]

    def flat2(a):
        return a.reshape(1, DR) if a.shape == (1, NH, HN) else a.reshape(a.shape[-2:])

    deltas, new_ms, new_vs = _adam_many(*[[flat2(d[n]) for n in names] for d in (
        {n: named[n] for n in names}, small_g, {n: named["m_" + n] for n in names}, {n: named["v_" + n] for n in names})],
        "adam_small")
    for n, shp, dl, nm_, nv_ in zip(names, shapes, deltas, new_ms, new_vs):
        res[n] = [small_g[n], dl.reshape(shp), nm_.reshape(shp), nv_.reshape(shp)]

    order = ["ffn1_norm_pre", "ffn1_norm_post", "ffn1_w_gu", "ffn1_w_down", "mix_norm_pre", "mix_norm_post", "w_in",
             "shift_mu", "w_up", "w0", "a_up", "a0", "g_up", "k_k", "k_a", "r_k", "gn_w", "gn_b", "conv_dw", "conv_b",
             "conv_ln_w", "conv_ln_b", "w_out", "ffn2_norm_pre", "ffn2_norm_post", "ffn2_w_gu", "ffn2_w_down"]
    return (loss, dx0.reshape(1, t, D), *[res[n][0] for n in order], *[res[n][1] for n in order],
            *[res[n][2] for n in order], *[res[n][3] for n in order])
```
